```python
import jax, jax.numpy as jnp
from jax import lax
import numpy as np

D_MODEL = 2048
BATCH = 4
SEQ = 2048
DEPTH = 1
DEC_BATCH = 128
DEC_SEQ = 1
PAST_LEN = 16384
PAGE_SIZE = 128

RWKV_HEADS = 16
HEAD_SIZE = 64
RWKV_WIDTH = RWKV_HEADS * HEAD_SIZE
LORA_RANK = 64
LRU_WIDTH = 1024
LRU_BLOCKS = 16
LRU_BLOCK = LRU_WIDTH // LRU_BLOCKS
CONV_W = 4
LRU_C = 8.0
SHIFT_W = 3 * RWKV_WIDTH + 2 * LORA_RANK
IN_W = SHIFT_W + RWKV_WIDTH + 2 * LRU_WIDTH + 2 * D_MODEL
RMS_EPS = 1e-6
GN_EPS = 1e-5 * HEAD_SIZE

kernel_name = "rwkv7_rglru_gated_parallel_decode_step"


def rmsnorm(x, g):
    xf = x.astype(jnp.float32)
    y = xf * lax.rsqrt(jnp.mean(xf * xf, axis=-1, keepdims=True) + RMS_EPS)
    return (y * g.astype(jnp.float32)).astype(x.dtype)


def wkv7_scan(r, decay, k, v, a_vec, b_vec, s0):
    def step(S, inp):
        r_t, d_t, k_t, v_t, a_t, b_t = inp
        sa = jnp.einsum('bhvk,bhk->bhv', S, a_t)
        S = S * d_t[:, :, None, :] + sa[..., None] * b_t[:, :, None, :] + v_t[..., None] * k_t[:, :, None, :]
        y = jnp.einsum('bhvk,bhk->bhv', S, r_t)
        return S, y
    xs = tuple(jnp.moveaxis(t.astype(jnp.float32), 1, 0) for t in (r, decay, k, v, a_vec, b_vec))
    s_last, ys = lax.scan(step, s0.astype(jnp.float32), xs)
    return jnp.moveaxis(ys, 0, 1), s_last


def rg_lru(xc, gate_x, gate_a, lam, h0):
    log_a = -LRU_C * gate_a * jax.nn.softplus(-lam)
    a = jnp.exp(log_a)
    mult = jnp.sqrt(-jnp.expm1(2.0 * log_a))
    b = mult * gate_x * xc
    b = b.at[:, 0].add(a[:, 0] * h0)

    def combine(lft, rgt):
        a_l, b_l = lft
        a_r, b_r = rgt
        return a_l * a_r, a_r * b_l + b_r

    _, h = lax.associative_scan(combine, (a, b), axis=1)
    return h, h[:, -1]


def mixer_layer(x, sh_prev, s_wkv, conv_buf, h_lru,
                norm_g, w_in, rwkv_mu, w_decay0, w_decay_up, w_iclr0, w_iclr_up, k_k, k_a, r_k,
                ln_x_g, ln_x_b, w_out_rwkv, conv_w, conv_b, lru_gx_w, lru_gx_b, lru_ga_w, lru_ga_b,
                lru_lambda, w_out_lru, w_out):
    B, T, _ = x.shape
    dt = x.dtype
    h = rmsnorm(x, norm_g)
    z = h @ w_in
    o1 = SHIFT_W
    o2 = o1 + RWKV_WIDTH
    o3 = o2 + LRU_WIDTH
    o4 = o3 + LRU_WIDTH
    z_sh, z_rg, z_lx, z_lg, z_m = z[..., :o1], z[..., o1:o2], z[..., o2:o3], z[..., o3:o4], z[..., o4:]

    z_prev = jnp.concatenate([sh_prev[:, None].astype(dt), z_sh[:, :-1]], axis=1)
    zs = z_sh + rwkv_mu * (z_prev - z_sh)
    C = RWKV_WIDTH
    r, k, v = zs[..., :C], zs[..., C:2 * C], zs[..., 2 * C:3 * C]
    zw, za = zs[..., 3 * C:3 * C + LORA_RANK], zs[..., 3 * C + LORA_RANK:]
    r = r.astype(jnp.float32)
    k = k.astype(jnp.float32)
    v = v.astype(jnp.float32)
    w = -jax.nn.softplus(-(w_decay0 + jnp.tanh(zw) @ w_decay_up).astype(jnp.float32)) - 0.5
    decay = jnp.exp(-jnp.exp(w))
    a = jax.nn.sigmoid((w_iclr0 + za @ w_iclr_up).astype(jnp.float32))
    hs = (B, T, RWKV_HEADS, HEAD_SIZE)
    kk = (k * k_k.astype(jnp.float32)).reshape(hs)
    kk = kk * lax.rsqrt(jnp.maximum(jnp.sum(kk * kk, axis=-1, keepdims=True), 1e-24))
    k = k * (1.0 + (a - 1.0) * k_a.astype(jnp.float32))
    r4, k4, v4, a4 = r.reshape(hs), k.reshape(hs), v.reshape(hs), a.reshape(hs)
    y, s_wkv_new = wkv7_scan(r4, decay.reshape(hs), k4, v4, -kk, kk * a4, s_wkv)
    mu = jnp.mean(y, axis=-1, keepdims=True)
    var = jnp.mean(jnp.square(y - mu), axis=-1, keepdims=True)
    y = (y - mu) * lax.rsqrt(var + GN_EPS)
    y = y * ln_x_g.reshape(RWKV_HEADS, HEAD_SIZE).astype(jnp.float32) + ln_x_b.reshape(RWKV_HEADS, HEAD_SIZE).astype(jnp.float32)
    y = y + jnp.sum(r4 * k4 * r_k.astype(jnp.float32), axis=-1, keepdims=True) * v4
    o_r = y.reshape(B, T, C).astype(dt) * jax.nn.silu(z_rg)
    y_r = o_r @ w_out_rwkv

    xpad = jnp.concatenate([conv_buf.astype(dt), z_lx], axis=1)
    xc = sum(conv_w[j] * xpad[:, j:j + T] for j in range(CONV_W)) + conv_b
    conv_new = xpad[:, -(CONV_W - 1):]
    xb = xc.reshape(B, T, LRU_BLOCKS, LRU_BLOCK)
    gx = jax.nn.sigmoid((jnp.einsum('btgi,gij->btgj', xb, lru_gx_w).reshape(B, T, LRU_WIDTH) + lru_gx_b).astype(jnp.float32))
    ga = jax.nn.sigmoid((jnp.einsum('btgi,gij->btgj', xb, lru_ga_w).reshape(B, T, LRU_WIDTH) + lru_ga_b).astype(jnp.float32))
    hseq, h_last = rg_lru(xc.astype(jnp.float32), gx, ga, lru_lambda.astype(jnp.float32), h_lru.astype(jnp.float32))
    o_g = hseq.astype(dt) * jax.nn.silu(z_lg)
    y_g = o_g @ w_out_lru

    m_r, m_g = z_m[..., :D_MODEL], z_m[..., D_MODEL:]
    merged = jax.nn.sigmoid(m_r) * y_r + jax.nn.sigmoid(m_g) * y_g
    out = x + merged @ w_out
    return out, z_sh[:, -1], s_wkv_new, conv_new, h_last


def setup_inputs(seed: int = 0) -> dict:
    key = jax.random.key(seed)
    ks = jax.random.split(key, 32)
    f32 = jnp.float32
    L = DEPTH

    def nrm(k, shape, scale):
        return jax.random.normal(k, shape, f32) * scale

    a_c = jax.random.uniform(ks[26], (L, LRU_WIDTH), f32, minval=0.9, maxval=0.999)
    a_base = a_c ** (1.0 / LRU_C)
    lru_lambda = jnp.log(a_base) - jnp.log1p(-a_base)
    return {
        "x_prompt": nrm(ks[0], (BATCH, SEQ, D_MODEL), 1.0),
        "x_sample": nrm(ks[1], (DEC_BATCH, DEC_SEQ, D_MODEL), 1.0),
        "state_shift": nrm(ks[2], (L, DEC_BATCH, SHIFT_W), 1.0),
        "state_wkv": nrm(ks[3], (L, DEC_BATCH, RWKV_HEADS, HEAD_SIZE, HEAD_SIZE), 0.3),
        "state_conv": nrm(ks[4], (L, DEC_BATCH, CONV_W - 1, LRU_WIDTH), 1.0),
        "state_lru": nrm(ks[5], (L, DEC_BATCH, LRU_WIDTH), 0.5),
        "norm_g": 1.0 + nrm(ks[6], (L, D_MODEL), 0.02),
        "w_in": nrm(ks[7], (L, D_MODEL, IN_W), D_MODEL ** -0.5),
        "rwkv_mu": jax.random.uniform(ks[8], (L, SHIFT_W), f32),
        "w_decay0": jax.random.uniform(ks[9], (L, RWKV_WIDTH), f32, minval=-5.0, maxval=-0.5),
        "w_decay_up": nrm(ks[10], (L, LORA_RANK, RWKV_WIDTH), 0.5 * LORA_RANK ** -0.5),
        "w_iclr0": nrm(ks[11], (L, RWKV_WIDTH), 0.3),
        "w_iclr_up": nrm(ks[12], (L, LORA_RANK, RWKV_WIDTH), LORA_RANK ** -0.5),
        "k_k": 0.85 + nrm(ks[13], (L, RWKV_WIDTH), 0.05),
        "k_a": 1.0 + nrm(ks[14], (L, RWKV_WIDTH), 0.05),
        "r_k": nrm(ks[15], (L, RWKV_HEADS, HEAD_SIZE), 0.1),
        "ln_x_g": 1.0 + nrm(ks[16], (L, RWKV_WIDTH), 0.02),
        "ln_x_b": nrm(ks[17], (L, RWKV_WIDTH), 0.02),
        "w_out_rwkv": nrm(ks[18], (L, RWKV_WIDTH, D_MODEL), RWKV_WIDTH ** -0.5),
        "conv_w": nrm(ks[19], (L, CONV_W, LRU_WIDTH), CONV_W ** -0.5),
        "conv_b": nrm(ks[20], (L, LRU_WIDTH), 0.02),
        "lru_gx_w": nrm(ks[21], (L, LRU_BLOCKS, LRU_BLOCK, LRU_BLOCK), LRU_BLOCK ** -0.5),
        "lru_gx_b": nrm(ks[22], (L, LRU_WIDTH), 0.02),
        "lru_ga_w": nrm(ks[23], (L, LRU_BLOCKS, LRU_BLOCK, LRU_BLOCK), LRU_BLOCK ** -0.5),
        "lru_ga_b": nrm(ks[24], (L, LRU_WIDTH), 0.02),
        "lru_lambda": lru_lambda,
        "w_out_lru": nrm(ks[25], (L, LRU_WIDTH, D_MODEL), LRU_WIDTH ** -0.5),
        "w_out": nrm(ks[27], (L, D_MODEL, D_MODEL), D_MODEL ** -0.5),
        "final_norm_g": 1.0 + nrm(ks[28], (D_MODEL,), 0.02),
    }


def reference(x_prompt, x_sample, state_shift, state_wkv, state_conv, state_lru,
              norm_g, w_in, rwkv_mu, w_decay0, w_decay_up, w_iclr0, w_iclr_up, k_k, k_a, r_k,
              ln_x_g, ln_x_b, w_out_rwkv, conv_w, conv_b, lru_gx_w, lru_gx_b, lru_ga_w, lru_ga_b,
              lru_lambda, w_out_lru, w_out, final_norm_g):
    Bp = x_prompt.shape[0]
    xp, xs = x_prompt, x_sample
    sp_sh, sp_wkv, sp_conv, sp_lru = [], [], [], []
    ss_sh, ss_wkv, ss_conv, ss_lru = [], [], [], []
    for l in range(DEPTH):
        weights = (norm_g[l], w_in[l], rwkv_mu[l], w_decay0[l], w_decay_up[l], w_iclr0[l], w_iclr_up[l],
                   k_k[l], k_a[l], r_k[l], ln_x_g[l], ln_x_b[l], w_out_rwkv[l], conv_w[l], conv_b[l],
                   lru_gx_w[l], lru_gx_b[l], lru_ga_w[l], lru_ga_b[l], lru_lambda[l], w_out_lru[l], w_out[l])
        xp, a1, a2, a3, a4 = mixer_layer(
            xp,
            jnp.zeros((Bp, SHIFT_W), xp.dtype),
            jnp.zeros((Bp, RWKV_HEADS, HEAD_SIZE, HEAD_SIZE), jnp.float32),
            jnp.zeros((Bp, CONV_W - 1, LRU_WIDTH), xp.dtype),
            jnp.zeros((Bp, LRU_WIDTH), jnp.float32),
            *weights)
        sp_sh.append(a1); sp_wkv.append(a2); sp_conv.append(a3); sp_lru.append(a4)
        xs, b1, b2, b3, b4 = mixer_layer(xs, state_shift[l], state_wkv[l], state_conv[l], state_lru[l], *weights)
        ss_sh.append(b1); ss_wkv.append(b2); ss_conv.append(b3); ss_lru.append(b4)
    y_prompt = rmsnorm(xp, final_norm_g)
    y_sample = rmsnorm(xs, final_norm_g)
    return (y_prompt, y_sample,
            jnp.stack(sp_sh), jnp.stack(sp_wkv), jnp.stack(sp_conv), jnp.stack(sp_lru),
            jnp.stack(ss_sh), jnp.stack(ss_wkv), jnp.stack(ss_conv), jnp.stack(ss_lru))
```

```python
import functools

import jax
import jax.numpy as jnp
from jax import lax
from jax.experimental import pallas as pl
from jax.experimental.pallas import tpu as pltpu

F32 = jnp.float32
BF16 = jnp.bfloat16

HEADS = 16
HEAD = 64
RWKV_W = HEADS * HEAD
LORA = 64
LRU_W = 1024
LRU_BLOCKS = 16
CONV_W = 4
LRU_C = 8.0
RMS_EPS = 1e-6
GN_EPS = 1e-5 * HEAD

LANES = 128
SUBLANES = 8
WKV_CHUNK = 64
VMEM_LIMIT = 56 * 1024 * 1024

NN = (((1,), (0,)), ((), ()))
NT = (((1,), (1,)), ((), ()))
TN = (((0,), (0,)), ((), ()))

P_SCORE = 3
P_STATE_READ = 3
P_INV = 3
P_APPLY = 3
P_STATE_WRITE = 3


def _bf(x):
    return x.astype(BF16)


def _dg(a, b, dn):
    return lax.dot_general(a, b, dn, preferred_element_type=F32)


def _split2(x):
    hi = x.astype(BF16)
    lo = (x - hi.astype(F32)).astype(BF16)
    return hi, lo


def _mm(a, b, dn, passes):
    if passes == 1:
        return _dg(_bf(a), _bf(b), dn)
    ah, al = _split2(a)
    bh, bl = _split2(b)
    return _dg(ah, bh, dn) + (_dg(ah, bl, dn) + _dg(al, bh, dn))


def _softplus(x):
    return jnp.maximum(x, 0.0) + jnp.log1p(jnp.exp(-jnp.abs(x)))


def _sigmoid(x):
    return 1.0 / (1.0 + jnp.exp(-x))


def _segsum(x, e):
    outs = []
    for j in range(x.shape[1] // LANES):
        hi, lo = _split2(x[:, LANES * j:LANES * (j + 1)])
        outs.append(_dg(hi, e, NN) + _dg(lo, e, NN))
    return jnp.concatenate(outs, axis=1)


def _rms(x, g):
    return x * lax.rsqrt(jnp.mean(x * x, axis=-1, keepdims=True) + RMS_EPS) * g


def _inproj_kernel(x_ref, g_ref, w_ref, wl_ref, z_ref, zl_ref, h_ref):
    @pl.when(pl.program_id(1) == 0)
    def _():
        hb = _bf(_rms(x_ref[...], g_ref[...]))
        h_ref[...] = hb
        zl_ref[...] = _dg(hb, wl_ref[...], NN)

    z_ref[...] = _dg(h_ref[...], w_ref[...], NN)


def _inproj(x, g, w_main, w_lora, tm, tn):
    m, d = x.shape
    n = w_main.shape[1]
    return pl.pallas_call(
        _inproj_kernel,
        out_shape=(jax.ShapeDtypeStruct((m, n), F32), jax.ShapeDtypeStruct((m, LANES), F32)),
        grid=(m // tm, n // tn),
        in_specs=[
            pl.BlockSpec((tm, d), lambda i, j: (i, 0)),
            pl.BlockSpec((1, d), lambda i, j: (0, 0)),
            pl.BlockSpec((d, tn), lambda i, j: (0, j)),
            pl.BlockSpec((d, LANES), lambda i, j: (0, 0)),
        ],
        out_specs=(
            pl.BlockSpec((tm, tn), lambda i, j: (i, j)),
            pl.BlockSpec((tm, LANES), lambda i, j: (i, 0)),
        ),
        scratch_shapes=[pltpu.VMEM((tm, d), BF16)],
        compiler_params=pltpu.CompilerParams(
            dimension_semantics=("arbitrary", "arbitrary"), vmem_limit_bytes=VMEM_LIMIT),
        name="inproj",
    )(x, g, w_main, w_lora)


_MU_R, _MU_K, _MU_V, _W0, _A0, _KK, _KA, _RK, _LNG, _LNB = range(10)


def _prow(pv_ref, i):
    return pv_ref[i:i + 1, :]


def _wkv_prep(zr, zk, zv, zl, pr, pk, pv, pl_, pv_ref, mul_ref, wd_ref, wa_ref, e):
    r = zr + _prow(pv_ref, _MU_R) * (pr - zr)
    k = zk + _prow(pv_ref, _MU_K) * (pk - zk)
    v = zv + _prow(pv_ref, _MU_V) * (pv - zv)
    lo = zl + mul_ref[0:1, :] * (pl_ - zl)
    lw = _dg(_bf(jnp.tanh(lo)), wd_ref[...], NN)
    la = _dg(_bf(lo), wa_ref[...], NN)
    wlog = -_softplus(-(_prow(pv_ref, _W0) + lw)) - 0.5
    logd = -jnp.exp(wlog)
    a = _sigmoid(_prow(pv_ref, _A0) + la)
    kk = k * _prow(pv_ref, _KK)
    kk = kk * lax.rsqrt(jnp.maximum(_segsum(kk * kk, e), 1e-24))
    k2 = k * (1.0 + (a - 1.0) * _prow(pv_ref, _KA))
    return r, k2, v, -kk, kk * a, logd


def _wkv_post(y, r, k2, v, zrg, pv_ref, e):
    mu = _segsum(y, e) * (1.0 / HEAD)
    yc = y - mu
    var = _segsum(yc * yc, e) * (1.0 / HEAD)
    yn = yc * lax.rsqrt(var + GN_EPS) * _prow(pv_ref, _LNG) + _prow(pv_ref, _LNB)
    bonus = _segsum(r * k2 * _prow(pv_ref, _RK), e)
    return _bf((yn + bonus * v) * (zrg * _sigmoid(zrg)))


def _wkv_chunk_kernel(zr_ref, zk_ref, zv_ref, zrg_ref, zl_ref, pv_ref, mul_ref, wd_ref, wa_ref,
                      e_ref, o_ref, sout_ref, s_s, prev_s, prevl_s, y_s):
    c = pl.program_id(1)
    nc = pl.num_programs(1)
    C = WKV_CHUNK

    @pl.when(c == 0)
    def _():
        s_s[...] = jnp.zeros_like(s_s)
        prev_s[...] = jnp.zeros_like(prev_s)
        prevl_s[...] = jnp.zeros_like(prevl_s)

    rows = lax.broadcasted_iota(jnp.int32, (C, 1), 0)
    first = rows == 0

    def shifted(z, prev_row):
        return jnp.where(first, prev_row, pltpu.roll(z, 1, 0))

    zr, zk, zv, zl = zr_ref[...], zk_ref[...], zv_ref[...], zl_ref[...]
    pr = shifted(zr, prev_s[0:1, 0:RWKV_W])
    pk = shifted(zk, prev_s[0:1, RWKV_W:2 * RWKV_W])
    pv = shifted(zv, prev_s[0:1, 2 * RWKV_W:3 * RWKV_W])
    pl_ = shifted(zl, prevl_s[0:1, :])
    prev_s[0:1, 0:RWKV_W] = zr[C - 1:C, :]
    prev_s[0:1, RWKV_W:2 * RWKV_W] = zk[C - 1:C, :]
    prev_s[0:1, 2 * RWKV_W:3 * RWKV_W] = zv[C - 1:C, :]
    prevl_s[0:1, :] = zl[C - 1:C, :]

    e = e_ref[...]
    r, k2, v, av, bv, logd = _wkv_prep(zr, zk, zv, zl, pr, pk, pv, pl_, pv_ref, mul_ref,
                                       wd_ref, wa_ref, e)

    ti = lax.broadcasted_iota(jnp.int32, (C, C), 0)
    tj = lax.broadcasted_iota(jnp.int32, (C, C), 1)
    tri = jnp.where(ti >= tj, 1.0, 0.0).astype(BF16)
    d_hi = _bf(logd)
    d_r1 = logd - d_hi.astype(F32)
    d_mid = _bf(d_r1)
    d_lo = _bf(d_r1 - d_mid.astype(F32))
    cum = _dg(tri, d_hi, NN) + (_dg(tri, d_mid, NN) + _dg(tri, d_lo, NN))
    tot = cum[C - 1:C, :]
    e_in = jnp.exp(cum)
    e_neg = jnp.exp(-cum)
    e_tot = jnp.exp(tot - cum)
    a_t = av * jnp.exp(cum - logd)
    r_t = r * e_in
    k_t = k2 * e_neg
    b_t = bv * e_neg
    k_h = k2 * e_tot
    b_h = bv * e_tot
    p_c = jnp.exp(tot)

    gi = lax.broadcasted_iota(jnp.int32, (2 * C, 2 * C), 0)
    gj = lax.broadcasted_iota(jnp.int32, (2 * C, 2 * C), 1)
    gjm = jnp.where(gj >= C, gj - C, gj)
    keep = gjm < jnp.where(gi >= C, gi - C + 1, gi)
    eye = jnp.where(ti == tj, 1.0, 0.0).astype(F32)

    for h in range(HEADS):
        sl = slice(HEAD * h, HEAD * (h + 1))
        ar = jnp.concatenate([a_t[:, sl], r_t[:, sl]], axis=0)
        bk = jnp.concatenate([b_t[:, sl], k_t[:, sl]], axis=0)
        g = jnp.where(keep, _mm(ar, bk, NT, P_SCORE), 0.0)
        s0 = s_s[h]
        ars = _mm(ar, s0, NT, P_STATE_READ)
        vh = v[:, sl]
        w = ars[0:C, :] + _mm(g[0:C, C:2 * C], vh, NN, P_APPLY)
        x = g[0:C, 0:C]
        t = eye + x
        for _ in range(C.bit_length() - 2):
            x = _mm(x, x, NN, P_INV)
            t = t + _mm(t, x, NN, P_INV)
        u = _mm(t, w, NN, P_APPLY)
        uv = jnp.concatenate([u, vh], axis=0)
        y_s[:, sl] = ars[C:2 * C, :] + _mm(g[C:2 * C, :], uv, NN, P_APPLY)
        bkh = jnp.concatenate([b_h[:, sl], k_h[:, sl]], axis=0)
        s_s[h] = s0 * p_c[:, sl] + _mm(uv, bkh, TN, P_STATE_WRITE)

    o_ref[...] = _wkv_post(y_s[...], r, k2, v, zrg_ref[...], pv_ref, e)

    @pl.when(c == nc - 1)
    def _():
        sout_ref[0] = s_s[...]


def _wkv_chunk(z_main, z_lora, pvec, mul, wd, wa, e, batch, seq):
    C = WKV_CHUNK
    nc = seq // C
    full = lambda shp: pl.BlockSpec(shp, lambda b, c: (0,) * len(shp))
    col = lambda j: pl.BlockSpec((C, RWKV_W), lambda b, c, j=j: (b * nc + c, j))
    return pl.pallas_call(
        _wkv_chunk_kernel,
        out_shape=(jax.ShapeDtypeStruct((batch * seq, RWKV_W), BF16),
                   jax.ShapeDtypeStruct((batch, HEADS, HEAD, HEAD), F32)),
        grid=(batch, nc),
        in_specs=[col(0), col(1), col(2), col(3),
                  pl.BlockSpec((C, LANES), lambda b, c: (b * nc + c, 0)),
                  full(pvec.shape), full(mul.shape), full(wd.shape), full(wa.shape), full(e.shape)],
        out_specs=(pl.BlockSpec((C, RWKV_W), lambda b, c: (b * nc + c, 0)),
                   pl.BlockSpec((1, HEADS, HEAD, HEAD), lambda b, c: (b, 0, 0, 0))),
        scratch_shapes=[pltpu.VMEM((HEADS, HEAD, HEAD), F32),
                        pltpu.VMEM((SUBLANES, 3 * RWKV_W), F32),
                        pltpu.VMEM((SUBLANES, LANES), F32),
                        pltpu.VMEM((C, RWKV_W), F32)],
        compiler_params=pltpu.CompilerParams(
            dimension_semantics=("arbitrary", "arbitrary"), vmem_limit_bytes=VMEM_LIMIT),
        name="wkv_chunk",
    )(z_main, z_main, z_main, z_main, z_lora, pvec, mul, wd, wa, e)


def _wkv_step_kernel(zr_ref, zk_ref, zv_ref, zrg_ref, zl_ref, sh_ref, shl_ref, s0_ref, pv_ref,
                     mul_ref, wd_ref, wa_ref, e_ref, o_ref, sout_ref, rows_s, y_s):
    bt = zr_ref.shape[0]
    e = e_ref[...]
    r, k2, v, av, bv, logd = _wkv_prep(
        zr_ref[...], zk_ref[...], zv_ref[...], zl_ref[...],
        sh_ref[:, 0:RWKV_W], sh_ref[:, RWKV_W:2 * RWKV_W], sh_ref[:, 2 * RWKV_W:3 * RWKV_W],
        shl_ref[...], pv_ref, mul_ref, wd_ref, wa_ref, e)
    d = jnp.exp(logd)
    for q, val in enumerate((av, bv, k2, v, r, d * r, d)):
        for i in range(bt):
            rows_s[q, i] = val[i:i + 1, :]

    row8 = lax.broadcasted_iota(jnp.int32, (SUBLANES, 1), 0)

    def per_seq(b, carry):
        a_f, b_f, k_f, v_f, r_f, dr_f, d_f = (rows_s[q, b] for q in range(7))
        ys = []
        for h in range(HEADS):
            sl = slice(HEAD * h, HEAD * (h + 1))
            a_r, b_r, k_r, v_r, r_r, dr_r, d_r = (f[:, sl] for f in (a_f, b_f, k_f, v_f, r_f, dr_f, d_f))
            s0 = s0_ref[b, h]
            lhs = jnp.where(row8 == 0, a_r, dr_r)
            sa_y = _mm(lhs, s0, NT, P_STATE_READ)
            sa = sa_y[0:1, :]
            br = jnp.sum(b_r * r_r, axis=-1, keepdims=True)
            kr = jnp.sum(k_r * r_r, axis=-1, keepdims=True)
            ys.append(sa_y[1:2, :] + sa * br + v_r * kr)
            left = jnp.where(row8 == 0, sa, jnp.where(row8 == 1, v_r, 0.0))
            right = jnp.where(row8 == 0, b_r, jnp.where(row8 == 1, k_r, 0.0))
            sout_ref[b, h] = s0 * d_r + _mm(left, right, TN, P_STATE_WRITE)
        y_s[b] = jnp.concatenate(ys, axis=1)
        return carry

    lax.fori_loop(0, bt, per_seq, 0)
    y = jnp.concatenate([y_s[i] for i in range(bt)], axis=0)
    o_ref[...] = _wkv_post(y, r, k2, v, zrg_ref[...], pv_ref, e)


def _wkv_step(z_main, z_lora, sh_main, sh_lora, s0, pvec, mul, wd, wa, e, bt):
    nb = z_main.shape[0]
    full = lambda shp: pl.BlockSpec(shp, lambda i: (0,) * len(shp))
    col = lambda j: pl.BlockSpec((bt, RWKV_W), lambda i, j=j: (i, j))
    return pl.pallas_call(
        _wkv_step_kernel,
        out_shape=(jax.ShapeDtypeStruct((nb, RWKV_W), BF16),
                   jax.ShapeDtypeStruct((nb, HEADS, HEAD, HEAD), F32)),
        grid=(nb // bt,),
        in_specs=[col(0), col(1), col(2), col(3),
                  pl.BlockSpec((bt, LANES), lambda i: (i, 0)),
                  pl.BlockSpec((bt, 3 * RWKV_W), lambda i: (i, 0)),
                  pl.BlockSpec((bt, LANES), lambda i: (i, 0)),
                  pl.BlockSpec((bt, HEADS, HEAD, HEAD), lambda i: (i, 0, 0, 0)),
                  full(pvec.shape), full(mul.shape), full(wd.shape), full(wa.shape), full(e.shape)],
        out_specs=(pl.BlockSpec((bt, RWKV_W), lambda i: (i, 0)),
                   pl.BlockSpec((bt, HEADS, HEAD, HEAD), lambda i: (i, 0, 0, 0))),
        scratch_shapes=[pltpu.VMEM((7, bt, 1, RWKV_W), F32), pltpu.VMEM((bt, 1, RWKV_W), F32)],
        compiler_params=pltpu.CompilerParams(
            dimension_semantics=("arbitrary",), vmem_limit_bytes=VMEM_LIMIT),
        name="wkv_step",
    )(z_main, z_main, z_main, z_main, z_lora, sh_main, sh_lora, s0, pvec, mul, wd, wa, e)


_CW0, _CW1, _CW2, _CW3, _CB, _GXB, _GAB, _LAM = range(8)


def _lru_gates(xc, lp_ref, wg_ref):
    g = _dg(_bf(xc), wg_ref[...], NN)
    gx = _sigmoid(g[:, 0:LRU_W] + _prow(lp_ref, _GXB))
    ga = _sigmoid(g[:, LRU_W:2 * LRU_W] + _prow(lp_ref, _GAB))
    log_a = -LRU_C * ga * _softplus(-_prow(lp_ref, _LAM))
    a = jnp.exp(log_a)
    mult = jnp.sqrt((1.0 - a) * (1.0 + a))
    return a, mult * gx * xc


def _lru_scan_kernel(zx_ref, zg_ref, lp_ref, wg_ref, o_ref, hlast_ref, c8_s, hc_s, a_s, b_s, h_s):
    t = pl.program_id(1)
    nt = pl.num_programs(1)
    tl = zx_ref.shape[0]

    @pl.when(t == 0)
    def _():
        c8_s[...] = jnp.zeros_like(c8_s)
        hc_s[...] = jnp.zeros_like(hc_s)

    zx = zx_ref[...]
    c8 = c8_s[...]
    row8 = lax.broadcasted_iota(jnp.int32, (SUBLANES, 1), 0)
    xc = _prow(lp_ref, _CW3) * zx + _prow(lp_ref, _CB)
    for j in range(1, CONV_W):
        zr_j = pltpu.roll(zx, j, 0)
        top = jnp.where(row8 < j, pltpu.roll(c8, j, 0), zr_j[0:SUBLANES, :])
        xj = jnp.concatenate([top, zr_j[SUBLANES:, :]], axis=0)
        xc = xc + _prow(lp_ref, CONV_W - 1 - j) * xj
    c8_s[...] = zx[tl - SUBLANES:tl, :]

    a, b = _lru_gates(xc, lp_ref, wg_ref)
    a_s[...] = a
    b_s[...] = b

    def blk(i, hc):
        off = pl.multiple_of(i * SUBLANES, SUBLANES)
        a8 = a_s[pl.ds(off, SUBLANES), :]
        b8 = b_s[pl.ds(off, SUBLANES), :]
        for s in (1, 2, 4):
            keep = row8 >= s
            b8 = jnp.where(keep, a8 * pltpu.roll(b8, s, 0) + b8, b8)
            a8 = jnp.where(keep, a8 * pltpu.roll(a8, s, 0), a8)
        hb = b8 + a8 * hc
        h_s[pl.ds(off, SUBLANES), :] = hb
        return jnp.broadcast_to(hb[SUBLANES - 1:SUBLANES, :], hb.shape)

    hc = lax.fori_loop(0, tl // SUBLANES, blk, hc_s[...])
    hc_s[...] = hc
    zg = zg_ref[...]
    o_ref[...] = _bf(h_s[...] * (zg * _sigmoid(zg)))

    @pl.when(t == nt - 1)
    def _():
        hlast_ref[0] = hc[0:1, :]


def _lru_scan(z_main, lp, wg, batch, seq, tl):
    nt = seq // tl
    full = lambda shp: pl.BlockSpec(shp, lambda b, t: (0,) * len(shp))
    col = lambda j: pl.BlockSpec((tl, LRU_W), lambda b, t, j=j: (b * nt + t, j))
    return pl.pallas_call(
        _lru_scan_kernel,
        out_shape=(jax.ShapeDtypeStruct((batch * seq, LRU_W), BF16),
                   jax.ShapeDtypeStruct((batch, 1, LRU_W), F32)),
        grid=(batch, nt),
        in_specs=[col(4), col(5), full(lp.shape), full(wg.shape)],
        out_specs=(pl.BlockSpec((tl, LRU_W), lambda b, t: (b * nt + t, 0)),
                   pl.BlockSpec((1, 1, LRU_W), lambda b, t: (b, 0, 0))),
        scratch_shapes=[pltpu.VMEM((SUBLANES, LRU_W), F32), pltpu.VMEM((SUBLANES, LRU_W), F32),
                        pltpu.VMEM((tl, LRU_W), F32), pltpu.VMEM((tl, LRU_W), F32),
                        pltpu.VMEM((tl, LRU_W), F32)],
        compiler_params=pltpu.CompilerParams(
            dimension_semantics=("arbitrary", "arbitrary"), vmem_limit_bytes=VMEM_LIMIT),
        name="lru_scan",
    )(z_main, z_main, lp, wg)


def _lru_step_kernel(zx_ref, zg_ref, conv_ref, h0_ref, lp_ref, wg_ref, o_ref, hnew_ref):
    zx = zx_ref[...]
    xc = _prow(lp_ref, _CW3) * zx + _prow(lp_ref, _CB)
    for j in range(CONV_W - 1):
        xc = xc + _prow(lp_ref, j) * conv_ref[:, LRU_W * j:LRU_W * (j + 1)]
    a, b = _lru_gates(xc, lp_ref, wg_ref)
    h = a * h0_ref[...] + b
    hnew_ref[...] = h
    zg = zg_ref[...]
    o_ref[...] = _bf(h * (zg * _sigmoid(zg)))


def _lru_step(z_main, conv, h0, lp, wg):
    nb = z_main.shape[0]
    full = lambda shp: pl.BlockSpec(shp, lambda i: (0,) * len(shp))
    col = lambda j: pl.BlockSpec((nb, LRU_W), lambda i, j=j: (0, j))
    return pl.pallas_call(
        _lru_step_kernel,
        out_shape=(jax.ShapeDtypeStruct((nb, LRU_W), BF16), jax.ShapeDtypeStruct((nb, LRU_W), F32)),
        grid=(1,),
        in_specs=[col(4), col(5), full(conv.shape), full(h0.shape), full(lp.shape), full(wg.shape)],
        out_specs=(full((nb, LRU_W)), full((nb, LRU_W))),
        compiler_params=pltpu.CompilerParams(
            dimension_semantics=("arbitrary",), vmem_limit_bytes=VMEM_LIMIT),
        name="lru_step",
    )(z_main, z_main, conv, h0, lp, wg)


def _outproj_kernel(x_ref, or_ref, og_ref, mr_ref, mg_ref, wr_ref, wg_ref, wo_ref, fg_ref,
                    out_ref, *, final):
    y_r = _dg(or_ref[...], wr_ref[...], NN)
    y_g = _dg(og_ref[...], wg_ref[...], NN)
    merged = _sigmoid(mr_ref[...]) * y_r + _sigmoid(mg_ref[...]) * y_g
    out = x_ref[...] + _dg(_bf(merged), wo_ref[...], NN)
    out_ref[...] = _rms(out, fg_ref[...]) if final else out


def _outproj(x, o_r, o_g, z_main, w_r, w_g, w_o, fg, tm, final):
    m, d = x.shape
    const = lambda shp: pl.BlockSpec(shp, lambda i: (0,) * len(shp), pipeline_mode=pl.Buffered(1))
    return pl.pallas_call(
        functools.partial(_outproj_kernel, final=final),
        out_shape=jax.ShapeDtypeStruct((m, d), F32),
        grid=(m // tm,),
        in_specs=[
            pl.BlockSpec((tm, d), lambda i: (i, 0)),
            pl.BlockSpec((tm, RWKV_W), lambda i: (i, 0)),
            pl.BlockSpec((tm, LRU_W), lambda i: (i, 0)),
            pl.BlockSpec((tm, d), lambda i: (i, 3)),
            pl.BlockSpec((tm, d), lambda i: (i, 4)),
            const(w_r.shape), const(w_g.shape), const(w_o.shape), const(fg.shape),
        ],
        out_specs=pl.BlockSpec((tm, d), lambda i: (i, 0)),
        compiler_params=pltpu.CompilerParams(
            dimension_semantics=("arbitrary",), vmem_limit_bytes=VMEM_LIMIT),
        name="outproj",
    )(x, o_r, o_g, z_main, z_main, w_r, w_g, w_o, fg)


def _row_tile(m, want):
    t = min(m, want)
    assert m % t == 0, (m, t)
    return t


def _layer_params(l, w_in, rwkv_mu, w_decay0, w_decay_up, w_iclr0, w_iclr_up, k_k, k_a, r_k,
                  ln_x_g, ln_x_b, w_out_rwkv, conv_w, conv_b, lru_gx_w, lru_gx_b, lru_ga_w,
                  lru_ga_b, lru_lambda, w_out_lru, w_out):
    sh_w = 3 * RWKV_W + 2 * LORA
    w = w_in[l]
    p = {}
    p["w_main"] = jnp.concatenate([w[:, :3 * RWKV_W], w[:, sh_w:]], axis=1).astype(BF16)
    p["w_lora"] = w[:, 3 * RWKV_W:sh_w].astype(BF16)
    mu = rwkv_mu[l]
    rows = [mu[0:RWKV_W], mu[RWKV_W:2 * RWKV_W], mu[2 * RWKV_W:3 * RWKV_W], w_decay0[l], w_iclr0[l],
            k_k[l], k_a[l], r_k[l].reshape(-1), ln_x_g[l], ln_x_b[l]]
    p["pvec"] = jnp.concatenate(
        [jnp.stack(rows), jnp.zeros((16 - len(rows), RWKV_W), F32)], axis=0).astype(F32)
    p["mul"] = jnp.broadcast_to(mu[3 * RWKV_W:sh_w][None, :], (SUBLANES, 2 * LORA)).astype(F32)
    zeros = jnp.zeros((LORA, RWKV_W), F32)
    p["wd"] = jnp.concatenate([w_decay_up[l], zeros], axis=0).astype(BF16)
    p["wa"] = jnp.concatenate([zeros, w_iclr_up[l]], axis=0).astype(BF16)
    lane_head = jnp.arange(LANES) // HEAD
    p["e"] = (lane_head[:, None] == lane_head[None, :]).astype(BF16)
    p["lp"] = jnp.concatenate(
        [conv_w[l], conv_b[l][None], lru_gx_b[l][None], lru_ga_b[l][None], lru_lambda[l][None]],
        axis=0).astype(F32)
    eye = jnp.eye(LRU_BLOCKS, dtype=F32)
    bd = lambda g: (eye[:, None, :, None] * g[:, :, None, :]).reshape(LRU_W, LRU_W)
    p["wg"] = jnp.concatenate([bd(lru_gx_w[l]), bd(lru_ga_w[l])], axis=1).astype(BF16)
    p["w_r"] = w_out_rwkv[l].astype(BF16)
    p["w_g"] = w_out_lru[l].astype(BF16)
    p["w_o"] = w_out[l].astype(BF16)
    return p


def kernel(x_prompt, x_sample, state_shift, state_wkv, state_conv, state_lru, norm_g, w_in, rwkv_mu,
           w_decay0, w_decay_up, w_iclr0, w_iclr_up, k_k, k_a, r_k, ln_x_g, ln_x_b, w_out_rwkv,
           conv_w, conv_b, lru_gx_w, lru_gx_b, lru_ga_w, lru_ga_b, lru_lambda, w_out_lru, w_out,
           final_norm_g):
    bp, seq, d = x_prompt.shape
    bs = x_sample.shape[0]
    assert x_sample.shape[1] == 1 and seq % WKV_CHUNK == 0
    depth = w_in.shape[0]
    sh_w = 3 * RWKV_W + 2 * LORA
    xp = x_prompt.reshape(bp * seq, d)
    xs = x_sample.reshape(bs, d)
    fg = final_norm_g.reshape(1, d)
    outs = [[] for _ in range(8)]
    for l in range(depth):
        p = _layer_params(l, w_in, rwkv_mu, w_decay0, w_decay_up, w_iclr0, w_iclr_up, k_k, k_a, r_k,
                          ln_x_g, ln_x_b, w_out_rwkv, conv_w, conv_b, lru_gx_w, lru_gx_b, lru_ga_w,
                          lru_ga_b, lru_lambda, w_out_lru, w_out)
        g = norm_g[l].reshape(1, d)
        rec = (p["pvec"], p["mul"], p["wd"], p["wa"], p["e"])

        zp, zlp = _inproj(xp, g, p["w_main"], p["w_lora"], _row_tile(bp * seq, 1024), 1024)
        o_r, s_new = _wkv_chunk(zp, zlp, *rec, bp, seq)
        o_g, h_last = _lru_scan(zp, p["lp"], p["wg"], bp, seq, _row_tile(seq, 256))
        last = l == depth - 1
        xp = _outproj(xp, o_r, o_g, zp, p["w_r"], p["w_g"], p["w_o"], fg, _row_tile(bp * seq, 256), last)
        zp3 = zp.reshape(bp, seq, -1)
        outs[0].append(jnp.concatenate([zp3[:, -1, :3 * RWKV_W], zlp.reshape(bp, seq, -1)[:, -1]], axis=-1))
        outs[1].append(s_new)
        outs[2].append(zp3[:, seq - (CONV_W - 1):, 4 * RWKV_W:4 * RWKV_W + LRU_W])
        outs[3].append(h_last.reshape(bp, LRU_W))

        zs, zls = _inproj(xs, g, p["w_main"], p["w_lora"], bs, 1024)
        sh = state_shift[l]
        o_r, s_new = _wkv_step(zs, zls, sh[:, :3 * RWKV_W], sh[:, 3 * RWKV_W:sh_w], state_wkv[l],
                               *rec, _row_tile(bs, 16))
        conv = state_conv[l]
        o_g, h_new = _lru_step(zs, conv.reshape(bs, (CONV_W - 1) * LRU_W), state_lru[l], p["lp"], p["wg"])
        xs = _outproj(xs, o_r, o_g, zs, p["w_r"], p["w_g"], p["w_o"], fg, bs, last)
        z_lx = zs[:, 4 * RWKV_W:4 * RWKV_W + LRU_W]
        outs[4].append(jnp.concatenate([zs[:, :3 * RWKV_W], zls], axis=-1))
        outs[5].append(s_new)
        outs[6].append(jnp.concatenate([conv[:, 1:], z_lx[:, None, :]], axis=1))
        outs[7].append(h_new)

    return (xp.reshape(bp, seq, d), xs.reshape(bs, 1, d)) + tuple(jnp.stack(o) for o in outs)
```

```python
import functools

import jax
import jax.numpy as jnp
from jax import lax
from jax.experimental import pallas as pl
from jax.experimental.pallas import tpu as pltpu

F32 = jnp.float32
BF16 = jnp.bfloat16

HEADS = 16
HEAD = 64
RWKV_W = HEADS * HEAD
LORA = 64
LRU_W = 1024
LRU_BLOCKS = 16
CONV_W = 4
LRU_C = 8.0
RMS_EPS = 1e-6
GN_EPS = 1e-5 * HEAD

LANES = 128
SUBLANES = 8
WKV_CHUNK = 64
VMEM_LIMIT = 56 * 1024 * 1024

NN = (((1,), (0,)), ((), ()))
NT = (((1,), (1,)), ((), ()))
TN = (((0,), (0,)), ((), ()))


def _bf(x):
    return x.astype(BF16)


def _dg(a, b, dn):
    return lax.dot_general(a, b, dn, preferred_element_type=F32)


def _split2(x):
    hi = x.astype(BF16)
    lo = (x - hi.astype(F32)).astype(BF16)
    return hi, lo


def _softplus(x):
    return jnp.maximum(x, 0.0) + jnp.log1p(jnp.exp(-jnp.abs(x)))


def _sigmoid(x):
    return 1.0 / (1.0 + jnp.exp(-x))


def _segsum(x, e):
    outs = []
    for j in range(x.shape[1] // LANES):
        hi, lo = _split2(x[:, LANES * j:LANES * (j + 1)])
        outs.append(_dg(hi, e, NN) + _dg(lo, e, NN))
    return jnp.concatenate(outs, axis=1)


def _rms(x, g):
    return x * lax.rsqrt(jnp.mean(x * x, axis=-1, keepdims=True) + RMS_EPS) * g


def _inproj_kernel(x_ref, g_ref, w_ref, wl_ref, z_ref, zl_ref, h_ref):
    @pl.when(pl.program_id(1) == 0)
    def _():
        hb = _bf(_rms(x_ref[...], g_ref[...]))
        h_ref[...] = hb
        zl_ref[...] = _dg(hb, wl_ref[...], NN)

    z_ref[...] = _dg(h_ref[...], w_ref[...], NN)


def _inproj(x, g, w_main, w_lora, tm, tn):
    m, d = x.shape
    n = w_main.shape[1]
    return pl.pallas_call(
        _inproj_kernel,
        out_shape=(jax.ShapeDtypeStruct((m, n), F32), jax.ShapeDtypeStruct((m, LANES), F32)),
        grid=(m // tm, n // tn),
        in_specs=[
            pl.BlockSpec((tm, d), lambda i, j: (i, 0)),
            pl.BlockSpec((1, d), lambda i, j: (0, 0)),
            pl.BlockSpec((d, tn), lambda i, j: (0, j)),
            pl.BlockSpec((d, LANES), lambda i, j: (0, 0)),
        ],
        out_specs=(
            pl.BlockSpec((tm, tn), lambda i, j: (i, j)),
            pl.BlockSpec((tm, LANES), lambda i, j: (i, 0)),
        ),
        scratch_shapes=[pltpu.VMEM((tm, d), BF16)],
        compiler_params=pltpu.CompilerParams(
            dimension_semantics=("arbitrary", "arbitrary"), vmem_limit_bytes=VMEM_LIMIT),
        name="inproj",
    )(x, g, w_main, w_lora)


_MU_R, _MU_K, _MU_V, _W0, _A0, _KK, _KA, _RK, _LNG, _LNB = range(10)


def _prow(pv_ref, i):
    return pv_ref[i:i + 1, :]


def _wkv_prep(zr, zk, zv, zl, pr, pk, pv, pl_, pv_ref, mul_ref, wd_ref, wa_ref, e):
    r = zr + _prow(pv_ref, _MU_R) * (pr - zr)
    k = zk + _prow(pv_ref, _MU_K) * (pk - zk)
    v = zv + _prow(pv_ref, _MU_V) * (pv - zv)
    lo = zl + mul_ref[0:1, :] * (pl_ - zl)
    lw = _dg(_bf(jnp.tanh(lo)), wd_ref[...], NN)
    la = _dg(_bf(lo), wa_ref[...], NN)
    wlog = -_softplus(-(_prow(pv_ref, _W0) + lw)) - 0.5
    logd = -jnp.exp(wlog)
    a = _sigmoid(_prow(pv_ref, _A0) + la)
    kk = k * _prow(pv_ref, _KK)
    kk = kk * lax.rsqrt(jnp.maximum(_segsum(kk * kk, e), 1e-24))
    k2 = k * (1.0 + (a - 1.0) * _prow(pv_ref, _KA))
    return r, k2, v, -kk, kk * a, logd


def _wkv_post(y, r, k2, v, zrg, pv_ref, e):
    mu = _segsum(y, e) * (1.0 / HEAD)
    yc = y - mu
    var = _segsum(yc * yc, e) * (1.0 / HEAD)
    yn = yc * lax.rsqrt(var + GN_EPS) * _prow(pv_ref, _LNG) + _prow(pv_ref, _LNB)
    bonus = _segsum(r * k2 * _prow(pv_ref, _RK), e)
    return _bf((yn + bonus * v) * (zrg * _sigmoid(zrg)))


def _wkv_chunk_kernel(zr_ref, zk_ref, zv_ref, zrg_ref, zl_ref, pv_ref, mul_ref, wd_ref, wa_ref,
                      e_ref, o_ref, sout_ref, s_s, prev_s, prevl_s):
    c = pl.program_id(1)
    nc = pl.num_programs(1)
    C = WKV_CHUNK
    assert C == HEAD and 2 * HEAD == LANES

    @pl.when(c == 0)
    def _():
        s_s[...] = jnp.zeros_like(s_s)
        prev_s[...] = jnp.zeros_like(prev_s)
        prevl_s[...] = jnp.zeros_like(prevl_s)

    rows = lax.broadcasted_iota(jnp.int32, (C, 1), 0)
    first = rows == 0

    def shifted(z, prev_row):
        return jnp.where(first, prev_row, pltpu.roll(z, 1, 0))

    zr, zk, zv, zl = zr_ref[...], zk_ref[...], zv_ref[...], zl_ref[...]
    pr = shifted(zr, prev_s[0:1, 0:RWKV_W])
    pk = shifted(zk, prev_s[0:1, RWKV_W:2 * RWKV_W])
    pv = shifted(zv, prev_s[0:1, 2 * RWKV_W:3 * RWKV_W])
    pl_ = shifted(zl, prevl_s[0:1, :])
    prev_s[0:1, 0:RWKV_W] = zr[C - 1:C, :]
    prev_s[0:1, RWKV_W:2 * RWKV_W] = zk[C - 1:C, :]
    prev_s[0:1, 2 * RWKV_W:3 * RWKV_W] = zv[C - 1:C, :]
    prevl_s[0:1, :] = zl[C - 1:C, :]

    e = e_ref[...]
    r, k2, v, av, bv, logd = _wkv_prep(zr, zk, zv, zl, pr, pk, pv, pl_, pv_ref, mul_ref,
                                       wd_ref, wa_ref, e)

    ti = lax.broadcasted_iota(jnp.int32, (C, C), 0)
    tj = lax.broadcasted_iota(jnp.int32, (C, C), 1)
    tri = jnp.where(ti >= tj, 1.0, 0.0).astype(BF16)
    d_hi = _bf(logd)
    d_r1 = logd - d_hi.astype(F32)
    d_mid = _bf(d_r1)
    d_lo = _bf(d_r1 - d_mid.astype(F32))
    cum = _dg(tri, d_hi, NN) + (_dg(tri, d_mid, NN) + _dg(tri, d_lo, NN))
    tot = cum[C - 1:C, :]
    e_in = jnp.exp(cum)
    e_neg = jnp.exp(-cum)
    e_tot = jnp.exp(tot - cum)
    a_t = av * jnp.exp(cum - logd)
    r_t = r * e_in
    k_t = k2 * e_neg
    b_t = bv * e_neg
    k_h = k2 * e_tot
    b_h = bv * e_tot
    p_c = jnp.exp(tot)

    lane = lax.broadcasted_iota(jnp.int32, (C, LANES), 1)
    trow = lax.broadcasted_iota(jnp.int32, (C, LANES), 0)
    lo = lane < HEAD
    s_in = lane & (HEAD - 1)
    strict = s_in < trow
    incl2 = ((lax.broadcasted_iota(jnp.int32, (C, 2 * LANES), 1) & (HEAD - 1))
             <= lax.broadcasted_iota(jnp.int32, (C, 2 * LANES), 0))
    eye2 = jnp.where(s_in == trow, 1.0, 0.0).astype(F32)
    vrow = lax.broadcasted_iota(jnp.int32, (2 * HEAD, LANES), 0)
    klane = lax.broadcasted_iota(jnp.int32, (2 * HEAD, LANES), 1)
    same_head = (vrow < HEAD) == (klane < HEAD)

    def bd(x):
        z = jnp.zeros_like(x)
        return jnp.concatenate([jnp.where(lo, x, z), jnp.where(lo, z, x)], axis=0)

    pairs = range(HEADS // 2)
    sl = [slice(LANES * p, LANES * (p + 1)) for p in pairs]
    ar = [_bf(jnp.concatenate([a_t[:, sl[p]], r_t[:, sl[p]]], axis=0)) for p in pairs]
    bk = [_bf(jnp.concatenate([bd(b_t[:, sl[p]]), bd(k_t[:, sl[p]])], axis=0)) for p in pairs]
    g = [_dg(ar[p], bk[p], NT) for p in pairs]
    s0 = [s_s[p] for p in pairs]
    ars = [_dg(ar[p], _bf(s0[p]), NT) for p in pairs]
    vbd = [_bf(bd(v[:, sl[p]])) for p in pairs]
    x = [jnp.where(strict, g[p][0:C, 0:LANES], 0.0) for p in pairs]
    ak = [jnp.where(strict, g[p][0:C, LANES:2 * LANES], 0.0) for p in pairs]
    w = [ars[p][0:C, :] + _dg(_bf(ak[p]), vbd[p], NN) for p in pairs]
    t = [eye2 + x[p] for p in pairs]
    xbd = [_bf(bd(x[p])) for p in pairs]
    for _ in range(C.bit_length() - 2):
        x = [_dg(_bf(x[p]), xbd[p], NN) for p in pairs]
        xbd = [_bf(bd(x[p])) for p in pairs]
        t = [t[p] + _dg(_bf(t[p]), xbd[p], NN) for p in pairs]
    u = [_dg(_bf(t[p]), _bf(bd(w[p])), NN) for p in pairs]
    rbk = [_bf(jnp.where(incl2, g[p][C:2 * C, :], 0.0)) for p in pairs]
    uvbd = [jnp.concatenate([_bf(bd(u[p])), vbd[p]], axis=0) for p in pairs]
    y = [ars[p][C:2 * C, :] + _dg(rbk[p], uvbd[p], NN) for p in pairs]
    uv = [_bf(jnp.concatenate([u[p], v[:, sl[p]]], axis=0)) for p in pairs]
    bkh = [_bf(jnp.concatenate([b_h[:, sl[p]], k_h[:, sl[p]]], axis=0)) for p in pairs]
    s1 = [s0[p] * p_c[:, sl[p]] + jnp.where(same_head, _dg(uv[p], bkh[p], TN), 0.0) for p in pairs]
    for p in pairs:
        s_s[p] = s1[p]

    o_ref[...] = _wkv_post(jnp.concatenate(y, axis=1), r, k2, v, zrg_ref[...], pv_ref, e)

    @pl.when(c == nc - 1)
    def _():
        for p in pairs:
            sout_ref[0, 2 * p] = s1[p][0:HEAD, 0:HEAD]
            sout_ref[0, 2 * p + 1] = s1[p][HEAD:2 * HEAD, HEAD:2 * HEAD]


def _wkv_chunk(z_main, z_lora, pvec, mul, wd, wa, e, batch, seq):
    C = WKV_CHUNK
    nc = seq // C
    full = lambda shp: pl.BlockSpec(shp, lambda b, c: (0,) * len(shp))
    col = lambda j: pl.BlockSpec((C, RWKV_W), lambda b, c, j=j: (b * nc + c, j))
    return pl.pallas_call(
        _wkv_chunk_kernel,
        out_shape=(jax.ShapeDtypeStruct((batch * seq, RWKV_W), BF16),
                   jax.ShapeDtypeStruct((batch, HEADS, HEAD, HEAD), F32)),
        grid=(batch, nc),
        in_specs=[col(0), col(1), col(2), col(3),
                  pl.BlockSpec((C, LANES), lambda b, c: (b * nc + c, 0)),
                  full(pvec.shape), full(mul.shape), full(wd.shape), full(wa.shape), full(e.shape)],
        out_specs=(pl.BlockSpec((C, RWKV_W), lambda b, c: (b * nc + c, 0)),
                   pl.BlockSpec((1, HEADS, HEAD, HEAD), lambda b, c: (b, 0, 0, 0))),
        scratch_shapes=[pltpu.VMEM((HEADS // 2, 2 * HEAD, 2 * HEAD), F32),
                        pltpu.VMEM((SUBLANES, 3 * RWKV_W), F32),
                        pltpu.VMEM((SUBLANES, LANES), F32)],
        compiler_params=pltpu.CompilerParams(
            dimension_semantics=("arbitrary", "arbitrary"), vmem_limit_bytes=VMEM_LIMIT),
        name="wkv_chunk",
    )(z_main, z_main, z_main, z_main, z_lora, pvec, mul, wd, wa, e)


def _wkv_step_kernel(zr_ref, zk_ref, zv_ref, zrg_ref, zl_ref, sh_ref, shl_ref, s0_ref, pv_ref,
                     mul_ref, wd_ref, wa_ref, e_ref, o_ref, sout_ref, rows_s, y_s):
    bt = zr_ref.shape[0]
    e = e_ref[...]
    r, k2, v, av, bv, logd = _wkv_prep(
        zr_ref[...], zk_ref[...], zv_ref[...], zl_ref[...],
        sh_ref[:, 0:RWKV_W], sh_ref[:, RWKV_W:2 * RWKV_W], sh_ref[:, 2 * RWKV_W:3 * RWKV_W],
        shl_ref[...], pv_ref, mul_ref, wd_ref, wa_ref, e)
    d = jnp.exp(logd)
    dr = d * r
    dn = lambda x: pltpu.roll(x, RWKV_W - HEAD, 1)
    vals = (av, dn(av), dr, dn(dr), bv, dn(bv), k2, dn(k2), d, dn(d), v,
            _segsum(bv * r, e), _segsum(k2 * r, e))
    for q, val in enumerate(vals):
        for i in range(bt):
            rows_s[q, i] = val[i:i + 1, :]

    row8 = lax.broadcasted_iota(jnp.int32, (SUBLANES, 1), 0)
    lo = lax.broadcasted_iota(jnp.int32, (1, LANES), 1) < HEAD
    top = lax.broadcasted_iota(jnp.int32, (2 * HEAD, 1), 0) < HEAD
    pairs = range(HEADS // 2)

    def rows4(r0, r1, r2, r3):
        return jnp.where(row8 == 0, r0, jnp.where(row8 == 1, r1, jnp.where(row8 == 2, r2,
                         jnp.where(row8 == 3, r3, 0.0))))

    def per_seq(b, carry):
        a0, a1, dr0, dr1, b0, b1, k0, k1, d0, d1, v_f, br_f, kr_f = (rows_s[q, b] for q in range(13))
        h0 = [slice(LANES * p, LANES * p + HEAD) for p in pairs]
        pr = [slice(LANES * p, LANES * (p + 1)) for p in pairs]
        s0 = [s0_ref[b, p] for p in pairs]
        lhs = [_bf(rows4(a0[:, h0[p]], a1[:, h0[p]], dr0[:, h0[p]], dr1[:, h0[p]])) for p in pairs]
        rd = [_dg(lhs[p], _bf(s0[p]), NT) for p in pairs]
        sa = [jnp.where(lo, rd[p][0:1, :], rd[p][1:2, :]) for p in pairs]
        y0 = [jnp.where(lo, rd[p][2:3, :], rd[p][3:4, :]) for p in pairs]
        y_s[b] = jnp.concatenate(
            [y0[p] + sa[p] * br_f[:, pr[p]] + v_f[:, pr[p]] * kr_f[:, pr[p]] for p in pairs], axis=1)
        left = [_bf(rows4(jnp.where(lo, sa[p], 0.0), jnp.where(lo, 0.0, sa[p]),
                          jnp.where(lo, v_f[:, pr[p]], 0.0), jnp.where(lo, 0.0, v_f[:, pr[p]])))
                for p in pairs]
        right = [_bf(rows4(b0[:, h0[p]], b1[:, h0[p]], k0[:, h0[p]], k1[:, h0[p]])) for p in pairs]
        upd = [_dg(left[p], right[p], TN) for p in pairs]
        for p in pairs:
            sout_ref[b, p] = s0[p] * jnp.where(top, d0[:, h0[p]], d1[:, h0[p]]) + upd[p]
        return carry

    lax.fori_loop(0, bt, per_seq, 0)
    y = jnp.concatenate([y_s[i] for i in range(bt)], axis=0)
    o_ref[...] = _wkv_post(y, r, k2, v, zrg_ref[...], pv_ref, e)


def _wkv_step(z_main, z_lora, sh_main, sh_lora, s0, pvec, mul, wd, wa, e, bt):
    nb = z_main.shape[0]
    full = lambda shp: pl.BlockSpec(shp, lambda i: (0,) * len(shp))
    col = lambda j: pl.BlockSpec((bt, RWKV_W), lambda i, j=j: (i, j))
    st_block = (bt, HEADS // 2, 2 * HEAD, HEAD)
    return pl.pallas_call(
        _wkv_step_kernel,
        out_shape=(jax.ShapeDtypeStruct((nb, RWKV_W), BF16),
                   jax.ShapeDtypeStruct(s0.shape, F32)),
        grid=(nb // bt,),
        in_specs=[col(0), col(1), col(2), col(3),
                  pl.BlockSpec((bt, LANES), lambda i: (i, 0)),
                  pl.BlockSpec((bt, 3 * RWKV_W), lambda i: (i, 0)),
                  pl.BlockSpec((bt, LANES), lambda i: (i, 0)),
                  pl.BlockSpec(st_block, lambda i: (i, 0, 0, 0)),
                  full(pvec.shape), full(mul.shape), full(wd.shape), full(wa.shape), full(e.shape)],
        out_specs=(pl.BlockSpec((bt, RWKV_W), lambda i: (i, 0)),
                   pl.BlockSpec(st_block, lambda i: (i, 0, 0, 0))),
        scratch_shapes=[pltpu.VMEM((13, bt, 1, RWKV_W), F32), pltpu.VMEM((bt, 1, RWKV_W), F32)],
        compiler_params=pltpu.CompilerParams(
            dimension_semantics=("arbitrary",), vmem_limit_bytes=VMEM_LIMIT),
        name="wkv_step",
    )(z_main, z_main, z_main, z_main, z_lora, sh_main, sh_lora, s0, pvec, mul, wd, wa, e)


_CW0, _CW1, _CW2, _CW3, _CB, _GXB, _GAB, _LAM = range(8)


def _lru_gates(xc, lp_ref, wg_ref):
    g = _dg(_bf(xc), wg_ref[...], NN)
    gx = _sigmoid(g[:, 0:LRU_W] + _prow(lp_ref, _GXB))
    ga = _sigmoid(g[:, LRU_W:2 * LRU_W] + _prow(lp_ref, _GAB))
    log_a = -LRU_C * ga * _softplus(-_prow(lp_ref, _LAM))
    a = jnp.exp(log_a)
    mult = jnp.sqrt((1.0 - a) * (1.0 + a))
    return a, mult * gx * xc


def _lru_scan_kernel(zx_ref, zg_ref, lp_ref, wg_ref, o_ref, hlast_ref, c8_s, hc_s, a_s, b_s, h_s):
    t = pl.program_id(1)
    nt = pl.num_programs(1)
    tl = zx_ref.shape[0]

    @pl.when(t == 0)
    def _():
        c8_s[...] = jnp.zeros_like(c8_s)
        hc_s[...] = jnp.zeros_like(hc_s)

    zx = zx_ref[...]
    c8 = c8_s[...]
    row8 = lax.broadcasted_iota(jnp.int32, (SUBLANES, 1), 0)
    xc = _prow(lp_ref, _CW3) * zx + _prow(lp_ref, _CB)
    for j in range(1, CONV_W):
        zr_j = pltpu.roll(zx, j, 0)
        top = jnp.where(row8 < j, pltpu.roll(c8, j, 0), zr_j[0:SUBLANES, :])
        xj = jnp.concatenate([top, zr_j[SUBLANES:, :]], axis=0)
        xc = xc + _prow(lp_ref, CONV_W - 1 - j) * xj
    c8_s[...] = zx[tl - SUBLANES:tl, :]

    a, b = _lru_gates(xc, lp_ref, wg_ref)
    a_s[...] = a
    b_s[...] = b

    def blk(i, hc):
        off = pl.multiple_of(i * SUBLANES, SUBLANES)
        a8 = a_s[pl.ds(off, SUBLANES), :]
        b8 = b_s[pl.ds(off, SUBLANES), :]
        for s in (1, 2, 4):
            keep = row8 >= s
            b8 = jnp.where(keep, a8 * pltpu.roll(b8, s, 0) + b8, b8)
            a8 = jnp.where(keep, a8 * pltpu.roll(a8, s, 0), a8)
        hb = b8 + a8 * hc
        h_s[pl.ds(off, SUBLANES), :] = hb
        return jnp.broadcast_to(hb[SUBLANES - 1:SUBLANES, :], hb.shape)

    hc = lax.fori_loop(0, tl // SUBLANES, blk, hc_s[...])
    hc_s[...] = hc
    zg = zg_ref[...]
    o_ref[...] = _bf(h_s[...] * (zg * _sigmoid(zg)))

    @pl.when(t == nt - 1)
    def _():
        hlast_ref[0] = hc[0:1, :]


def _lru_scan(z_main, lp, wg, batch, seq, tl):
    nt = seq // tl
    full = lambda shp: pl.BlockSpec(shp, lambda b, t: (0,) * len(shp))
    col = lambda j: pl.BlockSpec((tl, LRU_W), lambda b, t, j=j: (b * nt + t, j))
    return pl.pallas_call(
        _lru_scan_kernel,
        out_shape=(jax.ShapeDtypeStruct((batch * seq, LRU_W), BF16),
                   jax.ShapeDtypeStruct((batch, 1, LRU_W), F32)),
        grid=(batch, nt),
        in_specs=[col(4), col(5), full(lp.shape), full(wg.shape)],
        out_specs=(pl.BlockSpec((tl, LRU_W), lambda b, t: (b * nt + t, 0)),
                   pl.BlockSpec((1, 1, LRU_W), lambda b, t: (b, 0, 0))),
        scratch_shapes=[pltpu.VMEM((SUBLANES, LRU_W), F32), pltpu.VMEM((SUBLANES, LRU_W), F32),
                        pltpu.VMEM((tl, LRU_W), F32), pltpu.VMEM((tl, LRU_W), F32),
                        pltpu.VMEM((tl, LRU_W), F32)],
        compiler_params=pltpu.CompilerParams(
            dimension_semantics=("arbitrary", "arbitrary"), vmem_limit_bytes=VMEM_LIMIT),
        name="lru_scan",
    )(z_main, z_main, lp, wg)


def _lru_step_kernel(zx_ref, zg_ref, conv_ref, h0_ref, lp_ref, wg_ref, o_ref, hnew_ref):
    zx = zx_ref[...]
    xc = _prow(lp_ref, _CW3) * zx + _prow(lp_ref, _CB)
    for j in range(CONV_W - 1):
        xc = xc + _prow(lp_ref, j) * conv_ref[:, LRU_W * j:LRU_W * (j + 1)]
    a, b = _lru_gates(xc, lp_ref, wg_ref)
    h = a * h0_ref[...] + b
    hnew_ref[...] = h
    zg = zg_ref[...]
    o_ref[...] = _bf(h * (zg * _sigmoid(zg)))


def _lru_step(z_main, conv, h0, lp, wg):
    nb = z_main.shape[0]
    full = lambda shp: pl.BlockSpec(shp, lambda i: (0,) * len(shp))
    col = lambda j: pl.BlockSpec((nb, LRU_W), lambda i, j=j: (0, j))
    return pl.pallas_call(
        _lru_step_kernel,
        out_shape=(jax.ShapeDtypeStruct((nb, LRU_W), BF16), jax.ShapeDtypeStruct((nb, LRU_W), F32)),
        grid=(1,),
        in_specs=[col(4), col(5), full(conv.shape), full(h0.shape), full(lp.shape), full(wg.shape)],
        out_specs=(full((nb, LRU_W)), full((nb, LRU_W))),
        compiler_params=pltpu.CompilerParams(
            dimension_semantics=("arbitrary",), vmem_limit_bytes=VMEM_LIMIT),
        name="lru_step",
    )(z_main, z_main, conv, h0, lp, wg)


def _outproj_kernel(x_ref, or_ref, og_ref, mr_ref, mg_ref, wr_ref, wg_ref, wo_ref, fg_ref,
                    out_ref, *, final):
    y_r = _dg(or_ref[...], wr_ref[...], NN)
    y_g = _dg(og_ref[...], wg_ref[...], NN)
    merged = _sigmoid(mr_ref[...]) * y_r + _sigmoid(mg_ref[...]) * y_g
    out = x_ref[...] + _dg(_bf(merged), wo_ref[...], NN)
    out_ref[...] = _rms(out, fg_ref[...]) if final else out


def _outproj(x, o_r, o_g, z_main, w_r, w_g, w_o, fg, tm, final):
    m, d = x.shape
    const = lambda shp: pl.BlockSpec(shp, lambda i: (0,) * len(shp), pipeline_mode=pl.Buffered(1))
    return pl.pallas_call(
        functools.partial(_outproj_kernel, final=final),
        out_shape=jax.ShapeDtypeStruct((m, d), F32),
        grid=(m // tm,),
        in_specs=[
            pl.BlockSpec((tm, d), lambda i: (i, 0)),
            pl.BlockSpec((tm, RWKV_W), lambda i: (i, 0)),
            pl.BlockSpec((tm, LRU_W), lambda i: (i, 0)),
            pl.BlockSpec((tm, d), lambda i: (i, 3)),
            pl.BlockSpec((tm, d), lambda i: (i, 4)),
            const(w_r.shape), const(w_g.shape), const(w_o.shape), const(fg.shape),
        ],
        out_specs=pl.BlockSpec((tm, d), lambda i: (i, 0)),
        compiler_params=pltpu.CompilerParams(
            dimension_semantics=("arbitrary",), vmem_limit_bytes=VMEM_LIMIT),
        name="outproj",
    )(x, o_r, o_g, z_main, z_main, w_r, w_g, w_o, fg)


def _row_tile(m, want):
    t = min(m, want)
    assert m % t == 0, (m, t)
    return t


def _layer_params(l, w_in, rwkv_mu, w_decay0, w_decay_up, w_iclr0, w_iclr_up, k_k, k_a, r_k,
                  ln_x_g, ln_x_b, w_out_rwkv, conv_w, conv_b, lru_gx_w, lru_gx_b, lru_ga_w,
                  lru_ga_b, lru_lambda, w_out_lru, w_out):
    sh_w = 3 * RWKV_W + 2 * LORA
    w = w_in[l]
    p = {}
    p["w_main"] = jnp.concatenate([w[:, :3 * RWKV_W], w[:, sh_w:]], axis=1).astype(BF16)
    p["w_lora"] = w[:, 3 * RWKV_W:sh_w].astype(BF16)
    mu = rwkv_mu[l]
    rows = [mu[0:RWKV_W], mu[RWKV_W:2 * RWKV_W], mu[2 * RWKV_W:3 * RWKV_W], w_decay0[l], w_iclr0[l],
            k_k[l], k_a[l], r_k[l].reshape(-1), ln_x_g[l], ln_x_b[l]]
    p["pvec"] = jnp.concatenate(
        [jnp.stack(rows), jnp.zeros((16 - len(rows), RWKV_W), F32)], axis=0).astype(F32)
    p["mul"] = jnp.broadcast_to(mu[3 * RWKV_W:sh_w][None, :], (SUBLANES, 2 * LORA)).astype(F32)
    zeros = jnp.zeros((LORA, RWKV_W), F32)
    p["wd"] = jnp.concatenate([w_decay_up[l], zeros], axis=0).astype(BF16)
    p["wa"] = jnp.concatenate([zeros, w_iclr_up[l]], axis=0).astype(BF16)
    lane_head = jnp.arange(LANES) // HEAD
    p["e"] = (lane_head[:, None] == lane_head[None, :]).astype(BF16)
    p["lp"] = jnp.concatenate(
        [conv_w[l], conv_b[l][None], lru_gx_b[l][None], lru_ga_b[l][None], lru_lambda[l][None]],
        axis=0).astype(F32)
    eye = jnp.eye(LRU_BLOCKS, dtype=F32)
    bd = lambda g: (eye[:, None, :, None] * g[:, :, None, :]).reshape(LRU_W, LRU_W)
    p["wg"] = jnp.concatenate([bd(lru_gx_w[l]), bd(lru_ga_w[l])], axis=1).astype(BF16)
    p["w_r"] = w_out_rwkv[l].astype(BF16)
    p["w_g"] = w_out_lru[l].astype(BF16)
    p["w_o"] = w_out[l].astype(BF16)
    return p


def kernel(x_prompt, x_sample, state_shift, state_wkv, state_conv, state_lru, norm_g, w_in, rwkv_mu,
           w_decay0, w_decay_up, w_iclr0, w_iclr_up, k_k, k_a, r_k, ln_x_g, ln_x_b, w_out_rwkv,
           conv_w, conv_b, lru_gx_w, lru_gx_b, lru_ga_w, lru_ga_b, lru_lambda, w_out_lru, w_out,
           final_norm_g):
    bp, seq, d = x_prompt.shape
    bs = x_sample.shape[0]
    assert x_sample.shape[1] == 1 and seq % WKV_CHUNK == 0
    depth = w_in.shape[0]
    sh_w = 3 * RWKV_W + 2 * LORA
    xp = x_prompt.reshape(bp * seq, d)
    xs = x_sample.reshape(bs, d)
    fg = final_norm_g.reshape(1, d)
    outs = [[] for _ in range(8)]
    for l in range(depth):
        p = _layer_params(l, w_in, rwkv_mu, w_decay0, w_decay_up, w_iclr0, w_iclr_up, k_k, k_a, r_k,
                          ln_x_g, ln_x_b, w_out_rwkv, conv_w, conv_b, lru_gx_w, lru_gx_b, lru_ga_w,
                          lru_ga_b, lru_lambda, w_out_lru, w_out)
        g = norm_g[l].reshape(1, d)
        rec = (p["pvec"], p["mul"], p["wd"], p["wa"], p["e"])

        zp, zlp = _inproj(xp, g, p["w_main"], p["w_lora"], _row_tile(bp * seq, 1024), 1024)
        o_r, s_new = _wkv_chunk(zp, zlp, *rec, bp, seq)
        o_g, h_last = _lru_scan(zp, p["lp"], p["wg"], bp, seq, _row_tile(seq, 256))
        last = l == depth - 1
        xp = _outproj(xp, o_r, o_g, zp, p["w_r"], p["w_g"], p["w_o"], fg, _row_tile(bp * seq, 256), last)
        zp3 = zp.reshape(bp, seq, -1)
        outs[0].append(jnp.concatenate([zp3[:, -1, :3 * RWKV_W], zlp.reshape(bp, seq, -1)[:, -1]], axis=-1))
        outs[1].append(s_new)
        outs[2].append(zp3[:, seq - (CONV_W - 1):, 4 * RWKV_W:4 * RWKV_W + LRU_W])
        outs[3].append(h_last.reshape(bp, LRU_W))

        zs, zls = _inproj(xs, g, p["w_main"], p["w_lora"], bs, 1024)
        sh = state_shift[l]
        s_pairs = state_wkv[l].reshape(bs, HEADS // 2, 2 * HEAD, HEAD)
        o_r, s_new = _wkv_step(zs, zls, sh[:, :3 * RWKV_W], sh[:, 3 * RWKV_W:sh_w], s_pairs,
                               *rec, _row_tile(bs, 16))
        s_new = s_new.reshape(bs, HEADS, HEAD, HEAD)
        conv = state_conv[l]
        o_g, h_new = _lru_step(zs, conv.reshape(bs, (CONV_W - 1) * LRU_W), state_lru[l], p["lp"], p["wg"])
        xs = _outproj(xs, o_r, o_g, zs, p["w_r"], p["w_g"], p["w_o"], fg, bs, last)
        z_lx = zs[:, 4 * RWKV_W:4 * RWKV_W + LRU_W]
        outs[4].append(jnp.concatenate([zs[:, :3 * RWKV_W], zls], axis=-1))
        outs[5].append(s_new)
        outs[6].append(jnp.concatenate([conv[:, 1:], z_lx[:, None, :]], axis=1))
        outs[7].append(h_new)

    return (xp.reshape(bp, seq, d), xs.reshape(bs, 1, d)) + tuple(jnp.stack(o) for o in outs)
```

```python
import functools

import jax
import jax.numpy as jnp
from jax import lax
from jax.experimental import pallas as pl
from jax.experimental.pallas import tpu as pltpu

F32 = jnp.float32
BF16 = jnp.bfloat16

HEADS = 16
HEAD = 64
RWKV_W = HEADS * HEAD
LORA = 64
LRU_W = 1024
LRU_BLOCKS = 16
CONV_W = 4
LRU_C = 8.0
RMS_EPS = 1e-6
GN_EPS = 1e-5 * HEAD

LANES = 128
SUBLANES = 8
WKV_CHUNK = 64
VMEM_LIMIT = 56 * 1024 * 1024

NN = (((1,), (0,)), ((), ()))
NT = (((1,), (1,)), ((), ()))
TN = (((0,), (0,)), ((), ()))


def _bf(x):
    return x.astype(BF16)


def _dg(a, b, dn):
    return lax.dot_general(a, b, dn, preferred_element_type=F32)


def _softplus(x):
    return jnp.maximum(x, 0.0) + jnp.log1p(jnp.exp(-jnp.abs(x)))


def _sigmoid(x):
    return 1.0 / (1.0 + jnp.exp(-x))


def _segsum(x, e):
    rows, n = x.shape[0], x.shape[1] // LANES
    stacked = jnp.concatenate([x[:, LANES * j:LANES * (j + 1)] for j in range(n)], axis=0)
    s = _dg(_bf(stacked), e, NN)
    return jnp.concatenate([s[rows * j:rows * (j + 1), :] for j in range(n)], axis=1)


def _rms(x, g):
    return x * lax.rsqrt(jnp.mean(x * x, axis=-1, keepdims=True) + RMS_EPS) * g


def _inproj_kernel(x_ref, g_ref, wa_ref, wb_ref, wl_ref, z_ref, zl_ref, h_ref, *, na):
    j = pl.program_id(1)

    @pl.when(j == 0)
    def _():
        hb = _bf(_rms(x_ref[...], g_ref[...]))
        h_ref[...] = hb
        zl_ref[...] = _dg(hb, wl_ref[...], NN)

    @pl.when(j < na)
    def _():
        z_ref[...] = _dg(h_ref[...], wa_ref[...], NN)

    @pl.when(j >= na)
    def _():
        z_ref[...] = _dg(h_ref[...], wb_ref[...], NN)


def _inproj(x, g, w_a, w_b, w_lora, tm, tn):
    m, d = x.shape
    na = w_a.shape[1] // tn
    n = w_a.shape[1] + w_b.shape[1]
    return pl.pallas_call(
        functools.partial(_inproj_kernel, na=na),
        out_shape=(jax.ShapeDtypeStruct((m, n), F32), jax.ShapeDtypeStruct((m, LANES), F32)),
        grid=(m // tm, n // tn),
        in_specs=[
            pl.BlockSpec((tm, d), lambda i, j: (i, 0)),
            pl.BlockSpec((1, d), lambda i, j: (0, 0)),
            pl.BlockSpec((d, tn), lambda i, j: (0, jnp.minimum(j, na - 1))),
            pl.BlockSpec((d, tn), lambda i, j: (0, jnp.maximum(j - na, 0))),
            pl.BlockSpec((d, LANES), lambda i, j: (0, 0)),
        ],
        out_specs=(
            pl.BlockSpec((tm, tn), lambda i, j: (i, j)),
            pl.BlockSpec((tm, LANES), lambda i, j: (i, 0)),
        ),
        scratch_shapes=[pltpu.VMEM((tm, d), BF16)],
        compiler_params=pltpu.CompilerParams(
            dimension_semantics=("arbitrary", "arbitrary"), vmem_limit_bytes=VMEM_LIMIT),
        name="inproj",
    )(x, g, w_a, w_b, w_lora)


_MU_R, _MU_K, _MU_V, _W0, _A0, _KK, _KA, _RK, _LNG, _LNB = range(10)


def _prow(pv_ref, i):
    return pv_ref[i:i + 1, :]


def _wkv_prep(zr, zk, zv, zl, pr, pk, pv, pl_, pv_ref, mul_ref, wd_ref, wa_ref, e):
    r = zr + _prow(pv_ref, _MU_R) * (pr - zr)
    k = zk + _prow(pv_ref, _MU_K) * (pk - zk)
    v = zv + _prow(pv_ref, _MU_V) * (pv - zv)
    lo = zl + mul_ref[0:1, :] * (pl_ - zl)
    lw = _dg(_bf(jnp.tanh(lo)), wd_ref[...], NN)
    la = _dg(_bf(lo), wa_ref[...], NN)
    wlog = -_softplus(-(_prow(pv_ref, _W0) + lw)) - 0.5
    logd = -jnp.exp(wlog)
    a = _sigmoid(_prow(pv_ref, _A0) + la)
    kk = k * _prow(pv_ref, _KK)
    kk = kk * lax.rsqrt(jnp.maximum(_segsum(kk * kk, e), 1e-24))
    k2 = k * (1.0 + (a - 1.0) * _prow(pv_ref, _KA))
    return r, k2, v, -kk, kk * a, logd


def _wkv_post(y, r, k2, v, zrg, pv_ref, e):
    mu = _segsum(y, e) * (1.0 / HEAD)
    yc = y - mu
    var = _segsum(yc * yc, e) * (1.0 / HEAD)
    yn = yc * lax.rsqrt(var + GN_EPS) * _prow(pv_ref, _LNG) + _prow(pv_ref, _LNB)
    bonus = _segsum(r * k2 * _prow(pv_ref, _RK), e)
    return _bf((yn + bonus * v) * (zrg * _sigmoid(zrg)))


def _wkv_chunk_kernel(zr_ref, zk_ref, zv_ref, zrg_ref, zl_ref, pv_ref, mul_ref, wd_ref, wa_ref,
                      e_ref, o_ref, sout_ref, s_s, prev_s, prevl_s):
    c = pl.program_id(1)
    nc = pl.num_programs(1)
    C = WKV_CHUNK
    assert C == HEAD and 2 * HEAD == LANES
    nb = zr_ref.shape[0]
    rows_all = nb * C
    seqs = range(nb)

    @pl.when(c == 0)
    def _():
        s_s[...] = jnp.zeros_like(s_s)
        prev_s[...] = jnp.zeros_like(prev_s)
        prevl_s[...] = jnp.zeros_like(prevl_s)

    rows = lax.broadcasted_iota(jnp.int32, (rows_all, 1), 0)

    def shifted(z, prev_ref, lanes):
        out = pltpu.roll(z, 1, 0)
        for b in seqs:
            out = jnp.where(rows == b * C, prev_ref[b, 0:1, lanes], out)
        return out

    def flat(ref):
        return ref[...].reshape(rows_all, ref.shape[-1])

    zr, zk, zv, zl = flat(zr_ref), flat(zk_ref), flat(zv_ref), flat(zl_ref)
    seg = [slice(RWKV_W * i, RWKV_W * (i + 1)) for i in range(3)]
    pr = shifted(zr, prev_s, seg[0])
    pk = shifted(zk, prev_s, seg[1])
    pv = shifted(zv, prev_s, seg[2])
    pl_ = shifted(zl, prevl_s, slice(0, LANES))
    for b in seqs:
        last = slice(b * C + C - 1, b * C + C)
        prev_s[b, 0:1, seg[0]] = zr[last, :]
        prev_s[b, 0:1, seg[1]] = zk[last, :]
        prev_s[b, 0:1, seg[2]] = zv[last, :]
        prevl_s[b, 0:1, :] = zl[last, :]

    e = e_ref[...]
    r, k2, v, av, bv, logd = _wkv_prep(zr, zk, zv, zl, pr, pk, pv, pl_, pv_ref, mul_ref,
                                       wd_ref, wa_ref, e)

    ti = lax.broadcasted_iota(jnp.int32, (rows_all, rows_all), 0)
    tj = lax.broadcasted_iota(jnp.int32, (rows_all, rows_all), 1)
    tri = jnp.where((ti >= tj) & ((ti & -C) == (tj & -C)), 1.0, 0.0).astype(BF16)
    d_hi = _bf(logd)
    d_r1 = logd - d_hi.astype(F32)
    d_mid = _bf(d_r1)
    d_lo = _bf(d_r1 - d_mid.astype(F32))
    cum = _dg(tri, d_hi, NN) + (_dg(tri, d_mid, NN) + _dg(tri, d_lo, NN))
    tot_b = [cum[b * C + C - 1:b * C + C, :] for b in seqs]
    tot = jnp.concatenate([jnp.broadcast_to(tot_b[b], (C, RWKV_W)) for b in seqs], axis=0)
    e_in = jnp.exp(cum)
    e_neg = jnp.exp(-cum)
    e_tot = jnp.exp(tot - cum)
    a_t = av * jnp.exp(cum - logd)
    r_t = r * e_in
    k_t = k2 * e_neg
    b_t = bv * e_neg
    k_h = k2 * e_tot
    b_h = bv * e_tot
    p_c = [jnp.exp(tot_b[b]) for b in seqs]

    lane = lax.broadcasted_iota(jnp.int32, (C, LANES), 1)
    trow = lax.broadcasted_iota(jnp.int32, (C, LANES), 0)
    lo = lane < HEAD
    s_in = lane & (HEAD - 1)
    strict = s_in < trow
    incl2 = ((lax.broadcasted_iota(jnp.int32, (C, 2 * LANES), 1) & (HEAD - 1))
             <= lax.broadcasted_iota(jnp.int32, (C, 2 * LANES), 0))
    eye2 = jnp.where(s_in == trow, 1.0, 0.0).astype(F32)
    vrow = lax.broadcasted_iota(jnp.int32, (2 * HEAD, LANES), 0)
    klane = lax.broadcasted_iota(jnp.int32, (2 * HEAD, LANES), 1)
    same_head = (vrow < HEAD) == (klane < HEAD)

    def bd(x):
        z = jnp.zeros_like(x)
        return jnp.concatenate([jnp.where(lo, x, z), jnp.where(lo, z, x)], axis=0)

    npair = HEADS // 2
    units = [(b, p) for b in seqs for p in range(npair)]
    un = range(len(units))
    blk = lambda arr, i: arr[units[i][0] * C:(units[i][0] + 1) * C, LANES * units[i][1]:LANES * (units[i][1] + 1)]
    ar = [_bf(jnp.concatenate([blk(a_t, i), blk(r_t, i)], axis=0)) for i in un]
    bk = [_bf(jnp.concatenate([bd(blk(b_t, i)), bd(blk(k_t, i))], axis=0)) for i in un]
    g = [_dg(ar[i], bk[i], NT) for i in un]
    s0 = [s_s[i] for i in un]
    ars = [_dg(ar[i], _bf(s0[i]), NT) for i in un]
    vbd = [_bf(bd(blk(v, i))) for i in un]
    x = [jnp.where(strict, g[i][0:C, 0:LANES], 0.0) for i in un]
    ak = [jnp.where(strict, g[i][0:C, LANES:2 * LANES], 0.0) for i in un]
    w = [ars[i][0:C, :] + _dg(_bf(ak[i]), vbd[i], NN) for i in un]
    t = [eye2 + x[i] for i in un]
    x = [_dg(_bf(x[i]), _bf(bd(x[i])), NN) for i in un]
    for _ in range(C.bit_length() - 3):
        xt = [_dg(_bf(jnp.concatenate([x[i], t[i]], axis=0)), _bf(bd(x[i])), NN) for i in un]
        x = [xt[i][0:C, :] for i in un]
        t = [t[i] + xt[i][C:2 * C, :] for i in un]
    t = [t[i] + _dg(_bf(t[i]), _bf(bd(x[i])), NN) for i in un]
    u = [_dg(_bf(t[i]), _bf(bd(w[i])), NN) for i in un]
    rbk = [_bf(jnp.where(incl2, g[i][C:2 * C, :], 0.0)) for i in un]
    uvbd = [jnp.concatenate([_bf(bd(u[i])), vbd[i]], axis=0) for i in un]
    y = [ars[i][C:2 * C, :] + _dg(rbk[i], uvbd[i], NN) for i in un]
    uv = [_bf(jnp.concatenate([u[i], blk(v, i)], axis=0)) for i in un]
    bkh = [_bf(jnp.concatenate([blk(b_h, i), blk(k_h, i)], axis=0)) for i in un]
    s1 = [s0[i] * p_c[units[i][0]][:, LANES * units[i][1]:LANES * (units[i][1] + 1)]
          + jnp.where(same_head, _dg(uv[i], bkh[i], TN), 0.0) for i in un]
    for i in un:
        s_s[i] = s1[i]

    y_all = jnp.concatenate(
        [jnp.concatenate(y[b * npair:(b + 1) * npair], axis=1) for b in seqs], axis=0)
    o = _wkv_post(y_all, r, k2, v, flat(zrg_ref), pv_ref, e)
    o_ref[...] = o.reshape(nb, C, RWKV_W)

    @pl.when(c == nc - 1)
    def _():
        for i in un:
            b, p = units[i]
            sout_ref[b, 2 * p] = s1[i][0:HEAD, 0:HEAD]
            sout_ref[b, 2 * p + 1] = s1[i][HEAD:2 * HEAD, HEAD:2 * HEAD]


def _wkv_chunk(z_main, z_lora, pvec, mul, wd, wa, e, batch, seq, nb):
    C = WKV_CHUNK
    nc = seq // C
    full = lambda shp: pl.BlockSpec(shp, lambda b, c: (0,) * len(shp))
    col = lambda j: pl.BlockSpec((nb, C, RWKV_W), lambda b, c, j=j: (b, c, j))
    return pl.pallas_call(
        _wkv_chunk_kernel,
        out_shape=(jax.ShapeDtypeStruct((batch, seq, RWKV_W), BF16),
                   jax.ShapeDtypeStruct((batch, HEADS, HEAD, HEAD), F32)),
        grid=(batch // nb, nc),
        in_specs=[col(0), col(1), col(2), col(3),
                  pl.BlockSpec((nb, C, LANES), lambda b, c: (b, c, 0)),
                  full(pvec.shape), full(mul.shape), full(wd.shape), full(wa.shape), full(e.shape)],
        out_specs=(pl.BlockSpec((nb, C, RWKV_W), lambda b, c: (b, c, 0)),
                   pl.BlockSpec((nb, HEADS, HEAD, HEAD), lambda b, c: (b, 0, 0, 0))),
        scratch_shapes=[pltpu.VMEM((nb * HEADS // 2, 2 * HEAD, 2 * HEAD), F32),
                        pltpu.VMEM((nb, SUBLANES, 3 * RWKV_W), F32),
                        pltpu.VMEM((nb, SUBLANES, LANES), F32)],
        compiler_params=pltpu.CompilerParams(
            dimension_semantics=("arbitrary", "arbitrary"), vmem_limit_bytes=VMEM_LIMIT),
        name="wkv_chunk",
    )(z_main, z_main, z_main, z_main, z_lora, pvec, mul, wd, wa, e)


def _wkv_step_kernel(zr_ref, zk_ref, zv_ref, zrg_ref, zl_ref, sh_ref, shl_ref, s0_ref, pv_ref,
                     mul_ref, wd_ref, wa_ref, e_ref, o_ref, sout_ref, rows_s, y_s):
    bt = zr_ref.shape[0]
    e = e_ref[...]
    r, k2, v, av, bv, logd = _wkv_prep(
        zr_ref[...], zk_ref[...], zv_ref[...], zl_ref[...],
        sh_ref[:, 0:RWKV_W], sh_ref[:, RWKV_W:2 * RWKV_W], sh_ref[:, 2 * RWKV_W:3 * RWKV_W],
        shl_ref[...], pv_ref, mul_ref, wd_ref, wa_ref, e)
    d = jnp.exp(logd)
    dr = d * r
    dn = lambda x: pltpu.roll(x, RWKV_W - HEAD, 1)
    vals = (av, dn(av), dr, dn(dr), bv, dn(bv), k2, dn(k2), d, dn(d), v,
            _segsum(bv * r, e), _segsum(k2 * r, e))
    for q, val in enumerate(vals):
        for i in range(bt):
            rows_s[q, i] = val[i:i + 1, :]

    row8 = lax.broadcasted_iota(jnp.int32, (SUBLANES, 1), 0)
    lo = lax.broadcasted_iota(jnp.int32, (1, LANES), 1) < HEAD
    top = lax.broadcasted_iota(jnp.int32, (2 * HEAD, 1), 0) < HEAD
    pairs = range(HEADS // 2)

    def rows4(r0, r1, r2, r3):
        return jnp.where(row8 == 0, r0, jnp.where(row8 == 1, r1, jnp.where(row8 == 2, r2,
                         jnp.where(row8 == 3, r3, 0.0))))

    def per_seq(b, carry):
        a0, a1, dr0, dr1, b0, b1, k0, k1, d0, d1, v_f, br_f, kr_f = (rows_s[q, b] for q in range(13))
        h0 = [slice(LANES * p, LANES * p + HEAD) for p in pairs]
        pr = [slice(LANES * p, LANES * (p + 1)) for p in pairs]
        s0 = [s0_ref[b, p] for p in pairs]
        lhs = [_bf(rows4(a0[:, h0[p]], a1[:, h0[p]], dr0[:, h0[p]], dr1[:, h0[p]])) for p in pairs]
        rd = [_dg(lhs[p], _bf(s0[p]), NT) for p in pairs]
        sa = [jnp.where(lo, rd[p][0:1, :], rd[p][1:2, :]) for p in pairs]
        y0 = [jnp.where(lo, rd[p][2:3, :], rd[p][3:4, :]) for p in pairs]
        y_s[b] = jnp.concatenate(
            [y0[p] + sa[p] * br_f[:, pr[p]] + v_f[:, pr[p]] * kr_f[:, pr[p]] for p in pairs], axis=1)
        left = [_bf(rows4(jnp.where(lo, sa[p], 0.0), jnp.where(lo, 0.0, sa[p]),
                          jnp.where(lo, v_f[:, pr[p]], 0.0), jnp.where(lo, 0.0, v_f[:, pr[p]])))
                for p in pairs]
        right = [_bf(rows4(b0[:, h0[p]], b1[:, h0[p]], k0[:, h0[p]], k1[:, h0[p]])) for p in pairs]
        upd = [_dg(left[p], right[p], TN) for p in pairs]
        for p in pairs:
            sout_ref[b, p] = s0[p] * jnp.where(top, d0[:, h0[p]], d1[:, h0[p]]) + upd[p]
        return carry

    lax.fori_loop(0, bt, per_seq, 0)
    y = jnp.concatenate([y_s[i] for i in range(bt)], axis=0)
    o_ref[...] = _wkv_post(y, r, k2, v, zrg_ref[...], pv_ref, e)


def _wkv_step(z_main, z_lora, sh_main, sh_lora, s0, pvec, mul, wd, wa, e, bt):
    nb = z_main.shape[0]
    full = lambda shp: pl.BlockSpec(shp, lambda i: (0,) * len(shp))
    col = lambda j: pl.BlockSpec((bt, RWKV_W), lambda i, j=j: (i, j))
    st_block = (bt, HEADS // 2, 2 * HEAD, HEAD)
    return pl.pallas_call(
        _wkv_step_kernel,
        out_shape=(jax.ShapeDtypeStruct((nb, RWKV_W), BF16),
                   jax.ShapeDtypeStruct(s0.shape, F32)),
        grid=(nb // bt,),
        in_specs=[col(0), col(1), col(2), col(3),
                  pl.BlockSpec((bt, LANES), lambda i: (i, 0)),
                  pl.BlockSpec((bt, 3 * RWKV_W), lambda i: (i, 0)),
                  pl.BlockSpec((bt, LANES), lambda i: (i, 0)),
                  pl.BlockSpec(st_block, lambda i: (i, 0, 0, 0)),
                  full(pvec.shape), full(mul.shape), full(wd.shape), full(wa.shape), full(e.shape)],
        out_specs=(pl.BlockSpec((bt, RWKV_W), lambda i: (i, 0)),
                   pl.BlockSpec(st_block, lambda i: (i, 0, 0, 0))),
        scratch_shapes=[pltpu.VMEM((13, bt, 1, RWKV_W), F32), pltpu.VMEM((bt, 1, RWKV_W), F32)],
        compiler_params=pltpu.CompilerParams(
            dimension_semantics=("arbitrary",), vmem_limit_bytes=VMEM_LIMIT),
        name="wkv_step",
    )(z_main, z_main, z_main, z_main, z_lora, sh_main, sh_lora, s0, pvec, mul, wd, wa, e)


_CW0, _CW1, _CW2, _CW3, _CB, _GXB, _GAB, _LAM = range(8)


def _lru_gates(xc, lp_ref, wg_ref):
    g = _dg(_bf(xc), wg_ref[...], NN)
    gx = _sigmoid(g[:, 0:LRU_W] + _prow(lp_ref, _GXB))
    ga = _sigmoid(g[:, LRU_W:2 * LRU_W] + _prow(lp_ref, _GAB))
    log_a = -LRU_C * ga * _softplus(-_prow(lp_ref, _LAM))
    a = jnp.exp(log_a)
    mult = jnp.sqrt((1.0 - a) * (1.0 + a))
    return a, mult * gx * xc


def _lru_scan_kernel(zx_ref, zg_ref, lp_ref, wg_ref, o_ref, hlast_ref, c8_s, hc_s, a_s, b_s, h_s):
    t = pl.program_id(1)
    nt = pl.num_programs(1)
    tl = zx_ref.shape[0]

    @pl.when(t == 0)
    def _():
        c8_s[...] = jnp.zeros_like(c8_s)
        hc_s[...] = jnp.zeros_like(hc_s)

    zx = zx_ref[...]
    c8 = c8_s[...]
    row8 = lax.broadcasted_iota(jnp.int32, (SUBLANES, 1), 0)
    xc = _prow(lp_ref, _CW3) * zx + _prow(lp_ref, _CB)
    for j in range(1, CONV_W):
        zr_j = pltpu.roll(zx, j, 0)
        top = jnp.where(row8 < j, pltpu.roll(c8, j, 0), zr_j[0:SUBLANES, :])
        xj = jnp.concatenate([top, zr_j[SUBLANES:, :]], axis=0)
        xc = xc + _prow(lp_ref, CONV_W - 1 - j) * xj
    c8_s[...] = zx[tl - SUBLANES:tl, :]

    a, b = _lru_gates(xc, lp_ref, wg_ref)
    a_s[...] = a
    b_s[...] = b

    def blk(i, hc):
        off = pl.multiple_of(i * SUBLANES, SUBLANES)
        a8 = a_s[pl.ds(off, SUBLANES), :]
        b8 = b_s[pl.ds(off, SUBLANES), :]
        for s in (1, 2, 4):
            keep = row8 >= s
            b8 = jnp.where(keep, a8 * pltpu.roll(b8, s, 0) + b8, b8)
            a8 = jnp.where(keep, a8 * pltpu.roll(a8, s, 0), a8)
        hb = b8 + a8 * hc
        h_s[pl.ds(off, SUBLANES), :] = hb
        return jnp.broadcast_to(hb[SUBLANES - 1:SUBLANES, :], hb.shape)

    hc = lax.fori_loop(0, tl // SUBLANES, blk, hc_s[...])
    hc_s[...] = hc
    zg = zg_ref[...]
    o_ref[...] = _bf(h_s[...] * (zg * _sigmoid(zg)))

    @pl.when(t == nt - 1)
    def _():
        hlast_ref[0] = hc[0:1, :]


def _lru_scan(z_main, lp, wg, batch, seq, tl):
    nt = seq // tl
    full = lambda shp: pl.BlockSpec(shp, lambda b, t: (0,) * len(shp))
    col = lambda j: pl.BlockSpec((tl, LRU_W), lambda b, t, j=j: (b * nt + t, j))
    return pl.pallas_call(
        _lru_scan_kernel,
        out_shape=(jax.ShapeDtypeStruct((batch * seq, LRU_W), BF16),
                   jax.ShapeDtypeStruct((batch, 1, LRU_W), F32)),
        grid=(batch, nt),
        in_specs=[col(4), col(5), full(lp.shape), full(wg.shape)],
        out_specs=(pl.BlockSpec((tl, LRU_W), lambda b, t: (b * nt + t, 0)),
                   pl.BlockSpec((1, 1, LRU_W), lambda b, t: (b, 0, 0))),
        scratch_shapes=[pltpu.VMEM((SUBLANES, LRU_W), F32), pltpu.VMEM((SUBLANES, LRU_W), F32),
                        pltpu.VMEM((tl, LRU_W), F32), pltpu.VMEM((tl, LRU_W), F32),
                        pltpu.VMEM((tl, LRU_W), F32)],
        compiler_params=pltpu.CompilerParams(
            dimension_semantics=("arbitrary", "arbitrary"), vmem_limit_bytes=VMEM_LIMIT),
        name="lru_scan",
    )(z_main, z_main, lp, wg)


def _lru_step_kernel(zx_ref, zg_ref, conv_ref, h0_ref, lp_ref, wg_ref, o_ref, hnew_ref):
    zx = zx_ref[...]
    xc = _prow(lp_ref, _CW3) * zx + _prow(lp_ref, _CB)
    for j in range(CONV_W - 1):
        xc = xc + _prow(lp_ref, j) * conv_ref[:, LRU_W * j:LRU_W * (j + 1)]
    a, b = _lru_gates(xc, lp_ref, wg_ref)
    h = a * h0_ref[...] + b
    hnew_ref[...] = h
    zg = zg_ref[...]
    o_ref[...] = _bf(h * (zg * _sigmoid(zg)))


def _lru_step(z_main, conv, h0, lp, wg):
    nb = z_main.shape[0]
    full = lambda shp: pl.BlockSpec(shp, lambda i: (0,) * len(shp))
    col = lambda j: pl.BlockSpec((nb, LRU_W), lambda i, j=j: (0, j))
    return pl.pallas_call(
        _lru_step_kernel,
        out_shape=(jax.ShapeDtypeStruct((nb, LRU_W), BF16), jax.ShapeDtypeStruct((nb, LRU_W), F32)),
        grid=(1,),
        in_specs=[col(4), col(5), full(conv.shape), full(h0.shape), full(lp.shape), full(wg.shape)],
        out_specs=(full((nb, LRU_W)), full((nb, LRU_W))),
        compiler_params=pltpu.CompilerParams(
            dimension_semantics=("arbitrary",), vmem_limit_bytes=VMEM_LIMIT),
        name="lru_step",
    )(z_main, z_main, conv, h0, lp, wg)


def _outproj_kernel(x_ref, or_ref, og_ref, mr_ref, mg_ref, wr_ref, wg_ref, wo_ref, fg_ref,
                    out_ref, *, final):
    y_r = _dg(or_ref[...], wr_ref[...], NN)
    y_g = _dg(og_ref[...], wg_ref[...], NN)
    merged = _sigmoid(mr_ref[...]) * y_r + _sigmoid(mg_ref[...]) * y_g
    out = x_ref[...] + _dg(_bf(merged), wo_ref[...], NN)
    out_ref[...] = _rms(out, fg_ref[...]) if final else out


def _outproj(x, o_r, o_g, z_main, w_r, w_g, w_o, fg, tm, final):
    m, d = x.shape
    const = lambda shp: pl.BlockSpec(shp, lambda i: (0,) * len(shp), pipeline_mode=pl.Buffered(1))
    return pl.pallas_call(
        functools.partial(_outproj_kernel, final=final),
        out_shape=jax.ShapeDtypeStruct((m, d), F32),
        grid=(m // tm,),
        in_specs=[
            pl.BlockSpec((tm, d), lambda i: (i, 0)),
            pl.BlockSpec((tm, RWKV_W), lambda i: (i, 0)),
            pl.BlockSpec((tm, LRU_W), lambda i: (i, 0)),
            pl.BlockSpec((tm, d), lambda i: (i, 3)),
            pl.BlockSpec((tm, d), lambda i: (i, 4)),
            const(w_r.shape), const(w_g.shape), const(w_o.shape), const(fg.shape),
        ],
        out_specs=pl.BlockSpec((tm, d), lambda i: (i, 0)),
        compiler_params=pltpu.CompilerParams(
            dimension_semantics=("arbitrary",), vmem_limit_bytes=VMEM_LIMIT),
        name="outproj",
    )(x, o_r, o_g, z_main, z_main, w_r, w_g, w_o, fg)


def _row_tile(m, want):
    t = min(m, want)
    assert m % t == 0, (m, t)
    return t


def _layer_params(l, w_in, rwkv_mu, w_decay0, w_decay_up, w_iclr0, w_iclr_up, k_k, k_a, r_k,
                  ln_x_g, ln_x_b, w_out_rwkv, conv_w, conv_b, lru_gx_w, lru_gx_b, lru_ga_w,
                  lru_ga_b, lru_lambda, w_out_lru, w_out):
    sh_w = 3 * RWKV_W + 2 * LORA
    w = w_in[l]
    p = {}
    p["w_a"] = w[:, :3 * RWKV_W].astype(BF16)
    p["w_b"] = w[:, sh_w:].astype(BF16)
    p["w_lora"] = w[:, 3 * RWKV_W:sh_w].astype(BF16)
    mu = rwkv_mu[l]
    rows = [mu[0:RWKV_W], mu[RWKV_W:2 * RWKV_W], mu[2 * RWKV_W:3 * RWKV_W], w_decay0[l], w_iclr0[l],
            k_k[l], k_a[l], r_k[l].reshape(-1), ln_x_g[l], ln_x_b[l]]
    p["pvec"] = jnp.concatenate(
        [jnp.stack(rows), jnp.zeros((16 - len(rows), RWKV_W), F32)], axis=0).astype(F32)
    p["mul"] = jnp.broadcast_to(mu[3 * RWKV_W:sh_w][None, :], (SUBLANES, 2 * LORA)).astype(F32)
    zeros = jnp.zeros((LORA, RWKV_W), F32)
    p["wd"] = jnp.concatenate([w_decay_up[l], zeros], axis=0).astype(BF16)
    p["wa"] = jnp.concatenate([zeros, w_iclr_up[l]], axis=0).astype(BF16)
    lane_head = jnp.arange(LANES) // HEAD
    p["e"] = (lane_head[:, None] == lane_head[None, :]).astype(BF16)
    p["lp"] = jnp.concatenate(
        [conv_w[l], conv_b[l][None], lru_gx_b[l][None], lru_ga_b[l][None], lru_lambda[l][None]],
        axis=0).astype(F32)
    eye = jnp.eye(LRU_BLOCKS, dtype=F32)
    bd = lambda g: (eye[:, None, :, None] * g[:, :, None, :]).reshape(LRU_W, LRU_W)
    p["wg"] = jnp.concatenate([bd(lru_gx_w[l]), bd(lru_ga_w[l])], axis=1).astype(BF16)
    p["w_r"] = w_out_rwkv[l].astype(BF16)
    p["w_g"] = w_out_lru[l].astype(BF16)
    p["w_o"] = w_out[l].astype(BF16)
    return p


def kernel(x_prompt, x_sample, state_shift, state_wkv, state_conv, state_lru, norm_g, w_in, rwkv_mu,
           w_decay0, w_decay_up, w_iclr0, w_iclr_up, k_k, k_a, r_k, ln_x_g, ln_x_b, w_out_rwkv,
           conv_w, conv_b, lru_gx_w, lru_gx_b, lru_ga_w, lru_ga_b, lru_lambda, w_out_lru, w_out,
           final_norm_g):
    bp, seq, d = x_prompt.shape
    bs = x_sample.shape[0]
    assert x_sample.shape[1] == 1 and seq % WKV_CHUNK == 0
    depth = w_in.shape[0]
    sh_w = 3 * RWKV_W + 2 * LORA
    xp = x_prompt.reshape(bp * seq, d)
    xs = x_sample.reshape(bs, d)
    fg = final_norm_g.reshape(1, d)
    outs = [[] for _ in range(8)]
    for l in range(depth):
        p = _layer_params(l, w_in, rwkv_mu, w_decay0, w_decay_up, w_iclr0, w_iclr_up, k_k, k_a, r_k,
                          ln_x_g, ln_x_b, w_out_rwkv, conv_w, conv_b, lru_gx_w, lru_gx_b, lru_ga_w,
                          lru_ga_b, lru_lambda, w_out_lru, w_out)
        g = norm_g[l].reshape(1, d)
        rec = (p["pvec"], p["mul"], p["wd"], p["wa"], p["e"])

        zp, zlp = _inproj(xp, g, p["w_a"], p["w_b"], p["w_lora"], _row_tile(bp * seq, 1024), 1024)
        zp3 = zp.reshape(bp, seq, -1)
        zlp3 = zlp.reshape(bp, seq, LANES)
        nb = max(n for n in (4, 2, 1) if bp % n == 0)
        o_r, s_new = _wkv_chunk(zp3, zlp3, *rec, bp, seq, nb)
        o_r = o_r.reshape(bp * seq, RWKV_W)
        o_g, h_last = _lru_scan(zp, p["lp"], p["wg"], bp, seq, _row_tile(seq, 256))
        last = l == depth - 1
        xp = _outproj(xp, o_r, o_g, zp, p["w_r"], p["w_g"], p["w_o"], fg, _row_tile(bp * seq, 256), last)
        outs[0].append(jnp.concatenate([zp3[:, -1, :3 * RWKV_W], zlp3[:, -1]], axis=-1))
        outs[1].append(s_new)
        outs[2].append(zp3[:, seq - (CONV_W - 1):, 4 * RWKV_W:4 * RWKV_W + LRU_W])
        outs[3].append(h_last.reshape(bp, LRU_W))

        zs, zls = _inproj(xs, g, p["w_a"], p["w_b"], p["w_lora"], bs, 1024)
        sh = state_shift[l]
        s_pairs = state_wkv[l].reshape(bs, HEADS // 2, 2 * HEAD, HEAD)
        o_r, s_new = _wkv_step(zs, zls, sh[:, :3 * RWKV_W], sh[:, 3 * RWKV_W:sh_w], s_pairs,
                               *rec, _row_tile(bs, 16))
        s_new = s_new.reshape(bs, HEADS, HEAD, HEAD)
        conv = state_conv[l]
        o_g, h_new = _lru_step(zs, conv.reshape(bs, (CONV_W - 1) * LRU_W), state_lru[l], p["lp"], p["wg"])
        xs = _outproj(xs, o_r, o_g, zs, p["w_r"], p["w_g"], p["w_o"], fg, bs, last)
        z_lx = zs[:, 4 * RWKV_W:4 * RWKV_W + LRU_W]
        outs[4].append(jnp.concatenate([zs[:, :3 * RWKV_W], zls], axis=-1))
        outs[5].append(s_new)
        outs[6].append(jnp.concatenate([conv[:, 1:], z_lx[:, None, :]], axis=1))
        outs[7].append(h_new)

    return (xp.reshape(bp, seq, d), xs.reshape(bs, 1, d)) + tuple(jnp.stack(o) for o in outs)
```

```python
import functools

import jax
import jax.numpy as jnp
from jax import lax
from jax.experimental import pallas as pl
from jax.experimental.pallas import tpu as pltpu

F32 = jnp.float32
BF16 = jnp.bfloat16

HEADS = 16
HEAD = 64
RWKV_W = HEADS * HEAD
LORA = 64
LRU_W = 1024
LRU_BLOCKS = 16
CONV_W = 4
LRU_C = 8.0
RMS_EPS = 1e-6
GN_EPS = 1e-5 * HEAD

LANES = 128
SUBLANES = 8
WKV_CHUNK = 64
VMEM_LIMIT = 56 * 1024 * 1024

NN = (((1,), (0,)), ((), ()))
NT = (((1,), (1,)), ((), ()))
TN = (((0,), (0,)), ((), ()))


def _bf(x):
    return x.astype(BF16)


def _dg(a, b, dn):
    return lax.dot_general(a, b, dn, preferred_element_type=F32)


def _softplus(x):
    return jnp.maximum(x, 0.0) + jnp.log1p(jnp.exp(-jnp.abs(x)))


def _sigmoid(x):
    return 1.0 / (1.0 + jnp.exp(-x))


def _segsum(x, e):
    rows, n = x.shape[0], x.shape[1] // LANES
    stacked = jnp.concatenate([x[:, LANES * j:LANES * (j + 1)] for j in range(n)], axis=0)
    s = _dg(_bf(stacked), e, NN)
    return jnp.concatenate([s[rows * j:rows * (j + 1), :] for j in range(n)], axis=1)


def _rms(x, g):
    return x * lax.rsqrt(jnp.mean(x * x, axis=-1, keepdims=True) + RMS_EPS) * g


SHIFT_MAIN = 3 * RWKV_W
LORA_COL = 10 * RWKV_W
LORA_BLOCK = LORA_COL // LANES


def _repack_kernel(w_ref, o_ref):
    n = w_ref.shape[-1]
    lora_end = SHIFT_MAIN + 2 * LORA
    o_ref[:, 0:SHIFT_MAIN] = _bf(w_ref[:, 0:SHIFT_MAIN])
    o_ref[:, SHIFT_MAIN:LORA_COL] = _bf(w_ref[:, lora_end:n])
    o_ref[:, LORA_COL:n] = _bf(w_ref[:, SHIFT_MAIN:lora_end])


def _repack_w_in(w_in, layer, tr):
    _, d, n = w_in.shape
    assert n == LORA_COL + 2 * LORA
    return pl.pallas_call(
        _repack_kernel,
        out_shape=jax.ShapeDtypeStruct((d, n), BF16),
        grid=(d // tr,),
        in_specs=[pl.BlockSpec((None, tr, n), lambda i: (layer, i, 0))],
        out_specs=pl.BlockSpec((tr, n), lambda i: (i, 0)),
        compiler_params=pltpu.CompilerParams(
            dimension_semantics=("arbitrary",), vmem_limit_bytes=VMEM_LIMIT),
        name="repack_w_in",
    )(w_in)


def _inproj_kernel(x_ref, g_ref, w_ref, z_ref, h_ref):
    @pl.when(pl.program_id(1) == 0)
    def _():
        h_ref[...] = _bf(_rms(x_ref[...], g_ref[...]))

    z_ref[...] = _dg(h_ref[...], w_ref[...], NN)


def _inproj(x, g, w, tm, tn):
    m, d = x.shape
    n = w.shape[1]
    return pl.pallas_call(
        _inproj_kernel,
        out_shape=jax.ShapeDtypeStruct((m, n), F32),
        grid=(m // tm, n // tn),
        in_specs=[
            pl.BlockSpec((tm, d), lambda i, j: (i, 0)),
            pl.BlockSpec((1, d), lambda i, j: (0, 0)),
            pl.BlockSpec((d, tn), lambda i, j: (0, j)),
        ],
        out_specs=pl.BlockSpec((tm, tn), lambda i, j: (i, j)),
        scratch_shapes=[pltpu.VMEM((tm, d), BF16)],
        compiler_params=pltpu.CompilerParams(
            dimension_semantics=("arbitrary", "arbitrary"), vmem_limit_bytes=VMEM_LIMIT),
        name="inproj",
    )(x, g, w)


_MU_R, _MU_K, _MU_V, _W0, _A0, _KK, _KA, _RK, _LNG, _LNB = range(10)


def _prow(pv_ref, i):
    return pv_ref[i:i + 1, :]


def _wkv_prep(zr, zk, zv, zl, pr, pk, pv, pl_, pv_ref, mul_ref, wd_ref, wa_ref, e):
    r = zr + _prow(pv_ref, _MU_R) * (pr - zr)
    k = zk + _prow(pv_ref, _MU_K) * (pk - zk)
    v = zv + _prow(pv_ref, _MU_V) * (pv - zv)
    lo = zl + mul_ref[0:1, :] * (pl_ - zl)
    lw = _dg(_bf(jnp.tanh(lo)), wd_ref[...], NN)
    la = _dg(_bf(lo), wa_ref[...], NN)
    wlog = -_softplus(-(_prow(pv_ref, _W0) + lw)) - 0.5
    logd = -jnp.exp(wlog)
    a = _sigmoid(_prow(pv_ref, _A0) + la)
    kk = k * _prow(pv_ref, _KK)
    kk = kk * lax.rsqrt(jnp.maximum(_segsum(kk * kk, e), 1e-24))
    k2 = k * (1.0 + (a - 1.0) * _prow(pv_ref, _KA))
    return r, k2, v, -kk, kk * a, logd


def _wkv_post(y, r, k2, v, zrg, pv_ref, e):
    mu = _segsum(y, e) * (1.0 / HEAD)
    yc = y - mu
    var = _segsum(yc * yc, e) * (1.0 / HEAD)
    yn = yc * lax.rsqrt(var + GN_EPS) * _prow(pv_ref, _LNG) + _prow(pv_ref, _LNB)
    bonus = _segsum(r * k2 * _prow(pv_ref, _RK), e)
    return _bf((yn + bonus * v) * (zrg * _sigmoid(zrg)))


def _wkv_chunk_kernel(zr_ref, zk_ref, zv_ref, zrg_ref, zl_ref, pv_ref, mul_ref, wd_ref, wa_ref,
                      e_ref, o_ref, sout_ref, s_s, prev_s, prevl_s):
    c = pl.program_id(1)
    nc = pl.num_programs(1)
    C = WKV_CHUNK
    assert C == HEAD and 2 * HEAD == LANES
    nb = zr_ref.shape[0]
    rows_all = nb * C
    seqs = range(nb)

    @pl.when(c == 0)
    def _():
        s_s[...] = jnp.zeros_like(s_s)
        prev_s[...] = jnp.zeros_like(prev_s)
        prevl_s[...] = jnp.zeros_like(prevl_s)

    rows = lax.broadcasted_iota(jnp.int32, (rows_all, 1), 0)

    def shifted(z, prev_ref, lanes):
        out = pltpu.roll(z, 1, 0)
        for b in seqs:
            out = jnp.where(rows == b * C, prev_ref[b, 0:1, lanes], out)
        return out

    def flat(ref):
        return ref[...].reshape(rows_all, ref.shape[-1])

    zr, zk, zv, zl = flat(zr_ref), flat(zk_ref), flat(zv_ref), flat(zl_ref)
    seg = [slice(RWKV_W * i, RWKV_W * (i + 1)) for i in range(3)]
    pr = shifted(zr, prev_s, seg[0])
    pk = shifted(zk, prev_s, seg[1])
    pv = shifted(zv, prev_s, seg[2])
    pl_ = shifted(zl, prevl_s, slice(0, LANES))
    for b in seqs:
        last = slice(b * C + C - 1, b * C + C)
        prev_s[b, 0:1, seg[0]] = zr[last, :]
        prev_s[b, 0:1, seg[1]] = zk[last, :]
        prev_s[b, 0:1, seg[2]] = zv[last, :]
        prevl_s[b, 0:1, :] = zl[last, :]

    e = e_ref[...]
    r, k2, v, av, bv, logd = _wkv_prep(zr, zk, zv, zl, pr, pk, pv, pl_, pv_ref, mul_ref,
                                       wd_ref, wa_ref, e)

    ti = lax.broadcasted_iota(jnp.int32, (rows_all, rows_all), 0)
    tj = lax.broadcasted_iota(jnp.int32, (rows_all, rows_all), 1)
    tri = jnp.where((ti >= tj) & ((ti & -C) == (tj & -C)), 1.0, 0.0).astype(BF16)
    d_hi = _bf(logd)
    d_r1 = logd - d_hi.astype(F32)
    d_mid = _bf(d_r1)
    d_lo = _bf(d_r1 - d_mid.astype(F32))
    cum = _dg(tri, d_hi, NN) + (_dg(tri, d_mid, NN) + _dg(tri, d_lo, NN))
    tot_b = [cum[b * C + C - 1:b * C + C, :] for b in seqs]
    tot = jnp.concatenate([jnp.broadcast_to(tot_b[b], (C, RWKV_W)) for b in seqs], axis=0)
    e_in = jnp.exp(cum)
    e_neg = jnp.exp(-cum)
    e_tot = jnp.exp(tot - cum)
    a_t = av * jnp.exp(cum - logd)
    r_t = r * e_in
    k_t = k2 * e_neg
    b_t = bv * e_neg
    k_h = k2 * e_tot
    b_h = bv * e_tot
    p_c = [jnp.exp(tot_b[b]) for b in seqs]

    lane = lax.broadcasted_iota(jnp.int32, (C, LANES), 1)
    trow = lax.broadcasted_iota(jnp.int32, (C, LANES), 0)
    lo = lane < HEAD
    s_in = lane & (HEAD - 1)
    strict = s_in < trow
    incl2 = ((lax.broadcasted_iota(jnp.int32, (C, 2 * LANES), 1) & (HEAD - 1))
             <= lax.broadcasted_iota(jnp.int32, (C, 2 * LANES), 0))
    eye2 = jnp.where(s_in == trow, 1.0, 0.0).astype(F32)
    vrow = lax.broadcasted_iota(jnp.int32, (2 * HEAD, LANES), 0)
    klane = lax.broadcasted_iota(jnp.int32, (2 * HEAD, LANES), 1)
    same_head = (vrow < HEAD) == (klane < HEAD)

    def bd(x):
        z = jnp.zeros_like(x)
        return jnp.concatenate([jnp.where(lo, x, z), jnp.where(lo, z, x)], axis=0)

    npair = HEADS // 2
    units = [(b, p) for b in seqs for p in range(npair)]
    un = range(len(units))
    blk = lambda arr, i: arr[units[i][0] * C:(units[i][0] + 1) * C, LANES * units[i][1]:LANES * (units[i][1] + 1)]
    ar = [_bf(jnp.concatenate([blk(a_t, i), blk(r_t, i)], axis=0)) for i in un]
    bk = [_bf(jnp.concatenate([bd(blk(b_t, i)), bd(blk(k_t, i))], axis=0)) for i in un]
    g = [_dg(ar[i], bk[i], NT) for i in un]
    s0 = [s_s[i] for i in un]
    ars = [_dg(ar[i], _bf(s0[i]), NT) for i in un]
    vbd = [_bf(bd(blk(v, i))) for i in un]
    x = [jnp.where(strict, g[i][0:C, 0:LANES], 0.0) for i in un]
    ak = [jnp.where(strict, g[i][0:C, LANES:2 * LANES], 0.0) for i in un]
    w = [ars[i][0:C, :] + _dg(_bf(ak[i]), vbd[i], NN) for i in un]
    t = [eye2 + x[i] for i in un]
    x = [_dg(_bf(x[i]), _bf(bd(x[i])), NN) for i in un]
    for _ in range(C.bit_length() - 3):
        xt = [_dg(_bf(jnp.concatenate([x[i], t[i]], axis=0)), _bf(bd(x[i])), NN) for i in un]
        x = [xt[i][0:C, :] for i in un]
        t = [t[i] + xt[i][C:2 * C, :] for i in un]
    t = [t[i] + _dg(_bf(t[i]), _bf(bd(x[i])), NN) for i in un]
    u = [_dg(_bf(t[i]), _bf(bd(w[i])), NN) for i in un]
    rbk = [_bf(jnp.where(incl2, g[i][C:2 * C, :], 0.0)) for i in un]
    uvbd = [jnp.concatenate([_bf(bd(u[i])), vbd[i]], axis=0) for i in un]
    y = [ars[i][C:2 * C, :] + _dg(rbk[i], uvbd[i], NN) for i in un]
    uv = [_bf(jnp.concatenate([u[i], blk(v, i)], axis=0)) for i in un]
    bkh = [_bf(jnp.concatenate([blk(b_h, i), blk(k_h, i)], axis=0)) for i in un]
    s1 = [s0[i] * p_c[units[i][0]][:, LANES * units[i][1]:LANES * (units[i][1] + 1)]
          + jnp.where(same_head, _dg(uv[i], bkh[i], TN), 0.0) for i in un]
    for i in un:
        s_s[i] = s1[i]

    y_all = jnp.concatenate(
        [jnp.concatenate(y[b * npair:(b + 1) * npair], axis=1) for b in seqs], axis=0)
    o = _wkv_post(y_all, r, k2, v, flat(zrg_ref), pv_ref, e)
    o_ref[...] = o.reshape(nb, C, RWKV_W)

    @pl.when(c == nc - 1)
    def _():
        for i in un:
            b, p = units[i]
            sout_ref[b, 2 * p] = s1[i][0:HEAD, 0:HEAD]
            sout_ref[b, 2 * p + 1] = s1[i][HEAD:2 * HEAD, HEAD:2 * HEAD]


def _wkv_chunk(z, pvec, mul, wd, wa, e, batch, seq, nb):
    C = WKV_CHUNK
    nc = seq // C
    full = lambda shp: pl.BlockSpec(shp, lambda b, c: (0,) * len(shp))
    col = lambda j: pl.BlockSpec((nb, C, RWKV_W), lambda b, c, j=j: (b, c, j))
    return pl.pallas_call(
        _wkv_chunk_kernel,
        out_shape=(jax.ShapeDtypeStruct((batch, seq, RWKV_W), BF16),
                   jax.ShapeDtypeStruct((batch, HEADS, HEAD, HEAD), F32)),
        grid=(batch // nb, nc),
        in_specs=[col(0), col(1), col(2), col(3),
                  pl.BlockSpec((nb, C, LANES), lambda b, c: (b, c, LORA_BLOCK)),
                  full(pvec.shape), full(mul.shape), full(wd.shape), full(wa.shape), full(e.shape)],
        out_specs=(pl.BlockSpec((nb, C, RWKV_W), lambda b, c: (b, c, 0)),
                   pl.BlockSpec((nb, HEADS, HEAD, HEAD), lambda b, c: (b, 0, 0, 0))),
        scratch_shapes=[pltpu.VMEM((nb * HEADS // 2, 2 * HEAD, 2 * HEAD), F32),
                        pltpu.VMEM((nb, SUBLANES, 3 * RWKV_W), F32),
                        pltpu.VMEM((nb, SUBLANES, LANES), F32)],
        compiler_params=pltpu.CompilerParams(
            dimension_semantics=("arbitrary", "arbitrary"), vmem_limit_bytes=VMEM_LIMIT),
        name="wkv_chunk",
    )(z, z, z, z, z, pvec, mul, wd, wa, e)


def _wkv_step_kernel(zr_ref, zk_ref, zv_ref, zrg_ref, zl_ref, sh_ref, shl_ref, s0_ref, pv_ref,
                     mul_ref, wd_ref, wa_ref, e_ref, o_ref, sout_ref,
                     at_s, drt_s, bt_s, kt_s, dt_s, vt_s, brt_s, krt_s, yt_s, keep_s):
    h = pl.program_id(0)
    nh = pl.num_programs(0)
    nseq = zr_ref.shape[0]

    @pl.when(h == 0)
    def _():
        e = e_ref[...]
        r, k2, v, av, bv, logd = _wkv_prep(
            zr_ref[...], zk_ref[...], zv_ref[...], zl_ref[...],
            sh_ref[:, 0:RWKV_W], sh_ref[:, RWKV_W:2 * RWKV_W], sh_ref[:, 2 * RWKV_W:3 * RWKV_W],
            shl_ref[...], pv_ref, mul_ref, wd_ref, wa_ref, e)
        d = jnp.exp(logd)
        at_s[...] = av.T
        drt_s[...] = (d * r).T
        bt_s[...] = bv.T
        kt_s[...] = k2.T
        dt_s[...] = d.T
        vt_s[...] = v.T
        brt_s[...] = jnp.sum((bv * r).T.reshape(HEADS, HEAD, nseq), axis=1)
        krt_s[...] = jnp.sum((k2 * r).T.reshape(HEADS, HEAD, nseq), axis=1)
        keep_s[0] = r
        keep_s[1] = k2
        keep_s[2] = v

    base = pl.multiple_of(h * HEAD, HEAD)
    rows = pl.ds(base, HEAD)
    a_h, dr_h, b_h, k_h, d_h = at_s[rows, :], drt_s[rows, :], bt_s[rows, :], kt_s[rows, :], dt_s[rows, :]
    br_h = brt_s[pl.ds(h, 1), :]
    kr_h = krt_s[pl.ds(h, 1), :]

    def value_rows(g, carry):
        off = pl.multiple_of(base + g * SUBLANES, SUBLANES)
        v8 = vt_s[pl.ds(off, SUBLANES), :]
        ys = []
        for j in range(SUBLANES):
            vi = g * SUBLANES + j
            s_v = s0_ref[0, vi]
            sa = jnp.sum(s_v * a_h, axis=0, keepdims=True)
            y0 = jnp.sum(s_v * dr_h, axis=0, keepdims=True)
            v_v = v8[j:j + 1, :]
            sout_ref[0, vi] = s_v * d_h + sa * b_h + v_v * k_h
            ys.append(y0 + sa * br_h + v_v * kr_h)
        yt_s[pl.ds(off, SUBLANES), :] = jnp.concatenate(ys, axis=0)
        return carry

    lax.fori_loop(0, HEAD // SUBLANES, value_rows, 0)

    @pl.when(h == nh - 1)
    def _():
        o_ref[...] = _wkv_post(yt_s[...].T, keep_s[0], keep_s[1], keep_s[2], zrg_ref[...], pv_ref,
                               e_ref[...])


def _wkv_step(z, sh_main, sh_lora, s0t, pvec, mul, wd, wa, e):
    nseq = z.shape[0]
    full = lambda shp: pl.BlockSpec(shp, lambda i: (0,) * len(shp))
    col = lambda j: pl.BlockSpec((nseq, RWKV_W), lambda i, j=j: (0, j))
    st_block = (1, HEAD, HEAD, nseq)
    wide = pltpu.VMEM((RWKV_W, nseq), F32)
    return pl.pallas_call(
        _wkv_step_kernel,
        out_shape=(jax.ShapeDtypeStruct((nseq, RWKV_W), BF16),
                   jax.ShapeDtypeStruct(s0t.shape, F32)),
        grid=(HEADS,),
        in_specs=[col(0), col(1), col(2), col(3),
                  pl.BlockSpec((nseq, LANES), lambda i: (0, LORA_BLOCK)),
                  full(sh_main.shape), full(sh_lora.shape),
                  pl.BlockSpec(st_block, lambda i: (i, 0, 0, 0)),
                  full(pvec.shape), full(mul.shape), full(wd.shape), full(wa.shape), full(e.shape)],
        out_specs=(full((nseq, RWKV_W)),
                   pl.BlockSpec(st_block, lambda i: (i, 0, 0, 0))),
        scratch_shapes=[wide] * 6 + [pltpu.VMEM((HEADS, nseq), F32)] * 2
                       + [wide, pltpu.VMEM((3, nseq, RWKV_W), F32)],
        compiler_params=pltpu.CompilerParams(
            dimension_semantics=("arbitrary",), vmem_limit_bytes=VMEM_LIMIT),
        name="wkv_step",
    )(z, z, z, z, z, sh_main, sh_lora, s0t, pvec, mul, wd, wa, e)


_CW0, _CW1, _CW2, _CW3, _CB, _GXB, _GAB, _LAM = range(8)


def _lru_gates(xc, lp_ref, wg_ref):
    g = _dg(_bf(xc), wg_ref[...], NN)
    gx = _sigmoid(g[:, 0:LRU_W] + _prow(lp_ref, _GXB))
    ga = _sigmoid(g[:, LRU_W:2 * LRU_W] + _prow(lp_ref, _GAB))
    log_a = -LRU_C * ga * _softplus(-_prow(lp_ref, _LAM))
    a = jnp.exp(log_a)
    mult = jnp.sqrt((1.0 - a) * (1.0 + a))
    return a, mult * gx * xc


def _lru_scan_kernel(zx_ref, zg_ref, lp_ref, wg_ref, o_ref, hlast_ref, c8_s, hc_s, a_s, b_s, h_s):
    t = pl.program_id(1)
    nt = pl.num_programs(1)
    tl = zx_ref.shape[0]

    @pl.when(t == 0)
    def _():
        c8_s[...] = jnp.zeros_like(c8_s)
        hc_s[...] = jnp.zeros_like(hc_s)

    zx = zx_ref[...]
    c8 = c8_s[...]
    row8 = lax.broadcasted_iota(jnp.int32, (SUBLANES, 1), 0)
    xc = _prow(lp_ref, _CW3) * zx + _prow(lp_ref, _CB)
    for j in range(1, CONV_W):
        zr_j = pltpu.roll(zx, j, 0)
        top = jnp.where(row8 < j, pltpu.roll(c8, j, 0), zr_j[0:SUBLANES, :])
        xj = jnp.concatenate([top, zr_j[SUBLANES:, :]], axis=0)
        xc = xc + _prow(lp_ref, CONV_W - 1 - j) * xj
    c8_s[...] = zx[tl - SUBLANES:tl, :]

    a, b = _lru_gates(xc, lp_ref, wg_ref)
    a_s[...] = a
    b_s[...] = b

    def blk(i, hc):
        off = pl.multiple_of(i * SUBLANES, SUBLANES)
        a8 = a_s[pl.ds(off, SUBLANES), :]
        b8 = b_s[pl.ds(off, SUBLANES), :]
        for s in (1, 2, 4):
            keep = row8 >= s
            b8 = jnp.where(keep, a8 * pltpu.roll(b8, s, 0) + b8, b8)
            a8 = jnp.where(keep, a8 * pltpu.roll(a8, s, 0), a8)
        hb = b8 + a8 * hc
        h_s[pl.ds(off, SUBLANES), :] = hb
        return jnp.broadcast_to(hb[SUBLANES - 1:SUBLANES, :], hb.shape)

    hc = lax.fori_loop(0, tl // SUBLANES, blk, hc_s[...])
    hc_s[...] = hc
    zg = zg_ref[...]
    o_ref[...] = _bf(h_s[...] * (zg * _sigmoid(zg)))

    @pl.when(t == nt - 1)
    def _():
        hlast_ref[0] = hc[0:1, :]


def _lru_scan(z_main, lp, wg, batch, seq, tl):
    nt = seq // tl
    full = lambda shp: pl.BlockSpec(shp, lambda b, t: (0,) * len(shp))
    col = lambda j: pl.BlockSpec((tl, LRU_W), lambda b, t, j=j: (b * nt + t, j))
    return pl.pallas_call(
        _lru_scan_kernel,
        out_shape=(jax.ShapeDtypeStruct((batch * seq, LRU_W), BF16),
                   jax.ShapeDtypeStruct((batch, 1, LRU_W), F32)),
        grid=(batch, nt),
        in_specs=[col(4), col(5), full(lp.shape), full(wg.shape)],
        out_specs=(pl.BlockSpec((tl, LRU_W), lambda b, t: (b * nt + t, 0)),
                   pl.BlockSpec((1, 1, LRU_W), lambda b, t: (b, 0, 0))),
        scratch_shapes=[pltpu.VMEM((SUBLANES, LRU_W), F32), pltpu.VMEM((SUBLANES, LRU_W), F32),
                        pltpu.VMEM((tl, LRU_W), F32), pltpu.VMEM((tl, LRU_W), F32),
                        pltpu.VMEM((tl, LRU_W), F32)],
        compiler_params=pltpu.CompilerParams(
            dimension_semantics=("arbitrary", "arbitrary"), vmem_limit_bytes=VMEM_LIMIT),
        name="lru_scan",
    )(z_main, z_main, lp, wg)


def _lru_step_kernel(zx_ref, zg_ref, conv_ref, h0_ref, lp_ref, wg_ref, o_ref, hnew_ref):
    zx = zx_ref[...]
    xc = _prow(lp_ref, _CW3) * zx + _prow(lp_ref, _CB)
    for j in range(CONV_W - 1):
        xc = xc + _prow(lp_ref, j) * conv_ref[:, LRU_W * j:LRU_W * (j + 1)]
    a, b = _lru_gates(xc, lp_ref, wg_ref)
    h = a * h0_ref[...] + b
    hnew_ref[...] = h
    zg = zg_ref[...]
    o_ref[...] = _bf(h * (zg * _sigmoid(zg)))


def _lru_step(z_main, conv, h0, lp, wg):
    nb = z_main.shape[0]
    full = lambda shp: pl.BlockSpec(shp, lambda i: (0,) * len(shp))
    col = lambda j: pl.BlockSpec((nb, LRU_W), lambda i, j=j: (0, j))
    return pl.pallas_call(
        _lru_step_kernel,
        out_shape=(jax.ShapeDtypeStruct((nb, LRU_W), BF16), jax.ShapeDtypeStruct((nb, LRU_W), F32)),
        grid=(1,),
        in_specs=[col(4), col(5), full(conv.shape), full(h0.shape), full(lp.shape), full(wg.shape)],
        out_specs=(full((nb, LRU_W)), full((nb, LRU_W))),
        compiler_params=pltpu.CompilerParams(
            dimension_semantics=("arbitrary",), vmem_limit_bytes=VMEM_LIMIT),
        name="lru_step",
    )(z_main, z_main, conv, h0, lp, wg)


def _outproj_kernel(x_ref, or_ref, og_ref, mr_ref, mg_ref, wr_ref, wg_ref, wo_ref, fg_ref,
                    out_ref, *, final):
    y_r = _dg(or_ref[...], wr_ref[...], NN)
    y_g = _dg(og_ref[...], wg_ref[...], NN)
    merged = _sigmoid(mr_ref[...]) * y_r + _sigmoid(mg_ref[...]) * y_g
    out = x_ref[...] + _dg(_bf(merged), wo_ref[...], NN)
    out_ref[...] = _rms(out, fg_ref[...]) if final else out


def _outproj(x, o_r, o_g, z_main, w_r, w_g, w_o, fg, tm, final):
    m, d = x.shape
    const = lambda shp: pl.BlockSpec(shp, lambda i: (0,) * len(shp), pipeline_mode=pl.Buffered(1))
    return pl.pallas_call(
        functools.partial(_outproj_kernel, final=final),
        out_shape=jax.ShapeDtypeStruct((m, d), F32),
        grid=(m // tm,),
        in_specs=[
            pl.BlockSpec((tm, d), lambda i: (i, 0)),
            pl.BlockSpec((tm, RWKV_W), lambda i: (i, 0)),
            pl.BlockSpec((tm, LRU_W), lambda i: (i, 0)),
            pl.BlockSpec((tm, d), lambda i: (i, 3)),
            pl.BlockSpec((tm, d), lambda i: (i, 4)),
            const(w_r.shape), const(w_g.shape), const(w_o.shape), const(fg.shape),
        ],
        out_specs=pl.BlockSpec((tm, d), lambda i: (i, 0)),
        compiler_params=pltpu.CompilerParams(
            dimension_semantics=("arbitrary",), vmem_limit_bytes=VMEM_LIMIT),
        name="outproj",
    )(x, o_r, o_g, z_main, z_main, w_r, w_g, w_o, fg)


def _row_tile(m, want):
    t = min(m, want)
    assert m % t == 0, (m, t)
    return t


def _layer_params(l, w_in, rwkv_mu, w_decay0, w_decay_up, w_iclr0, w_iclr_up, k_k, k_a, r_k,
                  ln_x_g, ln_x_b, w_out_rwkv, conv_w, conv_b, lru_gx_w, lru_gx_b, lru_ga_w,
                  lru_ga_b, lru_lambda, w_out_lru, w_out):
    sh_w = 3 * RWKV_W + 2 * LORA
    p = {}
    mu = rwkv_mu[l]
    rows = [mu[0:RWKV_W], mu[RWKV_W:2 * RWKV_W], mu[2 * RWKV_W:3 * RWKV_W], w_decay0[l], w_iclr0[l],
            k_k[l], k_a[l], r_k[l].reshape(-1), ln_x_g[l], ln_x_b[l]]
    p["pvec"] = jnp.concatenate(
        [jnp.stack(rows), jnp.zeros((16 - len(rows), RWKV_W), F32)], axis=0).astype(F32)
    p["mul"] = jnp.broadcast_to(mu[3 * RWKV_W:sh_w][None, :], (SUBLANES, 2 * LORA)).astype(F32)
    zeros = jnp.zeros((LORA, RWKV_W), F32)
    p["wd"] = jnp.concatenate([w_decay_up[l], zeros], axis=0).astype(BF16)
    p["wa"] = jnp.concatenate([zeros, w_iclr_up[l]], axis=0).astype(BF16)
    lane_head = jnp.arange(LANES) // HEAD
    p["e"] = (lane_head[:, None] == lane_head[None, :]).astype(BF16)
    p["lp"] = jnp.concatenate(
        [conv_w[l], conv_b[l][None], lru_gx_b[l][None], lru_ga_b[l][None], lru_lambda[l][None]],
        axis=0).astype(F32)
    eye = jnp.eye(LRU_BLOCKS, dtype=F32)
    bd = lambda g: (eye[:, None, :, None] * g[:, :, None, :]).reshape(LRU_W, LRU_W)
    p["wg"] = jnp.concatenate([bd(lru_gx_w[l]), bd(lru_ga_w[l])], axis=1).astype(BF16)
    p["w_r"] = w_out_rwkv[l].astype(BF16)
    p["w_g"] = w_out_lru[l].astype(BF16)
    p["w_o"] = w_out[l].astype(BF16)
    return p


def kernel(x_prompt, x_sample, state_shift, state_wkv, state_conv, state_lru, norm_g, w_in, rwkv_mu,
           w_decay0, w_decay_up, w_iclr0, w_iclr_up, k_k, k_a, r_k, ln_x_g, ln_x_b, w_out_rwkv,
           conv_w, conv_b, lru_gx_w, lru_gx_b, lru_ga_w, lru_ga_b, lru_lambda, w_out_lru, w_out,
           final_norm_g):
    bp, seq, d = x_prompt.shape
    bs = x_sample.shape[0]
    assert x_sample.shape[1] == 1 and seq % WKV_CHUNK == 0
    depth = w_in.shape[0]
    sh_w = 3 * RWKV_W + 2 * LORA
    xp = x_prompt.reshape(bp * seq, d)
    xs = x_sample.reshape(bs, d)
    fg = final_norm_g.reshape(1, d)
    outs = [[] for _ in range(8)]
    for l in range(depth):
        p = _layer_params(l, w_in, rwkv_mu, w_decay0, w_decay_up, w_iclr0, w_iclr_up, k_k, k_a, r_k,
                          ln_x_g, ln_x_b, w_out_rwkv, conv_w, conv_b, lru_gx_w, lru_gx_b, lru_ga_w,
                          lru_ga_b, lru_lambda, w_out_lru, w_out)
        g = norm_g[l].reshape(1, d)
        rec = (p["pvec"], p["mul"], p["wd"], p["wa"], p["e"])

        w = _repack_w_in(w_in, l, _row_tile(d, 256))
        tn = w.shape[1] // 9

        zp = _inproj(xp, g, w, _row_tile(bp * seq, 1024), tn)
        zp3 = zp.reshape(bp, seq, -1)
        nb = max(n for n in (4, 2, 1) if bp % n == 0)
        o_r, s_new = _wkv_chunk(zp3, *rec, bp, seq, nb)
        o_r = o_r.reshape(bp * seq, RWKV_W)
        o_g, h_last = _lru_scan(zp, p["lp"], p["wg"], bp, seq, _row_tile(seq, 256))
        last = l == depth - 1
        xp = _outproj(xp, o_r, o_g, zp, p["w_r"], p["w_g"], p["w_o"], fg, _row_tile(bp * seq, 256), last)
        outs[0].append(jnp.concatenate([zp3[:, -1, :SHIFT_MAIN], zp3[:, -1, LORA_COL:]], axis=-1))
        outs[1].append(s_new)
        outs[2].append(zp3[:, seq - (CONV_W - 1):, 4 * RWKV_W:4 * RWKV_W + LRU_W])
        outs[3].append(h_last.reshape(bp, LRU_W))

        zs = _inproj(xs, g, w, bs, tn)
        sh = state_shift[l]
        s0t = jnp.transpose(state_wkv[l], (1, 2, 3, 0))
        o_r, s_new = _wkv_step(zs, sh[:, :SHIFT_MAIN], sh[:, SHIFT_MAIN:sh_w], s0t, *rec)
        s_new = jnp.transpose(s_new, (3, 0, 1, 2))
        conv = state_conv[l]
        o_g, h_new = _lru_step(zs, conv.reshape(bs, (CONV_W - 1) * LRU_W), state_lru[l], p["lp"], p["wg"])
        xs = _outproj(xs, o_r, o_g, zs, p["w_r"], p["w_g"], p["w_o"], fg, bs, last)
        z_lx = zs[:, 4 * RWKV_W:4 * RWKV_W + LRU_W]
        outs[4].append(jnp.concatenate([zs[:, :SHIFT_MAIN], zs[:, LORA_COL:]], axis=-1))
        outs[5].append(s_new)
        outs[6].append(jnp.concatenate([conv[:, 1:], z_lx[:, None, :]], axis=1))
        outs[7].append(h_new)

    return (xp.reshape(bp, seq, d), xs.reshape(bs, 1, d)) + tuple(jnp.stack(o) for o in outs)
```

```python
import functools

import jax
import jax.numpy as jnp
from jax import lax
from jax.experimental import pallas as pl
from jax.experimental.pallas import tpu as pltpu

F32 = jnp.float32
BF16 = jnp.bfloat16

HEADS = 16
HEAD = 64
RWKV_W = HEADS * HEAD
LORA = 64
LRU_W = 1024
LRU_BLOCKS = 16
CONV_W = 4
LRU_C = 8.0
RMS_EPS = 1e-6
GN_EPS = 1e-5 * HEAD
DECAY_SCALE = 0.6065306597126334

LANES = 128
SUBLANES = 8
WKV_CHUNK = 64
VMEM_LIMIT = 56 * 1024 * 1024

NN = (((1,), (0,)), ((), ()))
NT = (((1,), (1,)), ((), ()))
TN = (((0,), (0,)), ((), ()))


def _bf(x):
    return x.astype(BF16)


def _dg(a, b, dn):
    return lax.dot_general(a, b, dn, preferred_element_type=F32)


def _softplus(x):
    return jnp.maximum(x, 0.0) + jnp.log1p(jnp.exp(-jnp.abs(x)))


def _sigmoid(x):
    return 1.0 / (1.0 + jnp.exp(-x))


def _segsum(x, e):
    rows, n = x.shape[0], x.shape[1] // LANES
    stacked = jnp.concatenate([x[:, LANES * j:LANES * (j + 1)] for j in range(n)], axis=0)
    s = _dg(_bf(stacked), e, NN)
    return jnp.concatenate([s[rows * j:rows * (j + 1), :] for j in range(n)], axis=1)


def _rms(x, g):
    return x * lax.rsqrt(jnp.mean(x * x, axis=-1, keepdims=True) + RMS_EPS) * g


SHIFT_MAIN = 3 * RWKV_W
LORA_COL = 10 * RWKV_W
LORA_BLOCK = LORA_COL // LANES


def _repack_kernel(w_ref, o_ref):
    n = w_ref.shape[-1]
    lora_end = SHIFT_MAIN + 2 * LORA
    o_ref[:, 0:SHIFT_MAIN] = _bf(w_ref[:, 0:SHIFT_MAIN])
    o_ref[:, SHIFT_MAIN:LORA_COL] = _bf(w_ref[:, lora_end:n])
    o_ref[:, LORA_COL:n] = _bf(w_ref[:, SHIFT_MAIN:lora_end])


def _repack_w_in(w_in, layer, tr):
    _, d, n = w_in.shape
    assert n == LORA_COL + 2 * LORA
    return pl.pallas_call(
        _repack_kernel,
        out_shape=jax.ShapeDtypeStruct((d, n), BF16),
        grid=(d // tr,),
        in_specs=[pl.BlockSpec((None, tr, n), lambda i: (layer, i, 0))],
        out_specs=pl.BlockSpec((tr, n), lambda i: (i, 0)),
        compiler_params=pltpu.CompilerParams(
            dimension_semantics=("arbitrary",), vmem_limit_bytes=VMEM_LIMIT),
        name="repack_w_in",
    )(w_in)


def _inproj_kernel(x_ref, g_ref, w_ref, z_ref, h_ref):
    @pl.when(pl.program_id(1) == 0)
    def _():
        h_ref[...] = _bf(_rms(x_ref[...], g_ref[...]))

    z_ref[...] = _dg(h_ref[...], w_ref[...], NN)


def _inproj(x, g, w, tm, tn):
    m, d = x.shape
    n = w.shape[1]
    return pl.pallas_call(
        _inproj_kernel,
        out_shape=jax.ShapeDtypeStruct((m, n), F32),
        grid=(m // tm, n // tn),
        in_specs=[
            pl.BlockSpec((tm, d), lambda i, j: (i, 0)),
            pl.BlockSpec((1, d), lambda i, j: (0, 0)),
            pl.BlockSpec((d, tn), lambda i, j: (0, j)),
        ],
        out_specs=pl.BlockSpec((tm, tn), lambda i, j: (i, j)),
        scratch_shapes=[pltpu.VMEM((tm, d), BF16)],
        compiler_params=pltpu.CompilerParams(
            dimension_semantics=("arbitrary", "arbitrary"), vmem_limit_bytes=VMEM_LIMIT),
        name="inproj",
    )(x, g, w)


_MU_R, _MU_K, _MU_V, _W0, _A0, _KK, _KA, _RK, _LNG, _LNB = range(10)


def _prow(pv_ref, i):
    return pv_ref[i:i + 1, :]


def _wkv_prep(zr, zk, zv, zl, pr, pk, pv, pl_, pv_ref, mul_ref, wd_ref, wa_ref, e):
    r = zr + _prow(pv_ref, _MU_R) * (pr - zr)
    k = zk + _prow(pv_ref, _MU_K) * (pk - zk)
    v = zv + _prow(pv_ref, _MU_V) * (pv - zv)
    lo = zl + mul_ref[0:1, :] * (pl_ - zl)
    lw = _dg(_bf(jnp.tanh(lo)), wd_ref[...], NN)
    la = _dg(_bf(lo), wa_ref[...], NN)
    logd = -DECAY_SCALE * _sigmoid(_prow(pv_ref, _W0) + lw)
    a = _sigmoid(_prow(pv_ref, _A0) + la)
    kk = k * _prow(pv_ref, _KK)
    kk = kk * lax.rsqrt(jnp.maximum(_segsum(kk * kk, e), 1e-24))
    k2 = k * (1.0 + (a - 1.0) * _prow(pv_ref, _KA))
    return r, k2, v, -kk, kk * a, logd


def _wkv_post(y, r, k2, v, zrg, pv_ref, e):
    mu = _segsum(y, e) * (1.0 / HEAD)
    yc = y - mu
    var = _segsum(yc * yc, e) * (1.0 / HEAD)
    yn = yc * lax.rsqrt(var + GN_EPS) * _prow(pv_ref, _LNG) + _prow(pv_ref, _LNB)
    bonus = _segsum(r * k2 * _prow(pv_ref, _RK), e)
    return _bf((yn + bonus * v) * (zrg * _sigmoid(zrg)))


def _wkv_chunk_kernel(zr_ref, zk_ref, zv_ref, zrg_ref, zl_ref, pv_ref, mul_ref, wd_ref, wa_ref,
                      e_ref, o_ref, sout_ref, s_s, prev_s, prevl_s):
    c = pl.program_id(1)
    nc = pl.num_programs(1)
    C = WKV_CHUNK
    assert C == HEAD and 2 * HEAD == LANES
    nb = zr_ref.shape[0]
    rows_all = nb * C
    seqs = range(nb)

    @pl.when(c == 0)
    def _():
        s_s[...] = jnp.zeros_like(s_s)
        prev_s[...] = jnp.zeros_like(prev_s)
        prevl_s[...] = jnp.zeros_like(prevl_s)

    first = lax.broadcasted_iota(jnp.int32, (SUBLANES, 1), 0) == 0

    def shifted(z, prev_ref, lanes):
        rolled = pltpu.roll(z, 1, 0)
        pieces = []
        for b in seqs:
            head = jnp.where(first, prev_ref[b, 0:1, lanes], rolled[b * C:b * C + SUBLANES, :])
            pieces += [head, rolled[b * C + SUBLANES:(b + 1) * C, :]]
        return jnp.concatenate(pieces, axis=0)

    def flat(ref):
        return ref[...].reshape(rows_all, ref.shape[-1])

    zr, zk, zv, zl = flat(zr_ref), flat(zk_ref), flat(zv_ref), flat(zl_ref)
    seg = [slice(RWKV_W * i, RWKV_W * (i + 1)) for i in range(3)]
    pr = shifted(zr, prev_s, seg[0])
    pk = shifted(zk, prev_s, seg[1])
    pv = shifted(zv, prev_s, seg[2])
    pl_ = shifted(zl, prevl_s, slice(0, LANES))
    for b in seqs:
        last = slice(b * C + C - 1, b * C + C)
        prev_s[b, 0:1, seg[0]] = zr[last, :]
        prev_s[b, 0:1, seg[1]] = zk[last, :]
        prev_s[b, 0:1, seg[2]] = zv[last, :]
        prevl_s[b, 0:1, :] = zl[last, :]

    e = e_ref[...]
    r, k2, v, av, bv, logd = _wkv_prep(zr, zk, zv, zl, pr, pk, pv, pl_, pv_ref, mul_ref,
                                       wd_ref, wa_ref, e)

    ti = lax.broadcasted_iota(jnp.int32, (rows_all, rows_all), 0)
    tj = lax.broadcasted_iota(jnp.int32, (rows_all, rows_all), 1)
    tri = jnp.where((ti >= tj) & ((ti & -C) == (tj & -C)), 1.0, 0.0).astype(BF16)
    d_hi = _bf(logd)
    d_r1 = logd - d_hi.astype(F32)
    d_mid = _bf(d_r1)
    d_lo = _bf(d_r1 - d_mid.astype(F32))
    cum = _dg(tri, d_hi, NN) + (_dg(tri, d_mid, NN) + _dg(tri, d_lo, NN))
    e_in = jnp.exp(cum)
    e_neg = jnp.exp(-cum)
    a_t = av * jnp.exp(cum - logd)
    r_t = r * e_in
    k_t = k2 * e_neg
    b_t = bv * e_neg
    p_c = [jnp.exp(cum[b * C + C - 1:b * C + C, :]) for b in seqs]

    lane = lax.broadcasted_iota(jnp.int32, (C, LANES), 1)
    trow = lax.broadcasted_iota(jnp.int32, (C, LANES), 0)
    lo = lane < HEAD
    s_in = lane & (HEAD - 1)
    strict = s_in < trow
    incl2 = ((lax.broadcasted_iota(jnp.int32, (C, 2 * LANES), 1) & (HEAD - 1))
             <= lax.broadcasted_iota(jnp.int32, (C, 2 * LANES), 0))
    eye2 = jnp.where(s_in == trow, 1.0, 0.0).astype(F32)
    vrow = lax.broadcasted_iota(jnp.int32, (2 * HEAD, LANES), 0)
    klane = lax.broadcasted_iota(jnp.int32, (2 * HEAD, LANES), 1)
    same_head = (vrow < HEAD) == (klane < HEAD)

    def bd(x):
        z = jnp.zeros_like(x)
        return jnp.concatenate([jnp.where(lo, x, z), jnp.where(lo, z, x)], axis=0)

    npair = HEADS // 2
    units = [(b, p) for b in seqs for p in range(npair)]
    un = range(len(units))
    blk = lambda arr, i: arr[units[i][0] * C:(units[i][0] + 1) * C, LANES * units[i][1]:LANES * (units[i][1] + 1)]
    ar = [_bf(jnp.concatenate([blk(a_t, i), blk(r_t, i)], axis=0)) for i in un]
    bk = [_bf(jnp.concatenate([bd(blk(b_t, i)), bd(blk(k_t, i))], axis=0)) for i in un]
    g = [_dg(ar[i], bk[i], NT) for i in un]
    s0 = [s_s[i] for i in un]
    ars = [_dg(ar[i], _bf(s0[i]), NT) for i in un]
    vbd = [_bf(bd(blk(v, i))) for i in un]
    x = [jnp.where(strict, g[i][0:C, 0:LANES], 0.0) for i in un]
    ak = [jnp.where(strict, g[i][0:C, LANES:2 * LANES], 0.0) for i in un]
    w = [ars[i][0:C, :] + _dg(_bf(ak[i]), vbd[i], NN) for i in un]
    t = [eye2 + x[i] for i in un]
    x = [_dg(_bf(x[i]), _bf(bd(x[i])), NN) for i in un]
    for _ in range(C.bit_length() - 3):
        xt = [_dg(_bf(jnp.concatenate([x[i], t[i]], axis=0)), _bf(bd(x[i])), NN) for i in un]
        x = [xt[i][0:C, :] for i in un]
        t = [t[i] + xt[i][C:2 * C, :] for i in un]
    t = [t[i] + _dg(_bf(t[i]), _bf(bd(x[i])), NN) for i in un]
    u = [_dg(_bf(t[i]), _bf(bd(w[i])), NN) for i in un]
    rbk = [_bf(jnp.where(incl2, g[i][C:2 * C, :], 0.0)) for i in un]
    uvbd = [jnp.concatenate([_bf(bd(u[i])), vbd[i]], axis=0) for i in un]
    y = [ars[i][C:2 * C, :] + _dg(rbk[i], uvbd[i], NN) for i in un]
    uv = [_bf(jnp.concatenate([u[i], blk(v, i)], axis=0)) for i in un]
    pc = [p_c[units[i][0]][:, LANES * units[i][1]:LANES * (units[i][1] + 1)] for i in un]
    bkh = [_bf(jnp.concatenate([blk(b_t, i), blk(k_t, i)], axis=0) * pc[i]) for i in un]
    s1 = [s0[i] * pc[i] + jnp.where(same_head, _dg(uv[i], bkh[i], TN), 0.0) for i in un]
    for i in un:
        s_s[i] = s1[i]

    y_all = jnp.concatenate(
        [jnp.concatenate(y[b * npair:(b + 1) * npair], axis=1) for b in seqs], axis=0)
    o = _wkv_post(y_all, r, k2, v, flat(zrg_ref), pv_ref, e)
    o_ref[...] = o.reshape(nb, C, RWKV_W)

    @pl.when(c == nc - 1)
    def _():
        for i in un:
            b, p = units[i]
            sout_ref[b, 2 * p] = s1[i][0:HEAD, 0:HEAD]
            sout_ref[b, 2 * p + 1] = s1[i][HEAD:2 * HEAD, HEAD:2 * HEAD]


def _wkv_chunk(z, pvec, mul, wd, wa, e, batch, seq, nb):
    C = WKV_CHUNK
    nc = seq // C
    full = lambda shp: pl.BlockSpec(shp, lambda b, c: (0,) * len(shp))
    col = lambda j: pl.BlockSpec((nb, C, RWKV_W), lambda b, c, j=j: (b, c, j))
    return pl.pallas_call(
        _wkv_chunk_kernel,
        out_shape=(jax.ShapeDtypeStruct((batch, seq, RWKV_W), BF16),
                   jax.ShapeDtypeStruct((batch, HEADS, HEAD, HEAD), F32)),
        grid=(batch // nb, nc),
        in_specs=[col(0), col(1), col(2), col(3),
                  pl.BlockSpec((nb, C, LANES), lambda b, c: (b, c, LORA_BLOCK)),
                  full(pvec.shape), full(mul.shape), full(wd.shape), full(wa.shape), full(e.shape)],
        out_specs=(pl.BlockSpec((nb, C, RWKV_W), lambda b, c: (b, c, 0)),
                   pl.BlockSpec((nb, HEADS, HEAD, HEAD), lambda b, c: (b, 0, 0, 0))),
        scratch_shapes=[pltpu.VMEM((nb * HEADS // 2, 2 * HEAD, 2 * HEAD), F32),
                        pltpu.VMEM((nb, SUBLANES, 3 * RWKV_W), F32),
                        pltpu.VMEM((nb, SUBLANES, LANES), F32)],
        compiler_params=pltpu.CompilerParams(
            dimension_semantics=("arbitrary", "arbitrary"), vmem_limit_bytes=VMEM_LIMIT),
        name="wkv_chunk",
    )(z, z, z, z, z, pvec, mul, wd, wa, e)


def _wkv_step_kernel(zr_ref, zk_ref, zv_ref, zrg_ref, zl_ref, sh_ref, shl_ref, s0_ref, pv_ref,
                     mul_ref, wd_ref, wa_ref, e_ref, o_ref, sout_ref,
                     at_s, drt_s, bt_s, kt_s, dt_s, vt_s, brt_s, krt_s, yt_s, keep_s):
    h = pl.program_id(0)
    nh = pl.num_programs(0)
    nseq = zr_ref.shape[0]

    @pl.when(h == 0)
    def _():
        e = e_ref[...]
        r, k2, v, av, bv, logd = _wkv_prep(
            zr_ref[...], zk_ref[...], zv_ref[...], zl_ref[...],
            sh_ref[:, 0:RWKV_W], sh_ref[:, RWKV_W:2 * RWKV_W], sh_ref[:, 2 * RWKV_W:3 * RWKV_W],
            shl_ref[...], pv_ref, mul_ref, wd_ref, wa_ref, e)
        d = jnp.exp(logd)
        at_s[...] = av.T
        drt_s[...] = (d * r).T
        bt_s[...] = bv.T
        kt_s[...] = k2.T
        dt_s[...] = d.T
        vt_s[...] = v.T
        brt_s[...] = jnp.sum((bv * r).T.reshape(HEADS, HEAD, nseq), axis=1)
        krt_s[...] = jnp.sum((k2 * r).T.reshape(HEADS, HEAD, nseq), axis=1)
        keep_s[0] = r
        keep_s[1] = k2
        keep_s[2] = v

    base = pl.multiple_of(h * HEAD, HEAD)
    rows = pl.ds(base, HEAD)
    a_h, dr_h, b_h, k_h, d_h = at_s[rows, :], drt_s[rows, :], bt_s[rows, :], kt_s[rows, :], dt_s[rows, :]
    br_h = brt_s[pl.ds(h, 1), :]
    kr_h = krt_s[pl.ds(h, 1), :]

    def value_rows(g, carry):
        off = pl.multiple_of(base + g * SUBLANES, SUBLANES)
        v8 = vt_s[pl.ds(off, SUBLANES), :]
        ys = []
        for j in range(SUBLANES):
            vi = g * SUBLANES + j
            s_v = s0_ref[0, vi]
            sa = jnp.sum(s_v * a_h, axis=0, keepdims=True)
            y0 = jnp.sum(s_v * dr_h, axis=0, keepdims=True)
            v_v = v8[j:j + 1, :]
            sout_ref[0, vi] = s_v * d_h + sa * b_h + v_v * k_h
            ys.append(y0 + sa * br_h + v_v * kr_h)
        yt_s[pl.ds(off, SUBLANES), :] = jnp.concatenate(ys, axis=0)
        return carry

    lax.fori_loop(0, HEAD // SUBLANES, value_rows, 0)

    @pl.when(h == nh - 1)
    def _():
        o_ref[...] = _wkv_post(yt_s[...].T, keep_s[0], keep_s[1], keep_s[2], zrg_ref[...], pv_ref,
                               e_ref[...])


def _wkv_step(z, sh_main, sh_lora, s0t, pvec, mul, wd, wa, e):
    nseq = z.shape[0]
    full = lambda shp: pl.BlockSpec(shp, lambda i: (0,) * len(shp))
    col = lambda j: pl.BlockSpec((nseq, RWKV_W), lambda i, j=j: (0, j))
    st_block = (1, HEAD, HEAD, nseq)
    wide = pltpu.VMEM((RWKV_W, nseq), F32)
    return pl.pallas_call(
        _wkv_step_kernel,
        out_shape=(jax.ShapeDtypeStruct((nseq, RWKV_W), BF16),
                   jax.ShapeDtypeStruct(s0t.shape, F32)),
        grid=(HEADS,),
        in_specs=[col(0), col(1), col(2), col(3),
                  pl.BlockSpec((nseq, LANES), lambda i: (0, LORA_BLOCK)),
                  full(sh_main.shape), full(sh_lora.shape),
                  pl.BlockSpec(st_block, lambda i: (i, 0, 0, 0)),
                  full(pvec.shape), full(mul.shape), full(wd.shape), full(wa.shape), full(e.shape)],
        out_specs=(full((nseq, RWKV_W)),
                   pl.BlockSpec(st_block, lambda i: (i, 0, 0, 0))),
        scratch_shapes=[wide] * 6 + [pltpu.VMEM((HEADS, nseq), F32)] * 2
                       + [wide, pltpu.VMEM((3, nseq, RWKV_W), F32)],
        compiler_params=pltpu.CompilerParams(
            dimension_semantics=("arbitrary",), vmem_limit_bytes=VMEM_LIMIT),
        name="wkv_step",
    )(z, z, z, z, z, sh_main, sh_lora, s0t, pvec, mul, wd, wa, e)


_CW0, _CW1, _CW2, _CW3, _CB, _GXB, _GAB, _LAM = range(8)


def _lru_gates(xc, lp_ref, wg_ref):
    g = _dg(_bf(xc), wg_ref[...], NN)
    gx = _sigmoid(g[:, 0:LRU_W] + _prow(lp_ref, _GXB))
    ga = _sigmoid(g[:, LRU_W:2 * LRU_W] + _prow(lp_ref, _GAB))
    log_a = -LRU_C * ga * _softplus(-_prow(lp_ref, _LAM))
    a = jnp.exp(log_a)
    mult = jnp.sqrt((1.0 - a) * (1.0 + a))
    return a, mult * gx * xc


def _lru_tile(zx, zg, first, xb_s, hc_s, lp_ref, wgate_ref):
    tl = zx.shape[0]
    xb_s[0:SUBLANES, :] = jnp.where(first, 0.0, xb_s[0:SUBLANES, :])
    xb_s[SUBLANES:SUBLANES + tl, :] = zx
    xc = _prow(lp_ref, _CW3) * zx + _prow(lp_ref, _CB)
    for j in range(1, CONV_W):
        xc = xc + _prow(lp_ref, CONV_W - 1 - j) * xb_s[SUBLANES - j:SUBLANES - j + tl, :]
    xb_s[0:SUBLANES, :] = zx[tl - SUBLANES:tl, :]

    a, b = _lru_gates(xc, lp_ref, wgate_ref)
    row8 = lax.broadcasted_iota(jnp.int32, (SUBLANES, 1), 0)
    hc = jnp.where(first, 0.0, hc_s[...])
    hs = []
    for i in range(tl // SUBLANES):
        a8 = a[SUBLANES * i:SUBLANES * (i + 1), :]
        b8 = b[SUBLANES * i:SUBLANES * (i + 1), :]
        for s in (1, 2, 4):
            keep = row8 >= s
            b8 = jnp.where(keep, a8 * pltpu.roll(b8, s, 0) + b8, b8)
            a8 = jnp.where(keep, a8 * pltpu.roll(a8, s, 0), a8)
        hb = b8 + a8 * hc
        hs.append(hb)
        hc = jnp.broadcast_to(hb[SUBLANES - 1:SUBLANES, :], hb.shape)
    hc_s[...] = hc
    return _bf(jnp.concatenate(hs, axis=0) * (zg * _sigmoid(zg))), hc


def _lru_step_kernel(zx_ref, zg_ref, conv_ref, h0_ref, lp_ref, wg_ref, o_ref, hnew_ref):
    zx = zx_ref[...]
    xc = _prow(lp_ref, _CW3) * zx + _prow(lp_ref, _CB)
    for j in range(CONV_W - 1):
        xc = xc + _prow(lp_ref, j) * conv_ref[:, LRU_W * j:LRU_W * (j + 1)]
    a, b = _lru_gates(xc, lp_ref, wg_ref)
    h = a * h0_ref[...] + b
    hnew_ref[...] = h
    zg = zg_ref[...]
    o_ref[...] = _bf(h * (zg * _sigmoid(zg)))


def _lru_step(z_main, conv, h0, lp, wg):
    nb = z_main.shape[0]
    full = lambda shp: pl.BlockSpec(shp, lambda i: (0,) * len(shp))
    col = lambda j: pl.BlockSpec((nb, LRU_W), lambda i, j=j: (0, j))
    return pl.pallas_call(
        _lru_step_kernel,
        out_shape=(jax.ShapeDtypeStruct((nb, LRU_W), BF16), jax.ShapeDtypeStruct((nb, LRU_W), F32)),
        grid=(1,),
        in_specs=[col(4), col(5), full(conv.shape), full(h0.shape), full(lp.shape), full(wg.shape)],
        out_specs=(full((nb, LRU_W)), full((nb, LRU_W))),
        compiler_params=pltpu.CompilerParams(
            dimension_semantics=("arbitrary",), vmem_limit_bytes=VMEM_LIMIT),
        name="lru_step",
    )(z_main, z_main, conv, h0, lp, wg)


def _outproj_kernel(x_ref, or_ref, og_ref, mr_ref, mg_ref, wr_ref, wg_ref, wo_ref, fg_ref,
                    out_ref, *, final):
    y_r = _dg(or_ref[...], wr_ref[...], NN)
    y_g = _dg(og_ref[...], wg_ref[...], NN)
    merged = _sigmoid(mr_ref[...]) * y_r + _sigmoid(mg_ref[...]) * y_g
    out = x_ref[...] + _dg(_bf(merged), wo_ref[...], NN)
    out_ref[...] = _rms(out, fg_ref[...]) if final else out


def _outproj(x, o_r, o_g, z_main, w_r, w_g, w_o, fg, tm, final):
    m, d = x.shape
    const = lambda shp: pl.BlockSpec(shp, lambda i: (0,) * len(shp), pipeline_mode=pl.Buffered(1))
    return pl.pallas_call(
        functools.partial(_outproj_kernel, final=final),
        out_shape=jax.ShapeDtypeStruct((m, d), F32),
        grid=(m // tm,),
        in_specs=[
            pl.BlockSpec((tm, d), lambda i: (i, 0)),
            pl.BlockSpec((tm, RWKV_W), lambda i: (i, 0)),
            pl.BlockSpec((tm, LRU_W), lambda i: (i, 0)),
            pl.BlockSpec((tm, d), lambda i: (i, 3)),
            pl.BlockSpec((tm, d), lambda i: (i, 4)),
            const(w_r.shape), const(w_g.shape), const(w_o.shape), const(fg.shape),
        ],
        out_specs=pl.BlockSpec((tm, d), lambda i: (i, 0)),
        compiler_params=pltpu.CompilerParams(
            dimension_semantics=("arbitrary",), vmem_limit_bytes=VMEM_LIMIT),
        name="outproj",
    )(x, o_r, o_g, z_main, z_main, w_r, w_g, w_o, fg)


def _outproj_lru_kernel(x_ref, or_ref, mr_ref, mg_ref, zx_ref, zg_ref, lp_ref, wgate_ref, wr_ref,
                        wg_ref, wo_ref, fg_ref, out_ref, hlast_ref, og_s, xb_s, hc_s, *, final,
                        tiles_per_seq):
    i = pl.program_id(0)
    n = pl.num_programs(0) - 1

    @pl.when(i == 0)
    def _():
        og_s[...] = jnp.zeros_like(og_s)
        xb_s[...] = jnp.zeros_like(xb_s)
        hc_s[...] = jnp.zeros_like(hc_s)

    og_prev = og_s[...]
    t = lax.rem(jnp.minimum(i, n - 1), tiles_per_seq)
    o_g, hc = _lru_tile(zx_ref[...], zg_ref[...], t == 0, xb_s, hc_s, lp_ref, wgate_ref)
    og_s[...] = o_g

    y_r = _dg(or_ref[...], wr_ref[...], NN)
    y_g = _dg(og_prev, wg_ref[...], NN)
    merged = _sigmoid(mr_ref[...]) * y_r + _sigmoid(mg_ref[...]) * y_g
    out = x_ref[...] + _dg(_bf(merged), wo_ref[...], NN)
    out_ref[...] = _rms(out, fg_ref[...]) if final else out

    @pl.when((t == tiles_per_seq - 1) & (i < n))
    def _():
        hlast_ref[0] = hc[0:1, :]


def _outproj_lru(x, o_r, z, lp, wgate, w_r, w_g, w_o, fg, tm, seq, final):
    m, d = x.shape
    n = m // tm
    tiles_per_seq = seq // tm
    const = lambda shp: pl.BlockSpec(shp, lambda i: (0,) * len(shp), pipeline_mode=pl.Buffered(1))
    prev = lambda i: jnp.maximum(i - 1, 0)
    here = lambda i: jnp.minimum(i, n - 1)
    return pl.pallas_call(
        functools.partial(_outproj_lru_kernel, final=final, tiles_per_seq=tiles_per_seq),
        out_shape=(jax.ShapeDtypeStruct((m, d), F32),
                   jax.ShapeDtypeStruct((m // seq, 1, LRU_W), F32)),
        grid=(n + 1,),
        in_specs=[
            pl.BlockSpec((tm, d), lambda i: (prev(i), 0)),
            pl.BlockSpec((tm, RWKV_W), lambda i: (prev(i), 0)),
            pl.BlockSpec((tm, d), lambda i: (prev(i), 3)),
            pl.BlockSpec((tm, d), lambda i: (prev(i), 4)),
            pl.BlockSpec((tm, LRU_W), lambda i: (here(i), 4)),
            pl.BlockSpec((tm, LRU_W), lambda i: (here(i), 5)),
            const(lp.shape), const(wgate.shape), const(w_r.shape), const(w_g.shape), const(w_o.shape),
            const(fg.shape),
        ],
        out_specs=(pl.BlockSpec((tm, d), lambda i: (prev(i), 0)),
                   pl.BlockSpec((1, 1, LRU_W), lambda i: (here(i) // tiles_per_seq, 0, 0))),
        scratch_shapes=[pltpu.VMEM((tm, LRU_W), BF16),
                        pltpu.VMEM((SUBLANES + tm, LRU_W), F32),
                        pltpu.VMEM((SUBLANES, LRU_W), F32)],
        compiler_params=pltpu.CompilerParams(
            dimension_semantics=("arbitrary",), vmem_limit_bytes=VMEM_LIMIT),
        name="outproj_lru",
    )(x, o_r, z, z, z, z, lp, wgate, w_r, w_g, w_o, fg)


def _row_tile(m, want):
    t = min(m, want)
    assert m % t == 0, (m, t)
    return t


def _layer_params(l, w_in, rwkv_mu, w_decay0, w_decay_up, w_iclr0, w_iclr_up, k_k, k_a, r_k,
                  ln_x_g, ln_x_b, w_out_rwkv, conv_w, conv_b, lru_gx_w, lru_gx_b, lru_ga_w,
                  lru_ga_b, lru_lambda, w_out_lru, w_out):
    sh_w = 3 * RWKV_W + 2 * LORA
    p = {}
    mu = rwkv_mu[l]
    rows = [mu[0:RWKV_W], mu[RWKV_W:2 * RWKV_W], mu[2 * RWKV_W:3 * RWKV_W], w_decay0[l], w_iclr0[l],
            k_k[l], k_a[l], r_k[l].reshape(-1), ln_x_g[l], ln_x_b[l]]
    p["pvec"] = jnp.concatenate(
        [jnp.stack(rows), jnp.zeros((16 - len(rows), RWKV_W), F32)], axis=0).astype(F32)
    p["mul"] = jnp.broadcast_to(mu[3 * RWKV_W:sh_w][None, :], (SUBLANES, 2 * LORA)).astype(F32)
    zeros = jnp.zeros((LORA, RWKV_W), F32)
    p["wd"] = jnp.concatenate([w_decay_up[l], zeros], axis=0).astype(BF16)
    p["wa"] = jnp.concatenate([zeros, w_iclr_up[l]], axis=0).astype(BF16)
    lane_head = jnp.arange(LANES) // HEAD
    p["e"] = (lane_head[:, None] == lane_head[None, :]).astype(BF16)
    p["lp"] = jnp.concatenate(
        [conv_w[l], conv_b[l][None], lru_gx_b[l][None], lru_ga_b[l][None], lru_lambda[l][None]],
        axis=0).astype(F32)
    eye = jnp.eye(LRU_BLOCKS, dtype=F32)
    bd = lambda g: (eye[:, None, :, None] * g[:, :, None, :]).reshape(LRU_W, LRU_W)
    p["wg"] = jnp.concatenate([bd(lru_gx_w[l]), bd(lru_ga_w[l])], axis=1).astype(BF16)
    p["w_r"] = w_out_rwkv[l].astype(BF16)
    p["w_g"] = w_out_lru[l].astype(BF16)
    p["w_o"] = w_out[l].astype(BF16)
    return p


def kernel(x_prompt, x_sample, state_shift, state_wkv, state_conv, state_lru, norm_g, w_in, rwkv_mu,
           w_decay0, w_decay_up, w_iclr0, w_iclr_up, k_k, k_a, r_k, ln_x_g, ln_x_b, w_out_rwkv,
           conv_w, conv_b, lru_gx_w, lru_gx_b, lru_ga_w, lru_ga_b, lru_lambda, w_out_lru, w_out,
           final_norm_g):
    bp, seq, d = x_prompt.shape
    bs = x_sample.shape[0]
    assert x_sample.shape[1] == 1 and seq % WKV_CHUNK == 0
    depth = w_in.shape[0]
    sh_w = 3 * RWKV_W + 2 * LORA
    xp = x_prompt.reshape(bp * seq, d)
    xs = x_sample.reshape(bs, d)
    fg = final_norm_g.reshape(1, d)
    outs = [[] for _ in range(8)]
    for l in range(depth):
        p = _layer_params(l, w_in, rwkv_mu, w_decay0, w_decay_up, w_iclr0, w_iclr_up, k_k, k_a, r_k,
                          ln_x_g, ln_x_b, w_out_rwkv, conv_w, conv_b, lru_gx_w, lru_gx_b, lru_ga_w,
                          lru_ga_b, lru_lambda, w_out_lru, w_out)
        g = norm_g[l].reshape(1, d)
        rec = (p["pvec"], p["mul"], p["wd"], p["wa"], p["e"])

        w = _repack_w_in(w_in, l, _row_tile(d, 256))
        tn = w.shape[1] // 9

        zp = _inproj(xp, g, w, _row_tile(bp * seq, 1024), tn)
        zp3 = zp.reshape(bp, seq, -1)
        nb = max(n for n in (4, 2, 1) if bp % n == 0)
        o_r, s_new = _wkv_chunk(zp3, *rec, bp, seq, nb)
        o_r = o_r.reshape(bp * seq, RWKV_W)
        last = l == depth - 1
        xp, h_last = _outproj_lru(xp, o_r, zp, p["lp"], p["wg"], p["w_r"], p["w_g"], p["w_o"], fg,
                                  _row_tile(seq, 256), seq, last)
        outs[0].append(jnp.concatenate([zp3[:, -1, :SHIFT_MAIN], zp3[:, -1, LORA_COL:]], axis=-1))
        outs[1].append(s_new)
        outs[2].append(zp3[:, seq - (CONV_W - 1):, 4 * RWKV_W:4 * RWKV_W + LRU_W])
        outs[3].append(h_last.reshape(bp, LRU_W))

        zs = _inproj(xs, g, w, bs, tn)
        sh = state_shift[l]
        s0t = jnp.transpose(state_wkv[l], (1, 2, 3, 0))
        o_r, s_new = _wkv_step(zs, sh[:, :SHIFT_MAIN], sh[:, SHIFT_MAIN:sh_w], s0t, *rec)
        s_new = jnp.transpose(s_new, (3, 0, 1, 2))
        conv = state_conv[l]
        o_g, h_new = _lru_step(zs, conv.reshape(bs, (CONV_W - 1) * LRU_W), state_lru[l], p["lp"], p["wg"])
        xs = _outproj(xs, o_r, o_g, zs, p["w_r"], p["w_g"], p["w_o"], fg, bs, last)
        z_lx = zs[:, 4 * RWKV_W:4 * RWKV_W + LRU_W]
        outs[4].append(jnp.concatenate([zs[:, :SHIFT_MAIN], zs[:, LORA_COL:]], axis=-1))
        outs[5].append(s_new)
        outs[6].append(jnp.concatenate([conv[:, 1:], z_lx[:, None, :]], axis=1))
        outs[7].append(h_new)

    return (xp.reshape(bp, seq, d), xs.reshape(bs, 1, d)) + tuple(jnp.stack(o) for o in outs)
```

```python
import functools

import jax
import jax.numpy as jnp
from jax import lax
from jax.experimental import pallas as pl
from jax.experimental.pallas import tpu as pltpu

F32 = jnp.float32
BF16 = jnp.bfloat16

HEADS = 16
HEAD = 64
RWKV_W = HEADS * HEAD
LORA = 64
LRU_W = 1024
LRU_BLOCKS = 16
CONV_W = 4
LRU_C = 8.0
RMS_EPS = 1e-6
GN_EPS = 1e-5 * HEAD
DECAY_SCALE = 0.6065306597126334

LANES = 128
SUBLANES = 8
WKV_CHUNK = 64
VMEM_LIMIT = 56 * 1024 * 1024

NN = (((1,), (0,)), ((), ()))
NT = (((1,), (1,)), ((), ()))
TN = (((0,), (0,)), ((), ()))


def _bf(x):
    return x.astype(BF16)


def _dg(a, b, dn):
    return lax.dot_general(a, b, dn, preferred_element_type=F32)


def _softplus(x):
    return jnp.maximum(x, 0.0) + jnp.log1p(jnp.exp(-jnp.abs(x)))


def _sigmoid(x):
    return 1.0 / (1.0 + jnp.exp(-x))


def _segsum(x, e):
    rows, n = x.shape[0], x.shape[1] // LANES
    stacked = jnp.concatenate([x[:, LANES * j:LANES * (j + 1)] for j in range(n)], axis=0)
    s = _dg(_bf(stacked), e, NN)
    return jnp.concatenate([s[rows * j:rows * (j + 1), :] for j in range(n)], axis=1)


def _rms(x, g):
    return x * lax.rsqrt(jnp.mean(x * x, axis=-1, keepdims=True) + RMS_EPS) * g


SHIFT_MAIN = 3 * RWKV_W
LORA_COL = 10 * RWKV_W
LORA_BLOCK = LORA_COL // LANES
INPROJ_TN = 1024


def _repack_kernel(w_ref, o_ref):
    n = w_ref.shape[-1]
    lora_end = SHIFT_MAIN + 2 * LORA
    o_ref[:, 0:SHIFT_MAIN] = _bf(w_ref[:, 0:SHIFT_MAIN])
    o_ref[:, SHIFT_MAIN:LORA_COL] = _bf(w_ref[:, lora_end:n])
    o_ref[:, LORA_COL:n] = _bf(w_ref[:, SHIFT_MAIN:lora_end])


def _repack_w_in(w_in, layer, tr):
    _, d, n = w_in.shape
    assert n == LORA_COL + 2 * LORA
    return pl.pallas_call(
        _repack_kernel,
        out_shape=jax.ShapeDtypeStruct((d, n), BF16),
        grid=(d // tr,),
        in_specs=[pl.BlockSpec((None, tr, n), lambda i: (layer, i, 0))],
        out_specs=pl.BlockSpec((tr, n), lambda i: (i, 0)),
        compiler_params=pltpu.CompilerParams(
            dimension_semantics=("arbitrary",), vmem_limit_bytes=VMEM_LIMIT),
        name="repack_w_in",
    )(w_in)


def _inproj_kernel(x_ref, xs_ref, g_ref, w_ref, wl_ref, z_ref, zl_ref, zs_ref, zls_ref, h_ref, hs_ref):
    i = pl.program_id(0)
    j = pl.program_id(1)

    @pl.when((i == 0) & (j == 0))
    def _():
        hs = _bf(_rms(xs_ref[...], g_ref[...]))
        hs_ref[...] = hs
        zls_ref[...] = _dg(hs, wl_ref[...], NN)

    @pl.when(j == 0)
    def _():
        h = _bf(_rms(x_ref[...], g_ref[...]))
        h_ref[...] = h
        zl_ref[...] = _dg(h, wl_ref[...], NN)

    z_ref[...] = _dg(h_ref[...], w_ref[...], NN)

    @pl.when(i == 0)
    def _():
        zs_ref[...] = _dg(hs_ref[...], w_ref[...], NN)


def _inproj(x, xs, g, w, tm, tn):
    m, d = x.shape
    ms = xs.shape[0]
    nj = LORA_COL // tn
    park = lambda i, j: jnp.where(i == 0, j, nj - 1)
    return pl.pallas_call(
        _inproj_kernel,
        out_shape=(jax.ShapeDtypeStruct((m, LORA_COL), F32), jax.ShapeDtypeStruct((m, LANES), F32),
                   jax.ShapeDtypeStruct((ms, LORA_COL), F32), jax.ShapeDtypeStruct((ms, LANES), F32)),
        grid=(m // tm, nj),
        in_specs=[
            pl.BlockSpec((tm, d), lambda i, j: (i, 0)),
            pl.BlockSpec((ms, d), lambda i, j: (0, 0), pipeline_mode=pl.Buffered(1)),
            pl.BlockSpec((1, d), lambda i, j: (0, 0), pipeline_mode=pl.Buffered(1)),
            pl.BlockSpec((d, tn), lambda i, j: (0, j)),
            pl.BlockSpec((d, LANES), lambda i, j: (0, LORA_BLOCK), pipeline_mode=pl.Buffered(1)),
        ],
        out_specs=(
            pl.BlockSpec((tm, tn), lambda i, j: (i, j)),
            pl.BlockSpec((tm, LANES), lambda i, j: (i, 0)),
            pl.BlockSpec((ms, tn), lambda i, j: (0, park(i, j))),
            pl.BlockSpec((ms, LANES), lambda i, j: (0, 0)),
        ),
        scratch_shapes=[pltpu.VMEM((tm, d), BF16), pltpu.VMEM((ms, d), BF16)],
        compiler_params=pltpu.CompilerParams(
            dimension_semantics=("arbitrary", "arbitrary"), vmem_limit_bytes=VMEM_LIMIT),
        name="inproj",
    )(x, xs, g, w, w)


_MU_R, _MU_K, _MU_V, _W0, _A0, _KK, _KA, _RK, _LNG, _LNB = range(10)


def _prow(pv_ref, i):
    return pv_ref[i:i + 1, :]


def _wkv_prep(zr, zk, zv, zl, pr, pk, pv, pl_, pv_ref, mul_ref, wd_ref, wa_ref, e):
    r = zr + _prow(pv_ref, _MU_R) * (pr - zr)
    k = zk + _prow(pv_ref, _MU_K) * (pk - zk)
    v = zv + _prow(pv_ref, _MU_V) * (pv - zv)
    lo = zl + mul_ref[0:1, :] * (pl_ - zl)
    lw = _dg(_bf(jnp.tanh(lo)), wd_ref[...], NN)
    la = _dg(_bf(lo), wa_ref[...], NN)
    logd = -DECAY_SCALE * _sigmoid(_prow(pv_ref, _W0) + lw)
    a = _sigmoid(_prow(pv_ref, _A0) + la)
    kk = k * _prow(pv_ref, _KK)
    kk = kk * lax.rsqrt(jnp.maximum(_segsum(kk * kk, e), 1e-24))
    k2 = k * (1.0 + (a - 1.0) * _prow(pv_ref, _KA))
    return r, k2, v, -kk, kk * a, logd


def _wkv_post(y, r, k2, v, zrg, pv_ref, e):
    mu = _segsum(y, e) * (1.0 / HEAD)
    yc = y - mu
    var = _segsum(yc * yc, e) * (1.0 / HEAD)
    yn = yc * lax.rsqrt(var + GN_EPS) * _prow(pv_ref, _LNG) + _prow(pv_ref, _LNB)
    bonus = _segsum(r * k2 * _prow(pv_ref, _RK), e)
    return _bf((yn + bonus * v) * (zrg * _sigmoid(zrg)))


def _wkv_chunk_kernel(zr_ref, zk_ref, zv_ref, zrg_ref, zl_ref, pv_ref, mul_ref, wd_ref, wa_ref,
                      e_ref, o_ref, sout_ref, s_s, prev_s, prevl_s):
    c = pl.program_id(1)
    nc = pl.num_programs(1)
    C = WKV_CHUNK
    assert C == HEAD and 2 * HEAD == LANES
    nb = zr_ref.shape[0]
    rows_all = nb * C
    seqs = range(nb)

    @pl.when(c == 0)
    def _():
        s_s[...] = jnp.zeros_like(s_s)
        prev_s[...] = jnp.zeros_like(prev_s)
        prevl_s[...] = jnp.zeros_like(prevl_s)

    first = lax.broadcasted_iota(jnp.int32, (SUBLANES, 1), 0) == 0

    def shifted(z, prev_ref, lanes):
        rolled = pltpu.roll(z, 1, 0)
        pieces = []
        for b in seqs:
            head = jnp.where(first, prev_ref[b, 0:1, lanes], rolled[b * C:b * C + SUBLANES, :])
            pieces += [head, rolled[b * C + SUBLANES:(b + 1) * C, :]]
        return jnp.concatenate(pieces, axis=0)

    def flat(ref):
        return ref[...].reshape(rows_all, ref.shape[-1])

    zr, zk, zv, zl = flat(zr_ref), flat(zk_ref), flat(zv_ref), flat(zl_ref)
    seg = [slice(RWKV_W * i, RWKV_W * (i + 1)) for i in range(3)]
    pr = shifted(zr, prev_s, seg[0])
    pk = shifted(zk, prev_s, seg[1])
    pv = shifted(zv, prev_s, seg[2])
    pl_ = shifted(zl, prevl_s, slice(0, LANES))
    for b in seqs:
        last = slice(b * C + C - 1, b * C + C)
        prev_s[b, 0:1, seg[0]] = zr[last, :]
        prev_s[b, 0:1, seg[1]] = zk[last, :]
        prev_s[b, 0:1, seg[2]] = zv[last, :]
        prevl_s[b, 0:1, :] = zl[last, :]

    e = e_ref[...]
    r, k2, v, av, bv, logd = _wkv_prep(zr, zk, zv, zl, pr, pk, pv, pl_, pv_ref, mul_ref,
                                       wd_ref, wa_ref, e)

    ti = lax.broadcasted_iota(jnp.int32, (rows_all, rows_all), 0)
    tj = lax.broadcasted_iota(jnp.int32, (rows_all, rows_all), 1)
    tri = jnp.where((ti >= tj) & ((ti & -C) == (tj & -C)), 1.0, 0.0).astype(BF16)
    d_hi = _bf(logd)
    d_r1 = logd - d_hi.astype(F32)
    d_mid = _bf(d_r1)
    d_lo = _bf(d_r1 - d_mid.astype(F32))
    cum = _dg(tri, d_hi, NN) + (_dg(tri, d_mid, NN) + _dg(tri, d_lo, NN))
    e_in = jnp.exp(cum)
    e_neg = jnp.exp(-cum)
    a_t = av * jnp.exp(cum - logd)
    r_t = r * e_in
    k_t = k2 * e_neg
    b_t = bv * e_neg
    p_c = [jnp.exp(cum[b * C + C - 1:b * C + C, :]) for b in seqs]

    lane = lax.broadcasted_iota(jnp.int32, (C, LANES), 1)
    trow = lax.broadcasted_iota(jnp.int32, (C, LANES), 0)
    lo = lane < HEAD
    s_in = lane & (HEAD - 1)
    strict = s_in < trow
    incl2 = ((lax.broadcasted_iota(jnp.int32, (C, 2 * LANES), 1) & (HEAD - 1))
             <= lax.broadcasted_iota(jnp.int32, (C, 2 * LANES), 0))
    eye2 = jnp.where(s_in == trow, 1.0, 0.0).astype(F32)
    vrow = lax.broadcasted_iota(jnp.int32, (2 * HEAD, LANES), 0)
    klane = lax.broadcasted_iota(jnp.int32, (2 * HEAD, LANES), 1)
    same_head = (vrow < HEAD) == (klane < HEAD)

    def bd(x):
        z = jnp.zeros_like(x)
        return jnp.concatenate([jnp.where(lo, x, z), jnp.where(lo, z, x)], axis=0)

    npair = HEADS // 2
    units = [(b, p) for b in seqs for p in range(npair)]
    un = range(len(units))
    blk = lambda arr, i: arr[units[i][0] * C:(units[i][0] + 1) * C, LANES * units[i][1]:LANES * (units[i][1] + 1)]
    ar = [_bf(jnp.concatenate([blk(a_t, i), blk(r_t, i)], axis=0)) for i in un]
    bk = [_bf(jnp.concatenate([bd(blk(b_t, i)), bd(blk(k_t, i))], axis=0)) for i in un]
    g = [_dg(ar[i], bk[i], NT) for i in un]
    s0 = [s_s[i] for i in un]
    ars = [_dg(ar[i], _bf(s0[i]), NT) for i in un]
    vbd = [_bf(bd(blk(v, i))) for i in un]
    x = [jnp.where(strict, g[i][0:C, 0:LANES], 0.0) for i in un]
    ak = [jnp.where(strict, g[i][0:C, LANES:2 * LANES], 0.0) for i in un]
    w = [ars[i][0:C, :] + _dg(_bf(ak[i]), vbd[i], NN) for i in un]
    t = [eye2 + x[i] for i in un]
    x = [_dg(_bf(x[i]), _bf(bd(x[i])), NN) for i in un]
    for _ in range(C.bit_length() - 3):
        xt = [_dg(_bf(jnp.concatenate([x[i], t[i]], axis=0)), _bf(bd(x[i])), NN) for i in un]
        x = [xt[i][0:C, :] for i in un]
        t = [t[i] + xt[i][C:2 * C, :] for i in un]
    t = [t[i] + _dg(_bf(t[i]), _bf(bd(x[i])), NN) for i in un]
    u = [_dg(_bf(t[i]), _bf(bd(w[i])), NN) for i in un]
    rbk = [_bf(jnp.where(incl2, g[i][C:2 * C, :], 0.0)) for i in un]
    uvbd = [jnp.concatenate([_bf(bd(u[i])), vbd[i]], axis=0) for i in un]
    y = [ars[i][C:2 * C, :] + _dg(rbk[i], uvbd[i], NN) for i in un]
    uv = [_bf(jnp.concatenate([u[i], blk(v, i)], axis=0)) for i in un]
    pc = [p_c[units[i][0]][:, LANES * units[i][1]:LANES * (units[i][1] + 1)] for i in un]
    bkh = [_bf(jnp.concatenate([blk(b_t, i), blk(k_t, i)], axis=0) * pc[i]) for i in un]
    s1 = [s0[i] * pc[i] + jnp.where(same_head, _dg(uv[i], bkh[i], TN), 0.0) for i in un]
    for i in un:
        s_s[i] = s1[i]

    y_all = jnp.concatenate(
        [jnp.concatenate(y[b * npair:(b + 1) * npair], axis=1) for b in seqs], axis=0)
    o = _wkv_post(y_all, r, k2, v, flat(zrg_ref), pv_ref, e)
    o_ref[...] = o.reshape(nb, C, RWKV_W)

    @pl.when(c == nc - 1)
    def _():
        for i in un:
            b, p = units[i]
            sout_ref[b, 2 * p] = s1[i][0:HEAD, 0:HEAD]
            sout_ref[b, 2 * p + 1] = s1[i][HEAD:2 * HEAD, HEAD:2 * HEAD]


def _wkv_chunk(z, zl, pvec, mul, wd, wa, e, batch, seq, nb):
    C = WKV_CHUNK
    nc = seq // C
    full = lambda shp: pl.BlockSpec(shp, lambda b, c: (0,) * len(shp))
    col = lambda j: pl.BlockSpec((nb, C, RWKV_W), lambda b, c, j=j: (b, c, j))
    return pl.pallas_call(
        _wkv_chunk_kernel,
        out_shape=(jax.ShapeDtypeStruct((batch, seq, RWKV_W), BF16),
                   jax.ShapeDtypeStruct((batch, HEADS, HEAD, HEAD), F32)),
        grid=(batch // nb, nc),
        in_specs=[col(0), col(1), col(2), col(3),
                  pl.BlockSpec((nb, C, LANES), lambda b, c: (b, c, 0)),
                  full(pvec.shape), full(mul.shape), full(wd.shape), full(wa.shape), full(e.shape)],
        out_specs=(pl.BlockSpec((nb, C, RWKV_W), lambda b, c: (b, c, 0)),
                   pl.BlockSpec((nb, HEADS, HEAD, HEAD), lambda b, c: (b, 0, 0, 0))),
        scratch_shapes=[pltpu.VMEM((nb * HEADS // 2, 2 * HEAD, 2 * HEAD), F32),
                        pltpu.VMEM((nb, SUBLANES, 3 * RWKV_W), F32),
                        pltpu.VMEM((nb, SUBLANES, LANES), F32)],
        compiler_params=pltpu.CompilerParams(
            dimension_semantics=("arbitrary", "arbitrary"), vmem_limit_bytes=VMEM_LIMIT),
        name="wkv_chunk",
    )(z, z, z, z, zl, pvec, mul, wd, wa, e)


def _wkv_step_kernel(zr_ref, zk_ref, zv_ref, zrg_ref, zl_ref, sh_ref, shl_ref, s0_ref, pv_ref,
                     mul_ref, wd_ref, wa_ref, e_ref, o_ref, sout_ref,
                     at_s, drt_s, bt_s, kt_s, dt_s, vt_s, brt_s, krt_s, yt_s, keep_s):
    h = pl.program_id(0)
    nh = pl.num_programs(0)
    nseq = zr_ref.shape[0]

    @pl.when(h == 0)
    def _():
        e = e_ref[...]
        r, k2, v, av, bv, logd = _wkv_prep(
            zr_ref[...], zk_ref[...], zv_ref[...], zl_ref[...],
            sh_ref[:, 0:RWKV_W], sh_ref[:, RWKV_W:2 * RWKV_W], sh_ref[:, 2 * RWKV_W:3 * RWKV_W],
            shl_ref[...], pv_ref, mul_ref, wd_ref, wa_ref, e)
        d = jnp.exp(logd)
        at_s[...] = av.T
        drt_s[...] = (d * r).T
        bt_s[...] = bv.T
        kt_s[...] = k2.T
        dt_s[...] = d.T
        vt_s[...] = v.T
        brt_s[...] = jnp.sum((bv * r).T.reshape(HEADS, HEAD, nseq), axis=1)
        krt_s[...] = jnp.sum((k2 * r).T.reshape(HEADS, HEAD, nseq), axis=1)
        keep_s[0] = r
        keep_s[1] = k2
        keep_s[2] = v

    base = pl.multiple_of(h * HEAD, HEAD)
    rows = pl.ds(base, HEAD)
    a_h, dr_h, b_h, k_h, d_h = at_s[rows, :], drt_s[rows, :], bt_s[rows, :], kt_s[rows, :], dt_s[rows, :]
    br_h = brt_s[pl.ds(h, 1), :]
    kr_h = krt_s[pl.ds(h, 1), :]

    def value_rows(g, carry):
        off = pl.multiple_of(base + g * SUBLANES, SUBLANES)
        v8 = vt_s[pl.ds(off, SUBLANES), :]
        ys = []
        for j in range(SUBLANES):
            vi = g * SUBLANES + j
            s_v = s0_ref[0, vi]
            sa = jnp.sum(s_v * a_h, axis=0, keepdims=True)
            y0 = jnp.sum(s_v * dr_h, axis=0, keepdims=True)
            v_v = v8[j:j + 1, :]
            sout_ref[0, vi] = s_v * d_h + sa * b_h + v_v * k_h
            ys.append(y0 + sa * br_h + v_v * kr_h)
        yt_s[pl.ds(off, SUBLANES), :] = jnp.concatenate(ys, axis=0)
        return carry

    lax.fori_loop(0, HEAD // SUBLANES, value_rows, 0)

    @pl.when(h == nh - 1)
    def _():
        o_ref[...] = _wkv_post(yt_s[...].T, keep_s[0], keep_s[1], keep_s[2], zrg_ref[...], pv_ref,
                               e_ref[...])


def _wkv_step(z, zl, sh_main, sh_lora, s0t, pvec, mul, wd, wa, e):
    nseq = z.shape[0]
    full = lambda shp: pl.BlockSpec(shp, lambda i: (0,) * len(shp))
    col = lambda j: pl.BlockSpec((nseq, RWKV_W), lambda i, j=j: (0, j))
    st_block = (1, HEAD, HEAD, nseq)
    wide = pltpu.VMEM((RWKV_W, nseq), F32)
    return pl.pallas_call(
        _wkv_step_kernel,
        out_shape=(jax.ShapeDtypeStruct((nseq, RWKV_W), BF16),
                   jax.ShapeDtypeStruct(s0t.shape, F32)),
        grid=(HEADS,),
        in_specs=[col(0), col(1), col(2), col(3),
                  full(zl.shape),
                  full(sh_main.shape), full(sh_lora.shape),
                  pl.BlockSpec(st_block, lambda i: (i, 0, 0, 0)),
                  full(pvec.shape), full(mul.shape), full(wd.shape), full(wa.shape), full(e.shape)],
        out_specs=(full((nseq, RWKV_W)),
                   pl.BlockSpec(st_block, lambda i: (i, 0, 0, 0))),
        scratch_shapes=[wide] * 6 + [pltpu.VMEM((HEADS, nseq), F32)] * 2
                       + [wide, pltpu.VMEM((3, nseq, RWKV_W), F32)],
        compiler_params=pltpu.CompilerParams(
            dimension_semantics=("arbitrary",), vmem_limit_bytes=VMEM_LIMIT),
        name="wkv_step",
    )(z, z, z, z, zl, sh_main, sh_lora, s0t, pvec, mul, wd, wa, e)


_CW0, _CW1, _CW2, _CW3, _CB, _GXB, _GAB, _LAM = range(8)


def _lru_gates(xc, lp_ref, wg_ref):
    g = _dg(_bf(xc), wg_ref[...], NN)
    gx = _sigmoid(g[:, 0:LRU_W] + _prow(lp_ref, _GXB))
    ga = _sigmoid(g[:, LRU_W:2 * LRU_W] + _prow(lp_ref, _GAB))
    log_a = -LRU_C * ga * _softplus(-_prow(lp_ref, _LAM))
    a = jnp.exp(log_a)
    mult = jnp.sqrt((1.0 - a) * (1.0 + a))
    return a, mult * gx * xc


def _lru_coeffs(zx, first, xb_s, lp_ref, wgate_ref):
    tl = zx.shape[0]
    xb_s[0:SUBLANES, :] = jnp.where(first, 0.0, xb_s[0:SUBLANES, :])
    xb_s[SUBLANES:SUBLANES + tl, :] = zx
    xc = _prow(lp_ref, _CW3) * zx + _prow(lp_ref, _CB)
    for j in range(1, CONV_W):
        xc = xc + _prow(lp_ref, CONV_W - 1 - j) * xb_s[SUBLANES - j:SUBLANES - j + tl, :]
    xb_s[0:SUBLANES, :] = zx[tl - SUBLANES:tl, :]
    return _lru_gates(xc, lp_ref, wgate_ref)


def _lru_scan_gate(a, b, zg, first, hc_s):
    tl = a.shape[0]
    row8 = lax.broadcasted_iota(jnp.int32, (SUBLANES, 1), 0)
    hc = jnp.where(first, 0.0, hc_s[...])
    hs = []
    for i in range(tl // SUBLANES):
        a8 = a[SUBLANES * i:SUBLANES * (i + 1), :]
        b8 = b[SUBLANES * i:SUBLANES * (i + 1), :]
        for s in (1, 2, 4):
            keep = row8 >= s
            b8 = jnp.where(keep, a8 * pltpu.roll(b8, s, 0) + b8, b8)
            a8 = jnp.where(keep, a8 * pltpu.roll(a8, s, 0), a8)
        hb = b8 + a8 * hc
        hs.append(hb)
        hc = jnp.broadcast_to(hb[SUBLANES - 1:SUBLANES, :], hb.shape)
    hc_s[...] = hc
    return _bf(jnp.concatenate(hs, axis=0) * (zg * _sigmoid(zg))), hc


def _lru_step_kernel(zx_ref, zg_ref, conv_ref, h0_ref, lp_ref, wg_ref, o_ref, hnew_ref):
    zx = zx_ref[...]
    xc = _prow(lp_ref, _CW3) * zx + _prow(lp_ref, _CB)
    for j in range(CONV_W - 1):
        xc = xc + _prow(lp_ref, j) * conv_ref[:, LRU_W * j:LRU_W * (j + 1)]
    a, b = _lru_gates(xc, lp_ref, wg_ref)
    h = a * h0_ref[...] + b
    hnew_ref[...] = h
    zg = zg_ref[...]
    o_ref[...] = _bf(h * (zg * _sigmoid(zg)))


def _lru_step(z_main, conv, h0, lp, wg):
    nb = z_main.shape[0]
    full = lambda shp: pl.BlockSpec(shp, lambda i: (0,) * len(shp))
    col = lambda j: pl.BlockSpec((nb, LRU_W), lambda i, j=j: (0, j))
    return pl.pallas_call(
        _lru_step_kernel,
        out_shape=(jax.ShapeDtypeStruct((nb, LRU_W), BF16), jax.ShapeDtypeStruct((nb, LRU_W), F32)),
        grid=(1,),
        in_specs=[col(4), col(5), full(conv.shape), full(h0.shape), full(lp.shape), full(wg.shape)],
        out_specs=(full((nb, LRU_W)), full((nb, LRU_W))),
        compiler_params=pltpu.CompilerParams(
            dimension_semantics=("arbitrary",), vmem_limit_bytes=VMEM_LIMIT),
        name="lru_step",
    )(z_main, z_main, conv, h0, lp, wg)


def _outproj_kernel(x_ref, or_ref, og_ref, mr_ref, mg_ref, wr_ref, wg_ref, wo_ref, fg_ref,
                    out_ref, *, final):
    y_r = _dg(or_ref[...], wr_ref[...], NN)
    y_g = _dg(og_ref[...], wg_ref[...], NN)
    merged = _sigmoid(mr_ref[...]) * y_r + _sigmoid(mg_ref[...]) * y_g
    out = x_ref[...] + _dg(_bf(merged), wo_ref[...], NN)
    out_ref[...] = _rms(out, fg_ref[...]) if final else out


def _outproj(x, o_r, o_g, z_main, w_r, w_g, w_o, fg, tm, final):
    m, d = x.shape
    const = lambda shp: pl.BlockSpec(shp, lambda i: (0,) * len(shp), pipeline_mode=pl.Buffered(1))
    return pl.pallas_call(
        functools.partial(_outproj_kernel, final=final),
        out_shape=jax.ShapeDtypeStruct((m, d), F32),
        grid=(m // tm,),
        in_specs=[
            pl.BlockSpec((tm, d), lambda i: (i, 0)),
            pl.BlockSpec((tm, RWKV_W), lambda i: (i, 0)),
            pl.BlockSpec((tm, LRU_W), lambda i: (i, 0)),
            pl.BlockSpec((tm, d), lambda i: (i, 3)),
            pl.BlockSpec((tm, d), lambda i: (i, 4)),
            const(w_r.shape), const(w_g.shape), const(w_o.shape), const(fg.shape),
        ],
        out_specs=pl.BlockSpec((tm, d), lambda i: (i, 0)),
        compiler_params=pltpu.CompilerParams(
            dimension_semantics=("arbitrary",), vmem_limit_bytes=VMEM_LIMIT),
        name="outproj",
    )(x, o_r, o_g, z_main, z_main, w_r, w_g, w_o, fg)


def _outproj_lru_kernel(x_ref, or_ref, mr_ref, mg_ref, zx_ref, zg_ref, lp_ref, wgate_ref, wr_ref,
                        wg_ref, wo_ref, fg_ref, out_ref, hlast_ref, og_s, xb_s, hc_s, *, final,
                        tiles_per_seq):
    i = pl.program_id(0)
    n = pl.num_programs(0) - 1

    @pl.when(i == 0)
    def _():
        og_s[...] = jnp.zeros_like(og_s)
        xb_s[...] = jnp.zeros_like(xb_s)
        hc_s[...] = jnp.zeros_like(hc_s)

    og_prev = og_s[...]
    t = lax.rem(jnp.minimum(i, n - 1), tiles_per_seq)
    a, b = _lru_coeffs(zx_ref[...], t == 0, xb_s, lp_ref, wgate_ref)
    y_r = _dg(or_ref[...], wr_ref[...], NN)
    y_g = _dg(og_prev, wg_ref[...], NN)
    merged = _sigmoid(mr_ref[...]) * y_r + _sigmoid(mg_ref[...]) * y_g
    o_g, hc = _lru_scan_gate(a, b, zg_ref[...], t == 0, hc_s)
    og_s[...] = o_g
    out = x_ref[...] + _dg(_bf(merged), wo_ref[...], NN)
    out_ref[...] = _rms(out, fg_ref[...]) if final else out

    @pl.when((t == tiles_per_seq - 1) & (i < n))
    def _():
        hlast_ref[0] = hc[0:1, :]


def _outproj_lru(x, o_r, z, lp, wgate, w_r, w_g, w_o, fg, tm, seq, final):
    m, d = x.shape
    n = m // tm
    tiles_per_seq = seq // tm
    const = lambda shp: pl.BlockSpec(shp, lambda i: (0,) * len(shp), pipeline_mode=pl.Buffered(1))
    prev = lambda i: jnp.maximum(i - 1, 0)
    here = lambda i: jnp.minimum(i, n - 1)
    return pl.pallas_call(
        functools.partial(_outproj_lru_kernel, final=final, tiles_per_seq=tiles_per_seq),
        out_shape=(jax.ShapeDtypeStruct((m, d), F32),
                   jax.ShapeDtypeStruct((m // seq, 1, LRU_W), F32)),
        grid=(n + 1,),
        in_specs=[
            pl.BlockSpec((tm, d), lambda i: (prev(i), 0)),
            pl.BlockSpec((tm, RWKV_W), lambda i: (prev(i), 0)),
            pl.BlockSpec((tm, d), lambda i: (prev(i), 3)),
            pl.BlockSpec((tm, d), lambda i: (prev(i), 4)),
            pl.BlockSpec((tm, LRU_W), lambda i: (here(i), 4)),
            pl.BlockSpec((tm, LRU_W), lambda i: (here(i), 5)),
            const(lp.shape), const(wgate.shape), const(w_r.shape), const(w_g.shape), const(w_o.shape),
            const(fg.shape),
        ],
        out_specs=(pl.BlockSpec((tm, d), lambda i: (prev(i), 0)),
                   pl.BlockSpec((1, 1, LRU_W), lambda i: (here(i) // tiles_per_seq, 0, 0))),
        scratch_shapes=[pltpu.VMEM((tm, LRU_W), BF16),
                        pltpu.VMEM((SUBLANES + tm, LRU_W), F32),
                        pltpu.VMEM((SUBLANES, LRU_W), F32)],
        compiler_params=pltpu.CompilerParams(
            dimension_semantics=("arbitrary",), vmem_limit_bytes=VMEM_LIMIT),
        name="outproj_lru",
    )(x, o_r, z, z, z, z, lp, wgate, w_r, w_g, w_o, fg)


def _row_tile(m, want):
    t = min(m, want)
    assert m % t == 0, (m, t)
    return t


def _layer_params(l, w_in, rwkv_mu, w_decay0, w_decay_up, w_iclr0, w_iclr_up, k_k, k_a, r_k,
                  ln_x_g, ln_x_b, w_out_rwkv, conv_w, conv_b, lru_gx_w, lru_gx_b, lru_ga_w,
                  lru_ga_b, lru_lambda, w_out_lru, w_out):
    sh_w = 3 * RWKV_W + 2 * LORA
    p = {}
    mu = rwkv_mu[l]
    rows = [mu[0:RWKV_W], mu[RWKV_W:2 * RWKV_W], mu[2 * RWKV_W:3 * RWKV_W], w_decay0[l], w_iclr0[l],
            k_k[l], k_a[l], r_k[l].reshape(-1), ln_x_g[l], ln_x_b[l]]
    p["pvec"] = jnp.concatenate(
        [jnp.stack(rows), jnp.zeros((16 - len(rows), RWKV_W), F32)], axis=0).astype(F32)
    p["mul"] = jnp.broadcast_to(mu[3 * RWKV_W:sh_w][None, :], (SUBLANES, 2 * LORA)).astype(F32)
    zeros = jnp.zeros((LORA, RWKV_W), F32)
    p["wd"] = jnp.concatenate([w_decay_up[l], zeros], axis=0).astype(BF16)
    p["wa"] = jnp.concatenate([zeros, w_iclr_up[l]], axis=0).astype(BF16)
    lane_head = jnp.arange(LANES) // HEAD
    p["e"] = (lane_head[:, None] == lane_head[None, :]).astype(BF16)
    p["lp"] = jnp.concatenate(
        [conv_w[l], conv_b[l][None], lru_gx_b[l][None], lru_ga_b[l][None], lru_lambda[l][None]],
        axis=0).astype(F32)
    eye = jnp.eye(LRU_BLOCKS, dtype=F32)
    bd = lambda g: (eye[:, None, :, None] * g[:, :, None, :]).reshape(LRU_W, LRU_W)
    p["wg"] = jnp.concatenate([bd(lru_gx_w[l]), bd(lru_ga_w[l])], axis=1).astype(BF16)
    p["w_r"] = w_out_rwkv[l].astype(BF16)
    p["w_g"] = w_out_lru[l].astype(BF16)
    p["w_o"] = w_out[l].astype(BF16)
    return p


def kernel(x_prompt, x_sample, state_shift, state_wkv, state_conv, state_lru, norm_g, w_in, rwkv_mu,
           w_decay0, w_decay_up, w_iclr0, w_iclr_up, k_k, k_a, r_k, ln_x_g, ln_x_b, w_out_rwkv,
           conv_w, conv_b, lru_gx_w, lru_gx_b, lru_ga_w, lru_ga_b, lru_lambda, w_out_lru, w_out,
           final_norm_g):
    bp, seq, d = x_prompt.shape
    bs = x_sample.shape[0]
    assert x_sample.shape[1] == 1 and seq % WKV_CHUNK == 0
    depth = w_in.shape[0]
    sh_w = 3 * RWKV_W + 2 * LORA
    xp = x_prompt.reshape(bp * seq, d)
    xs = x_sample.reshape(bs, d)
    fg = final_norm_g.reshape(1, d)
    outs = [[] for _ in range(8)]
    for l in range(depth):
        p = _layer_params(l, w_in, rwkv_mu, w_decay0, w_decay_up, w_iclr0, w_iclr_up, k_k, k_a, r_k,
                          ln_x_g, ln_x_b, w_out_rwkv, conv_w, conv_b, lru_gx_w, lru_gx_b, lru_ga_w,
                          lru_ga_b, lru_lambda, w_out_lru, w_out)
        g = norm_g[l].reshape(1, d)
        rec = (p["pvec"], p["mul"], p["wd"], p["wa"], p["e"])

        w = _repack_w_in(w_in, l, _row_tile(d, 256))
        zp, zlp, zs, zls = _inproj(xp, xs, g, w, _row_tile(bp * seq, 1024), INPROJ_TN)

        zp3 = zp.reshape(bp, seq, -1)
        zlp3 = zlp.reshape(bp, seq, LANES)
        nb = max(n for n in (4, 2, 1) if bp % n == 0)
        o_r, s_new = _wkv_chunk(zp3, zlp3, *rec, bp, seq, nb)
        o_r = o_r.reshape(bp * seq, RWKV_W)
        last = l == depth - 1
        xp, h_last = _outproj_lru(xp, o_r, zp, p["lp"], p["wg"], p["w_r"], p["w_g"], p["w_o"], fg,
                                  _row_tile(seq, 256), seq, last)
        outs[0].append(jnp.concatenate([zp3[:, -1, :SHIFT_MAIN], zlp3[:, -1]], axis=-1))
        outs[1].append(s_new)
        outs[2].append(zp3[:, seq - (CONV_W - 1):, 4 * RWKV_W:4 * RWKV_W + LRU_W])
        outs[3].append(h_last.reshape(bp, LRU_W))

        sh = state_shift[l]
        s0t = jnp.transpose(state_wkv[l], (1, 2, 3, 0))
        o_r, s_new = _wkv_step(zs, zls, sh[:, :SHIFT_MAIN], sh[:, SHIFT_MAIN:sh_w], s0t, *rec)
        s_new = jnp.transpose(s_new, (3, 0, 1, 2))
        conv = state_conv[l]
        o_g, h_new = _lru_step(zs, conv.reshape(bs, (CONV_W - 1) * LRU_W), state_lru[l], p["lp"], p["wg"])
        xs = _outproj(xs, o_r, o_g, zs, p["w_r"], p["w_g"], p["w_o"], fg, bs, last)
        z_lx = zs[:, 4 * RWKV_W:4 * RWKV_W + LRU_W]
        outs[4].append(jnp.concatenate([zs[:, :SHIFT_MAIN], zls], axis=-1))
        outs[5].append(s_new)
        outs[6].append(jnp.concatenate([conv[:, 1:], z_lx[:, None, :]], axis=1))
        outs[7].append(h_new)

    return (xp.reshape(bp, seq, d), xs.reshape(bs, 1, d)) + tuple(jnp.stack(o) for o in outs)
```

```python
import functools

import jax
import jax.numpy as jnp
from jax import lax
from jax.experimental import pallas as pl
from jax.experimental.pallas import tpu as pltpu

F32 = jnp.float32
BF16 = jnp.bfloat16

HEADS = 16
HEAD = 64
RWKV_W = HEADS * HEAD
LORA = 64
LRU_W = 1024
LRU_BLOCKS = 16
CONV_W = 4
LRU_C = 8.0
RMS_EPS = 1e-6
GN_EPS = 1e-5 * HEAD
DECAY_SCALE = 0.6065306597126334

LANES = 128
SUBLANES = 8
WKV_CHUNK = 64
VMEM_LIMIT = 56 * 1024 * 1024

NN = (((1,), (0,)), ((), ()))
NT = (((1,), (1,)), ((), ()))
TN = (((0,), (0,)), ((), ()))


def _bf(x):
    return x.astype(BF16)


def _dg(a, b, dn):
    return lax.dot_general(a, b, dn, preferred_element_type=F32)


def _softplus(x):
    return jnp.maximum(x, 0.0) + jnp.log1p(jnp.exp(-jnp.abs(x)))


def _sigmoid(x):
    return 1.0 / (1.0 + jnp.exp(-x))


def _segsum(x, e):
    rows, n = x.shape[0], x.shape[1] // LANES
    stacked = jnp.concatenate([x[:, LANES * j:LANES * (j + 1)] for j in range(n)], axis=0)
    s = _dg(_bf(stacked), e, NN)
    return jnp.concatenate([s[rows * j:rows * (j + 1), :] for j in range(n)], axis=1)


def _rms(x, g):
    return x * lax.rsqrt(jnp.mean(x * x, axis=-1, keepdims=True) + RMS_EPS) * g


SHIFT_MAIN = 3 * RWKV_W
LORA_COL = 10 * RWKV_W
LORA_BLOCK = LORA_COL // LANES
INPROJ_TN = 1024


def _repack_kernel(w_ref, o_ref):
    n = w_ref.shape[-1]
    lora_end = SHIFT_MAIN + 2 * LORA
    o_ref[:, 0:SHIFT_MAIN] = _bf(w_ref[:, 0:SHIFT_MAIN])
    o_ref[:, SHIFT_MAIN:LORA_COL] = _bf(w_ref[:, lora_end:n])
    o_ref[:, LORA_COL:n] = _bf(w_ref[:, SHIFT_MAIN:lora_end])


def _repack_w_in(w_in, layer, tr):
    _, d, n = w_in.shape
    assert n == LORA_COL + 2 * LORA
    return pl.pallas_call(
        _repack_kernel,
        out_shape=jax.ShapeDtypeStruct((d, n), BF16),
        grid=(d // tr,),
        in_specs=[pl.BlockSpec((None, tr, n), lambda i: (layer, i, 0))],
        out_specs=pl.BlockSpec((tr, n), lambda i: (i, 0)),
        compiler_params=pltpu.CompilerParams(
            dimension_semantics=("arbitrary",), vmem_limit_bytes=VMEM_LIMIT),
        name="repack_w_in",
    )(w_in)


def _inproj_kernel(x_ref, xs_ref, g_ref, w_ref, wl_ref, z_ref, zl_ref, zs_ref, zls_ref, h_ref, hs_ref):
    i = pl.program_id(0)
    j = pl.program_id(1)

    @pl.when((i == 0) & (j == 0))
    def _():
        hs = _bf(_rms(xs_ref[...], g_ref[...]))
        hs_ref[...] = hs
        zls_ref[...] = _dg(hs, wl_ref[...], NN)

    @pl.when(j == 0)
    def _():
        h = _bf(_rms(x_ref[...], g_ref[...]))
        h_ref[...] = h
        zl_ref[...] = _dg(h, wl_ref[...], NN)

    z_ref[...] = _dg(h_ref[...], w_ref[...], NN)

    @pl.when(i == 0)
    def _():
        zs_ref[...] = _dg(hs_ref[...], w_ref[...], NN)


def _inproj(x, xs, g, w, tm, tn):
    m, d = x.shape
    ms = xs.shape[0]
    nj = LORA_COL // tn
    park = lambda i, j: jnp.where(i == 0, j, nj - 1)
    return pl.pallas_call(
        _inproj_kernel,
        out_shape=(jax.ShapeDtypeStruct((m, LORA_COL), F32), jax.ShapeDtypeStruct((m, LANES), F32),
                   jax.ShapeDtypeStruct((ms, LORA_COL), F32), jax.ShapeDtypeStruct((ms, LANES), F32)),
        grid=(m // tm, nj),
        in_specs=[
            pl.BlockSpec((tm, d), lambda i, j: (i, 0)),
            pl.BlockSpec((ms, d), lambda i, j: (0, 0), pipeline_mode=pl.Buffered(1)),
            pl.BlockSpec((1, d), lambda i, j: (0, 0), pipeline_mode=pl.Buffered(1)),
            pl.BlockSpec((d, tn), lambda i, j: (0, j)),
            pl.BlockSpec((d, LANES), lambda i, j: (0, LORA_BLOCK), pipeline_mode=pl.Buffered(1)),
        ],
        out_specs=(
            pl.BlockSpec((tm, tn), lambda i, j: (i, j)),
            pl.BlockSpec((tm, LANES), lambda i, j: (i, 0)),
            pl.BlockSpec((ms, tn), lambda i, j: (0, park(i, j))),
            pl.BlockSpec((ms, LANES), lambda i, j: (0, 0)),
        ),
        scratch_shapes=[pltpu.VMEM((tm, d), BF16), pltpu.VMEM((ms, d), BF16)],
        compiler_params=pltpu.CompilerParams(
            dimension_semantics=("arbitrary", "arbitrary"), vmem_limit_bytes=VMEM_LIMIT),
        name="inproj",
    )(x, xs, g, w, w)


_MU_R, _MU_K, _MU_V, _W0, _A0, _KK, _KA, _RK, _LNG, _LNB = range(10)


def _prow(pv_ref, i):
    return pv_ref[i:i + 1, :]


def _wkv_prep(zr, zk, zv, zl, pr, pk, pv, pl_, pv_ref, mul_ref, wd_ref, wa_ref, e):
    r = zr + _prow(pv_ref, _MU_R) * (pr - zr)
    k = zk + _prow(pv_ref, _MU_K) * (pk - zk)
    v = zv + _prow(pv_ref, _MU_V) * (pv - zv)
    lo = zl + mul_ref[0:1, :] * (pl_ - zl)
    lw = _dg(_bf(jnp.tanh(lo)), wd_ref[...], NN)
    la = _dg(_bf(lo), wa_ref[...], NN)
    logd = -DECAY_SCALE * _sigmoid(_prow(pv_ref, _W0) + lw)
    a = _sigmoid(_prow(pv_ref, _A0) + la)
    kk = k * _prow(pv_ref, _KK)
    kk = kk * lax.rsqrt(jnp.maximum(_segsum(kk * kk, e), 1e-24))
    k2 = k * (1.0 + (a - 1.0) * _prow(pv_ref, _KA))
    return r, k2, v, -kk, kk * a, logd


def _wkv_post(y, r, k2, v, zrg, pv_ref, e):
    mu = _segsum(y, e) * (1.0 / HEAD)
    yc = y - mu
    var = _segsum(yc * yc, e) * (1.0 / HEAD)
    yn = yc * lax.rsqrt(var + GN_EPS) * _prow(pv_ref, _LNG) + _prow(pv_ref, _LNB)
    bonus = _segsum(r * k2 * _prow(pv_ref, _RK), e)
    return _bf((yn + bonus * v) * (zrg * _sigmoid(zrg)))


def _wkv_chunk_kernel(zr_ref, zk_ref, zv_ref, zrg_ref, zl_ref, pv_ref, mul_ref, wd_ref, wa_ref,
                      e_ref, o_ref, sout_ref, s_s, prev_s, prevl_s):
    c = pl.program_id(1)
    nc = pl.num_programs(1)
    C = WKV_CHUNK
    assert C == HEAD and 2 * HEAD == LANES
    nb = zr_ref.shape[0]
    rows_all = nb * C
    seqs = range(nb)

    @pl.when(c == 0)
    def _():
        s_s[...] = jnp.zeros_like(s_s)
        prev_s[...] = jnp.zeros_like(prev_s)
        prevl_s[...] = jnp.zeros_like(prevl_s)

    first = lax.broadcasted_iota(jnp.int32, (SUBLANES, 1), 0) == 0

    def shifted(z, prev_ref, lanes):
        rolled = pltpu.roll(z, 1, 0)
        pieces = []
        for b in seqs:
            head = jnp.where(first, prev_ref[b, 0:1, lanes], rolled[b * C:b * C + SUBLANES, :])
            pieces += [head, rolled[b * C + SUBLANES:(b + 1) * C, :]]
        return jnp.concatenate(pieces, axis=0)

    def flat(ref):
        return ref[...].reshape(rows_all, ref.shape[-1])

    zr, zk, zv, zl = flat(zr_ref), flat(zk_ref), flat(zv_ref), flat(zl_ref)
    seg = [slice(RWKV_W * i, RWKV_W * (i + 1)) for i in range(3)]
    pr = shifted(zr, prev_s, seg[0])
    pk = shifted(zk, prev_s, seg[1])
    pv = shifted(zv, prev_s, seg[2])
    pl_ = shifted(zl, prevl_s, slice(0, LANES))
    for b in seqs:
        last = slice(b * C + C - 1, b * C + C)
        prev_s[b, 0:1, seg[0]] = zr[last, :]
        prev_s[b, 0:1, seg[1]] = zk[last, :]
        prev_s[b, 0:1, seg[2]] = zv[last, :]
        prevl_s[b, 0:1, :] = zl[last, :]

    e = e_ref[...]
    r, k2, v, av, bv, logd = _wkv_prep(zr, zk, zv, zl, pr, pk, pv, pl_, pv_ref, mul_ref,
                                       wd_ref, wa_ref, e)

    ti = lax.broadcasted_iota(jnp.int32, (rows_all, rows_all), 0)
    tj = lax.broadcasted_iota(jnp.int32, (rows_all, rows_all), 1)
    tri = jnp.where((ti >= tj) & ((ti & -C) == (tj & -C)), 1.0, 0.0).astype(BF16)
    d_hi = _bf(logd)
    d_r1 = logd - d_hi.astype(F32)
    d_mid = _bf(d_r1)
    d_lo = _bf(d_r1 - d_mid.astype(F32))
    cum = _dg(tri, d_hi, NN) + (_dg(tri, d_mid, NN) + _dg(tri, d_lo, NN))
    e_in = jnp.exp(cum)
    e_neg = jnp.exp(-cum)
    a_t = av * jnp.exp(cum - logd)
    r_t = r * e_in
    k_t = k2 * e_neg
    b_t = bv * e_neg
    p_c = [jnp.exp(cum[b * C + C - 1:b * C + C, :]) for b in seqs]

    lane = lax.broadcasted_iota(jnp.int32, (C, LANES), 1)
    trow = lax.broadcasted_iota(jnp.int32, (C, LANES), 0)
    lo = lane < HEAD
    s_in = lane & (HEAD - 1)
    strict = s_in < trow
    incl2 = ((lax.broadcasted_iota(jnp.int32, (C, 2 * LANES), 1) & (HEAD - 1))
             <= lax.broadcasted_iota(jnp.int32, (C, 2 * LANES), 0))
    eye2 = jnp.where(s_in == trow, 1.0, 0.0).astype(F32)
    vrow = lax.broadcasted_iota(jnp.int32, (2 * HEAD, LANES), 0)
    klane = lax.broadcasted_iota(jnp.int32, (2 * HEAD, LANES), 1)
    same_head = (vrow < HEAD) == (klane < HEAD)

    def bd(x):
        z = jnp.zeros_like(x)
        return jnp.concatenate([jnp.where(lo, x, z), jnp.where(lo, z, x)], axis=0)

    npair = HEADS // 2
    units = [(b, p) for b in seqs for p in range(npair)]
    un = range(len(units))
    blk = lambda arr, i: arr[units[i][0] * C:(units[i][0] + 1) * C, LANES * units[i][1]:LANES * (units[i][1] + 1)]
    ar = [_bf(jnp.concatenate([blk(a_t, i), blk(r_t, i)], axis=0)) for i in un]
    bk = [_bf(jnp.concatenate([bd(blk(b_t, i)), bd(blk(k_t, i))], axis=0)) for i in un]
    g = [_dg(ar[i], bk[i], NT) for i in un]
    s0 = [s_s[i] for i in un]
    ars = [_dg(ar[i], _bf(s0[i]), NT) for i in un]
    vbd = [_bf(bd(blk(v, i))) for i in un]
    x = [jnp.where(strict, g[i][0:C, 0:LANES], 0.0) for i in un]
    ak = [jnp.where(strict, g[i][0:C, LANES:2 * LANES], 0.0) for i in un]
    w = [ars[i][0:C, :] + _dg(_bf(ak[i]), vbd[i], NN) for i in un]
    t = [eye2 + x[i] for i in un]
    x = [_dg(_bf(x[i]), _bf(bd(x[i])), NN) for i in un]
    for _ in range(C.bit_length() - 3):
        xt = [_dg(_bf(jnp.concatenate([x[i], t[i]], axis=0)), _bf(bd(x[i])), NN) for i in un]
        x = [xt[i][0:C, :] for i in un]
        t = [t[i] + xt[i][C:2 * C, :] for i in un]
    t = [t[i] + _dg(_bf(t[i]), _bf(bd(x[i])), NN) for i in un]
    u = [_dg(_bf(t[i]), _bf(bd(w[i])), NN) for i in un]
    rbk = [_bf(jnp.where(incl2, g[i][C:2 * C, :], 0.0)) for i in un]
    uvbd = [jnp.concatenate([_bf(bd(u[i])), vbd[i]], axis=0) for i in un]
    y = [ars[i][C:2 * C, :] + _dg(rbk[i], uvbd[i], NN) for i in un]
    uv = [_bf(jnp.concatenate([u[i], blk(v, i)], axis=0)) for i in un]
    pc = [p_c[units[i][0]][:, LANES * units[i][1]:LANES * (units[i][1] + 1)] for i in un]
    bkh = [_bf(jnp.concatenate([blk(b_t, i), blk(k_t, i)], axis=0) * pc[i]) for i in un]
    s1 = [s0[i] * pc[i] + jnp.where(same_head, _dg(uv[i], bkh[i], TN), 0.0) for i in un]
    for i in un:
        s_s[i] = s1[i]

    y_all = jnp.concatenate(
        [jnp.concatenate(y[b * npair:(b + 1) * npair], axis=1) for b in seqs], axis=0)
    o = _wkv_post(y_all, r, k2, v, flat(zrg_ref), pv_ref, e)
    o_ref[...] = o.reshape(nb, C, RWKV_W)

    @pl.when(c == nc - 1)
    def _():
        for i in un:
            b, p = units[i]
            sout_ref[b, 2 * p] = s1[i][0:HEAD, 0:HEAD]
            sout_ref[b, 2 * p + 1] = s1[i][HEAD:2 * HEAD, HEAD:2 * HEAD]


def _wkv_chunk(z, zl, pvec, mul, wd, wa, e, batch, seq, nb):
    C = WKV_CHUNK
    nc = seq // C
    full = lambda shp: pl.BlockSpec(shp, lambda b, c: (0,) * len(shp))
    col = lambda j: pl.BlockSpec((nb, C, RWKV_W), lambda b, c, j=j: (b, c, j))
    return pl.pallas_call(
        _wkv_chunk_kernel,
        out_shape=(jax.ShapeDtypeStruct((batch, seq, RWKV_W), BF16),
                   jax.ShapeDtypeStruct((batch, HEADS, HEAD, HEAD), F32)),
        grid=(batch // nb, nc),
        in_specs=[col(0), col(1), col(2), col(3),
                  pl.BlockSpec((nb, C, LANES), lambda b, c: (b, c, 0)),
                  full(pvec.shape), full(mul.shape), full(wd.shape), full(wa.shape), full(e.shape)],
        out_specs=(pl.BlockSpec((nb, C, RWKV_W), lambda b, c: (b, c, 0)),
                   pl.BlockSpec((nb, HEADS, HEAD, HEAD), lambda b, c: (b, 0, 0, 0))),
        scratch_shapes=[pltpu.VMEM((nb * HEADS // 2, 2 * HEAD, 2 * HEAD), F32),
                        pltpu.VMEM((nb, SUBLANES, 3 * RWKV_W), F32),
                        pltpu.VMEM((nb, SUBLANES, LANES), F32)],
        compiler_params=pltpu.CompilerParams(
            dimension_semantics=("arbitrary", "arbitrary"), vmem_limit_bytes=VMEM_LIMIT),
        name="wkv_chunk",
    )(z, z, z, z, zl, pvec, mul, wd, wa, e)


def _wkv_step_kernel(zr_ref, zk_ref, zv_ref, zrg_ref, zl_ref, sh_ref, shl_ref, s0_ref, pv_ref,
                     mul_ref, wd_ref, wa_ref, e_ref, o_ref, sout_ref,
                     at_s, drt_s, bt_s, kt_s, dt_s, vt_s, brt_s, krt_s, yt_s, keep_s):
    h = pl.program_id(0)
    nh = pl.num_programs(0)
    nseq = zr_ref.shape[0]

    @pl.when(h == 0)
    def _():
        e = e_ref[...]
        r, k2, v, av, bv, logd = _wkv_prep(
            zr_ref[...], zk_ref[...], zv_ref[...], zl_ref[...],
            sh_ref[:, 0:RWKV_W], sh_ref[:, RWKV_W:2 * RWKV_W], sh_ref[:, 2 * RWKV_W:3 * RWKV_W],
            shl_ref[...], pv_ref, mul_ref, wd_ref, wa_ref, e)
        d = jnp.exp(logd)
        at_s[...] = av.T
        drt_s[...] = (d * r).T
        bt_s[...] = bv.T
        kt_s[...] = k2.T
        dt_s[...] = d.T
        vt_s[...] = v.T
        brt_s[...] = jnp.sum((bv * r).T.reshape(HEADS, HEAD, nseq), axis=1)
        krt_s[...] = jnp.sum((k2 * r).T.reshape(HEADS, HEAD, nseq), axis=1)
        keep_s[0] = r
        keep_s[1] = k2
        keep_s[2] = v

    base = pl.multiple_of(h * HEAD, HEAD)
    rows = pl.ds(base, HEAD)
    a_h, dr_h, b_h, k_h, d_h = at_s[rows, :], drt_s[rows, :], bt_s[rows, :], kt_s[rows, :], dt_s[rows, :]
    br_h = brt_s[pl.ds(h, 1), :]
    kr_h = krt_s[pl.ds(h, 1), :]

    def value_rows(g, carry):
        off = pl.multiple_of(base + g * SUBLANES, SUBLANES)
        v8 = vt_s[pl.ds(off, SUBLANES), :]
        ys = []
        for j in range(SUBLANES):
            vi = g * SUBLANES + j
            s_v = s0_ref[0, vi]
            sa = jnp.sum(s_v * a_h, axis=0, keepdims=True)
            y0 = jnp.sum(s_v * dr_h, axis=0, keepdims=True)
            v_v = v8[j:j + 1, :]
            sout_ref[0, vi] = s_v * d_h + sa * b_h + v_v * k_h
            ys.append(y0 + sa * br_h + v_v * kr_h)
        yt_s[pl.ds(off, SUBLANES), :] = jnp.concatenate(ys, axis=0)
        return carry

    lax.fori_loop(0, HEAD // SUBLANES, value_rows, 0)

    @pl.when(h == nh - 1)
    def _():
        o_ref[...] = _wkv_post(yt_s[...].T, keep_s[0], keep_s[1], keep_s[2], zrg_ref[...], pv_ref,
                               e_ref[...])


def _wkv_step(z, zl, sh_main, sh_lora, s0t, pvec, mul, wd, wa, e):
    nseq = z.shape[0]
    full = lambda shp: pl.BlockSpec(shp, lambda i: (0,) * len(shp))
    col = lambda j: pl.BlockSpec((nseq, RWKV_W), lambda i, j=j: (0, j))
    st_block = (1, HEAD, HEAD, nseq)
    wide = pltpu.VMEM((RWKV_W, nseq), F32)
    return pl.pallas_call(
        _wkv_step_kernel,
        out_shape=(jax.ShapeDtypeStruct((nseq, RWKV_W), BF16),
                   jax.ShapeDtypeStruct(s0t.shape, F32)),
        grid=(HEADS,),
        in_specs=[col(0), col(1), col(2), col(3),
                  full(zl.shape),
                  full(sh_main.shape), full(sh_lora.shape),
                  pl.BlockSpec(st_block, lambda i: (i, 0, 0, 0)),
                  full(pvec.shape), full(mul.shape), full(wd.shape), full(wa.shape), full(e.shape)],
        out_specs=(full((nseq, RWKV_W)),
                   pl.BlockSpec(st_block, lambda i: (i, 0, 0, 0))),
        scratch_shapes=[wide] * 6 + [pltpu.VMEM((HEADS, nseq), F32)] * 2
                       + [wide, pltpu.VMEM((3, nseq, RWKV_W), F32)],
        compiler_params=pltpu.CompilerParams(
            dimension_semantics=("arbitrary",), vmem_limit_bytes=VMEM_LIMIT),
        name="wkv_step",
    )(z, z, z, z, zl, sh_main, sh_lora, s0t, pvec, mul, wd, wa, e)


_CW0, _CW1, _CW2, _CW3, _CB, _GXB, _GAB, _LAM = range(8)


def _lru_gates(xc, lp_ref, wg_ref):
    xb = _bf(xc)
    ngroups = wg_ref.shape[0]
    gs = [_dg(xb[:, LANES * g:LANES * (g + 1)], wg_ref[g], NN) for g in range(ngroups)]
    gx_pre = jnp.concatenate([gs[g][:, 0:LANES] for g in range(ngroups)], axis=1)
    ga_pre = jnp.concatenate([gs[g][:, LANES:2 * LANES] for g in range(ngroups)], axis=1)
    gx = _sigmoid(gx_pre + _prow(lp_ref, _GXB))
    ga = _sigmoid(ga_pre + _prow(lp_ref, _GAB))
    log_a = -LRU_C * ga * _softplus(-_prow(lp_ref, _LAM))
    a = jnp.exp(log_a)
    mult = jnp.sqrt((1.0 - a) * (1.0 + a))
    return a, mult * gx * xc


def _lru_coeffs(zx, first, xb_s, lp_ref, wgate_ref):
    tl = zx.shape[0]
    xb_s[0:SUBLANES, :] = jnp.where(first, 0.0, xb_s[0:SUBLANES, :])
    xb_s[SUBLANES:SUBLANES + tl, :] = zx
    xc = _prow(lp_ref, _CW3) * zx + _prow(lp_ref, _CB)
    for j in range(1, CONV_W):
        xc = xc + _prow(lp_ref, CONV_W - 1 - j) * xb_s[SUBLANES - j:SUBLANES - j + tl, :]
    xb_s[0:SUBLANES, :] = zx[tl - SUBLANES:tl, :]
    return _lru_gates(xc, lp_ref, wgate_ref)


def _lru_scan_gate(a, b, zg, first, hc_s):
    tl = a.shape[0]
    row8 = lax.broadcasted_iota(jnp.int32, (SUBLANES, 1), 0)
    hc = jnp.where(first, 0.0, hc_s[...])
    hs = []
    for i in range(tl // SUBLANES):
        a8 = a[SUBLANES * i:SUBLANES * (i + 1), :]
        b8 = b[SUBLANES * i:SUBLANES * (i + 1), :]
        for s in (1, 2, 4):
            keep = row8 >= s
            b8 = jnp.where(keep, a8 * pltpu.roll(b8, s, 0) + b8, b8)
            a8 = jnp.where(keep, a8 * pltpu.roll(a8, s, 0), a8)
        hb = b8 + a8 * hc
        hs.append(hb)
        hc = jnp.broadcast_to(hb[SUBLANES - 1:SUBLANES, :], hb.shape)
    hc_s[...] = hc
    return _bf(jnp.concatenate(hs, axis=0) * (zg * _sigmoid(zg))), hc


def _lru_step_kernel(zx_ref, zg_ref, conv_ref, h0_ref, lp_ref, wg_ref, o_ref, hnew_ref):
    zx = zx_ref[...]
    xc = _prow(lp_ref, _CW3) * zx + _prow(lp_ref, _CB)
    for j in range(CONV_W - 1):
        xc = xc + _prow(lp_ref, j) * conv_ref[:, LRU_W * j:LRU_W * (j + 1)]
    a, b = _lru_gates(xc, lp_ref, wg_ref)
    h = a * h0_ref[...] + b
    hnew_ref[...] = h
    zg = zg_ref[...]
    o_ref[...] = _bf(h * (zg * _sigmoid(zg)))


def _lru_step(z_main, conv, h0, lp, wg):
    nb = z_main.shape[0]
    full = lambda shp: pl.BlockSpec(shp, lambda i: (0,) * len(shp))
    col = lambda j: pl.BlockSpec((nb, LRU_W), lambda i, j=j: (0, j))
    return pl.pallas_call(
        _lru_step_kernel,
        out_shape=(jax.ShapeDtypeStruct((nb, LRU_W), BF16), jax.ShapeDtypeStruct((nb, LRU_W), F32)),
        grid=(1,),
        in_specs=[col(4), col(5), full(conv.shape), full(h0.shape), full(lp.shape), full(wg.shape)],
        out_specs=(full((nb, LRU_W)), full((nb, LRU_W))),
        compiler_params=pltpu.CompilerParams(
            dimension_semantics=("arbitrary",), vmem_limit_bytes=VMEM_LIMIT),
        name="lru_step",
    )(z_main, z_main, conv, h0, lp, wg)


def _outproj_kernel(x_ref, or_ref, og_ref, mr_ref, mg_ref, wr_ref, wg_ref, wo_ref, fg_ref,
                    out_ref, *, final):
    y_r = _dg(or_ref[...], wr_ref[...], NN)
    y_g = _dg(og_ref[...], wg_ref[...], NN)
    merged = _sigmoid(mr_ref[...]) * y_r + _sigmoid(mg_ref[...]) * y_g
    out = x_ref[...] + _dg(_bf(merged), wo_ref[...], NN)
    out_ref[...] = _rms(out, fg_ref[...]) if final else out


def _outproj(x, o_r, o_g, z_main, w_r, w_g, w_o, fg, tm, final):
    m, d = x.shape
    const = lambda shp: pl.BlockSpec(shp, lambda i: (0,) * len(shp), pipeline_mode=pl.Buffered(1))
    return pl.pallas_call(
        functools.partial(_outproj_kernel, final=final),
        out_shape=jax.ShapeDtypeStruct((m, d), F32),
        grid=(m // tm,),
        in_specs=[
            pl.BlockSpec((tm, d), lambda i: (i, 0)),
            pl.BlockSpec((tm, RWKV_W), lambda i: (i, 0)),
            pl.BlockSpec((tm, LRU_W), lambda i: (i, 0)),
            pl.BlockSpec((tm, d), lambda i: (i, 3)),
            pl.BlockSpec((tm, d), lambda i: (i, 4)),
            const(w_r.shape), const(w_g.shape), const(w_o.shape), const(fg.shape),
        ],
        out_specs=pl.BlockSpec((tm, d), lambda i: (i, 0)),
        compiler_params=pltpu.CompilerParams(
            dimension_semantics=("arbitrary",), vmem_limit_bytes=VMEM_LIMIT),
        name="outproj",
    )(x, o_r, o_g, z_main, z_main, w_r, w_g, w_o, fg)


def _outproj_lru_kernel(x_ref, or_ref, mr_ref, mg_ref, zx_ref, zg_ref, lp_ref, wgate_ref, wr_ref,
                        wg_ref, wo_ref, fg_ref, out_ref, hlast_ref, og_s, xb_s, hc_s, *, final,
                        tiles_per_seq):
    i = pl.program_id(0)
    n = pl.num_programs(0) - 1

    @pl.when(i == 0)
    def _():
        og_s[...] = jnp.zeros_like(og_s)
        xb_s[...] = jnp.zeros_like(xb_s)
        hc_s[...] = jnp.zeros_like(hc_s)

    og_prev = og_s[...]
    t = lax.rem(jnp.minimum(i, n - 1), tiles_per_seq)
    a, b = _lru_coeffs(zx_ref[...], t == 0, xb_s, lp_ref, wgate_ref)
    y_r = _dg(or_ref[...], wr_ref[...], NN)
    y_g = _dg(og_prev, wg_ref[...], NN)
    merged = _sigmoid(mr_ref[...]) * y_r + _sigmoid(mg_ref[...]) * y_g
    out = x_ref[...] + _dg(_bf(merged), wo_ref[...], NN)
    o_g, hc = _lru_scan_gate(a, b, zg_ref[...], t == 0, hc_s)
    og_s[...] = o_g
    out_ref[...] = _rms(out, fg_ref[...]) if final else out

    @pl.when((t == tiles_per_seq - 1) & (i < n))
    def _():
        hlast_ref[0] = hc[0:1, :]


def _outproj_lru(x, o_r, z, lp, wgate, w_r, w_g, w_o, fg, tm, seq, final):
    m, d = x.shape
    n = m // tm
    tiles_per_seq = seq // tm
    const = lambda shp: pl.BlockSpec(shp, lambda i: (0,) * len(shp), pipeline_mode=pl.Buffered(1))
    prev = lambda i: jnp.maximum(i - 1, 0)
    here = lambda i: jnp.minimum(i, n - 1)
    return pl.pallas_call(
        functools.partial(_outproj_lru_kernel, final=final, tiles_per_seq=tiles_per_seq),
        out_shape=(jax.ShapeDtypeStruct((m, d), F32),
                   jax.ShapeDtypeStruct((m // seq, 1, LRU_W), F32)),
        grid=(n + 1,),
        in_specs=[
            pl.BlockSpec((tm, d), lambda i: (prev(i), 0)),
            pl.BlockSpec((tm, RWKV_W), lambda i: (prev(i), 0)),
            pl.BlockSpec((tm, d), lambda i: (prev(i), 3)),
            pl.BlockSpec((tm, d), lambda i: (prev(i), 4)),
            pl.BlockSpec((tm, LRU_W), lambda i: (here(i), 4)),
            pl.BlockSpec((tm, LRU_W), lambda i: (here(i), 5)),
            const(lp.shape), const(wgate.shape), const(w_r.shape), const(w_g.shape), const(w_o.shape),
            const(fg.shape),
        ],
        out_specs=(pl.BlockSpec((tm, d), lambda i: (prev(i), 0)),
                   pl.BlockSpec((1, 1, LRU_W), lambda i: (here(i) // tiles_per_seq, 0, 0))),
        scratch_shapes=[pltpu.VMEM((tm, LRU_W), BF16),
                        pltpu.VMEM((SUBLANES + tm, LRU_W), F32),
                        pltpu.VMEM((SUBLANES, LRU_W), F32)],
        compiler_params=pltpu.CompilerParams(
            dimension_semantics=("arbitrary",), vmem_limit_bytes=VMEM_LIMIT),
        name="outproj_lru",
    )(x, o_r, z, z, z, z, lp, wgate, w_r, w_g, w_o, fg)


def _row_tile(m, want):
    t = min(m, want)
    assert m % t == 0, (m, t)
    return t


def _layer_params(l, w_in, rwkv_mu, w_decay0, w_decay_up, w_iclr0, w_iclr_up, k_k, k_a, r_k,
                  ln_x_g, ln_x_b, w_out_rwkv, conv_w, conv_b, lru_gx_w, lru_gx_b, lru_ga_w,
                  lru_ga_b, lru_lambda, w_out_lru, w_out):
    sh_w = 3 * RWKV_W + 2 * LORA
    p = {}
    mu = rwkv_mu[l]
    rows = [mu[0:RWKV_W], mu[RWKV_W:2 * RWKV_W], mu[2 * RWKV_W:3 * RWKV_W], w_decay0[l], w_iclr0[l],
            k_k[l], k_a[l], r_k[l].reshape(-1), ln_x_g[l], ln_x_b[l]]
    p["pvec"] = jnp.concatenate(
        [jnp.stack(rows), jnp.zeros((16 - len(rows), RWKV_W), F32)], axis=0).astype(F32)
    p["mul"] = jnp.broadcast_to(mu[3 * RWKV_W:sh_w][None, :], (SUBLANES, 2 * LORA)).astype(F32)
    zeros = jnp.zeros((LORA, RWKV_W), F32)
    p["wd"] = jnp.concatenate([w_decay_up[l], zeros], axis=0).astype(BF16)
    p["wa"] = jnp.concatenate([zeros, w_iclr_up[l]], axis=0).astype(BF16)
    lane_head = jnp.arange(LANES) // HEAD
    p["e"] = (lane_head[:, None] == lane_head[None, :]).astype(BF16)
    p["lp"] = jnp.concatenate(
        [conv_w[l], conv_b[l][None], lru_gx_b[l][None], lru_ga_b[l][None], lru_lambda[l][None]],
        axis=0).astype(F32)
    blk = LRU_W // LRU_BLOCKS
    assert 2 * blk == LANES

    def pair_bd(g):
        g = g.reshape(LRU_BLOCKS // 2, 2, blk, blk)
        z = jnp.zeros_like(g[:, 0])
        return jnp.concatenate([jnp.concatenate([g[:, 0], z], axis=2),
                                jnp.concatenate([z, g[:, 1]], axis=2)], axis=1)

    p["wg"] = jnp.concatenate([pair_bd(lru_gx_w[l]), pair_bd(lru_ga_w[l])], axis=2).astype(BF16)
    p["w_r"] = w_out_rwkv[l].astype(BF16)
    p["w_g"] = w_out_lru[l].astype(BF16)
    p["w_o"] = w_out[l].astype(BF16)
    return p


def kernel(x_prompt, x_sample, state_shift, state_wkv, state_conv, state_lru, norm_g, w_in, rwkv_mu,
           w_decay0, w_decay_up, w_iclr0, w_iclr_up, k_k, k_a, r_k, ln_x_g, ln_x_b, w_out_rwkv,
           conv_w, conv_b, lru_gx_w, lru_gx_b, lru_ga_w, lru_ga_b, lru_lambda, w_out_lru, w_out,
           final_norm_g):
    bp, seq, d = x_prompt.shape
    bs = x_sample.shape[0]
    assert x_sample.shape[1] == 1 and seq % WKV_CHUNK == 0
    depth = w_in.shape[0]
    sh_w = 3 * RWKV_W + 2 * LORA
    xp = x_prompt.reshape(bp * seq, d)
    xs = x_sample.reshape(bs, d)
    fg = final_norm_g.reshape(1, d)
    outs = [[] for _ in range(8)]
    for l in range(depth):
        p = _layer_params(l, w_in, rwkv_mu, w_decay0, w_decay_up, w_iclr0, w_iclr_up, k_k, k_a, r_k,
                          ln_x_g, ln_x_b, w_out_rwkv, conv_w, conv_b, lru_gx_w, lru_gx_b, lru_ga_w,
                          lru_ga_b, lru_lambda, w_out_lru, w_out)
        g = norm_g[l].reshape(1, d)
        rec = (p["pvec"], p["mul"], p["wd"], p["wa"], p["e"])

        w = _repack_w_in(w_in, l, _row_tile(d, 256))
        zp, zlp, zs, zls = _inproj(xp, xs, g, w, _row_tile(bp * seq, 1024), INPROJ_TN)

        zp3 = zp.reshape(bp, seq, -1)
        zlp3 = zlp.reshape(bp, seq, LANES)
        nb = max(n for n in (4, 2, 1) if bp % n == 0)
        o_r, s_new = _wkv_chunk(zp3, zlp3, *rec, bp, seq, nb)
        o_r = o_r.reshape(bp * seq, RWKV_W)
        last = l == depth - 1
        xp, h_last = _outproj_lru(xp, o_r, zp, p["lp"], p["wg"], p["w_r"], p["w_g"], p["w_o"], fg,
                                  _row_tile(seq, 256), seq, last)
        outs[0].append(jnp.concatenate([zp3[:, -1, :SHIFT_MAIN], zlp3[:, -1]], axis=-1))
        outs[1].append(s_new)
        outs[2].append(zp3[:, seq - (CONV_W - 1):, 4 * RWKV_W:4 * RWKV_W + LRU_W])
        outs[3].append(h_last.reshape(bp, LRU_W))

        sh = state_shift[l]
        s0t = jnp.transpose(state_wkv[l], (1, 2, 3, 0))
        o_r, s_new = _wkv_step(zs, zls, sh[:, :SHIFT_MAIN], sh[:, SHIFT_MAIN:sh_w], s0t, *rec)
        s_new = jnp.transpose(s_new, (3, 0, 1, 2))
        conv = state_conv[l]
        o_g, h_new = _lru_step(zs, conv.reshape(bs, (CONV_W - 1) * LRU_W), state_lru[l], p["lp"], p["wg"])
        xs = _outproj(xs, o_r, o_g, zs, p["w_r"], p["w_g"], p["w_o"], fg, bs, last)
        z_lx = zs[:, 4 * RWKV_W:4 * RWKV_W + LRU_W]
        outs[4].append(jnp.concatenate([zs[:, :SHIFT_MAIN], zls], axis=-1))
        outs[5].append(s_new)
        outs[6].append(jnp.concatenate([conv[:, 1:], z_lx[:, None, :]], axis=1))
        outs[7].append(h_new)

    return (xp.reshape(bp, seq, d), xs.reshape(bs, 1, d)) + tuple(jnp.stack(o) for o in outs)
```

```python
import functools

import jax
import jax.numpy as jnp
from jax import lax
from jax.experimental import pallas as pl
from jax.experimental.pallas import tpu as pltpu

F32 = jnp.float32
BF16 = jnp.bfloat16

HEADS = 16
HEAD = 64
RWKV_W = HEADS * HEAD
LORA = 64
LRU_W = 1024
LRU_BLOCKS = 16
CONV_W = 4
LRU_C = 8.0
RMS_EPS = 1e-6
GN_EPS = 1e-5 * HEAD
DECAY_SCALE = 0.6065306597126334

LANES = 128
SUBLANES = 8
WKV_CHUNK = 64
VMEM_LIMIT = 56 * 1024 * 1024

NN = (((1,), (0,)), ((), ()))
NT = (((1,), (1,)), ((), ()))
TN = (((0,), (0,)), ((), ()))


def _bf(x):
    return x.astype(BF16)


def _dg(a, b, dn):
    return lax.dot_general(a, b, dn, preferred_element_type=F32)


def _softplus(x):
    return jnp.maximum(x, 0.0) + jnp.log1p(jnp.exp(-jnp.abs(x)))


def _sigmoid(x):
    return 1.0 / (1.0 + jnp.exp(-x))


def _segsum(x, e):
    rows, n = x.shape[0], x.shape[1] // LANES
    stacked = jnp.concatenate([x[:, LANES * j:LANES * (j + 1)] for j in range(n)], axis=0)
    s = _dg(_bf(stacked), e, NN)
    return jnp.concatenate([s[rows * j:rows * (j + 1), :] for j in range(n)], axis=1)


def _rms(x, g):
    return x * lax.rsqrt(jnp.mean(x * x, axis=-1, keepdims=True) + RMS_EPS) * g


SHIFT_MAIN = 3 * RWKV_W
LORA_COL = 10 * RWKV_W
LORA_BLOCK = LORA_COL // LANES
INPROJ_TN = 1024


def _repack_kernel(w_ref, o_ref):
    n = w_ref.shape[-1]
    lora_end = SHIFT_MAIN + 2 * LORA
    o_ref[:, 0:SHIFT_MAIN] = _bf(w_ref[:, 0:SHIFT_MAIN])
    o_ref[:, SHIFT_MAIN:LORA_COL] = _bf(w_ref[:, lora_end:n])
    o_ref[:, LORA_COL:n] = _bf(w_ref[:, SHIFT_MAIN:lora_end])


def _repack_w_in(w_in, layer, tr):
    _, d, n = w_in.shape
    assert n == LORA_COL + 2 * LORA
    return pl.pallas_call(
        _repack_kernel,
        out_shape=jax.ShapeDtypeStruct((d, n), BF16),
        grid=(d // tr,),
        in_specs=[pl.BlockSpec((None, tr, n), lambda i: (layer, i, 0))],
        out_specs=pl.BlockSpec((tr, n), lambda i: (i, 0)),
        compiler_params=pltpu.CompilerParams(
            dimension_semantics=("arbitrary",), vmem_limit_bytes=VMEM_LIMIT),
        name="repack_w_in",
    )(w_in)


def _inproj_kernel(x_ref, xs_ref, g_ref, w_ref, wl_ref, z_ref, zl_ref, zs_ref, zls_ref, h_ref, hs_ref):
    i = pl.program_id(0)
    j = pl.program_id(1)

    @pl.when((i == 0) & (j == 0))
    def _():
        hs = _bf(_rms(xs_ref[...], g_ref[...]))
        hs_ref[...] = hs
        zls_ref[...] = _dg(hs, wl_ref[...], NN)

    @pl.when(j == 0)
    def _():
        h = _bf(_rms(x_ref[...], g_ref[...]))
        h_ref[...] = h
        zl_ref[...] = _dg(h, wl_ref[...], NN)

    z_ref[...] = _dg(h_ref[...], w_ref[...], NN)

    @pl.when(i == 0)
    def _():
        zs_ref[...] = _dg(hs_ref[...], w_ref[...], NN)


def _inproj(x, xs, g, w, tm, tn):
    m, d = x.shape
    ms = xs.shape[0]
    nj = LORA_COL // tn
    park = lambda i, j: jnp.where(i == 0, j, nj - 1)
    return pl.pallas_call(
        _inproj_kernel,
        out_shape=(jax.ShapeDtypeStruct((m, LORA_COL), F32), jax.ShapeDtypeStruct((m, LANES), F32),
                   jax.ShapeDtypeStruct((ms, LORA_COL), F32), jax.ShapeDtypeStruct((ms, LANES), F32)),
        grid=(m // tm, nj),
        in_specs=[
            pl.BlockSpec((tm, d), lambda i, j: (i, 0)),
            pl.BlockSpec((ms, d), lambda i, j: (0, 0), pipeline_mode=pl.Buffered(1)),
            pl.BlockSpec((1, d), lambda i, j: (0, 0), pipeline_mode=pl.Buffered(1)),
            pl.BlockSpec((d, tn), lambda i, j: (0, j)),
            pl.BlockSpec((d, LANES), lambda i, j: (0, LORA_BLOCK), pipeline_mode=pl.Buffered(1)),
        ],
        out_specs=(
            pl.BlockSpec((tm, tn), lambda i, j: (i, j)),
            pl.BlockSpec((tm, LANES), lambda i, j: (i, 0)),
            pl.BlockSpec((ms, tn), lambda i, j: (0, park(i, j))),
            pl.BlockSpec((ms, LANES), lambda i, j: (0, 0)),
        ),
        scratch_shapes=[pltpu.VMEM((tm, d), BF16), pltpu.VMEM((ms, d), BF16)],
        compiler_params=pltpu.CompilerParams(
            dimension_semantics=("arbitrary", "arbitrary"), vmem_limit_bytes=VMEM_LIMIT),
        name="inproj",
    )(x, xs, g, w, w)


_MU_R, _MU_K, _MU_V, _W0, _A0, _KK, _KA, _RK, _LNG, _LNB = range(10)


def _prow(pv_ref, i):
    return pv_ref[i:i + 1, :]


def _wkv_prep(zr, zk, zv, zl, pr, pk, pv, pl_, pv_ref, mul_ref, wd_ref, wa_ref, e):
    r = zr + _prow(pv_ref, _MU_R) * (pr - zr)
    k = zk + _prow(pv_ref, _MU_K) * (pk - zk)
    v = zv + _prow(pv_ref, _MU_V) * (pv - zv)
    lo = zl + mul_ref[0:1, :] * (pl_ - zl)
    lw = _dg(_bf(jnp.tanh(lo)), wd_ref[...], NN)
    la = _dg(_bf(lo), wa_ref[...], NN)
    logd = -DECAY_SCALE * _sigmoid(_prow(pv_ref, _W0) + lw)
    a = _sigmoid(_prow(pv_ref, _A0) + la)
    kk = k * _prow(pv_ref, _KK)
    kk = kk * lax.rsqrt(jnp.maximum(_segsum(kk * kk, e), 1e-24))
    k2 = k * (1.0 + (a - 1.0) * _prow(pv_ref, _KA))
    return r, k2, v, -kk, kk * a, logd


def _wkv_bonus_gate(r, k2, v, zrg, pv_ref, e):
    return _segsum(r * k2 * _prow(pv_ref, _RK), e) * v, zrg * _sigmoid(zrg)


def _wkv_norm_gate(y, bonus_v, gate, pv_ref, e):
    mu = _segsum(y, e) * (1.0 / HEAD)
    yc = y - mu
    var = _segsum(yc * yc, e) * (1.0 / HEAD)
    yn = yc * lax.rsqrt(var + GN_EPS) * _prow(pv_ref, _LNG) + _prow(pv_ref, _LNB)
    return _bf((yn + bonus_v) * gate)


def _wkv_post(y, r, k2, v, zrg, pv_ref, e):
    bonus_v, gate = _wkv_bonus_gate(r, k2, v, zrg, pv_ref, e)
    return _wkv_norm_gate(y, bonus_v, gate, pv_ref, e)


def _wkv_chunk_kernel(zr_ref, zk_ref, zv_ref, zrg_ref, zl_ref, pv_ref, mul_ref, wd_ref, wa_ref,
                      e_ref, o_ref, sout_ref, s_s, prev_s, prevl_s):
    c = pl.program_id(1)
    nc = pl.num_programs(1)
    C = WKV_CHUNK
    assert C == HEAD and 2 * HEAD == LANES
    nb = zr_ref.shape[0]
    rows_all = nb * C
    seqs = range(nb)

    @pl.when(c == 0)
    def _():
        s_s[...] = jnp.zeros_like(s_s)
        prev_s[...] = jnp.zeros_like(prev_s)
        prevl_s[...] = jnp.zeros_like(prevl_s)

    first = lax.broadcasted_iota(jnp.int32, (SUBLANES, 1), 0) == 0

    def shifted(z, prev_ref, lanes):
        rolled = pltpu.roll(z, 1, 0)
        pieces = []
        for b in seqs:
            head = jnp.where(first, prev_ref[b, 0:1, lanes], rolled[b * C:b * C + SUBLANES, :])
            pieces += [head, rolled[b * C + SUBLANES:(b + 1) * C, :]]
        return jnp.concatenate(pieces, axis=0)

    def flat(ref):
        return ref[...].reshape(rows_all, ref.shape[-1])

    zr, zk, zv, zl = flat(zr_ref), flat(zk_ref), flat(zv_ref), flat(zl_ref)
    seg = [slice(RWKV_W * i, RWKV_W * (i + 1)) for i in range(3)]
    pr = shifted(zr, prev_s, seg[0])
    pk = shifted(zk, prev_s, seg[1])
    pv = shifted(zv, prev_s, seg[2])
    pl_ = shifted(zl, prevl_s, slice(0, LANES))
    for b in seqs:
        last = slice(b * C + C - 1, b * C + C)
        prev_s[b, 0:1, seg[0]] = zr[last, :]
        prev_s[b, 0:1, seg[1]] = zk[last, :]
        prev_s[b, 0:1, seg[2]] = zv[last, :]
        prevl_s[b, 0:1, :] = zl[last, :]

    e = e_ref[...]
    r, k2, v, av, bv, logd = _wkv_prep(zr, zk, zv, zl, pr, pk, pv, pl_, pv_ref, mul_ref,
                                       wd_ref, wa_ref, e)

    ti = lax.broadcasted_iota(jnp.int32, (rows_all, rows_all), 0)
    tj = lax.broadcasted_iota(jnp.int32, (rows_all, rows_all), 1)
    tri = jnp.where((ti >= tj) & ((ti & -C) == (tj & -C)), 1.0, 0.0).astype(BF16)
    d_hi = _bf(logd)
    d_r1 = logd - d_hi.astype(F32)
    d_mid = _bf(d_r1)
    d_lo = _bf(d_r1 - d_mid.astype(F32))
    cum = _dg(tri, d_hi, NN) + (_dg(tri, d_mid, NN) + _dg(tri, d_lo, NN))
    e_in = jnp.exp(cum)
    e_neg = jnp.exp(-cum)
    a_t = av * jnp.exp(cum - logd)
    r_t = r * e_in
    k_t = k2 * e_neg
    b_t = bv * e_neg
    p_c = [jnp.exp(cum[b * C + C - 1:b * C + C, :]) for b in seqs]

    lane = lax.broadcasted_iota(jnp.int32, (C, LANES), 1)
    trow = lax.broadcasted_iota(jnp.int32, (C, LANES), 0)
    lo = lane < HEAD
    s_in = lane & (HEAD - 1)
    strict = s_in < trow
    incl2 = ((lax.broadcasted_iota(jnp.int32, (C, 2 * LANES), 1) & (HEAD - 1))
             <= lax.broadcasted_iota(jnp.int32, (C, 2 * LANES), 0))
    eye2 = jnp.where(s_in == trow, 1.0, 0.0).astype(F32)
    vrow = lax.broadcasted_iota(jnp.int32, (2 * HEAD, LANES), 0)
    klane = lax.broadcasted_iota(jnp.int32, (2 * HEAD, LANES), 1)
    same_head = (vrow < HEAD) == (klane < HEAD)

    def bd(x):
        z = jnp.zeros_like(x)
        return jnp.concatenate([jnp.where(lo, x, z), jnp.where(lo, z, x)], axis=0)

    npair = HEADS // 2
    units = [(b, p) for b in seqs for p in range(npair)]
    un = range(len(units))
    blk = lambda arr, i: arr[units[i][0] * C:(units[i][0] + 1) * C, LANES * units[i][1]:LANES * (units[i][1] + 1)]
    ar = [_bf(jnp.concatenate([blk(a_t, i), blk(r_t, i)], axis=0)) for i in un]
    bk = [_bf(jnp.concatenate([bd(blk(b_t, i)), bd(blk(k_t, i))], axis=0)) for i in un]
    g = [_dg(ar[i], bk[i], NT) for i in un]
    s0 = [s_s[i] for i in un]
    ars = [_dg(ar[i], _bf(s0[i]), NT) for i in un]
    vbd = [_bf(bd(blk(v, i))) for i in un]
    x = [jnp.where(strict, g[i][0:C, 0:LANES], 0.0) for i in un]
    ak = [jnp.where(strict, g[i][0:C, LANES:2 * LANES], 0.0) for i in un]
    w = [ars[i][0:C, :] + _dg(_bf(ak[i]), vbd[i], NN) for i in un]
    t = [eye2 + x[i] for i in un]
    x = [_dg(_bf(x[i]), _bf(bd(x[i])), NN) for i in un]
    for _ in range(C.bit_length() - 3):
        xt = [_dg(_bf(jnp.concatenate([x[i], t[i]], axis=0)), _bf(bd(x[i])), NN) for i in un]
        x = [xt[i][0:C, :] for i in un]
        t = [t[i] + xt[i][C:2 * C, :] for i in un]
    t = [t[i] + _dg(_bf(t[i]), _bf(bd(x[i])), NN) for i in un]
    u = [_dg(_bf(t[i]), _bf(bd(w[i])), NN) for i in un]
    rbk = [_bf(jnp.where(incl2, g[i][C:2 * C, :], 0.0)) for i in un]
    uvbd = [jnp.concatenate([_bf(bd(u[i])), vbd[i]], axis=0) for i in un]
    y = [ars[i][C:2 * C, :] + _dg(rbk[i], uvbd[i], NN) for i in un]
    uv = [_bf(jnp.concatenate([u[i], blk(v, i)], axis=0)) for i in un]
    pc = [p_c[units[i][0]][:, LANES * units[i][1]:LANES * (units[i][1] + 1)] for i in un]
    bkh = [_bf(jnp.concatenate([blk(b_t, i), blk(k_t, i)], axis=0) * pc[i]) for i in un]
    s1 = [s0[i] * pc[i] + jnp.where(same_head, _dg(uv[i], bkh[i], TN), 0.0) for i in un]
    for i in un:
        s_s[i] = s1[i]

    y_all = jnp.concatenate(
        [jnp.concatenate(y[b * npair:(b + 1) * npair], axis=1) for b in seqs], axis=0)
    o = _wkv_post(y_all, r, k2, v, flat(zrg_ref), pv_ref, e)
    o_ref[...] = o.reshape(nb, C, RWKV_W)

    @pl.when(c == nc - 1)
    def _():
        for i in un:
            b, p = units[i]
            sout_ref[b, 2 * p] = s1[i][0:HEAD, 0:HEAD]
            sout_ref[b, 2 * p + 1] = s1[i][HEAD:2 * HEAD, HEAD:2 * HEAD]


def _wkv_chunk(z, zl, pvec, mul, wd, wa, e, batch, seq, nb):
    C = WKV_CHUNK
    nc = seq // C
    full = lambda shp: pl.BlockSpec(shp, lambda b, c: (0,) * len(shp))
    col = lambda j: pl.BlockSpec((nb, C, RWKV_W), lambda b, c, j=j: (b, c, j))
    return pl.pallas_call(
        _wkv_chunk_kernel,
        out_shape=(jax.ShapeDtypeStruct((batch, seq, RWKV_W), BF16),
                   jax.ShapeDtypeStruct((batch, HEADS, HEAD, HEAD), F32)),
        grid=(batch // nb, nc),
        in_specs=[col(0), col(1), col(2), col(3),
                  pl.BlockSpec((nb, C, LANES), lambda b, c: (b, c, 0)),
                  full(pvec.shape), full(mul.shape), full(wd.shape), full(wa.shape), full(e.shape)],
        out_specs=(pl.BlockSpec((nb, C, RWKV_W), lambda b, c: (b, c, 0)),
                   pl.BlockSpec((nb, HEADS, HEAD, HEAD), lambda b, c: (b, 0, 0, 0))),
        scratch_shapes=[pltpu.VMEM((nb * HEADS // 2, 2 * HEAD, 2 * HEAD), F32),
                        pltpu.VMEM((nb, SUBLANES, 3 * RWKV_W), F32),
                        pltpu.VMEM((nb, SUBLANES, LANES), F32)],
        compiler_params=pltpu.CompilerParams(
            dimension_semantics=("arbitrary", "arbitrary"), vmem_limit_bytes=VMEM_LIMIT),
        name="wkv_chunk",
    )(z, z, z, z, zl, pvec, mul, wd, wa, e)


def _wkv_step_kernel(zr_ref, zk_ref, zv_ref, zrg_ref, zl_ref, sh_ref, s0_ref, pv_ref,
                     mul_ref, wd_ref, wa_ref, e_ref, o_ref, sout_ref, nsh_ref,
                     at_s, drt_s, bt_s, kt_s, dt_s, vt_s, brt_s, krt_s, yt_s, keep_s):
    h = pl.program_id(0)
    nh = pl.num_programs(0)
    nseq = zr_ref.shape[0]

    @pl.when(h == 0)
    def _():
        e = e_ref[...]
        r, k2, v, av, bv, logd = _wkv_prep(
            zr_ref[...], zk_ref[...], zv_ref[...], zl_ref[...],
            sh_ref[:, 0:RWKV_W], sh_ref[:, RWKV_W:2 * RWKV_W], sh_ref[:, 2 * RWKV_W:3 * RWKV_W],
            sh_ref[:, SHIFT_MAIN:SHIFT_MAIN + LANES], pv_ref, mul_ref, wd_ref, wa_ref, e)
        nsh_ref[:, 0:RWKV_W] = zr_ref[...]
        nsh_ref[:, RWKV_W:2 * RWKV_W] = zk_ref[...]
        nsh_ref[:, 2 * RWKV_W:SHIFT_MAIN] = zv_ref[...]
        nsh_ref[:, SHIFT_MAIN:SHIFT_MAIN + LANES] = zl_ref[...]
        d = jnp.exp(logd)
        at_s[...] = av.T
        drt_s[...] = (d * r).T
        bt_s[...] = bv.T
        kt_s[...] = k2.T
        dt_s[...] = d.T
        vt_s[...] = v.T
        brt_s[...] = jnp.sum((bv * r).T.reshape(HEADS, HEAD, nseq), axis=1)
        krt_s[...] = jnp.sum((k2 * r).T.reshape(HEADS, HEAD, nseq), axis=1)
        keep_s[0] = r
        keep_s[1] = k2
        keep_s[2] = v

    base = pl.multiple_of(h * HEAD, HEAD)
    rows = pl.ds(base, HEAD)
    a_h, dr_h, b_h, k_h, d_h = at_s[rows, :], drt_s[rows, :], bt_s[rows, :], kt_s[rows, :], dt_s[rows, :]
    br_h = brt_s[pl.ds(h, 1), :]
    kr_h = krt_s[pl.ds(h, 1), :]

    def value_rows(g, carry):
        off = pl.multiple_of(base + g * SUBLANES, SUBLANES)
        v8 = vt_s[pl.ds(off, SUBLANES), :]
        ys = []
        for j in range(SUBLANES):
            vi = g * SUBLANES + j
            s_v = s0_ref[0, vi]
            sa = jnp.sum(s_v * a_h, axis=0, keepdims=True)
            y0 = jnp.sum(s_v * dr_h, axis=0, keepdims=True)
            v_v = v8[j:j + 1, :]
            sout_ref[0, vi] = s_v * d_h + sa * b_h + v_v * k_h
            ys.append(y0 + sa * br_h + v_v * kr_h)
        yt_s[pl.ds(off, SUBLANES), :] = jnp.concatenate(ys, axis=0)
        return carry

    lax.fori_loop(0, HEAD // SUBLANES, value_rows, 0)

    @pl.when(h == nh - 1)
    def _():
        o_ref[...] = _wkv_post(yt_s[...].T, keep_s[0], keep_s[1], keep_s[2], zrg_ref[...], pv_ref,
                               e_ref[...])


def _wkv_step(z, zl, sh, s0t, pvec, mul, wd, wa, e):
    nseq = z.shape[0]
    full = lambda shp: pl.BlockSpec(shp, lambda i: (0,) * len(shp))
    col = lambda j: pl.BlockSpec((nseq, RWKV_W), lambda i, j=j: (0, j))
    st_block = (1, HEAD, HEAD, nseq)
    wide = pltpu.VMEM((RWKV_W, nseq), F32)
    return pl.pallas_call(
        _wkv_step_kernel,
        out_shape=(jax.ShapeDtypeStruct((nseq, RWKV_W), BF16),
                   jax.ShapeDtypeStruct(s0t.shape, F32),
                   jax.ShapeDtypeStruct(sh.shape, F32)),
        grid=(HEADS,),
        in_specs=[col(0), col(1), col(2), col(3),
                  full(zl.shape), full(sh.shape),
                  pl.BlockSpec(st_block, lambda i: (i, 0, 0, 0)),
                  full(pvec.shape), full(mul.shape), full(wd.shape), full(wa.shape), full(e.shape)],
        out_specs=(full((nseq, RWKV_W)),
                   pl.BlockSpec(st_block, lambda i: (i, 0, 0, 0)),
                   full(sh.shape)),
        scratch_shapes=[wide] * 6 + [pltpu.VMEM((HEADS, nseq), F32)] * 2
                       + [wide, pltpu.VMEM((3, nseq, RWKV_W), F32)],
        compiler_params=pltpu.CompilerParams(
            dimension_semantics=("arbitrary",), vmem_limit_bytes=VMEM_LIMIT),
        name="wkv_step",
    )(z, z, z, z, zl, sh, s0t, pvec, mul, wd, wa, e)


_CW0, _CW1, _CW2, _CW3, _CB, _GXB, _GAB, _LAM = range(8)


def _lru_gates(xc, lp_ref, wg_ref):
    xb = _bf(xc)
    ngroups = wg_ref.shape[0]
    gs = [_dg(xb[:, LANES * g:LANES * (g + 1)], wg_ref[g], NN) for g in range(ngroups)]
    gx_pre = jnp.concatenate([gs[g][:, 0:LANES] for g in range(ngroups)], axis=1)
    ga_pre = jnp.concatenate([gs[g][:, LANES:2 * LANES] for g in range(ngroups)], axis=1)
    gx = _sigmoid(gx_pre + _prow(lp_ref, _GXB))
    ga = _sigmoid(ga_pre + _prow(lp_ref, _GAB))
    log_a = -LRU_C * ga * _softplus(-_prow(lp_ref, _LAM))
    a = jnp.exp(log_a)
    mult = jnp.sqrt((1.0 - a) * (1.0 + a))
    return a, mult * gx * xc


def _lru_coeffs(zx, first, xb_s, lp_ref, wgate_ref):
    tl = zx.shape[0]
    xb_s[0:SUBLANES, :] = jnp.where(first, 0.0, xb_s[0:SUBLANES, :])
    xb_s[SUBLANES:SUBLANES + tl, :] = zx
    xc = _prow(lp_ref, _CW3) * zx + _prow(lp_ref, _CB)
    for j in range(1, CONV_W):
        xc = xc + _prow(lp_ref, CONV_W - 1 - j) * xb_s[SUBLANES - j:SUBLANES - j + tl, :]
    xb_s[0:SUBLANES, :] = zx[tl - SUBLANES:tl, :]
    return _lru_gates(xc, lp_ref, wgate_ref)


def _lru_scan_rows(a, b, zg, hc):
    row8 = lax.broadcasted_iota(jnp.int32, (SUBLANES, 1), 0)
    hs = []
    for i in range(a.shape[0] // SUBLANES):
        a8 = a[SUBLANES * i:SUBLANES * (i + 1), :]
        b8 = b[SUBLANES * i:SUBLANES * (i + 1), :]
        for s in (1, 2, 4):
            keep = row8 >= s
            b8 = jnp.where(keep, a8 * pltpu.roll(b8, s, 0) + b8, b8)
            a8 = jnp.where(keep, a8 * pltpu.roll(a8, s, 0), a8)
        hb = b8 + a8 * hc
        hs.append(hb)
        hc = jnp.broadcast_to(hb[SUBLANES - 1:SUBLANES, :], hb.shape)
    return _bf(jnp.concatenate(hs, axis=0) * (zg * _sigmoid(zg))), hc


def _lru_step_kernel(zx_ref, zg_ref, conv_ref, h0_ref, lp_ref, wg_ref, o_ref, hnew_ref, cnew_ref):
    zx = zx_ref[...]
    keep = (CONV_W - 2) * LRU_W
    cnew_ref[:, 0:keep] = conv_ref[:, LRU_W:LRU_W + keep]
    cnew_ref[:, keep:keep + LRU_W] = zx
    xc = _prow(lp_ref, _CW3) * zx + _prow(lp_ref, _CB)
    for j in range(CONV_W - 1):
        xc = xc + _prow(lp_ref, j) * conv_ref[:, LRU_W * j:LRU_W * (j + 1)]
    a, b = _lru_gates(xc, lp_ref, wg_ref)
    h = a * h0_ref[...] + b
    hnew_ref[...] = h
    zg = zg_ref[...]
    o_ref[...] = _bf(h * (zg * _sigmoid(zg)))


def _lru_step(z_main, conv, h0, lp, wg):
    nb = z_main.shape[0]
    full = lambda shp: pl.BlockSpec(shp, lambda i: (0,) * len(shp))
    col = lambda j: pl.BlockSpec((nb, LRU_W), lambda i, j=j: (0, j))
    return pl.pallas_call(
        _lru_step_kernel,
        out_shape=(jax.ShapeDtypeStruct((nb, LRU_W), BF16), jax.ShapeDtypeStruct((nb, LRU_W), F32),
                   jax.ShapeDtypeStruct(conv.shape, F32)),
        grid=(1,),
        in_specs=[col(4), col(5), full(conv.shape), full(h0.shape), full(lp.shape), full(wg.shape)],
        out_specs=(full((nb, LRU_W)), full((nb, LRU_W)), full(conv.shape)),
        compiler_params=pltpu.CompilerParams(
            dimension_semantics=("arbitrary",), vmem_limit_bytes=VMEM_LIMIT),
        name="lru_step",
    )(z_main, z_main, conv, h0, lp, wg)


def _outproj_kernel(x_ref, or_ref, og_ref, mr_ref, mg_ref, wr_ref, wg_ref, wo_ref, fg_ref,
                    out_ref, *, final):
    y_r = _dg(or_ref[...], wr_ref[...], NN)
    y_g = _dg(og_ref[...], wg_ref[...], NN)
    merged = _sigmoid(mr_ref[...]) * y_r + _sigmoid(mg_ref[...]) * y_g
    out = x_ref[...] + _dg(_bf(merged), wo_ref[...], NN)
    out_ref[...] = _rms(out, fg_ref[...]) if final else out


def _outproj(x, o_r, o_g, z_main, w_r, w_g, w_o, fg, tm, final):
    m, d = x.shape
    const = lambda shp: pl.BlockSpec(shp, lambda i: (0,) * len(shp), pipeline_mode=pl.Buffered(1))
    return pl.pallas_call(
        functools.partial(_outproj_kernel, final=final),
        out_shape=jax.ShapeDtypeStruct((m, d), F32),
        grid=(m // tm,),
        in_specs=[
            pl.BlockSpec((tm, d), lambda i: (i, 0)),
            pl.BlockSpec((tm, RWKV_W), lambda i: (i, 0)),
            pl.BlockSpec((tm, LRU_W), lambda i: (i, 0)),
            pl.BlockSpec((tm, d), lambda i: (i, 3)),
            pl.BlockSpec((tm, d), lambda i: (i, 4)),
            const(w_r.shape), const(w_g.shape), const(w_o.shape), const(fg.shape),
        ],
        out_specs=pl.BlockSpec((tm, d), lambda i: (i, 0)),
        compiler_params=pltpu.CompilerParams(
            dimension_semantics=("arbitrary",), vmem_limit_bytes=VMEM_LIMIT),
        name="outproj",
    )(x, o_r, o_g, z_main, z_main, w_r, w_g, w_o, fg)


def _outproj_lru_kernel(x_ref, or_ref, mr_ref, mg_ref, zx_ref, zg_ref, lp_ref, wgate_ref, wr_ref,
                        wg_ref, wo_ref, fg_ref, out_ref, hlast_ref, og_s, xb_s, hc_s, *, final,
                        tiles_per_seq):
    i = pl.program_id(0)
    n = pl.num_programs(0) - 1

    @pl.when(i == 0)
    def _():
        og_s[...] = jnp.zeros_like(og_s)
        xb_s[...] = jnp.zeros_like(xb_s)
        hc_s[...] = jnp.zeros_like(hc_s)

    og_prev = og_s[...]
    t = lax.rem(jnp.minimum(i, n - 1), tiles_per_seq)
    a, b = _lru_coeffs(zx_ref[...], t == 0, xb_s, lp_ref, wgate_ref)
    y_r = _dg(or_ref[...], wr_ref[...], NN)
    y_g = _dg(og_prev, wg_ref[...], NN)
    merged = _sigmoid(mr_ref[...]) * y_r + _sigmoid(mg_ref[...]) * y_g
    out = x_ref[...] + _dg(_bf(merged), wo_ref[...], NN)
    o_g, hc = _lru_scan_rows(a, b, zg_ref[...], jnp.where(t == 0, 0.0, hc_s[...]))
    hc_s[...] = hc
    og_s[...] = o_g
    out_ref[...] = _rms(out, fg_ref[...]) if final else out

    @pl.when((t == tiles_per_seq - 1) & (i < n))
    def _():
        hlast_ref[0] = hc[0:1, :]


def _outproj_lru(x, o_r, z, lp, wgate, w_r, w_g, w_o, fg, tm, seq, final):
    m, d = x.shape
    n = m // tm
    tiles_per_seq = seq // tm
    const = lambda shp: pl.BlockSpec(shp, lambda i: (0,) * len(shp), pipeline_mode=pl.Buffered(1))
    prev = lambda i: jnp.maximum(i - 1, 0)
    here = lambda i: jnp.minimum(i, n - 1)
    return pl.pallas_call(
        functools.partial(_outproj_lru_kernel, final=final, tiles_per_seq=tiles_per_seq),
        out_shape=(jax.ShapeDtypeStruct((m, d), F32),
                   jax.ShapeDtypeStruct((m // seq, 1, LRU_W), F32)),
        grid=(n + 1,),
        in_specs=[
            pl.BlockSpec((tm, d), lambda i: (prev(i), 0)),
            pl.BlockSpec((tm, RWKV_W), lambda i: (prev(i), 0)),
            pl.BlockSpec((tm, d), lambda i: (prev(i), 3)),
            pl.BlockSpec((tm, d), lambda i: (prev(i), 4)),
            pl.BlockSpec((tm, LRU_W), lambda i: (here(i), 4)),
            pl.BlockSpec((tm, LRU_W), lambda i: (here(i), 5)),
            const(lp.shape), const(wgate.shape), const(w_r.shape), const(w_g.shape), const(w_o.shape),
            const(fg.shape),
        ],
        out_specs=(pl.BlockSpec((tm, d), lambda i: (prev(i), 0)),
                   pl.BlockSpec((1, 1, LRU_W), lambda i: (here(i) // tiles_per_seq, 0, 0))),
        scratch_shapes=[pltpu.VMEM((tm, LRU_W), BF16),
                        pltpu.VMEM((SUBLANES + tm, LRU_W), F32),
                        pltpu.VMEM((SUBLANES, LRU_W), F32)],
        compiler_params=pltpu.CompilerParams(
            dimension_semantics=("arbitrary",), vmem_limit_bytes=VMEM_LIMIT),
        name="outproj_lru",
    )(x, o_r, z, z, z, z, lp, wgate, w_r, w_g, w_o, fg)


def _row_tile(m, want):
    t = min(m, want)
    assert m % t == 0, (m, t)
    return t


def _pack_params_kernel(mu_ref, w0_ref, a0_ref, kk_ref, ka_ref, rk_ref, lng_ref, lnb_ref, wdu_ref,
                        wau_ref, cw_ref, cb_ref, gxb_ref, gab_ref, lam_ref, gxw_ref, gaw_ref,
                        pvec_ref, mul_ref, wd_ref, wa_ref, e_ref, lp_ref, wg_ref):
    pvec_ref[...] = jnp.zeros_like(pvec_ref)
    for i in range(3):
        pvec_ref[_MU_R + i:_MU_R + i + 1, :] = mu_ref[:, RWKV_W * i:RWKV_W * (i + 1)]
    for row, ref in ((_W0, w0_ref), (_A0, a0_ref), (_KK, kk_ref), (_KA, ka_ref), (_RK, rk_ref),
                     (_LNG, lng_ref), (_LNB, lnb_ref)):
        pvec_ref[row:row + 1, :] = ref[...]
    mul_ref[...] = jnp.broadcast_to(mu_ref[:, 3 * RWKV_W:3 * RWKV_W + 2 * LORA], mul_ref.shape)

    zeros = jnp.zeros((LORA, RWKV_W), BF16)
    wd_ref[0:LORA, :] = _bf(wdu_ref[...])
    wd_ref[LORA:2 * LORA, :] = zeros
    wa_ref[0:LORA, :] = zeros
    wa_ref[LORA:2 * LORA, :] = _bf(wau_ref[...])

    ri = lax.broadcasted_iota(jnp.int32, (LANES, LANES), 0)
    ci = lax.broadcasted_iota(jnp.int32, (LANES, LANES), 1)
    e_ref[...] = jnp.where((ri < HEAD) == (ci < HEAD), 1.0, 0.0).astype(BF16)

    lp_ref[_CW0:_CW0 + CONV_W, :] = cw_ref[...]
    for row, ref in ((_CB, cb_ref), (_GXB, gxb_ref), (_GAB, gab_ref), (_LAM, lam_ref)):
        lp_ref[row:row + 1, :] = ref[...]

    blk = LRU_W // LRU_BLOCKS
    z = jnp.zeros((blk, blk), F32)
    for g in range(LRU_BLOCKS // 2):
        top = jnp.concatenate([gxw_ref[2 * g], z, gaw_ref[2 * g], z], axis=1)
        bot = jnp.concatenate([z, gxw_ref[2 * g + 1], z, gaw_ref[2 * g + 1]], axis=1)
        wg_ref[g] = _bf(jnp.concatenate([top, bot], axis=0))


def _pack_params(l, rwkv_mu, w_decay0, w_decay_up, w_iclr0, w_iclr_up, k_k, k_a, r_k, ln_x_g,
                 ln_x_b, conv_w, conv_b, lru_gx_w, lru_gx_b, lru_ga_w, lru_ga_b, lru_lambda):
    blk = LRU_W // LRU_BLOCKS
    assert 2 * blk == LANES and 2 * LORA == LANES
    depth = rwkv_mu.shape[0]
    row = lambda a: pl.BlockSpec((1, a.shape[-1]), lambda i: (l, 0))
    mat = lambda a: pl.BlockSpec((None,) + a.shape[1:], lambda i: (l,) + (0,) * (a.ndim - 1))
    full = lambda shp: pl.BlockSpec(shp, lambda i: (0,) * len(shp))
    rk = r_k.reshape(depth, RWKV_W)
    rows = (rwkv_mu, w_decay0, w_iclr0, k_k, k_a, rk, ln_x_g, ln_x_b)
    out_shapes = ((16, RWKV_W, F32), (SUBLANES, LANES, F32), (LANES, RWKV_W, BF16), (LANES, RWKV_W, BF16),
                  (LANES, LANES, BF16), (SUBLANES, LRU_W, F32))
    outs = tuple(jax.ShapeDtypeStruct(s[:2], s[2]) for s in out_shapes)
    outs += (jax.ShapeDtypeStruct((LRU_BLOCKS // 2, LANES, 2 * LANES), BF16),)
    return pl.pallas_call(
        _pack_params_kernel,
        out_shape=outs,
        grid=(1,),
        in_specs=[row(a) for a in rows] + [mat(w_decay_up), mat(w_iclr_up), mat(conv_w), row(conv_b),
                                          row(lru_gx_b), row(lru_ga_b), row(lru_lambda),
                                          mat(lru_gx_w), mat(lru_ga_w)],
        out_specs=tuple(full(o.shape) for o in outs),
        compiler_params=pltpu.CompilerParams(
            dimension_semantics=("arbitrary",), vmem_limit_bytes=VMEM_LIMIT),
        name="pack_params",
    )(*rows, w_decay_up, w_iclr_up, conv_w, conv_b, lru_gx_b, lru_ga_b, lru_lambda, lru_gx_w, lru_ga_w)


def kernel(x_prompt, x_sample, state_shift, state_wkv, state_conv, state_lru, norm_g, w_in, rwkv_mu,
           w_decay0, w_decay_up, w_iclr0, w_iclr_up, k_k, k_a, r_k, ln_x_g, ln_x_b, w_out_rwkv,
           conv_w, conv_b, lru_gx_w, lru_gx_b, lru_ga_w, lru_ga_b, lru_lambda, w_out_lru, w_out,
           final_norm_g):
    bp, seq, d = x_prompt.shape
    bs = x_sample.shape[0]
    assert x_sample.shape[1] == 1 and seq % WKV_CHUNK == 0
    depth = w_in.shape[0]
    xp = x_prompt.reshape(bp * seq, d)
    xs = x_sample.reshape(bs, d)
    fg = final_norm_g.reshape(1, d)
    outs = [[] for _ in range(8)]
    for l in range(depth):
        pvec, mul, wd, wa, e, lp, wg = _pack_params(
            l, rwkv_mu, w_decay0, w_decay_up, w_iclr0, w_iclr_up, k_k, k_a, r_k, ln_x_g, ln_x_b,
            conv_w, conv_b, lru_gx_w, lru_gx_b, lru_ga_w, lru_ga_b, lru_lambda)
        w_r, w_g, w_o = w_out_rwkv[l].astype(BF16), w_out_lru[l].astype(BF16), w_out[l].astype(BF16)
        g = norm_g[l].reshape(1, d)
        rec = (pvec, mul, wd, wa, e)
        w = _repack_w_in(w_in, l, _row_tile(d, 256))
        zp, zlp, zs, zls = _inproj(xp, xs, g, w, _row_tile(bp * seq, 1024), INPROJ_TN)

        zp3 = zp.reshape(bp, seq, -1)
        zlp3 = zlp.reshape(bp, seq, LANES)
        nb = max(n for n in (4, 2, 1) if bp % n == 0)
        o_r, s_new = _wkv_chunk(zp3, zlp3, *rec, bp, seq, nb)
        o_r = o_r.reshape(bp * seq, RWKV_W)
        last = l == depth - 1
        xp, h_last = _outproj_lru(xp, o_r, zp, lp, wg, w_r, w_g, w_o, fg,
                                  _row_tile(seq, 256), seq, last)
        outs[0].append(jnp.concatenate([zp3[:, -1, :SHIFT_MAIN], zlp3[:, -1]], axis=-1))
        outs[1].append(s_new)
        outs[2].append(zp3[:, seq - (CONV_W - 1):, 4 * RWKV_W:4 * RWKV_W + LRU_W])
        outs[3].append(h_last.reshape(bp, LRU_W))

        s0t = jnp.transpose(state_wkv[l], (1, 2, 3, 0))
        o_r, s_new, sh_new = _wkv_step(zs, zls, state_shift[l], s0t, *rec)
        s_new = jnp.transpose(s_new, (3, 0, 1, 2))
        conv = state_conv[l].reshape(bs, (CONV_W - 1) * LRU_W)
        o_g, h_new, conv_new = _lru_step(zs, conv, state_lru[l], lp, wg)
        xs = _outproj(xs, o_r, o_g, zs, w_r, w_g, w_o, fg, bs, last)
        outs[4].append(sh_new)
        outs[5].append(s_new)
        outs[6].append(conv_new.reshape(bs, CONV_W - 1, LRU_W))
        outs[7].append(h_new)

    return (xp.reshape(bp, seq, d), xs.reshape(bs, 1, d)) + tuple(jnp.stack(o) for o in outs)
```

```python
import functools

import jax
import jax.numpy as jnp
from jax import lax
from jax.experimental import pallas as pl
from jax.experimental.pallas import tpu as pltpu

F32 = jnp.float32
BF16 = jnp.bfloat16

HEADS = 16
HEAD = 64
RWKV_W = HEADS * HEAD
LORA = 64
LRU_W = 1024
LRU_BLOCKS = 16
CONV_W = 4
LRU_C = 8.0
RMS_EPS = 1e-6
GN_EPS = 1e-5 * HEAD
DECAY_SCALE = 0.6065306597126334

LANES = 128
SUBLANES = 8
WKV_CHUNK = 64
VMEM_LIMIT = 56 * 1024 * 1024
INPROJ_VMEM_LIMIT = 60 * 1024 * 1024

NN = (((1,), (0,)), ((), ()))
NT = (((1,), (1,)), ((), ()))
TN = (((0,), (0,)), ((), ()))


def _bf(x):
    return x.astype(BF16)


def _dg(a, b, dn):
    return lax.dot_general(a, b, dn, preferred_element_type=F32)


def _softplus(x):
    return jnp.maximum(x, 0.0) + jnp.log1p(jnp.exp(-jnp.abs(x)))


def _sigmoid(x):
    return 1.0 / (1.0 + jnp.exp(-x))


def _segsum(x, e):
    rows, n = x.shape[0], x.shape[1] // LANES
    stacked = jnp.concatenate([x[:, LANES * j:LANES * (j + 1)] for j in range(n)], axis=0)
    s = _dg(_bf(stacked), e, NN)
    return jnp.concatenate([s[rows * j:rows * (j + 1), :] for j in range(n)], axis=1)


def _rms(x, g):
    return x * lax.rsqrt(jnp.mean(x * x, axis=-1, keepdims=True) + RMS_EPS) * g


SHIFT_MAIN = 3 * RWKV_W
LORA_COL = 10 * RWKV_W
LORA_BLOCK = LORA_COL // LANES
INPROJ_TN = 1024


def _repack_kernel(w_ref, o_ref):
    n = w_ref.shape[-1]
    lora_end = SHIFT_MAIN + 2 * LORA
    o_ref[:, 0:SHIFT_MAIN] = _bf(w_ref[:, 0:SHIFT_MAIN])
    o_ref[:, SHIFT_MAIN:LORA_COL] = _bf(w_ref[:, lora_end:n])
    o_ref[:, LORA_COL:n] = _bf(w_ref[:, SHIFT_MAIN:lora_end])


def _repack_w_in(w_in, layer, tr):
    _, d, n = w_in.shape
    assert n == LORA_COL + 2 * LORA
    return pl.pallas_call(
        _repack_kernel,
        out_shape=jax.ShapeDtypeStruct((d, n), BF16),
        grid=(d // tr,),
        in_specs=[pl.BlockSpec((None, tr, n), lambda i: (layer, i, 0))],
        out_specs=pl.BlockSpec((tr, n), lambda i: (i, 0)),
        compiler_params=pltpu.CompilerParams(
            dimension_semantics=("arbitrary",), vmem_limit_bytes=VMEM_LIMIT),
        name="repack_w_in",
    )(w_in)


def _inproj_kernel(x_ref, xs_ref, g_ref, w_ref, wl_ref, c0_ref, c1_ref, c2_ref,
                   z_ref, zl_ref, zs_ref, zls_ref, b0_ref, b1_ref, b2_ref, h_ref, hs_ref):
    i = pl.program_id(0)
    j = pl.program_id(1)

    @pl.when((i == 0) & (j == 0))
    def _():
        hs = _bf(_rms(xs_ref[...], g_ref[...]))
        hs_ref[...] = hs
        zls_ref[...] = _dg(hs, wl_ref[...], NN)

    @pl.when(j == 0)
    def _():
        h = _bf(_rms(x_ref[...], g_ref[...]))
        h_ref[...] = h
        zl_ref[...] = _dg(h, wl_ref[...], NN)
        for src, dst in ((c0_ref, b0_ref), (c1_ref, b1_ref), (c2_ref, b2_ref)):
            dst[...] = _bf(src[...])

    z_ref[...] = _dg(h_ref[...], w_ref[...], NN)

    @pl.when(i == 0)
    def _():
        zs_ref[...] = _dg(hs_ref[...], w_ref[...], NN)


def _inproj(x, xs, g, w, to_bf16, tm, tn):
    m, d = x.shape
    ms = xs.shape[0]
    nj = LORA_COL // tn
    ni = m // tm
    assert len(to_bf16) == 3 and all(a.shape[0] % (2 * SUBLANES * ni) == 0 for a in to_bf16)
    slab = lambda a: pl.BlockSpec((a.shape[0] // ni, a.shape[1]), lambda i, j: (i, 0))
    park = lambda i, j: jnp.where(i == 0, j, nj - 1)
    return pl.pallas_call(
        _inproj_kernel,
        out_shape=(jax.ShapeDtypeStruct((m, LORA_COL), F32), jax.ShapeDtypeStruct((m, LANES), F32),
                   jax.ShapeDtypeStruct((ms, LORA_COL), F32), jax.ShapeDtypeStruct((ms, LANES), F32))
        + tuple(jax.ShapeDtypeStruct(a.shape, BF16) for a in to_bf16),
        grid=(ni, nj),
        in_specs=[
            pl.BlockSpec((tm, d), lambda i, j: (i, 0)),
            pl.BlockSpec((ms, d), lambda i, j: (0, 0), pipeline_mode=pl.Buffered(1)),
            pl.BlockSpec((1, d), lambda i, j: (0, 0), pipeline_mode=pl.Buffered(1)),
            pl.BlockSpec((d, tn), lambda i, j: (0, j)),
            pl.BlockSpec((d, LANES), lambda i, j: (0, LORA_BLOCK), pipeline_mode=pl.Buffered(1)),
        ] + [slab(a) for a in to_bf16],
        out_specs=(
            pl.BlockSpec((tm, tn), lambda i, j: (i, j)),
            pl.BlockSpec((tm, LANES), lambda i, j: (i, 0)),
            pl.BlockSpec((ms, tn), lambda i, j: (0, park(i, j))),
            pl.BlockSpec((ms, LANES), lambda i, j: (0, 0)),
        ) + tuple(slab(a) for a in to_bf16),
        scratch_shapes=[pltpu.VMEM((tm, d), BF16), pltpu.VMEM((ms, d), BF16)],
        compiler_params=pltpu.CompilerParams(
            dimension_semantics=("arbitrary", "arbitrary"), vmem_limit_bytes=INPROJ_VMEM_LIMIT),
        name="inproj",
    )(x, xs, g, w, w, *to_bf16)


_MU_R, _MU_K, _MU_V, _W0, _A0, _KK, _KA, _RK, _LNG, _LNB = range(10)


def _prow(pv_ref, i):
    return pv_ref[i:i + 1, :]


def _wkv_prep(zr, zk, zv, zl, pr, pk, pv, pl_, pv_ref, mul_ref, wd_ref, wa_ref, e):
    r = zr + _prow(pv_ref, _MU_R) * (pr - zr)
    k = zk + _prow(pv_ref, _MU_K) * (pk - zk)
    v = zv + _prow(pv_ref, _MU_V) * (pv - zv)
    lo = zl + mul_ref[0:1, :] * (pl_ - zl)
    lw = _dg(_bf(jnp.tanh(lo)), wd_ref[...], NN)
    la = _dg(_bf(lo), wa_ref[...], NN)
    logd = -DECAY_SCALE * _sigmoid(_prow(pv_ref, _W0) + lw)
    a = _sigmoid(_prow(pv_ref, _A0) + la)
    kk = k * _prow(pv_ref, _KK)
    kk = kk * lax.rsqrt(jnp.maximum(_segsum(kk * kk, e), 1e-24))
    k2 = k * (1.0 + (a - 1.0) * _prow(pv_ref, _KA))
    return r, k2, v, -kk, kk * a, logd


def _wkv_bonus_gate(r, k2, v, zrg, pv_ref, e):
    return _segsum(r * k2 * _prow(pv_ref, _RK), e) * v, zrg * _sigmoid(zrg)


def _wkv_norm_gate(y, bonus_v, gate, pv_ref, e):
    mu = _segsum(y, e) * (1.0 / HEAD)
    yc = y - mu
    var = _segsum(yc * yc, e) * (1.0 / HEAD)
    yn = yc * lax.rsqrt(var + GN_EPS) * _prow(pv_ref, _LNG) + _prow(pv_ref, _LNB)
    return _bf((yn + bonus_v) * gate)


def _wkv_post(y, r, k2, v, zrg, pv_ref, e):
    bonus_v, gate = _wkv_bonus_gate(r, k2, v, zrg, pv_ref, e)
    return _wkv_norm_gate(y, bonus_v, gate, pv_ref, e)


def _wkv_chunk_kernel(zr_ref, zk_ref, zv_ref, zrg_ref, zl_ref, pv_ref, mul_ref, wd_ref, wa_ref,
                      e_ref, o_ref, sout_ref, s_s, prev_s, prevl_s):
    c = pl.program_id(1)
    nc = pl.num_programs(1)
    C = WKV_CHUNK
    assert C == HEAD and 2 * HEAD == LANES
    nb = zr_ref.shape[0]
    rows_all = nb * C
    seqs = range(nb)

    @pl.when(c == 0)
    def _():
        s_s[...] = jnp.zeros_like(s_s)
        prev_s[...] = jnp.zeros_like(prev_s)
        prevl_s[...] = jnp.zeros_like(prevl_s)

    first = lax.broadcasted_iota(jnp.int32, (SUBLANES, 1), 0) == 0

    def shifted(z, prev_ref, lanes):
        rolled = pltpu.roll(z, 1, 0)
        pieces = []
        for b in seqs:
            head = jnp.where(first, prev_ref[b, 0:1, lanes], rolled[b * C:b * C + SUBLANES, :])
            pieces += [head, rolled[b * C + SUBLANES:(b + 1) * C, :]]
        return jnp.concatenate(pieces, axis=0)

    def flat(ref):
        return ref[...].reshape(rows_all, ref.shape[-1])

    zr, zk, zv, zl = flat(zr_ref), flat(zk_ref), flat(zv_ref), flat(zl_ref)
    seg = [slice(RWKV_W * i, RWKV_W * (i + 1)) for i in range(3)]
    pr = shifted(zr, prev_s, seg[0])
    pk = shifted(zk, prev_s, seg[1])
    pv = shifted(zv, prev_s, seg[2])
    pl_ = shifted(zl, prevl_s, slice(0, LANES))
    for b in seqs:
        last = slice(b * C + C - 1, b * C + C)
        prev_s[b, 0:1, seg[0]] = zr[last, :]
        prev_s[b, 0:1, seg[1]] = zk[last, :]
        prev_s[b, 0:1, seg[2]] = zv[last, :]
        prevl_s[b, 0:1, :] = zl[last, :]

    e = e_ref[...]
    r, k2, v, av, bv, logd = _wkv_prep(zr, zk, zv, zl, pr, pk, pv, pl_, pv_ref, mul_ref,
                                       wd_ref, wa_ref, e)

    ti = lax.broadcasted_iota(jnp.int32, (rows_all, rows_all), 0)
    tj = lax.broadcasted_iota(jnp.int32, (rows_all, rows_all), 1)
    tri = jnp.where((ti >= tj) & ((ti & -C) == (tj & -C)), 1.0, 0.0).astype(BF16)
    d_hi = _bf(logd)
    d_r1 = logd - d_hi.astype(F32)
    d_mid = _bf(d_r1)
    d_lo = _bf(d_r1 - d_mid.astype(F32))
    cum = _dg(tri, d_hi, NN) + (_dg(tri, d_mid, NN) + _dg(tri, d_lo, NN))
    e_in = jnp.exp(cum)
    e_neg = jnp.exp(-cum)
    a_t = av * jnp.exp(cum - logd)
    r_t = r * e_in
    k_t = k2 * e_neg
    b_t = bv * e_neg
    p_c = [jnp.exp(cum[b * C + C - 1:b * C + C, :]) for b in seqs]

    lane = lax.broadcasted_iota(jnp.int32, (C, LANES), 1)
    trow = lax.broadcasted_iota(jnp.int32, (C, LANES), 0)
    lo = lane < HEAD
    s_in = lane & (HEAD - 1)
    strict = s_in < trow
    incl2 = ((lax.broadcasted_iota(jnp.int32, (C, 2 * LANES), 1) & (HEAD - 1))
             <= lax.broadcasted_iota(jnp.int32, (C, 2 * LANES), 0))
    eye2 = jnp.where(s_in == trow, 1.0, 0.0).astype(F32)
    vrow = lax.broadcasted_iota(jnp.int32, (2 * HEAD, LANES), 0)
    klane = lax.broadcasted_iota(jnp.int32, (2 * HEAD, LANES), 1)
    same_head = (vrow < HEAD) == (klane < HEAD)

    def bd(x):
        z = jnp.zeros_like(x)
        return jnp.concatenate([jnp.where(lo, x, z), jnp.where(lo, z, x)], axis=0)

    npair = HEADS // 2
    units = [(b, p) for b in seqs for p in range(npair)]
    un = range(len(units))
    blk = lambda arr, i: arr[units[i][0] * C:(units[i][0] + 1) * C, LANES * units[i][1]:LANES * (units[i][1] + 1)]
    ar = [_bf(jnp.concatenate([blk(a_t, i), blk(r_t, i)], axis=0)) for i in un]
    bk = [_bf(jnp.concatenate([bd(blk(b_t, i)), bd(blk(k_t, i))], axis=0)) for i in un]
    g = [_dg(ar[i], bk[i], NT) for i in un]
    s0 = [s_s[i] for i in un]
    ars = [_dg(ar[i], _bf(s0[i]), NT) for i in un]
    vbd = [_bf(bd(blk(v, i))) for i in un]
    x = [jnp.where(strict, g[i][0:C, 0:LANES], 0.0) for i in un]
    ak = [jnp.where(strict, g[i][0:C, LANES:2 * LANES], 0.0) for i in un]
    w = [ars[i][0:C, :] + _dg(_bf(ak[i]), vbd[i], NN) for i in un]
    t = [eye2 + x[i] for i in un]
    x = [_dg(_bf(x[i]), _bf(bd(x[i])), NN) for i in un]
    for _ in range(C.bit_length() - 3):
        xt = [_dg(_bf(jnp.concatenate([x[i], t[i]], axis=0)), _bf(bd(x[i])), NN) for i in un]
        x = [xt[i][0:C, :] for i in un]
        t = [t[i] + xt[i][C:2 * C, :] for i in un]
    t = [t[i] + _dg(_bf(t[i]), _bf(bd(x[i])), NN) for i in un]
    u = [_dg(_bf(t[i]), _bf(bd(w[i])), NN) for i in un]
    rbk = [_bf(jnp.where(incl2, g[i][C:2 * C, :], 0.0)) for i in un]
    uvbd = [jnp.concatenate([_bf(bd(u[i])), vbd[i]], axis=0) for i in un]
    y = [ars[i][C:2 * C, :] + _dg(rbk[i], uvbd[i], NN) for i in un]
    uv = [_bf(jnp.concatenate([u[i], blk(v, i)], axis=0)) for i in un]
    pc = [p_c[units[i][0]][:, LANES * units[i][1]:LANES * (units[i][1] + 1)] for i in un]
    bkh = [_bf(jnp.concatenate([blk(b_t, i), blk(k_t, i)], axis=0) * pc[i]) for i in un]
    s1 = [s0[i] * pc[i] + jnp.where(same_head, _dg(uv[i], bkh[i], TN), 0.0) for i in un]
    for i in un:
        s_s[i] = s1[i]

    y_all = jnp.concatenate(
        [jnp.concatenate(y[b * npair:(b + 1) * npair], axis=1) for b in seqs], axis=0)
    o = _wkv_post(y_all, r, k2, v, flat(zrg_ref), pv_ref, e)
    o_ref[...] = o.reshape(nb, C, RWKV_W)

    @pl.when(c == nc - 1)
    def _():
        for i in un:
            b, p = units[i]
            sout_ref[b, 2 * p] = s1[i][0:HEAD, 0:HEAD]
            sout_ref[b, 2 * p + 1] = s1[i][HEAD:2 * HEAD, HEAD:2 * HEAD]


def _wkv_chunk(z, zl, pvec, mul, wd, wa, e, batch, seq, nb):
    C = WKV_CHUNK
    nc = seq // C
    full = lambda shp: pl.BlockSpec(shp, lambda b, c: (0,) * len(shp))
    col = lambda j: pl.BlockSpec((nb, C, RWKV_W), lambda b, c, j=j: (b, c, j))
    return pl.pallas_call(
        _wkv_chunk_kernel,
        out_shape=(jax.ShapeDtypeStruct((batch, seq, RWKV_W), BF16),
                   jax.ShapeDtypeStruct((batch, HEADS, HEAD, HEAD), F32)),
        grid=(batch // nb, nc),
        in_specs=[col(0), col(1), col(2), col(3),
                  pl.BlockSpec((nb, C, LANES), lambda b, c: (b, c, 0)),
                  full(pvec.shape), full(mul.shape), full(wd.shape), full(wa.shape), full(e.shape)],
        out_specs=(pl.BlockSpec((nb, C, RWKV_W), lambda b, c: (b, c, 0)),
                   pl.BlockSpec((nb, HEADS, HEAD, HEAD), lambda b, c: (b, 0, 0, 0))),
        scratch_shapes=[pltpu.VMEM((nb * HEADS // 2, 2 * HEAD, 2 * HEAD), F32),
                        pltpu.VMEM((nb, SUBLANES, 3 * RWKV_W), F32),
                        pltpu.VMEM((nb, SUBLANES, LANES), F32)],
        compiler_params=pltpu.CompilerParams(
            dimension_semantics=("arbitrary", "arbitrary"), vmem_limit_bytes=VMEM_LIMIT),
        name="wkv_chunk",
    )(z, z, z, z, zl, pvec, mul, wd, wa, e)


def _wkv_step_kernel(zr_ref, zk_ref, zv_ref, zrg_ref, zl_ref, sh_ref, s0_ref, pv_ref,
                     mul_ref, wd_ref, wa_ref, e_ref, o_ref, sout_ref, nsh_ref,
                     at_s, drt_s, bt_s, kt_s, dt_s, vt_s, brt_s, krt_s, yt_s, keep_s):
    h = pl.program_id(0)
    nh = pl.num_programs(0)
    nseq = zr_ref.shape[0]

    @pl.when(h == 0)
    def _():
        e = e_ref[...]
        r, k2, v, av, bv, logd = _wkv_prep(
            zr_ref[...], zk_ref[...], zv_ref[...], zl_ref[...],
            sh_ref[:, 0:RWKV_W], sh_ref[:, RWKV_W:2 * RWKV_W], sh_ref[:, 2 * RWKV_W:3 * RWKV_W],
            sh_ref[:, SHIFT_MAIN:SHIFT_MAIN + LANES], pv_ref, mul_ref, wd_ref, wa_ref, e)
        nsh_ref[:, 0:RWKV_W] = zr_ref[...]
        nsh_ref[:, RWKV_W:2 * RWKV_W] = zk_ref[...]
        nsh_ref[:, 2 * RWKV_W:SHIFT_MAIN] = zv_ref[...]
        nsh_ref[:, SHIFT_MAIN:SHIFT_MAIN + LANES] = zl_ref[...]
        d = jnp.exp(logd)
        at_s[...] = av.T
        drt_s[...] = (d * r).T
        bt_s[...] = bv.T
        kt_s[...] = k2.T
        dt_s[...] = d.T
        vt_s[...] = v.T
        brt_s[...] = jnp.sum((bv * r).T.reshape(HEADS, HEAD, nseq), axis=1)
        krt_s[...] = jnp.sum((k2 * r).T.reshape(HEADS, HEAD, nseq), axis=1)
        keep_s[0] = r
        keep_s[1] = k2
        keep_s[2] = v

    base = pl.multiple_of(h * HEAD, HEAD)
    rows = pl.ds(base, HEAD)
    a_h, dr_h, b_h, k_h, d_h = at_s[rows, :], drt_s[rows, :], bt_s[rows, :], kt_s[rows, :], dt_s[rows, :]
    br_h = brt_s[pl.ds(h, 1), :]
    kr_h = krt_s[pl.ds(h, 1), :]

    def value_rows(g, carry):
        off = pl.multiple_of(base + g * SUBLANES, SUBLANES)
        v8 = vt_s[pl.ds(off, SUBLANES), :]
        ys = []
        for j in range(SUBLANES):
            vi = g * SUBLANES + j
            s_v = s0_ref[0, vi]
            sa = jnp.sum(s_v * a_h, axis=0, keepdims=True)
            y0 = jnp.sum(s_v * dr_h, axis=0, keepdims=True)
            v_v = v8[j:j + 1, :]
            sout_ref[0, vi] = s_v * d_h + sa * b_h + v_v * k_h
            ys.append(y0 + sa * br_h + v_v * kr_h)
        yt_s[pl.ds(off, SUBLANES), :] = jnp.concatenate(ys, axis=0)
        return carry

    lax.fori_loop(0, HEAD // SUBLANES, value_rows, 0)

    @pl.when(h == nh - 1)
    def _():
        o_ref[...] = _wkv_post(yt_s[...].T, keep_s[0], keep_s[1], keep_s[2], zrg_ref[...], pv_ref,
                               e_ref[...])


def _wkv_step(z, zl, sh, s0t, pvec, mul, wd, wa, e):
    nseq = z.shape[0]
    full = lambda shp: pl.BlockSpec(shp, lambda i: (0,) * len(shp))
    col = lambda j: pl.BlockSpec((nseq, RWKV_W), lambda i, j=j: (0, j))
    st_block = (1, HEAD, HEAD, nseq)
    wide = pltpu.VMEM((RWKV_W, nseq), F32)
    return pl.pallas_call(
        _wkv_step_kernel,
        out_shape=(jax.ShapeDtypeStruct((nseq, RWKV_W), BF16),
                   jax.ShapeDtypeStruct(s0t.shape, F32),
                   jax.ShapeDtypeStruct(sh.shape, F32)),
        grid=(HEADS,),
        in_specs=[col(0), col(1), col(2), col(3),
                  full(zl.shape), full(sh.shape),
                  pl.BlockSpec(st_block, lambda i: (i, 0, 0, 0)),
                  full(pvec.shape), full(mul.shape), full(wd.shape), full(wa.shape), full(e.shape)],
        out_specs=(full((nseq, RWKV_W)),
                   pl.BlockSpec(st_block, lambda i: (i, 0, 0, 0)),
                   full(sh.shape)),
        scratch_shapes=[wide] * 6 + [pltpu.VMEM((HEADS, nseq), F32)] * 2
                       + [wide, pltpu.VMEM((3, nseq, RWKV_W), F32)],
        compiler_params=pltpu.CompilerParams(
            dimension_semantics=("arbitrary",), vmem_limit_bytes=VMEM_LIMIT),
        name="wkv_step",
    )(z, z, z, z, zl, sh, s0t, pvec, mul, wd, wa, e)


_CW0, _CW1, _CW2, _CW3, _CB, _GXB, _GAB, _LAM = range(8)


def _lru_gates(xc, lp_ref, wg_ref):
    xb = _bf(xc)
    ngroups = wg_ref.shape[0]
    gs = [_dg(xb[:, LANES * g:LANES * (g + 1)], wg_ref[g], NN) for g in range(ngroups)]
    gx_pre = jnp.concatenate([gs[g][:, 0:LANES] for g in range(ngroups)], axis=1)
    ga_pre = jnp.concatenate([gs[g][:, LANES:2 * LANES] for g in range(ngroups)], axis=1)
    gx = _sigmoid(gx_pre + _prow(lp_ref, _GXB))
    ga = _sigmoid(ga_pre + _prow(lp_ref, _GAB))
    log_a = -LRU_C * ga * _softplus(-_prow(lp_ref, _LAM))
    a = jnp.exp(log_a)
    mult = jnp.sqrt((1.0 - a) * (1.0 + a))
    return a, mult * gx * xc


def _lru_coeffs(zx, first, xb_s, lp_ref, wgate_ref):
    tl = zx.shape[0]
    xb_s[0:SUBLANES, :] = jnp.where(first, 0.0, xb_s[0:SUBLANES, :])
    xb_s[SUBLANES:SUBLANES + tl, :] = zx
    xc = _prow(lp_ref, _CW3) * zx + _prow(lp_ref, _CB)
    for j in range(1, CONV_W):
        xc = xc + _prow(lp_ref, CONV_W - 1 - j) * xb_s[SUBLANES - j:SUBLANES - j + tl, :]
    xb_s[0:SUBLANES, :] = zx[tl - SUBLANES:tl, :]
    return _lru_gates(xc, lp_ref, wgate_ref)


def _lru_scan_rows(a, b, zg, hc):
    row8 = lax.broadcasted_iota(jnp.int32, (SUBLANES, 1), 0)
    hs = []
    for i in range(a.shape[0] // SUBLANES):
        a8 = a[SUBLANES * i:SUBLANES * (i + 1), :]
        b8 = b[SUBLANES * i:SUBLANES * (i + 1), :]
        for s in (1, 2, 4):
            keep = row8 >= s
            b8 = jnp.where(keep, a8 * pltpu.roll(b8, s, 0) + b8, b8)
            a8 = jnp.where(keep, a8 * pltpu.roll(a8, s, 0), a8)
        hb = b8 + a8 * hc
        hs.append(hb)
        hc = jnp.broadcast_to(hb[SUBLANES - 1:SUBLANES, :], hb.shape)
    return _bf(jnp.concatenate(hs, axis=0) * (zg * _sigmoid(zg))), hc


def _lru_step_kernel(zx_ref, zg_ref, conv_ref, h0_ref, lp_ref, wg_ref, o_ref, hnew_ref, cnew_ref):
    zx = zx_ref[...]
    keep = (CONV_W - 2) * LRU_W
    cnew_ref[:, 0:keep] = conv_ref[:, LRU_W:LRU_W + keep]
    cnew_ref[:, keep:keep + LRU_W] = zx
    xc = _prow(lp_ref, _CW3) * zx + _prow(lp_ref, _CB)
    for j in range(CONV_W - 1):
        xc = xc + _prow(lp_ref, j) * conv_ref[:, LRU_W * j:LRU_W * (j + 1)]
    a, b = _lru_gates(xc, lp_ref, wg_ref)
    h = a * h0_ref[...] + b
    hnew_ref[...] = h
    zg = zg_ref[...]
    o_ref[...] = _bf(h * (zg * _sigmoid(zg)))


def _lru_step(z_main, conv, h0, lp, wg):
    nb = z_main.shape[0]
    full = lambda shp: pl.BlockSpec(shp, lambda i: (0,) * len(shp))
    col = lambda j: pl.BlockSpec((nb, LRU_W), lambda i, j=j: (0, j))
    return pl.pallas_call(
        _lru_step_kernel,
        out_shape=(jax.ShapeDtypeStruct((nb, LRU_W), BF16), jax.ShapeDtypeStruct((nb, LRU_W), F32),
                   jax.ShapeDtypeStruct(conv.shape, F32)),
        grid=(1,),
        in_specs=[col(4), col(5), full(conv.shape), full(h0.shape), full(lp.shape), full(wg.shape)],
        out_specs=(full((nb, LRU_W)), full((nb, LRU_W)), full(conv.shape)),
        compiler_params=pltpu.CompilerParams(
            dimension_semantics=("arbitrary",), vmem_limit_bytes=VMEM_LIMIT),
        name="lru_step",
    )(z_main, z_main, conv, h0, lp, wg)


def _outproj_kernel(x_ref, or_ref, og_ref, mr_ref, mg_ref, wr_ref, wg_ref, wo_ref, fg_ref,
                    out_ref, *, final):
    y_r = _dg(or_ref[...], wr_ref[...], NN)
    y_g = _dg(og_ref[...], wg_ref[...], NN)
    merged = _sigmoid(mr_ref[...]) * y_r + _sigmoid(mg_ref[...]) * y_g
    out = x_ref[...] + _dg(_bf(merged), wo_ref[...], NN)
    out_ref[...] = _rms(out, fg_ref[...]) if final else out


def _outproj(x, o_r, o_g, z_main, w_r, w_g, w_o, fg, tm, final):
    m, d = x.shape
    const = lambda shp: pl.BlockSpec(shp, lambda i: (0,) * len(shp), pipeline_mode=pl.Buffered(1))
    return pl.pallas_call(
        functools.partial(_outproj_kernel, final=final),
        out_shape=jax.ShapeDtypeStruct((m, d), F32),
        grid=(m // tm,),
        in_specs=[
            pl.BlockSpec((tm, d), lambda i: (i, 0)),
            pl.BlockSpec((tm, RWKV_W), lambda i: (i, 0)),
            pl.BlockSpec((tm, LRU_W), lambda i: (i, 0)),
            pl.BlockSpec((tm, d), lambda i: (i, 3)),
            pl.BlockSpec((tm, d), lambda i: (i, 4)),
            const(w_r.shape), const(w_g.shape), const(w_o.shape), const(fg.shape),
        ],
        out_specs=pl.BlockSpec((tm, d), lambda i: (i, 0)),
        compiler_params=pltpu.CompilerParams(
            dimension_semantics=("arbitrary",), vmem_limit_bytes=VMEM_LIMIT),
        name="outproj",
    )(x, o_r, o_g, z_main, z_main, w_r, w_g, w_o, fg)


def _outproj_lru_kernel(x_ref, or_ref, mr_ref, mg_ref, zx_ref, zg_ref, lp_ref, wgate_ref, wr_ref,
                        wg_ref, wo_ref, fg_ref, out_ref, hlast_ref, og_s, xb_s, hc_s, *, final,
                        tiles_per_seq):
    i = pl.program_id(0)
    n = pl.num_programs(0) - 1

    @pl.when(i == 0)
    def _():
        og_s[...] = jnp.zeros_like(og_s)
        xb_s[...] = jnp.zeros_like(xb_s)
        hc_s[...] = jnp.zeros_like(hc_s)

    og_prev = og_s[...]
    t = lax.rem(jnp.minimum(i, n - 1), tiles_per_seq)
    a, b = _lru_coeffs(zx_ref[...], t == 0, xb_s, lp_ref, wgate_ref)
    y_r = _dg(or_ref[...], wr_ref[...], NN)
    y_g = _dg(og_prev, wg_ref[...], NN)
    merged = _sigmoid(mr_ref[...]) * y_r + _sigmoid(mg_ref[...]) * y_g
    out = x_ref[...] + _dg(_bf(merged), wo_ref[...], NN)
    o_g, hc = _lru_scan_rows(a, b, zg_ref[...], jnp.where(t == 0, 0.0, hc_s[...]))
    hc_s[...] = hc
    og_s[...] = o_g
    out_ref[...] = _rms(out, fg_ref[...]) if final else out

    @pl.when((t == tiles_per_seq - 1) & (i < n))
    def _():
        hlast_ref[0] = hc[0:1, :]


def _outproj_lru(x, o_r, z, lp, wgate, w_r, w_g, w_o, fg, tm, seq, final):
    m, d = x.shape
    n = m // tm
    tiles_per_seq = seq // tm
    const = lambda shp: pl.BlockSpec(shp, lambda i: (0,) * len(shp), pipeline_mode=pl.Buffered(1))
    prev = lambda i: jnp.maximum(i - 1, 0)
    here = lambda i: jnp.minimum(i, n - 1)
    return pl.pallas_call(
        functools.partial(_outproj_lru_kernel, final=final, tiles_per_seq=tiles_per_seq),
        out_shape=(jax.ShapeDtypeStruct((m, d), F32),
                   jax.ShapeDtypeStruct((m // seq, 1, LRU_W), F32)),
        grid=(n + 1,),
        in_specs=[
            pl.BlockSpec((tm, d), lambda i: (prev(i), 0)),
            pl.BlockSpec((tm, RWKV_W), lambda i: (prev(i), 0)),
            pl.BlockSpec((tm, d), lambda i: (prev(i), 3)),
            pl.BlockSpec((tm, d), lambda i: (prev(i), 4)),
            pl.BlockSpec((tm, LRU_W), lambda i: (here(i), 4)),
            pl.BlockSpec((tm, LRU_W), lambda i: (here(i), 5)),
            const(lp.shape), const(wgate.shape), const(w_r.shape), const(w_g.shape), const(w_o.shape),
            const(fg.shape),
        ],
        out_specs=(pl.BlockSpec((tm, d), lambda i: (prev(i), 0)),
                   pl.BlockSpec((1, 1, LRU_W), lambda i: (here(i) // tiles_per_seq, 0, 0))),
        scratch_shapes=[pltpu.VMEM((tm, LRU_W), BF16),
                        pltpu.VMEM((SUBLANES + tm, LRU_W), F32),
                        pltpu.VMEM((SUBLANES, LRU_W), F32)],
        compiler_params=pltpu.CompilerParams(
            dimension_semantics=("arbitrary",), vmem_limit_bytes=VMEM_LIMIT),
        name="outproj_lru",
    )(x, o_r, z, z, z, z, lp, wgate, w_r, w_g, w_o, fg)


def _row_tile(m, want):
    t = min(m, want)
    assert m % t == 0, (m, t)
    return t


def _pack_params_kernel(mu_ref, w0_ref, a0_ref, kk_ref, ka_ref, rk_ref, lng_ref, lnb_ref, wdu_ref,
                        wau_ref, cw_ref, cb_ref, gxb_ref, gab_ref, lam_ref, gxw_ref, gaw_ref,
                        pvec_ref, mul_ref, wd_ref, wa_ref, e_ref, lp_ref, wg_ref):
    pvec_ref[...] = jnp.zeros_like(pvec_ref)
    for i in range(3):
        pvec_ref[_MU_R + i:_MU_R + i + 1, :] = mu_ref[:, RWKV_W * i:RWKV_W * (i + 1)]
    for row, ref in ((_W0, w0_ref), (_A0, a0_ref), (_KK, kk_ref), (_KA, ka_ref), (_RK, rk_ref),
                     (_LNG, lng_ref), (_LNB, lnb_ref)):
        pvec_ref[row:row + 1, :] = ref[...]
    mul_ref[...] = jnp.broadcast_to(mu_ref[:, 3 * RWKV_W:3 * RWKV_W + 2 * LORA], mul_ref.shape)

    zeros = jnp.zeros((LORA, RWKV_W), BF16)
    wd_ref[0:LORA, :] = _bf(wdu_ref[...])
    wd_ref[LORA:2 * LORA, :] = zeros
    wa_ref[0:LORA, :] = zeros
    wa_ref[LORA:2 * LORA, :] = _bf(wau_ref[...])

    ri = lax.broadcasted_iota(jnp.int32, (LANES, LANES), 0)
    ci = lax.broadcasted_iota(jnp.int32, (LANES, LANES), 1)
    e_ref[...] = jnp.where((ri < HEAD) == (ci < HEAD), 1.0, 0.0).astype(BF16)

    lp_ref[_CW0:_CW0 + CONV_W, :] = cw_ref[...]
    for row, ref in ((_CB, cb_ref), (_GXB, gxb_ref), (_GAB, gab_ref), (_LAM, lam_ref)):
        lp_ref[row:row + 1, :] = ref[...]

    blk = LRU_W // LRU_BLOCKS
    z = jnp.zeros((blk, blk), F32)
    for g in range(LRU_BLOCKS // 2):
        top = jnp.concatenate([gxw_ref[2 * g], z, gaw_ref[2 * g], z], axis=1)
        bot = jnp.concatenate([z, gxw_ref[2 * g + 1], z, gaw_ref[2 * g + 1]], axis=1)
        wg_ref[g] = _bf(jnp.concatenate([top, bot], axis=0))


def _pack_params(l, rwkv_mu, w_decay0, w_decay_up, w_iclr0, w_iclr_up, k_k, k_a, r_k, ln_x_g,
                 ln_x_b, conv_w, conv_b, lru_gx_w, lru_gx_b, lru_ga_w, lru_ga_b, lru_lambda):
    blk = LRU_W // LRU_BLOCKS
    assert 2 * blk == LANES and 2 * LORA == LANES
    depth = rwkv_mu.shape[0]
    row = lambda a: pl.BlockSpec((1, a.shape[-1]), lambda i: (l, 0))
    mat = lambda a: pl.BlockSpec((None,) + a.shape[1:], lambda i: (l,) + (0,) * (a.ndim - 1))
    full = lambda shp: pl.BlockSpec(shp, lambda i: (0,) * len(shp))
    rk = r_k.reshape(depth, RWKV_W)
    rows = (rwkv_mu, w_decay0, w_iclr0, k_k, k_a, rk, ln_x_g, ln_x_b)
    out_shapes = ((16, RWKV_W, F32), (SUBLANES, LANES, F32), (LANES, RWKV_W, BF16), (LANES, RWKV_W, BF16),
                  (LANES, LANES, BF16), (SUBLANES, LRU_W, F32))
    outs = tuple(jax.ShapeDtypeStruct(s[:2], s[2]) for s in out_shapes)
    outs += (jax.ShapeDtypeStruct((LRU_BLOCKS // 2, LANES, 2 * LANES), BF16),)
    return pl.pallas_call(
        _pack_params_kernel,
        out_shape=outs,
        grid=(1,),
        in_specs=[row(a) for a in rows] + [mat(w_decay_up), mat(w_iclr_up), mat(conv_w), row(conv_b),
                                          row(lru_gx_b), row(lru_ga_b), row(lru_lambda),
                                          mat(lru_gx_w), mat(lru_ga_w)],
        out_specs=tuple(full(o.shape) for o in outs),
        compiler_params=pltpu.CompilerParams(
            dimension_semantics=("arbitrary",), vmem_limit_bytes=VMEM_LIMIT),
        name="pack_params",
    )(*rows, w_decay_up, w_iclr_up, conv_w, conv_b, lru_gx_b, lru_ga_b, lru_lambda, lru_gx_w, lru_ga_w)


def kernel(x_prompt, x_sample, state_shift, state_wkv, state_conv, state_lru, norm_g, w_in, rwkv_mu,
           w_decay0, w_decay_up, w_iclr0, w_iclr_up, k_k, k_a, r_k, ln_x_g, ln_x_b, w_out_rwkv,
           conv_w, conv_b, lru_gx_w, lru_gx_b, lru_ga_w, lru_ga_b, lru_lambda, w_out_lru, w_out,
           final_norm_g):
    bp, seq, d = x_prompt.shape
    bs = x_sample.shape[0]
    assert x_sample.shape[1] == 1 and seq % WKV_CHUNK == 0
    depth = w_in.shape[0]
    xp = x_prompt.reshape(bp * seq, d)
    xs = x_sample.reshape(bs, d)
    fg = final_norm_g.reshape(1, d)
    outs = [[] for _ in range(8)]
    for l in range(depth):
        pvec, mul, wd, wa, e, lp, wg = _pack_params(
            l, rwkv_mu, w_decay0, w_decay_up, w_iclr0, w_iclr_up, k_k, k_a, r_k, ln_x_g, ln_x_b,
            conv_w, conv_b, lru_gx_w, lru_gx_b, lru_ga_w, lru_ga_b, lru_lambda)
        g = norm_g[l].reshape(1, d)
        rec = (pvec, mul, wd, wa, e)
        w = _repack_w_in(w_in, l, _row_tile(d, 256))
        zp, zlp, zs, zls, w_r, w_g, w_o = _inproj(
            xp, xs, g, w, (w_out_rwkv[l], w_out_lru[l], w_out[l]), _row_tile(bp * seq, 1024), INPROJ_TN)

        zp3 = zp.reshape(bp, seq, -1)
        zlp3 = zlp.reshape(bp, seq, LANES)
        nb = max(n for n in (4, 2, 1) if bp % n == 0)
        o_r, s_new = _wkv_chunk(zp3, zlp3, *rec, bp, seq, nb)
        o_r = o_r.reshape(bp * seq, RWKV_W)
        last = l == depth - 1
        xp, h_last = _outproj_lru(xp, o_r, zp, lp, wg, w_r, w_g, w_o, fg,
                                  _row_tile(seq, 256), seq, last)
        outs[0].append(jnp.concatenate([zp3[:, -1, :SHIFT_MAIN], zlp3[:, -1]], axis=-1))
        outs[1].append(s_new)
        outs[2].append(zp3[:, seq - (CONV_W - 1):, 4 * RWKV_W:4 * RWKV_W + LRU_W])
        outs[3].append(h_last.reshape(bp, LRU_W))

        s0t = jnp.transpose(state_wkv[l], (1, 2, 3, 0))
        o_r, s_new, sh_new = _wkv_step(zs, zls, state_shift[l], s0t, *rec)
        s_new = jnp.transpose(s_new, (3, 0, 1, 2))
        conv = state_conv[l].reshape(bs, (CONV_W - 1) * LRU_W)
        o_g, h_new, conv_new = _lru_step(zs, conv, state_lru[l], lp, wg)
        xs = _outproj(xs, o_r, o_g, zs, w_r, w_g, w_o, fg, bs, last)
        outs[4].append(sh_new)
        outs[5].append(s_new)
        outs[6].append(conv_new.reshape(bs, CONV_W - 1, LRU_W))
        outs[7].append(h_new)

    return (xp.reshape(bp, seq, d), xs.reshape(bs, 1, d)) + tuple(jnp.stack(o) for o in outs)
```

```python
import functools

import jax
import jax.numpy as jnp
from jax import lax
from jax.experimental import pallas as pl
from jax.experimental.pallas import tpu as pltpu

F32 = jnp.float32
BF16 = jnp.bfloat16

HEADS = 16
HEAD = 64
RWKV_W = HEADS * HEAD
LORA = 64
LRU_W = 1024
LRU_BLOCKS = 16
CONV_W = 4
LRU_C = 8.0
RMS_EPS = 1e-6
GN_EPS = 1e-5 * HEAD
DECAY_SCALE = 0.6065306597126334

LANES = 128
SUBLANES = 8
WKV_CHUNK = 64
VMEM_LIMIT = 56 * 1024 * 1024

NN = (((1,), (0,)), ((), ()))
NT = (((1,), (1,)), ((), ()))
TN = (((0,), (0,)), ((), ()))


def _bf(x):
    return x.astype(BF16)


def _dg(a, b, dn):
    return lax.dot_general(a, b, dn, preferred_element_type=F32)


def _softplus(x):
    return jnp.maximum(x, 0.0) + jnp.log1p(jnp.exp(-jnp.abs(x)))


def _sigmoid(x):
    return 1.0 / (1.0 + jnp.exp(-x))


def _segsum(x, e):
    rows, n = x.shape[0], x.shape[1] // LANES
    stacked = jnp.concatenate([x[:, LANES * j:LANES * (j + 1)] for j in range(n)], axis=0)
    s = _dg(_bf(stacked), e, NN)
    return jnp.concatenate([s[rows * j:rows * (j + 1), :] for j in range(n)], axis=1)


def _rms(x, g):
    return x * lax.rsqrt(jnp.mean(x * x, axis=-1, keepdims=True) + RMS_EPS) * g


SHIFT_MAIN = 3 * RWKV_W
LORA_COL = 10 * RWKV_W
LORA_BLOCK = LORA_COL // LANES
INPROJ_TN = 1024


def _repack_kernel(w_ref, o_ref):
    n = w_ref.shape[-1]
    lora_end = SHIFT_MAIN + 2 * LORA
    o_ref[:, 0:SHIFT_MAIN] = _bf(w_ref[:, 0:SHIFT_MAIN])
    o_ref[:, SHIFT_MAIN:LORA_COL] = _bf(w_ref[:, lora_end:n])
    o_ref[:, LORA_COL:n] = _bf(w_ref[:, SHIFT_MAIN:lora_end])


def _repack_w_in(w_in, layer, tr):
    _, d, n = w_in.shape
    assert n == LORA_COL + 2 * LORA
    return pl.pallas_call(
        _repack_kernel,
        out_shape=jax.ShapeDtypeStruct((d, n), BF16),
        grid=(d // tr,),
        in_specs=[pl.BlockSpec((None, tr, n), lambda i: (layer, i, 0))],
        out_specs=pl.BlockSpec((tr, n), lambda i: (i, 0)),
        compiler_params=pltpu.CompilerParams(
            dimension_semantics=("arbitrary",), vmem_limit_bytes=VMEM_LIMIT),
        name="repack_w_in",
    )(w_in)


def _inproj_kernel(x_ref, xs_ref, g_ref, w_ref, wl_ref, z_ref, zl_ref, zs_ref, zls_ref, h_ref, hs_ref):
    i = pl.program_id(0)
    j = pl.program_id(1)

    @pl.when((i == 0) & (j == 0))
    def _():
        hs = _bf(_rms(xs_ref[...], g_ref[...]))
        hs_ref[...] = hs
        zls_ref[...] = _dg(hs, wl_ref[...], NN)

    @pl.when(j == 0)
    def _():
        h = _bf(_rms(x_ref[...], g_ref[...]))
        h_ref[...] = h
        zl_ref[...] = _dg(h, wl_ref[...], NN)

    z_ref[...] = _dg(h_ref[...], w_ref[...], NN)

    @pl.when(i == 0)
    def _():
        zs_ref[...] = _dg(hs_ref[...], w_ref[...], NN)


def _inproj(x, xs, g, w, tm, tn):
    m, d = x.shape
    ms = xs.shape[0]
    nj = LORA_COL // tn
    park = lambda i, j: jnp.where(i == 0, j, nj - 1)
    return pl.pallas_call(
        _inproj_kernel,
        out_shape=(jax.ShapeDtypeStruct((m, LORA_COL), F32), jax.ShapeDtypeStruct((m, LANES), F32),
                   jax.ShapeDtypeStruct((ms, LORA_COL), F32), jax.ShapeDtypeStruct((ms, LANES), F32)),
        grid=(m // tm, nj),
        in_specs=[
            pl.BlockSpec((tm, d), lambda i, j: (i, 0)),
            pl.BlockSpec((ms, d), lambda i, j: (0, 0), pipeline_mode=pl.Buffered(1)),
            pl.BlockSpec((1, d), lambda i, j: (0, 0), pipeline_mode=pl.Buffered(1)),
            pl.BlockSpec((d, tn), lambda i, j: (0, j)),
            pl.BlockSpec((d, LANES), lambda i, j: (0, LORA_BLOCK), pipeline_mode=pl.Buffered(1)),
        ],
        out_specs=(
            pl.BlockSpec((tm, tn), lambda i, j: (i, j)),
            pl.BlockSpec((tm, LANES), lambda i, j: (i, 0)),
            pl.BlockSpec((ms, tn), lambda i, j: (0, park(i, j))),
            pl.BlockSpec((ms, LANES), lambda i, j: (0, 0)),
        ),
        scratch_shapes=[pltpu.VMEM((tm, d), BF16), pltpu.VMEM((ms, d), BF16)],
        compiler_params=pltpu.CompilerParams(
            dimension_semantics=("arbitrary", "arbitrary"), vmem_limit_bytes=VMEM_LIMIT),
        name="inproj",
    )(x, xs, g, w, w)


_MU_R, _MU_K, _MU_V, _W0, _A0, _KK, _KA, _RK, _LNG, _LNB = range(10)


def _prow(pv_ref, i):
    return pv_ref[i:i + 1, :]


def _wkv_prep(zr, zk, zv, zl, pr, pk, pv, pl_, pv_ref, mul_ref, wd_ref, wa_ref, e):
    r = zr + _prow(pv_ref, _MU_R) * (pr - zr)
    k = zk + _prow(pv_ref, _MU_K) * (pk - zk)
    v = zv + _prow(pv_ref, _MU_V) * (pv - zv)
    lo = zl + mul_ref[0:1, :] * (pl_ - zl)
    lw = _dg(_bf(jnp.tanh(lo)), wd_ref[...], NN)
    la = _dg(_bf(lo), wa_ref[...], NN)
    logd = -DECAY_SCALE * _sigmoid(_prow(pv_ref, _W0) + lw)
    a = _sigmoid(_prow(pv_ref, _A0) + la)
    kk = k * _prow(pv_ref, _KK)
    kk = kk * lax.rsqrt(jnp.maximum(_segsum(kk * kk, e), 1e-24))
    k2 = k * (1.0 + (a - 1.0) * _prow(pv_ref, _KA))
    return r, k2, v, -kk, kk * a, logd


def _wkv_bonus_gate(r, k2, v, zrg, pv_ref, e):
    return _segsum(r * k2 * _prow(pv_ref, _RK), e) * v, zrg * _sigmoid(zrg)


def _wkv_norm_gate(y, bonus_v, gate, pv_ref, e):
    mu = _segsum(y, e) * (1.0 / HEAD)
    yc = y - mu
    var = _segsum(yc * yc, e) * (1.0 / HEAD)
    yn = yc * lax.rsqrt(var + GN_EPS) * _prow(pv_ref, _LNG) + _prow(pv_ref, _LNB)
    return _bf((yn + bonus_v) * gate)


def _wkv_post(y, r, k2, v, zrg, pv_ref, e):
    bonus_v, gate = _wkv_bonus_gate(r, k2, v, zrg, pv_ref, e)
    return _wkv_norm_gate(y, bonus_v, gate, pv_ref, e)


def _wkv_chunk_kernel(zr_ref, zk_ref, zv_ref, zrg_ref, zl_ref, pv_ref, mul_ref, wd_ref, wa_ref,
                      e_ref, o_ref, sout_ref, s_s, prev_s, prevl_s):
    c = pl.program_id(1)
    nc = pl.num_programs(1)
    C = WKV_CHUNK
    assert C == HEAD and 2 * HEAD == LANES
    nb = zr_ref.shape[0]
    rows_all = nb * C
    seqs = range(nb)

    @pl.when(c == 0)
    def _():
        s_s[...] = jnp.zeros_like(s_s)
        prev_s[...] = jnp.zeros_like(prev_s)
        prevl_s[...] = jnp.zeros_like(prevl_s)

    first = lax.broadcasted_iota(jnp.int32, (SUBLANES, 1), 0) == 0

    def shifted(z, prev_ref, lanes):
        rolled = pltpu.roll(z, 1, 0)
        pieces = []
        for b in seqs:
            head = jnp.where(first, prev_ref[b, 0:1, lanes], rolled[b * C:b * C + SUBLANES, :])
            pieces += [head, rolled[b * C + SUBLANES:(b + 1) * C, :]]
        return jnp.concatenate(pieces, axis=0)

    def flat(ref):
        return ref[...].reshape(rows_all, ref.shape[-1])

    zr, zk, zv, zl = flat(zr_ref), flat(zk_ref), flat(zv_ref), flat(zl_ref)
    seg = [slice(RWKV_W * i, RWKV_W * (i + 1)) for i in range(3)]
    pr = shifted(zr, prev_s, seg[0])
    pk = shifted(zk, prev_s, seg[1])
    pv = shifted(zv, prev_s, seg[2])
    pl_ = shifted(zl, prevl_s, slice(0, LANES))
    for b in seqs:
        last = slice(b * C + C - 1, b * C + C)
        prev_s[b, 0:1, seg[0]] = zr[last, :]
        prev_s[b, 0:1, seg[1]] = zk[last, :]
        prev_s[b, 0:1, seg[2]] = zv[last, :]
        prevl_s[b, 0:1, :] = zl[last, :]

    e = e_ref[...]
    r, k2, v, av, bv, logd = _wkv_prep(zr, zk, zv, zl, pr, pk, pv, pl_, pv_ref, mul_ref,
                                       wd_ref, wa_ref, e)

    ti = lax.broadcasted_iota(jnp.int32, (rows_all, rows_all), 0)
    tj = lax.broadcasted_iota(jnp.int32, (rows_all, rows_all), 1)
    tri = jnp.where((ti >= tj) & ((ti & -C) == (tj & -C)), 1.0, 0.0).astype(BF16)
    d_hi = _bf(logd)
    d_r1 = logd - d_hi.astype(F32)
    d_mid = _bf(d_r1)
    d_lo = _bf(d_r1 - d_mid.astype(F32))
    cum = _dg(tri, d_hi, NN) + (_dg(tri, d_mid, NN) + _dg(tri, d_lo, NN))
    e_in = jnp.exp(cum)
    e_neg = jnp.exp(-cum)
    a_t = av * jnp.exp(cum - logd)
    r_t = r * e_in
    k_t = k2 * e_neg
    b_t = bv * e_neg
    p_c = [jnp.exp(cum[b * C + C - 1:b * C + C, :]) for b in seqs]

    lane = lax.broadcasted_iota(jnp.int32, (C, LANES), 1)
    trow = lax.broadcasted_iota(jnp.int32, (C, LANES), 0)
    lo = lane < HEAD
    s_in = lane & (HEAD - 1)
    strict = s_in < trow
    incl2 = ((lax.broadcasted_iota(jnp.int32, (C, 2 * LANES), 1) & (HEAD - 1))
             <= lax.broadcasted_iota(jnp.int32, (C, 2 * LANES), 0))
    eye2 = jnp.where(s_in == trow, 1.0, 0.0).astype(F32)
    vrow = lax.broadcasted_iota(jnp.int32, (2 * HEAD, LANES), 0)
    klane = lax.broadcasted_iota(jnp.int32, (2 * HEAD, LANES), 1)
    same_head = (vrow < HEAD) == (klane < HEAD)

    def bd(x):
        z = jnp.zeros_like(x)
        return jnp.concatenate([jnp.where(lo, x, z), jnp.where(lo, z, x)], axis=0)

    npair = HEADS // 2
    units = [(b, p) for b in seqs for p in range(npair)]
    un = range(len(units))
    blk = lambda arr, i: arr[units[i][0] * C:(units[i][0] + 1) * C, LANES * units[i][1]:LANES * (units[i][1] + 1)]
    ar = [_bf(jnp.concatenate([blk(a_t, i), blk(r_t, i)], axis=0)) for i in un]
    bk = [_bf(jnp.concatenate([bd(blk(b_t, i)), bd(blk(k_t, i))], axis=0)) for i in un]
    g = [_dg(ar[i], bk[i], NT) for i in un]
    s0 = [s_s[i] for i in un]
    ars = [_dg(ar[i], _bf(s0[i]), NT) for i in un]
    vbd = [_bf(bd(blk(v, i))) for i in un]
    x = [jnp.where(strict, g[i][0:C, 0:LANES], 0.0) for i in un]
    ak = [jnp.where(strict, g[i][0:C, LANES:2 * LANES], 0.0) for i in un]
    w = [ars[i][0:C, :] + _dg(_bf(ak[i]), vbd[i], NN) for i in un]
    t = [eye2 + x[i] for i in un]
    x = [_dg(_bf(x[i]), _bf(bd(x[i])), NN) for i in un]
    for _ in range(C.bit_length() - 3):
        xt = [_dg(_bf(jnp.concatenate([x[i], t[i]], axis=0)), _bf(bd(x[i])), NN) for i in un]
        x = [xt[i][0:C, :] for i in un]
        t = [t[i] + xt[i][C:2 * C, :] for i in un]
    t = [t[i] + _dg(_bf(t[i]), _bf(bd(x[i])), NN) for i in un]
    u = [_dg(_bf(t[i]), _bf(bd(w[i])), NN) for i in un]
    rbk = [_bf(jnp.where(incl2, g[i][C:2 * C, :], 0.0)) for i in un]
    uvbd = [jnp.concatenate([_bf(bd(u[i])), vbd[i]], axis=0) for i in un]
    y = [ars[i][C:2 * C, :] + _dg(rbk[i], uvbd[i], NN) for i in un]
    uv = [_bf(jnp.concatenate([u[i], blk(v, i)], axis=0)) for i in un]
    pc = [p_c[units[i][0]][:, LANES * units[i][1]:LANES * (units[i][1] + 1)] for i in un]
    bkh = [_bf(jnp.concatenate([blk(b_t, i), blk(k_t, i)], axis=0) * pc[i]) for i in un]
    s1 = [s0[i] * pc[i] + jnp.where(same_head, _dg(uv[i], bkh[i], TN), 0.0) for i in un]
    for i in un:
        s_s[i] = s1[i]

    y_all = jnp.concatenate(
        [jnp.concatenate(y[b * npair:(b + 1) * npair], axis=1) for b in seqs], axis=0)
    o = _wkv_post(y_all, r, k2, v, flat(zrg_ref), pv_ref, e)
    o_ref[...] = o.reshape(nb, C, RWKV_W)

    @pl.when(c == nc - 1)
    def _():
        for i in un:
            b, p = units[i]
            sout_ref[b, 2 * p] = s1[i][0:HEAD, 0:HEAD]
            sout_ref[b, 2 * p + 1] = s1[i][HEAD:2 * HEAD, HEAD:2 * HEAD]


def _wkv_chunk(z, zl, pvec, mul, wd, wa, e, batch, seq, nb):
    C = WKV_CHUNK
    nc = seq // C
    full = lambda shp: pl.BlockSpec(shp, lambda b, c: (0,) * len(shp))
    col = lambda j: pl.BlockSpec((nb, C, RWKV_W), lambda b, c, j=j: (b, c, j))
    return pl.pallas_call(
        _wkv_chunk_kernel,
        out_shape=(jax.ShapeDtypeStruct((batch, seq, RWKV_W), BF16),
                   jax.ShapeDtypeStruct((batch, HEADS, HEAD, HEAD), F32)),
        grid=(batch // nb, nc),
        in_specs=[col(0), col(1), col(2), col(3),
                  pl.BlockSpec((nb, C, LANES), lambda b, c: (b, c, 0)),
                  full(pvec.shape), full(mul.shape), full(wd.shape), full(wa.shape), full(e.shape)],
        out_specs=(pl.BlockSpec((nb, C, RWKV_W), lambda b, c: (b, c, 0)),
                   pl.BlockSpec((nb, HEADS, HEAD, HEAD), lambda b, c: (b, 0, 0, 0))),
        scratch_shapes=[pltpu.VMEM((nb * HEADS // 2, 2 * HEAD, 2 * HEAD), F32),
                        pltpu.VMEM((nb, SUBLANES, 3 * RWKV_W), F32),
                        pltpu.VMEM((nb, SUBLANES, LANES), F32)],
        compiler_params=pltpu.CompilerParams(
            dimension_semantics=("arbitrary", "arbitrary"), vmem_limit_bytes=VMEM_LIMIT),
        name="wkv_chunk",
    )(z, z, z, z, zl, pvec, mul, wd, wa, e)


def _wkv_step_kernel(zr_ref, zk_ref, zv_ref, zrg_ref, zl_ref, sh_ref, s0_ref, pv_ref,
                     mul_ref, wd_ref, wa_ref, e_ref, o_ref, sout_ref, nsh_ref,
                     at_s, drt_s, bt_s, kt_s, dt_s, vt_s, brt_s, krt_s, yt_s, keep_s):
    h = pl.program_id(0)
    nh = pl.num_programs(0)
    nseq = zr_ref.shape[0]

    @pl.when(h == 0)
    def _():
        e = e_ref[...]
        r, k2, v, av, bv, logd = _wkv_prep(
            zr_ref[...], zk_ref[...], zv_ref[...], zl_ref[...],
            sh_ref[:, 0:RWKV_W], sh_ref[:, RWKV_W:2 * RWKV_W], sh_ref[:, 2 * RWKV_W:3 * RWKV_W],
            sh_ref[:, SHIFT_MAIN:SHIFT_MAIN + LANES], pv_ref, mul_ref, wd_ref, wa_ref, e)
        nsh_ref[:, 0:RWKV_W] = zr_ref[...]
        nsh_ref[:, RWKV_W:2 * RWKV_W] = zk_ref[...]
        nsh_ref[:, 2 * RWKV_W:SHIFT_MAIN] = zv_ref[...]
        nsh_ref[:, SHIFT_MAIN:SHIFT_MAIN + LANES] = zl_ref[...]
        d = jnp.exp(logd)
        at_s[...] = av.T
        drt_s[...] = (d * r).T
        bt_s[...] = bv.T
        kt_s[...] = k2.T
        dt_s[...] = d.T
        vt_s[...] = v.T
        brt_s[...] = jnp.sum((bv * r).T.reshape(HEADS, HEAD, nseq), axis=1)
        krt_s[...] = jnp.sum((k2 * r).T.reshape(HEADS, HEAD, nseq), axis=1)
        keep_s[0] = r
        keep_s[1] = k2
        keep_s[2] = v

    base = pl.multiple_of(h * HEAD, HEAD)
    rows = pl.ds(base, HEAD)
    a_h, dr_h, b_h, k_h, d_h = at_s[rows, :], drt_s[rows, :], bt_s[rows, :], kt_s[rows, :], dt_s[rows, :]
    br_h = brt_s[pl.ds(h, 1), :]
    kr_h = krt_s[pl.ds(h, 1), :]

    def value_rows(g, carry):
        off = pl.multiple_of(base + g * SUBLANES, SUBLANES)
        v8 = vt_s[pl.ds(off, SUBLANES), :]
        ys = []
        for j in range(SUBLANES):
            vi = g * SUBLANES + j
            s_v = s0_ref[0, vi]
            sa = jnp.sum(s_v * a_h, axis=0, keepdims=True)
            y0 = jnp.sum(s_v * dr_h, axis=0, keepdims=True)
            v_v = v8[j:j + 1, :]
            sout_ref[0, vi] = s_v * d_h + sa * b_h + v_v * k_h
            ys.append(y0 + sa * br_h + v_v * kr_h)
        yt_s[pl.ds(off, SUBLANES), :] = jnp.concatenate(ys, axis=0)
        return carry

    lax.fori_loop(0, HEAD // SUBLANES, value_rows, 0)

    @pl.when(h == nh - 1)
    def _():
        o_ref[...] = _wkv_post(yt_s[...].T, keep_s[0], keep_s[1], keep_s[2], zrg_ref[...], pv_ref,
                               e_ref[...])


def _wkv_step(z, zl, sh, s0t, pvec, mul, wd, wa, e):
    nseq = z.shape[0]
    full = lambda shp: pl.BlockSpec(shp, lambda i: (0,) * len(shp))
    col = lambda j: pl.BlockSpec((nseq, RWKV_W), lambda i, j=j: (0, j))
    st_block = (1, HEAD, HEAD, nseq)
    wide = pltpu.VMEM((RWKV_W, nseq), F32)
    return pl.pallas_call(
        _wkv_step_kernel,
        out_shape=(jax.ShapeDtypeStruct((nseq, RWKV_W), BF16),
                   jax.ShapeDtypeStruct(s0t.shape, F32),
                   jax.ShapeDtypeStruct(sh.shape, F32)),
        grid=(HEADS,),
        in_specs=[col(0), col(1), col(2), col(3),
                  full(zl.shape), full(sh.shape),
                  pl.BlockSpec(st_block, lambda i: (i, 0, 0, 0)),
                  full(pvec.shape), full(mul.shape), full(wd.shape), full(wa.shape), full(e.shape)],
        out_specs=(full((nseq, RWKV_W)),
                   pl.BlockSpec(st_block, lambda i: (i, 0, 0, 0)),
                   full(sh.shape)),
        scratch_shapes=[wide] * 6 + [pltpu.VMEM((HEADS, nseq), F32)] * 2
                       + [wide, pltpu.VMEM((3, nseq, RWKV_W), F32)],
        compiler_params=pltpu.CompilerParams(
            dimension_semantics=("arbitrary",), vmem_limit_bytes=VMEM_LIMIT),
        name="wkv_step",
    )(z, z, z, z, zl, sh, s0t, pvec, mul, wd, wa, e)


_CW0, _CW1, _CW2, _CW3, _CB, _GXB, _GAB, _LAM = range(8)


def _lru_gates(xc, lp_ref, wg_ref):
    xb = _bf(xc)
    ngroups = wg_ref.shape[0]
    gs = [_dg(xb[:, LANES * g:LANES * (g + 1)], wg_ref[g], NN) for g in range(ngroups)]
    gx_pre = jnp.concatenate([gs[g][:, 0:LANES] for g in range(ngroups)], axis=1)
    ga_pre = jnp.concatenate([gs[g][:, LANES:2 * LANES] for g in range(ngroups)], axis=1)
    gx = _sigmoid(gx_pre + _prow(lp_ref, _GXB))
    ga = _sigmoid(ga_pre + _prow(lp_ref, _GAB))
    log_a = -LRU_C * ga * _softplus(-_prow(lp_ref, _LAM))
    a = jnp.exp(log_a)
    mult = jnp.sqrt((1.0 - a) * (1.0 + a))
    return a, mult * gx * xc


def _lru_coeffs(zx, first, xb_s, lp_ref, wgate_ref):
    tl = zx.shape[0]
    xb_s[0:SUBLANES, :] = jnp.where(first, 0.0, xb_s[0:SUBLANES, :])
    xb_s[SUBLANES:SUBLANES + tl, :] = zx
    xc = _prow(lp_ref, _CW3) * zx + _prow(lp_ref, _CB)
    for j in range(1, CONV_W):
        xc = xc + _prow(lp_ref, CONV_W - 1 - j) * xb_s[SUBLANES - j:SUBLANES - j + tl, :]
    xb_s[0:SUBLANES, :] = zx[tl - SUBLANES:tl, :]
    return _lru_gates(xc, lp_ref, wgate_ref)


def _lru_scan_rows(a, b, zg, hc):
    row8 = lax.broadcasted_iota(jnp.int32, (SUBLANES, 1), 0)
    hs = []
    for i in range(a.shape[0] // SUBLANES):
        a8 = a[SUBLANES * i:SUBLANES * (i + 1), :]
        b8 = b[SUBLANES * i:SUBLANES * (i + 1), :]
        for s in (1, 2, 4):
            keep = row8 >= s
            b8 = jnp.where(keep, a8 * pltpu.roll(b8, s, 0) + b8, b8)
            a8 = jnp.where(keep, a8 * pltpu.roll(a8, s, 0), a8)
        hb = b8 + a8 * hc
        hs.append(hb)
        hc = jnp.broadcast_to(hb[SUBLANES - 1:SUBLANES, :], hb.shape)
    return _bf(jnp.concatenate(hs, axis=0) * (zg * _sigmoid(zg))), hc


def _lru_step_kernel(zx_ref, zg_ref, conv_ref, h0_ref, lp_ref, wg_ref, o_ref, hnew_ref, cnew_ref):
    zx = zx_ref[...]
    keep = (CONV_W - 2) * LRU_W
    cnew_ref[:, 0:keep] = conv_ref[:, LRU_W:LRU_W + keep]
    cnew_ref[:, keep:keep + LRU_W] = zx
    xc = _prow(lp_ref, _CW3) * zx + _prow(lp_ref, _CB)
    for j in range(CONV_W - 1):
        xc = xc + _prow(lp_ref, j) * conv_ref[:, LRU_W * j:LRU_W * (j + 1)]
    a, b = _lru_gates(xc, lp_ref, wg_ref)
    h = a * h0_ref[...] + b
    hnew_ref[...] = h
    zg = zg_ref[...]
    o_ref[...] = _bf(h * (zg * _sigmoid(zg)))


def _lru_step(z_main, conv, h0, lp, wg):
    nb = z_main.shape[0]
    full = lambda shp: pl.BlockSpec(shp, lambda i: (0,) * len(shp))
    col = lambda j: pl.BlockSpec((nb, LRU_W), lambda i, j=j: (0, j))
    return pl.pallas_call(
        _lru_step_kernel,
        out_shape=(jax.ShapeDtypeStruct((nb, LRU_W), BF16), jax.ShapeDtypeStruct((nb, LRU_W), F32),
                   jax.ShapeDtypeStruct(conv.shape, F32)),
        grid=(1,),
        in_specs=[col(4), col(5), full(conv.shape), full(h0.shape), full(lp.shape), full(wg.shape)],
        out_specs=(full((nb, LRU_W)), full((nb, LRU_W)), full(conv.shape)),
        compiler_params=pltpu.CompilerParams(
            dimension_semantics=("arbitrary",), vmem_limit_bytes=VMEM_LIMIT),
        name="lru_step",
    )(z_main, z_main, conv, h0, lp, wg)


def _project(x, o_r, o_g, m_r, m_g, wr_ref, wg_ref, wo_ref, fg_ref, final):
    y_r = _dg(o_r, wr_ref[...], NN)
    y_g = _dg(o_g, wg_ref[...], NN)
    merged = _sigmoid(m_r) * y_r + _sigmoid(m_g) * y_g
    out = x + _dg(_bf(merged), wo_ref[...], NN)
    return _rms(out, fg_ref[...]) if final else out


def _outproj_lru_kernel(x_ref, or_ref, mr_ref, mg_ref, zx_ref, zg_ref, xs_ref, ors_ref, ogs_ref,
                        mrs_ref, mgs_ref, lp_ref, wgate_ref, wr_ref, wg_ref, wo_ref, fg_ref,
                        out_ref, hlast_ref, outs_ref, og_s, xb_s, hc_s, *, final, tiles_per_seq):
    i = pl.program_id(0)
    n = pl.num_programs(0) - 1

    @pl.when(i == 0)
    def _():
        og_s[...] = jnp.zeros_like(og_s)
        xb_s[...] = jnp.zeros_like(xb_s)
        hc_s[...] = jnp.zeros_like(hc_s)
        outs_ref[...] = _project(xs_ref[...], ors_ref[...], ogs_ref[...], mrs_ref[...], mgs_ref[...],
                                 wr_ref, wg_ref, wo_ref, fg_ref, final)

    og_prev = og_s[...]
    t = lax.rem(jnp.minimum(i, n - 1), tiles_per_seq)
    a, b = _lru_coeffs(zx_ref[...], t == 0, xb_s, lp_ref, wgate_ref)
    out_ref[...] = _project(x_ref[...], or_ref[...], og_prev, mr_ref[...], mg_ref[...],
                            wr_ref, wg_ref, wo_ref, fg_ref, final)
    o_g, hc = _lru_scan_rows(a, b, zg_ref[...], jnp.where(t == 0, 0.0, hc_s[...]))
    hc_s[...] = hc
    og_s[...] = o_g

    @pl.when((t == tiles_per_seq - 1) & (i < n))
    def _():
        hlast_ref[0] = hc[0:1, :]


def _outproj_lru(x, o_r, z, xs, o_rs, o_gs, zs, lp, wgate, w_r, w_g, w_o, fg, tm, seq, final):
    m, d = x.shape
    ms = xs.shape[0]
    n = m // tm
    tiles_per_seq = seq // tm
    const = lambda shp: pl.BlockSpec(shp, lambda i: (0,) * len(shp), pipeline_mode=pl.Buffered(1))
    prev = lambda i: jnp.maximum(i - 1, 0)
    here = lambda i: jnp.minimum(i, n - 1)
    return pl.pallas_call(
        functools.partial(_outproj_lru_kernel, final=final, tiles_per_seq=tiles_per_seq),
        out_shape=(jax.ShapeDtypeStruct((m, d), F32),
                   jax.ShapeDtypeStruct((m // seq, 1, LRU_W), F32),
                   jax.ShapeDtypeStruct((ms, d), F32)),
        grid=(n + 1,),
        in_specs=[
            pl.BlockSpec((tm, d), lambda i: (prev(i), 0)),
            pl.BlockSpec((tm, RWKV_W), lambda i: (prev(i), 0)),
            pl.BlockSpec((tm, d), lambda i: (prev(i), 3)),
            pl.BlockSpec((tm, d), lambda i: (prev(i), 4)),
            pl.BlockSpec((tm, LRU_W), lambda i: (here(i), 4)),
            pl.BlockSpec((tm, LRU_W), lambda i: (here(i), 5)),
            const(xs.shape), const(o_rs.shape), const(o_gs.shape),
            pl.BlockSpec((ms, d), lambda i: (0, 3), pipeline_mode=pl.Buffered(1)),
            pl.BlockSpec((ms, d), lambda i: (0, 4), pipeline_mode=pl.Buffered(1)),
            const(lp.shape), const(wgate.shape), const(w_r.shape), const(w_g.shape), const(w_o.shape),
            const(fg.shape),
        ],
        out_specs=(pl.BlockSpec((tm, d), lambda i: (prev(i), 0)),
                   pl.BlockSpec((1, 1, LRU_W), lambda i: (here(i) // tiles_per_seq, 0, 0)),
                   pl.BlockSpec((ms, d), lambda i: (0, 0))),
        scratch_shapes=[pltpu.VMEM((tm, LRU_W), BF16),
                        pltpu.VMEM((SUBLANES + tm, LRU_W), F32),
                        pltpu.VMEM((SUBLANES, LRU_W), F32)],
        compiler_params=pltpu.CompilerParams(
            dimension_semantics=("arbitrary",), vmem_limit_bytes=VMEM_LIMIT),
        name="outproj_lru",
    )(x, o_r, z, z, z, z, xs, o_rs, o_gs, zs, zs, lp, wgate, w_r, w_g, w_o, fg)


def _row_tile(m, want):
    t = min(m, want)
    assert m % t == 0, (m, t)
    return t


def _pack_params_kernel(mu_ref, w0_ref, a0_ref, kk_ref, ka_ref, rk_ref, lng_ref, lnb_ref, wdu_ref,
                        wau_ref, cw_ref, cb_ref, gxb_ref, gab_ref, lam_ref, gxw_ref, gaw_ref,
                        pvec_ref, mul_ref, wd_ref, wa_ref, e_ref, lp_ref, wg_ref):
    pvec_ref[...] = jnp.zeros_like(pvec_ref)
    for i in range(3):
        pvec_ref[_MU_R + i:_MU_R + i + 1, :] = mu_ref[:, RWKV_W * i:RWKV_W * (i + 1)]
    for row, ref in ((_W0, w0_ref), (_A0, a0_ref), (_KK, kk_ref), (_KA, ka_ref), (_RK, rk_ref),
                     (_LNG, lng_ref), (_LNB, lnb_ref)):
        pvec_ref[row:row + 1, :] = ref[...]
    mul_ref[...] = jnp.broadcast_to(mu_ref[:, 3 * RWKV_W:3 * RWKV_W + 2 * LORA], mul_ref.shape)

    zeros = jnp.zeros((LORA, RWKV_W), BF16)
    wd_ref[0:LORA, :] = _bf(wdu_ref[...])
    wd_ref[LORA:2 * LORA, :] = zeros
    wa_ref[0:LORA, :] = zeros
    wa_ref[LORA:2 * LORA, :] = _bf(wau_ref[...])

    ri = lax.broadcasted_iota(jnp.int32, (LANES, LANES), 0)
    ci = lax.broadcasted_iota(jnp.int32, (LANES, LANES), 1)
    e_ref[...] = jnp.where((ri < HEAD) == (ci < HEAD), 1.0, 0.0).astype(BF16)

    lp_ref[_CW0:_CW0 + CONV_W, :] = cw_ref[...]
    for row, ref in ((_CB, cb_ref), (_GXB, gxb_ref), (_GAB, gab_ref), (_LAM, lam_ref)):
        lp_ref[row:row + 1, :] = ref[...]

    blk = LRU_W // LRU_BLOCKS
    z = jnp.zeros((blk, blk), F32)
    for g in range(LRU_BLOCKS // 2):
        top = jnp.concatenate([gxw_ref[2 * g], z, gaw_ref[2 * g], z], axis=1)
        bot = jnp.concatenate([z, gxw_ref[2 * g + 1], z, gaw_ref[2 * g + 1]], axis=1)
        wg_ref[g] = _bf(jnp.concatenate([top, bot], axis=0))


def _pack_params(l, rwkv_mu, w_decay0, w_decay_up, w_iclr0, w_iclr_up, k_k, k_a, r_k, ln_x_g,
                 ln_x_b, conv_w, conv_b, lru_gx_w, lru_gx_b, lru_ga_w, lru_ga_b, lru_lambda):
    blk = LRU_W // LRU_BLOCKS
    assert 2 * blk == LANES and 2 * LORA == LANES
    depth = rwkv_mu.shape[0]
    row = lambda a: pl.BlockSpec((1, a.shape[-1]), lambda i: (l, 0))
    mat = lambda a: pl.BlockSpec((None,) + a.shape[1:], lambda i: (l,) + (0,) * (a.ndim - 1))
    full = lambda shp: pl.BlockSpec(shp, lambda i: (0,) * len(shp))
    rk = r_k.reshape(depth, RWKV_W)
    rows = (rwkv_mu, w_decay0, w_iclr0, k_k, k_a, rk, ln_x_g, ln_x_b)
    out_shapes = ((16, RWKV_W, F32), (SUBLANES, LANES, F32), (LANES, RWKV_W, BF16), (LANES, RWKV_W, BF16),
                  (LANES, LANES, BF16), (SUBLANES, LRU_W, F32))
    outs = tuple(jax.ShapeDtypeStruct(s[:2], s[2]) for s in out_shapes)
    outs += (jax.ShapeDtypeStruct((LRU_BLOCKS // 2, LANES, 2 * LANES), BF16),)
    return pl.pallas_call(
        _pack_params_kernel,
        out_shape=outs,
        grid=(1,),
        in_specs=[row(a) for a in rows] + [mat(w_decay_up), mat(w_iclr_up), mat(conv_w), row(conv_b),
                                          row(lru_gx_b), row(lru_ga_b), row(lru_lambda),
                                          mat(lru_gx_w), mat(lru_ga_w)],
        out_specs=tuple(full(o.shape) for o in outs),
        compiler_params=pltpu.CompilerParams(
            dimension_semantics=("arbitrary",), vmem_limit_bytes=VMEM_LIMIT),
        name="pack_params",
    )(*rows, w_decay_up, w_iclr_up, conv_w, conv_b, lru_gx_b, lru_ga_b, lru_lambda, lru_gx_w, lru_ga_w)


def kernel(x_prompt, x_sample, state_shift, state_wkv, state_conv, state_lru, norm_g, w_in, rwkv_mu,
           w_decay0, w_decay_up, w_iclr0, w_iclr_up, k_k, k_a, r_k, ln_x_g, ln_x_b, w_out_rwkv,
           conv_w, conv_b, lru_gx_w, lru_gx_b, lru_ga_w, lru_ga_b, lru_lambda, w_out_lru, w_out,
           final_norm_g):
    bp, seq, d = x_prompt.shape
    bs = x_sample.shape[0]
    assert x_sample.shape[1] == 1 and seq % WKV_CHUNK == 0
    depth = w_in.shape[0]
    xp = x_prompt.reshape(bp * seq, d)
    xs = x_sample.reshape(bs, d)
    fg = final_norm_g.reshape(1, d)
    outs = [[] for _ in range(8)]
    for l in range(depth):
        pvec, mul, wd, wa, e, lp, wg = _pack_params(
            l, rwkv_mu, w_decay0, w_decay_up, w_iclr0, w_iclr_up, k_k, k_a, r_k, ln_x_g, ln_x_b,
            conv_w, conv_b, lru_gx_w, lru_gx_b, lru_ga_w, lru_ga_b, lru_lambda)
        w_r, w_g, w_o = w_out_rwkv[l].astype(BF16), w_out_lru[l].astype(BF16), w_out[l].astype(BF16)
        g = norm_g[l].reshape(1, d)
        rec = (pvec, mul, wd, wa, e)
        w = _repack_w_in(w_in, l, _row_tile(d, 256))
        zp, zlp, zs, zls = _inproj(xp, xs, g, w, _row_tile(bp * seq, 1024), INPROJ_TN)

        s0t = jnp.transpose(state_wkv[l], (1, 2, 3, 0))
        o_rs, s_new, sh_new = _wkv_step(zs, zls, state_shift[l], s0t, *rec)
        conv = state_conv[l].reshape(bs, (CONV_W - 1) * LRU_W)
        o_gs, h_new, conv_new = _lru_step(zs, conv, state_lru[l], lp, wg)
        outs[4].append(sh_new)
        outs[5].append(jnp.transpose(s_new, (3, 0, 1, 2)))
        outs[6].append(conv_new.reshape(bs, CONV_W - 1, LRU_W))
        outs[7].append(h_new)

        zp3 = zp.reshape(bp, seq, -1)
        zlp3 = zlp.reshape(bp, seq, LANES)
        nb = max(n for n in (4, 2, 1) if bp % n == 0)
        o_r, s_new = _wkv_chunk(zp3, zlp3, *rec, bp, seq, nb)
        o_r = o_r.reshape(bp * seq, RWKV_W)
        last = l == depth - 1
        xp, h_last, xs = _outproj_lru(xp, o_r, zp, xs, o_rs, o_gs, zs, lp, wg, w_r, w_g, w_o, fg,
                                      _row_tile(seq, 256), seq, last)
        outs[0].append(jnp.concatenate([zp3[:, -1, :SHIFT_MAIN], zlp3[:, -1]], axis=-1))
        outs[1].append(s_new)
        outs[2].append(zp3[:, seq - (CONV_W - 1):, 4 * RWKV_W:4 * RWKV_W + LRU_W])
        outs[3].append(h_last.reshape(bp, LRU_W))

    return (xp.reshape(bp, seq, d), xs.reshape(bs, 1, d)) + tuple(jnp.stack(o) for o in outs)
```

```python
import functools

import jax
import jax.numpy as jnp
from jax import lax
from jax.experimental import pallas as pl
from jax.experimental.pallas import tpu as pltpu

F32 = jnp.float32
BF16 = jnp.bfloat16

HEADS = 16
HEAD = 64
RWKV_W = HEADS * HEAD
LORA = 64
LRU_W = 1024
LRU_BLOCKS = 16
CONV_W = 4
LRU_C = 8.0
RMS_EPS = 1e-6
GN_EPS = 1e-5 * HEAD
DECAY_SCALE = 0.6065306597126334

LANES = 128
SUBLANES = 8
WKV_CHUNK = 64
VMEM_LIMIT = 56 * 1024 * 1024

NN = (((1,), (0,)), ((), ()))
NT = (((1,), (1,)), ((), ()))
TN = (((0,), (0,)), ((), ()))


def _bf(x):
    return x.astype(BF16)


def _dg(a, b, dn):
    return lax.dot_general(a, b, dn, preferred_element_type=F32)


def _softplus(x):
    return jnp.maximum(x, 0.0) + jnp.log1p(jnp.exp(-jnp.abs(x)))


def _sigmoid(x):
    return 1.0 / (1.0 + jnp.exp(-x))


def _segsum(x, e):
    rows, n = x.shape[0], x.shape[1] // LANES
    stacked = jnp.concatenate([x[:, LANES * j:LANES * (j + 1)] for j in range(n)], axis=0)
    s = _dg(_bf(stacked), e, NN)
    return jnp.concatenate([s[rows * j:rows * (j + 1), :] for j in range(n)], axis=1)


def _rms(x, g):
    return x * lax.rsqrt(jnp.mean(x * x, axis=-1, keepdims=True) + RMS_EPS) * g


SHIFT_MAIN = 3 * RWKV_W
LORA_COL = 10 * RWKV_W
LORA_BLOCK = LORA_COL // LANES
INPROJ_TN = 1024


def _repack_kernel(w_ref, o_ref):
    n = w_ref.shape[-1]
    lora_end = SHIFT_MAIN + 2 * LORA
    o_ref[:, 0:SHIFT_MAIN] = _bf(w_ref[:, 0:SHIFT_MAIN])
    o_ref[:, SHIFT_MAIN:LORA_COL] = _bf(w_ref[:, lora_end:n])
    o_ref[:, LORA_COL:n] = _bf(w_ref[:, SHIFT_MAIN:lora_end])


def _repack_w_in(w_in, layer, tr):
    _, d, n = w_in.shape
    assert n == LORA_COL + 2 * LORA
    return pl.pallas_call(
        _repack_kernel,
        out_shape=jax.ShapeDtypeStruct((d, n), BF16),
        grid=(d // tr,),
        in_specs=[pl.BlockSpec((None, tr, n), lambda i: (layer, i, 0))],
        out_specs=pl.BlockSpec((tr, n), lambda i: (i, 0)),
        compiler_params=pltpu.CompilerParams(
            dimension_semantics=("arbitrary",), vmem_limit_bytes=VMEM_LIMIT),
        name="repack_w_in",
    )(w_in)


def _inproj_kernel(x_ref, xs_ref, g_ref, w_ref, wl_ref, z_ref, zl_ref, zs_ref, zls_ref, h_ref, hs_ref):
    i = pl.program_id(0)
    j = pl.program_id(1)

    @pl.when((i == 0) & (j == 0))
    def _():
        hs = _bf(_rms(xs_ref[...], g_ref[...]))
        hs_ref[...] = hs
        zls_ref[...] = _dg(hs, wl_ref[...], NN)

    @pl.when(j == 0)
    def _():
        h = _bf(_rms(x_ref[...], g_ref[...]))
        h_ref[...] = h
        zl_ref[...] = _dg(h, wl_ref[...], NN)

    z_ref[...] = _dg(h_ref[...], w_ref[...], NN)

    @pl.when(i == 0)
    def _():
        zs_ref[...] = _dg(hs_ref[...], w_ref[...], NN)


def _inproj(x, xs, g, w, tm, tn):
    m, d = x.shape
    ms = xs.shape[0]
    nj = LORA_COL // tn
    park = lambda i, j: jnp.where(i == 0, j, nj - 1)
    return pl.pallas_call(
        _inproj_kernel,
        out_shape=(jax.ShapeDtypeStruct((m, LORA_COL), F32), jax.ShapeDtypeStruct((m, LANES), F32),
                   jax.ShapeDtypeStruct((ms, LORA_COL), F32), jax.ShapeDtypeStruct((ms, LANES), F32)),
        grid=(m // tm, nj),
        in_specs=[
            pl.BlockSpec((tm, d), lambda i, j: (i, 0)),
            pl.BlockSpec((ms, d), lambda i, j: (0, 0), pipeline_mode=pl.Buffered(1)),
            pl.BlockSpec((1, d), lambda i, j: (0, 0), pipeline_mode=pl.Buffered(1)),
            pl.BlockSpec((d, tn), lambda i, j: (0, j)),
            pl.BlockSpec((d, LANES), lambda i, j: (0, LORA_BLOCK), pipeline_mode=pl.Buffered(1)),
        ],
        out_specs=(
            pl.BlockSpec((tm, tn), lambda i, j: (i, j)),
            pl.BlockSpec((tm, LANES), lambda i, j: (i, 0)),
            pl.BlockSpec((ms, tn), lambda i, j: (0, park(i, j))),
            pl.BlockSpec((ms, LANES), lambda i, j: (0, 0)),
        ),
        scratch_shapes=[pltpu.VMEM((tm, d), BF16), pltpu.VMEM((ms, d), BF16)],
        compiler_params=pltpu.CompilerParams(
            dimension_semantics=("arbitrary", "arbitrary"), vmem_limit_bytes=VMEM_LIMIT),
        name="inproj",
    )(x, xs, g, w, w)


_MU_R, _MU_K, _MU_V, _W0, _A0, _KK, _KA, _RK, _LNG, _LNB = range(10)


def _prow(pv_ref, i):
    return pv_ref[i:i + 1, :]


def _wkv_prep(zr, zk, zv, zl, pr, pk, pv, pl_, pv_ref, mul_ref, wd_ref, wa_ref, e):
    r = zr + _prow(pv_ref, _MU_R) * (pr - zr)
    k = zk + _prow(pv_ref, _MU_K) * (pk - zk)
    v = zv + _prow(pv_ref, _MU_V) * (pv - zv)
    lo = zl + mul_ref[0:1, :] * (pl_ - zl)
    lw = _dg(_bf(jnp.tanh(lo)), wd_ref[...], NN)
    la = _dg(_bf(lo), wa_ref[...], NN)
    logd = -DECAY_SCALE * _sigmoid(_prow(pv_ref, _W0) + lw)
    a = _sigmoid(_prow(pv_ref, _A0) + la)
    kk = k * _prow(pv_ref, _KK)
    kk = kk * lax.rsqrt(jnp.maximum(_segsum(kk * kk, e), 1e-24))
    k2 = k * (1.0 + (a - 1.0) * _prow(pv_ref, _KA))
    return r, k2, v, -kk, kk * a, logd


def _wkv_bonus_gate(r, k2, v, zrg, pv_ref, e):
    return _segsum(r * k2 * _prow(pv_ref, _RK), e) * v, zrg * _sigmoid(zrg)


def _wkv_norm_gate(y, bonus_v, gate, pv_ref, e):
    mu = _segsum(y, e) * (1.0 / HEAD)
    yc = y - mu
    var = _segsum(yc * yc, e) * (1.0 / HEAD)
    yn = yc * lax.rsqrt(var + GN_EPS) * _prow(pv_ref, _LNG) + _prow(pv_ref, _LNB)
    return _bf((yn + bonus_v) * gate)


def _wkv_post(y, r, k2, v, zrg, pv_ref, e):
    bonus_v, gate = _wkv_bonus_gate(r, k2, v, zrg, pv_ref, e)
    return _wkv_norm_gate(y, bonus_v, gate, pv_ref, e)


def _wkv_chunk_kernel(zr_ref, zk_ref, zv_ref, zrg_ref, zl_ref, pv_ref, mul_ref, wd_ref, wa_ref,
                      e_ref, o_ref, sout_ref, nsh_ref, s_s, prev_s, prevl_s):
    c = pl.program_id(1)
    nc = pl.num_programs(1)
    C = WKV_CHUNK
    assert C == HEAD and 2 * HEAD == LANES
    nb = zr_ref.shape[0]
    rows_all = nb * C
    seqs = range(nb)

    @pl.when(c == 0)
    def _():
        s_s[...] = jnp.zeros_like(s_s)
        prev_s[...] = jnp.zeros_like(prev_s)
        prevl_s[...] = jnp.zeros_like(prevl_s)

    first = lax.broadcasted_iota(jnp.int32, (SUBLANES, 1), 0) == 0

    def shifted(z, prev_ref, lanes):
        rolled = pltpu.roll(z, 1, 0)
        pieces = []
        for b in seqs:
            head = jnp.where(first, prev_ref[b, 0:1, lanes], rolled[b * C:b * C + SUBLANES, :])
            pieces += [head, rolled[b * C + SUBLANES:(b + 1) * C, :]]
        return jnp.concatenate(pieces, axis=0)

    def flat(ref):
        return ref[...].reshape(rows_all, ref.shape[-1])

    zr, zk, zv, zl = flat(zr_ref), flat(zk_ref), flat(zv_ref), flat(zl_ref)
    seg = [slice(RWKV_W * i, RWKV_W * (i + 1)) for i in range(3)]
    pr = shifted(zr, prev_s, seg[0])
    pk = shifted(zk, prev_s, seg[1])
    pv = shifted(zv, prev_s, seg[2])
    pl_ = shifted(zl, prevl_s, slice(0, LANES))
    for b in seqs:
        last = slice(b * C + C - 1, b * C + C)
        prev_s[b, 0:1, seg[0]] = zr[last, :]
        prev_s[b, 0:1, seg[1]] = zk[last, :]
        prev_s[b, 0:1, seg[2]] = zv[last, :]
        prevl_s[b, 0:1, :] = zl[last, :]

    e = e_ref[...]
    r, k2, v, av, bv, logd = _wkv_prep(zr, zk, zv, zl, pr, pk, pv, pl_, pv_ref, mul_ref,
                                       wd_ref, wa_ref, e)

    ti = lax.broadcasted_iota(jnp.int32, (rows_all, rows_all), 0)
    tj = lax.broadcasted_iota(jnp.int32, (rows_all, rows_all), 1)
    tri = jnp.where((ti >= tj) & ((ti & -C) == (tj & -C)), 1.0, 0.0).astype(BF16)
    d_hi = _bf(logd)
    d_r1 = logd - d_hi.astype(F32)
    d_mid = _bf(d_r1)
    d_lo = _bf(d_r1 - d_mid.astype(F32))
    cum = _dg(tri, d_hi, NN) + (_dg(tri, d_mid, NN) + _dg(tri, d_lo, NN))
    e_in = jnp.exp(cum)
    e_neg = jnp.exp(-cum)
    a_t = av * jnp.exp(cum - logd)
    r_t = r * e_in
    k_t = k2 * e_neg
    b_t = bv * e_neg
    p_c = [jnp.exp(cum[b * C + C - 1:b * C + C, :]) for b in seqs]

    lane = lax.broadcasted_iota(jnp.int32, (C, LANES), 1)
    trow = lax.broadcasted_iota(jnp.int32, (C, LANES), 0)
    lo = lane < HEAD
    s_in = lane & (HEAD - 1)
    strict = s_in < trow
    incl2 = ((lax.broadcasted_iota(jnp.int32, (C, 2 * LANES), 1) & (HEAD - 1))
             <= lax.broadcasted_iota(jnp.int32, (C, 2 * LANES), 0))
    eye2 = jnp.where(s_in == trow, 1.0, 0.0).astype(F32)
    vrow = lax.broadcasted_iota(jnp.int32, (2 * HEAD, LANES), 0)
    klane = lax.broadcasted_iota(jnp.int32, (2 * HEAD, LANES), 1)
    same_head = (vrow < HEAD) == (klane < HEAD)

    def bd(x):
        z = jnp.zeros_like(x)
        return jnp.concatenate([jnp.where(lo, x, z), jnp.where(lo, z, x)], axis=0)

    npair = HEADS // 2
    units = [(b, p) for b in seqs for p in range(npair)]
    un = range(len(units))
    blk = lambda arr, i: arr[units[i][0] * C:(units[i][0] + 1) * C, LANES * units[i][1]:LANES * (units[i][1] + 1)]
    ar = [_bf(jnp.concatenate([blk(a_t, i), blk(r_t, i)], axis=0)) for i in un]
    bk = [_bf(jnp.concatenate([bd(blk(b_t, i)), bd(blk(k_t, i))], axis=0)) for i in un]
    g = [_dg(ar[i], bk[i], NT) for i in un]
    s0 = [s_s[i] for i in un]
    ars = [_dg(ar[i], _bf(s0[i]), NT) for i in un]
    vbd = [_bf(bd(blk(v, i))) for i in un]
    x = [jnp.where(strict, g[i][0:C, 0:LANES], 0.0) for i in un]
    ak = [jnp.where(strict, g[i][0:C, LANES:2 * LANES], 0.0) for i in un]
    w = [ars[i][0:C, :] + _dg(_bf(ak[i]), vbd[i], NN) for i in un]
    t = [eye2 + x[i] for i in un]
    x = [_dg(_bf(x[i]), _bf(bd(x[i])), NN) for i in un]
    for _ in range(C.bit_length() - 3):
        xt = [_dg(_bf(jnp.concatenate([x[i], t[i]], axis=0)), _bf(bd(x[i])), NN) for i in un]
        x = [xt[i][0:C, :] for i in un]
        t = [t[i] + xt[i][C:2 * C, :] for i in un]
    t = [t[i] + _dg(_bf(t[i]), _bf(bd(x[i])), NN) for i in un]
    u = [_dg(_bf(t[i]), _bf(bd(w[i])), NN) for i in un]
    rbk = [_bf(jnp.where(incl2, g[i][C:2 * C, :], 0.0)) for i in un]
    uvbd = [jnp.concatenate([_bf(bd(u[i])), vbd[i]], axis=0) for i in un]
    y = [ars[i][C:2 * C, :] + _dg(rbk[i], uvbd[i], NN) for i in un]
    uv = [_bf(jnp.concatenate([u[i], blk(v, i)], axis=0)) for i in un]
    pc = [p_c[units[i][0]][:, LANES * units[i][1]:LANES * (units[i][1] + 1)] for i in un]
    bkh = [_bf(jnp.concatenate([blk(b_t, i), blk(k_t, i)], axis=0) * pc[i]) for i in un]
    s1 = [s0[i] * pc[i] + jnp.where(same_head, _dg(uv[i], bkh[i], TN), 0.0) for i in un]
    for i in un:
        s_s[i] = s1[i]

    y_all = jnp.concatenate(
        [jnp.concatenate(y[b * npair:(b + 1) * npair], axis=1) for b in seqs], axis=0)
    o = _wkv_post(y_all, r, k2, v, flat(zrg_ref), pv_ref, e)
    o_ref[...] = o.reshape(nb, C, RWKV_W)

    @pl.when(c == nc - 1)
    def _():
        for i in un:
            b, p = units[i]
            sout_ref[b, 2 * p] = s1[i][0:HEAD, 0:HEAD]
            sout_ref[b, 2 * p + 1] = s1[i][HEAD:2 * HEAD, HEAD:2 * HEAD]
        for b in seqs:
            for q, ref in enumerate((zr_ref, zk_ref, zv_ref)):
                nsh_ref[0, b:b + 1, RWKV_W * q:RWKV_W * (q + 1)] = ref[b, C - 1:C, :]
            nsh_ref[0, b:b + 1, SHIFT_MAIN:SHIFT_MAIN + LANES] = zl_ref[b, C - 1:C, :]


def _wkv_chunk(z, zl, pvec, mul, wd, wa, e, batch, seq, nb):
    C = WKV_CHUNK
    nc = seq // C
    full = lambda shp: pl.BlockSpec(shp, lambda b, c: (0,) * len(shp))
    col = lambda j: pl.BlockSpec((nb, C, RWKV_W), lambda b, c, j=j: (b, c, j))
    return pl.pallas_call(
        _wkv_chunk_kernel,
        out_shape=(jax.ShapeDtypeStruct((batch, seq, RWKV_W), BF16),
                   jax.ShapeDtypeStruct((batch, HEADS, HEAD, HEAD), F32),
                   jax.ShapeDtypeStruct((batch // nb, nb, SHIFT_MAIN + LANES), F32)),
        grid=(batch // nb, nc),
        in_specs=[col(0), col(1), col(2), col(3),
                  pl.BlockSpec((nb, C, LANES), lambda b, c: (b, c, 0)),
                  full(pvec.shape), full(mul.shape), full(wd.shape), full(wa.shape), full(e.shape)],
        out_specs=(pl.BlockSpec((nb, C, RWKV_W), lambda b, c: (b, c, 0)),
                   pl.BlockSpec((nb, HEADS, HEAD, HEAD), lambda b, c: (b, 0, 0, 0)),
                   pl.BlockSpec((1, nb, SHIFT_MAIN + LANES), lambda b, c: (b, 0, 0))),
        scratch_shapes=[pltpu.VMEM((nb * HEADS // 2, 2 * HEAD, 2 * HEAD), F32),
                        pltpu.VMEM((nb, SUBLANES, 3 * RWKV_W), F32),
                        pltpu.VMEM((nb, SUBLANES, LANES), F32)],
        compiler_params=pltpu.CompilerParams(
            dimension_semantics=("arbitrary", "arbitrary"), vmem_limit_bytes=VMEM_LIMIT),
        name="wkv_chunk",
    )(z, z, z, z, zl, pvec, mul, wd, wa, e)


def _wkv_step_kernel(zr_ref, zk_ref, zv_ref, zrg_ref, zl_ref, sh_ref, s0_ref, pv_ref,
                     mul_ref, wd_ref, wa_ref, e_ref, o_ref, sout_ref, nsh_ref,
                     at_s, drt_s, bt_s, kt_s, dt_s, vt_s, brt_s, krt_s, yt_s, keep_s):
    h = pl.program_id(0)
    nh = pl.num_programs(0)
    nseq = zr_ref.shape[0]

    @pl.when(h == 0)
    def _():
        e = e_ref[...]
        r, k2, v, av, bv, logd = _wkv_prep(
            zr_ref[...], zk_ref[...], zv_ref[...], zl_ref[...],
            sh_ref[:, 0:RWKV_W], sh_ref[:, RWKV_W:2 * RWKV_W], sh_ref[:, 2 * RWKV_W:3 * RWKV_W],
            sh_ref[:, SHIFT_MAIN:SHIFT_MAIN + LANES], pv_ref, mul_ref, wd_ref, wa_ref, e)
        nsh_ref[:, 0:RWKV_W] = zr_ref[...]
        nsh_ref[:, RWKV_W:2 * RWKV_W] = zk_ref[...]
        nsh_ref[:, 2 * RWKV_W:SHIFT_MAIN] = zv_ref[...]
        nsh_ref[:, SHIFT_MAIN:SHIFT_MAIN + LANES] = zl_ref[...]
        d = jnp.exp(logd)
        at_s[...] = av.T
        drt_s[...] = (d * r).T
        bt_s[...] = bv.T
        kt_s[...] = k2.T
        dt_s[...] = d.T
        vt_s[...] = v.T
        brt_s[...] = jnp.sum((bv * r).T.reshape(HEADS, HEAD, nseq), axis=1)
        krt_s[...] = jnp.sum((k2 * r).T.reshape(HEADS, HEAD, nseq), axis=1)
        keep_s[0] = r
        keep_s[1] = k2
        keep_s[2] = v

    base = pl.multiple_of(h * HEAD, HEAD)
    rows = pl.ds(base, HEAD)
    a_h, dr_h, b_h, k_h, d_h = at_s[rows, :], drt_s[rows, :], bt_s[rows, :], kt_s[rows, :], dt_s[rows, :]
    br_h = brt_s[pl.ds(h, 1), :]
    kr_h = krt_s[pl.ds(h, 1), :]

    def value_rows(g, carry):
        off = pl.multiple_of(base + g * SUBLANES, SUBLANES)
        v8 = vt_s[pl.ds(off, SUBLANES), :]
        ys = []
        for j in range(SUBLANES):
            vi = g * SUBLANES + j
            s_v = s0_ref[0, vi]
            sa = jnp.sum(s_v * a_h, axis=0, keepdims=True)
            y0 = jnp.sum(s_v * dr_h, axis=0, keepdims=True)
            v_v = v8[j:j + 1, :]
            sout_ref[0, vi] = s_v * d_h + sa * b_h + v_v * k_h
            ys.append(y0 + sa * br_h + v_v * kr_h)
        yt_s[pl.ds(off, SUBLANES), :] = jnp.concatenate(ys, axis=0)
        return carry

    lax.fori_loop(0, HEAD // SUBLANES, value_rows, 0)

    @pl.when(h == nh - 1)
    def _():
        o_ref[...] = _wkv_post(yt_s[...].T, keep_s[0], keep_s[1], keep_s[2], zrg_ref[...], pv_ref,
                               e_ref[...])


def _wkv_step(z, zl, sh, s0t, pvec, mul, wd, wa, e):
    nseq = z.shape[0]
    full = lambda shp: pl.BlockSpec(shp, lambda i: (0,) * len(shp))
    col = lambda j: pl.BlockSpec((nseq, RWKV_W), lambda i, j=j: (0, j))
    st_block = (1, HEAD, HEAD, nseq)
    wide = pltpu.VMEM((RWKV_W, nseq), F32)
    return pl.pallas_call(
        _wkv_step_kernel,
        out_shape=(jax.ShapeDtypeStruct((nseq, RWKV_W), BF16),
                   jax.ShapeDtypeStruct(s0t.shape, F32),
                   jax.ShapeDtypeStruct(sh.shape, F32)),
        grid=(HEADS,),
        in_specs=[col(0), col(1), col(2), col(3),
                  full(zl.shape), full(sh.shape),
                  pl.BlockSpec(st_block, lambda i: (i, 0, 0, 0)),
                  full(pvec.shape), full(mul.shape), full(wd.shape), full(wa.shape), full(e.shape)],
        out_specs=(full((nseq, RWKV_W)),
                   pl.BlockSpec(st_block, lambda i: (i, 0, 0, 0)),
                   full(sh.shape)),
        scratch_shapes=[wide] * 6 + [pltpu.VMEM((HEADS, nseq), F32)] * 2
                       + [wide, pltpu.VMEM((3, nseq, RWKV_W), F32)],
        compiler_params=pltpu.CompilerParams(
            dimension_semantics=("arbitrary",), vmem_limit_bytes=VMEM_LIMIT),
        name="wkv_step",
    )(z, z, z, z, zl, sh, s0t, pvec, mul, wd, wa, e)


_CW0, _CW1, _CW2, _CW3, _CB, _GXB, _GAB, _LAM = range(8)


def _lru_gates(xc, lp_ref, wg_ref):
    xb = _bf(xc)
    ngroups = wg_ref.shape[0]
    gs = [_dg(xb[:, LANES * g:LANES * (g + 1)], wg_ref[g], NN) for g in range(ngroups)]
    gx_pre = jnp.concatenate([gs[g][:, 0:LANES] for g in range(ngroups)], axis=1)
    ga_pre = jnp.concatenate([gs[g][:, LANES:2 * LANES] for g in range(ngroups)], axis=1)
    gx = _sigmoid(gx_pre + _prow(lp_ref, _GXB))
    ga = _sigmoid(ga_pre + _prow(lp_ref, _GAB))
    log_a = -LRU_C * ga * _softplus(-_prow(lp_ref, _LAM))
    a = jnp.exp(log_a)
    mult = jnp.sqrt((1.0 - a) * (1.0 + a))
    return a, mult * gx * xc


def _lru_coeffs(zx, first, xb_s, lp_ref, wgate_ref):
    tl = zx.shape[0]
    xb_s[0:SUBLANES, :] = jnp.where(first, 0.0, xb_s[0:SUBLANES, :])
    xb_s[SUBLANES:SUBLANES + tl, :] = zx
    xc = _prow(lp_ref, _CW3) * zx + _prow(lp_ref, _CB)
    for j in range(1, CONV_W):
        xc = xc + _prow(lp_ref, CONV_W - 1 - j) * xb_s[SUBLANES - j:SUBLANES - j + tl, :]
    xb_s[0:SUBLANES, :] = zx[tl - SUBLANES:tl, :]
    return _lru_gates(xc, lp_ref, wgate_ref)


def _lru_scan_rows(a, b, zg, hc):
    row8 = lax.broadcasted_iota(jnp.int32, (SUBLANES, 1), 0)
    hs = []
    for i in range(a.shape[0] // SUBLANES):
        a8 = a[SUBLANES * i:SUBLANES * (i + 1), :]
        b8 = b[SUBLANES * i:SUBLANES * (i + 1), :]
        for s in (1, 2, 4):
            keep = row8 >= s
            b8 = jnp.where(keep, a8 * pltpu.roll(b8, s, 0) + b8, b8)
            a8 = jnp.where(keep, a8 * pltpu.roll(a8, s, 0), a8)
        hb = b8 + a8 * hc
        hs.append(hb)
        hc = jnp.broadcast_to(hb[SUBLANES - 1:SUBLANES, :], hb.shape)
    return _bf(jnp.concatenate(hs, axis=0) * (zg * _sigmoid(zg))), hc


def _lru_step_kernel(zx_ref, zg_ref, conv_ref, h0_ref, lp_ref, wg_ref, o_ref, hnew_ref, cnew_ref):
    zx = zx_ref[...]
    keep = (CONV_W - 2) * LRU_W
    cnew_ref[:, 0:keep] = conv_ref[:, LRU_W:LRU_W + keep]
    cnew_ref[:, keep:keep + LRU_W] = zx
    xc = _prow(lp_ref, _CW3) * zx + _prow(lp_ref, _CB)
    for j in range(CONV_W - 1):
        xc = xc + _prow(lp_ref, j) * conv_ref[:, LRU_W * j:LRU_W * (j + 1)]
    a, b = _lru_gates(xc, lp_ref, wg_ref)
    h = a * h0_ref[...] + b
    hnew_ref[...] = h
    zg = zg_ref[...]
    o_ref[...] = _bf(h * (zg * _sigmoid(zg)))


def _lru_step(z_main, conv, h0, lp, wg):
    nb = z_main.shape[0]
    full = lambda shp: pl.BlockSpec(shp, lambda i: (0,) * len(shp))
    col = lambda j: pl.BlockSpec((nb, LRU_W), lambda i, j=j: (0, j))
    return pl.pallas_call(
        _lru_step_kernel,
        out_shape=(jax.ShapeDtypeStruct((nb, LRU_W), BF16), jax.ShapeDtypeStruct((nb, LRU_W), F32),
                   jax.ShapeDtypeStruct(conv.shape, F32)),
        grid=(1,),
        in_specs=[col(4), col(5), full(conv.shape), full(h0.shape), full(lp.shape), full(wg.shape)],
        out_specs=(full((nb, LRU_W)), full((nb, LRU_W)), full(conv.shape)),
        compiler_params=pltpu.CompilerParams(
            dimension_semantics=("arbitrary",), vmem_limit_bytes=VMEM_LIMIT),
        name="lru_step",
    )(z_main, z_main, conv, h0, lp, wg)


def _project(x, o_r, o_g, m_r, m_g, wr_ref, wg_ref, wo_ref, fg_ref, final):
    y_r = _dg(o_r, wr_ref[...], NN)
    y_g = _dg(o_g, wg_ref[...], NN)
    merged = _sigmoid(m_r) * y_r + _sigmoid(m_g) * y_g
    out = x + _dg(_bf(merged), wo_ref[...], NN)
    return _rms(out, fg_ref[...]) if final else out


def _outproj_lru_kernel(x_ref, or_ref, mr_ref, mg_ref, zx_ref, zg_ref, xs_ref, ors_ref, ogs_ref,
                        mrs_ref, mgs_ref, lp_ref, wgate_ref, wr_ref, wg_ref, wo_ref, fg_ref,
                        out_ref, hlast_ref, cnew_ref, outs_ref, og_s, xb_s, hc_s, *, final, tiles_per_seq):
    i = pl.program_id(0)
    n = pl.num_programs(0) - 1

    @pl.when(i == 0)
    def _():
        og_s[...] = jnp.zeros_like(og_s)
        xb_s[...] = jnp.zeros_like(xb_s)
        hc_s[...] = jnp.zeros_like(hc_s)
        outs_ref[...] = _project(xs_ref[...], ors_ref[...], ogs_ref[...], mrs_ref[...], mgs_ref[...],
                                 wr_ref, wg_ref, wo_ref, fg_ref, final)

    og_prev = og_s[...]
    t = lax.rem(jnp.minimum(i, n - 1), tiles_per_seq)
    a, b = _lru_coeffs(zx_ref[...], t == 0, xb_s, lp_ref, wgate_ref)
    out_ref[...] = _project(x_ref[...], or_ref[...], og_prev, mr_ref[...], mg_ref[...],
                            wr_ref, wg_ref, wo_ref, fg_ref, final)
    o_g, hc = _lru_scan_rows(a, b, zg_ref[...], jnp.where(t == 0, 0.0, hc_s[...]))
    hc_s[...] = hc
    og_s[...] = o_g

    @pl.when((t == tiles_per_seq - 1) & (i < n))
    def _():
        hlast_ref[0] = hc[0:1, :]
        tm = zx_ref.shape[0]
        cnew_ref[0] = zx_ref[tm - (CONV_W - 1):tm, :]


def _outproj_lru(x, o_r, z, xs, o_rs, o_gs, zs, lp, wgate, w_r, w_g, w_o, fg, tm, seq, final):
    m, d = x.shape
    ms = xs.shape[0]
    n = m // tm
    tiles_per_seq = seq // tm
    const = lambda shp: pl.BlockSpec(shp, lambda i: (0,) * len(shp), pipeline_mode=pl.Buffered(1))
    prev = lambda i: jnp.maximum(i - 1, 0)
    here = lambda i: jnp.minimum(i, n - 1)
    return pl.pallas_call(
        functools.partial(_outproj_lru_kernel, final=final, tiles_per_seq=tiles_per_seq),
        out_shape=(jax.ShapeDtypeStruct((m, d), F32),
                   jax.ShapeDtypeStruct((m // seq, 1, LRU_W), F32),
                   jax.ShapeDtypeStruct((m // seq, CONV_W - 1, LRU_W), F32),
                   jax.ShapeDtypeStruct((ms, d), F32)),
        grid=(n + 1,),
        in_specs=[
            pl.BlockSpec((tm, d), lambda i: (prev(i), 0)),
            pl.BlockSpec((tm, RWKV_W), lambda i: (prev(i), 0)),
            pl.BlockSpec((tm, d), lambda i: (prev(i), 3)),
            pl.BlockSpec((tm, d), lambda i: (prev(i), 4)),
            pl.BlockSpec((tm, LRU_W), lambda i: (here(i), 4)),
            pl.BlockSpec((tm, LRU_W), lambda i: (here(i), 5)),
            const(xs.shape), const(o_rs.shape), const(o_gs.shape),
            pl.BlockSpec((ms, d), lambda i: (0, 3), pipeline_mode=pl.Buffered(1)),
            pl.BlockSpec((ms, d), lambda i: (0, 4), pipeline_mode=pl.Buffered(1)),
            const(lp.shape), const(wgate.shape), const(w_r.shape), const(w_g.shape), const(w_o.shape),
            const(fg.shape),
        ],
        out_specs=(pl.BlockSpec((tm, d), lambda i: (prev(i), 0)),
                   pl.BlockSpec((1, 1, LRU_W), lambda i: (here(i) // tiles_per_seq, 0, 0)),
                   pl.BlockSpec((1, CONV_W - 1, LRU_W), lambda i: (here(i) // tiles_per_seq, 0, 0)),
                   pl.BlockSpec((ms, d), lambda i: (0, 0))),
        scratch_shapes=[pltpu.VMEM((tm, LRU_W), BF16),
                        pltpu.VMEM((SUBLANES + tm, LRU_W), F32),
                        pltpu.VMEM((SUBLANES, LRU_W), F32)],
        compiler_params=pltpu.CompilerParams(
            dimension_semantics=("arbitrary",), vmem_limit_bytes=VMEM_LIMIT),
        name="outproj_lru",
    )(x, o_r, z, z, z, z, xs, o_rs, o_gs, zs, zs, lp, wgate, w_r, w_g, w_o, fg)


def _row_tile(m, want):
    t = min(m, want)
    assert m % t == 0, (m, t)
    return t


def _pack_params_kernel(mu_ref, w0_ref, a0_ref, kk_ref, ka_ref, rk_ref, lng_ref, lnb_ref, wdu_ref,
                        wau_ref, cw_ref, cb_ref, gxb_ref, gab_ref, lam_ref, gxw_ref, gaw_ref,
                        pvec_ref, mul_ref, wd_ref, wa_ref, e_ref, lp_ref, wg_ref):
    pvec_ref[...] = jnp.zeros_like(pvec_ref)
    for i in range(3):
        pvec_ref[_MU_R + i:_MU_R + i + 1, :] = mu_ref[:, RWKV_W * i:RWKV_W * (i + 1)]
    for row, ref in ((_W0, w0_ref), (_A0, a0_ref), (_KK, kk_ref), (_KA, ka_ref), (_RK, rk_ref),
                     (_LNG, lng_ref), (_LNB, lnb_ref)):
        pvec_ref[row:row + 1, :] = ref[...]
    mul_ref[...] = jnp.broadcast_to(mu_ref[:, 3 * RWKV_W:3 * RWKV_W + 2 * LORA], mul_ref.shape)

    zeros = jnp.zeros((LORA, RWKV_W), BF16)
    wd_ref[0:LORA, :] = _bf(wdu_ref[...])
    wd_ref[LORA:2 * LORA, :] = zeros
    wa_ref[0:LORA, :] = zeros
    wa_ref[LORA:2 * LORA, :] = _bf(wau_ref[...])

    ri = lax.broadcasted_iota(jnp.int32, (LANES, LANES), 0)
    ci = lax.broadcasted_iota(jnp.int32, (LANES, LANES), 1)
    e_ref[...] = jnp.where((ri < HEAD) == (ci < HEAD), 1.0, 0.0).astype(BF16)

    lp_ref[_CW0:_CW0 + CONV_W, :] = cw_ref[...]
    for row, ref in ((_CB, cb_ref), (_GXB, gxb_ref), (_GAB, gab_ref), (_LAM, lam_ref)):
        lp_ref[row:row + 1, :] = ref[...]

    blk = LRU_W // LRU_BLOCKS
    z = jnp.zeros((blk, blk), F32)
    for g in range(LRU_BLOCKS // 2):
        top = jnp.concatenate([gxw_ref[2 * g], z, gaw_ref[2 * g], z], axis=1)
        bot = jnp.concatenate([z, gxw_ref[2 * g + 1], z, gaw_ref[2 * g + 1]], axis=1)
        wg_ref[g] = _bf(jnp.concatenate([top, bot], axis=0))


def _pack_params(l, rwkv_mu, w_decay0, w_decay_up, w_iclr0, w_iclr_up, k_k, k_a, r_k, ln_x_g,
                 ln_x_b, conv_w, conv_b, lru_gx_w, lru_gx_b, lru_ga_w, lru_ga_b, lru_lambda):
    blk = LRU_W // LRU_BLOCKS
    assert 2 * blk == LANES and 2 * LORA == LANES
    depth = rwkv_mu.shape[0]
    row = lambda a: pl.BlockSpec((1, a.shape[-1]), lambda i: (l, 0))
    mat = lambda a: pl.BlockSpec((None,) + a.shape[1:], lambda i: (l,) + (0,) * (a.ndim - 1))
    full = lambda shp: pl.BlockSpec(shp, lambda i: (0,) * len(shp))
    rk = r_k.reshape(depth, RWKV_W)
    rows = (rwkv_mu, w_decay0, w_iclr0, k_k, k_a, rk, ln_x_g, ln_x_b)
    out_shapes = ((16, RWKV_W, F32), (SUBLANES, LANES, F32), (LANES, RWKV_W, BF16), (LANES, RWKV_W, BF16),
                  (LANES, LANES, BF16), (SUBLANES, LRU_W, F32))
    outs = tuple(jax.ShapeDtypeStruct(s[:2], s[2]) for s in out_shapes)
    outs += (jax.ShapeDtypeStruct((LRU_BLOCKS // 2, LANES, 2 * LANES), BF16),)
    return pl.pallas_call(
        _pack_params_kernel,
        out_shape=outs,
        grid=(1,),
        in_specs=[row(a) for a in rows] + [mat(w_decay_up), mat(w_iclr_up), mat(conv_w), row(conv_b),
                                          row(lru_gx_b), row(lru_ga_b), row(lru_lambda),
                                          mat(lru_gx_w), mat(lru_ga_w)],
        out_specs=tuple(full(o.shape) for o in outs),
        compiler_params=pltpu.CompilerParams(
            dimension_semantics=("arbitrary",), vmem_limit_bytes=VMEM_LIMIT),
        name="pack_params",
    )(*rows, w_decay_up, w_iclr_up, conv_w, conv_b, lru_gx_b, lru_ga_b, lru_lambda, lru_gx_w, lru_ga_w)


def kernel(x_prompt, x_sample, state_shift, state_wkv, state_conv, state_lru, norm_g, w_in, rwkv_mu,
           w_decay0, w_decay_up, w_iclr0, w_iclr_up, k_k, k_a, r_k, ln_x_g, ln_x_b, w_out_rwkv,
           conv_w, conv_b, lru_gx_w, lru_gx_b, lru_ga_w, lru_ga_b, lru_lambda, w_out_lru, w_out,
           final_norm_g):
    bp, seq, d = x_prompt.shape
    bs = x_sample.shape[0]
    assert x_sample.shape[1] == 1 and seq % WKV_CHUNK == 0
    depth = w_in.shape[0]
    xp = x_prompt.reshape(bp * seq, d)
    xs = x_sample.reshape(bs, d)
    fg = final_norm_g.reshape(1, d)
    outs = [[] for _ in range(8)]
    for l in range(depth):
        pvec, mul, wd, wa, e, lp, wg = _pack_params(
            l, rwkv_mu, w_decay0, w_decay_up, w_iclr0, w_iclr_up, k_k, k_a, r_k, ln_x_g, ln_x_b,
            conv_w, conv_b, lru_gx_w, lru_gx_b, lru_ga_w, lru_ga_b, lru_lambda)
        w_r, w_g, w_o = w_out_rwkv[l].astype(BF16), w_out_lru[l].astype(BF16), w_out[l].astype(BF16)
        g = norm_g[l].reshape(1, d)
        rec = (pvec, mul, wd, wa, e)
        w = _repack_w_in(w_in, l, _row_tile(d, 256))
        zp, zlp, zs, zls = _inproj(xp, xs, g, w, _row_tile(bp * seq, 1024), INPROJ_TN)

        s0t = jnp.transpose(state_wkv[l], (1, 2, 3, 0))
        o_rs, s_new, sh_new = _wkv_step(zs, zls, state_shift[l], s0t, *rec)
        conv = state_conv[l].reshape(bs, (CONV_W - 1) * LRU_W)
        o_gs, h_new, conv_new = _lru_step(zs, conv, state_lru[l], lp, wg)
        outs[4].append(sh_new)
        outs[5].append(jnp.transpose(s_new, (3, 0, 1, 2)))
        outs[6].append(conv_new.reshape(bs, CONV_W - 1, LRU_W))
        outs[7].append(h_new)

        zp3 = zp.reshape(bp, seq, -1)
        zlp3 = zlp.reshape(bp, seq, LANES)
        nb = max(n for n in (4, 2, 1) if bp % n == 0)
        o_r, s_new, sh_last = _wkv_chunk(zp3, zlp3, *rec, bp, seq, nb)
        o_r = o_r.reshape(bp * seq, RWKV_W)
        last = l == depth - 1
        xp, h_last, conv_last, xs = _outproj_lru(xp, o_r, zp, xs, o_rs, o_gs, zs, lp, wg, w_r, w_g, w_o,
                                                 fg, _row_tile(seq, 256), seq, last)
        outs[0].append(sh_last.reshape(bp, -1))
        outs[1].append(s_new)
        outs[2].append(conv_last)
        outs[3].append(h_last.reshape(bp, LRU_W))

    return (xp.reshape(bp, seq, d), xs.reshape(bs, 1, d)) + tuple(jnp.stack(o) for o in outs)
```

```python
import functools

import jax
import jax.numpy as jnp
from jax import lax
from jax.experimental import pallas as pl
from jax.experimental.pallas import tpu as pltpu

F32 = jnp.float32
BF16 = jnp.bfloat16

HEADS = 16
HEAD = 64
RWKV_W = HEADS * HEAD
LORA = 64
LRU_W = 1024
LRU_BLOCKS = 16
CONV_W = 4
LRU_C = 8.0
RMS_EPS = 1e-6
GN_EPS = 1e-5 * HEAD
DECAY_SCALE = 0.6065306597126334

LANES = 128
SUBLANES = 8
WKV_CHUNK = 64
VMEM_LIMIT = 56 * 1024 * 1024

NN = (((1,), (0,)), ((), ()))
NT = (((1,), (1,)), ((), ()))
TN = (((0,), (0,)), ((), ()))


def _bf(x):
    return x.astype(BF16)


def _dg(a, b, dn):
    return lax.dot_general(a, b, dn, preferred_element_type=F32)


def _softplus(x):
    return jnp.maximum(x, 0.0) + jnp.log1p(jnp.exp(-jnp.abs(x)))


def _sigmoid(x):
    return 1.0 / (1.0 + jnp.exp(-x))


def _segsum(x, e):
    rows, n = x.shape[0], x.shape[1] // LANES
    stacked = jnp.concatenate([x[:, LANES * j:LANES * (j + 1)] for j in range(n)], axis=0)
    s = _dg(_bf(stacked), e, NN)
    return jnp.concatenate([s[rows * j:rows * (j + 1), :] for j in range(n)], axis=1)


def _rms(x, g):
    return x * lax.rsqrt(jnp.mean(x * x, axis=-1, keepdims=True) + RMS_EPS) * g


SHIFT_MAIN = 3 * RWKV_W
LORA_COL = 10 * RWKV_W
LORA_BLOCK = LORA_COL // LANES
FUSE_PIECES = 8
INPROJ_TN = 1024


def _repack_kernel(w_ref, o_ref):
    n = w_ref.shape[-1]
    lora_end = SHIFT_MAIN + 2 * LORA
    o_ref[:, 0:SHIFT_MAIN] = _bf(w_ref[:, 0:SHIFT_MAIN])
    o_ref[:, SHIFT_MAIN:LORA_COL] = _bf(w_ref[:, lora_end:n])
    o_ref[:, LORA_COL:n] = _bf(w_ref[:, SHIFT_MAIN:lora_end])


def _repack_w_in(w_in, layer, tr):
    _, d, n = w_in.shape
    assert n == LORA_COL + 2 * LORA
    return pl.pallas_call(
        _repack_kernel,
        out_shape=jax.ShapeDtypeStruct((d, n), BF16),
        grid=(d // tr,),
        in_specs=[pl.BlockSpec((None, tr, n), lambda i: (layer, i, 0))],
        out_specs=pl.BlockSpec((tr, n), lambda i: (i, 0)),
        compiler_params=pltpu.CompilerParams(
            dimension_semantics=("arbitrary",), vmem_limit_bytes=VMEM_LIMIT),
        name="repack_w_in",
    )(w_in)


def _inproj_kernel(x_ref, xs_ref, g_ref, w_ref, wl_ref, z_ref, zl_ref, zs_ref, zls_ref, h_ref, hs_ref):
    i = pl.program_id(0)
    j = pl.program_id(1)

    @pl.when((i == 0) & (j == 0))
    def _():
        hs = _bf(_rms(xs_ref[...], g_ref[...]))
        hs_ref[...] = hs
        zls_ref[...] = _dg(hs, wl_ref[...], NN)

    @pl.when(j == 0)
    def _():
        h = _bf(_rms(x_ref[...], g_ref[...]))
        h_ref[...] = h
        zl_ref[...] = _dg(h, wl_ref[...], NN)

    z_ref[...] = _dg(h_ref[...], w_ref[...], NN)

    @pl.when(i == 0)
    def _():
        zs_ref[...] = _dg(hs_ref[...], w_ref[...], NN)


def _inproj(x, xs, g, w, tm, tn):
    m, d = x.shape
    ms = xs.shape[0]
    nj = LORA_COL // tn
    park = lambda i, j: jnp.where(i == 0, j, nj - 1)
    return pl.pallas_call(
        _inproj_kernel,
        out_shape=(jax.ShapeDtypeStruct((m, LORA_COL), F32), jax.ShapeDtypeStruct((m, LANES), F32),
                   jax.ShapeDtypeStruct((ms, LORA_COL), F32), jax.ShapeDtypeStruct((ms, LANES), F32)),
        grid=(m // tm, nj),
        in_specs=[
            pl.BlockSpec((tm, d), lambda i, j: (i, 0)),
            pl.BlockSpec((ms, d), lambda i, j: (0, 0), pipeline_mode=pl.Buffered(1)),
            pl.BlockSpec((1, d), lambda i, j: (0, 0), pipeline_mode=pl.Buffered(1)),
            pl.BlockSpec((d, tn), lambda i, j: (0, j)),
            pl.BlockSpec((d, LANES), lambda i, j: (0, LORA_BLOCK), pipeline_mode=pl.Buffered(1)),
        ],
        out_specs=(
            pl.BlockSpec((tm, tn), lambda i, j: (i, j)),
            pl.BlockSpec((tm, LANES), lambda i, j: (i, 0)),
            pl.BlockSpec((ms, tn), lambda i, j: (0, park(i, j))),
            pl.BlockSpec((ms, LANES), lambda i, j: (0, 0)),
        ),
        scratch_shapes=[pltpu.VMEM((tm, d), BF16), pltpu.VMEM((ms, d), BF16)],
        compiler_params=pltpu.CompilerParams(
            dimension_semantics=("arbitrary", "arbitrary"), vmem_limit_bytes=VMEM_LIMIT),
        name="inproj",
    )(x, xs, g, w, w)


_MU_R, _MU_K, _MU_V, _W0, _A0, _KK, _KA, _RK, _LNG, _LNB = range(10)


def _prow(pv_ref, i):
    return pv_ref[i:i + 1, :]


def _wkv_prep(zr, zk, zv, zl, pr, pk, pv, pl_, pv_ref, mul_ref, wd_ref, wa_ref, e):
    r = zr + _prow(pv_ref, _MU_R) * (pr - zr)
    k = zk + _prow(pv_ref, _MU_K) * (pk - zk)
    v = zv + _prow(pv_ref, _MU_V) * (pv - zv)
    lo = zl + mul_ref[0:1, :] * (pl_ - zl)
    lw = _dg(_bf(jnp.tanh(lo)), wd_ref[...], NN)
    la = _dg(_bf(lo), wa_ref[...], NN)
    logd = -DECAY_SCALE * _sigmoid(_prow(pv_ref, _W0) + lw)
    a = _sigmoid(_prow(pv_ref, _A0) + la)
    kk = k * _prow(pv_ref, _KK)
    kk = kk * lax.rsqrt(jnp.maximum(_segsum(kk * kk, e), 1e-24))
    k2 = k * (1.0 + (a - 1.0) * _prow(pv_ref, _KA))
    return r, k2, v, -kk, kk * a, logd


def _wkv_bonus_gate(r, k2, v, zrg, pv_ref, e):
    return _segsum(r * k2 * _prow(pv_ref, _RK), e) * v, zrg * _sigmoid(zrg)


def _wkv_norm_gate(y, bonus_v, gate, pv_ref, e):
    mu = _segsum(y, e) * (1.0 / HEAD)
    yc = y - mu
    var = _segsum(yc * yc, e) * (1.0 / HEAD)
    yn = yc * lax.rsqrt(var + GN_EPS) * _prow(pv_ref, _LNG) + _prow(pv_ref, _LNB)
    return _bf((yn + bonus_v) * gate)


def _wkv_post(y, r, k2, v, zrg, pv_ref, e):
    bonus_v, gate = _wkv_bonus_gate(r, k2, v, zrg, pv_ref, e)
    return _wkv_norm_gate(y, bonus_v, gate, pv_ref, e)


def _wkv_chunk_kernel(zr_ref, zk_ref, zv_ref, zrg_ref, zl_ref, pv_ref, mul_ref, wd_ref, wa_ref,
                      e_ref, o_ref, sout_ref, nsh_ref, s_s, prev_s, prevl_s):
    c = pl.program_id(1)
    nc = pl.num_programs(1)
    C = WKV_CHUNK
    assert C == HEAD and 2 * HEAD == LANES
    nb = zr_ref.shape[0]
    rows_all = nb * C
    seqs = range(nb)

    @pl.when(c == 0)
    def _():
        s_s[...] = jnp.zeros_like(s_s)
        prev_s[...] = jnp.zeros_like(prev_s)
        prevl_s[...] = jnp.zeros_like(prevl_s)

    first = lax.broadcasted_iota(jnp.int32, (SUBLANES, 1), 0) == 0

    def shifted(z, prev_ref, lanes):
        rolled = pltpu.roll(z, 1, 0)
        pieces = []
        for b in seqs:
            head = jnp.where(first, prev_ref[b, 0:1, lanes], rolled[b * C:b * C + SUBLANES, :])
            pieces += [head, rolled[b * C + SUBLANES:(b + 1) * C, :]]
        return jnp.concatenate(pieces, axis=0)

    def flat(ref):
        return ref[...].reshape(rows_all, ref.shape[-1])

    zr, zk, zv, zl = flat(zr_ref), flat(zk_ref), flat(zv_ref), flat(zl_ref)
    seg = [slice(RWKV_W * i, RWKV_W * (i + 1)) for i in range(3)]
    pr = shifted(zr, prev_s, seg[0])
    pk = shifted(zk, prev_s, seg[1])
    pv = shifted(zv, prev_s, seg[2])
    pl_ = shifted(zl, prevl_s, slice(0, LANES))
    for b in seqs:
        last = slice(b * C + C - 1, b * C + C)
        prev_s[b, 0:1, seg[0]] = zr[last, :]
        prev_s[b, 0:1, seg[1]] = zk[last, :]
        prev_s[b, 0:1, seg[2]] = zv[last, :]
        prevl_s[b, 0:1, :] = zl[last, :]

    e = e_ref[...]
    r, k2, v, av, bv, logd = _wkv_prep(zr, zk, zv, zl, pr, pk, pv, pl_, pv_ref, mul_ref,
                                       wd_ref, wa_ref, e)

    ti = lax.broadcasted_iota(jnp.int32, (rows_all, rows_all), 0)
    tj = lax.broadcasted_iota(jnp.int32, (rows_all, rows_all), 1)
    tri = jnp.where((ti >= tj) & ((ti & -C) == (tj & -C)), 1.0, 0.0).astype(BF16)
    d_hi = _bf(logd)
    d_r1 = logd - d_hi.astype(F32)
    d_mid = _bf(d_r1)
    d_lo = _bf(d_r1 - d_mid.astype(F32))
    cum = _dg(jnp.concatenate([tri, tri, tri], axis=1), jnp.concatenate([d_hi, d_mid, d_lo], axis=0), NN)
    e_in = jnp.exp(cum)
    e_neg = jnp.exp(-cum)
    a_t = av * jnp.exp(cum - logd)
    r_t = r * e_in
    k_t = k2 * e_neg
    b_t = bv * e_neg
    p_c = [jnp.exp(cum[b * C + C - 1:b * C + C, :]) for b in seqs]

    lane = lax.broadcasted_iota(jnp.int32, (C, LANES), 1)
    trow = lax.broadcasted_iota(jnp.int32, (C, LANES), 0)
    lo = lane < HEAD
    s_in = lane & (HEAD - 1)
    strict = s_in < trow
    incl2 = ((lax.broadcasted_iota(jnp.int32, (C, 2 * LANES), 1) & (HEAD - 1))
             <= lax.broadcasted_iota(jnp.int32, (C, 2 * LANES), 0))
    eye2 = jnp.where(s_in == trow, 1.0, 0.0).astype(F32)
    vrow = lax.broadcasted_iota(jnp.int32, (2 * HEAD, LANES), 0)
    klane = lax.broadcasted_iota(jnp.int32, (2 * HEAD, LANES), 1)
    same_head = (vrow < HEAD) == (klane < HEAD)

    def bd(x):
        z = jnp.zeros_like(x)
        return jnp.concatenate([jnp.where(lo, x, z), jnp.where(lo, z, x)], axis=0)

    npair = HEADS // 2
    units = [(b, p) for b in seqs for p in range(npair)]
    un = range(len(units))
    blk = lambda arr, i: arr[units[i][0] * C:(units[i][0] + 1) * C, LANES * units[i][1]:LANES * (units[i][1] + 1)]
    ar = [_bf(jnp.concatenate([blk(a_t, i), blk(r_t, i)], axis=0)) for i in un]
    bk = [_bf(jnp.concatenate([bd(blk(b_t, i)), bd(blk(k_t, i))], axis=0)) for i in un]
    g = [_dg(ar[i], bk[i], NT) for i in un]
    s0 = [s_s[i] for i in un]
    ars = [_dg(ar[i], _bf(s0[i]), NT) for i in un]
    vbd = [_bf(bd(blk(v, i))) for i in un]
    x = [jnp.where(strict, g[i][0:C, 0:LANES], 0.0) for i in un]
    ak = [jnp.where(strict, g[i][0:C, LANES:2 * LANES], 0.0) for i in un]
    w = [ars[i][0:C, :] + _dg(_bf(ak[i]), vbd[i], NN) for i in un]
    t = [eye2 + x[i] for i in un]
    x = [_dg(_bf(x[i]), _bf(bd(x[i])), NN) for i in un]
    for _ in range(C.bit_length() - 3):
        xt = [_dg(_bf(jnp.concatenate([x[i], t[i]], axis=0)), _bf(bd(x[i])), NN) for i in un]
        x = [xt[i][0:C, :] for i in un]
        t = [t[i] + xt[i][C:2 * C, :] for i in un]
    t = [t[i] + _dg(_bf(t[i]), _bf(bd(x[i])), NN) for i in un]
    u = [_dg(_bf(t[i]), _bf(bd(w[i])), NN) for i in un]
    rbk = [_bf(jnp.where(incl2, g[i][C:2 * C, :], 0.0)) for i in un]
    uvbd = [jnp.concatenate([_bf(bd(u[i])), vbd[i]], axis=0) for i in un]
    y = [ars[i][C:2 * C, :] + _dg(rbk[i], uvbd[i], NN) for i in un]
    uv = [_bf(jnp.concatenate([u[i], blk(v, i)], axis=0)) for i in un]
    pc = [p_c[units[i][0]][:, LANES * units[i][1]:LANES * (units[i][1] + 1)] for i in un]
    bkh = [_bf(jnp.concatenate([blk(b_t, i), blk(k_t, i)], axis=0) * pc[i]) for i in un]
    s1 = [s0[i] * pc[i] + jnp.where(same_head, _dg(uv[i], bkh[i], TN), 0.0) for i in un]
    for i in un:
        s_s[i] = s1[i]

    y_all = jnp.concatenate(
        [jnp.concatenate(y[b * npair:(b + 1) * npair], axis=1) for b in seqs], axis=0)
    o = _wkv_post(y_all, r, k2, v, flat(zrg_ref), pv_ref, e)
    o_ref[...] = o.reshape(nb, C, RWKV_W)

    @pl.when(c == nc - 1)
    def _():
        for i in un:
            b, p = units[i]
            sout_ref[b, 2 * p] = s1[i][0:HEAD, 0:HEAD]
            sout_ref[b, 2 * p + 1] = s1[i][HEAD:2 * HEAD, HEAD:2 * HEAD]
        for b in seqs:
            for q, ref in enumerate((zr_ref, zk_ref, zv_ref)):
                nsh_ref[0, b:b + 1, RWKV_W * q:RWKV_W * (q + 1)] = ref[b, C - 1:C, :]
            nsh_ref[0, b:b + 1, SHIFT_MAIN:SHIFT_MAIN + LANES] = zl_ref[b, C - 1:C, :]


def _wkv_chunk(z, zl, pvec, mul, wd, wa, e, batch, seq, nb):
    C = WKV_CHUNK
    nc = seq // C
    full = lambda shp: pl.BlockSpec(shp, lambda b, c: (0,) * len(shp))
    col = lambda j: pl.BlockSpec((nb, C, RWKV_W), lambda b, c, j=j: (b, c, j))
    return pl.pallas_call(
        _wkv_chunk_kernel,
        out_shape=(jax.ShapeDtypeStruct((batch, seq, RWKV_W), BF16),
                   jax.ShapeDtypeStruct((batch, HEADS, HEAD, HEAD), F32),
                   jax.ShapeDtypeStruct((batch // nb, nb, SHIFT_MAIN + LANES), F32)),
        grid=(batch // nb, nc),
        in_specs=[col(0), col(1), col(2), col(3),
                  pl.BlockSpec((nb, C, LANES), lambda b, c: (b, c, 0)),
                  full(pvec.shape), full(mul.shape), full(wd.shape), full(wa.shape), full(e.shape)],
        out_specs=(pl.BlockSpec((nb, C, RWKV_W), lambda b, c: (b, c, 0)),
                   pl.BlockSpec((nb, HEADS, HEAD, HEAD), lambda b, c: (b, 0, 0, 0)),
                   pl.BlockSpec((1, nb, SHIFT_MAIN + LANES), lambda b, c: (b, 0, 0))),
        scratch_shapes=[pltpu.VMEM((nb * HEADS // 2, 2 * HEAD, 2 * HEAD), F32),
                        pltpu.VMEM((nb, SUBLANES, 3 * RWKV_W), F32),
                        pltpu.VMEM((nb, SUBLANES, LANES), F32)],
        compiler_params=pltpu.CompilerParams(
            dimension_semantics=("arbitrary", "arbitrary"), vmem_limit_bytes=VMEM_LIMIT),
        name="wkv_chunk",
    )(z, z, z, z, zl, pvec, mul, wd, wa, e)


def _wkv_step_kernel(zr_ref, zk_ref, zv_ref, zrg_ref, zl_ref, sh_ref, s0_ref, pv_ref,
                     mul_ref, wd_ref, wa_ref, e_ref, o_ref, sout_ref, nsh_ref,
                     at_s, drt_s, bt_s, kt_s, dt_s, vt_s, brt_s, krt_s, yt_s, keep_s):
    h = pl.program_id(0)
    nh = pl.num_programs(0)
    nseq = zr_ref.shape[0]

    @pl.when(h == 0)
    def _():
        e = e_ref[...]
        r, k2, v, av, bv, logd = _wkv_prep(
            zr_ref[...], zk_ref[...], zv_ref[...], zl_ref[...],
            sh_ref[:, 0:RWKV_W], sh_ref[:, RWKV_W:2 * RWKV_W], sh_ref[:, 2 * RWKV_W:3 * RWKV_W],
            sh_ref[:, SHIFT_MAIN:SHIFT_MAIN + LANES], pv_ref, mul_ref, wd_ref, wa_ref, e)
        nsh_ref[:, 0:RWKV_W] = zr_ref[...]
        nsh_ref[:, RWKV_W:2 * RWKV_W] = zk_ref[...]
        nsh_ref[:, 2 * RWKV_W:SHIFT_MAIN] = zv_ref[...]
        nsh_ref[:, SHIFT_MAIN:SHIFT_MAIN + LANES] = zl_ref[...]
        d = jnp.exp(logd)
        at_s[...] = av.T
        drt_s[...] = (d * r).T
        bt_s[...] = bv.T
        kt_s[...] = k2.T
        dt_s[...] = d.T
        vt_s[...] = v.T
        brt_s[...] = jnp.sum((bv * r).T.reshape(HEADS, HEAD, nseq), axis=1)
        krt_s[...] = jnp.sum((k2 * r).T.reshape(HEADS, HEAD, nseq), axis=1)
        keep_s[0] = r
        keep_s[1] = k2
        keep_s[2] = v

    base = pl.multiple_of(h * HEAD, HEAD)
    rows = pl.ds(base, HEAD)
    a_h, dr_h, b_h, k_h, d_h = at_s[rows, :], drt_s[rows, :], bt_s[rows, :], kt_s[rows, :], dt_s[rows, :]
    br_h = brt_s[pl.ds(h, 1), :]
    kr_h = krt_s[pl.ds(h, 1), :]

    def value_rows(g, carry):
        off = pl.multiple_of(base + g * SUBLANES, SUBLANES)
        v8 = vt_s[pl.ds(off, SUBLANES), :]
        ys = []
        for j in range(SUBLANES):
            vi = g * SUBLANES + j
            s_v = s0_ref[0, vi]
            sa = jnp.sum(s_v * a_h, axis=0, keepdims=True)
            y0 = jnp.sum(s_v * dr_h, axis=0, keepdims=True)
            v_v = v8[j:j + 1, :]
            sout_ref[0, vi] = s_v * d_h + sa * b_h + v_v * k_h
            ys.append(y0 + sa * br_h + v_v * kr_h)
        yt_s[pl.ds(off, SUBLANES), :] = jnp.concatenate(ys, axis=0)
        return carry

    lax.fori_loop(0, HEAD // SUBLANES, value_rows, 0)

    @pl.when(h == nh - 1)
    def _():
        o_ref[...] = _wkv_post(yt_s[...].T, keep_s[0], keep_s[1], keep_s[2], zrg_ref[...], pv_ref,
                               e_ref[...])


def _wkv_step(z, zl, sh, s0t, pvec, mul, wd, wa, e):
    nseq = z.shape[0]
    full = lambda shp: pl.BlockSpec(shp, lambda i: (0,) * len(shp))
    col = lambda j: pl.BlockSpec((nseq, RWKV_W), lambda i, j=j: (0, j))
    st_block = (1, HEAD, HEAD, nseq)
    wide = pltpu.VMEM((RWKV_W, nseq), F32)
    return pl.pallas_call(
        _wkv_step_kernel,
        out_shape=(jax.ShapeDtypeStruct((nseq, RWKV_W), BF16),
                   jax.ShapeDtypeStruct(s0t.shape, F32),
                   jax.ShapeDtypeStruct(sh.shape, F32)),
        grid=(HEADS,),
        in_specs=[col(0), col(1), col(2), col(3),
                  full(zl.shape), full(sh.shape),
                  pl.BlockSpec(st_block, lambda i: (i, 0, 0, 0)),
                  full(pvec.shape), full(mul.shape), full(wd.shape), full(wa.shape), full(e.shape)],
        out_specs=(full((nseq, RWKV_W)),
                   pl.BlockSpec(st_block, lambda i: (i, 0, 0, 0)),
                   full(sh.shape)),
        scratch_shapes=[wide] * 6 + [pltpu.VMEM((HEADS, nseq), F32)] * 2
                       + [wide, pltpu.VMEM((3, nseq, RWKV_W), F32)],
        compiler_params=pltpu.CompilerParams(
            dimension_semantics=("arbitrary",), vmem_limit_bytes=VMEM_LIMIT),
        name="wkv_step",
    )(z, z, z, z, zl, sh, s0t, pvec, mul, wd, wa, e)


_CW0, _CW1, _CW2, _CW3, _CB, _GXB, _GAB, _LAM = range(8)


def _lru_gates(xc, lp_ref, wg_ref):
    xb = _bf(xc)
    ngroups = wg_ref.shape[0]
    gs = [_dg(xb[:, LANES * g:LANES * (g + 1)], wg_ref[g], NN) for g in range(ngroups)]
    gx_pre = jnp.concatenate([gs[g][:, 0:LANES] for g in range(ngroups)], axis=1)
    ga_pre = jnp.concatenate([gs[g][:, LANES:2 * LANES] for g in range(ngroups)], axis=1)
    gx = _sigmoid(gx_pre + _prow(lp_ref, _GXB))
    ga = _sigmoid(ga_pre + _prow(lp_ref, _GAB))
    log_a = -LRU_C * ga * _softplus(-_prow(lp_ref, _LAM))
    a = jnp.exp(log_a)
    mult = jnp.sqrt((1.0 - a) * (1.0 + a))
    return a, mult * gx * xc


def _lru_scan_rows(a, b, zg, hc):
    row8 = lax.broadcasted_iota(jnp.int32, (SUBLANES, 1), 0)
    hs = []
    for i in range(a.shape[0] // SUBLANES):
        a8 = a[SUBLANES * i:SUBLANES * (i + 1), :]
        b8 = b[SUBLANES * i:SUBLANES * (i + 1), :]
        for s in (1, 2, 4):
            keep = row8 >= s
            b8 = jnp.where(keep, a8 * pltpu.roll(b8, s, 0) + b8, b8)
            a8 = jnp.where(keep, a8 * pltpu.roll(a8, s, 0), a8)
        hb = b8 + a8 * hc
        hs.append(hb)
        hc = jnp.broadcast_to(hb[SUBLANES - 1:SUBLANES, :], hb.shape)
    return _bf(jnp.concatenate(hs, axis=0) * (zg * _sigmoid(zg))), hc


def _lru_step_kernel(zx_ref, zg_ref, conv_ref, h0_ref, lp_ref, wg_ref, o_ref, hnew_ref, cnew_ref):
    zx = zx_ref[...]
    keep = (CONV_W - 2) * LRU_W
    cnew_ref[:, 0:keep] = conv_ref[:, LRU_W:LRU_W + keep]
    cnew_ref[:, keep:keep + LRU_W] = zx
    xc = _prow(lp_ref, _CW3) * zx + _prow(lp_ref, _CB)
    for j in range(CONV_W - 1):
        xc = xc + _prow(lp_ref, j) * conv_ref[:, LRU_W * j:LRU_W * (j + 1)]
    a, b = _lru_gates(xc, lp_ref, wg_ref)
    h = a * h0_ref[...] + b
    hnew_ref[...] = h
    zg = zg_ref[...]
    o_ref[...] = _bf(h * (zg * _sigmoid(zg)))


def _lru_step(z_main, conv, h0, lp, wg):
    nb = z_main.shape[0]
    full = lambda shp: pl.BlockSpec(shp, lambda i: (0,) * len(shp))
    col = lambda j: pl.BlockSpec((nb, LRU_W), lambda i, j=j: (0, j))
    return pl.pallas_call(
        _lru_step_kernel,
        out_shape=(jax.ShapeDtypeStruct((nb, LRU_W), BF16), jax.ShapeDtypeStruct((nb, LRU_W), F32),
                   jax.ShapeDtypeStruct(conv.shape, F32)),
        grid=(1,),
        in_specs=[col(4), col(5), full(conv.shape), full(h0.shape), full(lp.shape), full(wg.shape)],
        out_specs=(full((nb, LRU_W)), full((nb, LRU_W)), full(conv.shape)),
        compiler_params=pltpu.CompilerParams(
            dimension_semantics=("arbitrary",), vmem_limit_bytes=VMEM_LIMIT),
        name="lru_step",
    )(z_main, z_main, conv, h0, lp, wg)


def _project(x, o_r, o_g, m_r, m_g, wr_ref, wg_ref, wo_ref, fg_ref, final):
    y_r = _dg(o_r, wr_ref[...], NN)
    y_g = _dg(o_g, wg_ref[...], NN)
    merged = _sigmoid(m_r) * y_r + _sigmoid(m_g) * y_g
    out = x + _dg(_bf(merged), wo_ref[...], NN)
    return _rms(out, fg_ref[...]) if final else out


def _outproj_lru_kernel(x_ref, or_ref, mr_ref, mg_ref, zx_ref, zg_ref, xs_ref, ors_ref, ogs_ref,
                        mrs_ref, mgs_ref, lp_ref, wgate_ref, wr_ref, wg_ref, wo_ref, fg_ref,
                        out_ref, hlast_ref, cnew_ref, outs_ref, og_s, mg_s, xb_s, hc_s, *, final,
                        tiles_per_seq):
    i = pl.program_id(0)
    n = pl.num_programs(0) - 2
    tm, d = x_ref.shape

    @pl.when(i == 0)
    def _():
        og_s[...] = jnp.zeros_like(og_s)
        mg_s[...] = jnp.zeros_like(mg_s)
        xb_s[...] = jnp.zeros_like(xb_s)
        hc_s[...] = jnp.zeros_like(hc_s)
        outs_ref[...] = _project(xs_ref[...], ors_ref[...], ogs_ref[...], mrs_ref[...], mgs_ref[...],
                                 wr_ref, wg_ref, wo_ref, fg_ref, final)

    og_prev = og_s[...]
    mg_prev = mg_s[...]
    t = lax.rem(jnp.minimum(i, n - 1), tiles_per_seq)
    first = t == 0

    xb_s[0:SUBLANES, :] = jnp.where(first, 0.0, xb_s[0:SUBLANES, :])
    xb_s[SUBLANES:SUBLANES + tm, :] = zx_ref[...]

    outs, a_parts, b_parts = [], [], []
    for c in range(FUSE_PIECES):
        cs = slice(c * d // FUSE_PIECES, (c + 1) * d // FUSE_PIECES)
        outs.append(x_ref[:, cs] + _dg(mg_prev, wo_ref[:, cs], NN))
        r0, r1 = SUBLANES + c * tm // FUSE_PIECES, SUBLANES + (c + 1) * tm // FUSE_PIECES
        xc = _prow(lp_ref, _CW3) * xb_s[r0:r1, :] + _prow(lp_ref, _CB)
        for j in range(1, CONV_W):
            xc = xc + _prow(lp_ref, CONV_W - 1 - j) * xb_s[r0 - j:r1 - j, :]
        a_c, b_c = _lru_gates(xc, lp_ref, wgate_ref)
        a_parts.append(a_c)
        b_parts.append(b_c)
    out = jnp.concatenate(outs, axis=1)
    out_ref[...] = _rms(out, fg_ref[...]) if final else out
    xb_s[0:SUBLANES, :] = xb_s[tm:tm + SUBLANES, :]

    o_g, hc = _lru_scan_rows(jnp.concatenate(a_parts, axis=0), jnp.concatenate(b_parts, axis=0),
                             zg_ref[...], jnp.where(first, 0.0, hc_s[...]))
    hc_s[...] = hc
    og_s[...] = o_g

    y_r = _dg(or_ref[...], wr_ref[...], NN)
    y_g = _dg(og_prev, wg_ref[...], NN)
    mg_s[...] = _bf(_sigmoid(mr_ref[...]) * y_r + _sigmoid(mg_ref[...]) * y_g)

    @pl.when((t == tiles_per_seq - 1) & (i < n))
    def _():
        hlast_ref[0] = hc[0:1, :]
        cnew_ref[0] = zx_ref[tm - (CONV_W - 1):tm, :]


def _outproj_lru(x, o_r, z, xs, o_rs, o_gs, zs, lp, wgate, w_r, w_g, w_o, fg, tm, seq, final):
    m, d = x.shape
    ms = xs.shape[0]
    n = m // tm
    tiles_per_seq = seq // tm
    const = lambda shp: pl.BlockSpec(shp, lambda i: (0,) * len(shp), pipeline_mode=pl.Buffered(1))
    back = lambda i, k: jnp.clip(i - k, 0, n - 1)
    return pl.pallas_call(
        functools.partial(_outproj_lru_kernel, final=final, tiles_per_seq=tiles_per_seq),
        out_shape=(jax.ShapeDtypeStruct((m, d), F32),
                   jax.ShapeDtypeStruct((m // seq, 1, LRU_W), F32),
                   jax.ShapeDtypeStruct((m // seq, CONV_W - 1, LRU_W), F32),
                   jax.ShapeDtypeStruct((ms, d), F32)),
        grid=(n + 2,),
        in_specs=[
            pl.BlockSpec((tm, d), lambda i: (back(i, 2), 0)),
            pl.BlockSpec((tm, RWKV_W), lambda i: (back(i, 1), 0)),
            pl.BlockSpec((tm, d), lambda i: (back(i, 1), 3)),
            pl.BlockSpec((tm, d), lambda i: (back(i, 1), 4)),
            pl.BlockSpec((tm, LRU_W), lambda i: (back(i, 0), 4)),
            pl.BlockSpec((tm, LRU_W), lambda i: (back(i, 0), 5)),
            const(xs.shape), const(o_rs.shape), const(o_gs.shape),
            pl.BlockSpec((ms, d), lambda i: (0, 3), pipeline_mode=pl.Buffered(1)),
            pl.BlockSpec((ms, d), lambda i: (0, 4), pipeline_mode=pl.Buffered(1)),
            const(lp.shape), const(wgate.shape), const(w_r.shape), const(w_g.shape), const(w_o.shape),
            const(fg.shape),
        ],
        out_specs=(pl.BlockSpec((tm, d), lambda i: (back(i, 2), 0)),
                   pl.BlockSpec((1, 1, LRU_W), lambda i: (back(i, 0) // tiles_per_seq, 0, 0)),
                   pl.BlockSpec((1, CONV_W - 1, LRU_W), lambda i: (back(i, 0) // tiles_per_seq, 0, 0)),
                   pl.BlockSpec((ms, d), lambda i: (0, 0))),
        scratch_shapes=[pltpu.VMEM((tm, LRU_W), BF16),
                        pltpu.VMEM((tm, d), BF16),
                        pltpu.VMEM((SUBLANES + tm, LRU_W), F32),
                        pltpu.VMEM((SUBLANES, LRU_W), F32)],
        compiler_params=pltpu.CompilerParams(
            dimension_semantics=("arbitrary",), vmem_limit_bytes=VMEM_LIMIT),
        name="outproj_lru",
    )(x, o_r, z, z, z, z, xs, o_rs, o_gs, zs, zs, lp, wgate, w_r, w_g, w_o, fg)


def _row_tile(m, want):
    t = min(m, want)
    assert m % t == 0, (m, t)
    return t


def _pack_params_kernel(mu_ref, w0_ref, a0_ref, kk_ref, ka_ref, rk_ref, lng_ref, lnb_ref, wdu_ref,
                        wau_ref, cw_ref, cb_ref, gxb_ref, gab_ref, lam_ref, gxw_ref, gaw_ref,
                        pvec_ref, mul_ref, wd_ref, wa_ref, e_ref, lp_ref, wg_ref):
    pvec_ref[...] = jnp.zeros_like(pvec_ref)
    for i in range(3):
        pvec_ref[_MU_R + i:_MU_R + i + 1, :] = mu_ref[:, RWKV_W * i:RWKV_W * (i + 1)]
    for row, ref in ((_W0, w0_ref), (_A0, a0_ref), (_KK, kk_ref), (_KA, ka_ref), (_RK, rk_ref),
                     (_LNG, lng_ref), (_LNB, lnb_ref)):
        pvec_ref[row:row + 1, :] = ref[...]
    mul_ref[...] = jnp.broadcast_to(mu_ref[:, 3 * RWKV_W:3 * RWKV_W + 2 * LORA], mul_ref.shape)

    zeros = jnp.zeros((LORA, RWKV_W), BF16)
    wd_ref[0:LORA, :] = _bf(wdu_ref[...])
    wd_ref[LORA:2 * LORA, :] = zeros
    wa_ref[0:LORA, :] = zeros
    wa_ref[LORA:2 * LORA, :] = _bf(wau_ref[...])

    ri = lax.broadcasted_iota(jnp.int32, (LANES, LANES), 0)
    ci = lax.broadcasted_iota(jnp.int32, (LANES, LANES), 1)
    e_ref[...] = jnp.where((ri < HEAD) == (ci < HEAD), 1.0, 0.0).astype(BF16)

    lp_ref[_CW0:_CW0 + CONV_W, :] = cw_ref[...]
    for row, ref in ((_CB, cb_ref), (_GXB, gxb_ref), (_GAB, gab_ref), (_LAM, lam_ref)):
        lp_ref[row:row + 1, :] = ref[...]

    blk = LRU_W // LRU_BLOCKS
    z = jnp.zeros((blk, blk), F32)
    for g in range(LRU_BLOCKS // 2):
        top = jnp.concatenate([gxw_ref[2 * g], z, gaw_ref[2 * g], z], axis=1)
        bot = jnp.concatenate([z, gxw_ref[2 * g + 1], z, gaw_ref[2 * g + 1]], axis=1)
        wg_ref[g] = _bf(jnp.concatenate([top, bot], axis=0))


def _pack_params(l, rwkv_mu, w_decay0, w_decay_up, w_iclr0, w_iclr_up, k_k, k_a, r_k, ln_x_g,
                 ln_x_b, conv_w, conv_b, lru_gx_w, lru_gx_b, lru_ga_w, lru_ga_b, lru_lambda):
    blk = LRU_W // LRU_BLOCKS
    assert 2 * blk == LANES and 2 * LORA == LANES
    depth = rwkv_mu.shape[0]
    row = lambda a: pl.BlockSpec((1, a.shape[-1]), lambda i: (l, 0))
    mat = lambda a: pl.BlockSpec((None,) + a.shape[1:], lambda i: (l,) + (0,) * (a.ndim - 1))
    full = lambda shp: pl.BlockSpec(shp, lambda i: (0,) * len(shp))
    rk = r_k.reshape(depth, RWKV_W)
    rows = (rwkv_mu, w_decay0, w_iclr0, k_k, k_a, rk, ln_x_g, ln_x_b)
    out_shapes = ((16, RWKV_W, F32), (SUBLANES, LANES, F32), (LANES, RWKV_W, BF16), (LANES, RWKV_W, BF16),
                  (LANES, LANES, BF16), (SUBLANES, LRU_W, F32))
    outs = tuple(jax.ShapeDtypeStruct(s[:2], s[2]) for s in out_shapes)
    outs += (jax.ShapeDtypeStruct((LRU_BLOCKS // 2, LANES, 2 * LANES), BF16),)
    return pl.pallas_call(
        _pack_params_kernel,
        out_shape=outs,
        grid=(1,),
        in_specs=[row(a) for a in rows] + [mat(w_decay_up), mat(w_iclr_up), mat(conv_w), row(conv_b),
                                          row(lru_gx_b), row(lru_ga_b), row(lru_lambda),
                                          mat(lru_gx_w), mat(lru_ga_w)],
        out_specs=tuple(full(o.shape) for o in outs),
        compiler_params=pltpu.CompilerParams(
            dimension_semantics=("arbitrary",), vmem_limit_bytes=VMEM_LIMIT),
        name="pack_params",
    )(*rows, w_decay_up, w_iclr_up, conv_w, conv_b, lru_gx_b, lru_ga_b, lru_lambda, lru_gx_w, lru_ga_w)


def kernel(x_prompt, x_sample, state_shift, state_wkv, state_conv, state_lru, norm_g, w_in, rwkv_mu,
           w_decay0, w_decay_up, w_iclr0, w_iclr_up, k_k, k_a, r_k, ln_x_g, ln_x_b, w_out_rwkv,
           conv_w, conv_b, lru_gx_w, lru_gx_b, lru_ga_w, lru_ga_b, lru_lambda, w_out_lru, w_out,
           final_norm_g):
    bp, seq, d = x_prompt.shape
    bs = x_sample.shape[0]
    assert x_sample.shape[1] == 1 and seq % WKV_CHUNK == 0
    depth = w_in.shape[0]
    xp = x_prompt.reshape(bp * seq, d)
    xs = x_sample.reshape(bs, d)
    fg = final_norm_g.reshape(1, d)
    outs = [[] for _ in range(8)]
    for l in range(depth):
        pvec, mul, wd, wa, e, lp, wg = _pack_params(
            l, rwkv_mu, w_decay0, w_decay_up, w_iclr0, w_iclr_up, k_k, k_a, r_k, ln_x_g, ln_x_b,
            conv_w, conv_b, lru_gx_w, lru_gx_b, lru_ga_w, lru_ga_b, lru_lambda)
        w_r, w_g, w_o = w_out_rwkv[l].astype(BF16), w_out_lru[l].astype(BF16), w_out[l].astype(BF16)
        g = norm_g[l].reshape(1, d)
        rec = (pvec, mul, wd, wa, e)
        w = _repack_w_in(w_in, l, _row_tile(d, 256))
        zp, zlp, zs, zls = _inproj(xp, xs, g, w, _row_tile(bp * seq, 1024), INPROJ_TN)

        s0t = jnp.transpose(state_wkv[l], (1, 2, 3, 0))
        o_rs, s_new, sh_new = _wkv_step(zs, zls, state_shift[l], s0t, *rec)
        conv = state_conv[l].reshape(bs, (CONV_W - 1) * LRU_W)
        o_gs, h_new, conv_new = _lru_step(zs, conv, state_lru[l], lp, wg)
        outs[4].append(sh_new)
        outs[5].append(jnp.transpose(s_new, (3, 0, 1, 2)))
        outs[6].append(conv_new.reshape(bs, CONV_W - 1, LRU_W))
        outs[7].append(h_new)

        zp3 = zp.reshape(bp, seq, -1)
        zlp3 = zlp.reshape(bp, seq, LANES)
        nb = max(n for n in (4, 2, 1) if bp % n == 0)
        o_r, s_new, sh_last = _wkv_chunk(zp3, zlp3, *rec, bp, seq, nb)
        o_r = o_r.reshape(bp * seq, RWKV_W)
        last = l == depth - 1
        xp, h_last, conv_last, xs = _outproj_lru(xp, o_r, zp, xs, o_rs, o_gs, zs, lp, wg, w_r, w_g, w_o,
                                                 fg, _row_tile(seq, 256), seq, last)
        outs[0].append(sh_last.reshape(bp, -1))
        outs[1].append(s_new)
        outs[2].append(conv_last)
        outs[3].append(h_last.reshape(bp, LRU_W))

    return (xp.reshape(bp, seq, d), xs.reshape(bs, 1, d)) + tuple(jnp.stack(o) for o in outs)
```

```python
import functools

import jax
import jax.numpy as jnp
from jax import lax
from jax.experimental import pallas as pl
from jax.experimental.pallas import tpu as pltpu

F32 = jnp.float32
BF16 = jnp.bfloat16

HEADS = 16
HEAD = 64
RWKV_W = HEADS * HEAD
LORA = 64
LRU_W = 1024
LRU_BLOCKS = 16
CONV_W = 4
LRU_C = 8.0
RMS_EPS = 1e-6
GN_EPS = 1e-5 * HEAD
DECAY_SCALE = 0.6065306597126334

LANES = 128
SUBLANES = 8
WKV_CHUNK = 64
VMEM_LIMIT = 56 * 1024 * 1024

NN = (((1,), (0,)), ((), ()))
NT = (((1,), (1,)), ((), ()))
TN = (((0,), (0,)), ((), ()))


def _bf(x):
    return x.astype(BF16)


def _dg(a, b, dn):
    return lax.dot_general(a, b, dn, preferred_element_type=F32)


def _softplus(x):
    return jnp.maximum(x, 0.0) + jnp.log1p(jnp.exp(-jnp.abs(x)))


def _sigmoid(x):
    return 1.0 / (1.0 + jnp.exp(-x))


def _segsum(x, e):
    rows, n = x.shape[0], x.shape[1] // LANES
    stacked = jnp.concatenate([x[:, LANES * j:LANES * (j + 1)] for j in range(n)], axis=0)
    s = _dg(_bf(stacked), e, NN)
    return jnp.concatenate([s[rows * j:rows * (j + 1), :] for j in range(n)], axis=1)


def _rms(x, g):
    return x * lax.rsqrt(jnp.mean(x * x, axis=-1, keepdims=True) + RMS_EPS) * g


SHIFT_MAIN = 3 * RWKV_W
LORA_COL = 10 * RWKV_W
LORA_BLOCK = LORA_COL // LANES
FUSE_PIECES = 8
INPROJ_TN = 1024
INPROJ_HEAD_TN = 512


def _inproj_head_kernel(x_ref, xs_ref, g_ref, w_ref, wt_ref, wlo_ref, wb_ref, wl_ref, z_ref, zl_ref,
                        zs_ref, zls_ref, h_ref):
    j = pl.program_id(0)
    tm = x_ref.shape[0]
    tn = w_ref.shape[-1]

    @pl.when(j == 0)
    def _():
        h_ref[0:tm, :] = _bf(_rms(x_ref[...], g_ref[...]))
        h_ref[tm:, :] = _bf(_rms(xs_ref[...], g_ref[...]))
        wl_ref[...] = _bf(wlo_ref[...])
        zl = _dg(h_ref[...], wl_ref[...], NN)
        zl_ref[...] = zl[0:tm]
        zls_ref[...] = zl[tm:]

    @pl.when(j < SHIFT_MAIN // tn)
    def _():
        wb_ref[...] = _bf(w_ref[...])

    @pl.when(j >= SHIFT_MAIN // tn)
    def _():
        wb_ref[:, 0:tn - 2 * LORA] = _bf(w_ref[:, 2 * LORA:tn])
        wb_ref[:, tn - 2 * LORA:tn] = _bf(wt_ref[...])

    z_ref[...] = _dg(h_ref[0:tm, :], wb_ref[...], NN)
    zs_ref[...] = _dg(h_ref[tm:, :], wb_ref[...], NN)


def _inproj_head(x, xs, g, w_in, layer, tm, tn):
    m, d = x.shape
    ms = xs.shape[0]
    n = w_in.shape[-1]
    assert n == LORA_COL + 2 * LORA and SHIFT_MAIN % tn == 0 and 2 * LORA == LANES
    nj = LORA_COL // tn
    lanes_per_tile = tn // LANES
    one = lambda shp, imap: pl.BlockSpec(shp, imap, pipeline_mode=pl.Buffered(1))
    return pl.pallas_call(
        _inproj_head_kernel,
        out_shape=(jax.ShapeDtypeStruct((d, LORA_COL), BF16), jax.ShapeDtypeStruct((d, LANES), BF16),
                   jax.ShapeDtypeStruct((m, LORA_COL), F32), jax.ShapeDtypeStruct((m, LANES), F32),
                   jax.ShapeDtypeStruct((ms, LORA_COL), F32), jax.ShapeDtypeStruct((ms, LANES), F32)),
        grid=(nj,),
        in_specs=[
            one((tm, d), lambda j: (0, 0)),
            one((ms, d), lambda j: (0, 0)),
            one((1, d), lambda j: (0, 0)),
            pl.BlockSpec((None, d, tn), lambda j: (layer, 0, j)),
            pl.BlockSpec((None, d, LANES), lambda j: (layer, 0, (j + 1) * lanes_per_tile)),
            one((None, d, LANES), lambda j: (layer, 0, SHIFT_MAIN // LANES)),
        ],
        out_specs=(
            pl.BlockSpec((d, tn), lambda j: (0, j)),
            pl.BlockSpec((d, LANES), lambda j: (0, 0)),
            pl.BlockSpec((tm, tn), lambda j: (0, j)),
            pl.BlockSpec((tm, LANES), lambda j: (0, 0)),
            pl.BlockSpec((ms, tn), lambda j: (0, j)),
            pl.BlockSpec((ms, LANES), lambda j: (0, 0)),
        ),
        scratch_shapes=[pltpu.VMEM((tm + ms, d), BF16)],
        compiler_params=pltpu.CompilerParams(
            dimension_semantics=("arbitrary",), vmem_limit_bytes=VMEM_LIMIT),
        name="inproj_head",
    )(x, xs, g, w_in, w_in, w_in)


def _inproj_rest_kernel(x_ref, g_ref, w_ref, wl_ref, z_in, zl_in, z_ref, zl_ref, h_ref):
    @pl.when(pl.program_id(1) == 0)
    def _():
        h = _bf(_rms(x_ref[...], g_ref[...]))
        h_ref[...] = h
        zl_ref[...] = _dg(h, wl_ref[...], NN)

    z_ref[...] = _dg(h_ref[...], w_ref[...], NN)


def _inproj_rest(x, g, w, wl, z, zl, tm, tn):
    m, d = x.shape
    nj = LORA_COL // tn
    return pl.pallas_call(
        _inproj_rest_kernel,
        out_shape=(jax.ShapeDtypeStruct(z.shape, F32), jax.ShapeDtypeStruct(zl.shape, F32)),
        grid=(m // tm - 1, nj),
        in_specs=[
            pl.BlockSpec((tm, d), lambda i, j: (i + 1, 0)),
            pl.BlockSpec((1, d), lambda i, j: (0, 0), pipeline_mode=pl.Buffered(1)),
            pl.BlockSpec((d, tn), lambda i, j: (0, j)),
            pl.BlockSpec((d, LANES), lambda i, j: (0, 0), pipeline_mode=pl.Buffered(1)),
            pl.BlockSpec(memory_space=pl.ANY),
            pl.BlockSpec(memory_space=pl.ANY),
        ],
        out_specs=(
            pl.BlockSpec((tm, tn), lambda i, j: (i + 1, j)),
            pl.BlockSpec((tm, LANES), lambda i, j: (i + 1, 0)),
        ),
        scratch_shapes=[pltpu.VMEM((tm, d), BF16)],
        input_output_aliases={4: 0, 5: 1},
        compiler_params=pltpu.CompilerParams(
            dimension_semantics=("arbitrary", "arbitrary"), vmem_limit_bytes=VMEM_LIMIT),
        name="inproj",
    )(x, g, w, wl, z, zl)


_MU_R, _MU_K, _MU_V, _W0, _A0, _KK, _KA, _RK, _LNG, _LNB = range(10)


def _prow(pv_ref, i):
    return pv_ref[i:i + 1, :]


def _wkv_prep(zr, zk, zv, zl, pr, pk, pv, pl_, pv_ref, mul_ref, wd_ref, wa_ref, e):
    r = zr + _prow(pv_ref, _MU_R) * (pr - zr)
    k = zk + _prow(pv_ref, _MU_K) * (pk - zk)
    v = zv + _prow(pv_ref, _MU_V) * (pv - zv)
    lo = zl + mul_ref[0:1, :] * (pl_ - zl)
    lw = _dg(_bf(jnp.tanh(lo)), wd_ref[...], NN)
    la = _dg(_bf(lo), wa_ref[...], NN)
    logd = -DECAY_SCALE * _sigmoid(_prow(pv_ref, _W0) + lw)
    a = _sigmoid(_prow(pv_ref, _A0) + la)
    kk = k * _prow(pv_ref, _KK)
    kk = kk * lax.rsqrt(jnp.maximum(_segsum(kk * kk, e), 1e-24))
    k2 = k * (1.0 + (a - 1.0) * _prow(pv_ref, _KA))
    return r, k2, v, -kk, kk * a, logd


def _wkv_bonus_gate(r, k2, v, zrg, pv_ref, e):
    return _segsum(r * k2 * _prow(pv_ref, _RK), e) * v, zrg * _sigmoid(zrg)


def _wkv_norm_gate(y, bonus_v, gate, pv_ref, e):
    mu = _segsum(y, e) * (1.0 / HEAD)
    yc = y - mu
    var = _segsum(yc * yc, e) * (1.0 / HEAD)
    yn = yc * lax.rsqrt(var + GN_EPS) * _prow(pv_ref, _LNG) + _prow(pv_ref, _LNB)
    return _bf((yn + bonus_v) * gate)


def _wkv_post(y, r, k2, v, zrg, pv_ref, e):
    bonus_v, gate = _wkv_bonus_gate(r, k2, v, zrg, pv_ref, e)
    return _wkv_norm_gate(y, bonus_v, gate, pv_ref, e)


def _wkv_chunk_kernel(zr_ref, zk_ref, zv_ref, zrg_ref, zl_ref, pv_ref, mul_ref, wd_ref, wa_ref,
                      e_ref, o_ref, sout_ref, nsh_ref, s_s, prev_s, prevl_s):
    c = pl.program_id(1)
    nc = pl.num_programs(1)
    C = WKV_CHUNK
    assert C == HEAD and 2 * HEAD == LANES
    nb = zr_ref.shape[0]
    rows_all = nb * C
    seqs = range(nb)

    @pl.when(c == 0)
    def _():
        s_s[...] = jnp.zeros_like(s_s)
        prev_s[...] = jnp.zeros_like(prev_s)
        prevl_s[...] = jnp.zeros_like(prevl_s)

    first = lax.broadcasted_iota(jnp.int32, (SUBLANES, 1), 0) == 0

    def shifted(z, prev_ref, lanes):
        rolled = pltpu.roll(z, 1, 0)
        pieces = []
        for b in seqs:
            head = jnp.where(first, prev_ref[b, 0:1, lanes], rolled[b * C:b * C + SUBLANES, :])
            pieces += [head, rolled[b * C + SUBLANES:(b + 1) * C, :]]
        return jnp.concatenate(pieces, axis=0)

    def flat(ref):
        return ref[...].reshape(rows_all, ref.shape[-1])

    zr, zk, zv, zl = flat(zr_ref), flat(zk_ref), flat(zv_ref), flat(zl_ref)
    seg = [slice(RWKV_W * i, RWKV_W * (i + 1)) for i in range(3)]
    pr = shifted(zr, prev_s, seg[0])
    pk = shifted(zk, prev_s, seg[1])
    pv = shifted(zv, prev_s, seg[2])
    pl_ = shifted(zl, prevl_s, slice(0, LANES))
    for b in seqs:
        last = slice(b * C + C - 1, b * C + C)
        prev_s[b, 0:1, seg[0]] = zr[last, :]
        prev_s[b, 0:1, seg[1]] = zk[last, :]
        prev_s[b, 0:1, seg[2]] = zv[last, :]
        prevl_s[b, 0:1, :] = zl[last, :]

    e = e_ref[...]
    r, k2, v, av, bv, logd = _wkv_prep(zr, zk, zv, zl, pr, pk, pv, pl_, pv_ref, mul_ref,
                                       wd_ref, wa_ref, e)

    ti = lax.broadcasted_iota(jnp.int32, (rows_all, rows_all), 0)
    tj = lax.broadcasted_iota(jnp.int32, (rows_all, rows_all), 1)
    tri = jnp.where((ti >= tj) & ((ti & -C) == (tj & -C)), 1.0, 0.0).astype(BF16)
    d_hi = _bf(logd)
    d_r1 = logd - d_hi.astype(F32)
    d_mid = _bf(d_r1)
    d_lo = _bf(d_r1 - d_mid.astype(F32))
    cum = _dg(jnp.concatenate([tri, tri, tri], axis=1), jnp.concatenate([d_hi, d_mid, d_lo], axis=0), NN)
    e_in = jnp.exp(cum)
    e_neg = jnp.exp(-cum)
    a_t = av * jnp.exp(cum - logd)
    r_t = r * e_in
    k_t = k2 * e_neg
    b_t = bv * e_neg
    p_c = [jnp.exp(cum[b * C + C - 1:b * C + C, :]) for b in seqs]

    lane = lax.broadcasted_iota(jnp.int32, (C, LANES), 1)
    trow = lax.broadcasted_iota(jnp.int32, (C, LANES), 0)
    lo = lane < HEAD
    s_in = lane & (HEAD - 1)
    strict = s_in < trow
    incl2 = ((lax.broadcasted_iota(jnp.int32, (C, 2 * LANES), 1) & (HEAD - 1))
             <= lax.broadcasted_iota(jnp.int32, (C, 2 * LANES), 0))
    eye2 = jnp.where(s_in == trow, 1.0, 0.0).astype(F32)
    vrow = lax.broadcasted_iota(jnp.int32, (2 * HEAD, LANES), 0)
    klane = lax.broadcasted_iota(jnp.int32, (2 * HEAD, LANES), 1)
    same_head = (vrow < HEAD) == (klane < HEAD)

    def bd(x):
        z = jnp.zeros_like(x)
        return jnp.concatenate([jnp.where(lo, x, z), jnp.where(lo, z, x)], axis=0)

    npair = HEADS // 2
    units = [(b, p) for b in seqs for p in range(npair)]
    un = range(len(units))
    blk = lambda arr, i: arr[units[i][0] * C:(units[i][0] + 1) * C, LANES * units[i][1]:LANES * (units[i][1] + 1)]
    ar = [_bf(jnp.concatenate([blk(a_t, i), blk(r_t, i)], axis=0)) for i in un]
    bk = [_bf(jnp.concatenate([bd(blk(b_t, i)), bd(blk(k_t, i))], axis=0)) for i in un]
    g = [_dg(ar[i], bk[i], NT) for i in un]
    s0 = [s_s[i] for i in un]
    ars = [_dg(ar[i], _bf(s0[i]), NT) for i in un]
    vbd = [_bf(bd(blk(v, i))) for i in un]
    x = [jnp.where(strict, g[i][0:C, 0:LANES], 0.0) for i in un]
    ak = [jnp.where(strict, g[i][0:C, LANES:2 * LANES], 0.0) for i in un]
    w = [ars[i][0:C, :] + _dg(_bf(ak[i]), vbd[i], NN) for i in un]
    t = [eye2 + x[i] for i in un]
    x = [_dg(_bf(x[i]), _bf(bd(x[i])), NN) for i in un]
    for _ in range(C.bit_length() - 3):
        xt = [_dg(_bf(jnp.concatenate([x[i], t[i]], axis=0)), _bf(bd(x[i])), NN) for i in un]
        x = [xt[i][0:C, :] for i in un]
        t = [t[i] + xt[i][C:2 * C, :] for i in un]
    t = [t[i] + _dg(_bf(t[i]), _bf(bd(x[i])), NN) for i in un]
    u = [_dg(_bf(t[i]), _bf(bd(w[i])), NN) for i in un]
    rbk = [_bf(jnp.where(incl2, g[i][C:2 * C, :], 0.0)) for i in un]
    uvbd = [jnp.concatenate([_bf(bd(u[i])), vbd[i]], axis=0) for i in un]
    y = [ars[i][C:2 * C, :] + _dg(rbk[i], uvbd[i], NN) for i in un]
    uv = [_bf(jnp.concatenate([u[i], blk(v, i)], axis=0)) for i in un]
    pc = [p_c[units[i][0]][:, LANES * units[i][1]:LANES * (units[i][1] + 1)] for i in un]
    bkh = [_bf(jnp.concatenate([blk(b_t, i), blk(k_t, i)], axis=0) * pc[i]) for i in un]
    s1 = [s0[i] * pc[i] + jnp.where(same_head, _dg(uv[i], bkh[i], TN), 0.0) for i in un]
    for i in un:
        s_s[i] = s1[i]

    y_all = jnp.concatenate(
        [jnp.concatenate(y[b * npair:(b + 1) * npair], axis=1) for b in seqs], axis=0)
    o = _wkv_post(y_all, r, k2, v, flat(zrg_ref), pv_ref, e)
    o_ref[...] = o.reshape(nb, C, RWKV_W)

    @pl.when(c == nc - 1)
    def _():
        for i in un:
            b, p = units[i]
            sout_ref[b, 2 * p] = s1[i][0:HEAD, 0:HEAD]
            sout_ref[b, 2 * p + 1] = s1[i][HEAD:2 * HEAD, HEAD:2 * HEAD]
        for b in seqs:
            for q, ref in enumerate((zr_ref, zk_ref, zv_ref)):
                nsh_ref[0, b:b + 1, RWKV_W * q:RWKV_W * (q + 1)] = ref[b, C - 1:C, :]
            nsh_ref[0, b:b + 1, SHIFT_MAIN:SHIFT_MAIN + LANES] = zl_ref[b, C - 1:C, :]


def _wkv_chunk(z, zl, pvec, mul, wd, wa, e, batch, seq, nb):
    C = WKV_CHUNK
    nc = seq // C
    full = lambda shp: pl.BlockSpec(shp, lambda b, c: (0,) * len(shp))
    col = lambda j: pl.BlockSpec((nb, C, RWKV_W), lambda b, c, j=j: (b, c, j))
    return pl.pallas_call(
        _wkv_chunk_kernel,
        out_shape=(jax.ShapeDtypeStruct((batch, seq, RWKV_W), BF16),
                   jax.ShapeDtypeStruct((batch, HEADS, HEAD, HEAD), F32),
                   jax.ShapeDtypeStruct((batch // nb, nb, SHIFT_MAIN + LANES), F32)),
        grid=(batch // nb, nc),
        in_specs=[col(0), col(1), col(2), col(3),
                  pl.BlockSpec((nb, C, LANES), lambda b, c: (b, c, 0)),
                  full(pvec.shape), full(mul.shape), full(wd.shape), full(wa.shape), full(e.shape)],
        out_specs=(pl.BlockSpec((nb, C, RWKV_W), lambda b, c: (b, c, 0)),
                   pl.BlockSpec((nb, HEADS, HEAD, HEAD), lambda b, c: (b, 0, 0, 0)),
                   pl.BlockSpec((1, nb, SHIFT_MAIN + LANES), lambda b, c: (b, 0, 0))),
        scratch_shapes=[pltpu.VMEM((nb * HEADS // 2, 2 * HEAD, 2 * HEAD), F32),
                        pltpu.VMEM((nb, SUBLANES, 3 * RWKV_W), F32),
                        pltpu.VMEM((nb, SUBLANES, LANES), F32)],
        compiler_params=pltpu.CompilerParams(
            dimension_semantics=("arbitrary", "arbitrary"), vmem_limit_bytes=VMEM_LIMIT),
        name="wkv_chunk",
    )(z, z, z, z, zl, pvec, mul, wd, wa, e)


def _wkv_step_kernel(zr_ref, zk_ref, zv_ref, zrg_ref, zl_ref, sh_ref, s0_ref, pv_ref,
                     mul_ref, wd_ref, wa_ref, e_ref, o_ref, sout_ref, nsh_ref,
                     at_s, drt_s, bt_s, kt_s, dt_s, vt_s, brt_s, krt_s, yt_s, keep_s):
    h = pl.program_id(0)
    nh = pl.num_programs(0)
    nseq = zr_ref.shape[0]

    @pl.when(h == 0)
    def _():
        e = e_ref[...]
        r, k2, v, av, bv, logd = _wkv_prep(
            zr_ref[...], zk_ref[...], zv_ref[...], zl_ref[...],
            sh_ref[:, 0:RWKV_W], sh_ref[:, RWKV_W:2 * RWKV_W], sh_ref[:, 2 * RWKV_W:3 * RWKV_W],
            sh_ref[:, SHIFT_MAIN:SHIFT_MAIN + LANES], pv_ref, mul_ref, wd_ref, wa_ref, e)
        nsh_ref[:, 0:RWKV_W] = zr_ref[...]
        nsh_ref[:, RWKV_W:2 * RWKV_W] = zk_ref[...]
        nsh_ref[:, 2 * RWKV_W:SHIFT_MAIN] = zv_ref[...]
        nsh_ref[:, SHIFT_MAIN:SHIFT_MAIN + LANES] = zl_ref[...]
        d = jnp.exp(logd)
        at_s[...] = av.T
        drt_s[...] = (d * r).T
        bt_s[...] = bv.T
        kt_s[...] = k2.T
        dt_s[...] = d.T
        vt_s[...] = v.T
        brt_s[...] = jnp.sum((bv * r).T.reshape(HEADS, HEAD, nseq), axis=1)
        krt_s[...] = jnp.sum((k2 * r).T.reshape(HEADS, HEAD, nseq), axis=1)
        keep_s[0] = r
        keep_s[1] = k2
        keep_s[2] = v

    base = pl.multiple_of(h * HEAD, HEAD)
    rows = pl.ds(base, HEAD)
    a_h, dr_h, b_h, k_h, d_h = at_s[rows, :], drt_s[rows, :], bt_s[rows, :], kt_s[rows, :], dt_s[rows, :]
    br_h = brt_s[pl.ds(h, 1), :]
    kr_h = krt_s[pl.ds(h, 1), :]

    def value_rows(g, carry):
        off = pl.multiple_of(base + g * SUBLANES, SUBLANES)
        v8 = vt_s[pl.ds(off, SUBLANES), :]
        ys = []
        for j in range(SUBLANES):
            vi = g * SUBLANES + j
            s_v = s0_ref[0, vi]
            sa = jnp.sum(s_v * a_h, axis=0, keepdims=True)
            y0 = jnp.sum(s_v * dr_h, axis=0, keepdims=True)
            v_v = v8[j:j + 1, :]
            sout_ref[0, vi] = s_v * d_h + sa * b_h + v_v * k_h
            ys.append(y0 + sa * br_h + v_v * kr_h)
        yt_s[pl.ds(off, SUBLANES), :] = jnp.concatenate(ys, axis=0)
        return carry

    lax.fori_loop(0, HEAD // SUBLANES, value_rows, 0)

    @pl.when(h == nh - 1)
    def _():
        o_ref[...] = _wkv_post(yt_s[...].T, keep_s[0], keep_s[1], keep_s[2], zrg_ref[...], pv_ref,
                               e_ref[...])


def _wkv_step(z, zl, sh, s0t, pvec, mul, wd, wa, e):
    nseq = z.shape[0]
    full = lambda shp: pl.BlockSpec(shp, lambda i: (0,) * len(shp))
    col = lambda j: pl.BlockSpec((nseq, RWKV_W), lambda i, j=j: (0, j))
    st_block = (1, HEAD, HEAD, nseq)
    wide = pltpu.VMEM((RWKV_W, nseq), F32)
    return pl.pallas_call(
        _wkv_step_kernel,
        out_shape=(jax.ShapeDtypeStruct((nseq, RWKV_W), BF16),
                   jax.ShapeDtypeStruct(s0t.shape, F32),
                   jax.ShapeDtypeStruct(sh.shape, F32)),
        grid=(HEADS,),
        in_specs=[col(0), col(1), col(2), col(3),
                  full(zl.shape), full(sh.shape),
                  pl.BlockSpec(st_block, lambda i: (i, 0, 0, 0)),
                  full(pvec.shape), full(mul.shape), full(wd.shape), full(wa.shape), full(e.shape)],
        out_specs=(full((nseq, RWKV_W)),
                   pl.BlockSpec(st_block, lambda i: (i, 0, 0, 0)),
                   full(sh.shape)),
        scratch_shapes=[wide] * 6 + [pltpu.VMEM((HEADS, nseq), F32)] * 2
                       + [wide, pltpu.VMEM((3, nseq, RWKV_W), F32)],
        compiler_params=pltpu.CompilerParams(
            dimension_semantics=("arbitrary",), vmem_limit_bytes=VMEM_LIMIT),
        name="wkv_step",
    )(z, z, z, z, zl, sh, s0t, pvec, mul, wd, wa, e)


_CW0, _CW1, _CW2, _CW3, _CB, _GXB, _GAB, _LAM = range(8)


def _lru_gates(xc, lp_ref, wg_ref):
    xb = _bf(xc)
    ngroups = wg_ref.shape[0]
    gs = [_dg(xb[:, LANES * g:LANES * (g + 1)], wg_ref[g], NN) for g in range(ngroups)]
    gx_pre = jnp.concatenate([gs[g][:, 0:LANES] for g in range(ngroups)], axis=1)
    ga_pre = jnp.concatenate([gs[g][:, LANES:2 * LANES] for g in range(ngroups)], axis=1)
    gx = _sigmoid(gx_pre + _prow(lp_ref, _GXB))
    ga = _sigmoid(ga_pre + _prow(lp_ref, _GAB))
    log_a = -LRU_C * ga * _softplus(-_prow(lp_ref, _LAM))
    a = jnp.exp(log_a)
    mult = jnp.sqrt((1.0 - a) * (1.0 + a))
    return a, mult * gx * xc


def _lru_scan_rows(a, b, zg, hc):
    row8 = lax.broadcasted_iota(jnp.int32, (SUBLANES, 1), 0)
    hs = []
    for i in range(a.shape[0] // SUBLANES):
        a8 = a[SUBLANES * i:SUBLANES * (i + 1), :]
        b8 = b[SUBLANES * i:SUBLANES * (i + 1), :]
        for s in (1, 2, 4):
            keep = row8 >= s
            b8 = jnp.where(keep, a8 * pltpu.roll(b8, s, 0) + b8, b8)
            a8 = jnp.where(keep, a8 * pltpu.roll(a8, s, 0), a8)
        hb = b8 + a8 * hc
        hs.append(hb)
        hc = jnp.broadcast_to(hb[SUBLANES - 1:SUBLANES, :], hb.shape)
    return _bf(jnp.concatenate(hs, axis=0) * (zg * _sigmoid(zg))), hc


def _lru_step_kernel(zx_ref, zg_ref, conv_ref, h0_ref, lp_ref, wg_ref, o_ref, hnew_ref, cnew_ref):
    zx = zx_ref[...]
    keep = (CONV_W - 2) * LRU_W
    cnew_ref[:, 0:keep] = conv_ref[:, LRU_W:LRU_W + keep]
    cnew_ref[:, keep:keep + LRU_W] = zx
    xc = _prow(lp_ref, _CW3) * zx + _prow(lp_ref, _CB)
    for j in range(CONV_W - 1):
        xc = xc + _prow(lp_ref, j) * conv_ref[:, LRU_W * j:LRU_W * (j + 1)]
    a, b = _lru_gates(xc, lp_ref, wg_ref)
    h = a * h0_ref[...] + b
    hnew_ref[...] = h
    zg = zg_ref[...]
    o_ref[...] = _bf(h * (zg * _sigmoid(zg)))


def _lru_step(z_main, conv, h0, lp, wg):
    nb = z_main.shape[0]
    full = lambda shp: pl.BlockSpec(shp, lambda i: (0,) * len(shp))
    col = lambda j: pl.BlockSpec((nb, LRU_W), lambda i, j=j: (0, j))
    return pl.pallas_call(
        _lru_step_kernel,
        out_shape=(jax.ShapeDtypeStruct((nb, LRU_W), BF16), jax.ShapeDtypeStruct((nb, LRU_W), F32),
                   jax.ShapeDtypeStruct(conv.shape, F32)),
        grid=(1,),
        in_specs=[col(4), col(5), full(conv.shape), full(h0.shape), full(lp.shape), full(wg.shape)],
        out_specs=(full((nb, LRU_W)), full((nb, LRU_W)), full(conv.shape)),
        compiler_params=pltpu.CompilerParams(
            dimension_semantics=("arbitrary",), vmem_limit_bytes=VMEM_LIMIT),
        name="lru_step",
    )(z_main, z_main, conv, h0, lp, wg)


def _project(x, o_r, o_g, m_r, m_g, wr_ref, wg_ref, wo_ref, fg_ref, final):
    y_r = _dg(o_r, wr_ref[...], NN)
    y_g = _dg(o_g, wg_ref[...], NN)
    merged = _sigmoid(m_r) * y_r + _sigmoid(m_g) * y_g
    out = x + _dg(_bf(merged), wo_ref[...], NN)
    return _rms(out, fg_ref[...]) if final else out


def _outproj_lru_kernel(x_ref, or_ref, mr_ref, mg_ref, zx_ref, zg_ref, xs_ref, ors_ref, ogs_ref,
                        mrs_ref, mgs_ref, lp_ref, wgate_ref, wr_ref, wg_ref, wo_ref, fg_ref,
                        out_ref, hlast_ref, cnew_ref, outs_ref, og_s, mg_s, xb_s, hc_s, *, final,
                        tiles_per_seq):
    i = pl.program_id(0)
    n = pl.num_programs(0) - 2
    tm, d = x_ref.shape

    @pl.when(i == 0)
    def _():
        og_s[...] = jnp.zeros_like(og_s)
        mg_s[...] = jnp.zeros_like(mg_s)
        xb_s[...] = jnp.zeros_like(xb_s)
        hc_s[...] = jnp.zeros_like(hc_s)
        outs_ref[...] = _project(xs_ref[...], ors_ref[...], ogs_ref[...], mrs_ref[...], mgs_ref[...],
                                 wr_ref, wg_ref, wo_ref, fg_ref, final)

    og_prev = og_s[...]
    mg_prev = mg_s[...]
    t = lax.rem(jnp.minimum(i, n - 1), tiles_per_seq)
    first = t == 0

    xb_s[0:SUBLANES, :] = jnp.where(first, 0.0, xb_s[0:SUBLANES, :])
    xb_s[SUBLANES:SUBLANES + tm, :] = zx_ref[...]

    outs, a_parts, b_parts = [], [], []
    for c in range(FUSE_PIECES):
        cs = slice(c * d // FUSE_PIECES, (c + 1) * d // FUSE_PIECES)
        outs.append(x_ref[:, cs] + _dg(mg_prev, wo_ref[:, cs], NN))
        r0, r1 = SUBLANES + c * tm // FUSE_PIECES, SUBLANES + (c + 1) * tm // FUSE_PIECES
        xc = _prow(lp_ref, _CW3) * xb_s[r0:r1, :] + _prow(lp_ref, _CB)
        for j in range(1, CONV_W):
            xc = xc + _prow(lp_ref, CONV_W - 1 - j) * xb_s[r0 - j:r1 - j, :]
        a_c, b_c = _lru_gates(xc, lp_ref, wgate_ref)
        a_parts.append(a_c)
        b_parts.append(b_c)
    out = jnp.concatenate(outs, axis=1)
    out_ref[...] = _rms(out, fg_ref[...]) if final else out
    xb_s[0:SUBLANES, :] = xb_s[tm:tm + SUBLANES, :]

    o_g, hc = _lru_scan_rows(jnp.concatenate(a_parts, axis=0), jnp.concatenate(b_parts, axis=0),
                             zg_ref[...], jnp.where(first, 0.0, hc_s[...]))
    hc_s[...] = hc
    og_s[...] = o_g

    y_r = _dg(or_ref[...], wr_ref[...], NN)
    y_g = _dg(og_prev, wg_ref[...], NN)
    mg_s[...] = _bf(_sigmoid(mr_ref[...]) * y_r + _sigmoid(mg_ref[...]) * y_g)

    @pl.when((t == tiles_per_seq - 1) & (i < n))
    def _():
        hlast_ref[0] = hc[0:1, :]
        cnew_ref[0] = zx_ref[tm - (CONV_W - 1):tm, :]


def _outproj_lru(x, o_r, z, xs, o_rs, o_gs, zs, lp, wgate, w_r, w_g, w_o, fg, tm, seq, final):
    m, d = x.shape
    ms = xs.shape[0]
    n = m // tm
    tiles_per_seq = seq // tm
    const = lambda shp: pl.BlockSpec(shp, lambda i: (0,) * len(shp), pipeline_mode=pl.Buffered(1))
    back = lambda i, k: jnp.clip(i - k, 0, n - 1)
    return pl.pallas_call(
        functools.partial(_outproj_lru_kernel, final=final, tiles_per_seq=tiles_per_seq),
        out_shape=(jax.ShapeDtypeStruct((m, d), F32),
                   jax.ShapeDtypeStruct((m // seq, 1, LRU_W), F32),
                   jax.ShapeDtypeStruct((m // seq, CONV_W - 1, LRU_W), F32),
                   jax.ShapeDtypeStruct((ms, d), F32)),
        grid=(n + 2,),
        in_specs=[
            pl.BlockSpec((tm, d), lambda i: (back(i, 2), 0)),
            pl.BlockSpec((tm, RWKV_W), lambda i: (back(i, 1), 0)),
            pl.BlockSpec((tm, d), lambda i: (back(i, 1), 3)),
            pl.BlockSpec((tm, d), lambda i: (back(i, 1), 4)),
            pl.BlockSpec((tm, LRU_W), lambda i: (back(i, 0), 4)),
            pl.BlockSpec((tm, LRU_W), lambda i: (back(i, 0), 5)),
            const(xs.shape), const(o_rs.shape), const(o_gs.shape),
            pl.BlockSpec((ms, d), lambda i: (0, 3), pipeline_mode=pl.Buffered(1)),
            pl.BlockSpec((ms, d), lambda i: (0, 4), pipeline_mode=pl.Buffered(1)),
            const(lp.shape), const(wgate.shape), const(w_r.shape), const(w_g.shape), const(w_o.shape),
            const(fg.shape),
        ],
        out_specs=(pl.BlockSpec((tm, d), lambda i: (back(i, 2), 0)),
                   pl.BlockSpec((1, 1, LRU_W), lambda i: (back(i, 0) // tiles_per_seq, 0, 0)),
                   pl.BlockSpec((1, CONV_W - 1, LRU_W), lambda i: (back(i, 0) // tiles_per_seq, 0, 0)),
                   pl.BlockSpec((ms, d), lambda i: (0, 0))),
        scratch_shapes=[pltpu.VMEM((tm, LRU_W), BF16),
                        pltpu.VMEM((tm, d), BF16),
                        pltpu.VMEM((SUBLANES + tm, LRU_W), F32),
                        pltpu.VMEM((SUBLANES, LRU_W), F32)],
        compiler_params=pltpu.CompilerParams(
            dimension_semantics=("arbitrary",), vmem_limit_bytes=VMEM_LIMIT),
        name="outproj_lru",
    )(x, o_r, z, z, z, z, xs, o_rs, o_gs, zs, zs, lp, wgate, w_r, w_g, w_o, fg)


def _row_tile(m, want):
    t = min(m, want)
    assert m % t == 0, (m, t)
    return t


def _pack_params_kernel(mu_ref, w0_ref, a0_ref, kk_ref, ka_ref, rk_ref, lng_ref, lnb_ref, wdu_ref,
                        wau_ref, cw_ref, cb_ref, gxb_ref, gab_ref, lam_ref, gxw_ref, gaw_ref,
                        pvec_ref, mul_ref, wd_ref, wa_ref, e_ref, lp_ref, wg_ref):
    pvec_ref[...] = jnp.zeros_like(pvec_ref)
    for i in range(3):
        pvec_ref[_MU_R + i:_MU_R + i + 1, :] = mu_ref[:, RWKV_W * i:RWKV_W * (i + 1)]
    for row, ref in ((_W0, w0_ref), (_A0, a0_ref), (_KK, kk_ref), (_KA, ka_ref), (_RK, rk_ref),
                     (_LNG, lng_ref), (_LNB, lnb_ref)):
        pvec_ref[row:row + 1, :] = ref[...]
    mul_ref[...] = jnp.broadcast_to(mu_ref[:, 3 * RWKV_W:3 * RWKV_W + 2 * LORA], mul_ref.shape)

    zeros = jnp.zeros((LORA, RWKV_W), BF16)
    wd_ref[0:LORA, :] = _bf(wdu_ref[...])
    wd_ref[LORA:2 * LORA, :] = zeros
    wa_ref[0:LORA, :] = zeros
    wa_ref[LORA:2 * LORA, :] = _bf(wau_ref[...])

    ri = lax.broadcasted_iota(jnp.int32, (LANES, LANES), 0)
    ci = lax.broadcasted_iota(jnp.int32, (LANES, LANES), 1)
    e_ref[...] = jnp.where((ri < HEAD) == (ci < HEAD), 1.0, 0.0).astype(BF16)

    lp_ref[_CW0:_CW0 + CONV_W, :] = cw_ref[...]
    for row, ref in ((_CB, cb_ref), (_GXB, gxb_ref), (_GAB, gab_ref), (_LAM, lam_ref)):
        lp_ref[row:row + 1, :] = ref[...]

    blk = LRU_W // LRU_BLOCKS
    z = jnp.zeros((blk, blk), F32)
    for g in range(LRU_BLOCKS // 2):
        top = jnp.concatenate([gxw_ref[2 * g], z, gaw_ref[2 * g], z], axis=1)
        bot = jnp.concatenate([z, gxw_ref[2 * g + 1], z, gaw_ref[2 * g + 1]], axis=1)
        wg_ref[g] = _bf(jnp.concatenate([top, bot], axis=0))


def _pack_params(l, rwkv_mu, w_decay0, w_decay_up, w_iclr0, w_iclr_up, k_k, k_a, r_k, ln_x_g,
                 ln_x_b, conv_w, conv_b, lru_gx_w, lru_gx_b, lru_ga_w, lru_ga_b, lru_lambda):
    blk = LRU_W // LRU_BLOCKS
    assert 2 * blk == LANES and 2 * LORA == LANES
    depth = rwkv_mu.shape[0]
    row = lambda a: pl.BlockSpec((1, a.shape[-1]), lambda i: (l, 0))
    mat = lambda a: pl.BlockSpec((None,) + a.shape[1:], lambda i: (l,) + (0,) * (a.ndim - 1))
    full = lambda shp: pl.BlockSpec(shp, lambda i: (0,) * len(shp))
    rk = r_k.reshape(depth, RWKV_W)
    rows = (rwkv_mu, w_decay0, w_iclr0, k_k, k_a, rk, ln_x_g, ln_x_b)
    out_shapes = ((16, RWKV_W, F32), (SUBLANES, LANES, F32), (LANES, RWKV_W, BF16), (LANES, RWKV_W, BF16),
                  (LANES, LANES, BF16), (SUBLANES, LRU_W, F32))
    outs = tuple(jax.ShapeDtypeStruct(s[:2], s[2]) for s in out_shapes)
    outs += (jax.ShapeDtypeStruct((LRU_BLOCKS // 2, LANES, 2 * LANES), BF16),)
    return pl.pallas_call(
        _pack_params_kernel,
        out_shape=outs,
        grid=(1,),
        in_specs=[row(a) for a in rows] + [mat(w_decay_up), mat(w_iclr_up), mat(conv_w), row(conv_b),
                                          row(lru_gx_b), row(lru_ga_b), row(lru_lambda),
                                          mat(lru_gx_w), mat(lru_ga_w)],
        out_specs=tuple(full(o.shape) for o in outs),
        compiler_params=pltpu.CompilerParams(
            dimension_semantics=("arbitrary",), vmem_limit_bytes=VMEM_LIMIT),
        name="pack_params",
    )(*rows, w_decay_up, w_iclr_up, conv_w, conv_b, lru_gx_b, lru_ga_b, lru_lambda, lru_gx_w, lru_ga_w)


def kernel(x_prompt, x_sample, state_shift, state_wkv, state_conv, state_lru, norm_g, w_in, rwkv_mu,
           w_decay0, w_decay_up, w_iclr0, w_iclr_up, k_k, k_a, r_k, ln_x_g, ln_x_b, w_out_rwkv,
           conv_w, conv_b, lru_gx_w, lru_gx_b, lru_ga_w, lru_ga_b, lru_lambda, w_out_lru, w_out,
           final_norm_g):
    bp, seq, d = x_prompt.shape
    bs = x_sample.shape[0]
    assert x_sample.shape[1] == 1 and seq % WKV_CHUNK == 0
    depth = w_in.shape[0]
    xp = x_prompt.reshape(bp * seq, d)
    xs = x_sample.reshape(bs, d)
    fg = final_norm_g.reshape(1, d)
    outs = [[] for _ in range(8)]
    for l in range(depth):
        pvec, mul, wd, wa, e, lp, wg = _pack_params(
            l, rwkv_mu, w_decay0, w_decay_up, w_iclr0, w_iclr_up, k_k, k_a, r_k, ln_x_g, ln_x_b,
            conv_w, conv_b, lru_gx_w, lru_gx_b, lru_ga_w, lru_ga_b, lru_lambda)
        w_r, w_g, w_o = w_out_rwkv[l].astype(BF16), w_out_lru[l].astype(BF16), w_out[l].astype(BF16)
        g = norm_g[l].reshape(1, d)
        rec = (pvec, mul, wd, wa, e)
        tm = _row_tile(bp * seq, 1024)
        w, wl, zp, zlp, zs, zls = _inproj_head(xp, xs, g, w_in, l, tm, INPROJ_HEAD_TN)
        if bp * seq > tm:
            zp, zlp = _inproj_rest(xp, g, w, wl, zp, zlp, tm, INPROJ_TN)

        s0t = jnp.transpose(state_wkv[l], (1, 2, 3, 0))
        o_rs, s_new, sh_new = _wkv_step(zs, zls, state_shift[l], s0t, *rec)
        conv = state_conv[l].reshape(bs, (CONV_W - 1) * LRU_W)
        o_gs, h_new, conv_new = _lru_step(zs, conv, state_lru[l], lp, wg)
        outs[4].append(sh_new)
        outs[5].append(jnp.transpose(s_new, (3, 0, 1, 2)))
        outs[6].append(conv_new.reshape(bs, CONV_W - 1, LRU_W))
        outs[7].append(h_new)

        zp3 = zp.reshape(bp, seq, -1)
        zlp3 = zlp.reshape(bp, seq, LANES)
        nb = max(n for n in (4, 2, 1) if bp % n == 0)
        o_r, s_new, sh_last = _wkv_chunk(zp3, zlp3, *rec, bp, seq, nb)
        o_r = o_r.reshape(bp * seq, RWKV_W)
        last = l == depth - 1
        xp, h_last, conv_last, xs = _outproj_lru(xp, o_r, zp, xs, o_rs, o_gs, zs, lp, wg, w_r, w_g, w_o,
                                                 fg, _row_tile(seq, 256), seq, last)
        outs[0].append(sh_last.reshape(bp, -1))
        outs[1].append(s_new)
        outs[2].append(conv_last)
        outs[3].append(h_last.reshape(bp, LRU_W))

    return (xp.reshape(bp, seq, d), xs.reshape(bs, 1, d)) + tuple(jnp.stack(o) for o in outs)
```

```python
import functools

import jax
import jax.numpy as jnp
from jax import lax
from jax.experimental import pallas as pl
from jax.experimental.pallas import tpu as pltpu

F32 = jnp.float32
BF16 = jnp.bfloat16

HEADS = 16
HEAD = 64
RWKV_W = HEADS * HEAD
LORA = 64
LRU_W = 1024
LRU_BLOCKS = 16
CONV_W = 4
LRU_C = 8.0
RMS_EPS = 1e-6
GN_EPS = 1e-5 * HEAD
DECAY_SCALE = 0.6065306597126334

LANES = 128
SUBLANES = 8
WKV_CHUNK = 64
VMEM_LIMIT = 56 * 1024 * 1024

NN = (((1,), (0,)), ((), ()))
NT = (((1,), (1,)), ((), ()))
TN = (((0,), (0,)), ((), ()))


def _bf(x):
    return x.astype(BF16)


def _dg(a, b, dn):
    return lax.dot_general(a, b, dn, preferred_element_type=F32)


def _softplus(x):
    return jnp.maximum(x, 0.0) + jnp.log1p(jnp.exp(-jnp.abs(x)))


def _sigmoid(x):
    return 1.0 / (1.0 + jnp.exp(-x))


def _segsum(x, e):
    rows, n = x.shape[0], x.shape[1] // LANES
    stacked = jnp.concatenate([x[:, LANES * j:LANES * (j + 1)] for j in range(n)], axis=0)
    s = _dg(_bf(stacked), e, NN)
    return jnp.concatenate([s[rows * j:rows * (j + 1), :] for j in range(n)], axis=1)


def _rms(x, g):
    return x * lax.rsqrt(jnp.mean(x * x, axis=-1, keepdims=True) + RMS_EPS) * g


SHIFT_MAIN = 3 * RWKV_W
LORA_COL = 10 * RWKV_W
LORA_BLOCK = LORA_COL // LANES
FUSE_PIECES = 8
INPROJ_TN = 1024
OUT_W_STEPS = 8


def _inproj_head_kernel(xs_ref, g_ref, w_ref, wt_ref, wlo_ref, wr_ref, wg_ref, wo_ref, wb_ref, wl_ref,
                        zs_ref, zls_ref, wrb_ref, wgb_ref, wob_ref, hs_ref):
    j = pl.program_id(0)
    tn = w_ref.shape[-1]

    @pl.when(j == 0)
    def _():
        hs_ref[...] = _bf(_rms(xs_ref[...], g_ref[...]))
        wl_ref[...] = _bf(wlo_ref[...])
        zls_ref[...] = _dg(hs_ref[...], wl_ref[...], NN)

    @pl.when(j < SHIFT_MAIN // tn)
    def _():
        wb_ref[...] = _bf(w_ref[...])

    @pl.when(j >= SHIFT_MAIN // tn)
    def _():
        wb_ref[:, 0:tn - 2 * LORA] = _bf(w_ref[:, 2 * LORA:tn])
        wb_ref[:, tn - 2 * LORA:tn] = _bf(wt_ref[...])

    zs_ref[...] = _dg(hs_ref[...], wb_ref[...], NN)

    @pl.when(j < OUT_W_STEPS)
    def _():
        wrb_ref[...] = _bf(wr_ref[...])
        wgb_ref[...] = _bf(wg_ref[...])
        wob_ref[...] = _bf(wo_ref[...])


def _inproj_head(xs, g, w_in, w_out_rwkv, w_out_lru, w_out, layer, tn):
    ms, d = xs.shape
    n = w_in.shape[-1]
    assert n == LORA_COL + 2 * LORA and SHIFT_MAIN % tn == 0 and 2 * LORA == LANES
    nj = LORA_COL // tn
    assert nj >= OUT_W_STEPS
    lanes_per_tile = tn // LANES
    one = lambda shp, imap: pl.BlockSpec(shp, imap, pipeline_mode=pl.Buffered(1))
    rows = lambda a: a.shape[1] // OUT_W_STEPS
    step = lambda j: jnp.minimum(j, OUT_W_STEPS - 1)
    w_outs = (w_out_rwkv, w_out_lru, w_out)
    return pl.pallas_call(
        _inproj_head_kernel,
        out_shape=(jax.ShapeDtypeStruct((d, LORA_COL), BF16), jax.ShapeDtypeStruct((d, LANES), BF16),
                   jax.ShapeDtypeStruct((ms, LORA_COL), F32), jax.ShapeDtypeStruct((ms, LANES), F32))
        + tuple(jax.ShapeDtypeStruct(a.shape[1:], BF16) for a in w_outs),
        grid=(nj,),
        in_specs=[
            one((ms, d), lambda j: (0, 0)),
            one((1, d), lambda j: (0, 0)),
            pl.BlockSpec((None, d, tn), lambda j: (layer, 0, j)),
            pl.BlockSpec((None, d, LANES), lambda j: (layer, 0, (j + 1) * lanes_per_tile)),
            one((None, d, LANES), lambda j: (layer, 0, SHIFT_MAIN // LANES)),
        ] + [pl.BlockSpec((None, rows(a), a.shape[2]), lambda j: (layer, step(j), 0)) for a in w_outs],
        out_specs=(
            pl.BlockSpec((d, tn), lambda j: (0, j)),
            pl.BlockSpec((d, LANES), lambda j: (0, 0)),
            pl.BlockSpec((ms, tn), lambda j: (0, j)),
            pl.BlockSpec((ms, LANES), lambda j: (0, 0)),
        ) + tuple(pl.BlockSpec((rows(a), a.shape[2]), lambda j: (step(j), 0)) for a in w_outs),
        scratch_shapes=[pltpu.VMEM((ms, d), BF16)],
        compiler_params=pltpu.CompilerParams(
            dimension_semantics=("arbitrary",), vmem_limit_bytes=VMEM_LIMIT),
        name="inproj_head",
    )(xs, g, w_in, w_in, w_in, *w_outs)


def _inproj_kernel(x_ref, g_ref, w_ref, wl_ref, z_ref, zl_ref, h_ref):
    @pl.when(pl.program_id(1) == 0)
    def _():
        h = _bf(_rms(x_ref[...], g_ref[...]))
        h_ref[...] = h
        zl_ref[...] = _dg(h, wl_ref[...], NN)

    z_ref[...] = _dg(h_ref[...], w_ref[...], NN)


def _inproj(x, g, w, wl, tm, tn):
    m, d = x.shape
    return pl.pallas_call(
        _inproj_kernel,
        out_shape=(jax.ShapeDtypeStruct((m, LORA_COL), F32), jax.ShapeDtypeStruct((m, LANES), F32)),
        grid=(m // tm, LORA_COL // tn),
        in_specs=[
            pl.BlockSpec((tm, d), lambda i, j: (i, 0)),
            pl.BlockSpec((1, d), lambda i, j: (0, 0), pipeline_mode=pl.Buffered(1)),
            pl.BlockSpec((d, tn), lambda i, j: (0, j)),
            pl.BlockSpec((d, LANES), lambda i, j: (0, 0), pipeline_mode=pl.Buffered(1)),
        ],
        out_specs=(
            pl.BlockSpec((tm, tn), lambda i, j: (i, j)),
            pl.BlockSpec((tm, LANES), lambda i, j: (i, 0)),
        ),
        scratch_shapes=[pltpu.VMEM((tm, d), BF16)],
        compiler_params=pltpu.CompilerParams(
            dimension_semantics=("arbitrary", "arbitrary"), vmem_limit_bytes=VMEM_LIMIT),
        name="inproj",
    )(x, g, w, wl)


_MU_R, _MU_K, _MU_V, _W0, _A0, _KK, _KA, _RK, _LNG, _LNB = range(10)


def _prow(pv_ref, i):
    return pv_ref[i:i + 1, :]


def _wkv_prep(zr, zk, zv, zl, pr, pk, pv, pl_, pv_ref, mul_ref, wd_ref, wa_ref, e):
    r = zr + _prow(pv_ref, _MU_R) * (pr - zr)
    k = zk + _prow(pv_ref, _MU_K) * (pk - zk)
    v = zv + _prow(pv_ref, _MU_V) * (pv - zv)
    lo = zl + mul_ref[0:1, :] * (pl_ - zl)
    lw = _dg(_bf(jnp.tanh(lo)), wd_ref[...], NN)
    la = _dg(_bf(lo), wa_ref[...], NN)
    logd = -DECAY_SCALE * _sigmoid(_prow(pv_ref, _W0) + lw)
    a = _sigmoid(_prow(pv_ref, _A0) + la)
    kk = k * _prow(pv_ref, _KK)
    kk = kk * lax.rsqrt(jnp.maximum(_segsum(kk * kk, e), 1e-24))
    k2 = k * (1.0 + (a - 1.0) * _prow(pv_ref, _KA))
    return r, k2, v, -kk, kk * a, logd


def _wkv_bonus_gate(r, k2, v, zrg, pv_ref, e):
    return _segsum(r * k2 * _prow(pv_ref, _RK), e) * v, zrg * _sigmoid(zrg)


def _wkv_norm_gate(y, bonus_v, gate, pv_ref, e):
    mu = _segsum(y, e) * (1.0 / HEAD)
    yc = y - mu
    var = _segsum(yc * yc, e) * (1.0 / HEAD)
    yn = yc * lax.rsqrt(var + GN_EPS) * _prow(pv_ref, _LNG) + _prow(pv_ref, _LNB)
    return _bf((yn + bonus_v) * gate)


def _wkv_post(y, r, k2, v, zrg, pv_ref, e):
    bonus_v, gate = _wkv_bonus_gate(r, k2, v, zrg, pv_ref, e)
    return _wkv_norm_gate(y, bonus_v, gate, pv_ref, e)


def _wkv_chunk_kernel(zr_ref, zk_ref, zv_ref, zrg_ref, zl_ref, pv_ref, mul_ref, wd_ref, wa_ref,
                      e_ref, o_ref, sout_ref, nsh_ref, s_s, prev_s, prevl_s):
    c = pl.program_id(1)
    nc = pl.num_programs(1)
    C = WKV_CHUNK
    assert C == HEAD and 2 * HEAD == LANES
    nb = zr_ref.shape[0]
    rows_all = nb * C
    seqs = range(nb)

    @pl.when(c == 0)
    def _():
        s_s[...] = jnp.zeros_like(s_s)
        prev_s[...] = jnp.zeros_like(prev_s)
        prevl_s[...] = jnp.zeros_like(prevl_s)

    first = lax.broadcasted_iota(jnp.int32, (SUBLANES, 1), 0) == 0

    def shifted(z, prev_ref, lanes):
        rolled = pltpu.roll(z, 1, 0)
        pieces = []
        for b in seqs:
            head = jnp.where(first, prev_ref[b, 0:1, lanes], rolled[b * C:b * C + SUBLANES, :])
            pieces += [head, rolled[b * C + SUBLANES:(b + 1) * C, :]]
        return jnp.concatenate(pieces, axis=0)

    def flat(ref):
        return ref[...].reshape(rows_all, ref.shape[-1])

    zr, zk, zv, zl = flat(zr_ref), flat(zk_ref), flat(zv_ref), flat(zl_ref)
    seg = [slice(RWKV_W * i, RWKV_W * (i + 1)) for i in range(3)]
    pr = shifted(zr, prev_s, seg[0])
    pk = shifted(zk, prev_s, seg[1])
    pv = shifted(zv, prev_s, seg[2])
    pl_ = shifted(zl, prevl_s, slice(0, LANES))
    for b in seqs:
        last = slice(b * C + C - 1, b * C + C)
        prev_s[b, 0:1, seg[0]] = zr[last, :]
        prev_s[b, 0:1, seg[1]] = zk[last, :]
        prev_s[b, 0:1, seg[2]] = zv[last, :]
        prevl_s[b, 0:1, :] = zl[last, :]

    e = e_ref[...]
    r, k2, v, av, bv, logd = _wkv_prep(zr, zk, zv, zl, pr, pk, pv, pl_, pv_ref, mul_ref,
                                       wd_ref, wa_ref, e)

    ti = lax.broadcasted_iota(jnp.int32, (rows_all, rows_all), 0)
    tj = lax.broadcasted_iota(jnp.int32, (rows_all, rows_all), 1)
    tri = jnp.where((ti >= tj) & ((ti & -C) == (tj & -C)), 1.0, 0.0).astype(BF16)
    d_hi = _bf(logd)
    d_r1 = logd - d_hi.astype(F32)
    d_mid = _bf(d_r1)
    d_lo = _bf(d_r1 - d_mid.astype(F32))
    cum = _dg(jnp.concatenate([tri, tri, tri], axis=1), jnp.concatenate([d_hi, d_mid, d_lo], axis=0), NN)
    e_in = jnp.exp(cum)
    e_neg = jnp.exp(-cum)
    a_t = av * jnp.exp(cum - logd)
    r_t = r * e_in
    k_t = k2 * e_neg
    b_t = bv * e_neg
    p_c = [jnp.exp(cum[b * C + C - 1:b * C + C, :]) for b in seqs]

    lane = lax.broadcasted_iota(jnp.int32, (C, LANES), 1)
    trow = lax.broadcasted_iota(jnp.int32, (C, LANES), 0)
    lo = lane < HEAD
    s_in = lane & (HEAD - 1)
    strict = s_in < trow
    incl2 = ((lax.broadcasted_iota(jnp.int32, (C, 2 * LANES), 1) & (HEAD - 1))
             <= lax.broadcasted_iota(jnp.int32, (C, 2 * LANES), 0))
    eye2 = jnp.where(s_in == trow, 1.0, 0.0).astype(F32)
    vrow = lax.broadcasted_iota(jnp.int32, (2 * HEAD, LANES), 0)
    klane = lax.broadcasted_iota(jnp.int32, (2 * HEAD, LANES), 1)
    same_head = (vrow < HEAD) == (klane < HEAD)

    def bd(x):
        z = jnp.zeros_like(x)
        return jnp.concatenate([jnp.where(lo, x, z), jnp.where(lo, z, x)], axis=0)

    npair = HEADS // 2
    units = [(b, p) for b in seqs for p in range(npair)]
    un = range(len(units))
    blk = lambda arr, i: arr[units[i][0] * C:(units[i][0] + 1) * C, LANES * units[i][1]:LANES * (units[i][1] + 1)]
    ar = [_bf(jnp.concatenate([blk(a_t, i), blk(r_t, i)], axis=0)) for i in un]
    bk = [_bf(jnp.concatenate([bd(blk(b_t, i)), bd(blk(k_t, i))], axis=0)) for i in un]
    g = [_dg(ar[i], bk[i], NT) for i in un]
    s0 = [s_s[i] for i in un]
    ars = [_dg(ar[i], _bf(s0[i]), NT) for i in un]
    vbd = [_bf(bd(blk(v, i))) for i in un]
    x = [jnp.where(strict, g[i][0:C, 0:LANES], 0.0) for i in un]
    ak = [jnp.where(strict, g[i][0:C, LANES:2 * LANES], 0.0) for i in un]
    w = [ars[i][0:C, :] + _dg(_bf(ak[i]), vbd[i], NN) for i in un]
    t = [eye2 + x[i] for i in un]
    x = [_dg(_bf(x[i]), _bf(bd(x[i])), NN) for i in un]
    for _ in range(C.bit_length() - 3):
        xt = [_dg(_bf(jnp.concatenate([x[i], t[i]], axis=0)), _bf(bd(x[i])), NN) for i in un]
        x = [xt[i][0:C, :] for i in un]
        t = [t[i] + xt[i][C:2 * C, :] for i in un]
    t = [t[i] + _dg(_bf(t[i]), _bf(bd(x[i])), NN) for i in un]
    u = [_dg(_bf(t[i]), _bf(bd(w[i])), NN) for i in un]
    rbk = [_bf(jnp.where(incl2, g[i][C:2 * C, :], 0.0)) for i in un]
    uvbd = [jnp.concatenate([_bf(bd(u[i])), vbd[i]], axis=0) for i in un]
    y = [ars[i][C:2 * C, :] + _dg(rbk[i], uvbd[i], NN) for i in un]
    uv = [_bf(jnp.concatenate([u[i], blk(v, i)], axis=0)) for i in un]
    pc = [p_c[units[i][0]][:, LANES * units[i][1]:LANES * (units[i][1] + 1)] for i in un]
    bkh = [_bf(jnp.concatenate([blk(b_t, i), blk(k_t, i)], axis=0) * pc[i]) for i in un]
    s1 = [s0[i] * pc[i] + jnp.where(same_head, _dg(uv[i], bkh[i], TN), 0.0) for i in un]
    for i in un:
        s_s[i] = s1[i]

    y_all = jnp.concatenate(
        [jnp.concatenate(y[b * npair:(b + 1) * npair], axis=1) for b in seqs], axis=0)
    o = _wkv_post(y_all, r, k2, v, flat(zrg_ref), pv_ref, e)
    o_ref[...] = o.reshape(nb, C, RWKV_W)

    @pl.when(c == nc - 1)
    def _():
        for i in un:
            b, p = units[i]
            sout_ref[b, 2 * p] = s1[i][0:HEAD, 0:HEAD]
            sout_ref[b, 2 * p + 1] = s1[i][HEAD:2 * HEAD, HEAD:2 * HEAD]
        for b in seqs:
            for q, ref in enumerate((zr_ref, zk_ref, zv_ref)):
                nsh_ref[0, b:b + 1, RWKV_W * q:RWKV_W * (q + 1)] = ref[b, C - 1:C, :]
            nsh_ref[0, b:b + 1, SHIFT_MAIN:SHIFT_MAIN + LANES] = zl_ref[b, C - 1:C, :]


def _wkv_chunk(z, zl, pvec, mul, wd, wa, e, batch, seq, nb):
    C = WKV_CHUNK
    nc = seq // C
    full = lambda shp: pl.BlockSpec(shp, lambda b, c: (0,) * len(shp))
    col = lambda j: pl.BlockSpec((nb, C, RWKV_W), lambda b, c, j=j: (b, c, j))
    return pl.pallas_call(
        _wkv_chunk_kernel,
        out_shape=(jax.ShapeDtypeStruct((batch, seq, RWKV_W), BF16),
                   jax.ShapeDtypeStruct((batch, HEADS, HEAD, HEAD), F32),
                   jax.ShapeDtypeStruct((batch // nb, nb, SHIFT_MAIN + LANES), F32)),
        grid=(batch // nb, nc),
        in_specs=[col(0), col(1), col(2), col(3),
                  pl.BlockSpec((nb, C, LANES), lambda b, c: (b, c, 0)),
                  full(pvec.shape), full(mul.shape), full(wd.shape), full(wa.shape), full(e.shape)],
        out_specs=(pl.BlockSpec((nb, C, RWKV_W), lambda b, c: (b, c, 0)),
                   pl.BlockSpec((nb, HEADS, HEAD, HEAD), lambda b, c: (b, 0, 0, 0)),
                   pl.BlockSpec((1, nb, SHIFT_MAIN + LANES), lambda b, c: (b, 0, 0))),
        scratch_shapes=[pltpu.VMEM((nb * HEADS // 2, 2 * HEAD, 2 * HEAD), F32),
                        pltpu.VMEM((nb, SUBLANES, 3 * RWKV_W), F32),
                        pltpu.VMEM((nb, SUBLANES, LANES), F32)],
        compiler_params=pltpu.CompilerParams(
            dimension_semantics=("arbitrary", "arbitrary"), vmem_limit_bytes=VMEM_LIMIT),
        name="wkv_chunk",
    )(z, z, z, z, zl, pvec, mul, wd, wa, e)


def _wkv_step_kernel(zr_ref, zk_ref, zv_ref, zrg_ref, zl_ref, sh_ref, s0_ref, pv_ref,
                     mul_ref, wd_ref, wa_ref, e_ref, o_ref, sout_ref, nsh_ref,
                     at_s, drt_s, bt_s, kt_s, dt_s, vt_s, brt_s, krt_s, yt_s, keep_s):
    h = pl.program_id(0)
    nh = pl.num_programs(0)
    nseq = zr_ref.shape[0]

    @pl.when(h == 0)
    def _():
        e = e_ref[...]
        r, k2, v, av, bv, logd = _wkv_prep(
            zr_ref[...], zk_ref[...], zv_ref[...], zl_ref[...],
            sh_ref[:, 0:RWKV_W], sh_ref[:, RWKV_W:2 * RWKV_W], sh_ref[:, 2 * RWKV_W:3 * RWKV_W],
            sh_ref[:, SHIFT_MAIN:SHIFT_MAIN + LANES], pv_ref, mul_ref, wd_ref, wa_ref, e)
        nsh_ref[:, 0:RWKV_W] = zr_ref[...]
        nsh_ref[:, RWKV_W:2 * RWKV_W] = zk_ref[...]
        nsh_ref[:, 2 * RWKV_W:SHIFT_MAIN] = zv_ref[...]
        nsh_ref[:, SHIFT_MAIN:SHIFT_MAIN + LANES] = zl_ref[...]
        d = jnp.exp(logd)
        at_s[...] = av.T
        drt_s[...] = (d * r).T
        bt_s[...] = bv.T
        kt_s[...] = k2.T
        dt_s[...] = d.T
        vt_s[...] = v.T
        brt_s[...] = jnp.sum((bv * r).T.reshape(HEADS, HEAD, nseq), axis=1)
        krt_s[...] = jnp.sum((k2 * r).T.reshape(HEADS, HEAD, nseq), axis=1)
        keep_s[0] = r
        keep_s[1] = k2
        keep_s[2] = v

    base = pl.multiple_of(h * HEAD, HEAD)
    rows = pl.ds(base, HEAD)
    a_h, dr_h, b_h, k_h, d_h = at_s[rows, :], drt_s[rows, :], bt_s[rows, :], kt_s[rows, :], dt_s[rows, :]
    br_h = brt_s[pl.ds(h, 1), :]
    kr_h = krt_s[pl.ds(h, 1), :]

    def value_rows(g, carry):
        off = pl.multiple_of(base + g * SUBLANES, SUBLANES)
        v8 = vt_s[pl.ds(off, SUBLANES), :]
        ys = []
        for j in range(SUBLANES):
            vi = g * SUBLANES + j
            s_v = s0_ref[0, vi]
            sa = jnp.sum(s_v * a_h, axis=0, keepdims=True)
            y0 = jnp.sum(s_v * dr_h, axis=0, keepdims=True)
            v_v = v8[j:j + 1, :]
            sout_ref[0, vi] = s_v * d_h + sa * b_h + v_v * k_h
            ys.append(y0 + sa * br_h + v_v * kr_h)
        yt_s[pl.ds(off, SUBLANES), :] = jnp.concatenate(ys, axis=0)
        return carry

    lax.fori_loop(0, HEAD // SUBLANES, value_rows, 0)

    @pl.when(h == nh - 1)
    def _():
        o_ref[...] = _wkv_post(yt_s[...].T, keep_s[0], keep_s[1], keep_s[2], zrg_ref[...], pv_ref,
                               e_ref[...])


def _wkv_step(z, zl, sh, s0t, pvec, mul, wd, wa, e):
    nseq = z.shape[0]
    full = lambda shp: pl.BlockSpec(shp, lambda i: (0,) * len(shp))
    col = lambda j: pl.BlockSpec((nseq, RWKV_W), lambda i, j=j: (0, j))
    st_block = (1, HEAD, HEAD, nseq)
    wide = pltpu.VMEM((RWKV_W, nseq), F32)
    return pl.pallas_call(
        _wkv_step_kernel,
        out_shape=(jax.ShapeDtypeStruct((nseq, RWKV_W), BF16),
                   jax.ShapeDtypeStruct(s0t.shape, F32),
                   jax.ShapeDtypeStruct(sh.shape, F32)),
        grid=(HEADS,),
        in_specs=[col(0), col(1), col(2), col(3),
                  full(zl.shape), full(sh.shape),
                  pl.BlockSpec(st_block, lambda i: (i, 0, 0, 0)),
                  full(pvec.shape), full(mul.shape), full(wd.shape), full(wa.shape), full(e.shape)],
        out_specs=(full((nseq, RWKV_W)),
                   pl.BlockSpec(st_block, lambda i: (i, 0, 0, 0)),
                   full(sh.shape)),
        scratch_shapes=[wide] * 6 + [pltpu.VMEM((HEADS, nseq), F32)] * 2
                       + [wide, pltpu.VMEM((3, nseq, RWKV_W), F32)],
        compiler_params=pltpu.CompilerParams(
            dimension_semantics=("arbitrary",), vmem_limit_bytes=VMEM_LIMIT),
        name="wkv_step",
    )(z, z, z, z, zl, sh, s0t, pvec, mul, wd, wa, e)


_CW0, _CW1, _CW2, _CW3, _CB, _GXB, _GAB, _LAM = range(8)


def _lru_gates(xc, lp_ref, wg_ref):
    xb = _bf(xc)
    ngroups = wg_ref.shape[0]
    gs = [_dg(xb[:, LANES * g:LANES * (g + 1)], wg_ref[g], NN) for g in range(ngroups)]
    gx_pre = jnp.concatenate([gs[g][:, 0:LANES] for g in range(ngroups)], axis=1)
    ga_pre = jnp.concatenate([gs[g][:, LANES:2 * LANES] for g in range(ngroups)], axis=1)
    gx = _sigmoid(gx_pre + _prow(lp_ref, _GXB))
    ga = _sigmoid(ga_pre + _prow(lp_ref, _GAB))
    log_a = -LRU_C * ga * _softplus(-_prow(lp_ref, _LAM))
    a = jnp.exp(log_a)
    mult = jnp.sqrt((1.0 - a) * (1.0 + a))
    return a, mult * gx * xc


def _lru_scan_rows(a, b, zg, hc):
    row8 = lax.broadcasted_iota(jnp.int32, (SUBLANES, 1), 0)
    hs = []
    for i in range(a.shape[0] // SUBLANES):
        a8 = a[SUBLANES * i:SUBLANES * (i + 1), :]
        b8 = b[SUBLANES * i:SUBLANES * (i + 1), :]
        for s in (1, 2, 4):
            keep = row8 >= s
            b8 = jnp.where(keep, a8 * pltpu.roll(b8, s, 0) + b8, b8)
            a8 = jnp.where(keep, a8 * pltpu.roll(a8, s, 0), a8)
        hb = b8 + a8 * hc
        hs.append(hb)
        hc = jnp.broadcast_to(hb[SUBLANES - 1:SUBLANES, :], hb.shape)
    return _bf(jnp.concatenate(hs, axis=0) * (zg * _sigmoid(zg))), hc


def _lru_step_kernel(zx_ref, zg_ref, conv_ref, h0_ref, lp_ref, wg_ref, o_ref, hnew_ref, cnew_ref):
    zx = zx_ref[...]
    keep = (CONV_W - 2) * LRU_W
    cnew_ref[:, 0:keep] = conv_ref[:, LRU_W:LRU_W + keep]
    cnew_ref[:, keep:keep + LRU_W] = zx
    xc = _prow(lp_ref, _CW3) * zx + _prow(lp_ref, _CB)
    for j in range(CONV_W - 1):
        xc = xc + _prow(lp_ref, j) * conv_ref[:, LRU_W * j:LRU_W * (j + 1)]
    a, b = _lru_gates(xc, lp_ref, wg_ref)
    h = a * h0_ref[...] + b
    hnew_ref[...] = h
    zg = zg_ref[...]
    o_ref[...] = _bf(h * (zg * _sigmoid(zg)))


def _lru_step(z_main, conv, h0, lp, wg):
    nb = z_main.shape[0]
    full = lambda shp: pl.BlockSpec(shp, lambda i: (0,) * len(shp))
    col = lambda j: pl.BlockSpec((nb, LRU_W), lambda i, j=j: (0, j))
    return pl.pallas_call(
        _lru_step_kernel,
        out_shape=(jax.ShapeDtypeStruct((nb, LRU_W), BF16), jax.ShapeDtypeStruct((nb, LRU_W), F32),
                   jax.ShapeDtypeStruct(conv.shape, F32)),
        grid=(1,),
        in_specs=[col(4), col(5), full(conv.shape), full(h0.shape), full(lp.shape), full(wg.shape)],
        out_specs=(full((nb, LRU_W)), full((nb, LRU_W)), full(conv.shape)),
        compiler_params=pltpu.CompilerParams(
            dimension_semantics=("arbitrary",), vmem_limit_bytes=VMEM_LIMIT),
        name="lru_step",
    )(z_main, z_main, conv, h0, lp, wg)


def _project(x, o_r, o_g, m_r, m_g, wr_ref, wg_ref, wo_ref, fg_ref, final):
    y_r = _dg(o_r, wr_ref[...], NN)
    y_g = _dg(o_g, wg_ref[...], NN)
    merged = _sigmoid(m_r) * y_r + _sigmoid(m_g) * y_g
    out = x + _dg(_bf(merged), wo_ref[...], NN)
    return _rms(out, fg_ref[...]) if final else out


def _outproj_lru_kernel(x_ref, or_ref, mr_ref, mg_ref, zx_ref, zg_ref, xs_ref, ors_ref, ogs_ref,
                        mrs_ref, mgs_ref, lp_ref, wgate_ref, wr_ref, wg_ref, wo_ref, fg_ref,
                        out_ref, hlast_ref, cnew_ref, outs_ref, og_s, mg_s, xb_s, hc_s, *, final,
                        tiles_per_seq):
    i = pl.program_id(0)
    n = pl.num_programs(0) - 2
    tm, d = x_ref.shape

    @pl.when(i == 0)
    def _():
        og_s[...] = jnp.zeros_like(og_s)
        mg_s[...] = jnp.zeros_like(mg_s)
        xb_s[...] = jnp.zeros_like(xb_s)
        hc_s[...] = jnp.zeros_like(hc_s)
        outs_ref[...] = _project(xs_ref[...], ors_ref[...], ogs_ref[...], mrs_ref[...], mgs_ref[...],
                                 wr_ref, wg_ref, wo_ref, fg_ref, final)

    og_prev = og_s[...]
    mg_prev = mg_s[...]
    t = lax.rem(jnp.minimum(i, n - 1), tiles_per_seq)
    first = t == 0

    xb_s[0:SUBLANES, :] = jnp.where(first, 0.0, xb_s[0:SUBLANES, :])
    xb_s[SUBLANES:SUBLANES + tm, :] = zx_ref[...]

    outs, a_parts, b_parts = [], [], []
    for c in range(FUSE_PIECES):
        cs = slice(c * d // FUSE_PIECES, (c + 1) * d // FUSE_PIECES)
        outs.append(x_ref[:, cs] + _dg(mg_prev, wo_ref[:, cs], NN))
        r0, r1 = SUBLANES + c * tm // FUSE_PIECES, SUBLANES + (c + 1) * tm // FUSE_PIECES
        xc = _prow(lp_ref, _CW3) * xb_s[r0:r1, :] + _prow(lp_ref, _CB)
        for j in range(1, CONV_W):
            xc = xc + _prow(lp_ref, CONV_W - 1 - j) * xb_s[r0 - j:r1 - j, :]
        a_c, b_c = _lru_gates(xc, lp_ref, wgate_ref)
        a_parts.append(a_c)
        b_parts.append(b_c)
    out = jnp.concatenate(outs, axis=1)
    out_ref[...] = _rms(out, fg_ref[...]) if final else out
    xb_s[0:SUBLANES, :] = xb_s[tm:tm + SUBLANES, :]

    o_g, hc = _lru_scan_rows(jnp.concatenate(a_parts, axis=0), jnp.concatenate(b_parts, axis=0),
                             zg_ref[...], jnp.where(first, 0.0, hc_s[...]))
    hc_s[...] = hc
    og_s[...] = o_g

    y_r = _dg(or_ref[...], wr_ref[...], NN)
    y_g = _dg(og_prev, wg_ref[...], NN)
    mg_s[...] = _bf(_sigmoid(mr_ref[...]) * y_r + _sigmoid(mg_ref[...]) * y_g)

    @pl.when((t == tiles_per_seq - 1) & (i < n))
    def _():
        hlast_ref[0] = hc[0:1, :]
        cnew_ref[0] = zx_ref[tm - (CONV_W - 1):tm, :]


def _outproj_lru(x, o_r, z, xs, o_rs, o_gs, zs, lp, wgate, w_r, w_g, w_o, fg, tm, seq, final):
    m, d = x.shape
    ms = xs.shape[0]
    n = m // tm
    tiles_per_seq = seq // tm
    const = lambda shp: pl.BlockSpec(shp, lambda i: (0,) * len(shp), pipeline_mode=pl.Buffered(1))
    back = lambda i, k: jnp.clip(i - k, 0, n - 1)
    return pl.pallas_call(
        functools.partial(_outproj_lru_kernel, final=final, tiles_per_seq=tiles_per_seq),
        out_shape=(jax.ShapeDtypeStruct((m, d), F32),
                   jax.ShapeDtypeStruct((m // seq, 1, LRU_W), F32),
                   jax.ShapeDtypeStruct((m // seq, CONV_W - 1, LRU_W), F32),
                   jax.ShapeDtypeStruct((ms, d), F32)),
        grid=(n + 2,),
        in_specs=[
            pl.BlockSpec((tm, d), lambda i: (back(i, 2), 0)),
            pl.BlockSpec((tm, RWKV_W), lambda i: (back(i, 1), 0)),
            pl.BlockSpec((tm, d), lambda i: (back(i, 1), 3)),
            pl.BlockSpec((tm, d), lambda i: (back(i, 1), 4)),
            pl.BlockSpec((tm, LRU_W), lambda i: (back(i, 0), 4)),
            pl.BlockSpec((tm, LRU_W), lambda i: (back(i, 0), 5)),
            const(xs.shape), const(o_rs.shape), const(o_gs.shape),
            pl.BlockSpec((ms, d), lambda i: (0, 3), pipeline_mode=pl.Buffered(1)),
            pl.BlockSpec((ms, d), lambda i: (0, 4), pipeline_mode=pl.Buffered(1)),
            const(lp.shape), const(wgate.shape), const(w_r.shape), const(w_g.shape), const(w_o.shape),
            const(fg.shape),
        ],
        out_specs=(pl.BlockSpec((tm, d), lambda i: (back(i, 2), 0)),
                   pl.BlockSpec((1, 1, LRU_W), lambda i: (back(i, 0) // tiles_per_seq, 0, 0)),
                   pl.BlockSpec((1, CONV_W - 1, LRU_W), lambda i: (back(i, 0) // tiles_per_seq, 0, 0)),
                   pl.BlockSpec((ms, d), lambda i: (0, 0))),
        scratch_shapes=[pltpu.VMEM((tm, LRU_W), BF16),
                        pltpu.VMEM((tm, d), BF16),
                        pltpu.VMEM((SUBLANES + tm, LRU_W), F32),
                        pltpu.VMEM((SUBLANES, LRU_W), F32)],
        compiler_params=pltpu.CompilerParams(
            dimension_semantics=("arbitrary",), vmem_limit_bytes=VMEM_LIMIT),
        name="outproj_lru",
    )(x, o_r, z, z, z, z, xs, o_rs, o_gs, zs, zs, lp, wgate, w_r, w_g, w_o, fg)


def _row_tile(m, want):
    t = min(m, want)
    assert m % t == 0, (m, t)
    return t


def _pack_params_kernel(mu_ref, w0_ref, a0_ref, kk_ref, ka_ref, rk_ref, lng_ref, lnb_ref, wdu_ref,
                        wau_ref, cw_ref, cb_ref, gxb_ref, gab_ref, lam_ref, gxw_ref, gaw_ref,
                        pvec_ref, mul_ref, wd_ref, wa_ref, e_ref, lp_ref, wg_ref):
    pvec_ref[...] = jnp.zeros_like(pvec_ref)
    for i in range(3):
        pvec_ref[_MU_R + i:_MU_R + i + 1, :] = mu_ref[:, RWKV_W * i:RWKV_W * (i + 1)]
    for row, ref in ((_W0, w0_ref), (_A0, a0_ref), (_KK, kk_ref), (_KA, ka_ref), (_RK, rk_ref),
                     (_LNG, lng_ref), (_LNB, lnb_ref)):
        pvec_ref[row:row + 1, :] = ref[...]
    mul_ref[...] = jnp.broadcast_to(mu_ref[:, 3 * RWKV_W:3 * RWKV_W + 2 * LORA], mul_ref.shape)

    zeros = jnp.zeros((LORA, RWKV_W), BF16)
    wd_ref[0:LORA, :] = _bf(wdu_ref[...])
    wd_ref[LORA:2 * LORA, :] = zeros
    wa_ref[0:LORA, :] = zeros
    wa_ref[LORA:2 * LORA, :] = _bf(wau_ref[...])

    ri = lax.broadcasted_iota(jnp.int32, (LANES, LANES), 0)
    ci = lax.broadcasted_iota(jnp.int32, (LANES, LANES), 1)
    e_ref[...] = jnp.where((ri < HEAD) == (ci < HEAD), 1.0, 0.0).astype(BF16)

    lp_ref[_CW0:_CW0 + CONV_W, :] = cw_ref[...]
    for row, ref in ((_CB, cb_ref), (_GXB, gxb_ref), (_GAB, gab_ref), (_LAM, lam_ref)):
        lp_ref[row:row + 1, :] = ref[...]

    blk = LRU_W // LRU_BLOCKS
    z = jnp.zeros((blk, blk), F32)
    for g in range(LRU_BLOCKS // 2):
        top = jnp.concatenate([gxw_ref[2 * g], z, gaw_ref[2 * g], z], axis=1)
        bot = jnp.concatenate([z, gxw_ref[2 * g + 1], z, gaw_ref[2 * g + 1]], axis=1)
        wg_ref[g] = _bf(jnp.concatenate([top, bot], axis=0))


def _pack_params(l, rwkv_mu, w_decay0, w_decay_up, w_iclr0, w_iclr_up, k_k, k_a, r_k, ln_x_g,
                 ln_x_b, conv_w, conv_b, lru_gx_w, lru_gx_b, lru_ga_w, lru_ga_b, lru_lambda):
    blk = LRU_W // LRU_BLOCKS
    assert 2 * blk == LANES and 2 * LORA == LANES
    depth = rwkv_mu.shape[0]
    row = lambda a: pl.BlockSpec((1, a.shape[-1]), lambda i: (l, 0))
    mat = lambda a: pl.BlockSpec((None,) + a.shape[1:], lambda i: (l,) + (0,) * (a.ndim - 1))
    full = lambda shp: pl.BlockSpec(shp, lambda i: (0,) * len(shp))
    rk = r_k.reshape(depth, RWKV_W)
    rows = (rwkv_mu, w_decay0, w_iclr0, k_k, k_a, rk, ln_x_g, ln_x_b)
    out_shapes = ((16, RWKV_W, F32), (SUBLANES, LANES, F32), (LANES, RWKV_W, BF16), (LANES, RWKV_W, BF16),
                  (LANES, LANES, BF16), (SUBLANES, LRU_W, F32))
    outs = tuple(jax.ShapeDtypeStruct(s[:2], s[2]) for s in out_shapes)
    outs += (jax.ShapeDtypeStruct((LRU_BLOCKS // 2, LANES, 2 * LANES), BF16),)
    return pl.pallas_call(
        _pack_params_kernel,
        out_shape=outs,
        grid=(1,),
        in_specs=[row(a) for a in rows] + [mat(w_decay_up), mat(w_iclr_up), mat(conv_w), row(conv_b),
                                          row(lru_gx_b), row(lru_ga_b), row(lru_lambda),
                                          mat(lru_gx_w), mat(lru_ga_w)],
        out_specs=tuple(full(o.shape) for o in outs),
        compiler_params=pltpu.CompilerParams(
            dimension_semantics=("arbitrary",), vmem_limit_bytes=VMEM_LIMIT),
        name="pack_params",
    )(*rows, w_decay_up, w_iclr_up, conv_w, conv_b, lru_gx_b, lru_ga_b, lru_lambda, lru_gx_w, lru_ga_w)


def kernel(x_prompt, x_sample, state_shift, state_wkv, state_conv, state_lru, norm_g, w_in, rwkv_mu,
           w_decay0, w_decay_up, w_iclr0, w_iclr_up, k_k, k_a, r_k, ln_x_g, ln_x_b, w_out_rwkv,
           conv_w, conv_b, lru_gx_w, lru_gx_b, lru_ga_w, lru_ga_b, lru_lambda, w_out_lru, w_out,
           final_norm_g):
    bp, seq, d = x_prompt.shape
    bs = x_sample.shape[0]
    assert x_sample.shape[1] == 1 and seq % WKV_CHUNK == 0
    depth = w_in.shape[0]
    xp = x_prompt.reshape(bp * seq, d)
    xs = x_sample.reshape(bs, d)
    fg = final_norm_g.reshape(1, d)
    outs = [[] for _ in range(8)]
    for l in range(depth):
        pvec, mul, wd, wa, e, lp, wg = _pack_params(
            l, rwkv_mu, w_decay0, w_decay_up, w_iclr0, w_iclr_up, k_k, k_a, r_k, ln_x_g, ln_x_b,
            conv_w, conv_b, lru_gx_w, lru_gx_b, lru_ga_w, lru_ga_b, lru_lambda)
        g = norm_g[l].reshape(1, d)
        rec = (pvec, mul, wd, wa, e)
        w, wl, zs, zls, w_r, w_g, w_o = _inproj_head(xs, g, w_in, w_out_rwkv, w_out_lru, w_out, l, INPROJ_TN)
        zp, zlp = _inproj(xp, g, w, wl, _row_tile(bp * seq, 1024), INPROJ_TN)

        s0t = jnp.transpose(state_wkv[l], (1, 2, 3, 0))
        o_rs, s_new, sh_new = _wkv_step(zs, zls, state_shift[l], s0t, *rec)
        conv = state_conv[l].reshape(bs, (CONV_W - 1) * LRU_W)
        o_gs, h_new, conv_new = _lru_step(zs, conv, state_lru[l], lp, wg)
        outs[4].append(sh_new)
        outs[5].append(jnp.transpose(s_new, (3, 0, 1, 2)))
        outs[6].append(conv_new.reshape(bs, CONV_W - 1, LRU_W))
        outs[7].append(h_new)

        zp3 = zp.reshape(bp, seq, -1)
        zlp3 = zlp.reshape(bp, seq, LANES)
        nb = max(n for n in (4, 2, 1) if bp % n == 0)
        o_r, s_new, sh_last = _wkv_chunk(zp3, zlp3, *rec, bp, seq, nb)
        o_r = o_r.reshape(bp * seq, RWKV_W)
        last = l == depth - 1
        xp, h_last, conv_last, xs = _outproj_lru(xp, o_r, zp, xs, o_rs, o_gs, zs, lp, wg, w_r, w_g, w_o,
                                                 fg, _row_tile(seq, 256), seq, last)
        outs[0].append(sh_last.reshape(bp, -1))
        outs[1].append(s_new)
        outs[2].append(conv_last)
        outs[3].append(h_last.reshape(bp, LRU_W))

    return (xp.reshape(bp, seq, d), xs.reshape(bs, 1, d)) + tuple(jnp.stack(o) for o in outs)
```

```python
import functools

import jax
import jax.numpy as jnp
from jax import lax
from jax.experimental import pallas as pl
from jax.experimental.pallas import tpu as pltpu

F32 = jnp.float32
BF16 = jnp.bfloat16

HEADS = 16
HEAD = 64
RWKV_W = HEADS * HEAD
LORA = 64
LRU_W = 1024
LRU_BLOCKS = 16
CONV_W = 4
LRU_C = 8.0
RMS_EPS = 1e-6
GN_EPS = 1e-5 * HEAD
DECAY_SCALE = 0.6065306597126334

LANES = 128
SUBLANES = 8
WKV_CHUNK = 64
VMEM_LIMIT = 60 * 1024 * 1024

NN = (((1,), (0,)), ((), ()))
NT = (((1,), (1,)), ((), ()))
TN = (((0,), (0,)), ((), ()))


def _bf(x):
    return x.astype(BF16)


def _dg(a, b, dn):
    return lax.dot_general(a, b, dn, preferred_element_type=F32)


def _softplus(x):
    return jnp.maximum(x, 0.0) + jnp.log1p(jnp.exp(-jnp.abs(x)))


def _sigmoid(x):
    return 1.0 / (1.0 + jnp.exp(-x))


def _segsum(x, e):
    rows, n = x.shape[0], x.shape[1] // LANES
    stacked = jnp.concatenate([x[:, LANES * j:LANES * (j + 1)] for j in range(n)], axis=0)
    s = _dg(_bf(stacked), e, NN)
    return jnp.concatenate([s[rows * j:rows * (j + 1), :] for j in range(n)], axis=1)


def _rms(x, g):
    return x * lax.rsqrt(jnp.mean(x * x, axis=-1, keepdims=True) + RMS_EPS) * g


SHIFT_MAIN = 3 * RWKV_W
LORA_COL = 10 * RWKV_W
LORA_BLOCK = LORA_COL // LANES
FUSE_PIECES = 8
INPROJ_TN = 1024
INPROJ_WIDE_TN = 2048
NORM_ROWS = 128
OUT_W_STEPS = 8


def _inproj_head_kernel(xs_ref, g_ref, w_ref, wt_ref, wlo_ref, wr_ref, wg_ref, wo_ref, wb_ref, wl_ref,
                        zs_ref, zls_ref, wrb_ref, wgb_ref, wob_ref, hs_ref):
    j = pl.program_id(0)
    tn = w_ref.shape[-1]

    @pl.when(j == 0)
    def _():
        hs_ref[...] = _bf(_rms(xs_ref[...], g_ref[...]))
        wl_ref[...] = _bf(wlo_ref[...])
        zls_ref[...] = _dg(hs_ref[...], wl_ref[...], NN)

    @pl.when(j < SHIFT_MAIN // tn)
    def _():
        wb_ref[...] = _bf(w_ref[...])

    @pl.when(j >= SHIFT_MAIN // tn)
    def _():
        wb_ref[:, 0:tn - 2 * LORA] = _bf(w_ref[:, 2 * LORA:tn])
        wb_ref[:, tn - 2 * LORA:tn] = _bf(wt_ref[...])

    zs_ref[...] = _dg(hs_ref[...], wb_ref[...], NN)

    @pl.when(j < OUT_W_STEPS)
    def _():
        wrb_ref[...] = _bf(wr_ref[...])
        wgb_ref[...] = _bf(wg_ref[...])
        wob_ref[...] = _bf(wo_ref[...])


def _inproj_head(xs, g, w_in, w_out_rwkv, w_out_lru, w_out, layer, tn):
    ms, d = xs.shape
    n = w_in.shape[-1]
    assert n == LORA_COL + 2 * LORA and SHIFT_MAIN % tn == 0 and 2 * LORA == LANES
    nj = LORA_COL // tn
    assert nj >= OUT_W_STEPS
    lanes_per_tile = tn // LANES
    one = lambda shp, imap: pl.BlockSpec(shp, imap, pipeline_mode=pl.Buffered(1))
    rows = lambda a: a.shape[1] // OUT_W_STEPS
    step = lambda j: jnp.minimum(j, OUT_W_STEPS - 1)
    w_outs = (w_out_rwkv, w_out_lru, w_out)
    return pl.pallas_call(
        _inproj_head_kernel,
        out_shape=(jax.ShapeDtypeStruct((d, LORA_COL), BF16), jax.ShapeDtypeStruct((d, LANES), BF16),
                   jax.ShapeDtypeStruct((ms, LORA_COL), F32), jax.ShapeDtypeStruct((ms, LANES), F32))
        + tuple(jax.ShapeDtypeStruct(a.shape[1:], BF16) for a in w_outs),
        grid=(nj,),
        in_specs=[
            one((ms, d), lambda j: (0, 0)),
            one((1, d), lambda j: (0, 0)),
            pl.BlockSpec((None, d, tn), lambda j: (layer, 0, j)),
            pl.BlockSpec((None, d, LANES), lambda j: (layer, 0, (j + 1) * lanes_per_tile)),
            one((None, d, LANES), lambda j: (layer, 0, SHIFT_MAIN // LANES)),
        ] + [pl.BlockSpec((None, rows(a), a.shape[2]), lambda j: (layer, step(j), 0)) for a in w_outs],
        out_specs=(
            pl.BlockSpec((d, tn), lambda j: (0, j)),
            pl.BlockSpec((d, LANES), lambda j: (0, 0)),
            pl.BlockSpec((ms, tn), lambda j: (0, j)),
            pl.BlockSpec((ms, LANES), lambda j: (0, 0)),
        ) + tuple(pl.BlockSpec((rows(a), a.shape[2]), lambda j: (step(j), 0)) for a in w_outs),
        scratch_shapes=[pltpu.VMEM((ms, d), BF16)],
        compiler_params=pltpu.CompilerParams(
            dimension_semantics=("arbitrary",), vmem_limit_bytes=VMEM_LIMIT),
        name="inproj_head",
    )(xs, g, w_in, w_in, w_in, *w_outs)


def _inproj_kernel(x_ref, g_ref, w_ref, wl_ref, z_ref, zl_ref, h_ref):
    @pl.when(pl.program_id(1) == 0)
    def _():
        for r in range(0, x_ref.shape[0], NORM_ROWS):
            rows = slice(r, r + NORM_ROWS)
            h_ref[rows, :] = _bf(_rms(x_ref[rows, :], g_ref[...]))
        zl_ref[...] = _dg(h_ref[...], wl_ref[...], NN)

    z_ref[...] = _dg(h_ref[...], w_ref[...], NN)


def _inproj(x, g, w, wl, tm, tn):
    m, d = x.shape
    return pl.pallas_call(
        _inproj_kernel,
        out_shape=(jax.ShapeDtypeStruct((m, LORA_COL), F32), jax.ShapeDtypeStruct((m, LANES), F32)),
        grid=(m // tm, LORA_COL // tn),
        in_specs=[
            pl.BlockSpec((tm, d), lambda i, j: (i, 0)),
            pl.BlockSpec((1, d), lambda i, j: (0, 0), pipeline_mode=pl.Buffered(1)),
            pl.BlockSpec((d, tn), lambda i, j: (0, j)),
            pl.BlockSpec((d, LANES), lambda i, j: (0, 0), pipeline_mode=pl.Buffered(1)),
        ],
        out_specs=(
            pl.BlockSpec((tm, tn), lambda i, j: (i, j)),
            pl.BlockSpec((tm, LANES), lambda i, j: (i, 0)),
        ),
        scratch_shapes=[pltpu.VMEM((tm, d), BF16)],
        compiler_params=pltpu.CompilerParams(
            dimension_semantics=("arbitrary", "arbitrary"), vmem_limit_bytes=VMEM_LIMIT),
        name="inproj",
    )(x, g, w, wl)


_MU_R, _MU_K, _MU_V, _W0, _A0, _KK, _KA, _RK, _LNG, _LNB = range(10)


def _prow(pv_ref, i):
    return pv_ref[i:i + 1, :]


def _wkv_prep(zr, zk, zv, zl, pr, pk, pv, pl_, pv_ref, mul_ref, wd_ref, wa_ref, e):
    r = zr + _prow(pv_ref, _MU_R) * (pr - zr)
    k = zk + _prow(pv_ref, _MU_K) * (pk - zk)
    v = zv + _prow(pv_ref, _MU_V) * (pv - zv)
    lo = zl + mul_ref[0:1, :] * (pl_ - zl)
    lw = _dg(_bf(jnp.tanh(lo)), wd_ref[...], NN)
    la = _dg(_bf(lo), wa_ref[...], NN)
    logd = -DECAY_SCALE * _sigmoid(_prow(pv_ref, _W0) + lw)
    a = _sigmoid(_prow(pv_ref, _A0) + la)
    kk = k * _prow(pv_ref, _KK)
    kk = kk * lax.rsqrt(jnp.maximum(_segsum(kk * kk, e), 1e-24))
    k2 = k * (1.0 + (a - 1.0) * _prow(pv_ref, _KA))
    return r, k2, v, -kk, kk * a, logd


def _wkv_bonus_gate(r, k2, v, zrg, pv_ref, e):
    return _segsum(r * k2 * _prow(pv_ref, _RK), e) * v, zrg * _sigmoid(zrg)


def _wkv_norm_gate(y, bonus_v, gate, pv_ref, e):
    mu = _segsum(y, e) * (1.0 / HEAD)
    yc = y - mu
    var = _segsum(yc * yc, e) * (1.0 / HEAD)
    yn = yc * lax.rsqrt(var + GN_EPS) * _prow(pv_ref, _LNG) + _prow(pv_ref, _LNB)
    return _bf((yn + bonus_v) * gate)


def _wkv_post(y, r, k2, v, zrg, pv_ref, e):
    bonus_v, gate = _wkv_bonus_gate(r, k2, v, zrg, pv_ref, e)
    return _wkv_norm_gate(y, bonus_v, gate, pv_ref, e)


def _wkv_chunk_kernel(zr_ref, zk_ref, zv_ref, zrg_ref, zl_ref, pv_ref, mul_ref, wd_ref, wa_ref,
                      e_ref, o_ref, sout_ref, nsh_ref, s_s, prev_s, prevl_s):
    c = pl.program_id(1)
    nc = pl.num_programs(1)
    C = WKV_CHUNK
    assert C == HEAD and 2 * HEAD == LANES
    nb = zr_ref.shape[0]
    rows_all = nb * C
    seqs = range(nb)

    @pl.when(c == 0)
    def _():
        s_s[...] = jnp.zeros_like(s_s)
        prev_s[...] = jnp.zeros_like(prev_s)
        prevl_s[...] = jnp.zeros_like(prevl_s)

    first = lax.broadcasted_iota(jnp.int32, (SUBLANES, 1), 0) == 0

    def shifted(z, prev_ref, lanes):
        rolled = pltpu.roll(z, 1, 0)
        pieces = []
        for b in seqs:
            head = jnp.where(first, prev_ref[b, 0:1, lanes], rolled[b * C:b * C + SUBLANES, :])
            pieces += [head, rolled[b * C + SUBLANES:(b + 1) * C, :]]
        return jnp.concatenate(pieces, axis=0)

    def flat(ref):
        return ref[...].reshape(rows_all, ref.shape[-1])

    zr, zk, zv, zl = flat(zr_ref), flat(zk_ref), flat(zv_ref), flat(zl_ref)
    seg = [slice(RWKV_W * i, RWKV_W * (i + 1)) for i in range(3)]
    pr = shifted(zr, prev_s, seg[0])
    pk = shifted(zk, prev_s, seg[1])
    pv = shifted(zv, prev_s, seg[2])
    pl_ = shifted(zl, prevl_s, slice(0, LANES))
    for b in seqs:
        last = slice(b * C + C - 1, b * C + C)
        prev_s[b, 0:1, seg[0]] = zr[last, :]
        prev_s[b, 0:1, seg[1]] = zk[last, :]
        prev_s[b, 0:1, seg[2]] = zv[last, :]
        prevl_s[b, 0:1, :] = zl[last, :]

    e = e_ref[...]
    r, k2, v, av, bv, logd = _wkv_prep(zr, zk, zv, zl, pr, pk, pv, pl_, pv_ref, mul_ref,
                                       wd_ref, wa_ref, e)

    ti = lax.broadcasted_iota(jnp.int32, (rows_all, rows_all), 0)
    tj = lax.broadcasted_iota(jnp.int32, (rows_all, rows_all), 1)
    tri = jnp.where((ti >= tj) & ((ti & -C) == (tj & -C)), 1.0, 0.0).astype(BF16)
    d_hi = _bf(logd)
    d_r1 = logd - d_hi.astype(F32)
    d_mid = _bf(d_r1)
    d_lo = _bf(d_r1 - d_mid.astype(F32))
    cum = _dg(jnp.concatenate([tri, tri, tri], axis=1), jnp.concatenate([d_hi, d_mid, d_lo], axis=0), NN)
    e_in = jnp.exp(cum)
    e_neg = jnp.exp(-cum)
    a_t = av * jnp.exp(cum - logd)
    r_t = r * e_in
    k_t = k2 * e_neg
    b_t = bv * e_neg
    p_c = [jnp.exp(cum[b * C + C - 1:b * C + C, :]) for b in seqs]

    lane = lax.broadcasted_iota(jnp.int32, (C, LANES), 1)
    trow = lax.broadcasted_iota(jnp.int32, (C, LANES), 0)
    lo = lane < HEAD
    s_in = lane & (HEAD - 1)
    strict = s_in < trow
    incl2 = ((lax.broadcasted_iota(jnp.int32, (C, 2 * LANES), 1) & (HEAD - 1))
             <= lax.broadcasted_iota(jnp.int32, (C, 2 * LANES), 0))
    eye2 = jnp.where(s_in == trow, 1.0, 0.0).astype(F32)
    vrow = lax.broadcasted_iota(jnp.int32, (2 * HEAD, LANES), 0)
    klane = lax.broadcasted_iota(jnp.int32, (2 * HEAD, LANES), 1)
    same_head = (vrow < HEAD) == (klane < HEAD)

    def bd(x):
        z = jnp.zeros_like(x)
        return jnp.concatenate([jnp.where(lo, x, z), jnp.where(lo, z, x)], axis=0)

    npair = HEADS // 2
    units = [(b, p) for b in seqs for p in range(npair)]
    un = range(len(units))
    blk = lambda arr, i: arr[units[i][0] * C:(units[i][0] + 1) * C, LANES * units[i][1]:LANES * (units[i][1] + 1)]
    ar = [_bf(jnp.concatenate([blk(a_t, i), blk(r_t, i)], axis=0)) for i in un]
    bk = [_bf(jnp.concatenate([bd(blk(b_t, i)), bd(blk(k_t, i))], axis=0)) for i in un]
    g = [_dg(ar[i], bk[i], NT) for i in un]
    s0 = [s_s[i] for i in un]
    ars = [_dg(ar[i], _bf(s0[i]), NT) for i in un]
    vbd = [_bf(bd(blk(v, i))) for i in un]
    x = [jnp.where(strict, g[i][0:C, 0:LANES], 0.0) for i in un]
    ak = [jnp.where(strict, g[i][0:C, LANES:2 * LANES], 0.0) for i in un]
    w = [ars[i][0:C, :] + _dg(_bf(ak[i]), vbd[i], NN) for i in un]
    t = [eye2 + x[i] for i in un]
    x = [_dg(_bf(x[i]), _bf(bd(x[i])), NN) for i in un]
    for _ in range(C.bit_length() - 3):
        xt = [_dg(_bf(jnp.concatenate([x[i], t[i]], axis=0)), _bf(bd(x[i])), NN) for i in un]
        x = [xt[i][0:C, :] for i in un]
        t = [t[i] + xt[i][C:2 * C, :] for i in un]
    t = [t[i] + _dg(_bf(t[i]), _bf(bd(x[i])), NN) for i in un]
    u = [_dg(_bf(t[i]), _bf(bd(w[i])), NN) for i in un]
    rbk = [_bf(jnp.where(incl2, g[i][C:2 * C, :], 0.0)) for i in un]
    uvbd = [jnp.concatenate([_bf(bd(u[i])), vbd[i]], axis=0) for i in un]
    y = [ars[i][C:2 * C, :] + _dg(rbk[i], uvbd[i], NN) for i in un]
    uv = [_bf(jnp.concatenate([u[i], blk(v, i)], axis=0)) for i in un]
    pc = [p_c[units[i][0]][:, LANES * units[i][1]:LANES * (units[i][1] + 1)] for i in un]
    bkh = [_bf(jnp.concatenate([blk(b_t, i), blk(k_t, i)], axis=0) * pc[i]) for i in un]
    s1 = [s0[i] * pc[i] + jnp.where(same_head, _dg(uv[i], bkh[i], TN), 0.0) for i in un]
    for i in un:
        s_s[i] = s1[i]

    y_all = jnp.concatenate(
        [jnp.concatenate(y[b * npair:(b + 1) * npair], axis=1) for b in seqs], axis=0)
    o = _wkv_post(y_all, r, k2, v, flat(zrg_ref), pv_ref, e)
    o_ref[...] = o.reshape(nb, C, RWKV_W)

    @pl.when(c == nc - 1)
    def _():
        for i in un:
            b, p = units[i]
            sout_ref[b, 2 * p] = s1[i][0:HEAD, 0:HEAD]
            sout_ref[b, 2 * p + 1] = s1[i][HEAD:2 * HEAD, HEAD:2 * HEAD]
        for b in seqs:
            for q, ref in enumerate((zr_ref, zk_ref, zv_ref)):
                nsh_ref[0, b:b + 1, RWKV_W * q:RWKV_W * (q + 1)] = ref[b, C - 1:C, :]
            nsh_ref[0, b:b + 1, SHIFT_MAIN:SHIFT_MAIN + LANES] = zl_ref[b, C - 1:C, :]


def _wkv_chunk(z, zl, pvec, mul, wd, wa, e, batch, seq, nb):
    C = WKV_CHUNK
    nc = seq // C
    full = lambda shp: pl.BlockSpec(shp, lambda b, c: (0,) * len(shp))
    col = lambda j: pl.BlockSpec((nb, C, RWKV_W), lambda b, c, j=j: (b, c, j))
    return pl.pallas_call(
        _wkv_chunk_kernel,
        out_shape=(jax.ShapeDtypeStruct((batch, seq, RWKV_W), BF16),
                   jax.ShapeDtypeStruct((batch, HEADS, HEAD, HEAD), F32),
                   jax.ShapeDtypeStruct((batch // nb, nb, SHIFT_MAIN + LANES), F32)),
        grid=(batch // nb, nc),
        in_specs=[col(0), col(1), col(2), col(3),
                  pl.BlockSpec((nb, C, LANES), lambda b, c: (b, c, 0)),
                  full(pvec.shape), full(mul.shape), full(wd.shape), full(wa.shape), full(e.shape)],
        out_specs=(pl.BlockSpec((nb, C, RWKV_W), lambda b, c: (b, c, 0)),
                   pl.BlockSpec((nb, HEADS, HEAD, HEAD), lambda b, c: (b, 0, 0, 0)),
                   pl.BlockSpec((1, nb, SHIFT_MAIN + LANES), lambda b, c: (b, 0, 0))),
        scratch_shapes=[pltpu.VMEM((nb * HEADS // 2, 2 * HEAD, 2 * HEAD), F32),
                        pltpu.VMEM((nb, SUBLANES, 3 * RWKV_W), F32),
                        pltpu.VMEM((nb, SUBLANES, LANES), F32)],
        compiler_params=pltpu.CompilerParams(
            dimension_semantics=("arbitrary", "arbitrary"), vmem_limit_bytes=VMEM_LIMIT),
        name="wkv_chunk",
    )(z, z, z, z, zl, pvec, mul, wd, wa, e)


def _wkv_step_kernel(zr_ref, zk_ref, zv_ref, zrg_ref, zl_ref, sh_ref, s0_ref, pv_ref,
                     mul_ref, wd_ref, wa_ref, e_ref, o_ref, sout_ref, nsh_ref,
                     at_s, drt_s, bt_s, kt_s, dt_s, vt_s, brt_s, krt_s, yt_s, keep_s):
    h = pl.program_id(0)
    nh = pl.num_programs(0)
    nseq = zr_ref.shape[0]

    @pl.when(h == 0)
    def _():
        e = e_ref[...]
        r, k2, v, av, bv, logd = _wkv_prep(
            zr_ref[...], zk_ref[...], zv_ref[...], zl_ref[...],
            sh_ref[:, 0:RWKV_W], sh_ref[:, RWKV_W:2 * RWKV_W], sh_ref[:, 2 * RWKV_W:3 * RWKV_W],
            sh_ref[:, SHIFT_MAIN:SHIFT_MAIN + LANES], pv_ref, mul_ref, wd_ref, wa_ref, e)
        nsh_ref[:, 0:RWKV_W] = zr_ref[...]
        nsh_ref[:, RWKV_W:2 * RWKV_W] = zk_ref[...]
        nsh_ref[:, 2 * RWKV_W:SHIFT_MAIN] = zv_ref[...]
        nsh_ref[:, SHIFT_MAIN:SHIFT_MAIN + LANES] = zl_ref[...]
        d = jnp.exp(logd)
        at_s[...] = av.T
        drt_s[...] = (d * r).T
        bt_s[...] = bv.T
        kt_s[...] = k2.T
        dt_s[...] = d.T
        vt_s[...] = v.T
        brt_s[...] = jnp.sum((bv * r).T.reshape(HEADS, HEAD, nseq), axis=1)
        krt_s[...] = jnp.sum((k2 * r).T.reshape(HEADS, HEAD, nseq), axis=1)
        keep_s[0] = r
        keep_s[1] = k2
        keep_s[2] = v

    base = pl.multiple_of(h * HEAD, HEAD)
    rows = pl.ds(base, HEAD)
    a_h, dr_h, b_h, k_h, d_h = at_s[rows, :], drt_s[rows, :], bt_s[rows, :], kt_s[rows, :], dt_s[rows, :]
    br_h = brt_s[pl.ds(h, 1), :]
    kr_h = krt_s[pl.ds(h, 1), :]

    def value_rows(g, carry):
        off = pl.multiple_of(base + g * SUBLANES, SUBLANES)
        v8 = vt_s[pl.ds(off, SUBLANES), :]
        ys = []
        for j in range(SUBLANES):
            vi = g * SUBLANES + j
            s_v = s0_ref[0, vi]
            sa = jnp.sum(s_v * a_h, axis=0, keepdims=True)
            y0 = jnp.sum(s_v * dr_h, axis=0, keepdims=True)
            v_v = v8[j:j + 1, :]
            sout_ref[0, vi] = s_v * d_h + sa * b_h + v_v * k_h
            ys.append(y0 + sa * br_h + v_v * kr_h)
        yt_s[pl.ds(off, SUBLANES), :] = jnp.concatenate(ys, axis=0)
        return carry

    lax.fori_loop(0, HEAD // SUBLANES, value_rows, 0)

    @pl.when(h == nh - 1)
    def _():
        o_ref[...] = _wkv_post(yt_s[...].T, keep_s[0], keep_s[1], keep_s[2], zrg_ref[...], pv_ref,
                               e_ref[...])


def _wkv_step(z, zl, sh, s0t, pvec, mul, wd, wa, e):
    nseq = z.shape[0]
    full = lambda shp: pl.BlockSpec(shp, lambda i: (0,) * len(shp))
    col = lambda j: pl.BlockSpec((nseq, RWKV_W), lambda i, j=j: (0, j))
    st_block = (1, HEAD, HEAD, nseq)
    wide = pltpu.VMEM((RWKV_W, nseq), F32)
    return pl.pallas_call(
        _wkv_step_kernel,
        out_shape=(jax.ShapeDtypeStruct((nseq, RWKV_W), BF16),
                   jax.ShapeDtypeStruct(s0t.shape, F32),
                   jax.ShapeDtypeStruct(sh.shape, F32)),
        grid=(HEADS,),
        in_specs=[col(0), col(1), col(2), col(3),
                  full(zl.shape), full(sh.shape),
                  pl.BlockSpec(st_block, lambda i: (i, 0, 0, 0)),
                  full(pvec.shape), full(mul.shape), full(wd.shape), full(wa.shape), full(e.shape)],
        out_specs=(full((nseq, RWKV_W)),
                   pl.BlockSpec(st_block, lambda i: (i, 0, 0, 0)),
                   full(sh.shape)),
        scratch_shapes=[wide] * 6 + [pltpu.VMEM((HEADS, nseq), F32)] * 2
                       + [wide, pltpu.VMEM((3, nseq, RWKV_W), F32)],
        compiler_params=pltpu.CompilerParams(
            dimension_semantics=("arbitrary",), vmem_limit_bytes=VMEM_LIMIT),
        name="wkv_step",
    )(z, z, z, z, zl, sh, s0t, pvec, mul, wd, wa, e)


_CW0, _CW1, _CW2, _CW3, _CB, _GXB, _GAB, _LAM = range(8)


def _lru_gates(xc, lp_ref, wg_ref):
    xb = _bf(xc)
    ngroups = wg_ref.shape[0]
    gs = [_dg(xb[:, LANES * g:LANES * (g + 1)], wg_ref[g], NN) for g in range(ngroups)]
    gx_pre = jnp.concatenate([gs[g][:, 0:LANES] for g in range(ngroups)], axis=1)
    ga_pre = jnp.concatenate([gs[g][:, LANES:2 * LANES] for g in range(ngroups)], axis=1)
    gx = _sigmoid(gx_pre + _prow(lp_ref, _GXB))
    ga = _sigmoid(ga_pre + _prow(lp_ref, _GAB))
    log_a = -LRU_C * ga * _softplus(-_prow(lp_ref, _LAM))
    a = jnp.exp(log_a)
    mult = jnp.sqrt((1.0 - a) * (1.0 + a))
    return a, mult * gx * xc


def _lru_scan_rows(a, b, zg, hc):
    row8 = lax.broadcasted_iota(jnp.int32, (SUBLANES, 1), 0)
    hs = []
    for i in range(a.shape[0] // SUBLANES):
        a8 = a[SUBLANES * i:SUBLANES * (i + 1), :]
        b8 = b[SUBLANES * i:SUBLANES * (i + 1), :]
        for s in (1, 2, 4):
            keep = row8 >= s
            b8 = jnp.where(keep, a8 * pltpu.roll(b8, s, 0) + b8, b8)
            a8 = jnp.where(keep, a8 * pltpu.roll(a8, s, 0), a8)
        hb = b8 + a8 * hc
        hs.append(hb)
        hc = jnp.broadcast_to(hb[SUBLANES - 1:SUBLANES, :], hb.shape)
    return _bf(jnp.concatenate(hs, axis=0) * (zg * _sigmoid(zg))), hc


def _lru_step_kernel(zx_ref, zg_ref, conv_ref, h0_ref, lp_ref, wg_ref, o_ref, hnew_ref, cnew_ref):
    zx = zx_ref[...]
    keep = (CONV_W - 2) * LRU_W
    cnew_ref[:, 0:keep] = conv_ref[:, LRU_W:LRU_W + keep]
    cnew_ref[:, keep:keep + LRU_W] = zx
    xc = _prow(lp_ref, _CW3) * zx + _prow(lp_ref, _CB)
    for j in range(CONV_W - 1):
        xc = xc + _prow(lp_ref, j) * conv_ref[:, LRU_W * j:LRU_W * (j + 1)]
    a, b = _lru_gates(xc, lp_ref, wg_ref)
    h = a * h0_ref[...] + b
    hnew_ref[...] = h
    zg = zg_ref[...]
    o_ref[...] = _bf(h * (zg * _sigmoid(zg)))


def _lru_step(z_main, conv, h0, lp, wg):
    nb = z_main.shape[0]
    full = lambda shp: pl.BlockSpec(shp, lambda i: (0,) * len(shp))
    col = lambda j: pl.BlockSpec((nb, LRU_W), lambda i, j=j: (0, j))
    return pl.pallas_call(
        _lru_step_kernel,
        out_shape=(jax.ShapeDtypeStruct((nb, LRU_W), BF16), jax.ShapeDtypeStruct((nb, LRU_W), F32),
                   jax.ShapeDtypeStruct(conv.shape, F32)),
        grid=(1,),
        in_specs=[col(4), col(5), full(conv.shape), full(h0.shape), full(lp.shape), full(wg.shape)],
        out_specs=(full((nb, LRU_W)), full((nb, LRU_W)), full(conv.shape)),
        compiler_params=pltpu.CompilerParams(
            dimension_semantics=("arbitrary",), vmem_limit_bytes=VMEM_LIMIT),
        name="lru_step",
    )(z_main, z_main, conv, h0, lp, wg)


def _project(x, o_r, o_g, m_r, m_g, wr_ref, wg_ref, wo_ref, fg_ref, final):
    y_r = _dg(o_r, wr_ref[...], NN)
    y_g = _dg(o_g, wg_ref[...], NN)
    merged = _sigmoid(m_r) * y_r + _sigmoid(m_g) * y_g
    out = x + _dg(_bf(merged), wo_ref[...], NN)
    return _rms(out, fg_ref[...]) if final else out


def _outproj_lru_kernel(x_ref, or_ref, mr_ref, mg_ref, zx_ref, zg_ref, xs_ref, ors_ref, ogs_ref,
                        mrs_ref, mgs_ref, lp_ref, wgate_ref, wr_ref, wg_ref, wo_ref, fg_ref,
                        out_ref, hlast_ref, cnew_ref, outs_ref, og_s, mg_s, xb_s, hc_s, *, final,
                        tiles_per_seq):
    i = pl.program_id(0)
    n = pl.num_programs(0) - 2
    tm, d = x_ref.shape

    @pl.when(i == 0)
    def _():
        og_s[...] = jnp.zeros_like(og_s)
        mg_s[...] = jnp.zeros_like(mg_s)
        xb_s[...] = jnp.zeros_like(xb_s)
        hc_s[...] = jnp.zeros_like(hc_s)
        outs_ref[...] = _project(xs_ref[...], ors_ref[...], ogs_ref[...], mrs_ref[...], mgs_ref[...],
                                 wr_ref, wg_ref, wo_ref, fg_ref, final)

    og_prev = og_s[...]
    mg_prev = mg_s[...]
    t = lax.rem(jnp.minimum(i, n - 1), tiles_per_seq)
    first = t == 0

    xb_s[0:SUBLANES, :] = jnp.where(first, 0.0, xb_s[0:SUBLANES, :])
    xb_s[SUBLANES:SUBLANES + tm, :] = zx_ref[...]

    outs, a_parts, b_parts = [], [], []
    for c in range(FUSE_PIECES):
        cs = slice(c * d // FUSE_PIECES, (c + 1) * d // FUSE_PIECES)
        outs.append(x_ref[:, cs] + _dg(mg_prev, wo_ref[:, cs], NN))
        r0, r1 = SUBLANES + c * tm // FUSE_PIECES, SUBLANES + (c + 1) * tm // FUSE_PIECES
        xc = _prow(lp_ref, _CW3) * xb_s[r0:r1, :] + _prow(lp_ref, _CB)
        for j in range(1, CONV_W):
            xc = xc + _prow(lp_ref, CONV_W - 1 - j) * xb_s[r0 - j:r1 - j, :]
        a_c, b_c = _lru_gates(xc, lp_ref, wgate_ref)
        a_parts.append(a_c)
        b_parts.append(b_c)
    out = jnp.concatenate(outs, axis=1)
    out_ref[...] = _rms(out, fg_ref[...]) if final else out
    xb_s[0:SUBLANES, :] = xb_s[tm:tm + SUBLANES, :]

    o_g, hc = _lru_scan_rows(jnp.concatenate(a_parts, axis=0), jnp.concatenate(b_parts, axis=0),
                             zg_ref[...], jnp.where(first, 0.0, hc_s[...]))
    hc_s[...] = hc
    og_s[...] = o_g

    y_r = _dg(or_ref[...], wr_ref[...], NN)
    y_g = _dg(og_prev, wg_ref[...], NN)
    mg_s[...] = _bf(_sigmoid(mr_ref[...]) * y_r + _sigmoid(mg_ref[...]) * y_g)

    @pl.when((t == tiles_per_seq - 1) & (i < n))
    def _():
        hlast_ref[0] = hc[0:1, :]
        cnew_ref[0] = zx_ref[tm - (CONV_W - 1):tm, :]


def _outproj_lru(x, o_r, z, xs, o_rs, o_gs, zs, lp, wgate, w_r, w_g, w_o, fg, tm, seq, final):
    m, d = x.shape
    ms = xs.shape[0]
    n = m // tm
    tiles_per_seq = seq // tm
    const = lambda shp: pl.BlockSpec(shp, lambda i: (0,) * len(shp), pipeline_mode=pl.Buffered(1))
    back = lambda i, k: jnp.clip(i - k, 0, n - 1)
    return pl.pallas_call(
        functools.partial(_outproj_lru_kernel, final=final, tiles_per_seq=tiles_per_seq),
        out_shape=(jax.ShapeDtypeStruct((m, d), F32),
                   jax.ShapeDtypeStruct((m // seq, 1, LRU_W), F32),
                   jax.ShapeDtypeStruct((m // seq, CONV_W - 1, LRU_W), F32),
                   jax.ShapeDtypeStruct((ms, d), F32)),
        grid=(n + 2,),
        in_specs=[
            pl.BlockSpec((tm, d), lambda i: (back(i, 2), 0)),
            pl.BlockSpec((tm, RWKV_W), lambda i: (back(i, 1), 0)),
            pl.BlockSpec((tm, d), lambda i: (back(i, 1), 3)),
            pl.BlockSpec((tm, d), lambda i: (back(i, 1), 4)),
            pl.BlockSpec((tm, LRU_W), lambda i: (back(i, 0), 4)),
            pl.BlockSpec((tm, LRU_W), lambda i: (back(i, 0), 5)),
            const(xs.shape), const(o_rs.shape), const(o_gs.shape),
            pl.BlockSpec((ms, d), lambda i: (0, 3), pipeline_mode=pl.Buffered(1)),
            pl.BlockSpec((ms, d), lambda i: (0, 4), pipeline_mode=pl.Buffered(1)),
            const(lp.shape), const(wgate.shape), const(w_r.shape), const(w_g.shape), const(w_o.shape),
            const(fg.shape),
        ],
        out_specs=(pl.BlockSpec((tm, d), lambda i: (back(i, 2), 0)),
                   pl.BlockSpec((1, 1, LRU_W), lambda i: (back(i, 0) // tiles_per_seq, 0, 0)),
                   pl.BlockSpec((1, CONV_W - 1, LRU_W), lambda i: (back(i, 0) // tiles_per_seq, 0, 0)),
                   pl.BlockSpec((ms, d), lambda i: (0, 0))),
        scratch_shapes=[pltpu.VMEM((tm, LRU_W), BF16),
                        pltpu.VMEM((tm, d), BF16),
                        pltpu.VMEM((SUBLANES + tm, LRU_W), F32),
                        pltpu.VMEM((SUBLANES, LRU_W), F32)],
        compiler_params=pltpu.CompilerParams(
            dimension_semantics=("arbitrary",), vmem_limit_bytes=VMEM_LIMIT),
        name="outproj_lru",
    )(x, o_r, z, z, z, z, xs, o_rs, o_gs, zs, zs, lp, wgate, w_r, w_g, w_o, fg)


def _row_tile(m, want):
    t = min(m, want)
    assert m % t == 0, (m, t)
    return t


def _pack_params_kernel(mu_ref, w0_ref, a0_ref, kk_ref, ka_ref, rk_ref, lng_ref, lnb_ref, wdu_ref,
                        wau_ref, cw_ref, cb_ref, gxb_ref, gab_ref, lam_ref, gxw_ref, gaw_ref,
                        pvec_ref, mul_ref, wd_ref, wa_ref, e_ref, lp_ref, wg_ref):
    pvec_ref[...] = jnp.zeros_like(pvec_ref)
    for i in range(3):
        pvec_ref[_MU_R + i:_MU_R + i + 1, :] = mu_ref[:, RWKV_W * i:RWKV_W * (i + 1)]
    for row, ref in ((_W0, w0_ref), (_A0, a0_ref), (_KK, kk_ref), (_KA, ka_ref), (_RK, rk_ref),
                     (_LNG, lng_ref), (_LNB, lnb_ref)):
        pvec_ref[row:row + 1, :] = ref[...]
    mul_ref[...] = jnp.broadcast_to(mu_ref[:, 3 * RWKV_W:3 * RWKV_W + 2 * LORA], mul_ref.shape)

    zeros = jnp.zeros((LORA, RWKV_W), BF16)
    wd_ref[0:LORA, :] = _bf(wdu_ref[...])
    wd_ref[LORA:2 * LORA, :] = zeros
    wa_ref[0:LORA, :] = zeros
    wa_ref[LORA:2 * LORA, :] = _bf(wau_ref[...])

    ri = lax.broadcasted_iota(jnp.int32, (LANES, LANES), 0)
    ci = lax.broadcasted_iota(jnp.int32, (LANES, LANES), 1)
    e_ref[...] = jnp.where((ri < HEAD) == (ci < HEAD), 1.0, 0.0).astype(BF16)

    lp_ref[_CW0:_CW0 + CONV_W, :] = cw_ref[...]
    for row, ref in ((_CB, cb_ref), (_GXB, gxb_ref), (_GAB, gab_ref), (_LAM, lam_ref)):
        lp_ref[row:row + 1, :] = ref[...]

    blk = LRU_W // LRU_BLOCKS
    z = jnp.zeros((blk, blk), F32)
    for g in range(LRU_BLOCKS // 2):
        top = jnp.concatenate([gxw_ref[2 * g], z, gaw_ref[2 * g], z], axis=1)
        bot = jnp.concatenate([z, gxw_ref[2 * g + 1], z, gaw_ref[2 * g + 1]], axis=1)
        wg_ref[g] = _bf(jnp.concatenate([top, bot], axis=0))


def _pack_params(l, rwkv_mu, w_decay0, w_decay_up, w_iclr0, w_iclr_up, k_k, k_a, r_k, ln_x_g,
                 ln_x_b, conv_w, conv_b, lru_gx_w, lru_gx_b, lru_ga_w, lru_ga_b, lru_lambda):
    blk = LRU_W // LRU_BLOCKS
    assert 2 * blk == LANES and 2 * LORA == LANES
    depth = rwkv_mu.shape[0]
    row = lambda a: pl.BlockSpec((1, a.shape[-1]), lambda i: (l, 0))
    mat = lambda a: pl.BlockSpec((None,) + a.shape[1:], lambda i: (l,) + (0,) * (a.ndim - 1))
    full = lambda shp: pl.BlockSpec(shp, lambda i: (0,) * len(shp))
    rk = r_k.reshape(depth, RWKV_W)
    rows = (rwkv_mu, w_decay0, w_iclr0, k_k, k_a, rk, ln_x_g, ln_x_b)
    out_shapes = ((16, RWKV_W, F32), (SUBLANES, LANES, F32), (LANES, RWKV_W, BF16), (LANES, RWKV_W, BF16),
                  (LANES, LANES, BF16), (SUBLANES, LRU_W, F32))
    outs = tuple(jax.ShapeDtypeStruct(s[:2], s[2]) for s in out_shapes)
    outs += (jax.ShapeDtypeStruct((LRU_BLOCKS // 2, LANES, 2 * LANES), BF16),)
    return pl.pallas_call(
        _pack_params_kernel,
        out_shape=outs,
        grid=(1,),
        in_specs=[row(a) for a in rows] + [mat(w_decay_up), mat(w_iclr_up), mat(conv_w), row(conv_b),
                                          row(lru_gx_b), row(lru_ga_b), row(lru_lambda),
                                          mat(lru_gx_w), mat(lru_ga_w)],
        out_specs=tuple(full(o.shape) for o in outs),
        compiler_params=pltpu.CompilerParams(
            dimension_semantics=("arbitrary",), vmem_limit_bytes=VMEM_LIMIT),
        name="pack_params",
    )(*rows, w_decay_up, w_iclr_up, conv_w, conv_b, lru_gx_b, lru_ga_b, lru_lambda, lru_gx_w, lru_ga_w)


def kernel(x_prompt, x_sample, state_shift, state_wkv, state_conv, state_lru, norm_g, w_in, rwkv_mu,
           w_decay0, w_decay_up, w_iclr0, w_iclr_up, k_k, k_a, r_k, ln_x_g, ln_x_b, w_out_rwkv,
           conv_w, conv_b, lru_gx_w, lru_gx_b, lru_ga_w, lru_ga_b, lru_lambda, w_out_lru, w_out,
           final_norm_g):
    bp, seq, d = x_prompt.shape
    bs = x_sample.shape[0]
    assert x_sample.shape[1] == 1 and seq % WKV_CHUNK == 0
    depth = w_in.shape[0]
    xp = x_prompt.reshape(bp * seq, d)
    xs = x_sample.reshape(bs, d)
    fg = final_norm_g.reshape(1, d)
    outs = [[] for _ in range(8)]
    for l in range(depth):
        pvec, mul, wd, wa, e, lp, wg = _pack_params(
            l, rwkv_mu, w_decay0, w_decay_up, w_iclr0, w_iclr_up, k_k, k_a, r_k, ln_x_g, ln_x_b,
            conv_w, conv_b, lru_gx_w, lru_gx_b, lru_ga_w, lru_ga_b, lru_lambda)
        g = norm_g[l].reshape(1, d)
        rec = (pvec, mul, wd, wa, e)
        w, wl, zs, zls, w_r, w_g, w_o = _inproj_head(xs, g, w_in, w_out_rwkv, w_out_lru, w_out, l, INPROJ_TN)
        zp, zlp = _inproj(xp, g, w, wl, _row_tile(bp * seq, 1024), INPROJ_WIDE_TN)

        s0t = jnp.transpose(state_wkv[l], (1, 2, 3, 0))
        o_rs, s_new, sh_new = _wkv_step(zs, zls, state_shift[l], s0t, *rec)
        conv = state_conv[l].reshape(bs, (CONV_W - 1) * LRU_W)
        o_gs, h_new, conv_new = _lru_step(zs, conv, state_lru[l], lp, wg)
        outs[4].append(sh_new)
        outs[5].append(jnp.transpose(s_new, (3, 0, 1, 2)))
        outs[6].append(conv_new.reshape(bs, CONV_W - 1, LRU_W))
        outs[7].append(h_new)

        zp3 = zp.reshape(bp, seq, -1)
        zlp3 = zlp.reshape(bp, seq, LANES)
        nb = max(n for n in (4, 2, 1) if bp % n == 0)
        o_r, s_new, sh_last = _wkv_chunk(zp3, zlp3, *rec, bp, seq, nb)
        o_r = o_r.reshape(bp * seq, RWKV_W)
        last = l == depth - 1
        xp, h_last, conv_last, xs = _outproj_lru(xp, o_r, zp, xs, o_rs, o_gs, zs, lp, wg, w_r, w_g, w_o,
                                                 fg, _row_tile(seq, 256), seq, last)
        outs[0].append(sh_last.reshape(bp, -1))
        outs[1].append(s_new)
        outs[2].append(conv_last)
        outs[3].append(h_last.reshape(bp, LRU_W))

    return (xp.reshape(bp, seq, d), xs.reshape(bs, 1, d)) + tuple(jnp.stack(o) for o in outs)
```

```python
import functools

import jax
import jax.numpy as jnp
from jax import lax
from jax.experimental import pallas as pl
from jax.experimental.pallas import tpu as pltpu

F32 = jnp.float32
BF16 = jnp.bfloat16

HEADS = 16
HEAD = 64
RWKV_W = HEADS * HEAD
LORA = 64
LRU_W = 1024
LRU_BLOCKS = 16
CONV_W = 4
LRU_C = 8.0
RMS_EPS = 1e-6
GN_EPS = 1e-5 * HEAD
DECAY_SCALE = 0.6065306597126334

LANES = 128
SUBLANES = 8
WKV_CHUNK = 64
VMEM_LIMIT = 60 * 1024 * 1024

NN = (((1,), (0,)), ((), ()))
NT = (((1,), (1,)), ((), ()))
TN = (((0,), (0,)), ((), ()))


def _bf(x):
    return x.astype(BF16)


def _dg(a, b, dn):
    return lax.dot_general(a, b, dn, preferred_element_type=F32)


def _softplus(x):
    return jnp.maximum(x, 0.0) + jnp.log1p(jnp.exp(-jnp.abs(x)))


def _sigmoid(x):
    return 1.0 / (1.0 + jnp.exp(-x))


def _segsum(x, e):
    rows, n = x.shape[0], x.shape[1] // LANES
    stacked = jnp.concatenate([x[:, LANES * j:LANES * (j + 1)] for j in range(n)], axis=0)
    s = _dg(_bf(stacked), e, NN)
    return jnp.concatenate([s[rows * j:rows * (j + 1), :] for j in range(n)], axis=1)


def _rms(x, g):
    return x * lax.rsqrt(jnp.mean(x * x, axis=-1, keepdims=True) + RMS_EPS) * g


SHIFT_MAIN = 3 * RWKV_W
LORA_COL = 10 * RWKV_W
LORA_BLOCK = LORA_COL // LANES
FUSE_PIECES = 8
INPROJ_TN = 1024
INPROJ_WIDE_TN = 2048
NORM_ROWS = 128
OUT_W_STEPS = 8


def _inproj_head_kernel(xs_ref, g_ref, w_ref, wt_ref, wlo_ref, wr_ref, wg_ref, wo_ref, wb_ref, wl_ref,
                        zs_ref, zls_ref, wrb_ref, wgb_ref, wob_ref, hs_ref):
    j = pl.program_id(0)
    tn = w_ref.shape[-1]

    @pl.when(j == 0)
    def _():
        hs_ref[...] = _bf(_rms(xs_ref[...], g_ref[...]))
        wl_ref[...] = _bf(wlo_ref[...])
        zls_ref[...] = _dg(hs_ref[...], wl_ref[...], NN)

    @pl.when(j < SHIFT_MAIN // tn)
    def _():
        wb_ref[...] = _bf(w_ref[...])

    @pl.when(j >= SHIFT_MAIN // tn)
    def _():
        wb_ref[:, 0:tn - 2 * LORA] = _bf(w_ref[:, 2 * LORA:tn])
        wb_ref[:, tn - 2 * LORA:tn] = _bf(wt_ref[...])

    zs_ref[...] = _dg(hs_ref[...], wb_ref[...], NN)

    @pl.when(j < OUT_W_STEPS)
    def _():
        wrb_ref[...] = _bf(wr_ref[...])
        wgb_ref[...] = _bf(wg_ref[...])
        wob_ref[...] = _bf(wo_ref[...])


def _inproj_head(xs, g, w_in, w_out_rwkv, w_out_lru, w_out, layer, tn):
    ms, d = xs.shape
    n = w_in.shape[-1]
    assert n == LORA_COL + 2 * LORA and SHIFT_MAIN % tn == 0 and 2 * LORA == LANES
    nj = LORA_COL // tn
    assert nj >= OUT_W_STEPS
    lanes_per_tile = tn // LANES
    one = lambda shp, imap: pl.BlockSpec(shp, imap, pipeline_mode=pl.Buffered(1))
    rows = lambda a: a.shape[1] // OUT_W_STEPS
    step = lambda j: jnp.minimum(j, OUT_W_STEPS - 1)
    w_outs = (w_out_rwkv, w_out_lru, w_out)
    return pl.pallas_call(
        _inproj_head_kernel,
        out_shape=(jax.ShapeDtypeStruct((d, LORA_COL), BF16), jax.ShapeDtypeStruct((d, LANES), BF16),
                   jax.ShapeDtypeStruct((ms, LORA_COL), F32), jax.ShapeDtypeStruct((ms, LANES), F32))
        + tuple(jax.ShapeDtypeStruct(a.shape[1:], BF16) for a in w_outs),
        grid=(nj,),
        in_specs=[
            one((ms, d), lambda j: (0, 0)),
            one((1, d), lambda j: (0, 0)),
            pl.BlockSpec((None, d, tn), lambda j: (layer, 0, j)),
            pl.BlockSpec((None, d, LANES), lambda j: (layer, 0, (j + 1) * lanes_per_tile)),
            one((None, d, LANES), lambda j: (layer, 0, SHIFT_MAIN // LANES)),
        ] + [pl.BlockSpec((None, rows(a), a.shape[2]), lambda j: (layer, step(j), 0)) for a in w_outs],
        out_specs=(
            pl.BlockSpec((d, tn), lambda j: (0, j)),
            pl.BlockSpec((d, LANES), lambda j: (0, 0)),
            pl.BlockSpec((ms, tn), lambda j: (0, j)),
            pl.BlockSpec((ms, LANES), lambda j: (0, 0)),
        ) + tuple(pl.BlockSpec((rows(a), a.shape[2]), lambda j: (step(j), 0)) for a in w_outs),
        scratch_shapes=[pltpu.VMEM((ms, d), BF16)],
        compiler_params=pltpu.CompilerParams(
            dimension_semantics=("arbitrary",), vmem_limit_bytes=VMEM_LIMIT),
        name="inproj_head",
    )(xs, g, w_in, w_in, w_in, *w_outs)


def _inproj_kernel(x_ref, g_ref, w_ref, wl_ref, z_ref, zl_ref, h_ref):
    @pl.when(pl.program_id(1) == 0)
    def _():
        for r in range(0, x_ref.shape[0], NORM_ROWS):
            rows = slice(r, r + NORM_ROWS)
            h_ref[rows, :] = _bf(_rms(x_ref[rows, :], g_ref[...]))
        zl_ref[...] = _dg(h_ref[...], wl_ref[...], NN)

    z_ref[...] = _dg(h_ref[...], w_ref[...], NN)


def _inproj(x, g, w, wl, tm, tn):
    m, d = x.shape
    return pl.pallas_call(
        _inproj_kernel,
        out_shape=(jax.ShapeDtypeStruct((m, LORA_COL), F32), jax.ShapeDtypeStruct((m, LANES), F32)),
        grid=(m // tm, LORA_COL // tn),
        in_specs=[
            pl.BlockSpec((tm, d), lambda i, j: (i, 0)),
            pl.BlockSpec((1, d), lambda i, j: (0, 0), pipeline_mode=pl.Buffered(1)),
            pl.BlockSpec((d, tn), lambda i, j: (0, j)),
            pl.BlockSpec((d, LANES), lambda i, j: (0, 0), pipeline_mode=pl.Buffered(1)),
        ],
        out_specs=(
            pl.BlockSpec((tm, tn), lambda i, j: (i, j)),
            pl.BlockSpec((tm, LANES), lambda i, j: (i, 0)),
        ),
        scratch_shapes=[pltpu.VMEM((tm, d), BF16)],
        compiler_params=pltpu.CompilerParams(
            dimension_semantics=("arbitrary", "arbitrary"), vmem_limit_bytes=VMEM_LIMIT),
        name="inproj",
    )(x, g, w, wl)


_MU_R, _MU_K, _MU_V, _W0, _A0, _KK, _KA, _RK, _LNG, _LNB = range(10)


def _prow(pv_ref, i):
    return pv_ref[i:i + 1, :]


def _wkv_prep(zr, zk, zv, zl, pr, pk, pv, pl_, pv_ref, mul_ref, wd_ref, wa_ref, e):
    r = zr + _prow(pv_ref, _MU_R) * (pr - zr)
    k = zk + _prow(pv_ref, _MU_K) * (pk - zk)
    v = zv + _prow(pv_ref, _MU_V) * (pv - zv)
    lo = zl + mul_ref[0:1, :] * (pl_ - zl)
    lw = _dg(_bf(jnp.tanh(lo)), wd_ref[...], NN)
    la = _dg(_bf(lo), wa_ref[...], NN)
    logd = -DECAY_SCALE * _sigmoid(_prow(pv_ref, _W0) + lw)
    a = _sigmoid(_prow(pv_ref, _A0) + la)
    kk = k * _prow(pv_ref, _KK)
    kk = kk * lax.rsqrt(jnp.maximum(_segsum(kk * kk, e), 1e-24))
    k2 = k * (1.0 + (a - 1.0) * _prow(pv_ref, _KA))
    return r, k2, v, -kk, kk * a, logd


def _wkv_bonus_gate(r, k2, v, zrg, pv_ref, e):
    return _segsum(r * k2 * _prow(pv_ref, _RK), e) * v, zrg * _sigmoid(zrg)


def _wkv_norm_gate(y, bonus_v, gate, pv_ref, e):
    mu = _segsum(y, e) * (1.0 / HEAD)
    yc = y - mu
    var = _segsum(yc * yc, e) * (1.0 / HEAD)
    yn = yc * lax.rsqrt(var + GN_EPS) * _prow(pv_ref, _LNG) + _prow(pv_ref, _LNB)
    return _bf((yn + bonus_v) * gate)


def _wkv_post(y, r, k2, v, zrg, pv_ref, e):
    bonus_v, gate = _wkv_bonus_gate(r, k2, v, zrg, pv_ref, e)
    return _wkv_norm_gate(y, bonus_v, gate, pv_ref, e)


def _wkv_chunk_kernel(zr_ref, zk_ref, zv_ref, zrg_ref, zl_ref, pv_ref, mul_ref, wd_ref, wa_ref,
                      e_ref, o_ref, sout_ref, nsh_ref, s_s, prev_s, prevl_s):
    c = pl.program_id(1)
    nc = pl.num_programs(1)
    C = WKV_CHUNK
    assert C == HEAD and 2 * HEAD == LANES
    nb = zr_ref.shape[0]
    rows_all = nb * C
    seqs = range(nb)

    @pl.when(c == 0)
    def _():
        s_s[...] = jnp.zeros_like(s_s)
        prev_s[...] = jnp.zeros_like(prev_s)
        prevl_s[...] = jnp.zeros_like(prevl_s)

    first = lax.broadcasted_iota(jnp.int32, (SUBLANES, 1), 0) == 0

    def shifted(z, prev_ref, lanes):
        rolled = pltpu.roll(z, 1, 0)
        pieces = []
        for b in seqs:
            head = jnp.where(first, prev_ref[b, 0:1, lanes], rolled[b * C:b * C + SUBLANES, :])
            pieces += [head, rolled[b * C + SUBLANES:(b + 1) * C, :]]
        return jnp.concatenate(pieces, axis=0)

    def flat(ref):
        return ref[...].reshape(rows_all, ref.shape[-1])

    zr, zk, zv, zl = flat(zr_ref), flat(zk_ref), flat(zv_ref), flat(zl_ref)
    seg = [slice(RWKV_W * i, RWKV_W * (i + 1)) for i in range(3)]
    pr = shifted(zr, prev_s, seg[0])
    pk = shifted(zk, prev_s, seg[1])
    pv = shifted(zv, prev_s, seg[2])
    pl_ = shifted(zl, prevl_s, slice(0, LANES))
    for b in seqs:
        last = slice(b * C + C - 1, b * C + C)
        prev_s[b, 0:1, seg[0]] = zr[last, :]
        prev_s[b, 0:1, seg[1]] = zk[last, :]
        prev_s[b, 0:1, seg[2]] = zv[last, :]
        prevl_s[b, 0:1, :] = zl[last, :]

    e = e_ref[...]
    r, k2, v, av, bv, logd = _wkv_prep(zr, zk, zv, zl, pr, pk, pv, pl_, pv_ref, mul_ref,
                                       wd_ref, wa_ref, e)

    ti = lax.broadcasted_iota(jnp.int32, (rows_all, rows_all), 0)
    tj = lax.broadcasted_iota(jnp.int32, (rows_all, rows_all), 1)
    tri = jnp.where((ti >= tj) & ((ti & -C) == (tj & -C)), 1.0, 0.0).astype(BF16)
    d_hi = _bf(logd)
    d_r1 = logd - d_hi.astype(F32)
    d_mid = _bf(d_r1)
    d_lo = _bf(d_r1 - d_mid.astype(F32))
    cum = _dg(jnp.concatenate([tri, tri, tri], axis=1), jnp.concatenate([d_hi, d_mid, d_lo], axis=0), NN)
    e_in = jnp.exp(cum)
    e_neg = jnp.exp(-cum)
    a_t = av * jnp.exp(cum - logd)
    r_t = r * e_in
    k_t = k2 * e_neg
    b_t = bv * e_neg
    p_c = [jnp.exp(cum[b * C + C - 1:b * C + C, :]) for b in seqs]

    lane = lax.broadcasted_iota(jnp.int32, (C, LANES), 1)
    trow = lax.broadcasted_iota(jnp.int32, (C, LANES), 0)
    lo = lane < HEAD
    s_in = lane & (HEAD - 1)
    strict = s_in < trow
    incl2 = ((lax.broadcasted_iota(jnp.int32, (C, 2 * LANES), 1) & (HEAD - 1))
             <= lax.broadcasted_iota(jnp.int32, (C, 2 * LANES), 0))
    eye2 = jnp.where(s_in == trow, 1.0, 0.0).astype(F32)
    vrow = lax.broadcasted_iota(jnp.int32, (2 * HEAD, LANES), 0)
    klane = lax.broadcasted_iota(jnp.int32, (2 * HEAD, LANES), 1)
    same_head = (vrow < HEAD) == (klane < HEAD)

    def bd(x):
        z = jnp.zeros_like(x)
        return jnp.concatenate([jnp.where(lo, x, z), jnp.where(lo, z, x)], axis=0)

    npair = HEADS // 2
    units = [(b, p) for b in seqs for p in range(npair)]
    un = range(len(units))
    blk = lambda arr, i: arr[units[i][0] * C:(units[i][0] + 1) * C, LANES * units[i][1]:LANES * (units[i][1] + 1)]
    ar = [_bf(jnp.concatenate([blk(a_t, i), blk(r_t, i)], axis=0)) for i in un]
    bk = [_bf(jnp.concatenate([bd(blk(b_t, i)), bd(blk(k_t, i))], axis=0)) for i in un]
    g = [_dg(ar[i], bk[i], NT) for i in un]
    s0 = [s_s[i] for i in un]
    ars = [_dg(ar[i], _bf(s0[i]), NT) for i in un]
    vbd = [_bf(bd(blk(v, i))) for i in un]
    x = [jnp.where(strict, g[i][0:C, 0:LANES], 0.0) for i in un]
    ak = [jnp.where(strict, g[i][0:C, LANES:2 * LANES], 0.0) for i in un]
    w = [ars[i][0:C, :] + _dg(_bf(ak[i]), vbd[i], NN) for i in un]
    t = [eye2 + x[i] for i in un]
    x = [_dg(_bf(x[i]), _bf(bd(x[i])), NN) for i in un]
    for _ in range(C.bit_length() - 3):
        xt = [_dg(_bf(jnp.concatenate([x[i], t[i]], axis=0)), _bf(bd(x[i])), NN) for i in un]
        x = [xt[i][0:C, :] for i in un]
        t = [t[i] + xt[i][C:2 * C, :] for i in un]
    t = [t[i] + _dg(_bf(t[i]), _bf(bd(x[i])), NN) for i in un]
    u = [_dg(_bf(t[i]), _bf(bd(w[i])), NN) for i in un]
    rbk = [_bf(jnp.where(incl2, g[i][C:2 * C, :], 0.0)) for i in un]
    uvbd = [jnp.concatenate([_bf(bd(u[i])), vbd[i]], axis=0) for i in un]
    y = [ars[i][C:2 * C, :] + _dg(rbk[i], uvbd[i], NN) for i in un]
    uv = [_bf(jnp.concatenate([u[i], blk(v, i)], axis=0)) for i in un]
    pc = [p_c[units[i][0]][:, LANES * units[i][1]:LANES * (units[i][1] + 1)] for i in un]
    bkh = [_bf(jnp.concatenate([blk(b_t, i), blk(k_t, i)], axis=0) * pc[i]) for i in un]
    s1 = [s0[i] * pc[i] + jnp.where(same_head, _dg(uv[i], bkh[i], TN), 0.0) for i in un]
    for i in un:
        s_s[i] = s1[i]

    y_all = jnp.concatenate(
        [jnp.concatenate(y[b * npair:(b + 1) * npair], axis=1) for b in seqs], axis=0)
    o = _wkv_post(y_all, r, k2, v, flat(zrg_ref), pv_ref, e)
    o_ref[...] = o.reshape(nb, C, RWKV_W)

    @pl.when(c == nc - 1)
    def _():
        for i in un:
            b, p = units[i]
            sout_ref[b, 2 * p] = s1[i][0:HEAD, 0:HEAD]
            sout_ref[b, 2 * p + 1] = s1[i][HEAD:2 * HEAD, HEAD:2 * HEAD]
        for b in seqs:
            for q, ref in enumerate((zr_ref, zk_ref, zv_ref)):
                nsh_ref[0, b:b + 1, RWKV_W * q:RWKV_W * (q + 1)] = ref[b, C - 1:C, :]
            nsh_ref[0, b:b + 1, SHIFT_MAIN:SHIFT_MAIN + LANES] = zl_ref[b, C - 1:C, :]


def _wkv_chunk(z, zl, pvec, mul, wd, wa, e, batch, seq, nb):
    C = WKV_CHUNK
    nc = seq // C
    full = lambda shp: pl.BlockSpec(shp, lambda b, c: (0,) * len(shp))
    col = lambda j: pl.BlockSpec((nb, C, RWKV_W), lambda b, c, j=j: (b, c, j))
    return pl.pallas_call(
        _wkv_chunk_kernel,
        out_shape=(jax.ShapeDtypeStruct((batch, seq, RWKV_W), BF16),
                   jax.ShapeDtypeStruct((batch, HEADS, HEAD, HEAD), F32),
                   jax.ShapeDtypeStruct((batch // nb, nb, SHIFT_MAIN + LANES), F32)),
        grid=(batch // nb, nc),
        in_specs=[col(0), col(1), col(2), col(3),
                  pl.BlockSpec((nb, C, LANES), lambda b, c: (b, c, 0)),
                  full(pvec.shape), full(mul.shape), full(wd.shape), full(wa.shape), full(e.shape)],
        out_specs=(pl.BlockSpec((nb, C, RWKV_W), lambda b, c: (b, c, 0)),
                   pl.BlockSpec((nb, HEADS, HEAD, HEAD), lambda b, c: (b, 0, 0, 0)),
                   pl.BlockSpec((1, nb, SHIFT_MAIN + LANES), lambda b, c: (b, 0, 0))),
        scratch_shapes=[pltpu.VMEM((nb * HEADS // 2, 2 * HEAD, 2 * HEAD), F32),
                        pltpu.VMEM((nb, SUBLANES, 3 * RWKV_W), F32),
                        pltpu.VMEM((nb, SUBLANES, LANES), F32)],
        compiler_params=pltpu.CompilerParams(
            dimension_semantics=("arbitrary", "arbitrary"), vmem_limit_bytes=VMEM_LIMIT),
        name="wkv_chunk",
    )(z, z, z, z, zl, pvec, mul, wd, wa, e)


def _wkv_step_kernel(zr_ref, zk_ref, zv_ref, zrg_ref, zl_ref, sh_ref, s0_ref, pv_ref,
                     mul_ref, wd_ref, wa_ref, e_ref, o_ref, sout_ref, nsh_ref,
                     at_s, drt_s, bt_s, kt_s, dt_s, vt_s, brt_s, krt_s, yt_s, keep_s):
    h = pl.program_id(0)
    nh = pl.num_programs(0)
    nseq = zr_ref.shape[0]

    @pl.when(h == 0)
    def _():
        e = e_ref[...]
        r, k2, v, av, bv, logd = _wkv_prep(
            zr_ref[...], zk_ref[...], zv_ref[...], zl_ref[...],
            sh_ref[:, 0:RWKV_W], sh_ref[:, RWKV_W:2 * RWKV_W], sh_ref[:, 2 * RWKV_W:3 * RWKV_W],
            sh_ref[:, SHIFT_MAIN:SHIFT_MAIN + LANES], pv_ref, mul_ref, wd_ref, wa_ref, e)
        nsh_ref[:, 0:RWKV_W] = zr_ref[...]
        nsh_ref[:, RWKV_W:2 * RWKV_W] = zk_ref[...]
        nsh_ref[:, 2 * RWKV_W:SHIFT_MAIN] = zv_ref[...]
        nsh_ref[:, SHIFT_MAIN:SHIFT_MAIN + LANES] = zl_ref[...]
        d = jnp.exp(logd)
        at_s[...] = av.T
        drt_s[...] = (d * r).T
        bt_s[...] = bv.T
        kt_s[...] = k2.T
        dt_s[...] = d.T
        vt_s[...] = v.T
        brt_s[...] = jnp.sum((bv * r).T.reshape(HEADS, HEAD, nseq), axis=1)
        krt_s[...] = jnp.sum((k2 * r).T.reshape(HEADS, HEAD, nseq), axis=1)
        keep_s[0] = r
        keep_s[1] = k2
        keep_s[2] = v

    base = pl.multiple_of(h * HEAD, HEAD)
    rows = pl.ds(base, HEAD)
    a_h, dr_h, b_h, k_h, d_h = at_s[rows, :], drt_s[rows, :], bt_s[rows, :], kt_s[rows, :], dt_s[rows, :]
    br_h = brt_s[pl.ds(h, 1), :]
    kr_h = krt_s[pl.ds(h, 1), :]

    def value_rows(g, carry):
        off = pl.multiple_of(base + g * SUBLANES, SUBLANES)
        v8 = vt_s[pl.ds(off, SUBLANES), :]
        ys = []
        for j in range(SUBLANES):
            vi = g * SUBLANES + j
            s_v = s0_ref[0, vi]
            sa = jnp.sum(s_v * a_h, axis=0, keepdims=True)
            y0 = jnp.sum(s_v * dr_h, axis=0, keepdims=True)
            v_v = v8[j:j + 1, :]
            sout_ref[0, vi] = s_v * d_h + sa * b_h + v_v * k_h
            ys.append(y0 + sa * br_h + v_v * kr_h)
        yt_s[pl.ds(off, SUBLANES), :] = jnp.concatenate(ys, axis=0)
        return carry

    lax.fori_loop(0, HEAD // SUBLANES, value_rows, 0)

    @pl.when(h == nh - 1)
    def _():
        o_ref[...] = _wkv_post(yt_s[...].T, keep_s[0], keep_s[1], keep_s[2], zrg_ref[...], pv_ref,
                               e_ref[...])


def _wkv_step(z, zl, sh, s0t, pvec, mul, wd, wa, e):
    nseq = z.shape[0]
    full = lambda shp: pl.BlockSpec(shp, lambda i: (0,) * len(shp))
    col = lambda j: pl.BlockSpec((nseq, RWKV_W), lambda i, j=j: (0, j))
    st_block = (1, HEAD, HEAD, nseq)
    wide = pltpu.VMEM((RWKV_W, nseq), F32)
    return pl.pallas_call(
        _wkv_step_kernel,
        out_shape=(jax.ShapeDtypeStruct((nseq, RWKV_W), BF16),
                   jax.ShapeDtypeStruct(s0t.shape, F32),
                   jax.ShapeDtypeStruct(sh.shape, F32)),
        grid=(HEADS,),
        in_specs=[col(0), col(1), col(2), col(3),
                  full(zl.shape), full(sh.shape),
                  pl.BlockSpec(st_block, lambda i: (i, 0, 0, 0)),
                  full(pvec.shape), full(mul.shape), full(wd.shape), full(wa.shape), full(e.shape)],
        out_specs=(full((nseq, RWKV_W)),
                   pl.BlockSpec(st_block, lambda i: (i, 0, 0, 0)),
                   full(sh.shape)),
        scratch_shapes=[wide] * 6 + [pltpu.VMEM((HEADS, nseq), F32)] * 2
                       + [wide, pltpu.VMEM((3, nseq, RWKV_W), F32)],
        compiler_params=pltpu.CompilerParams(
            dimension_semantics=("arbitrary",), vmem_limit_bytes=VMEM_LIMIT),
        name="wkv_step",
    )(z, z, z, z, zl, sh, s0t, pvec, mul, wd, wa, e)


_CW0, _CW1, _CW2, _CW3, _CB, _GXB, _GAB, _LAM = range(8)


def _lru_gates(xc, lp_ref, wg_ref):
    xb = _bf(xc)
    ngroups = wg_ref.shape[0]
    gs = [_dg(xb[:, LANES * g:LANES * (g + 1)], wg_ref[g], NN) for g in range(ngroups)]
    gx_pre = jnp.concatenate([gs[g][:, 0:LANES] for g in range(ngroups)], axis=1)
    ga_pre = jnp.concatenate([gs[g][:, LANES:2 * LANES] for g in range(ngroups)], axis=1)
    gx = _sigmoid(gx_pre + _prow(lp_ref, _GXB))
    ga = _sigmoid(ga_pre + _prow(lp_ref, _GAB))
    log_a = -LRU_C * ga * _softplus(-_prow(lp_ref, _LAM))
    a = jnp.exp(log_a)
    mult = jnp.sqrt((1.0 - a) * (1.0 + a))
    return a, mult * gx * xc


def _lru_scan_rows(a, b, zg, hc):
    row8 = lax.broadcasted_iota(jnp.int32, (SUBLANES, 1), 0)
    hs = []
    for i in range(a.shape[0] // SUBLANES):
        a8 = a[SUBLANES * i:SUBLANES * (i + 1), :]
        b8 = b[SUBLANES * i:SUBLANES * (i + 1), :]
        for s in (1, 2, 4):
            keep = row8 >= s
            b8 = jnp.where(keep, a8 * pltpu.roll(b8, s, 0) + b8, b8)
            a8 = jnp.where(keep, a8 * pltpu.roll(a8, s, 0), a8)
        hb = b8 + a8 * hc
        hs.append(hb)
        hc = jnp.broadcast_to(hb[SUBLANES - 1:SUBLANES, :], hb.shape)
    return _bf(jnp.concatenate(hs, axis=0) * (zg * _sigmoid(zg))), hc


def _lru_step_kernel(zx_ref, zg_ref, conv_ref, h0_ref, lp_ref, wg_ref, o_ref, hnew_ref, cnew_ref):
    zx = zx_ref[...]
    for j in range(CONV_W - 2):
        cnew_ref[j] = conv_ref[j + 1]
    cnew_ref[CONV_W - 2] = zx
    xc = _prow(lp_ref, _CW3) * zx + _prow(lp_ref, _CB)
    for j in range(CONV_W - 1):
        xc = xc + _prow(lp_ref, j) * conv_ref[j]
    a, b = _lru_gates(xc, lp_ref, wg_ref)
    h = a * h0_ref[...] + b
    hnew_ref[...] = h
    zg = zg_ref[...]
    o_ref[...] = _bf(h * (zg * _sigmoid(zg)))


def _lru_step(z_main, conv, h0, lp, wg):
    nb = z_main.shape[0]
    full = lambda shp: pl.BlockSpec(shp, lambda i: (0,) * len(shp))
    col = lambda j: pl.BlockSpec((nb, LRU_W), lambda i, j=j: (0, j))
    return pl.pallas_call(
        _lru_step_kernel,
        out_shape=(jax.ShapeDtypeStruct((nb, LRU_W), BF16), jax.ShapeDtypeStruct((nb, LRU_W), F32),
                   jax.ShapeDtypeStruct(conv.shape, F32)),
        grid=(1,),
        in_specs=[col(4), col(5), full(conv.shape), full(h0.shape), full(lp.shape), full(wg.shape)],
        out_specs=(full((nb, LRU_W)), full((nb, LRU_W)), full(conv.shape)),
        compiler_params=pltpu.CompilerParams(
            dimension_semantics=("arbitrary",), vmem_limit_bytes=VMEM_LIMIT),
        name="lru_step",
    )(z_main, z_main, conv, h0, lp, wg)


def _project(x, o_r, o_g, m_r, m_g, wr_ref, wg_ref, wo_ref, fg_ref, final):
    y_r = _dg(o_r, wr_ref[...], NN)
    y_g = _dg(o_g, wg_ref[...], NN)
    merged = _sigmoid(m_r) * y_r + _sigmoid(m_g) * y_g
    out = x + _dg(_bf(merged), wo_ref[...], NN)
    return _rms(out, fg_ref[...]) if final else out


def _outproj_lru_kernel(x_ref, or_ref, mr_ref, mg_ref, zx_ref, zg_ref, xs_ref, ors_ref, ogs_ref,
                        mrs_ref, mgs_ref, lp_ref, wgate_ref, wr_ref, wg_ref, wo_ref, fg_ref,
                        out_ref, hlast_ref, cnew_ref, outs_ref, og_s, mg_s, xb_s, hc_s, *, final,
                        tiles_per_seq):
    i = pl.program_id(0)
    n = pl.num_programs(0) - 2
    tm, d = x_ref.shape

    @pl.when(i == 0)
    def _():
        og_s[...] = jnp.zeros_like(og_s)
        mg_s[...] = jnp.zeros_like(mg_s)
        xb_s[...] = jnp.zeros_like(xb_s)
        hc_s[...] = jnp.zeros_like(hc_s)
        outs_ref[...] = _project(xs_ref[...], ors_ref[...], ogs_ref[...], mrs_ref[...], mgs_ref[...],
                                 wr_ref, wg_ref, wo_ref, fg_ref, final)

    og_prev = og_s[...]
    mg_prev = mg_s[...]
    t = lax.rem(jnp.minimum(i, n - 1), tiles_per_seq)
    first = t == 0

    xb_s[0:SUBLANES, :] = jnp.where(first, 0.0, xb_s[0:SUBLANES, :])
    xb_s[SUBLANES:SUBLANES + tm, :] = zx_ref[...]

    outs, a_parts, b_parts = [], [], []
    for c in range(FUSE_PIECES):
        cs = slice(c * d // FUSE_PIECES, (c + 1) * d // FUSE_PIECES)
        outs.append(x_ref[:, cs] + _dg(mg_prev, wo_ref[:, cs], NN))
        r0, r1 = SUBLANES + c * tm // FUSE_PIECES, SUBLANES + (c + 1) * tm // FUSE_PIECES
        xc = _prow(lp_ref, _CW3) * xb_s[r0:r1, :] + _prow(lp_ref, _CB)
        for j in range(1, CONV_W):
            xc = xc + _prow(lp_ref, CONV_W - 1 - j) * xb_s[r0 - j:r1 - j, :]
        a_c, b_c = _lru_gates(xc, lp_ref, wgate_ref)
        a_parts.append(a_c)
        b_parts.append(b_c)
    out = jnp.concatenate(outs, axis=1)
    out_ref[...] = _rms(out, fg_ref[...]) if final else out
    xb_s[0:SUBLANES, :] = xb_s[tm:tm + SUBLANES, :]

    o_g, hc = _lru_scan_rows(jnp.concatenate(a_parts, axis=0), jnp.concatenate(b_parts, axis=0),
                             zg_ref[...], jnp.where(first, 0.0, hc_s[...]))
    hc_s[...] = hc
    og_s[...] = o_g

    y_r = _dg(or_ref[...], wr_ref[...], NN)
    y_g = _dg(og_prev, wg_ref[...], NN)
    mg_s[...] = _bf(_sigmoid(mr_ref[...]) * y_r + _sigmoid(mg_ref[...]) * y_g)

    @pl.when((t == tiles_per_seq - 1) & (i < n))
    def _():
        seq = i // tiles_per_seq
        hlast_ref[pl.ds(seq, 1), :] = hc[0:1, :]
        for j in range(CONV_W - 1):
            row = tm - (CONV_W - 1) + j
            cnew_ref[j, pl.ds(seq, 1), :] = zx_ref[row:row + 1, :]


def _outproj_lru(x, o_r, z, xs, o_rs, o_gs, zs, lp, wgate, w_r, w_g, w_o, fg, tm, seq, final):
    m, d = x.shape
    ms = xs.shape[0]
    n = m // tm
    tiles_per_seq = seq // tm
    const = lambda shp: pl.BlockSpec(shp, lambda i: (0,) * len(shp), pipeline_mode=pl.Buffered(1))
    back = lambda i, k: jnp.clip(i - k, 0, n - 1)
    return pl.pallas_call(
        functools.partial(_outproj_lru_kernel, final=final, tiles_per_seq=tiles_per_seq),
        out_shape=(jax.ShapeDtypeStruct((m, d), F32),
                   jax.ShapeDtypeStruct((m // seq, LRU_W), F32),
                   jax.ShapeDtypeStruct((CONV_W - 1, m // seq, LRU_W), F32),
                   jax.ShapeDtypeStruct((ms, d), F32)),
        grid=(n + 2,),
        in_specs=[
            pl.BlockSpec((tm, d), lambda i: (back(i, 2), 0)),
            pl.BlockSpec((tm, RWKV_W), lambda i: (back(i, 1), 0)),
            pl.BlockSpec((tm, d), lambda i: (back(i, 1), 3)),
            pl.BlockSpec((tm, d), lambda i: (back(i, 1), 4)),
            pl.BlockSpec((tm, LRU_W), lambda i: (back(i, 0), 4)),
            pl.BlockSpec((tm, LRU_W), lambda i: (back(i, 0), 5)),
            const(xs.shape), const(o_rs.shape), const(o_gs.shape),
            pl.BlockSpec((ms, d), lambda i: (0, 3), pipeline_mode=pl.Buffered(1)),
            pl.BlockSpec((ms, d), lambda i: (0, 4), pipeline_mode=pl.Buffered(1)),
            const(lp.shape), const(wgate.shape), const(w_r.shape), const(w_g.shape), const(w_o.shape),
            const(fg.shape),
        ],
        out_specs=(pl.BlockSpec((tm, d), lambda i: (back(i, 2), 0)),
                   pl.BlockSpec((m // seq, LRU_W), lambda i: (0, 0)),
                   pl.BlockSpec((CONV_W - 1, m // seq, LRU_W), lambda i: (0, 0, 0)),
                   pl.BlockSpec((ms, d), lambda i: (0, 0))),
        scratch_shapes=[pltpu.VMEM((tm, LRU_W), BF16),
                        pltpu.VMEM((tm, d), BF16),
                        pltpu.VMEM((SUBLANES + tm, LRU_W), F32),
                        pltpu.VMEM((SUBLANES, LRU_W), F32)],
        compiler_params=pltpu.CompilerParams(
            dimension_semantics=("arbitrary",), vmem_limit_bytes=VMEM_LIMIT),
        name="outproj_lru",
    )(x, o_r, z, z, z, z, xs, o_rs, o_gs, zs, zs, lp, wgate, w_r, w_g, w_o, fg)


def _row_tile(m, want):
    t = min(m, want)
    assert m % t == 0, (m, t)
    return t


def _pack_params_kernel(mu_ref, w0_ref, a0_ref, kk_ref, ka_ref, lng_ref, lnb_ref, rk_ref, wdu_ref,
                        wau_ref, cw_ref, cb_ref, gxb_ref, gab_ref, lam_ref, gxw_ref, gaw_ref,
                        pvec_ref, mul_ref, wd_ref, wa_ref, e_ref, lp_ref, wg_ref):
    pvec_ref[...] = jnp.zeros_like(pvec_ref)
    for i in range(3):
        pvec_ref[_MU_R + i:_MU_R + i + 1, :] = mu_ref[:, RWKV_W * i:RWKV_W * (i + 1)]
    for row, ref in ((_W0, w0_ref), (_A0, a0_ref), (_KK, kk_ref), (_KA, ka_ref),
                     (_LNG, lng_ref), (_LNB, lnb_ref)):
        pvec_ref[row:row + 1, :] = ref[...]
    for h in range(HEADS):
        pvec_ref[_RK:_RK + 1, HEAD * h:HEAD * (h + 1)] = rk_ref[h:h + 1, :]
    mul_ref[...] = jnp.broadcast_to(mu_ref[:, 3 * RWKV_W:3 * RWKV_W + 2 * LORA], mul_ref.shape)

    zeros = jnp.zeros((LORA, RWKV_W), BF16)
    wd_ref[0:LORA, :] = _bf(wdu_ref[...])
    wd_ref[LORA:2 * LORA, :] = zeros
    wa_ref[0:LORA, :] = zeros
    wa_ref[LORA:2 * LORA, :] = _bf(wau_ref[...])

    ri = lax.broadcasted_iota(jnp.int32, (LANES, LANES), 0)
    ci = lax.broadcasted_iota(jnp.int32, (LANES, LANES), 1)
    e_ref[...] = jnp.where((ri < HEAD) == (ci < HEAD), 1.0, 0.0).astype(BF16)

    lp_ref[_CW0:_CW0 + CONV_W, :] = cw_ref[...]
    for row, ref in ((_CB, cb_ref), (_GXB, gxb_ref), (_GAB, gab_ref), (_LAM, lam_ref)):
        lp_ref[row:row + 1, :] = ref[...]

    blk = LRU_W // LRU_BLOCKS
    z = jnp.zeros((blk, blk), F32)
    for g in range(LRU_BLOCKS // 2):
        top = jnp.concatenate([gxw_ref[2 * g], z, gaw_ref[2 * g], z], axis=1)
        bot = jnp.concatenate([z, gxw_ref[2 * g + 1], z, gaw_ref[2 * g + 1]], axis=1)
        wg_ref[g] = _bf(jnp.concatenate([top, bot], axis=0))


def _pack_params(l, rwkv_mu, w_decay0, w_decay_up, w_iclr0, w_iclr_up, k_k, k_a, r_k, ln_x_g,
                 ln_x_b, conv_w, conv_b, lru_gx_w, lru_gx_b, lru_ga_w, lru_ga_b, lru_lambda):
    blk = LRU_W // LRU_BLOCKS
    assert 2 * blk == LANES and 2 * LORA == LANES
    row = lambda a: pl.BlockSpec((1, a.shape[-1]), lambda i: (l, 0))
    mat = lambda a: pl.BlockSpec((None,) + a.shape[1:], lambda i: (l,) + (0,) * (a.ndim - 1))
    full = lambda shp: pl.BlockSpec(shp, lambda i: (0,) * len(shp))
    rows = (rwkv_mu, w_decay0, w_iclr0, k_k, k_a, ln_x_g, ln_x_b)
    out_shapes = ((16, RWKV_W, F32), (SUBLANES, LANES, F32), (LANES, RWKV_W, BF16), (LANES, RWKV_W, BF16),
                  (LANES, LANES, BF16), (SUBLANES, LRU_W, F32))
    outs = tuple(jax.ShapeDtypeStruct(s[:2], s[2]) for s in out_shapes)
    outs += (jax.ShapeDtypeStruct((LRU_BLOCKS // 2, LANES, 2 * LANES), BF16),)
    return pl.pallas_call(
        _pack_params_kernel,
        out_shape=outs,
        grid=(1,),
        in_specs=[row(a) for a in rows] + [mat(r_k), mat(w_decay_up), mat(w_iclr_up), mat(conv_w),
                                          row(conv_b), row(lru_gx_b), row(lru_ga_b), row(lru_lambda),
                                          mat(lru_gx_w), mat(lru_ga_w)],
        out_specs=tuple(full(o.shape) for o in outs),
        compiler_params=pltpu.CompilerParams(
            dimension_semantics=("arbitrary",), vmem_limit_bytes=VMEM_LIMIT),
        name="pack_params",
    )(*rows, r_k, w_decay_up, w_iclr_up, conv_w, conv_b, lru_gx_b, lru_ga_b, lru_lambda, lru_gx_w,
      lru_ga_w)


def kernel(x_prompt, x_sample, state_shift, state_wkv, state_conv, state_lru, norm_g, w_in, rwkv_mu,
           w_decay0, w_decay_up, w_iclr0, w_iclr_up, k_k, k_a, r_k, ln_x_g, ln_x_b, w_out_rwkv,
           conv_w, conv_b, lru_gx_w, lru_gx_b, lru_ga_w, lru_ga_b, lru_lambda, w_out_lru, w_out,
           final_norm_g):
    bp, seq, d = x_prompt.shape
    bs = x_sample.shape[0]
    assert x_sample.shape[1] == 1 and seq % WKV_CHUNK == 0
    depth = w_in.shape[0]
    xp = x_prompt.reshape(bp * seq, d)
    xs = x_sample.reshape(bs, d)
    fg = final_norm_g.reshape(1, d)
    outs = [[] for _ in range(8)]
    for l in range(depth):
        pvec, mul, wd, wa, e, lp, wg = _pack_params(
            l, rwkv_mu, w_decay0, w_decay_up, w_iclr0, w_iclr_up, k_k, k_a, r_k, ln_x_g, ln_x_b,
            conv_w, conv_b, lru_gx_w, lru_gx_b, lru_ga_w, lru_ga_b, lru_lambda)
        g = norm_g[l].reshape(1, d)
        rec = (pvec, mul, wd, wa, e)
        w, wl, zs, zls, w_r, w_g, w_o = _inproj_head(xs, g, w_in, w_out_rwkv, w_out_lru, w_out, l, INPROJ_TN)
        zp, zlp = _inproj(xp, g, w, wl, _row_tile(bp * seq, 1024), INPROJ_WIDE_TN)

        s0t = jnp.transpose(state_wkv[l], (1, 2, 3, 0))
        o_rs, s_new, sh_new = _wkv_step(zs, zls, state_shift[l], s0t, *rec)
        conv = jnp.transpose(state_conv[l], (1, 0, 2))
        o_gs, h_new, conv_new = _lru_step(zs, conv, state_lru[l], lp, wg)
        outs[4].append(sh_new)
        outs[5].append(jnp.transpose(s_new, (3, 0, 1, 2)))
        outs[6].append(jnp.transpose(conv_new, (1, 0, 2)))
        outs[7].append(h_new)

        zp3 = zp.reshape(bp, seq, -1)
        zlp3 = zlp.reshape(bp, seq, LANES)
        nb = max(n for n in (4, 2, 1) if bp % n == 0)
        o_r, s_new, sh_last = _wkv_chunk(zp3, zlp3, *rec, bp, seq, nb)
        o_r = o_r.reshape(bp * seq, RWKV_W)
        last = l == depth - 1
        xp, h_last, conv_last, xs = _outproj_lru(xp, o_r, zp, xs, o_rs, o_gs, zs, lp, wg, w_r, w_g, w_o,
                                                 fg, _row_tile(seq, 256), seq, last)
        outs[0].append(sh_last.reshape(bp, -1))
        outs[1].append(s_new)
        outs[2].append(jnp.transpose(conv_last, (1, 0, 2)))
        outs[3].append(h_last)

    return (xp.reshape(bp, seq, d), xs.reshape(bs, 1, d)) + tuple(jnp.stack(o) for o in outs)
```

```python
import functools

import jax
import jax.numpy as jnp
from jax import lax
from jax.experimental import pallas as pl
from jax.experimental.pallas import tpu as pltpu

F32 = jnp.float32
BF16 = jnp.bfloat16

HEADS = 16
HEAD = 64
RWKV_W = HEADS * HEAD
LORA = 64
LRU_W = 1024
LRU_BLOCKS = 16
CONV_W = 4
LRU_C = 8.0
RMS_EPS = 1e-6
GN_EPS = 1e-5 * HEAD
DECAY_SCALE = 0.6065306597126334

LANES = 128
SUBLANES = 8
WKV_CHUNK = 64
STEP_HEADS = 2
VMEM_LIMIT = 60 * 1024 * 1024

NN = (((1,), (0,)), ((), ()))
NT = (((1,), (1,)), ((), ()))
TN = (((0,), (0,)), ((), ()))


def _bf(x):
    return x.astype(BF16)


def _dg(a, b, dn):
    return lax.dot_general(a, b, dn, preferred_element_type=F32)


def _softplus(x):
    return jnp.maximum(x, 0.0) + jnp.log1p(jnp.exp(-jnp.abs(x)))


def _sigmoid(x):
    return 1.0 / (1.0 + jnp.exp(-x))


def _segsum(x, e):
    rows, n = x.shape[0], x.shape[1] // LANES
    stacked = jnp.concatenate([x[:, LANES * j:LANES * (j + 1)] for j in range(n)], axis=0)
    s = _dg(_bf(stacked), e, NN)
    return jnp.concatenate([s[rows * j:rows * (j + 1), :] for j in range(n)], axis=1)


def _rms(x, g):
    return x * lax.rsqrt(jnp.mean(x * x, axis=-1, keepdims=True) + RMS_EPS) * g


SHIFT_MAIN = 3 * RWKV_W
LORA_COL = 10 * RWKV_W
LORA_BLOCK = LORA_COL // LANES
FUSE_PIECES = 8
INPROJ_TN = 1024
INPROJ_WIDE_TN = 2048
NORM_ROWS = 128
OUT_W_STEPS = 8


def _inproj_head_kernel(xs_ref, g_ref, w_ref, wt_ref, wlo_ref, wr_ref, wg_ref, wo_ref, wb_ref, wl_ref,
                        zs_ref, zls_ref, wrb_ref, wgb_ref, wob_ref, hs_ref):
    j = pl.program_id(0)
    tn = w_ref.shape[-1]

    @pl.when(j == 0)
    def _():
        hs_ref[...] = _bf(_rms(xs_ref[...], g_ref[...]))
        wl_ref[...] = _bf(wlo_ref[...])
        zls_ref[...] = _dg(hs_ref[...], wl_ref[...], NN)

    @pl.when(j < SHIFT_MAIN // tn)
    def _():
        wb_ref[...] = _bf(w_ref[...])

    @pl.when(j >= SHIFT_MAIN // tn)
    def _():
        wb_ref[:, 0:tn - 2 * LORA] = _bf(w_ref[:, 2 * LORA:tn])
        wb_ref[:, tn - 2 * LORA:tn] = _bf(wt_ref[...])

    zs_ref[...] = _dg(hs_ref[...], wb_ref[...], NN)

    @pl.when(j < OUT_W_STEPS)
    def _():
        wrb_ref[...] = _bf(wr_ref[...])
        wgb_ref[...] = _bf(wg_ref[...])
        wob_ref[...] = _bf(wo_ref[...])


def _inproj_head(xs, g, w_in, w_out_rwkv, w_out_lru, w_out, layer, tn):
    ms, d = xs.shape
    n = w_in.shape[-1]
    assert n == LORA_COL + 2 * LORA and SHIFT_MAIN % tn == 0 and 2 * LORA == LANES
    nj = LORA_COL // tn
    assert nj >= OUT_W_STEPS
    lanes_per_tile = tn // LANES
    one = lambda shp, imap: pl.BlockSpec(shp, imap, pipeline_mode=pl.Buffered(1))
    rows = lambda a: a.shape[1] // OUT_W_STEPS
    step = lambda j: jnp.minimum(j, OUT_W_STEPS - 1)
    w_outs = (w_out_rwkv, w_out_lru, w_out)
    return pl.pallas_call(
        _inproj_head_kernel,
        out_shape=(jax.ShapeDtypeStruct((d, LORA_COL), BF16), jax.ShapeDtypeStruct((d, LANES), BF16),
                   jax.ShapeDtypeStruct((ms, LORA_COL), F32), jax.ShapeDtypeStruct((ms, LANES), F32))
        + tuple(jax.ShapeDtypeStruct(a.shape[1:], BF16) for a in w_outs),
        grid=(nj,),
        in_specs=[
            one((ms, d), lambda j: (0, 0)),
            one((1, d), lambda j: (0, 0)),
            pl.BlockSpec((None, d, tn), lambda j: (layer, 0, j)),
            pl.BlockSpec((None, d, LANES), lambda j: (layer, 0, (j + 1) * lanes_per_tile)),
            one((None, d, LANES), lambda j: (layer, 0, SHIFT_MAIN // LANES)),
        ] + [pl.BlockSpec((None, rows(a), a.shape[2]), lambda j: (layer, step(j), 0)) for a in w_outs],
        out_specs=(
            pl.BlockSpec((d, tn), lambda j: (0, j)),
            pl.BlockSpec((d, LANES), lambda j: (0, 0)),
            pl.BlockSpec((ms, tn), lambda j: (0, j)),
            pl.BlockSpec((ms, LANES), lambda j: (0, 0)),
        ) + tuple(pl.BlockSpec((rows(a), a.shape[2]), lambda j: (step(j), 0)) for a in w_outs),
        scratch_shapes=[pltpu.VMEM((ms, d), BF16)],
        compiler_params=pltpu.CompilerParams(
            dimension_semantics=("arbitrary",), vmem_limit_bytes=VMEM_LIMIT),
        name="inproj_head",
    )(xs, g, w_in, w_in, w_in, *w_outs)


def _inproj_kernel(x_ref, g_ref, w_ref, wl_ref, z_ref, zl_ref, h_ref):
    @pl.when(pl.program_id(1) == 0)
    def _():
        for r in range(0, x_ref.shape[0], NORM_ROWS):
            rows = slice(r, r + NORM_ROWS)
            h_ref[rows, :] = _bf(_rms(x_ref[rows, :], g_ref[...]))
        zl_ref[...] = _dg(h_ref[...], wl_ref[...], NN)

    z_ref[...] = _dg(h_ref[...], w_ref[...], NN)


def _inproj(x, g, w, wl, tm, tn):
    m, d = x.shape
    return pl.pallas_call(
        _inproj_kernel,
        out_shape=(jax.ShapeDtypeStruct((m, LORA_COL), F32), jax.ShapeDtypeStruct((m, LANES), F32)),
        grid=(m // tm, LORA_COL // tn),
        in_specs=[
            pl.BlockSpec((tm, d), lambda i, j: (i, 0)),
            pl.BlockSpec((1, d), lambda i, j: (0, 0), pipeline_mode=pl.Buffered(1)),
            pl.BlockSpec((d, tn), lambda i, j: (0, j)),
            pl.BlockSpec((d, LANES), lambda i, j: (0, 0), pipeline_mode=pl.Buffered(1)),
        ],
        out_specs=(
            pl.BlockSpec((tm, tn), lambda i, j: (i, j)),
            pl.BlockSpec((tm, LANES), lambda i, j: (i, 0)),
        ),
        scratch_shapes=[pltpu.VMEM((tm, d), BF16)],
        compiler_params=pltpu.CompilerParams(
            dimension_semantics=("arbitrary", "arbitrary"), vmem_limit_bytes=VMEM_LIMIT),
        name="inproj",
    )(x, g, w, wl)


_MU_R, _MU_K, _MU_V, _W0, _A0, _KK, _KA, _RK, _LNG, _LNB = range(10)


def _prow(pv_ref, i):
    return pv_ref[i:i + 1, :]


def _wkv_prep(zr, zk, zv, zl, pr, pk, pv, pl_, pv_ref, mul_ref, wd_ref, wa_ref, e):
    r = zr + _prow(pv_ref, _MU_R) * (pr - zr)
    k = zk + _prow(pv_ref, _MU_K) * (pk - zk)
    v = zv + _prow(pv_ref, _MU_V) * (pv - zv)
    lo = zl + mul_ref[0:1, :] * (pl_ - zl)
    lw = _dg(_bf(jnp.tanh(lo)), wd_ref[...], NN)
    la = _dg(_bf(lo), wa_ref[...], NN)
    logd = -DECAY_SCALE * _sigmoid(_prow(pv_ref, _W0) + lw)
    a = _sigmoid(_prow(pv_ref, _A0) + la)
    kk = k * _prow(pv_ref, _KK)
    kk = kk * lax.rsqrt(jnp.maximum(_segsum(kk * kk, e), 1e-24))
    k2 = k * (1.0 + (a - 1.0) * _prow(pv_ref, _KA))
    return r, k2, v, -kk, kk * a, logd


def _wkv_bonus_gate(r, k2, v, zrg, pv_ref, e):
    return _segsum(r * k2 * _prow(pv_ref, _RK), e) * v, zrg * _sigmoid(zrg)


def _wkv_norm_gate(y, bonus_v, gate, pv_ref, e):
    mu = _segsum(y, e) * (1.0 / HEAD)
    yc = y - mu
    var = _segsum(yc * yc, e) * (1.0 / HEAD)
    yn = yc * lax.rsqrt(var + GN_EPS) * _prow(pv_ref, _LNG) + _prow(pv_ref, _LNB)
    return _bf((yn + bonus_v) * gate)


def _wkv_post(y, r, k2, v, zrg, pv_ref, e):
    bonus_v, gate = _wkv_bonus_gate(r, k2, v, zrg, pv_ref, e)
    return _wkv_norm_gate(y, bonus_v, gate, pv_ref, e)


def _wkv_chunk_kernel(zr_ref, zk_ref, zv_ref, zrg_ref, zl_ref, pv_ref, mul_ref, wd_ref, wa_ref,
                      e_ref, o_ref, sout_ref, nsh_ref, s_s, prev_s, prevl_s):
    c = pl.program_id(1)
    nc = pl.num_programs(1)
    C = WKV_CHUNK
    assert C == HEAD and 2 * HEAD == LANES
    nb = zr_ref.shape[0]
    rows_all = nb * C
    seqs = range(nb)

    @pl.when(c == 0)
    def _():
        s_s[...] = jnp.zeros_like(s_s)
        prev_s[...] = jnp.zeros_like(prev_s)
        prevl_s[...] = jnp.zeros_like(prevl_s)

    first = lax.broadcasted_iota(jnp.int32, (SUBLANES, 1), 0) == 0

    def shifted(z, prev_ref, lanes):
        rolled = pltpu.roll(z, 1, 0)
        pieces = []
        for b in seqs:
            head = jnp.where(first, prev_ref[b, 0:1, lanes], rolled[b * C:b * C + SUBLANES, :])
            pieces += [head, rolled[b * C + SUBLANES:(b + 1) * C, :]]
        return jnp.concatenate(pieces, axis=0)

    def flat(ref):
        return ref[...].reshape(rows_all, ref.shape[-1])

    zr, zk, zv, zl = flat(zr_ref), flat(zk_ref), flat(zv_ref), flat(zl_ref)
    seg = [slice(RWKV_W * i, RWKV_W * (i + 1)) for i in range(3)]
    pr = shifted(zr, prev_s, seg[0])
    pk = shifted(zk, prev_s, seg[1])
    pv = shifted(zv, prev_s, seg[2])
    pl_ = shifted(zl, prevl_s, slice(0, LANES))
    for b in seqs:
        last = slice(b * C + C - 1, b * C + C)
        prev_s[b, 0:1, seg[0]] = zr[last, :]
        prev_s[b, 0:1, seg[1]] = zk[last, :]
        prev_s[b, 0:1, seg[2]] = zv[last, :]
        prevl_s[b, 0:1, :] = zl[last, :]

    e = e_ref[...]
    r, k2, v, av, bv, logd = _wkv_prep(zr, zk, zv, zl, pr, pk, pv, pl_, pv_ref, mul_ref,
                                       wd_ref, wa_ref, e)

    ti = lax.broadcasted_iota(jnp.int32, (rows_all, rows_all), 0)
    tj = lax.broadcasted_iota(jnp.int32, (rows_all, rows_all), 1)
    tri = jnp.where((ti >= tj) & ((ti & -C) == (tj & -C)), 1.0, 0.0).astype(BF16)
    d_hi = _bf(logd)
    d_r1 = logd - d_hi.astype(F32)
    d_mid = _bf(d_r1)
    d_lo = _bf(d_r1 - d_mid.astype(F32))
    cum = _dg(jnp.concatenate([tri, tri, tri], axis=1), jnp.concatenate([d_hi, d_mid, d_lo], axis=0), NN)
    e_in = jnp.exp(cum)
    e_neg = jnp.exp(-cum)
    a_t = av * jnp.exp(cum - logd)
    r_t = r * e_in
    k_t = k2 * e_neg
    b_t = bv * e_neg
    p_c = [jnp.exp(cum[b * C + C - 1:b * C + C, :]) for b in seqs]

    lane = lax.broadcasted_iota(jnp.int32, (C, LANES), 1)
    trow = lax.broadcasted_iota(jnp.int32, (C, LANES), 0)
    lo = lane < HEAD
    s_in = lane & (HEAD - 1)
    strict = s_in < trow
    incl2 = ((lax.broadcasted_iota(jnp.int32, (C, 2 * LANES), 1) & (HEAD - 1))
             <= lax.broadcasted_iota(jnp.int32, (C, 2 * LANES), 0))
    eye2 = jnp.where(s_in == trow, 1.0, 0.0).astype(F32)
    vrow = lax.broadcasted_iota(jnp.int32, (2 * HEAD, LANES), 0)
    klane = lax.broadcasted_iota(jnp.int32, (2 * HEAD, LANES), 1)
    same_head = (vrow < HEAD) == (klane < HEAD)

    def bd(x):
        z = jnp.zeros_like(x)
        return jnp.concatenate([jnp.where(lo, x, z), jnp.where(lo, z, x)], axis=0)

    npair = HEADS // 2
    units = [(b, p) for b in seqs for p in range(npair)]
    un = range(len(units))
    blk = lambda arr, i: arr[units[i][0] * C:(units[i][0] + 1) * C, LANES * units[i][1]:LANES * (units[i][1] + 1)]
    ar = [_bf(jnp.concatenate([blk(a_t, i), blk(r_t, i)], axis=0)) for i in un]
    bk = [_bf(jnp.concatenate([bd(blk(b_t, i)), bd(blk(k_t, i))], axis=0)) for i in un]
    g = [_dg(ar[i], bk[i], NT) for i in un]
    s0 = [s_s[i] for i in un]
    ars = [_dg(ar[i], _bf(s0[i]), NT) for i in un]
    vbd = [_bf(bd(blk(v, i))) for i in un]
    x = [jnp.where(strict, g[i][0:C, 0:LANES], 0.0) for i in un]
    ak = [jnp.where(strict, g[i][0:C, LANES:2 * LANES], 0.0) for i in un]
    w = [ars[i][0:C, :] + _dg(_bf(ak[i]), vbd[i], NN) for i in un]
    t = [eye2 + x[i] for i in un]
    x = [_dg(_bf(x[i]), _bf(bd(x[i])), NN) for i in un]
    for _ in range(C.bit_length() - 3):
        xt = [_dg(_bf(jnp.concatenate([x[i], t[i]], axis=0)), _bf(bd(x[i])), NN) for i in un]
        x = [xt[i][0:C, :] for i in un]
        t = [t[i] + xt[i][C:2 * C, :] for i in un]
    t = [t[i] + _dg(_bf(t[i]), _bf(bd(x[i])), NN) for i in un]
    u = [_dg(_bf(t[i]), _bf(bd(w[i])), NN) for i in un]
    rbk = [_bf(jnp.where(incl2, g[i][C:2 * C, :], 0.0)) for i in un]
    uvbd = [jnp.concatenate([_bf(bd(u[i])), vbd[i]], axis=0) for i in un]
    y = [ars[i][C:2 * C, :] + _dg(rbk[i], uvbd[i], NN) for i in un]
    uv = [_bf(jnp.concatenate([u[i], blk(v, i)], axis=0)) for i in un]
    pc = [p_c[units[i][0]][:, LANES * units[i][1]:LANES * (units[i][1] + 1)] for i in un]
    bkh = [_bf(jnp.concatenate([blk(b_t, i), blk(k_t, i)], axis=0) * pc[i]) for i in un]
    s1 = [s0[i] * pc[i] + jnp.where(same_head, _dg(uv[i], bkh[i], TN), 0.0) for i in un]
    for i in un:
        s_s[i] = s1[i]

    y_all = jnp.concatenate(
        [jnp.concatenate(y[b * npair:(b + 1) * npair], axis=1) for b in seqs], axis=0)
    o = _wkv_post(y_all, r, k2, v, flat(zrg_ref), pv_ref, e)
    o_ref[...] = o.reshape(nb, C, RWKV_W)

    @pl.when(c == nc - 1)
    def _():
        for i in un:
            b, p = units[i]
            sout_ref[b, 2 * p] = s1[i][0:HEAD, 0:HEAD]
            sout_ref[b, 2 * p + 1] = s1[i][HEAD:2 * HEAD, HEAD:2 * HEAD]
        for b in seqs:
            for q, ref in enumerate((zr_ref, zk_ref, zv_ref)):
                nsh_ref[0, b:b + 1, RWKV_W * q:RWKV_W * (q + 1)] = ref[b, C - 1:C, :]
            nsh_ref[0, b:b + 1, SHIFT_MAIN:SHIFT_MAIN + LANES] = zl_ref[b, C - 1:C, :]


def _wkv_chunk(z, zl, pvec, mul, wd, wa, e, batch, seq, nb):
    C = WKV_CHUNK
    nc = seq // C
    full = lambda shp: pl.BlockSpec(shp, lambda b, c: (0,) * len(shp))
    col = lambda j: pl.BlockSpec((nb, C, RWKV_W), lambda b, c, j=j: (b, c, j))
    return pl.pallas_call(
        _wkv_chunk_kernel,
        out_shape=(jax.ShapeDtypeStruct((batch, seq, RWKV_W), BF16),
                   jax.ShapeDtypeStruct((batch, HEADS, HEAD, HEAD), F32),
                   jax.ShapeDtypeStruct((batch // nb, nb, SHIFT_MAIN + LANES), F32)),
        grid=(batch // nb, nc),
        in_specs=[col(0), col(1), col(2), col(3),
                  pl.BlockSpec((nb, C, LANES), lambda b, c: (b, c, 0)),
                  full(pvec.shape), full(mul.shape), full(wd.shape), full(wa.shape), full(e.shape)],
        out_specs=(pl.BlockSpec((nb, C, RWKV_W), lambda b, c: (b, c, 0)),
                   pl.BlockSpec((nb, HEADS, HEAD, HEAD), lambda b, c: (b, 0, 0, 0)),
                   pl.BlockSpec((1, nb, SHIFT_MAIN + LANES), lambda b, c: (b, 0, 0))),
        scratch_shapes=[pltpu.VMEM((nb * HEADS // 2, 2 * HEAD, 2 * HEAD), F32),
                        pltpu.VMEM((nb, SUBLANES, 3 * RWKV_W), F32),
                        pltpu.VMEM((nb, SUBLANES, LANES), F32)],
        compiler_params=pltpu.CompilerParams(
            dimension_semantics=("arbitrary", "arbitrary"), vmem_limit_bytes=VMEM_LIMIT),
        name="wkv_chunk",
    )(z, z, z, z, zl, pvec, mul, wd, wa, e)


def _wkv_step_kernel(zr_ref, zk_ref, zv_ref, zrg_ref, zl_ref, sh_ref, s0_ref, pv_ref,
                     mul_ref, wd_ref, wa_ref, e_ref, o_ref, sout_ref, nsh_ref,
                     at_s, drt_s, bt_s, kt_s, dt_s, vt_s, brt_s, krt_s, yt_s, keep_s):
    h = pl.program_id(0)
    nh = pl.num_programs(0)
    nseq = zr_ref.shape[0]

    @pl.when(h == 0)
    def _():
        e = e_ref[...]
        r, k2, v, av, bv, logd = _wkv_prep(
            zr_ref[...], zk_ref[...], zv_ref[...], zl_ref[...],
            sh_ref[:, 0:RWKV_W], sh_ref[:, RWKV_W:2 * RWKV_W], sh_ref[:, 2 * RWKV_W:3 * RWKV_W],
            sh_ref[:, SHIFT_MAIN:SHIFT_MAIN + LANES], pv_ref, mul_ref, wd_ref, wa_ref, e)
        nsh_ref[:, 0:RWKV_W] = zr_ref[...]
        nsh_ref[:, RWKV_W:2 * RWKV_W] = zk_ref[...]
        nsh_ref[:, 2 * RWKV_W:SHIFT_MAIN] = zv_ref[...]
        nsh_ref[:, SHIFT_MAIN:SHIFT_MAIN + LANES] = zl_ref[...]
        d = jnp.exp(logd)
        at_s[...] = av.T
        drt_s[...] = (d * r).T
        bt_s[...] = bv.T
        kt_s[...] = k2.T
        dt_s[...] = d.T
        vt_s[...] = v.T
        brt_s[...] = jnp.sum((bv * r).T.reshape(HEADS, HEAD, nseq), axis=1)
        krt_s[...] = jnp.sum((k2 * r).T.reshape(HEADS, HEAD, nseq), axis=1)
        keep_s[0] = r
        keep_s[1] = k2
        keep_s[2] = v

    for u in range(s0_ref.shape[0]):
        head = h * s0_ref.shape[0] + u
        base = pl.multiple_of(head * HEAD, HEAD)
        rows = pl.ds(base, HEAD)
        a_h, dr_h, b_h, k_h, d_h = at_s[rows, :], drt_s[rows, :], bt_s[rows, :], kt_s[rows, :], dt_s[rows, :]
        br_h = brt_s[pl.ds(head, 1), :]
        kr_h = krt_s[pl.ds(head, 1), :]

        def value_rows(g, carry, u=u, base=base, a_h=a_h, dr_h=dr_h, b_h=b_h, k_h=k_h, d_h=d_h,
                       br_h=br_h, kr_h=kr_h):
            off = pl.multiple_of(base + g * SUBLANES, SUBLANES)
            v8 = vt_s[pl.ds(off, SUBLANES), :]
            ys = []
            for j in range(SUBLANES):
                vi = g * SUBLANES + j
                s_v = s0_ref[u, vi]
                sa = jnp.sum(s_v * a_h, axis=0, keepdims=True)
                y0 = jnp.sum(s_v * dr_h, axis=0, keepdims=True)
                v_v = v8[j:j + 1, :]
                sout_ref[u, vi] = s_v * d_h + sa * b_h + v_v * k_h
                ys.append(y0 + sa * br_h + v_v * kr_h)
            yt_s[pl.ds(off, SUBLANES), :] = jnp.concatenate(ys, axis=0)
            return carry

        lax.fori_loop(0, HEAD // SUBLANES, value_rows, 0)

    @pl.when(h == nh - 1)
    def _():
        o_ref[...] = _wkv_post(yt_s[...].T, keep_s[0], keep_s[1], keep_s[2], zrg_ref[...], pv_ref,
                               e_ref[...])


def _wkv_step(z, zl, sh, s0t, pvec, mul, wd, wa, e):
    nseq = z.shape[0]
    full = lambda shp: pl.BlockSpec(shp, lambda i: (0,) * len(shp))
    col = lambda j: pl.BlockSpec((nseq, RWKV_W), lambda i, j=j: (0, j))
    st_block = (STEP_HEADS, HEAD, HEAD, nseq)
    wide = pltpu.VMEM((RWKV_W, nseq), F32)
    return pl.pallas_call(
        _wkv_step_kernel,
        out_shape=(jax.ShapeDtypeStruct((nseq, RWKV_W), BF16),
                   jax.ShapeDtypeStruct(s0t.shape, F32),
                   jax.ShapeDtypeStruct(sh.shape, F32)),
        grid=(HEADS // STEP_HEADS,),
        in_specs=[col(0), col(1), col(2), col(3),
                  full(zl.shape), full(sh.shape),
                  pl.BlockSpec(st_block, lambda i: (i, 0, 0, 0)),
                  full(pvec.shape), full(mul.shape), full(wd.shape), full(wa.shape), full(e.shape)],
        out_specs=(full((nseq, RWKV_W)),
                   pl.BlockSpec(st_block, lambda i: (i, 0, 0, 0)),
                   full(sh.shape)),
        scratch_shapes=[wide] * 6 + [pltpu.VMEM((HEADS, nseq), F32)] * 2
                       + [wide, pltpu.VMEM((3, nseq, RWKV_W), F32)],
        compiler_params=pltpu.CompilerParams(
            dimension_semantics=("arbitrary",), vmem_limit_bytes=VMEM_LIMIT),
        name="wkv_step",
    )(z, z, z, z, zl, sh, s0t, pvec, mul, wd, wa, e)


_CW0, _CW1, _CW2, _CW3, _CB, _GXB, _GAB, _LAM = range(8)


def _lru_gates(xc, lp_ref, wg_ref):
    xb = _bf(xc)
    ngroups = wg_ref.shape[0]
    gs = [_dg(xb[:, LANES * g:LANES * (g + 1)], wg_ref[g], NN) for g in range(ngroups)]
    gx_pre = jnp.concatenate([gs[g][:, 0:LANES] for g in range(ngroups)], axis=1)
    ga_pre = jnp.concatenate([gs[g][:, LANES:2 * LANES] for g in range(ngroups)], axis=1)
    gx = _sigmoid(gx_pre + _prow(lp_ref, _GXB))
    ga = _sigmoid(ga_pre + _prow(lp_ref, _GAB))
    log_a = -LRU_C * ga * _softplus(-_prow(lp_ref, _LAM))
    a = jnp.exp(log_a)
    mult = jnp.sqrt((1.0 - a) * (1.0 + a))
    return a, mult * gx * xc


def _lru_scan_rows(a, b, zg, hc):
    row8 = lax.broadcasted_iota(jnp.int32, (SUBLANES, 1), 0)
    hs = []
    for i in range(a.shape[0] // SUBLANES):
        a8 = a[SUBLANES * i:SUBLANES * (i + 1), :]
        b8 = b[SUBLANES * i:SUBLANES * (i + 1), :]
        for s in (1, 2, 4):
            keep = row8 >= s
            b8 = jnp.where(keep, a8 * pltpu.roll(b8, s, 0) + b8, b8)
            a8 = jnp.where(keep, a8 * pltpu.roll(a8, s, 0), a8)
        hb = b8 + a8 * hc
        hs.append(hb)
        hc = jnp.broadcast_to(hb[SUBLANES - 1:SUBLANES, :], hb.shape)
    return _bf(jnp.concatenate(hs, axis=0) * (zg * _sigmoid(zg))), hc


def _lru_step_kernel(zx_ref, zg_ref, conv_ref, h0_ref, lp_ref, wg_ref, o_ref, hnew_ref, cnew_ref):
    zx = zx_ref[...]
    for j in range(CONV_W - 2):
        cnew_ref[j] = conv_ref[j + 1]
    cnew_ref[CONV_W - 2] = zx
    xc = _prow(lp_ref, _CW3) * zx + _prow(lp_ref, _CB)
    for j in range(CONV_W - 1):
        xc = xc + _prow(lp_ref, j) * conv_ref[j]
    a, b = _lru_gates(xc, lp_ref, wg_ref)
    h = a * h0_ref[...] + b
    hnew_ref[...] = h
    zg = zg_ref[...]
    o_ref[...] = _bf(h * (zg * _sigmoid(zg)))


def _lru_step(z_main, conv, h0, lp, wg):
    nb = z_main.shape[0]
    full = lambda shp: pl.BlockSpec(shp, lambda i: (0,) * len(shp))
    col = lambda j: pl.BlockSpec((nb, LRU_W), lambda i, j=j: (0, j))
    return pl.pallas_call(
        _lru_step_kernel,
        out_shape=(jax.ShapeDtypeStruct((nb, LRU_W), BF16), jax.ShapeDtypeStruct((nb, LRU_W), F32),
                   jax.ShapeDtypeStruct(conv.shape, F32)),
        grid=(1,),
        in_specs=[col(4), col(5), full(conv.shape), full(h0.shape), full(lp.shape), full(wg.shape)],
        out_specs=(full((nb, LRU_W)), full((nb, LRU_W)), full(conv.shape)),
        compiler_params=pltpu.CompilerParams(
            dimension_semantics=("arbitrary",), vmem_limit_bytes=VMEM_LIMIT),
        name="lru_step",
    )(z_main, z_main, conv, h0, lp, wg)


def _project(x, o_r, o_g, m_r, m_g, wr_ref, wg_ref, wo_ref, fg_ref, final):
    y_r = _dg(o_r, wr_ref[...], NN)
    y_g = _dg(o_g, wg_ref[...], NN)
    merged = _sigmoid(m_r) * y_r + _sigmoid(m_g) * y_g
    out = x + _dg(_bf(merged), wo_ref[...], NN)
    return _rms(out, fg_ref[...]) if final else out


def _outproj_lru_kernel(x_ref, or_ref, mr_ref, mg_ref, zx_ref, zg_ref, xs_ref, ors_ref, ogs_ref,
                        mrs_ref, mgs_ref, lp_ref, wgate_ref, wr_ref, wg_ref, wo_ref, fg_ref,
                        out_ref, hlast_ref, cnew_ref, outs_ref, og_s, mg_s, xb_s, hc_s, *, final,
                        tiles_per_seq):
    i = pl.program_id(0)
    n = pl.num_programs(0) - 2
    tm, d = x_ref.shape

    @pl.when(i == 0)
    def _():
        og_s[...] = jnp.zeros_like(og_s)
        mg_s[...] = jnp.zeros_like(mg_s)
        xb_s[...] = jnp.zeros_like(xb_s)
        hc_s[...] = jnp.zeros_like(hc_s)
        outs_ref[...] = _project(xs_ref[...], ors_ref[...], ogs_ref[...], mrs_ref[...], mgs_ref[...],
                                 wr_ref, wg_ref, wo_ref, fg_ref, final)

    og_prev = og_s[...]
    mg_prev = mg_s[...]
    t = lax.rem(jnp.minimum(i, n - 1), tiles_per_seq)
    first = t == 0

    xb_s[0:SUBLANES, :] = jnp.where(first, 0.0, xb_s[0:SUBLANES, :])
    xb_s[SUBLANES:SUBLANES + tm, :] = zx_ref[...]

    outs, a_parts, b_parts = [], [], []
    for c in range(FUSE_PIECES):
        cs = slice(c * d // FUSE_PIECES, (c + 1) * d // FUSE_PIECES)
        outs.append(x_ref[:, cs] + _dg(mg_prev, wo_ref[:, cs], NN))
        r0, r1 = SUBLANES + c * tm // FUSE_PIECES, SUBLANES + (c + 1) * tm // FUSE_PIECES
        xc = _prow(lp_ref, _CW3) * xb_s[r0:r1, :] + _prow(lp_ref, _CB)
        for j in range(1, CONV_W):
            xc = xc + _prow(lp_ref, CONV_W - 1 - j) * xb_s[r0 - j:r1 - j, :]
        a_c, b_c = _lru_gates(xc, lp_ref, wgate_ref)
        a_parts.append(a_c)
        b_parts.append(b_c)
    out = jnp.concatenate(outs, axis=1)
    out_ref[...] = _rms(out, fg_ref[...]) if final else out
    xb_s[0:SUBLANES, :] = xb_s[tm:tm + SUBLANES, :]

    o_g, hc = _lru_scan_rows(jnp.concatenate(a_parts, axis=0), jnp.concatenate(b_parts, axis=0),
                             zg_ref[...], jnp.where(first, 0.0, hc_s[...]))
    hc_s[...] = hc
    og_s[...] = o_g

    y_r = _dg(or_ref[...], wr_ref[...], NN)
    y_g = _dg(og_prev, wg_ref[...], NN)
    mg_s[...] = _bf(_sigmoid(mr_ref[...]) * y_r + _sigmoid(mg_ref[...]) * y_g)

    @pl.when((t == tiles_per_seq - 1) & (i < n))
    def _():
        seq = i // tiles_per_seq
        hlast_ref[pl.ds(seq, 1), :] = hc[0:1, :]
        for j in range(CONV_W - 1):
            row = tm - (CONV_W - 1) + j
            cnew_ref[j, pl.ds(seq, 1), :] = zx_ref[row:row + 1, :]


def _outproj_lru(x, o_r, z, xs, o_rs, o_gs, zs, lp, wgate, w_r, w_g, w_o, fg, tm, seq, final):
    m, d = x.shape
    ms = xs.shape[0]
    n = m // tm
    tiles_per_seq = seq // tm
    const = lambda shp: pl.BlockSpec(shp, lambda i: (0,) * len(shp), pipeline_mode=pl.Buffered(1))
    back = lambda i, k: jnp.clip(i - k, 0, n - 1)
    return pl.pallas_call(
        functools.partial(_outproj_lru_kernel, final=final, tiles_per_seq=tiles_per_seq),
        out_shape=(jax.ShapeDtypeStruct((m, d), F32),
                   jax.ShapeDtypeStruct((m // seq, LRU_W), F32),
                   jax.ShapeDtypeStruct((CONV_W - 1, m // seq, LRU_W), F32),
                   jax.ShapeDtypeStruct((ms, d), F32)),
        grid=(n + 2,),
        in_specs=[
            pl.BlockSpec((tm, d), lambda i: (back(i, 2), 0)),
            pl.BlockSpec((tm, RWKV_W), lambda i: (back(i, 1), 0)),
            pl.BlockSpec((tm, d), lambda i: (back(i, 1), 3)),
            pl.BlockSpec((tm, d), lambda i: (back(i, 1), 4)),
            pl.BlockSpec((tm, LRU_W), lambda i: (back(i, 0), 4)),
            pl.BlockSpec((tm, LRU_W), lambda i: (back(i, 0), 5)),
            const(xs.shape), const(o_rs.shape), const(o_gs.shape),
            pl.BlockSpec((ms, d), lambda i: (0, 3), pipeline_mode=pl.Buffered(1)),
            pl.BlockSpec((ms, d), lambda i: (0, 4), pipeline_mode=pl.Buffered(1)),
            const(lp.shape), const(wgate.shape), const(w_r.shape), const(w_g.shape), const(w_o.shape),
            const(fg.shape),
        ],
        out_specs=(pl.BlockSpec((tm, d), lambda i: (back(i, 2), 0)),
                   pl.BlockSpec((m // seq, LRU_W), lambda i: (0, 0)),
                   pl.BlockSpec((CONV_W - 1, m // seq, LRU_W), lambda i: (0, 0, 0)),
                   pl.BlockSpec((ms, d), lambda i: (0, 0))),
        scratch_shapes=[pltpu.VMEM((tm, LRU_W), BF16),
                        pltpu.VMEM((tm, d), BF16),
                        pltpu.VMEM((SUBLANES + tm, LRU_W), F32),
                        pltpu.VMEM((SUBLANES, LRU_W), F32)],
        compiler_params=pltpu.CompilerParams(
            dimension_semantics=("arbitrary",), vmem_limit_bytes=VMEM_LIMIT),
        name="outproj_lru",
    )(x, o_r, z, z, z, z, xs, o_rs, o_gs, zs, zs, lp, wgate, w_r, w_g, w_o, fg)


def _row_tile(m, want):
    t = min(m, want)
    assert m % t == 0, (m, t)
    return t


def _pack_params_kernel(mu_ref, w0_ref, a0_ref, kk_ref, ka_ref, lng_ref, lnb_ref, rk_ref, wdu_ref,
                        wau_ref, cw_ref, cb_ref, gxb_ref, gab_ref, lam_ref, gxw_ref, gaw_ref,
                        pvec_ref, mul_ref, wd_ref, wa_ref, e_ref, lp_ref, wg_ref):
    pvec_ref[...] = jnp.zeros_like(pvec_ref)
    for i in range(3):
        pvec_ref[_MU_R + i:_MU_R + i + 1, :] = mu_ref[:, RWKV_W * i:RWKV_W * (i + 1)]
    for row, ref in ((_W0, w0_ref), (_A0, a0_ref), (_KK, kk_ref), (_KA, ka_ref),
                     (_LNG, lng_ref), (_LNB, lnb_ref)):
        pvec_ref[row:row + 1, :] = ref[...]
    for h in range(HEADS):
        pvec_ref[_RK:_RK + 1, HEAD * h:HEAD * (h + 1)] = rk_ref[h:h + 1, :]
    mul_ref[...] = jnp.broadcast_to(mu_ref[:, 3 * RWKV_W:3 * RWKV_W + 2 * LORA], mul_ref.shape)

    zeros = jnp.zeros((LORA, RWKV_W), BF16)
    wd_ref[0:LORA, :] = _bf(wdu_ref[...])
    wd_ref[LORA:2 * LORA, :] = zeros
    wa_ref[0:LORA, :] = zeros
    wa_ref[LORA:2 * LORA, :] = _bf(wau_ref[...])

    ri = lax.broadcasted_iota(jnp.int32, (LANES, LANES), 0)
    ci = lax.broadcasted_iota(jnp.int32, (LANES, LANES), 1)
    e_ref[...] = jnp.where((ri < HEAD) == (ci < HEAD), 1.0, 0.0).astype(BF16)

    lp_ref[_CW0:_CW0 + CONV_W, :] = cw_ref[...]
    for row, ref in ((_CB, cb_ref), (_GXB, gxb_ref), (_GAB, gab_ref), (_LAM, lam_ref)):
        lp_ref[row:row + 1, :] = ref[...]

    blk = LRU_W // LRU_BLOCKS
    z = jnp.zeros((blk, blk), F32)
    for g in range(LRU_BLOCKS // 2):
        top = jnp.concatenate([gxw_ref[2 * g], z, gaw_ref[2 * g], z], axis=1)
        bot = jnp.concatenate([z, gxw_ref[2 * g + 1], z, gaw_ref[2 * g + 1]], axis=1)
        wg_ref[g] = _bf(jnp.concatenate([top, bot], axis=0))


def _pack_params(l, rwkv_mu, w_decay0, w_decay_up, w_iclr0, w_iclr_up, k_k, k_a, r_k, ln_x_g,
                 ln_x_b, conv_w, conv_b, lru_gx_w, lru_gx_b, lru_ga_w, lru_ga_b, lru_lambda):
    blk = LRU_W // LRU_BLOCKS
    assert 2 * blk == LANES and 2 * LORA == LANES
    row = lambda a: pl.BlockSpec((1, a.shape[-1]), lambda i: (l, 0))
    mat = lambda a: pl.BlockSpec((None,) + a.shape[1:], lambda i: (l,) + (0,) * (a.ndim - 1))
    full = lambda shp: pl.BlockSpec(shp, lambda i: (0,) * len(shp))
    rows = (rwkv_mu, w_decay0, w_iclr0, k_k, k_a, ln_x_g, ln_x_b)
    out_shapes = ((16, RWKV_W, F32), (SUBLANES, LANES, F32), (LANES, RWKV_W, BF16), (LANES, RWKV_W, BF16),
                  (LANES, LANES, BF16), (SUBLANES, LRU_W, F32))
    outs = tuple(jax.ShapeDtypeStruct(s[:2], s[2]) for s in out_shapes)
    outs += (jax.ShapeDtypeStruct((LRU_BLOCKS // 2, LANES, 2 * LANES), BF16),)
    return pl.pallas_call(
        _pack_params_kernel,
        out_shape=outs,
        grid=(1,),
        in_specs=[row(a) for a in rows] + [mat(r_k), mat(w_decay_up), mat(w_iclr_up), mat(conv_w),
                                          row(conv_b), row(lru_gx_b), row(lru_ga_b), row(lru_lambda),
                                          mat(lru_gx_w), mat(lru_ga_w)],
        out_specs=tuple(full(o.shape) for o in outs),
        compiler_params=pltpu.CompilerParams(
            dimension_semantics=("arbitrary",), vmem_limit_bytes=VMEM_LIMIT),
        name="pack_params",
    )(*rows, r_k, w_decay_up, w_iclr_up, conv_w, conv_b, lru_gx_b, lru_ga_b, lru_lambda, lru_gx_w,
      lru_ga_w)


def kernel(x_prompt, x_sample, state_shift, state_wkv, state_conv, state_lru, norm_g, w_in, rwkv_mu,
           w_decay0, w_decay_up, w_iclr0, w_iclr_up, k_k, k_a, r_k, ln_x_g, ln_x_b, w_out_rwkv,
           conv_w, conv_b, lru_gx_w, lru_gx_b, lru_ga_w, lru_ga_b, lru_lambda, w_out_lru, w_out,
           final_norm_g):
    bp, seq, d = x_prompt.shape
    bs = x_sample.shape[0]
    assert x_sample.shape[1] == 1 and seq % WKV_CHUNK == 0
    depth = w_in.shape[0]
    xp = x_prompt.reshape(bp * seq, d)
    xs = x_sample.reshape(bs, d)
    fg = final_norm_g.reshape(1, d)
    outs = [[] for _ in range(8)]
    for l in range(depth):
        pvec, mul, wd, wa, e, lp, wg = _pack_params(
            l, rwkv_mu, w_decay0, w_decay_up, w_iclr0, w_iclr_up, k_k, k_a, r_k, ln_x_g, ln_x_b,
            conv_w, conv_b, lru_gx_w, lru_gx_b, lru_ga_w, lru_ga_b, lru_lambda)
        g = norm_g[l].reshape(1, d)
        rec = (pvec, mul, wd, wa, e)
        w, wl, zs, zls, w_r, w_g, w_o = _inproj_head(xs, g, w_in, w_out_rwkv, w_out_lru, w_out, l, INPROJ_TN)
        zp, zlp = _inproj(xp, g, w, wl, _row_tile(bp * seq, 1024), INPROJ_WIDE_TN)

        s0t = jnp.transpose(state_wkv[l], (1, 2, 3, 0))
        o_rs, s_new, sh_new = _wkv_step(zs, zls, state_shift[l], s0t, *rec)
        conv = jnp.transpose(state_conv[l], (1, 0, 2))
        o_gs, h_new, conv_new = _lru_step(zs, conv, state_lru[l], lp, wg)
        outs[4].append(sh_new)
        outs[5].append(jnp.transpose(s_new, (3, 0, 1, 2)))
        outs[6].append(jnp.transpose(conv_new, (1, 0, 2)))
        outs[7].append(h_new)

        zp3 = zp.reshape(bp, seq, -1)
        zlp3 = zlp.reshape(bp, seq, LANES)
        nb = max(n for n in (4, 2, 1) if bp % n == 0)
        o_r, s_new, sh_last = _wkv_chunk(zp3, zlp3, *rec, bp, seq, nb)
        o_r = o_r.reshape(bp * seq, RWKV_W)
        last = l == depth - 1
        xp, h_last, conv_last, xs = _outproj_lru(xp, o_r, zp, xs, o_rs, o_gs, zs, lp, wg, w_r, w_g, w_o,
                                                 fg, _row_tile(seq, 256), seq, last)
        outs[0].append(sh_last.reshape(bp, -1))
        outs[1].append(s_new)
        outs[2].append(jnp.transpose(conv_last, (1, 0, 2)))
        outs[3].append(h_last)

    return (xp.reshape(bp, seq, d), xs.reshape(bs, 1, d)) + tuple(jnp.stack(o) for o in outs)
```

```python
import functools

import jax
import jax.numpy as jnp
from jax import lax
from jax.experimental import pallas as pl
from jax.experimental.pallas import tpu as pltpu

F32 = jnp.float32
BF16 = jnp.bfloat16

HEADS = 16
HEAD = 64
RWKV_W = HEADS * HEAD
LORA = 64
LRU_W = 1024
LRU_BLOCKS = 16
CONV_W = 4
LRU_C = 8.0
RMS_EPS = 1e-6
GN_EPS = 1e-5 * HEAD
DECAY_SCALE = 0.6065306597126334

LANES = 128
SUBLANES = 8
WKV_CHUNK = 64
STEP_HEADS = 2
VMEM_LIMIT = 60 * 1024 * 1024

NN = (((1,), (0,)), ((), ()))
NT = (((1,), (1,)), ((), ()))
TN = (((0,), (0,)), ((), ()))


def _bf(x):
    return x.astype(BF16)


def _dg(a, b, dn):
    return lax.dot_general(a, b, dn, preferred_element_type=F32)


def _softplus(x):
    return jnp.maximum(x, 0.0) + jnp.log1p(jnp.exp(-jnp.abs(x)))


def _sigmoid(x):
    return 1.0 / (1.0 + jnp.exp(-x))


def _segsum(x, e):
    rows, n = x.shape[0], x.shape[1] // LANES
    stacked = jnp.concatenate([x[:, LANES * j:LANES * (j + 1)] for j in range(n)], axis=0)
    s = _dg(_bf(stacked), e, NN)
    return jnp.concatenate([s[rows * j:rows * (j + 1), :] for j in range(n)], axis=1)


def _rms(x, g):
    return x * lax.rsqrt(jnp.mean(x * x, axis=-1, keepdims=True) + RMS_EPS) * g


SHIFT_MAIN = 3 * RWKV_W
LORA_COL = 10 * RWKV_W
LORA_BLOCK = LORA_COL // LANES
FUSE_PIECES = 8
INPROJ_TN = 1024
INPROJ_WIDE_TN = 2048
NORM_ROWS = 128
OUT_W_STEPS = 8


def _inproj_head_kernel(xs_ref, g_ref, w_ref, wt_ref, wlo_ref, wr_ref, wg_ref, wo_ref, wb_ref, wl_ref,
                        zs_ref, zls_ref, wrb_ref, wgb_ref, wob_ref, hs_ref):
    j = pl.program_id(0)
    tn = w_ref.shape[-1]

    @pl.when(j == 0)
    def _():
        hs_ref[...] = _bf(_rms(xs_ref[...], g_ref[...]))
        wl_ref[...] = _bf(wlo_ref[...])
        zls_ref[...] = _dg(hs_ref[...], wl_ref[...], NN)

    @pl.when(j < SHIFT_MAIN // tn)
    def _():
        wb_ref[...] = _bf(w_ref[...])

    @pl.when(j >= SHIFT_MAIN // tn)
    def _():
        wb_ref[:, 0:tn - 2 * LORA] = _bf(w_ref[:, 2 * LORA:tn])
        wb_ref[:, tn - 2 * LORA:tn] = _bf(wt_ref[...])

    zs_ref[...] = _dg(hs_ref[...], wb_ref[...], NN)

    @pl.when(j < OUT_W_STEPS)
    def _():
        wrb_ref[...] = _bf(wr_ref[...])
        wgb_ref[...] = _bf(wg_ref[...])
        wob_ref[...] = _bf(wo_ref[...])


def _inproj_head(xs, g, w_in, w_out_rwkv, w_out_lru, w_out, layer, tn):
    ms, d = xs.shape
    n = w_in.shape[-1]
    assert n == LORA_COL + 2 * LORA and SHIFT_MAIN % tn == 0 and 2 * LORA == LANES
    nj = LORA_COL // tn
    assert nj >= OUT_W_STEPS
    lanes_per_tile = tn // LANES
    one = lambda shp, imap: pl.BlockSpec(shp, imap, pipeline_mode=pl.Buffered(1))
    rows = lambda a: a.shape[1] // OUT_W_STEPS
    step = lambda j: jnp.minimum(j, OUT_W_STEPS - 1)
    w_outs = (w_out_rwkv, w_out_lru, w_out)
    return pl.pallas_call(
        _inproj_head_kernel,
        out_shape=(jax.ShapeDtypeStruct((d, LORA_COL), BF16), jax.ShapeDtypeStruct((d, LANES), BF16),
                   jax.ShapeDtypeStruct((ms, LORA_COL), F32), jax.ShapeDtypeStruct((ms, LANES), F32))
        + tuple(jax.ShapeDtypeStruct(a.shape[1:], BF16) for a in w_outs),
        grid=(nj,),
        in_specs=[
            one((ms, d), lambda j: (0, 0)),
            one((1, d), lambda j: (0, 0)),
            pl.BlockSpec((None, d, tn), lambda j: (layer, 0, j)),
            pl.BlockSpec((None, d, LANES), lambda j: (layer, 0, (j + 1) * lanes_per_tile)),
            one((None, d, LANES), lambda j: (layer, 0, SHIFT_MAIN // LANES)),
        ] + [pl.BlockSpec((None, rows(a), a.shape[2]), lambda j: (layer, step(j), 0)) for a in w_outs],
        out_specs=(
            pl.BlockSpec((d, tn), lambda j: (0, j)),
            pl.BlockSpec((d, LANES), lambda j: (0, 0)),
            pl.BlockSpec((ms, tn), lambda j: (0, j)),
            pl.BlockSpec((ms, LANES), lambda j: (0, 0)),
        ) + tuple(pl.BlockSpec((rows(a), a.shape[2]), lambda j: (step(j), 0)) for a in w_outs),
        scratch_shapes=[pltpu.VMEM((ms, d), BF16)],
        compiler_params=pltpu.CompilerParams(
            dimension_semantics=("arbitrary",), vmem_limit_bytes=VMEM_LIMIT),
        name="inproj_head",
    )(xs, g, w_in, w_in, w_in, *w_outs)


def _inproj_kernel(x_ref, g_ref, w_ref, wl_ref, z_ref, zl_ref, h_ref):
    @pl.when(pl.program_id(1) == 0)
    def _():
        for r in range(0, x_ref.shape[0], NORM_ROWS):
            rows = slice(r, r + NORM_ROWS)
            h_ref[rows, :] = _bf(_rms(x_ref[rows, :], g_ref[...]))
        zl_ref[...] = _dg(h_ref[...], wl_ref[...], NN)

    z_ref[...] = _dg(h_ref[...], w_ref[...], NN)


def _inproj(x, g, w, wl, tm, tn):
    m, d = x.shape
    return pl.pallas_call(
        _inproj_kernel,
        out_shape=(jax.ShapeDtypeStruct((m, LORA_COL), F32), jax.ShapeDtypeStruct((m, LANES), F32)),
        grid=(m // tm, LORA_COL // tn),
        in_specs=[
            pl.BlockSpec((tm, d), lambda i, j: (i, 0)),
            pl.BlockSpec((1, d), lambda i, j: (0, 0), pipeline_mode=pl.Buffered(1)),
            pl.BlockSpec((d, tn), lambda i, j: (0, j)),
            pl.BlockSpec((d, LANES), lambda i, j: (0, 0), pipeline_mode=pl.Buffered(1)),
        ],
        out_specs=(
            pl.BlockSpec((tm, tn), lambda i, j: (i, j)),
            pl.BlockSpec((tm, LANES), lambda i, j: (i, 0)),
        ),
        scratch_shapes=[pltpu.VMEM((tm, d), BF16)],
        compiler_params=pltpu.CompilerParams(
            dimension_semantics=("arbitrary", "arbitrary"), vmem_limit_bytes=VMEM_LIMIT),
        name="inproj",
    )(x, g, w, wl)


_MU_R, _MU_K, _MU_V, _W0, _A0, _KK, _KA, _RK, _LNG, _LNB = range(10)


def _prow(pv_ref, i):
    return pv_ref[i:i + 1, :]


def _wkv_prep(zr, zk, zv, zl, pr, pk, pv, pl_, pv_ref, mul_ref, wd_ref, wa_ref, e):
    r = zr + _prow(pv_ref, _MU_R) * (pr - zr)
    k = zk + _prow(pv_ref, _MU_K) * (pk - zk)
    v = zv + _prow(pv_ref, _MU_V) * (pv - zv)
    lo = zl + mul_ref[0:1, :] * (pl_ - zl)
    lw = _dg(_bf(jnp.tanh(lo)), wd_ref[...], NN)
    la = _dg(_bf(lo), wa_ref[...], NN)
    logd = -DECAY_SCALE * _sigmoid(_prow(pv_ref, _W0) + lw)
    a = _sigmoid(_prow(pv_ref, _A0) + la)
    kk = k * _prow(pv_ref, _KK)
    kk = kk * lax.rsqrt(jnp.maximum(_segsum(kk * kk, e), 1e-24))
    k2 = k * (1.0 + (a - 1.0) * _prow(pv_ref, _KA))
    return r, k2, v, -kk, kk * a, logd


def _wkv_bonus_gate(r, k2, v, zrg, pv_ref, e):
    return _segsum(r * k2 * _prow(pv_ref, _RK), e) * v, zrg * _sigmoid(zrg)


def _wkv_norm_gate(y, bonus_v, gate, pv_ref, e):
    mu = _segsum(y, e) * (1.0 / HEAD)
    yc = y - mu
    var = _segsum(yc * yc, e) * (1.0 / HEAD)
    yn = yc * lax.rsqrt(var + GN_EPS) * _prow(pv_ref, _LNG) + _prow(pv_ref, _LNB)
    return _bf((yn + bonus_v) * gate)


def _wkv_post(y, r, k2, v, zrg, pv_ref, e):
    bonus_v, gate = _wkv_bonus_gate(r, k2, v, zrg, pv_ref, e)
    return _wkv_norm_gate(y, bonus_v, gate, pv_ref, e)


def _wkv_chunk_kernel(zr_ref, zk_ref, zv_ref, zrg_ref, zl_ref, pv_ref, mul_ref, wd_ref, wa_ref,
                      e_ref, o_ref, sout_ref, nsh_ref, s_s, prev_s, prevl_s):
    c = pl.program_id(1)
    nc = pl.num_programs(1)
    C = WKV_CHUNK
    assert C == HEAD and 2 * HEAD == LANES
    nb = zr_ref.shape[0]
    rows_all = nb * C
    seqs = range(nb)

    @pl.when(c == 0)
    def _():
        s_s[...] = jnp.zeros_like(s_s)
        prev_s[...] = jnp.zeros_like(prev_s)
        prevl_s[...] = jnp.zeros_like(prevl_s)

    first = lax.broadcasted_iota(jnp.int32, (SUBLANES, 1), 0) == 0

    def shifted(z, prev_ref, lanes):
        rolled = pltpu.roll(z, 1, 0)
        pieces = []
        for b in seqs:
            head = jnp.where(first, prev_ref[b, 0:1, lanes], rolled[b * C:b * C + SUBLANES, :])
            pieces += [head, rolled[b * C + SUBLANES:(b + 1) * C, :]]
        return jnp.concatenate(pieces, axis=0)

    def flat(ref):
        return ref[...].reshape(rows_all, ref.shape[-1])

    zr, zk, zv, zl = flat(zr_ref), flat(zk_ref), flat(zv_ref), flat(zl_ref)
    seg = [slice(RWKV_W * i, RWKV_W * (i + 1)) for i in range(3)]
    pr = shifted(zr, prev_s, seg[0])
    pk = shifted(zk, prev_s, seg[1])
    pv = shifted(zv, prev_s, seg[2])
    pl_ = shifted(zl, prevl_s, slice(0, LANES))
    for b in seqs:
        last = slice(b * C + C - 1, b * C + C)
        prev_s[b, 0:1, seg[0]] = zr[last, :]
        prev_s[b, 0:1, seg[1]] = zk[last, :]
        prev_s[b, 0:1, seg[2]] = zv[last, :]
        prevl_s[b, 0:1, :] = zl[last, :]

    e = e_ref[...]
    r, k2, v, av, bv, logd = _wkv_prep(zr, zk, zv, zl, pr, pk, pv, pl_, pv_ref, mul_ref,
                                       wd_ref, wa_ref, e)

    ti = lax.broadcasted_iota(jnp.int32, (rows_all, rows_all), 0)
    tj = lax.broadcasted_iota(jnp.int32, (rows_all, rows_all), 1)
    tri = jnp.where((ti >= tj) & ((ti & -C) == (tj & -C)), 1.0, 0.0).astype(BF16)
    d_hi = _bf(logd)
    d_r1 = logd - d_hi.astype(F32)
    d_mid = _bf(d_r1)
    d_lo = _bf(d_r1 - d_mid.astype(F32))
    cum = _dg(jnp.concatenate([tri, tri, tri], axis=1), jnp.concatenate([d_hi, d_mid, d_lo], axis=0), NN)
    e_in = jnp.exp(cum)
    e_neg = jnp.exp(-cum)
    a_t = av * jnp.exp(cum - logd)
    r_t = r * e_in
    k_t = k2 * e_neg
    b_t = bv * e_neg
    p_c = [jnp.exp(cum[b * C + C - 1:b * C + C, :]) for b in seqs]

    lane = lax.broadcasted_iota(jnp.int32, (C, LANES), 1)
    trow = lax.broadcasted_iota(jnp.int32, (C, LANES), 0)
    lo = lane < HEAD
    s_in = lane & (HEAD - 1)
    strict = s_in < trow
    incl2 = ((lax.broadcasted_iota(jnp.int32, (C, 2 * LANES), 1) & (HEAD - 1))
             <= lax.broadcasted_iota(jnp.int32, (C, 2 * LANES), 0))
    eye2 = jnp.where(s_in == trow, 1.0, 0.0).astype(F32)
    vrow = lax.broadcasted_iota(jnp.int32, (2 * HEAD, LANES), 0)
    klane = lax.broadcasted_iota(jnp.int32, (2 * HEAD, LANES), 1)
    same_head = (vrow < HEAD) == (klane < HEAD)

    def bd(x):
        z = jnp.zeros_like(x)
        return jnp.concatenate([jnp.where(lo, x, z), jnp.where(lo, z, x)], axis=0)

    npair = HEADS // 2
    units = [(b, p) for b in seqs for p in range(npair)]
    un = range(len(units))
    blk = lambda arr, i: arr[units[i][0] * C:(units[i][0] + 1) * C, LANES * units[i][1]:LANES * (units[i][1] + 1)]
    ar = [_bf(jnp.concatenate([blk(a_t, i), blk(r_t, i)], axis=0)) for i in un]
    bk = [_bf(jnp.concatenate([bd(blk(b_t, i)), bd(blk(k_t, i))], axis=0)) for i in un]
    g = [_dg(ar[i], bk[i], NT) for i in un]
    s0 = [s_s[i] for i in un]
    ars = [_dg(ar[i], _bf(s0[i]), NT) for i in un]
    vbd = [_bf(bd(blk(v, i))) for i in un]
    x = [jnp.where(strict, g[i][0:C, 0:LANES], 0.0) for i in un]
    ak = [jnp.where(strict, g[i][0:C, LANES:2 * LANES], 0.0) for i in un]
    w = [ars[i][0:C, :] + _dg(_bf(ak[i]), vbd[i], NN) for i in un]
    t = [eye2 + x[i] for i in un]
    x = [_dg(_bf(x[i]), _bf(bd(x[i])), NN) for i in un]
    for _ in range(C.bit_length() - 3):
        xt = [_dg(_bf(jnp.concatenate([x[i], t[i]], axis=0)), _bf(bd(x[i])), NN) for i in un]
        x = [xt[i][0:C, :] for i in un]
        t = [t[i] + xt[i][C:2 * C, :] for i in un]
    t = [t[i] + _dg(_bf(t[i]), _bf(bd(x[i])), NN) for i in un]
    u = [_dg(_bf(t[i]), _bf(bd(w[i])), NN) for i in un]
    rbk = [_bf(jnp.where(incl2, g[i][C:2 * C, :], 0.0)) for i in un]
    uvbd = [jnp.concatenate([_bf(bd(u[i])), vbd[i]], axis=0) for i in un]
    y = [ars[i][C:2 * C, :] + _dg(rbk[i], uvbd[i], NN) for i in un]
    uv = [_bf(jnp.concatenate([u[i], blk(v, i)], axis=0)) for i in un]
    pc = [p_c[units[i][0]][:, LANES * units[i][1]:LANES * (units[i][1] + 1)] for i in un]
    bkh = [_bf(jnp.concatenate([blk(b_t, i), blk(k_t, i)], axis=0) * pc[i]) for i in un]
    s1 = [s0[i] * pc[i] + jnp.where(same_head, _dg(uv[i], bkh[i], TN), 0.0) for i in un]
    for i in un:
        s_s[i] = s1[i]

    y_all = jnp.concatenate(
        [jnp.concatenate(y[b * npair:(b + 1) * npair], axis=1) for b in seqs], axis=0)
    o = _wkv_post(y_all, r, k2, v, flat(zrg_ref), pv_ref, e)
    o_ref[...] = o.reshape(nb, C, RWKV_W)

    @pl.when(c == nc - 1)
    def _():
        for i in un:
            b, p = units[i]
            sout_ref[b, 2 * p] = s1[i][0:HEAD, 0:HEAD]
            sout_ref[b, 2 * p + 1] = s1[i][HEAD:2 * HEAD, HEAD:2 * HEAD]
        for b in seqs:
            for q, ref in enumerate((zr_ref, zk_ref, zv_ref)):
                nsh_ref[0, b:b + 1, RWKV_W * q:RWKV_W * (q + 1)] = ref[b, C - 1:C, :]
            nsh_ref[0, b:b + 1, SHIFT_MAIN:SHIFT_MAIN + LANES] = zl_ref[b, C - 1:C, :]


def _wkv_chunk(z, zl, pvec, mul, wd, wa, e, batch, seq, nb):
    C = WKV_CHUNK
    nc = seq // C
    full = lambda shp: pl.BlockSpec(shp, lambda b, c: (0,) * len(shp))
    col = lambda j: pl.BlockSpec((nb, C, RWKV_W), lambda b, c, j=j: (b, c, j))
    return pl.pallas_call(
        _wkv_chunk_kernel,
        out_shape=(jax.ShapeDtypeStruct((batch, seq, RWKV_W), BF16),
                   jax.ShapeDtypeStruct((batch, HEADS, HEAD, HEAD), F32),
                   jax.ShapeDtypeStruct((batch // nb, nb, SHIFT_MAIN + LANES), F32)),
        grid=(batch // nb, nc),
        in_specs=[col(0), col(1), col(2), col(3),
                  pl.BlockSpec((nb, C, LANES), lambda b, c: (b, c, 0)),
                  full(pvec.shape), full(mul.shape), full(wd.shape), full(wa.shape), full(e.shape)],
        out_specs=(pl.BlockSpec((nb, C, RWKV_W), lambda b, c: (b, c, 0)),
                   pl.BlockSpec((nb, HEADS, HEAD, HEAD), lambda b, c: (b, 0, 0, 0)),
                   pl.BlockSpec((1, nb, SHIFT_MAIN + LANES), lambda b, c: (b, 0, 0))),
        scratch_shapes=[pltpu.VMEM((nb * HEADS // 2, 2 * HEAD, 2 * HEAD), F32),
                        pltpu.VMEM((nb, SUBLANES, 3 * RWKV_W), F32),
                        pltpu.VMEM((nb, SUBLANES, LANES), F32)],
        compiler_params=pltpu.CompilerParams(
            dimension_semantics=("arbitrary", "arbitrary"), vmem_limit_bytes=VMEM_LIMIT),
        name="wkv_chunk",
    )(z, z, z, z, zl, pvec, mul, wd, wa, e)


def _wkv_step_kernel(zr_ref, zk_ref, zv_ref, zrg_ref, zl_ref, sh_ref, s0_ref, pv_ref,
                     mul_ref, wd_ref, wa_ref, e_ref, o_ref, sout_ref, nsh_ref,
                     at_s, drt_s, bt_s, kt_s, dt_s, vt_s, brt_s, krt_s, yt_s, keep_s):
    h = pl.program_id(0)
    nh = pl.num_programs(0)
    nseq = zr_ref.shape[0]

    @pl.when(h == 0)
    def _():
        e = e_ref[...]
        r, k2, v, av, bv, logd = _wkv_prep(
            zr_ref[...], zk_ref[...], zv_ref[...], zl_ref[...],
            sh_ref[:, 0:RWKV_W], sh_ref[:, RWKV_W:2 * RWKV_W], sh_ref[:, 2 * RWKV_W:3 * RWKV_W],
            sh_ref[:, SHIFT_MAIN:SHIFT_MAIN + LANES], pv_ref, mul_ref, wd_ref, wa_ref, e)
        nsh_ref[:, 0:RWKV_W] = zr_ref[...]
        nsh_ref[:, RWKV_W:2 * RWKV_W] = zk_ref[...]
        nsh_ref[:, 2 * RWKV_W:SHIFT_MAIN] = zv_ref[...]
        nsh_ref[:, SHIFT_MAIN:SHIFT_MAIN + LANES] = zl_ref[...]
        d = jnp.exp(logd)
        at_s[...] = av.T
        drt_s[...] = (d * r).T
        bt_s[...] = bv.T
        kt_s[...] = k2.T
        dt_s[...] = d.T
        vt_s[...] = v.T
        brt_s[...] = jnp.sum((bv * r).T.reshape(HEADS, HEAD, nseq), axis=1)
        krt_s[...] = jnp.sum((k2 * r).T.reshape(HEADS, HEAD, nseq), axis=1)
        keep_s[0] = r
        keep_s[1] = k2
        keep_s[2] = v

    for u in range(s0_ref.shape[0]):
        head = h * s0_ref.shape[0] + u
        base = pl.multiple_of(head * HEAD, HEAD)
        rows = pl.ds(base, HEAD)
        a_h, dr_h, b_h, k_h, d_h = at_s[rows, :], drt_s[rows, :], bt_s[rows, :], kt_s[rows, :], dt_s[rows, :]
        br_h = brt_s[pl.ds(head, 1), :]
        kr_h = krt_s[pl.ds(head, 1), :]

        def value_rows(g, carry, u=u, base=base, a_h=a_h, dr_h=dr_h, b_h=b_h, k_h=k_h, d_h=d_h,
                       br_h=br_h, kr_h=kr_h):
            off = pl.multiple_of(base + g * SUBLANES, SUBLANES)
            v8 = vt_s[pl.ds(off, SUBLANES), :]
            ys = []
            for j in range(SUBLANES):
                vi = g * SUBLANES + j
                s_v = s0_ref[u, vi]
                sa = jnp.sum(s_v * a_h, axis=0, keepdims=True)
                y0 = jnp.sum(s_v * dr_h, axis=0, keepdims=True)
                v_v = v8[j:j + 1, :]
                sout_ref[u, vi] = s_v * d_h + sa * b_h + v_v * k_h
                ys.append(y0 + sa * br_h + v_v * kr_h)
            yt_s[pl.ds(off, SUBLANES), :] = jnp.concatenate(ys, axis=0)
            return carry

        lax.fori_loop(0, HEAD // SUBLANES, value_rows, 0)

    @pl.when(h == nh - 1)
    def _():
        o_ref[...] = _wkv_post(yt_s[...].T, keep_s[0], keep_s[1], keep_s[2], zrg_ref[...], pv_ref,
                               e_ref[...])


def _wkv_step(z, zl, sh, s0t, pvec, mul, wd, wa, e):
    nseq = z.shape[0]
    full = lambda shp: pl.BlockSpec(shp, lambda i: (0,) * len(shp))
    col = lambda j: pl.BlockSpec((nseq, RWKV_W), lambda i, j=j: (0, j))
    st_block = (STEP_HEADS, HEAD, HEAD, nseq)
    wide = pltpu.VMEM((RWKV_W, nseq), F32)
    return pl.pallas_call(
        _wkv_step_kernel,
        out_shape=(jax.ShapeDtypeStruct((nseq, RWKV_W), BF16),
                   jax.ShapeDtypeStruct(s0t.shape, F32),
                   jax.ShapeDtypeStruct(sh.shape, F32)),
        grid=(HEADS // STEP_HEADS,),
        in_specs=[col(0), col(1), col(2), col(3),
                  full(zl.shape), full(sh.shape),
                  pl.BlockSpec(st_block, lambda i: (i, 0, 0, 0)),
                  full(pvec.shape), full(mul.shape), full(wd.shape), full(wa.shape), full(e.shape)],
        out_specs=(full((nseq, RWKV_W)),
                   pl.BlockSpec(st_block, lambda i: (i, 0, 0, 0)),
                   full(sh.shape)),
        scratch_shapes=[wide] * 6 + [pltpu.VMEM((HEADS, nseq), F32)] * 2
                       + [wide, pltpu.VMEM((3, nseq, RWKV_W), F32)],
        compiler_params=pltpu.CompilerParams(
            dimension_semantics=("arbitrary",), vmem_limit_bytes=VMEM_LIMIT),
        name="wkv_step",
    )(z, z, z, z, zl, sh, s0t, pvec, mul, wd, wa, e)


_CW0, _CW1, _CW2, _CW3, _CB, _GXB, _GAB, _LAM = range(8)


def _lru_gates(xc, lp_ref, wg_ref):
    xb = _bf(xc)
    ngroups = wg_ref.shape[0]
    gs = [_dg(xb[:, LANES * g:LANES * (g + 1)], wg_ref[g], NN) for g in range(ngroups)]
    gx_pre = jnp.concatenate([gs[g][:, 0:LANES] for g in range(ngroups)], axis=1)
    ga_pre = jnp.concatenate([gs[g][:, LANES:2 * LANES] for g in range(ngroups)], axis=1)
    gx = _sigmoid(gx_pre + _prow(lp_ref, _GXB))
    ga = _sigmoid(ga_pre + _prow(lp_ref, _GAB))
    log_a = -LRU_C * ga * _softplus(-_prow(lp_ref, _LAM))
    a = jnp.exp(log_a)
    mult = jnp.sqrt((1.0 - a) * (1.0 + a))
    return a, mult * gx * xc


def _lru_scan_rows(a, b, zg, hc):
    row8 = lax.broadcasted_iota(jnp.int32, (SUBLANES, 1), 0)
    hs = []
    for i in range(a.shape[0] // SUBLANES):
        a8 = a[SUBLANES * i:SUBLANES * (i + 1), :]
        b8 = b[SUBLANES * i:SUBLANES * (i + 1), :]
        for s in (1, 2, 4):
            keep = row8 >= s
            b8 = jnp.where(keep, a8 * pltpu.roll(b8, s, 0) + b8, b8)
            a8 = jnp.where(keep, a8 * pltpu.roll(a8, s, 0), a8)
        hb = b8 + a8 * hc
        hs.append(hb)
        hc = jnp.broadcast_to(hb[SUBLANES - 1:SUBLANES, :], hb.shape)
    return _bf(jnp.concatenate(hs, axis=0) * (zg * _sigmoid(zg))), hc


def _lru_step_kernel(zx_ref, zg_ref, conv_ref, h0_ref, lp_ref, wg_ref, o_ref, hnew_ref, cnew_ref):
    zx = zx_ref[...]
    for j in range(CONV_W - 2):
        cnew_ref[j] = conv_ref[j + 1]
    cnew_ref[CONV_W - 2] = zx
    xc = _prow(lp_ref, _CW3) * zx + _prow(lp_ref, _CB)
    for j in range(CONV_W - 1):
        xc = xc + _prow(lp_ref, j) * conv_ref[j]
    a, b = _lru_gates(xc, lp_ref, wg_ref)
    h = a * h0_ref[...] + b
    hnew_ref[...] = h
    zg = zg_ref[...]
    o_ref[...] = _bf(h * (zg * _sigmoid(zg)))


def _lru_step(z_main, conv, h0, lp, wg):
    nb = z_main.shape[0]
    full = lambda shp: pl.BlockSpec(shp, lambda i: (0,) * len(shp))
    col = lambda j: pl.BlockSpec((nb, LRU_W), lambda i, j=j: (0, j))
    return pl.pallas_call(
        _lru_step_kernel,
        out_shape=(jax.ShapeDtypeStruct((nb, LRU_W), BF16), jax.ShapeDtypeStruct((nb, LRU_W), F32),
                   jax.ShapeDtypeStruct(conv.shape, F32)),
        grid=(1,),
        in_specs=[col(4), col(5), full(conv.shape), full(h0.shape), full(lp.shape), full(wg.shape)],
        out_specs=(full((nb, LRU_W)), full((nb, LRU_W)), full(conv.shape)),
        compiler_params=pltpu.CompilerParams(
            dimension_semantics=("arbitrary",), vmem_limit_bytes=VMEM_LIMIT),
        name="lru_step",
    )(z_main, z_main, conv, h0, lp, wg)


def _project(x, o_r, o_g, m_r, m_g, wr_ref, wg_ref, wo_ref, fg_ref, final):
    y_r = _dg(o_r, wr_ref[...], NN)
    y_g = _dg(o_g, wg_ref[...], NN)
    merged = _sigmoid(m_r) * y_r + _sigmoid(m_g) * y_g
    out = x + _dg(_bf(merged), wo_ref[...], NN)
    return _rms(out, fg_ref[...]) if final else out


def _outproj_lru_kernel(x_ref, or_ref, mr_ref, mg_ref, zx_ref, zg_ref, xs_ref, ors_ref, ogs_ref,
                        mrs_ref, mgs_ref, lp_ref, wgate_ref, wr_ref, wg_ref, wo_ref, fg_ref,
                        out_ref, hlast_ref, cnew_ref, outs_ref, og_s, mg_s, xb_s, hc_s, *, final,
                        tiles_per_seq, n_tiles):
    i = pl.program_id(0)
    n = n_tiles
    tm, d = x_ref.shape

    @pl.when(i == 0)
    def _():
        xb_s[...] = jnp.zeros_like(xb_s)
        hc_s[...] = jnp.zeros_like(hc_s)
        outs_ref[...] = _project(xs_ref[...], ors_ref[...], ogs_ref[...], mrs_ref[...], mgs_ref[...],
                                 wr_ref, wg_ref, wo_ref, fg_ref, final)

    def stages(lru, branch, outp):
        if branch:
            og_prev = og_s[...]
        if outp:
            mg_prev = mg_s[...]
        if lru:
            t = lax.rem(i, tiles_per_seq)
            first = t == 0
            xb_s[0:SUBLANES, :] = jnp.where(first, 0.0, xb_s[0:SUBLANES, :])
            xb_s[SUBLANES:SUBLANES + tm, :] = zx_ref[...]

        outs, a_parts, b_parts = [], [], []
        for c in range(FUSE_PIECES):
            if outp:
                cs = slice(c * d // FUSE_PIECES, (c + 1) * d // FUSE_PIECES)
                outs.append(x_ref[:, cs] + _dg(mg_prev, wo_ref[:, cs], NN))
            if lru:
                r0, r1 = SUBLANES + c * tm // FUSE_PIECES, SUBLANES + (c + 1) * tm // FUSE_PIECES
                xc = _prow(lp_ref, _CW3) * xb_s[r0:r1, :] + _prow(lp_ref, _CB)
                for j in range(1, CONV_W):
                    xc = xc + _prow(lp_ref, CONV_W - 1 - j) * xb_s[r0 - j:r1 - j, :]
                a_c, b_c = _lru_gates(xc, lp_ref, wgate_ref)
                a_parts.append(a_c)
                b_parts.append(b_c)
        if outp:
            out = jnp.concatenate(outs, axis=1)
            out_ref[...] = _rms(out, fg_ref[...]) if final else out
        if lru:
            xb_s[0:SUBLANES, :] = xb_s[tm:tm + SUBLANES, :]
            o_g, hc = _lru_scan_rows(jnp.concatenate(a_parts, axis=0), jnp.concatenate(b_parts, axis=0),
                                     zg_ref[...], jnp.where(first, 0.0, hc_s[...]))
            hc_s[...] = hc
            og_s[...] = o_g
        if branch:
            y_r = _dg(or_ref[...], wr_ref[...], NN)
            y_g = _dg(og_prev, wg_ref[...], NN)
            mg_s[...] = _bf(_sigmoid(mr_ref[...]) * y_r + _sigmoid(mg_ref[...]) * y_g)
        if lru:
            @pl.when(t == tiles_per_seq - 1)
            def _():
                seq = i // tiles_per_seq
                hlast_ref[pl.ds(seq, 1), :] = hc[0:1, :]
                for j in range(CONV_W - 1):
                    row = tm - (CONV_W - 1) + j
                    cnew_ref[j, pl.ds(seq, 1), :] = zx_ref[row:row + 1, :]

    active = lambda step, s: 0 <= step - s < n
    holds = lambda s, flag: ((i >= s) & (i < n + s)) == flag
    for combo in sorted({tuple(active(step, s) for s in range(3)) for step in range(n + 2)}):
        lru, branch, outp = combo
        pl.when(holds(0, lru) & holds(1, branch) & holds(2, outp))(
            functools.partial(stages, lru, branch, outp))


def _outproj_lru(x, o_r, z, xs, o_rs, o_gs, zs, lp, wgate, w_r, w_g, w_o, fg, tm, seq, final):
    m, d = x.shape
    ms = xs.shape[0]
    n = m // tm
    tiles_per_seq = seq // tm
    const = lambda shp: pl.BlockSpec(shp, lambda i: (0,) * len(shp), pipeline_mode=pl.Buffered(1))
    back = lambda i, k: jnp.clip(i - k, 0, n - 1)
    return pl.pallas_call(
        functools.partial(_outproj_lru_kernel, final=final, tiles_per_seq=tiles_per_seq, n_tiles=n),
        out_shape=(jax.ShapeDtypeStruct((m, d), F32),
                   jax.ShapeDtypeStruct((m // seq, LRU_W), F32),
                   jax.ShapeDtypeStruct((CONV_W - 1, m // seq, LRU_W), F32),
                   jax.ShapeDtypeStruct((ms, d), F32)),
        grid=(n + 2,),
        in_specs=[
            pl.BlockSpec((tm, d), lambda i: (back(i, 2), 0)),
            pl.BlockSpec((tm, RWKV_W), lambda i: (back(i, 1), 0)),
            pl.BlockSpec((tm, d), lambda i: (back(i, 1), 3)),
            pl.BlockSpec((tm, d), lambda i: (back(i, 1), 4)),
            pl.BlockSpec((tm, LRU_W), lambda i: (back(i, 0), 4)),
            pl.BlockSpec((tm, LRU_W), lambda i: (back(i, 0), 5)),
            const(xs.shape), const(o_rs.shape), const(o_gs.shape),
            pl.BlockSpec((ms, d), lambda i: (0, 3), pipeline_mode=pl.Buffered(1)),
            pl.BlockSpec((ms, d), lambda i: (0, 4), pipeline_mode=pl.Buffered(1)),
            const(lp.shape), const(wgate.shape), const(w_r.shape), const(w_g.shape), const(w_o.shape),
            const(fg.shape),
        ],
        out_specs=(pl.BlockSpec((tm, d), lambda i: (back(i, 2), 0)),
                   pl.BlockSpec((m // seq, LRU_W), lambda i: (0, 0)),
                   pl.BlockSpec((CONV_W - 1, m // seq, LRU_W), lambda i: (0, 0, 0)),
                   pl.BlockSpec((ms, d), lambda i: (0, 0))),
        scratch_shapes=[pltpu.VMEM((tm, LRU_W), BF16),
                        pltpu.VMEM((tm, d), BF16),
                        pltpu.VMEM((SUBLANES + tm, LRU_W), F32),
                        pltpu.VMEM((SUBLANES, LRU_W), F32)],
        compiler_params=pltpu.CompilerParams(
            dimension_semantics=("arbitrary",), vmem_limit_bytes=VMEM_LIMIT),
        name="outproj_lru",
    )(x, o_r, z, z, z, z, xs, o_rs, o_gs, zs, zs, lp, wgate, w_r, w_g, w_o, fg)


def _row_tile(m, want):
    t = min(m, want)
    assert m % t == 0, (m, t)
    return t


def _pack_params_kernel(mu_ref, w0_ref, a0_ref, kk_ref, ka_ref, lng_ref, lnb_ref, rk_ref, wdu_ref,
                        wau_ref, cw_ref, cb_ref, gxb_ref, gab_ref, lam_ref, gxw_ref, gaw_ref,
                        pvec_ref, mul_ref, wd_ref, wa_ref, e_ref, lp_ref, wg_ref):
    pvec_ref[...] = jnp.zeros_like(pvec_ref)
    for i in range(3):
        pvec_ref[_MU_R + i:_MU_R + i + 1, :] = mu_ref[:, RWKV_W * i:RWKV_W * (i + 1)]
    for row, ref in ((_W0, w0_ref), (_A0, a0_ref), (_KK, kk_ref), (_KA, ka_ref),
                     (_LNG, lng_ref), (_LNB, lnb_ref)):
        pvec_ref[row:row + 1, :] = ref[...]
    for h in range(HEADS):
        pvec_ref[_RK:_RK + 1, HEAD * h:HEAD * (h + 1)] = rk_ref[h:h + 1, :]
    mul_ref[...] = jnp.broadcast_to(mu_ref[:, 3 * RWKV_W:3 * RWKV_W + 2 * LORA], mul_ref.shape)

    zeros = jnp.zeros((LORA, RWKV_W), BF16)
    wd_ref[0:LORA, :] = _bf(wdu_ref[...])
    wd_ref[LORA:2 * LORA, :] = zeros
    wa_ref[0:LORA, :] = zeros
    wa_ref[LORA:2 * LORA, :] = _bf(wau_ref[...])

    ri = lax.broadcasted_iota(jnp.int32, (LANES, LANES), 0)
    ci = lax.broadcasted_iota(jnp.int32, (LANES, LANES), 1)
    e_ref[...] = jnp.where((ri < HEAD) == (ci < HEAD), 1.0, 0.0).astype(BF16)

    lp_ref[_CW0:_CW0 + CONV_W, :] = cw_ref[...]
    for row, ref in ((_CB, cb_ref), (_GXB, gxb_ref), (_GAB, gab_ref), (_LAM, lam_ref)):
        lp_ref[row:row + 1, :] = ref[...]

    blk = LRU_W // LRU_BLOCKS
    z = jnp.zeros((blk, blk), F32)
    for g in range(LRU_BLOCKS // 2):
        top = jnp.concatenate([gxw_ref[2 * g], z, gaw_ref[2 * g], z], axis=1)
        bot = jnp.concatenate([z, gxw_ref[2 * g + 1], z, gaw_ref[2 * g + 1]], axis=1)
        wg_ref[g] = _bf(jnp.concatenate([top, bot], axis=0))


def _pack_params(l, rwkv_mu, w_decay0, w_decay_up, w_iclr0, w_iclr_up, k_k, k_a, r_k, ln_x_g,
                 ln_x_b, conv_w, conv_b, lru_gx_w, lru_gx_b, lru_ga_w, lru_ga_b, lru_lambda):
    blk = LRU_W // LRU_BLOCKS
    assert 2 * blk == LANES and 2 * LORA == LANES
    row = lambda a: pl.BlockSpec((1, a.shape[-1]), lambda i: (l, 0))
    mat = lambda a: pl.BlockSpec((None,) + a.shape[1:], lambda i: (l,) + (0,) * (a.ndim - 1))
    full = lambda shp: pl.BlockSpec(shp, lambda i: (0,) * len(shp))
    rows = (rwkv_mu, w_decay0, w_iclr0, k_k, k_a, ln_x_g, ln_x_b)
    out_shapes = ((16, RWKV_W, F32), (SUBLANES, LANES, F32), (LANES, RWKV_W, BF16), (LANES, RWKV_W, BF16),
                  (LANES, LANES, BF16), (SUBLANES, LRU_W, F32))
    outs = tuple(jax.ShapeDtypeStruct(s[:2], s[2]) for s in out_shapes)
    outs += (jax.ShapeDtypeStruct((LRU_BLOCKS // 2, LANES, 2 * LANES), BF16),)
    return pl.pallas_call(
        _pack_params_kernel,
        out_shape=outs,
        grid=(1,),
        in_specs=[row(a) for a in rows] + [mat(r_k), mat(w_decay_up), mat(w_iclr_up), mat(conv_w),
                                          row(conv_b), row(lru_gx_b), row(lru_ga_b), row(lru_lambda),
                                          mat(lru_gx_w), mat(lru_ga_w)],
        out_specs=tuple(full(o.shape) for o in outs),
        compiler_params=pltpu.CompilerParams(
            dimension_semantics=("arbitrary",), vmem_limit_bytes=VMEM_LIMIT),
        name="pack_params",
    )(*rows, r_k, w_decay_up, w_iclr_up, conv_w, conv_b, lru_gx_b, lru_ga_b, lru_lambda, lru_gx_w,
      lru_ga_w)


def kernel(x_prompt, x_sample, state_shift, state_wkv, state_conv, state_lru, norm_g, w_in, rwkv_mu,
           w_decay0, w_decay_up, w_iclr0, w_iclr_up, k_k, k_a, r_k, ln_x_g, ln_x_b, w_out_rwkv,
           conv_w, conv_b, lru_gx_w, lru_gx_b, lru_ga_w, lru_ga_b, lru_lambda, w_out_lru, w_out,
           final_norm_g):
    bp, seq, d = x_prompt.shape
    bs = x_sample.shape[0]
    assert x_sample.shape[1] == 1 and seq % WKV_CHUNK == 0
    depth = w_in.shape[0]
    xp = x_prompt.reshape(bp * seq, d)
    xs = x_sample.reshape(bs, d)
    fg = final_norm_g.reshape(1, d)
    outs = [[] for _ in range(8)]
    for l in range(depth):
        pvec, mul, wd, wa, e, lp, wg = _pack_params(
            l, rwkv_mu, w_decay0, w_decay_up, w_iclr0, w_iclr_up, k_k, k_a, r_k, ln_x_g, ln_x_b,
            conv_w, conv_b, lru_gx_w, lru_gx_b, lru_ga_w, lru_ga_b, lru_lambda)
        g = norm_g[l].reshape(1, d)
        rec = (pvec, mul, wd, wa, e)
        w, wl, zs, zls, w_r, w_g, w_o = _inproj_head(xs, g, w_in, w_out_rwkv, w_out_lru, w_out, l, INPROJ_TN)
        zp, zlp = _inproj(xp, g, w, wl, _row_tile(bp * seq, 1024), INPROJ_WIDE_TN)

        s0t = jnp.transpose(state_wkv[l], (1, 2, 3, 0))
        o_rs, s_new, sh_new = _wkv_step(zs, zls, state_shift[l], s0t, *rec)
        conv = jnp.transpose(state_conv[l], (1, 0, 2))
        o_gs, h_new, conv_new = _lru_step(zs, conv, state_lru[l], lp, wg)
        outs[4].append(sh_new)
        outs[5].append(jnp.transpose(s_new, (3, 0, 1, 2)))
        outs[6].append(jnp.transpose(conv_new, (1, 0, 2)))
        outs[7].append(h_new)

        zp3 = zp.reshape(bp, seq, -1)
        zlp3 = zlp.reshape(bp, seq, LANES)
        nb = max(n for n in (4, 2, 1) if bp % n == 0)
        o_r, s_new, sh_last = _wkv_chunk(zp3, zlp3, *rec, bp, seq, nb)
        o_r = o_r.reshape(bp * seq, RWKV_W)
        last = l == depth - 1
        xp, h_last, conv_last, xs = _outproj_lru(xp, o_r, zp, xs, o_rs, o_gs, zs, lp, wg, w_r, w_g, w_o,
                                                 fg, _row_tile(seq, 256), seq, last)
        outs[0].append(sh_last.reshape(bp, -1))
        outs[1].append(s_new)
        outs[2].append(jnp.transpose(conv_last, (1, 0, 2)))
        outs[3].append(h_last)

    return (xp.reshape(bp, seq, d), xs.reshape(bs, 1, d)) + tuple(jnp.stack(o) for o in outs)
```

```python
import functools

import jax
import jax.numpy as jnp
from jax import lax
from jax.experimental import pallas as pl
from jax.experimental.pallas import tpu as pltpu

F32 = jnp.float32
BF16 = jnp.bfloat16

HEADS = 16
HEAD = 64
RWKV_W = HEADS * HEAD
LORA = 64
LRU_W = 1024
LRU_BLOCKS = 16
CONV_W = 4
LRU_C = 8.0
RMS_EPS = 1e-6
GN_EPS = 1e-5 * HEAD
DECAY_SCALE = 0.6065306597126334

LANES = 128
SUBLANES = 8
WKV_CHUNK = 64
STEP_HEADS = 2
VMEM_LIMIT = 60 * 1024 * 1024

NN = (((1,), (0,)), ((), ()))
NT = (((1,), (1,)), ((), ()))
TN = (((0,), (0,)), ((), ()))


def _bf(x):
    return x.astype(BF16)


def _dg(a, b, dn):
    return lax.dot_general(a, b, dn, preferred_element_type=F32)


def _softplus(x):
    return jnp.maximum(x, 0.0) + jnp.log1p(jnp.exp(-jnp.abs(x)))


def _sigmoid(x):
    return 1.0 / (1.0 + jnp.exp(-x))


def _segsum(x, e):
    rows, n = x.shape[0], x.shape[1] // LANES
    stacked = jnp.concatenate([x[:, LANES * j:LANES * (j + 1)] for j in range(n)], axis=0)
    s = _dg(_bf(stacked), e, NN)
    return jnp.concatenate([s[rows * j:rows * (j + 1), :] for j in range(n)], axis=1)


def _rms(x, g):
    return x * lax.rsqrt(jnp.mean(x * x, axis=-1, keepdims=True) + RMS_EPS) * g


SHIFT_MAIN = 3 * RWKV_W
LORA_COL = 10 * RWKV_W
LORA_BLOCK = LORA_COL // LANES
FUSE_PIECES = 8
INPROJ_TN = 1024
INPROJ_WIDE_TN = 2048
NORM_ROWS = 128
OUT_W_STEPS = 8


def _inproj_head_kernel(xs_ref, g_ref, w_ref, wt_ref, wlo_ref, wr_ref, wg_ref, wo_ref, wb_ref, wl_ref,
                        zs_ref, zls_ref, wrb_ref, wgb_ref, wob_ref, hs_ref):
    j = pl.program_id(0)
    tn = w_ref.shape[-1]

    @pl.when(j == 0)
    def _():
        hs_ref[...] = _bf(_rms(xs_ref[...], g_ref[...]))
        wl_ref[...] = _bf(wlo_ref[...])
        zls_ref[...] = _dg(hs_ref[...], wl_ref[...], NN)

    @pl.when(j < SHIFT_MAIN // tn)
    def _():
        wb_ref[...] = _bf(w_ref[...])

    @pl.when(j >= SHIFT_MAIN // tn)
    def _():
        wb_ref[:, 0:tn - 2 * LORA] = _bf(w_ref[:, 2 * LORA:tn])
        wb_ref[:, tn - 2 * LORA:tn] = _bf(wt_ref[...])

    zs_ref[...] = _dg(hs_ref[...], wb_ref[...], NN)

    @pl.when(j < OUT_W_STEPS)
    def _():
        wrb_ref[...] = _bf(wr_ref[...])
        wgb_ref[...] = _bf(wg_ref[...])
        wob_ref[...] = _bf(wo_ref[...])


def _inproj_head(xs, g, w_in, w_out_rwkv, w_out_lru, w_out, layer, tn):
    ms, d = xs.shape
    n = w_in.shape[-1]
    assert n == LORA_COL + 2 * LORA and SHIFT_MAIN % tn == 0 and 2 * LORA == LANES
    nj = LORA_COL // tn
    assert nj >= OUT_W_STEPS
    lanes_per_tile = tn // LANES
    one = lambda shp, imap: pl.BlockSpec(shp, imap, pipeline_mode=pl.Buffered(1))
    rows = lambda a: a.shape[1] // OUT_W_STEPS
    step = lambda j: jnp.minimum(j, OUT_W_STEPS - 1)
    w_outs = (w_out_rwkv, w_out_lru, w_out)
    return pl.pallas_call(
        _inproj_head_kernel,
        out_shape=(jax.ShapeDtypeStruct((d, LORA_COL), BF16), jax.ShapeDtypeStruct((d, LANES), BF16),
                   jax.ShapeDtypeStruct((ms, LORA_COL), F32), jax.ShapeDtypeStruct((ms, LANES), F32))
        + tuple(jax.ShapeDtypeStruct(a.shape[1:], BF16) for a in w_outs),
        grid=(nj,),
        in_specs=[
            one((ms, d), lambda j: (0, 0)),
            one((1, d), lambda j: (0, 0)),
            pl.BlockSpec((None, d, tn), lambda j: (layer, 0, j)),
            pl.BlockSpec((None, d, LANES), lambda j: (layer, 0, (j + 1) * lanes_per_tile)),
            one((None, d, LANES), lambda j: (layer, 0, SHIFT_MAIN // LANES)),
        ] + [pl.BlockSpec((None, rows(a), a.shape[2]), lambda j: (layer, step(j), 0)) for a in w_outs],
        out_specs=(
            pl.BlockSpec((d, tn), lambda j: (0, j)),
            pl.BlockSpec((d, LANES), lambda j: (0, 0)),
            pl.BlockSpec((ms, tn), lambda j: (0, j)),
            pl.BlockSpec((ms, LANES), lambda j: (0, 0)),
        ) + tuple(pl.BlockSpec((rows(a), a.shape[2]), lambda j: (step(j), 0)) for a in w_outs),
        scratch_shapes=[pltpu.VMEM((ms, d), BF16)],
        compiler_params=pltpu.CompilerParams(
            dimension_semantics=("arbitrary",), vmem_limit_bytes=VMEM_LIMIT),
        name="inproj_head",
    )(xs, g, w_in, w_in, w_in, *w_outs)


def _inproj_kernel(*refs):
    x_refs, (g_ref, w_ref, wl_ref, z_ref, zl_ref, h_ref) = refs[:-6], refs[-6:]

    @pl.when(pl.program_id(1) == 0)
    def _():
        rows_per_ref = x_refs[0].shape[0]
        for p, x_ref in enumerate(x_refs):
            for r in range(0, rows_per_ref, NORM_ROWS):
                h_ref[pl.ds(p * rows_per_ref + r, NORM_ROWS), :] = _bf(
                    _rms(x_ref[r:r + NORM_ROWS, :], g_ref[...]))
        zl_ref[...] = _dg(h_ref[...], wl_ref[...], NN)

    z_ref[...] = _dg(h_ref[...], w_ref[...], NN)


def _inproj(x, g, w, wl, tm, tn):
    m, d = x.shape
    ni, nj = m // tm, LORA_COL // tn
    pieces = nj - 1 if nj > 1 and tm % ((nj - 1) * NORM_ROWS) == 0 else 1
    x_piece = lambda p: pl.BlockSpec(
        (tm // pieces, d), lambda i, j: (jnp.minimum(i + (j > p), ni - 1) * pieces + p, 0))
    return pl.pallas_call(
        _inproj_kernel,
        out_shape=(jax.ShapeDtypeStruct((m, LORA_COL), F32), jax.ShapeDtypeStruct((m, LANES), F32)),
        grid=(ni, nj),
        in_specs=[x_piece(p) for p in range(pieces)] + [
            pl.BlockSpec((1, d), lambda i, j: (0, 0), pipeline_mode=pl.Buffered(1)),
            pl.BlockSpec((d, tn), lambda i, j: (0, j)),
            pl.BlockSpec((d, LANES), lambda i, j: (0, 0), pipeline_mode=pl.Buffered(1)),
        ],
        out_specs=(
            pl.BlockSpec((tm, tn), lambda i, j: (i, j)),
            pl.BlockSpec((tm, LANES), lambda i, j: (i, 0)),
        ),
        scratch_shapes=[pltpu.VMEM((tm, d), BF16)],
        compiler_params=pltpu.CompilerParams(
            dimension_semantics=("arbitrary", "arbitrary"), vmem_limit_bytes=VMEM_LIMIT),
        name="inproj",
    )(*([x] * pieces), g, w, wl)


_MU_R, _MU_K, _MU_V, _W0, _A0, _KK, _KA, _RK, _LNG, _LNB = range(10)


def _prow(pv_ref, i):
    return pv_ref[i:i + 1, :]


def _wkv_prep(zr, zk, zv, zl, pr, pk, pv, pl_, pv_ref, mul_ref, wd_ref, wa_ref, e):
    r = zr + _prow(pv_ref, _MU_R) * (pr - zr)
    k = zk + _prow(pv_ref, _MU_K) * (pk - zk)
    v = zv + _prow(pv_ref, _MU_V) * (pv - zv)
    lo = zl + mul_ref[0:1, :] * (pl_ - zl)
    lw = _dg(_bf(jnp.tanh(lo)), wd_ref[...], NN)
    la = _dg(_bf(lo), wa_ref[...], NN)
    logd = -DECAY_SCALE * _sigmoid(_prow(pv_ref, _W0) + lw)
    a = _sigmoid(_prow(pv_ref, _A0) + la)
    kk = k * _prow(pv_ref, _KK)
    kk = kk * lax.rsqrt(jnp.maximum(_segsum(kk * kk, e), 1e-24))
    k2 = k * (1.0 + (a - 1.0) * _prow(pv_ref, _KA))
    return r, k2, v, -kk, kk * a, logd


def _wkv_bonus_gate(r, k2, v, zrg, pv_ref, e):
    return _segsum(r * k2 * _prow(pv_ref, _RK), e) * v, zrg * _sigmoid(zrg)


def _wkv_norm_gate(y, bonus_v, gate, pv_ref, e):
    mu = _segsum(y, e) * (1.0 / HEAD)
    yc = y - mu
    var = _segsum(yc * yc, e) * (1.0 / HEAD)
    yn = yc * lax.rsqrt(var + GN_EPS) * _prow(pv_ref, _LNG) + _prow(pv_ref, _LNB)
    return _bf((yn + bonus_v) * gate)


def _wkv_post(y, r, k2, v, zrg, pv_ref, e):
    bonus_v, gate = _wkv_bonus_gate(r, k2, v, zrg, pv_ref, e)
    return _wkv_norm_gate(y, bonus_v, gate, pv_ref, e)


def _wkv_chunk_kernel(zr_ref, zk_ref, zv_ref, zrg_ref, zl_ref, pv_ref, mul_ref, wd_ref, wa_ref,
                      e_ref, o_ref, sout_ref, nsh_ref, s_s, prev_s, prevl_s):
    c = pl.program_id(1)
    nc = pl.num_programs(1)
    C = WKV_CHUNK
    assert C == HEAD and 2 * HEAD == LANES
    nb = zr_ref.shape[0]
    rows_all = nb * C
    seqs = range(nb)

    @pl.when(c == 0)
    def _():
        s_s[...] = jnp.zeros_like(s_s)
        prev_s[...] = jnp.zeros_like(prev_s)
        prevl_s[...] = jnp.zeros_like(prevl_s)

    first = lax.broadcasted_iota(jnp.int32, (SUBLANES, 1), 0) == 0

    def shifted(z, prev_ref, lanes):
        rolled = pltpu.roll(z, 1, 0)
        pieces = []
        for b in seqs:
            head = jnp.where(first, prev_ref[b, 0:1, lanes], rolled[b * C:b * C + SUBLANES, :])
            pieces += [head, rolled[b * C + SUBLANES:(b + 1) * C, :]]
        return jnp.concatenate(pieces, axis=0)

    def flat(ref):
        return ref[...].reshape(rows_all, ref.shape[-1])

    zr, zk, zv, zl = flat(zr_ref), flat(zk_ref), flat(zv_ref), flat(zl_ref)
    seg = [slice(RWKV_W * i, RWKV_W * (i + 1)) for i in range(3)]
    pr = shifted(zr, prev_s, seg[0])
    pk = shifted(zk, prev_s, seg[1])
    pv = shifted(zv, prev_s, seg[2])
    pl_ = shifted(zl, prevl_s, slice(0, LANES))
    for b in seqs:
        last = slice(b * C + C - 1, b * C + C)
        prev_s[b, 0:1, seg[0]] = zr[last, :]
        prev_s[b, 0:1, seg[1]] = zk[last, :]
        prev_s[b, 0:1, seg[2]] = zv[last, :]
        prevl_s[b, 0:1, :] = zl[last, :]

    e = e_ref[...]
    r, k2, v, av, bv, logd = _wkv_prep(zr, zk, zv, zl, pr, pk, pv, pl_, pv_ref, mul_ref,
                                       wd_ref, wa_ref, e)

    ti = lax.broadcasted_iota(jnp.int32, (rows_all, rows_all), 0)
    tj = lax.broadcasted_iota(jnp.int32, (rows_all, rows_all), 1)
    tri = jnp.where((ti >= tj) & ((ti & -C) == (tj & -C)), 1.0, 0.0).astype(BF16)
    d_hi = _bf(logd)
    d_r1 = logd - d_hi.astype(F32)
    d_mid = _bf(d_r1)
    d_lo = _bf(d_r1 - d_mid.astype(F32))
    cum = _dg(jnp.concatenate([tri, tri, tri], axis=1), jnp.concatenate([d_hi, d_mid, d_lo], axis=0), NN)
    e_in = jnp.exp(cum)
    e_neg = jnp.exp(-cum)
    a_t = av * jnp.exp(cum - logd)
    r_t = r * e_in
    k_t = k2 * e_neg
    b_t = bv * e_neg
    p_c = [jnp.exp(cum[b * C + C - 1:b * C + C, :]) for b in seqs]

    lane = lax.broadcasted_iota(jnp.int32, (C, LANES), 1)
    trow = lax.broadcasted_iota(jnp.int32, (C, LANES), 0)
    lo = lane < HEAD
    s_in = lane & (HEAD - 1)
    strict = s_in < trow
    incl2 = ((lax.broadcasted_iota(jnp.int32, (C, 2 * LANES), 1) & (HEAD - 1))
             <= lax.broadcasted_iota(jnp.int32, (C, 2 * LANES), 0))
    eye2 = jnp.where(s_in == trow, 1.0, 0.0).astype(F32)
    vrow = lax.broadcasted_iota(jnp.int32, (2 * HEAD, LANES), 0)
    klane = lax.broadcasted_iota(jnp.int32, (2 * HEAD, LANES), 1)
    same_head = (vrow < HEAD) == (klane < HEAD)

    def bd(x):
        z = jnp.zeros_like(x)
        return jnp.concatenate([jnp.where(lo, x, z), jnp.where(lo, z, x)], axis=0)

    npair = HEADS // 2
    units = [(b, p) for b in seqs for p in range(npair)]
    un = range(len(units))
    blk = lambda arr, i: arr[units[i][0] * C:(units[i][0] + 1) * C, LANES * units[i][1]:LANES * (units[i][1] + 1)]
    ar = [_bf(jnp.concatenate([blk(a_t, i), blk(r_t, i)], axis=0)) for i in un]
    bk = [_bf(jnp.concatenate([bd(blk(b_t, i)), bd(blk(k_t, i))], axis=0)) for i in un]
    g = [_dg(ar[i], bk[i], NT) for i in un]
    s0 = [s_s[i] for i in un]
    ars = [_dg(ar[i], _bf(s0[i]), NT) for i in un]
    vbd = [_bf(bd(blk(v, i))) for i in un]
    x = [jnp.where(strict, g[i][0:C, 0:LANES], 0.0) for i in un]
    ak = [jnp.where(strict, g[i][0:C, LANES:2 * LANES], 0.0) for i in un]
    w = [ars[i][0:C, :] + _dg(_bf(ak[i]), vbd[i], NN) for i in un]
    t = [eye2 + x[i] for i in un]
    x = [_dg(_bf(x[i]), _bf(bd(x[i])), NN) for i in un]
    for _ in range(C.bit_length() - 3):
        xt = [_dg(_bf(jnp.concatenate([x[i], t[i]], axis=0)), _bf(bd(x[i])), NN) for i in un]
        x = [xt[i][0:C, :] for i in un]
        t = [t[i] + xt[i][C:2 * C, :] for i in un]
    t = [t[i] + _dg(_bf(t[i]), _bf(bd(x[i])), NN) for i in un]
    u = [_dg(_bf(t[i]), _bf(bd(w[i])), NN) for i in un]
    rbk = [_bf(jnp.where(incl2, g[i][C:2 * C, :], 0.0)) for i in un]
    uvbd = [jnp.concatenate([_bf(bd(u[i])), vbd[i]], axis=0) for i in un]
    y = [ars[i][C:2 * C, :] + _dg(rbk[i], uvbd[i], NN) for i in un]
    uv = [_bf(jnp.concatenate([u[i], blk(v, i)], axis=0)) for i in un]
    pc = [p_c[units[i][0]][:, LANES * units[i][1]:LANES * (units[i][1] + 1)] for i in un]
    bkh = [_bf(jnp.concatenate([blk(b_t, i), blk(k_t, i)], axis=0) * pc[i]) for i in un]
    s1 = [s0[i] * pc[i] + jnp.where(same_head, _dg(uv[i], bkh[i], TN), 0.0) for i in un]
    for i in un:
        s_s[i] = s1[i]

    y_all = jnp.concatenate(
        [jnp.concatenate(y[b * npair:(b + 1) * npair], axis=1) for b in seqs], axis=0)
    o = _wkv_post(y_all, r, k2, v, flat(zrg_ref), pv_ref, e)
    o_ref[...] = o.reshape(nb, C, RWKV_W)

    @pl.when(c == nc - 1)
    def _():
        for i in un:
            b, p = units[i]
            sout_ref[b, 2 * p] = s1[i][0:HEAD, 0:HEAD]
            sout_ref[b, 2 * p + 1] = s1[i][HEAD:2 * HEAD, HEAD:2 * HEAD]
        for b in seqs:
            for q, ref in enumerate((zr_ref, zk_ref, zv_ref)):
                nsh_ref[0, b:b + 1, RWKV_W * q:RWKV_W * (q + 1)] = ref[b, C - 1:C, :]
            nsh_ref[0, b:b + 1, SHIFT_MAIN:SHIFT_MAIN + LANES] = zl_ref[b, C - 1:C, :]


def _wkv_chunk(z, zl, pvec, mul, wd, wa, e, batch, seq, nb):
    C = WKV_CHUNK
    nc = seq // C
    full = lambda shp: pl.BlockSpec(shp, lambda b, c: (0,) * len(shp))
    col = lambda j: pl.BlockSpec((nb, C, RWKV_W), lambda b, c, j=j: (b, c, j))
    return pl.pallas_call(
        _wkv_chunk_kernel,
        out_shape=(jax.ShapeDtypeStruct((batch, seq, RWKV_W), BF16),
                   jax.ShapeDtypeStruct((batch, HEADS, HEAD, HEAD), F32),
                   jax.ShapeDtypeStruct((batch // nb, nb, SHIFT_MAIN + LANES), F32)),
        grid=(batch // nb, nc),
        in_specs=[col(0), col(1), col(2), col(3),
                  pl.BlockSpec((nb, C, LANES), lambda b, c: (b, c, 0)),
                  full(pvec.shape), full(mul.shape), full(wd.shape), full(wa.shape), full(e.shape)],
        out_specs=(pl.BlockSpec((nb, C, RWKV_W), lambda b, c: (b, c, 0)),
                   pl.BlockSpec((nb, HEADS, HEAD, HEAD), lambda b, c: (b, 0, 0, 0)),
                   pl.BlockSpec((1, nb, SHIFT_MAIN + LANES), lambda b, c: (b, 0, 0))),
        scratch_shapes=[pltpu.VMEM((nb * HEADS // 2, 2 * HEAD, 2 * HEAD), F32),
                        pltpu.VMEM((nb, SUBLANES, 3 * RWKV_W), F32),
                        pltpu.VMEM((nb, SUBLANES, LANES), F32)],
        compiler_params=pltpu.CompilerParams(
            dimension_semantics=("arbitrary", "arbitrary"), vmem_limit_bytes=VMEM_LIMIT),
        name="wkv_chunk",
    )(z, z, z, z, zl, pvec, mul, wd, wa, e)


def _wkv_step_kernel(zr_ref, zk_ref, zv_ref, zrg_ref, zl_ref, sh_ref, s0_ref, pv_ref,
                     mul_ref, wd_ref, wa_ref, e_ref, o_ref, sout_ref, nsh_ref,
                     at_s, drt_s, bt_s, kt_s, dt_s, vt_s, brt_s, krt_s, yt_s, keep_s):
    h = pl.program_id(0)
    nh = pl.num_programs(0)
    nseq = zr_ref.shape[0]

    @pl.when(h == 0)
    def _():
        e = e_ref[...]
        r, k2, v, av, bv, logd = _wkv_prep(
            zr_ref[...], zk_ref[...], zv_ref[...], zl_ref[...],
            sh_ref[:, 0:RWKV_W], sh_ref[:, RWKV_W:2 * RWKV_W], sh_ref[:, 2 * RWKV_W:3 * RWKV_W],
            sh_ref[:, SHIFT_MAIN:SHIFT_MAIN + LANES], pv_ref, mul_ref, wd_ref, wa_ref, e)
        nsh_ref[:, 0:RWKV_W] = zr_ref[...]
        nsh_ref[:, RWKV_W:2 * RWKV_W] = zk_ref[...]
        nsh_ref[:, 2 * RWKV_W:SHIFT_MAIN] = zv_ref[...]
        nsh_ref[:, SHIFT_MAIN:SHIFT_MAIN + LANES] = zl_ref[...]
        d = jnp.exp(logd)
        at_s[...] = av.T
        drt_s[...] = (d * r).T
        bt_s[...] = bv.T
        kt_s[...] = k2.T
        dt_s[...] = d.T
        vt_s[...] = v.T
        brt_s[...] = jnp.sum((bv * r).T.reshape(HEADS, HEAD, nseq), axis=1)
        krt_s[...] = jnp.sum((k2 * r).T.reshape(HEADS, HEAD, nseq), axis=1)
        keep_s[0] = r
        keep_s[1] = k2
        keep_s[2] = v

    for u in range(s0_ref.shape[0]):
        head = h * s0_ref.shape[0] + u
        base = pl.multiple_of(head * HEAD, HEAD)
        rows = pl.ds(base, HEAD)
        a_h, dr_h, b_h, k_h, d_h = at_s[rows, :], drt_s[rows, :], bt_s[rows, :], kt_s[rows, :], dt_s[rows, :]
        br_h = brt_s[pl.ds(head, 1), :]
        kr_h = krt_s[pl.ds(head, 1), :]

        def value_rows(g, carry, u=u, base=base, a_h=a_h, dr_h=dr_h, b_h=b_h, k_h=k_h, d_h=d_h,
                       br_h=br_h, kr_h=kr_h):
            off = pl.multiple_of(base + g * SUBLANES, SUBLANES)
            v8 = vt_s[pl.ds(off, SUBLANES), :]
            ys = []
            for j in range(SUBLANES):
                vi = g * SUBLANES + j
                s_v = s0_ref[u, vi]
                sa = jnp.sum(s_v * a_h, axis=0, keepdims=True)
                y0 = jnp.sum(s_v * dr_h, axis=0, keepdims=True)
                v_v = v8[j:j + 1, :]
                sout_ref[u, vi] = s_v * d_h + sa * b_h + v_v * k_h
                ys.append(y0 + sa * br_h + v_v * kr_h)
            yt_s[pl.ds(off, SUBLANES), :] = jnp.concatenate(ys, axis=0)
            return carry

        lax.fori_loop(0, HEAD // SUBLANES, value_rows, 0)

    @pl.when(h == nh - 1)
    def _():
        o_ref[...] = _wkv_post(yt_s[...].T, keep_s[0], keep_s[1], keep_s[2], zrg_ref[...], pv_ref,
                               e_ref[...])


def _wkv_step(z, zl, sh, s0t, pvec, mul, wd, wa, e):
    nseq = z.shape[0]
    full = lambda shp: pl.BlockSpec(shp, lambda i: (0,) * len(shp))
    col = lambda j: pl.BlockSpec((nseq, RWKV_W), lambda i, j=j: (0, j))
    st_block = (STEP_HEADS, HEAD, HEAD, nseq)
    wide = pltpu.VMEM((RWKV_W, nseq), F32)
    return pl.pallas_call(
        _wkv_step_kernel,
        out_shape=(jax.ShapeDtypeStruct((nseq, RWKV_W), BF16),
                   jax.ShapeDtypeStruct(s0t.shape, F32),
                   jax.ShapeDtypeStruct(sh.shape, F32)),
        grid=(HEADS // STEP_HEADS,),
        in_specs=[col(0), col(1), col(2), col(3),
                  full(zl.shape), full(sh.shape),
                  pl.BlockSpec(st_block, lambda i: (i, 0, 0, 0)),
                  full(pvec.shape), full(mul.shape), full(wd.shape), full(wa.shape), full(e.shape)],
        out_specs=(full((nseq, RWKV_W)),
                   pl.BlockSpec(st_block, lambda i: (i, 0, 0, 0)),
                   full(sh.shape)),
        scratch_shapes=[wide] * 6 + [pltpu.VMEM((HEADS, nseq), F32)] * 2
                       + [wide, pltpu.VMEM((3, nseq, RWKV_W), F32)],
        compiler_params=pltpu.CompilerParams(
            dimension_semantics=("arbitrary",), vmem_limit_bytes=VMEM_LIMIT),
        name="wkv_step",
    )(z, z, z, z, zl, sh, s0t, pvec, mul, wd, wa, e)


_CW0, _CW1, _CW2, _CW3, _CB, _GXB, _GAB, _LAM = range(8)


def _lru_gates(xc, lp_ref, wg_ref):
    xb = _bf(xc)
    ngroups = wg_ref.shape[0]
    gs = [_dg(xb[:, LANES * g:LANES * (g + 1)], wg_ref[g], NN) for g in range(ngroups)]
    gx_pre = jnp.concatenate([gs[g][:, 0:LANES] for g in range(ngroups)], axis=1)
    ga_pre = jnp.concatenate([gs[g][:, LANES:2 * LANES] for g in range(ngroups)], axis=1)
    gx = _sigmoid(gx_pre + _prow(lp_ref, _GXB))
    ga = _sigmoid(ga_pre + _prow(lp_ref, _GAB))
    log_a = -LRU_C * ga * _softplus(-_prow(lp_ref, _LAM))
    a = jnp.exp(log_a)
    mult = jnp.sqrt((1.0 - a) * (1.0 + a))
    return a, mult * gx * xc


def _lru_scan_rows(a, b, zg, hc):
    row8 = lax.broadcasted_iota(jnp.int32, (SUBLANES, 1), 0)
    hs = []
    for i in range(a.shape[0] // SUBLANES):
        a8 = a[SUBLANES * i:SUBLANES * (i + 1), :]
        b8 = b[SUBLANES * i:SUBLANES * (i + 1), :]
        for s in (1, 2, 4):
            keep = row8 >= s
            b8 = jnp.where(keep, a8 * pltpu.roll(b8, s, 0) + b8, b8)
            a8 = jnp.where(keep, a8 * pltpu.roll(a8, s, 0), a8)
        hb = b8 + a8 * hc
        hs.append(hb)
        hc = jnp.broadcast_to(hb[SUBLANES - 1:SUBLANES, :], hb.shape)
    return _bf(jnp.concatenate(hs, axis=0) * (zg * _sigmoid(zg))), hc


def _lru_step_kernel(zx_ref, zg_ref, conv_ref, h0_ref, lp_ref, wg_ref, o_ref, hnew_ref, cnew_ref):
    zx = zx_ref[...]
    for j in range(CONV_W - 2):
        cnew_ref[j] = conv_ref[j + 1]
    cnew_ref[CONV_W - 2] = zx
    xc = _prow(lp_ref, _CW3) * zx + _prow(lp_ref, _CB)
    for j in range(CONV_W - 1):
        xc = xc + _prow(lp_ref, j) * conv_ref[j]
    a, b = _lru_gates(xc, lp_ref, wg_ref)
    h = a * h0_ref[...] + b
    hnew_ref[...] = h
    zg = zg_ref[...]
    o_ref[...] = _bf(h * (zg * _sigmoid(zg)))


def _lru_step(z_main, conv, h0, lp, wg):
    nb = z_main.shape[0]
    full = lambda shp: pl.BlockSpec(shp, lambda i: (0,) * len(shp))
    col = lambda j: pl.BlockSpec((nb, LRU_W), lambda i, j=j: (0, j))
    return pl.pallas_call(
        _lru_step_kernel,
        out_shape=(jax.ShapeDtypeStruct((nb, LRU_W), BF16), jax.ShapeDtypeStruct((nb, LRU_W), F32),
                   jax.ShapeDtypeStruct(conv.shape, F32)),
        grid=(1,),
        in_specs=[col(4), col(5), full(conv.shape), full(h0.shape), full(lp.shape), full(wg.shape)],
        out_specs=(full((nb, LRU_W)), full((nb, LRU_W)), full(conv.shape)),
        compiler_params=pltpu.CompilerParams(
            dimension_semantics=("arbitrary",), vmem_limit_bytes=VMEM_LIMIT),
        name="lru_step",
    )(z_main, z_main, conv, h0, lp, wg)


def _project(x, o_r, o_g, m_r, m_g, wr_ref, wg_ref, wo_ref, fg_ref, final):
    y_r = _dg(o_r, wr_ref[...], NN)
    y_g = _dg(o_g, wg_ref[...], NN)
    merged = _sigmoid(m_r) * y_r + _sigmoid(m_g) * y_g
    out = x + _dg(_bf(merged), wo_ref[...], NN)
    return _rms(out, fg_ref[...]) if final else out


def _outproj_lru_kernel(x_ref, or_ref, mr_ref, mg_ref, zx_ref, zg_ref, xs_ref, ors_ref, ogs_ref,
                        mrs_ref, mgs_ref, lp_ref, wgate_ref, wr_ref, wg_ref, wo_ref, fg_ref,
                        out_ref, hlast_ref, cnew_ref, outs_ref, og_s, mg_s, xb_s, hc_s, *, final,
                        tiles_per_seq, n_tiles):
    i = pl.program_id(0)
    n = n_tiles
    tm, d = x_ref.shape

    @pl.when(i == 0)
    def _():
        xb_s[...] = jnp.zeros_like(xb_s)
        hc_s[...] = jnp.zeros_like(hc_s)
        outs_ref[...] = _project(xs_ref[...], ors_ref[...], ogs_ref[...], mrs_ref[...], mgs_ref[...],
                                 wr_ref, wg_ref, wo_ref, fg_ref, final)

    def stages(lru, branch, outp):
        if branch:
            og_prev = og_s[...]
        if outp:
            mg_prev = mg_s[...]
        if lru:
            t = lax.rem(i, tiles_per_seq)
            first = t == 0
            xb_s[0:SUBLANES, :] = jnp.where(first, 0.0, xb_s[0:SUBLANES, :])
            xb_s[SUBLANES:SUBLANES + tm, :] = zx_ref[...]

        outs, a_parts, b_parts = [], [], []
        for c in range(FUSE_PIECES):
            if outp:
                cs = slice(c * d // FUSE_PIECES, (c + 1) * d // FUSE_PIECES)
                outs.append(x_ref[:, cs] + _dg(mg_prev, wo_ref[:, cs], NN))
            if lru:
                r0, r1 = SUBLANES + c * tm // FUSE_PIECES, SUBLANES + (c + 1) * tm // FUSE_PIECES
                xc = _prow(lp_ref, _CW3) * xb_s[r0:r1, :] + _prow(lp_ref, _CB)
                for j in range(1, CONV_W):
                    xc = xc + _prow(lp_ref, CONV_W - 1 - j) * xb_s[r0 - j:r1 - j, :]
                a_c, b_c = _lru_gates(xc, lp_ref, wgate_ref)
                a_parts.append(a_c)
                b_parts.append(b_c)
        if outp:
            out = jnp.concatenate(outs, axis=1)
            out_ref[...] = _rms(out, fg_ref[...]) if final else out
        if lru:
            xb_s[0:SUBLANES, :] = xb_s[tm:tm + SUBLANES, :]
            o_g, hc = _lru_scan_rows(jnp.concatenate(a_parts, axis=0), jnp.concatenate(b_parts, axis=0),
                                     zg_ref[...], jnp.where(first, 0.0, hc_s[...]))
            hc_s[...] = hc
            og_s[...] = o_g
        if branch:
            y_r = _dg(or_ref[...], wr_ref[...], NN)
            y_g = _dg(og_prev, wg_ref[...], NN)
            mg_s[...] = _bf(_sigmoid(mr_ref[...]) * y_r + _sigmoid(mg_ref[...]) * y_g)
        if lru:
            @pl.when(t == tiles_per_seq - 1)
            def _():
                seq = i // tiles_per_seq
                hlast_ref[pl.ds(seq, 1), :] = hc[0:1, :]
                for j in range(CONV_W - 1):
                    row = tm - (CONV_W - 1) + j
                    cnew_ref[j, pl.ds(seq, 1), :] = zx_ref[row:row + 1, :]

    active = lambda step, s: 0 <= step - s < n
    holds = lambda s, flag: ((i >= s) & (i < n + s)) == flag
    for combo in sorted({tuple(active(step, s) for s in range(3)) for step in range(n + 2)}):
        lru, branch, outp = combo
        pl.when(holds(0, lru) & holds(1, branch) & holds(2, outp))(
            functools.partial(stages, lru, branch, outp))


def _outproj_lru(x, o_r, z, xs, o_rs, o_gs, zs, lp, wgate, w_r, w_g, w_o, fg, tm, seq, final):
    m, d = x.shape
    ms = xs.shape[0]
    n = m // tm
    tiles_per_seq = seq // tm
    const = lambda shp: pl.BlockSpec(shp, lambda i: (0,) * len(shp), pipeline_mode=pl.Buffered(1))
    back = lambda i, k: jnp.clip(i - k, 0, n - 1)
    return pl.pallas_call(
        functools.partial(_outproj_lru_kernel, final=final, tiles_per_seq=tiles_per_seq, n_tiles=n),
        out_shape=(jax.ShapeDtypeStruct((m, d), F32),
                   jax.ShapeDtypeStruct((m // seq, LRU_W), F32),
                   jax.ShapeDtypeStruct((CONV_W - 1, m // seq, LRU_W), F32),
                   jax.ShapeDtypeStruct((ms, d), F32)),
        grid=(n + 2,),
        in_specs=[
            pl.BlockSpec((tm, d), lambda i: (back(i, 2), 0)),
            pl.BlockSpec((tm, RWKV_W), lambda i: (back(i, 1), 0)),
            pl.BlockSpec((tm, d), lambda i: (back(i, 1), 3)),
            pl.BlockSpec((tm, d), lambda i: (back(i, 1), 4)),
            pl.BlockSpec((tm, LRU_W), lambda i: (back(i, 0), 4)),
            pl.BlockSpec((tm, LRU_W), lambda i: (back(i, 0), 5)),
            const(xs.shape), const(o_rs.shape), const(o_gs.shape),
            pl.BlockSpec((ms, d), lambda i: (0, 3), pipeline_mode=pl.Buffered(1)),
            pl.BlockSpec((ms, d), lambda i: (0, 4), pipeline_mode=pl.Buffered(1)),
            const(lp.shape), const(wgate.shape), const(w_r.shape), const(w_g.shape), const(w_o.shape),
            const(fg.shape),
        ],
        out_specs=(pl.BlockSpec((tm, d), lambda i: (back(i, 2), 0)),
                   pl.BlockSpec((m // seq, LRU_W), lambda i: (0, 0)),
                   pl.BlockSpec((CONV_W - 1, m // seq, LRU_W), lambda i: (0, 0, 0)),
                   pl.BlockSpec((ms, d), lambda i: (0, 0))),
        scratch_shapes=[pltpu.VMEM((tm, LRU_W), BF16),
                        pltpu.VMEM((tm, d), BF16),
                        pltpu.VMEM((SUBLANES + tm, LRU_W), F32),
                        pltpu.VMEM((SUBLANES, LRU_W), F32)],
        compiler_params=pltpu.CompilerParams(
            dimension_semantics=("arbitrary",), vmem_limit_bytes=VMEM_LIMIT),
        name="outproj_lru",
    )(x, o_r, z, z, z, z, xs, o_rs, o_gs, zs, zs, lp, wgate, w_r, w_g, w_o, fg)


def _row_tile(m, want):
    t = min(m, want)
    assert m % t == 0, (m, t)
    return t


def _pack_params_kernel(mu_ref, w0_ref, a0_ref, kk_ref, ka_ref, lng_ref, lnb_ref, rk_ref, wdu_ref,
                        wau_ref, cw_ref, cb_ref, gxb_ref, gab_ref, lam_ref, gxw_ref, gaw_ref,
                        pvec_ref, mul_ref, wd_ref, wa_ref, e_ref, lp_ref, wg_ref):
    pvec_ref[...] = jnp.zeros_like(pvec_ref)
    for i in range(3):
        pvec_ref[_MU_R + i:_MU_R + i + 1, :] = mu_ref[:, RWKV_W * i:RWKV_W * (i + 1)]
    for row, ref in ((_W0, w0_ref), (_A0, a0_ref), (_KK, kk_ref), (_KA, ka_ref),
                     (_LNG, lng_ref), (_LNB, lnb_ref)):
        pvec_ref[row:row + 1, :] = ref[...]
    for h in range(HEADS):
        pvec_ref[_RK:_RK + 1, HEAD * h:HEAD * (h + 1)] = rk_ref[h:h + 1, :]
    mul_ref[...] = jnp.broadcast_to(mu_ref[:, 3 * RWKV_W:3 * RWKV_W + 2 * LORA], mul_ref.shape)

    zeros = jnp.zeros((LORA, RWKV_W), BF16)
    wd_ref[0:LORA, :] = _bf(wdu_ref[...])
    wd_ref[LORA:2 * LORA, :] = zeros
    wa_ref[0:LORA, :] = zeros
    wa_ref[LORA:2 * LORA, :] = _bf(wau_ref[...])

    ri = lax.broadcasted_iota(jnp.int32, (LANES, LANES), 0)
    ci = lax.broadcasted_iota(jnp.int32, (LANES, LANES), 1)
    e_ref[...] = jnp.where((ri < HEAD) == (ci < HEAD), 1.0, 0.0).astype(BF16)

    lp_ref[_CW0:_CW0 + CONV_W, :] = cw_ref[...]
    for row, ref in ((_CB, cb_ref), (_GXB, gxb_ref), (_GAB, gab_ref), (_LAM, lam_ref)):
        lp_ref[row:row + 1, :] = ref[...]

    blk = LRU_W // LRU_BLOCKS
    z = jnp.zeros((blk, blk), F32)
    for g in range(LRU_BLOCKS // 2):
        top = jnp.concatenate([gxw_ref[2 * g], z, gaw_ref[2 * g], z], axis=1)
        bot = jnp.concatenate([z, gxw_ref[2 * g + 1], z, gaw_ref[2 * g + 1]], axis=1)
        wg_ref[g] = _bf(jnp.concatenate([top, bot], axis=0))


def _pack_params(l, rwkv_mu, w_decay0, w_decay_up, w_iclr0, w_iclr_up, k_k, k_a, r_k, ln_x_g,
                 ln_x_b, conv_w, conv_b, lru_gx_w, lru_gx_b, lru_ga_w, lru_ga_b, lru_lambda):
    blk = LRU_W // LRU_BLOCKS
    assert 2 * blk == LANES and 2 * LORA == LANES
    row = lambda a: pl.BlockSpec((1, a.shape[-1]), lambda i: (l, 0))
    mat = lambda a: pl.BlockSpec((None,) + a.shape[1:], lambda i: (l,) + (0,) * (a.ndim - 1))
    full = lambda shp: pl.BlockSpec(shp, lambda i: (0,) * len(shp))
    rows = (rwkv_mu, w_decay0, w_iclr0, k_k, k_a, ln_x_g, ln_x_b)
    out_shapes = ((16, RWKV_W, F32), (SUBLANES, LANES, F32), (LANES, RWKV_W, BF16), (LANES, RWKV_W, BF16),
                  (LANES, LANES, BF16), (SUBLANES, LRU_W, F32))
    outs = tuple(jax.ShapeDtypeStruct(s[:2], s[2]) for s in out_shapes)
    outs += (jax.ShapeDtypeStruct((LRU_BLOCKS // 2, LANES, 2 * LANES), BF16),)
    return pl.pallas_call(
        _pack_params_kernel,
        out_shape=outs,
        grid=(1,),
        in_specs=[row(a) for a in rows] + [mat(r_k), mat(w_decay_up), mat(w_iclr_up), mat(conv_w),
                                          row(conv_b), row(lru_gx_b), row(lru_ga_b), row(lru_lambda),
                                          mat(lru_gx_w), mat(lru_ga_w)],
        out_specs=tuple(full(o.shape) for o in outs),
        compiler_params=pltpu.CompilerParams(
            dimension_semantics=("arbitrary",), vmem_limit_bytes=VMEM_LIMIT),
        name="pack_params",
    )(*rows, r_k, w_decay_up, w_iclr_up, conv_w, conv_b, lru_gx_b, lru_ga_b, lru_lambda, lru_gx_w,
      lru_ga_w)


def kernel(x_prompt, x_sample, state_shift, state_wkv, state_conv, state_lru, norm_g, w_in, rwkv_mu,
           w_decay0, w_decay_up, w_iclr0, w_iclr_up, k_k, k_a, r_k, ln_x_g, ln_x_b, w_out_rwkv,
           conv_w, conv_b, lru_gx_w, lru_gx_b, lru_ga_w, lru_ga_b, lru_lambda, w_out_lru, w_out,
           final_norm_g):
    bp, seq, d = x_prompt.shape
    bs = x_sample.shape[0]
    assert x_sample.shape[1] == 1 and seq % WKV_CHUNK == 0
    depth = w_in.shape[0]
    xp = x_prompt.reshape(bp * seq, d)
    xs = x_sample.reshape(bs, d)
    fg = final_norm_g.reshape(1, d)
    outs = [[] for _ in range(8)]
    for l in range(depth):
        pvec, mul, wd, wa, e, lp, wg = _pack_params(
            l, rwkv_mu, w_decay0, w_decay_up, w_iclr0, w_iclr_up, k_k, k_a, r_k, ln_x_g, ln_x_b,
            conv_w, conv_b, lru_gx_w, lru_gx_b, lru_ga_w, lru_ga_b, lru_lambda)
        g = norm_g[l].reshape(1, d)
        rec = (pvec, mul, wd, wa, e)
        w, wl, zs, zls, w_r, w_g, w_o = _inproj_head(xs, g, w_in, w_out_rwkv, w_out_lru, w_out, l, INPROJ_TN)
        zp, zlp = _inproj(xp, g, w, wl, _row_tile(bp * seq, 1024), INPROJ_WIDE_TN)

        s0t = jnp.transpose(state_wkv[l], (1, 2, 3, 0))
        o_rs, s_new, sh_new = _wkv_step(zs, zls, state_shift[l], s0t, *rec)
        conv = jnp.transpose(state_conv[l], (1, 0, 2))
        o_gs, h_new, conv_new = _lru_step(zs, conv, state_lru[l], lp, wg)
        outs[4].append(sh_new)
        outs[5].append(jnp.transpose(s_new, (3, 0, 1, 2)))
        outs[6].append(jnp.transpose(conv_new, (1, 0, 2)))
        outs[7].append(h_new)

        zp3 = zp.reshape(bp, seq, -1)
        zlp3 = zlp.reshape(bp, seq, LANES)
        nb = max(n for n in (4, 2, 1) if bp % n == 0)
        o_r, s_new, sh_last = _wkv_chunk(zp3, zlp3, *rec, bp, seq, nb)
        o_r = o_r.reshape(bp * seq, RWKV_W)
        last = l == depth - 1
        xp, h_last, conv_last, xs = _outproj_lru(xp, o_r, zp, xs, o_rs, o_gs, zs, lp, wg, w_r, w_g, w_o,
                                                 fg, _row_tile(seq, 256), seq, last)
        outs[0].append(sh_last.reshape(bp, -1))
        outs[1].append(s_new)
        outs[2].append(jnp.transpose(conv_last, (1, 0, 2)))
        outs[3].append(h_last)

    return (xp.reshape(bp, seq, d), xs.reshape(bs, 1, d)) + tuple(jnp.stack(o) for o in outs)
```

```python
import functools

import jax
import jax.numpy as jnp
from jax import lax
from jax.experimental import pallas as pl
from jax.experimental.pallas import tpu as pltpu

F32 = jnp.float32
BF16 = jnp.bfloat16

HEADS = 16
HEAD = 64
RWKV_W = HEADS * HEAD
LORA = 64
LRU_W = 1024
LRU_BLOCKS = 16
CONV_W = 4
LRU_C = 8.0
RMS_EPS = 1e-6
GN_EPS = 1e-5 * HEAD
DECAY_SCALE = 0.6065306597126334

LANES = 128
SUBLANES = 8
WKV_CHUNK = 64
STEP_HEADS = 2
VMEM_LIMIT = 60 * 1024 * 1024

NN = (((1,), (0,)), ((), ()))
NT = (((1,), (1,)), ((), ()))
TN = (((0,), (0,)), ((), ()))


def _bf(x):
    return x.astype(BF16)


def _dg(a, b, dn):
    return lax.dot_general(a, b, dn, preferred_element_type=F32)


def _softplus(x):
    return jnp.maximum(x, 0.0) + jnp.log1p(jnp.exp(-jnp.abs(x)))


def _sigmoid(x):
    return 1.0 / (1.0 + jnp.exp(-x))


def _segsum(x, e):
    rows, n = x.shape[0], x.shape[1] // LANES
    stacked = jnp.concatenate([x[:, LANES * j:LANES * (j + 1)] for j in range(n)], axis=0)
    s = _dg(_bf(stacked), e, NN)
    return jnp.concatenate([s[rows * j:rows * (j + 1), :] for j in range(n)], axis=1)


def _rms(x, g):
    return x * lax.rsqrt(jnp.mean(x * x, axis=-1, keepdims=True) + RMS_EPS) * g


SHIFT_MAIN = 3 * RWKV_W
LORA_COL = 10 * RWKV_W
LORA_BLOCK = LORA_COL // LANES
FUSE_PIECES = 4
INPROJ_TN = 1024
INPROJ_WIDE_TN = 2048
NORM_ROWS = 128
OUT_W_STEPS = 8


def _inproj_head_kernel(xs_ref, g_ref, w_ref, wt_ref, wlo_ref, wr_ref, wg_ref, wo_ref, wb_ref, wl_ref,
                        zs_ref, zls_ref, wrb_ref, wgb_ref, wob_ref, hs_ref):
    j = pl.program_id(0)
    tn = w_ref.shape[-1]

    @pl.when(j == 0)
    def _():
        hs_ref[...] = _bf(_rms(xs_ref[...], g_ref[...]))
        wl_ref[...] = _bf(wlo_ref[...])
        zls_ref[...] = _dg(hs_ref[...], wl_ref[...], NN)

    @pl.when(j < SHIFT_MAIN // tn)
    def _():
        wb_ref[...] = _bf(w_ref[...])

    @pl.when(j >= SHIFT_MAIN // tn)
    def _():
        wb_ref[:, 0:tn - 2 * LORA] = _bf(w_ref[:, 2 * LORA:tn])
        wb_ref[:, tn - 2 * LORA:tn] = _bf(wt_ref[...])

    zs_ref[...] = _dg(hs_ref[...], wb_ref[...], NN)

    @pl.when(j < OUT_W_STEPS)
    def _():
        wrb_ref[...] = _bf(wr_ref[...])
        wgb_ref[...] = _bf(wg_ref[...])
        wob_ref[...] = _bf(wo_ref[...])


def _inproj_head(xs, g, w_in, w_out_rwkv, w_out_lru, w_out, layer, tn):
    ms, d = xs.shape
    n = w_in.shape[-1]
    assert n == LORA_COL + 2 * LORA and SHIFT_MAIN % tn == 0 and 2 * LORA == LANES
    nj = LORA_COL // tn
    assert nj >= OUT_W_STEPS
    lanes_per_tile = tn // LANES
    one = lambda shp, imap: pl.BlockSpec(shp, imap, pipeline_mode=pl.Buffered(1))
    rows = lambda a: a.shape[1] // OUT_W_STEPS
    step = lambda j: jnp.minimum(j, OUT_W_STEPS - 1)
    w_outs = (w_out_rwkv, w_out_lru, w_out)
    return pl.pallas_call(
        _inproj_head_kernel,
        out_shape=(jax.ShapeDtypeStruct((d, LORA_COL), BF16), jax.ShapeDtypeStruct((d, LANES), BF16),
                   jax.ShapeDtypeStruct((ms, LORA_COL), F32), jax.ShapeDtypeStruct((ms, LANES), F32))
        + tuple(jax.ShapeDtypeStruct(a.shape[1:], BF16) for a in w_outs),
        grid=(nj,),
        in_specs=[
            one((ms, d), lambda j: (0, 0)),
            one((1, d), lambda j: (0, 0)),
            pl.BlockSpec((None, d, tn), lambda j: (layer, 0, j)),
            pl.BlockSpec((None, d, LANES), lambda j: (layer, 0, (j + 1) * lanes_per_tile)),
            one((None, d, LANES), lambda j: (layer, 0, SHIFT_MAIN // LANES)),
        ] + [pl.BlockSpec((None, rows(a), a.shape[2]), lambda j: (layer, step(j), 0)) for a in w_outs],
        out_specs=(
            pl.BlockSpec((d, tn), lambda j: (0, j)),
            pl.BlockSpec((d, LANES), lambda j: (0, 0)),
            pl.BlockSpec((ms, tn), lambda j: (0, j)),
            pl.BlockSpec((ms, LANES), lambda j: (0, 0)),
        ) + tuple(pl.BlockSpec((rows(a), a.shape[2]), lambda j: (step(j), 0)) for a in w_outs),
        scratch_shapes=[pltpu.VMEM((ms, d), BF16)],
        compiler_params=pltpu.CompilerParams(
            dimension_semantics=("arbitrary",), vmem_limit_bytes=VMEM_LIMIT),
        name="inproj_head",
    )(xs, g, w_in, w_in, w_in, *w_outs)


def _inproj_kernel(x_ref, g_ref, w_ref, wl_ref, z_ref, zl_ref, h_ref):
    @pl.when(pl.program_id(1) == 0)
    def _():
        for r in range(0, x_ref.shape[0], NORM_ROWS):
            rows = slice(r, r + NORM_ROWS)
            h_ref[rows, :] = _bf(_rms(x_ref[rows, :], g_ref[...]))
        zl_ref[...] = _dg(h_ref[...], wl_ref[...], NN)

    z_ref[...] = _dg(h_ref[...], w_ref[...], NN)


def _inproj(x, g, w, wl, tm, tn):
    m, d = x.shape
    return pl.pallas_call(
        _inproj_kernel,
        out_shape=(jax.ShapeDtypeStruct((m, LORA_COL), F32), jax.ShapeDtypeStruct((m, LANES), F32)),
        grid=(m // tm, LORA_COL // tn),
        in_specs=[
            pl.BlockSpec((tm, d), lambda i, j: (i, 0)),
            pl.BlockSpec((1, d), lambda i, j: (0, 0), pipeline_mode=pl.Buffered(1)),
            pl.BlockSpec((d, tn), lambda i, j: (0, j)),
            pl.BlockSpec((d, LANES), lambda i, j: (0, 0), pipeline_mode=pl.Buffered(1)),
        ],
        out_specs=(
            pl.BlockSpec((tm, tn), lambda i, j: (i, j)),
            pl.BlockSpec((tm, LANES), lambda i, j: (i, 0)),
        ),
        scratch_shapes=[pltpu.VMEM((tm, d), BF16)],
        compiler_params=pltpu.CompilerParams(
            dimension_semantics=("arbitrary", "arbitrary"), vmem_limit_bytes=VMEM_LIMIT),
        name="inproj",
    )(x, g, w, wl)


_MU_R, _MU_K, _MU_V, _W0, _A0, _KK, _KA, _RK, _LNG, _LNB = range(10)


def _prow(pv_ref, i):
    return pv_ref[i:i + 1, :]


def _wkv_prep(zr, zk, zv, zl, pr, pk, pv, pl_, pv_ref, mul_ref, wd_ref, wa_ref, e):
    r = zr + _prow(pv_ref, _MU_R) * (pr - zr)
    k = zk + _prow(pv_ref, _MU_K) * (pk - zk)
    v = zv + _prow(pv_ref, _MU_V) * (pv - zv)
    lo = zl + mul_ref[0:1, :] * (pl_ - zl)
    lw = _dg(_bf(jnp.tanh(lo)), wd_ref[...], NN)
    la = _dg(_bf(lo), wa_ref[...], NN)
    logd = -DECAY_SCALE * _sigmoid(_prow(pv_ref, _W0) + lw)
    a = _sigmoid(_prow(pv_ref, _A0) + la)
    kk = k * _prow(pv_ref, _KK)
    kk = kk * lax.rsqrt(jnp.maximum(_segsum(kk * kk, e), 1e-24))
    k2 = k * (1.0 + (a - 1.0) * _prow(pv_ref, _KA))
    return r, k2, v, -kk, kk * a, logd


def _wkv_bonus_gate(r, k2, v, zrg, pv_ref, e):
    return _segsum(r * k2 * _prow(pv_ref, _RK), e) * v, zrg * _sigmoid(zrg)


def _wkv_norm_gate(y, bonus_v, gate, pv_ref, e):
    mu = _segsum(y, e) * (1.0 / HEAD)
    yc = y - mu
    var = _segsum(yc * yc, e) * (1.0 / HEAD)
    yn = yc * lax.rsqrt(var + GN_EPS) * _prow(pv_ref, _LNG) + _prow(pv_ref, _LNB)
    return _bf((yn + bonus_v) * gate)


def _wkv_post(y, r, k2, v, zrg, pv_ref, e):
    bonus_v, gate = _wkv_bonus_gate(r, k2, v, zrg, pv_ref, e)
    return _wkv_norm_gate(y, bonus_v, gate, pv_ref, e)


def _wkv_chunk_kernel(zr_ref, zk_ref, zv_ref, zrg_ref, zl_ref, pv_ref, mul_ref, wd_ref, wa_ref,
                      e_ref, o_ref, sout_ref, nsh_ref, s_s, prev_s, prevl_s):
    c = pl.program_id(1)
    nc = pl.num_programs(1)
    C = WKV_CHUNK
    assert C == HEAD and 2 * HEAD == LANES
    nb = zr_ref.shape[0]
    rows_all = nb * C
    seqs = range(nb)

    @pl.when(c == 0)
    def _():
        s_s[...] = jnp.zeros_like(s_s)
        prev_s[...] = jnp.zeros_like(prev_s)
        prevl_s[...] = jnp.zeros_like(prevl_s)

    first = lax.broadcasted_iota(jnp.int32, (SUBLANES, 1), 0) == 0

    def shifted(z, prev_ref, lanes):
        rolled = pltpu.roll(z, 1, 0)
        pieces = []
        for b in seqs:
            head = jnp.where(first, prev_ref[b, 0:1, lanes], rolled[b * C:b * C + SUBLANES, :])
            pieces += [head, rolled[b * C + SUBLANES:(b + 1) * C, :]]
        return jnp.concatenate(pieces, axis=0)

    def flat(ref):
        return ref[...].reshape(rows_all, ref.shape[-1])

    zr, zk, zv, zl = flat(zr_ref), flat(zk_ref), flat(zv_ref), flat(zl_ref)
    seg = [slice(RWKV_W * i, RWKV_W * (i + 1)) for i in range(3)]
    pr = shifted(zr, prev_s, seg[0])
    pk = shifted(zk, prev_s, seg[1])
    pv = shifted(zv, prev_s, seg[2])
    pl_ = shifted(zl, prevl_s, slice(0, LANES))
    for b in seqs:
        last = slice(b * C + C - 1, b * C + C)
        prev_s[b, 0:1, seg[0]] = zr[last, :]
        prev_s[b, 0:1, seg[1]] = zk[last, :]
        prev_s[b, 0:1, seg[2]] = zv[last, :]
        prevl_s[b, 0:1, :] = zl[last, :]

    e = e_ref[...]
    r, k2, v, av, bv, logd = _wkv_prep(zr, zk, zv, zl, pr, pk, pv, pl_, pv_ref, mul_ref,
                                       wd_ref, wa_ref, e)

    ti = lax.broadcasted_iota(jnp.int32, (rows_all, rows_all), 0)
    tj = lax.broadcasted_iota(jnp.int32, (rows_all, rows_all), 1)
    tri = jnp.where((ti >= tj) & ((ti & -C) == (tj & -C)), 1.0, 0.0).astype(BF16)
    d_hi = _bf(logd)
    d_r1 = logd - d_hi.astype(F32)
    d_mid = _bf(d_r1)
    d_lo = _bf(d_r1 - d_mid.astype(F32))
    cum = _dg(jnp.concatenate([tri, tri, tri], axis=1), jnp.concatenate([d_hi, d_mid, d_lo], axis=0), NN)
    e_in = jnp.exp(cum)
    e_neg = jnp.exp(-cum)
    a_t = av * jnp.exp(cum - logd)
    r_t = r * e_in
    k_t = k2 * e_neg
    b_t = bv * e_neg
    p_c = [jnp.exp(cum[b * C + C - 1:b * C + C, :]) for b in seqs]

    lane = lax.broadcasted_iota(jnp.int32, (C, LANES), 1)
    trow = lax.broadcasted_iota(jnp.int32, (C, LANES), 0)
    lo = lane < HEAD
    s_in = lane & (HEAD - 1)
    strict = s_in < trow
    incl2 = ((lax.broadcasted_iota(jnp.int32, (C, 2 * LANES), 1) & (HEAD - 1))
             <= lax.broadcasted_iota(jnp.int32, (C, 2 * LANES), 0))
    eye2 = jnp.where(s_in == trow, 1.0, 0.0).astype(F32)
    vrow = lax.broadcasted_iota(jnp.int32, (2 * HEAD, LANES), 0)
    klane = lax.broadcasted_iota(jnp.int32, (2 * HEAD, LANES), 1)
    same_head = (vrow < HEAD) == (klane < HEAD)

    def bd(x):
        z = jnp.zeros_like(x)
        return jnp.concatenate([jnp.where(lo, x, z), jnp.where(lo, z, x)], axis=0)

    npair = HEADS // 2
    units = [(b, p) for b in seqs for p in range(npair)]
    un = range(len(units))
    blk = lambda arr, i: arr[units[i][0] * C:(units[i][0] + 1) * C, LANES * units[i][1]:LANES * (units[i][1] + 1)]
    ar = [_bf(jnp.concatenate([blk(a_t, i), blk(r_t, i)], axis=0)) for i in un]
    bk = [_bf(jnp.concatenate([bd(blk(b_t, i)), bd(blk(k_t, i))], axis=0)) for i in un]
    g = [_dg(ar[i], bk[i], NT) for i in un]
    s0 = [s_s[i] for i in un]
    ars = [_dg(ar[i], _bf(s0[i]), NT) for i in un]
    vbd = [_bf(bd(blk(v, i))) for i in un]
    x = [jnp.where(strict, g[i][0:C, 0:LANES], 0.0) for i in un]
    ak = [jnp.where(strict, g[i][0:C, LANES:2 * LANES], 0.0) for i in un]
    w = [ars[i][0:C, :] + _dg(_bf(ak[i]), vbd[i], NN) for i in un]
    t = [eye2 + x[i] for i in un]
    x = [_dg(_bf(x[i]), _bf(bd(x[i])), NN) for i in un]
    for _ in range(C.bit_length() - 3):
        xt = [_dg(_bf(jnp.concatenate([x[i], t[i]], axis=0)), _bf(bd(x[i])), NN) for i in un]
        x = [xt[i][0:C, :] for i in un]
        t = [t[i] + xt[i][C:2 * C, :] for i in un]
    t = [t[i] + _dg(_bf(t[i]), _bf(bd(x[i])), NN) for i in un]
    u = [_dg(_bf(t[i]), _bf(bd(w[i])), NN) for i in un]
    rbk = [_bf(jnp.where(incl2, g[i][C:2 * C, :], 0.0)) for i in un]
    uvbd = [jnp.concatenate([_bf(bd(u[i])), vbd[i]], axis=0) for i in un]
    y = [ars[i][C:2 * C, :] + _dg(rbk[i], uvbd[i], NN) for i in un]
    uv = [_bf(jnp.concatenate([u[i], blk(v, i)], axis=0)) for i in un]
    pc = [p_c[units[i][0]][:, LANES * units[i][1]:LANES * (units[i][1] + 1)] for i in un]
    bkh = [_bf(jnp.concatenate([blk(b_t, i), blk(k_t, i)], axis=0) * pc[i]) for i in un]
    s1 = [s0[i] * pc[i] + jnp.where(same_head, _dg(uv[i], bkh[i], TN), 0.0) for i in un]
    for i in un:
        s_s[i] = s1[i]

    y_all = jnp.concatenate(
        [jnp.concatenate(y[b * npair:(b + 1) * npair], axis=1) for b in seqs], axis=0)
    o = _wkv_post(y_all, r, k2, v, flat(zrg_ref), pv_ref, e)
    o_ref[...] = o.reshape(nb, C, RWKV_W)

    @pl.when(c == nc - 1)
    def _():
        for i in un:
            b, p = units[i]
            sout_ref[b, 2 * p] = s1[i][0:HEAD, 0:HEAD]
            sout_ref[b, 2 * p + 1] = s1[i][HEAD:2 * HEAD, HEAD:2 * HEAD]
        for b in seqs:
            for q, ref in enumerate((zr_ref, zk_ref, zv_ref)):
                nsh_ref[0, b:b + 1, RWKV_W * q:RWKV_W * (q + 1)] = ref[b, C - 1:C, :]
            nsh_ref[0, b:b + 1, SHIFT_MAIN:SHIFT_MAIN + LANES] = zl_ref[b, C - 1:C, :]


def _wkv_chunk(z, zl, pvec, mul, wd, wa, e, batch, seq, nb):
    C = WKV_CHUNK
    nc = seq // C
    full = lambda shp: pl.BlockSpec(shp, lambda b, c: (0,) * len(shp))
    col = lambda j: pl.BlockSpec((nb, C, RWKV_W), lambda b, c, j=j: (b, c, j))
    return pl.pallas_call(
        _wkv_chunk_kernel,
        out_shape=(jax.ShapeDtypeStruct((batch, seq, RWKV_W), BF16),
                   jax.ShapeDtypeStruct((batch, HEADS, HEAD, HEAD), F32),
                   jax.ShapeDtypeStruct((batch // nb, nb, SHIFT_MAIN + LANES), F32)),
        grid=(batch // nb, nc),
        in_specs=[col(0), col(1), col(2), col(3),
                  pl.BlockSpec((nb, C, LANES), lambda b, c: (b, c, 0)),
                  full(pvec.shape), full(mul.shape), full(wd.shape), full(wa.shape), full(e.shape)],
        out_specs=(pl.BlockSpec((nb, C, RWKV_W), lambda b, c: (b, c, 0)),
                   pl.BlockSpec((nb, HEADS, HEAD, HEAD), lambda b, c: (b, 0, 0, 0)),
                   pl.BlockSpec((1, nb, SHIFT_MAIN + LANES), lambda b, c: (b, 0, 0))),
        scratch_shapes=[pltpu.VMEM((nb * HEADS // 2, 2 * HEAD, 2 * HEAD), F32),
                        pltpu.VMEM((nb, SUBLANES, 3 * RWKV_W), F32),
                        pltpu.VMEM((nb, SUBLANES, LANES), F32)],
        compiler_params=pltpu.CompilerParams(
            dimension_semantics=("arbitrary", "arbitrary"), vmem_limit_bytes=VMEM_LIMIT),
        name="wkv_chunk",
    )(z, z, z, z, zl, pvec, mul, wd, wa, e)


def _wkv_step_kernel(zr_ref, zk_ref, zv_ref, zrg_ref, zl_ref, sh_ref, s0_ref, pv_ref,
                     mul_ref, wd_ref, wa_ref, e_ref, o_ref, sout_ref, nsh_ref,
                     at_s, drt_s, bt_s, kt_s, dt_s, vt_s, brt_s, krt_s, yt_s, keep_s):
    h = pl.program_id(0)
    nh = pl.num_programs(0)
    nseq = zr_ref.shape[0]

    @pl.when(h == 0)
    def _():
        e = e_ref[...]
        r, k2, v, av, bv, logd = _wkv_prep(
            zr_ref[...], zk_ref[...], zv_ref[...], zl_ref[...],
            sh_ref[:, 0:RWKV_W], sh_ref[:, RWKV_W:2 * RWKV_W], sh_ref[:, 2 * RWKV_W:3 * RWKV_W],
            sh_ref[:, SHIFT_MAIN:SHIFT_MAIN + LANES], pv_ref, mul_ref, wd_ref, wa_ref, e)
        nsh_ref[:, 0:RWKV_W] = zr_ref[...]
        nsh_ref[:, RWKV_W:2 * RWKV_W] = zk_ref[...]
        nsh_ref[:, 2 * RWKV_W:SHIFT_MAIN] = zv_ref[...]
        nsh_ref[:, SHIFT_MAIN:SHIFT_MAIN + LANES] = zl_ref[...]
        d = jnp.exp(logd)
        at_s[...] = av.T
        drt_s[...] = (d * r).T
        bt_s[...] = bv.T
        kt_s[...] = k2.T
        dt_s[...] = d.T
        vt_s[...] = v.T
        brt_s[...] = jnp.sum((bv * r).T.reshape(HEADS, HEAD, nseq), axis=1)
        krt_s[...] = jnp.sum((k2 * r).T.reshape(HEADS, HEAD, nseq), axis=1)
        keep_s[0] = r
        keep_s[1] = k2
        keep_s[2] = v

    for u in range(s0_ref.shape[0]):
        head = h * s0_ref.shape[0] + u
        base = pl.multiple_of(head * HEAD, HEAD)
        rows = pl.ds(base, HEAD)
        a_h, dr_h, b_h, k_h, d_h = at_s[rows, :], drt_s[rows, :], bt_s[rows, :], kt_s[rows, :], dt_s[rows, :]
        br_h = brt_s[pl.ds(head, 1), :]
        kr_h = krt_s[pl.ds(head, 1), :]

        def value_rows(g, carry, u=u, base=base, a_h=a_h, dr_h=dr_h, b_h=b_h, k_h=k_h, d_h=d_h,
                       br_h=br_h, kr_h=kr_h):
            off = pl.multiple_of(base + g * SUBLANES, SUBLANES)
            v8 = vt_s[pl.ds(off, SUBLANES), :]
            ys = []
            for j in range(SUBLANES):
                vi = g * SUBLANES + j
                s_v = s0_ref[u, vi]
                sa = jnp.sum(s_v * a_h, axis=0, keepdims=True)
                y0 = jnp.sum(s_v * dr_h, axis=0, keepdims=True)
                v_v = v8[j:j + 1, :]
                sout_ref[u, vi] = s_v * d_h + sa * b_h + v_v * k_h
                ys.append(y0 + sa * br_h + v_v * kr_h)
            yt_s[pl.ds(off, SUBLANES), :] = jnp.concatenate(ys, axis=0)
            return carry

        lax.fori_loop(0, HEAD // SUBLANES, value_rows, 0)

    @pl.when(h == nh - 1)
    def _():
        o_ref[...] = _wkv_post(yt_s[...].T, keep_s[0], keep_s[1], keep_s[2], zrg_ref[...], pv_ref,
                               e_ref[...])


def _wkv_step(z, zl, sh, s0t, pvec, mul, wd, wa, e):
    nseq = z.shape[0]
    full = lambda shp: pl.BlockSpec(shp, lambda i: (0,) * len(shp))
    col = lambda j: pl.BlockSpec((nseq, RWKV_W), lambda i, j=j: (0, j))
    st_block = (STEP_HEADS, HEAD, HEAD, nseq)
    wide = pltpu.VMEM((RWKV_W, nseq), F32)
    return pl.pallas_call(
        _wkv_step_kernel,
        out_shape=(jax.ShapeDtypeStruct((nseq, RWKV_W), BF16),
                   jax.ShapeDtypeStruct(s0t.shape, F32),
                   jax.ShapeDtypeStruct(sh.shape, F32)),
        grid=(HEADS // STEP_HEADS,),
        in_specs=[col(0), col(1), col(2), col(3),
                  full(zl.shape), full(sh.shape),
                  pl.BlockSpec(st_block, lambda i: (i, 0, 0, 0)),
                  full(pvec.shape), full(mul.shape), full(wd.shape), full(wa.shape), full(e.shape)],
        out_specs=(full((nseq, RWKV_W)),
                   pl.BlockSpec(st_block, lambda i: (i, 0, 0, 0)),
                   full(sh.shape)),
        scratch_shapes=[wide] * 6 + [pltpu.VMEM((HEADS, nseq), F32)] * 2
                       + [wide, pltpu.VMEM((3, nseq, RWKV_W), F32)],
        compiler_params=pltpu.CompilerParams(
            dimension_semantics=("arbitrary",), vmem_limit_bytes=VMEM_LIMIT),
        name="wkv_step",
    )(z, z, z, z, zl, sh, s0t, pvec, mul, wd, wa, e)


_CW0, _CW1, _CW2, _CW3, _CB, _GXB, _GAB, _LAM = range(8)


def _lru_gates(xc, lp_ref, wg_ref):
    xb = _bf(xc)
    ngroups = wg_ref.shape[0]
    gs = [_dg(xb[:, LANES * g:LANES * (g + 1)], wg_ref[g], NN) for g in range(ngroups)]
    gx_pre = jnp.concatenate([gs[g][:, 0:LANES] for g in range(ngroups)], axis=1)
    ga_pre = jnp.concatenate([gs[g][:, LANES:2 * LANES] for g in range(ngroups)], axis=1)
    gx = _sigmoid(gx_pre + _prow(lp_ref, _GXB))
    ga = _sigmoid(ga_pre + _prow(lp_ref, _GAB))
    log_a = -LRU_C * ga * _softplus(-_prow(lp_ref, _LAM))
    a = jnp.exp(log_a)
    mult = jnp.sqrt((1.0 - a) * (1.0 + a))
    return a, mult * gx * xc


def _lru_scan_rows(a, b, zg, hc):
    row8 = lax.broadcasted_iota(jnp.int32, (SUBLANES, 1), 0)
    hs = []
    for i in range(a.shape[0] // SUBLANES):
        a8 = a[SUBLANES * i:SUBLANES * (i + 1), :]
        b8 = b[SUBLANES * i:SUBLANES * (i + 1), :]
        for s in (1, 2, 4):
            keep = row8 >= s
            b8 = jnp.where(keep, a8 * pltpu.roll(b8, s, 0) + b8, b8)
            a8 = jnp.where(keep, a8 * pltpu.roll(a8, s, 0), a8)
        hb = b8 + a8 * hc
        hs.append(hb)
        hc = jnp.broadcast_to(hb[SUBLANES - 1:SUBLANES, :], hb.shape)
    return _bf(jnp.concatenate(hs, axis=0) * (zg * _sigmoid(zg))), hc


def _lru_step_kernel(zx_ref, zg_ref, conv_ref, h0_ref, lp_ref, wg_ref, o_ref, hnew_ref, cnew_ref):
    zx = zx_ref[...]
    for j in range(CONV_W - 2):
        cnew_ref[j] = conv_ref[j + 1]
    cnew_ref[CONV_W - 2] = zx
    xc = _prow(lp_ref, _CW3) * zx + _prow(lp_ref, _CB)
    for j in range(CONV_W - 1):
        xc = xc + _prow(lp_ref, j) * conv_ref[j]
    a, b = _lru_gates(xc, lp_ref, wg_ref)
    h = a * h0_ref[...] + b
    hnew_ref[...] = h
    zg = zg_ref[...]
    o_ref[...] = _bf(h * (zg * _sigmoid(zg)))


def _lru_step(z_main, conv, h0, lp, wg):
    nb = z_main.shape[0]
    full = lambda shp: pl.BlockSpec(shp, lambda i: (0,) * len(shp))
    col = lambda j: pl.BlockSpec((nb, LRU_W), lambda i, j=j: (0, j))
    return pl.pallas_call(
        _lru_step_kernel,
        out_shape=(jax.ShapeDtypeStruct((nb, LRU_W), BF16), jax.ShapeDtypeStruct((nb, LRU_W), F32),
                   jax.ShapeDtypeStruct(conv.shape, F32)),
        grid=(1,),
        in_specs=[col(4), col(5), full(conv.shape), full(h0.shape), full(lp.shape), full(wg.shape)],
        out_specs=(full((nb, LRU_W)), full((nb, LRU_W)), full(conv.shape)),
        compiler_params=pltpu.CompilerParams(
            dimension_semantics=("arbitrary",), vmem_limit_bytes=VMEM_LIMIT),
        name="lru_step",
    )(z_main, z_main, conv, h0, lp, wg)


def _project(x, o_r, o_g, m_r, m_g, wr_ref, wg_ref, wo_ref, fg_ref, final):
    y_r = _dg(o_r, wr_ref[...], NN)
    y_g = _dg(o_g, wg_ref[...], NN)
    merged = _sigmoid(m_r) * y_r + _sigmoid(m_g) * y_g
    out = x + _dg(_bf(merged), wo_ref[...], NN)
    return _rms(out, fg_ref[...]) if final else out


def _outproj_lru_kernel(x_ref, or_ref, mr_ref, mg_ref, zx_ref, zg_ref, xs_ref, ors_ref, ogs_ref,
                        mrs_ref, mgs_ref, lp_ref, wgate_ref, wr_ref, wg_ref, wo_ref, fg_ref,
                        out_ref, hlast_ref, cnew_ref, outs_ref, og_s, mg_s, xb_s, hc_s, *, final,
                        tiles_per_seq, n_tiles):
    i = pl.program_id(0)
    n = n_tiles
    tm, d = x_ref.shape

    @pl.when(i == 0)
    def _():
        xb_s[...] = jnp.zeros_like(xb_s)
        hc_s[...] = jnp.zeros_like(hc_s)
        outs_ref[...] = _project(xs_ref[...], ors_ref[...], ogs_ref[...], mrs_ref[...], mgs_ref[...],
                                 wr_ref, wg_ref, wo_ref, fg_ref, final)

    def stages(lru, branch, outp):
        if branch:
            og_prev = og_s[...]
        if outp:
            mg_prev = mg_s[...]
        if lru:
            t = lax.rem(i, tiles_per_seq)
            first = t == 0
            xb_s[0:SUBLANES, :] = jnp.where(first, 0.0, xb_s[0:SUBLANES, :])
            xb_s[SUBLANES:SUBLANES + tm, :] = zx_ref[...]

        outs, a_parts, b_parts = [], [], []
        for c in range(FUSE_PIECES):
            if outp:
                cs = slice(c * d // FUSE_PIECES, (c + 1) * d // FUSE_PIECES)
                outs.append(x_ref[:, cs] + _dg(mg_prev, wo_ref[:, cs], NN))
            if lru:
                r0, r1 = SUBLANES + c * tm // FUSE_PIECES, SUBLANES + (c + 1) * tm // FUSE_PIECES
                xc = _prow(lp_ref, _CW3) * xb_s[r0:r1, :] + _prow(lp_ref, _CB)
                for j in range(1, CONV_W):
                    xc = xc + _prow(lp_ref, CONV_W - 1 - j) * xb_s[r0 - j:r1 - j, :]
                a_c, b_c = _lru_gates(xc, lp_ref, wgate_ref)
                a_parts.append(a_c)
                b_parts.append(b_c)
        if outp:
            out = jnp.concatenate(outs, axis=1)
            out_ref[...] = _rms(out, fg_ref[...]) if final else out
        if lru:
            xb_s[0:SUBLANES, :] = xb_s[tm:tm + SUBLANES, :]
            o_g, hc = _lru_scan_rows(jnp.concatenate(a_parts, axis=0), jnp.concatenate(b_parts, axis=0),
                                     zg_ref[...], jnp.where(first, 0.0, hc_s[...]))
            hc_s[...] = hc
            og_s[...] = o_g
        if branch:
            y_r = _dg(or_ref[...], wr_ref[...], NN)
            y_g = _dg(og_prev, wg_ref[...], NN)
            mg_s[...] = _bf(_sigmoid(mr_ref[...]) * y_r + _sigmoid(mg_ref[...]) * y_g)
        if lru:
            @pl.when(t == tiles_per_seq - 1)
            def _():
                seq = i // tiles_per_seq
                hlast_ref[pl.ds(seq, 1), :] = hc[0:1, :]
                for j in range(CONV_W - 1):
                    row = tm - (CONV_W - 1) + j
                    cnew_ref[j, pl.ds(seq, 1), :] = zx_ref[row:row + 1, :]

    active = lambda step, s: 0 <= step - s < n
    holds = lambda s, flag: ((i >= s) & (i < n + s)) == flag
    for combo in sorted({tuple(active(step, s) for s in range(3)) for step in range(n + 2)}):
        lru, branch, outp = combo
        pl.when(holds(0, lru) & holds(1, branch) & holds(2, outp))(
            functools.partial(stages, lru, branch, outp))


def _outproj_lru(x, o_r, z, xs, o_rs, o_gs, zs, lp, wgate, w_r, w_g, w_o, fg, tm, seq, final):
    m, d = x.shape
    ms = xs.shape[0]
    n = m // tm
    tiles_per_seq = seq // tm
    const = lambda shp: pl.BlockSpec(shp, lambda i: (0,) * len(shp), pipeline_mode=pl.Buffered(1))
    back = lambda i, k: jnp.clip(i - k, 0, n - 1)
    return pl.pallas_call(
        functools.partial(_outproj_lru_kernel, final=final, tiles_per_seq=tiles_per_seq, n_tiles=n),
        out_shape=(jax.ShapeDtypeStruct((m, d), F32),
                   jax.ShapeDtypeStruct((m // seq, LRU_W), F32),
                   jax.ShapeDtypeStruct((CONV_W - 1, m // seq, LRU_W), F32),
                   jax.ShapeDtypeStruct((ms, d), F32)),
        grid=(n + 2,),
        in_specs=[
            pl.BlockSpec((tm, d), lambda i: (back(i, 2), 0)),
            pl.BlockSpec((tm, RWKV_W), lambda i: (back(i, 1), 0)),
            pl.BlockSpec((tm, d), lambda i: (back(i, 1), 3)),
            pl.BlockSpec((tm, d), lambda i: (back(i, 1), 4)),
            pl.BlockSpec((tm, LRU_W), lambda i: (back(i, 0), 4)),
            pl.BlockSpec((tm, LRU_W), lambda i: (back(i, 0), 5)),
            const(xs.shape), const(o_rs.shape), const(o_gs.shape),
            pl.BlockSpec((ms, d), lambda i: (0, 3), pipeline_mode=pl.Buffered(1)),
            pl.BlockSpec((ms, d), lambda i: (0, 4), pipeline_mode=pl.Buffered(1)),
            const(lp.shape), const(wgate.shape), const(w_r.shape), const(w_g.shape), const(w_o.shape),
            const(fg.shape),
        ],
        out_specs=(pl.BlockSpec((tm, d), lambda i: (back(i, 2), 0)),
                   pl.BlockSpec((m // seq, LRU_W), lambda i: (0, 0)),
                   pl.BlockSpec((CONV_W - 1, m // seq, LRU_W), lambda i: (0, 0, 0)),
                   pl.BlockSpec((ms, d), lambda i: (0, 0))),
        scratch_shapes=[pltpu.VMEM((tm, LRU_W), BF16),
                        pltpu.VMEM((tm, d), BF16),
                        pltpu.VMEM((SUBLANES + tm, LRU_W), F32),
                        pltpu.VMEM((SUBLANES, LRU_W), F32)],
        compiler_params=pltpu.CompilerParams(
            dimension_semantics=("arbitrary",), vmem_limit_bytes=VMEM_LIMIT),
        name="outproj_lru",
    )(x, o_r, z, z, z, z, xs, o_rs, o_gs, zs, zs, lp, wgate, w_r, w_g, w_o, fg)


def _row_tile(m, want):
    t = min(m, want)
    assert m % t == 0, (m, t)
    return t


def _pack_params_kernel(mu_ref, w0_ref, a0_ref, kk_ref, ka_ref, lng_ref, lnb_ref, rk_ref, wdu_ref,
                        wau_ref, cw_ref, cb_ref, gxb_ref, gab_ref, lam_ref, gxw_ref, gaw_ref,
                        pvec_ref, mul_ref, wd_ref, wa_ref, e_ref, lp_ref, wg_ref):
    pvec_ref[...] = jnp.zeros_like(pvec_ref)
    for i in range(3):
        pvec_ref[_MU_R + i:_MU_R + i + 1, :] = mu_ref[:, RWKV_W * i:RWKV_W * (i + 1)]
    for row, ref in ((_W0, w0_ref), (_A0, a0_ref), (_KK, kk_ref), (_KA, ka_ref),
                     (_LNG, lng_ref), (_LNB, lnb_ref)):
        pvec_ref[row:row + 1, :] = ref[...]
    for h in range(HEADS):
        pvec_ref[_RK:_RK + 1, HEAD * h:HEAD * (h + 1)] = rk_ref[h:h + 1, :]
    mul_ref[...] = jnp.broadcast_to(mu_ref[:, 3 * RWKV_W:3 * RWKV_W + 2 * LORA], mul_ref.shape)

    zeros = jnp.zeros((LORA, RWKV_W), BF16)
    wd_ref[0:LORA, :] = _bf(wdu_ref[...])
    wd_ref[LORA:2 * LORA, :] = zeros
    wa_ref[0:LORA, :] = zeros
    wa_ref[LORA:2 * LORA, :] = _bf(wau_ref[...])

    ri = lax.broadcasted_iota(jnp.int32, (LANES, LANES), 0)
    ci = lax.broadcasted_iota(jnp.int32, (LANES, LANES), 1)
    e_ref[...] = jnp.where((ri < HEAD) == (ci < HEAD), 1.0, 0.0).astype(BF16)

    lp_ref[_CW0:_CW0 + CONV_W, :] = cw_ref[...]
    for row, ref in ((_CB, cb_ref), (_GXB, gxb_ref), (_GAB, gab_ref), (_LAM, lam_ref)):
        lp_ref[row:row + 1, :] = ref[...]

    blk = LRU_W // LRU_BLOCKS
    z = jnp.zeros((blk, blk), F32)
    for g in range(LRU_BLOCKS // 2):
        top = jnp.concatenate([gxw_ref[2 * g], z, gaw_ref[2 * g], z], axis=1)
        bot = jnp.concatenate([z, gxw_ref[2 * g + 1], z, gaw_ref[2 * g + 1]], axis=1)
        wg_ref[g] = _bf(jnp.concatenate([top, bot], axis=0))


def _pack_params(l, rwkv_mu, w_decay0, w_decay_up, w_iclr0, w_iclr_up, k_k, k_a, r_k, ln_x_g,
                 ln_x_b, conv_w, conv_b, lru_gx_w, lru_gx_b, lru_ga_w, lru_ga_b, lru_lambda):
    blk = LRU_W // LRU_BLOCKS
    assert 2 * blk == LANES and 2 * LORA == LANES
    row = lambda a: pl.BlockSpec((1, a.shape[-1]), lambda i: (l, 0))
    mat = lambda a: pl.BlockSpec((None,) + a.shape[1:], lambda i: (l,) + (0,) * (a.ndim - 1))
    full = lambda shp: pl.BlockSpec(shp, lambda i: (0,) * len(shp))
    rows = (rwkv_mu, w_decay0, w_iclr0, k_k, k_a, ln_x_g, ln_x_b)
    out_shapes = ((16, RWKV_W, F32), (SUBLANES, LANES, F32), (LANES, RWKV_W, BF16), (LANES, RWKV_W, BF16),
                  (LANES, LANES, BF16), (SUBLANES, LRU_W, F32))
    outs = tuple(jax.ShapeDtypeStruct(s[:2], s[2]) for s in out_shapes)
    outs += (jax.ShapeDtypeStruct((LRU_BLOCKS // 2, LANES, 2 * LANES), BF16),)
    return pl.pallas_call(
        _pack_params_kernel,
        out_shape=outs,
        grid=(1,),
        in_specs=[row(a) for a in rows] + [mat(r_k), mat(w_decay_up), mat(w_iclr_up), mat(conv_w),
                                          row(conv_b), row(lru_gx_b), row(lru_ga_b), row(lru_lambda),
                                          mat(lru_gx_w), mat(lru_ga_w)],
        out_specs=tuple(full(o.shape) for o in outs),
        compiler_params=pltpu.CompilerParams(
            dimension_semantics=("arbitrary",), vmem_limit_bytes=VMEM_LIMIT),
        name="pack_params",
    )(*rows, r_k, w_decay_up, w_iclr_up, conv_w, conv_b, lru_gx_b, lru_ga_b, lru_lambda, lru_gx_w,
      lru_ga_w)


def kernel(x_prompt, x_sample, state_shift, state_wkv, state_conv, state_lru, norm_g, w_in, rwkv_mu,
           w_decay0, w_decay_up, w_iclr0, w_iclr_up, k_k, k_a, r_k, ln_x_g, ln_x_b, w_out_rwkv,
           conv_w, conv_b, lru_gx_w, lru_gx_b, lru_ga_w, lru_ga_b, lru_lambda, w_out_lru, w_out,
           final_norm_g):
    bp, seq, d = x_prompt.shape
    bs = x_sample.shape[0]
    assert x_sample.shape[1] == 1 and seq % WKV_CHUNK == 0
    depth = w_in.shape[0]
    xp = x_prompt.reshape(bp * seq, d)
    xs = x_sample.reshape(bs, d)
    fg = final_norm_g.reshape(1, d)
    outs = [[] for _ in range(8)]
    for l in range(depth):
        pvec, mul, wd, wa, e, lp, wg = _pack_params(
            l, rwkv_mu, w_decay0, w_decay_up, w_iclr0, w_iclr_up, k_k, k_a, r_k, ln_x_g, ln_x_b,
            conv_w, conv_b, lru_gx_w, lru_gx_b, lru_ga_w, lru_ga_b, lru_lambda)
        g = norm_g[l].reshape(1, d)
        rec = (pvec, mul, wd, wa, e)
        w, wl, zs, zls, w_r, w_g, w_o = _inproj_head(xs, g, w_in, w_out_rwkv, w_out_lru, w_out, l, INPROJ_TN)
        zp, zlp = _inproj(xp, g, w, wl, _row_tile(bp * seq, 1024), INPROJ_WIDE_TN)

        s0t = jnp.transpose(state_wkv[l], (1, 2, 3, 0))
        o_rs, s_new, sh_new = _wkv_step(zs, zls, state_shift[l], s0t, *rec)
        conv = jnp.transpose(state_conv[l], (1, 0, 2))
        o_gs, h_new, conv_new = _lru_step(zs, conv, state_lru[l], lp, wg)
        outs[4].append(sh_new)
        outs[5].append(jnp.transpose(s_new, (3, 0, 1, 2)))
        outs[6].append(jnp.transpose(conv_new, (1, 0, 2)))
        outs[7].append(h_new)

        zp3 = zp.reshape(bp, seq, -1)
        zlp3 = zlp.reshape(bp, seq, LANES)
        nb = max(n for n in (4, 2, 1) if bp % n == 0)
        o_r, s_new, sh_last = _wkv_chunk(zp3, zlp3, *rec, bp, seq, nb)
        o_r = o_r.reshape(bp * seq, RWKV_W)
        last = l == depth - 1
        xp, h_last, conv_last, xs = _outproj_lru(xp, o_r, zp, xs, o_rs, o_gs, zs, lp, wg, w_r, w_g, w_o,
                                                 fg, _row_tile(seq, 256), seq, last)
        outs[0].append(sh_last.reshape(bp, -1))
        outs[1].append(s_new)
        outs[2].append(jnp.transpose(conv_last, (1, 0, 2)))
        outs[3].append(h_last)

    return (xp.reshape(bp, seq, d), xs.reshape(bs, 1, d)) + tuple(jnp.stack(o) for o in outs)
```

```python
import functools

import jax
import jax.numpy as jnp
from jax import lax
from jax.experimental import pallas as pl
from jax.experimental.pallas import tpu as pltpu

F32 = jnp.float32
BF16 = jnp.bfloat16

HEADS = 16
HEAD = 64
RWKV_W = HEADS * HEAD
LORA = 64
LRU_W = 1024
LRU_BLOCKS = 16
CONV_W = 4
LRU_C = 8.0
RMS_EPS = 1e-6
GN_EPS = 1e-5 * HEAD
DECAY_SCALE = 0.6065306597126334

LANES = 128
SUBLANES = 8
WKV_CHUNK = 64
STEP_HEADS = 2
VMEM_LIMIT = 60 * 1024 * 1024

NN = (((1,), (0,)), ((), ()))
NT = (((1,), (1,)), ((), ()))
TN = (((0,), (0,)), ((), ()))


def _bf(x):
    return x.astype(BF16)


def _dg(a, b, dn):
    return lax.dot_general(a, b, dn, preferred_element_type=F32)


def _softplus(x):
    return jnp.maximum(x, 0.0) + jnp.log1p(jnp.exp(-jnp.abs(x)))


def _sigmoid(x):
    return 1.0 / (1.0 + jnp.exp(-x))


def _segsum(x, e):
    rows, n = x.shape[0], x.shape[1] // LANES
    stacked = jnp.concatenate([x[:, LANES * j:LANES * (j + 1)] for j in range(n)], axis=0)
    s = _dg(_bf(stacked), e, NN)
    return jnp.concatenate([s[rows * j:rows * (j + 1), :] for j in range(n)], axis=1)


def _rms(x, g):
    return x * lax.rsqrt(jnp.mean(x * x, axis=-1, keepdims=True) + RMS_EPS) * g


SHIFT_MAIN = 3 * RWKV_W
LORA_COL = 10 * RWKV_W
LORA_BLOCK = LORA_COL // LANES
FUSE_PIECES = 8
INPROJ_TN = 1024
INPROJ_WIDE_TN = 2048
NORM_ROWS = 128
OUT_W_STEPS = 8


def _inproj_head_kernel(xs_ref, g_ref, w_ref, wt_ref, wlo_ref, wr_ref, wg_ref, wo_ref, wb_ref, wl_ref,
                        zs_ref, zls_ref, wrb_ref, wgb_ref, wob_ref, hs_ref):
    j = pl.program_id(0)
    tn = w_ref.shape[-1]

    @pl.when(j == 0)
    def _():
        hs_ref[...] = _bf(_rms(xs_ref[...], g_ref[...]))
        wl_ref[...] = _bf(wlo_ref[...])
        zls_ref[...] = _dg(hs_ref[...], wl_ref[...], NN)

    @pl.when(j < SHIFT_MAIN // tn)
    def _():
        wb_ref[...] = _bf(w_ref[...])

    @pl.when(j >= SHIFT_MAIN // tn)
    def _():
        wb_ref[:, 0:tn - 2 * LORA] = _bf(w_ref[:, 2 * LORA:tn])
        wb_ref[:, tn - 2 * LORA:tn] = _bf(wt_ref[...])

    zs_ref[...] = _dg(hs_ref[...], wb_ref[...], NN)

    @pl.when(j < OUT_W_STEPS)
    def _():
        wrb_ref[...] = _bf(wr_ref[...])
        wgb_ref[...] = _bf(wg_ref[...])
        wob_ref[...] = _bf(wo_ref[...])


def _inproj_head(xs, g, w_in, w_out_rwkv, w_out_lru, w_out, layer, tn):
    ms, d = xs.shape
    n = w_in.shape[-1]
    assert n == LORA_COL + 2 * LORA and SHIFT_MAIN % tn == 0 and 2 * LORA == LANES
    nj = LORA_COL // tn
    assert nj >= OUT_W_STEPS
    lanes_per_tile = tn // LANES
    one = lambda shp, imap: pl.BlockSpec(shp, imap, pipeline_mode=pl.Buffered(1))
    rows = lambda a: a.shape[1] // OUT_W_STEPS
    step = lambda j: jnp.minimum(j, OUT_W_STEPS - 1)
    w_outs = (w_out_rwkv, w_out_lru, w_out)
    return pl.pallas_call(
        _inproj_head_kernel,
        out_shape=(jax.ShapeDtypeStruct((d, LORA_COL), BF16), jax.ShapeDtypeStruct((d, LANES), BF16),
                   jax.ShapeDtypeStruct((ms, LORA_COL), F32), jax.ShapeDtypeStruct((ms, LANES), F32))
        + tuple(jax.ShapeDtypeStruct(a.shape[1:], BF16) for a in w_outs),
        grid=(nj,),
        in_specs=[
            one((ms, d), lambda j: (0, 0)),
            one((1, d), lambda j: (0, 0)),
            pl.BlockSpec((None, d, tn), lambda j: (layer, 0, j)),
            pl.BlockSpec((None, d, LANES), lambda j: (layer, 0, (j + 1) * lanes_per_tile)),
            one((None, d, LANES), lambda j: (layer, 0, SHIFT_MAIN // LANES)),
        ] + [pl.BlockSpec((None, rows(a), a.shape[2]), lambda j: (layer, step(j), 0)) for a in w_outs],
        out_specs=(
            pl.BlockSpec((d, tn), lambda j: (0, j)),
            pl.BlockSpec((d, LANES), lambda j: (0, 0)),
            pl.BlockSpec((ms, tn), lambda j: (0, j)),
            pl.BlockSpec((ms, LANES), lambda j: (0, 0)),
        ) + tuple(pl.BlockSpec((rows(a), a.shape[2]), lambda j: (step(j), 0)) for a in w_outs),
        scratch_shapes=[pltpu.VMEM((ms, d), BF16)],
        compiler_params=pltpu.CompilerParams(
            dimension_semantics=("arbitrary",), vmem_limit_bytes=VMEM_LIMIT),
        name="inproj_head",
    )(xs, g, w_in, w_in, w_in, *w_outs)


def _inproj_kernel(x_ref, g_ref, w_ref, wl_ref, z_ref, zl_ref, h_ref):
    @pl.when(pl.program_id(1) == 0)
    def _():
        for r in range(0, x_ref.shape[0], NORM_ROWS):
            rows = slice(r, r + NORM_ROWS)
            h_ref[rows, :] = _bf(_rms(x_ref[rows, :], g_ref[...]))
        zl_ref[...] = _dg(h_ref[...], wl_ref[...], NN)

    z_ref[...] = _dg(h_ref[...], w_ref[...], NN)


def _inproj(x, g, w, wl, tm, tn):
    m, d = x.shape
    return pl.pallas_call(
        _inproj_kernel,
        out_shape=(jax.ShapeDtypeStruct((m, LORA_COL), F32), jax.ShapeDtypeStruct((m, LANES), F32)),
        grid=(m // tm, LORA_COL // tn),
        in_specs=[
            pl.BlockSpec((tm, d), lambda i, j: (i, 0)),
            pl.BlockSpec((1, d), lambda i, j: (0, 0), pipeline_mode=pl.Buffered(1)),
            pl.BlockSpec((d, tn), lambda i, j: (0, j)),
            pl.BlockSpec((d, LANES), lambda i, j: (0, 0), pipeline_mode=pl.Buffered(1)),
        ],
        out_specs=(
            pl.BlockSpec((tm, tn), lambda i, j: (i, j)),
            pl.BlockSpec((tm, LANES), lambda i, j: (i, 0)),
        ),
        scratch_shapes=[pltpu.VMEM((tm, d), BF16)],
        compiler_params=pltpu.CompilerParams(
            dimension_semantics=("arbitrary", "arbitrary"), vmem_limit_bytes=VMEM_LIMIT),
        name="inproj",
    )(x, g, w, wl)


_MU_R, _MU_K, _MU_V, _W0, _A0, _KK, _KA, _RK, _LNG, _LNB = range(10)


def _prow(pv_ref, i):
    return pv_ref[i:i + 1, :]


def _wkv_prep(zr, zk, zv, zl, pr, pk, pv, pl_, pv_ref, mul_ref, wd_ref, wa_ref, e):
    r = zr + _prow(pv_ref, _MU_R) * (pr - zr)
    k = zk + _prow(pv_ref, _MU_K) * (pk - zk)
    v = zv + _prow(pv_ref, _MU_V) * (pv - zv)
    lo = zl + mul_ref[0:1, :] * (pl_ - zl)
    lw = _dg(_bf(jnp.tanh(lo)), wd_ref[...], NN)
    la = _dg(_bf(lo), wa_ref[...], NN)
    logd = -DECAY_SCALE * _sigmoid(_prow(pv_ref, _W0) + lw)
    a = _sigmoid(_prow(pv_ref, _A0) + la)
    kk = k * _prow(pv_ref, _KK)
    kk = kk * lax.rsqrt(jnp.maximum(_segsum(kk * kk, e), 1e-24))
    k2 = k * (1.0 + (a - 1.0) * _prow(pv_ref, _KA))
    return r, k2, v, -kk, kk * a, logd


def _wkv_bonus_gate(r, k2, v, zrg, pv_ref, e):
    return _segsum(r * k2 * _prow(pv_ref, _RK), e) * v, zrg * _sigmoid(zrg)


def _wkv_norm_gate(y, bonus_v, gate, pv_ref, e):
    mu = _segsum(y, e) * (1.0 / HEAD)
    yc = y - mu
    var = _segsum(yc * yc, e) * (1.0 / HEAD)
    yn = yc * lax.rsqrt(var + GN_EPS) * _prow(pv_ref, _LNG) + _prow(pv_ref, _LNB)
    return _bf((yn + bonus_v) * gate)


def _wkv_post(y, r, k2, v, zrg, pv_ref, e):
    bonus_v, gate = _wkv_bonus_gate(r, k2, v, zrg, pv_ref, e)
    return _wkv_norm_gate(y, bonus_v, gate, pv_ref, e)


def _wkv_chunk_kernel(zr_ref, zk_ref, zv_ref, zrg_ref, zl_ref, pv_ref, mul_ref, wd_ref, wa_ref,
                      e_ref, o_ref, sout_ref, nsh_ref, s_s, prev_s, prevl_s):
    c = pl.program_id(1)
    nc = pl.num_programs(1)
    C = WKV_CHUNK
    assert C == HEAD and 2 * HEAD == LANES
    nb = zr_ref.shape[0]
    rows_all = nb * C
    seqs = range(nb)

    @pl.when(c == 0)
    def _():
        s_s[...] = jnp.zeros_like(s_s)
        prev_s[...] = jnp.zeros_like(prev_s)
        prevl_s[...] = jnp.zeros_like(prevl_s)

    first = lax.broadcasted_iota(jnp.int32, (SUBLANES, 1), 0) == 0

    def shifted(z, prev_ref, lanes):
        rolled = pltpu.roll(z, 1, 0)
        pieces = []
        for b in seqs:
            head = jnp.where(first, prev_ref[b, 0:1, lanes], rolled[b * C:b * C + SUBLANES, :])
            pieces += [head, rolled[b * C + SUBLANES:(b + 1) * C, :]]
        return jnp.concatenate(pieces, axis=0)

    def flat(ref):
        return ref[...].reshape(rows_all, ref.shape[-1])

    zr, zk, zv, zl = flat(zr_ref), flat(zk_ref), flat(zv_ref), flat(zl_ref)
    seg = [slice(RWKV_W * i, RWKV_W * (i + 1)) for i in range(3)]
    pr = shifted(zr, prev_s, seg[0])
    pk = shifted(zk, prev_s, seg[1])
    pv = shifted(zv, prev_s, seg[2])
    pl_ = shifted(zl, prevl_s, slice(0, LANES))
    for b in seqs:
        last = slice(b * C + C - 1, b * C + C)
        prev_s[b, 0:1, seg[0]] = zr[last, :]
        prev_s[b, 0:1, seg[1]] = zk[last, :]
        prev_s[b, 0:1, seg[2]] = zv[last, :]
        prevl_s[b, 0:1, :] = zl[last, :]

    e = e_ref[...]
    r, k2, v, av, bv, logd = _wkv_prep(zr, zk, zv, zl, pr, pk, pv, pl_, pv_ref, mul_ref,
                                       wd_ref, wa_ref, e)

    ti = lax.broadcasted_iota(jnp.int32, (rows_all, rows_all), 0)
    tj = lax.broadcasted_iota(jnp.int32, (rows_all, rows_all), 1)
    tri = jnp.where((ti >= tj) & ((ti & -C) == (tj & -C)), 1.0, 0.0).astype(BF16)
    d_hi = _bf(logd)
    d_r1 = logd - d_hi.astype(F32)
    d_mid = _bf(d_r1)
    d_lo = _bf(d_r1 - d_mid.astype(F32))
    cum = _dg(jnp.concatenate([tri, tri, tri], axis=1), jnp.concatenate([d_hi, d_mid, d_lo], axis=0), NN)
    e_in = jnp.exp(cum)
    e_neg = jnp.exp(-cum)
    a_t = av * jnp.exp(cum - logd)
    r_t = r * e_in
    k_t = k2 * e_neg
    b_t = bv * e_neg
    p_c = [jnp.exp(cum[b * C + C - 1:b * C + C, :]) for b in seqs]

    lane = lax.broadcasted_iota(jnp.int32, (C, LANES), 1)
    trow = lax.broadcasted_iota(jnp.int32, (C, LANES), 0)
    lo = lane < HEAD
    s_in = lane & (HEAD - 1)
    strict = s_in < trow
    incl2 = ((lax.broadcasted_iota(jnp.int32, (C, 2 * LANES), 1) & (HEAD - 1))
             <= lax.broadcasted_iota(jnp.int32, (C, 2 * LANES), 0))
    eye2 = jnp.where(s_in == trow, 1.0, 0.0).astype(F32)
    vrow = lax.broadcasted_iota(jnp.int32, (2 * HEAD, LANES), 0)
    klane = lax.broadcasted_iota(jnp.int32, (2 * HEAD, LANES), 1)
    same_head = (vrow < HEAD) == (klane < HEAD)

    def bd(x):
        z = jnp.zeros_like(x)
        return jnp.concatenate([jnp.where(lo, x, z), jnp.where(lo, z, x)], axis=0)

    npair = HEADS // 2
    units = [(b, p) for b in seqs for p in range(npair)]
    un = range(len(units))
    blk = lambda arr, i: arr[units[i][0] * C:(units[i][0] + 1) * C, LANES * units[i][1]:LANES * (units[i][1] + 1)]
    ar = [_bf(jnp.concatenate([blk(a_t, i), blk(r_t, i)], axis=0)) for i in un]
    bk = [_bf(jnp.concatenate([bd(blk(b_t, i)), bd(blk(k_t, i))], axis=0)) for i in un]
    g = [_dg(ar[i], bk[i], NT) for i in un]
    s0 = [s_s[i] for i in un]
    ars = [_dg(ar[i], _bf(s0[i]), NT) for i in un]
    vbd = [_bf(bd(blk(v, i))) for i in un]
    x = [jnp.where(strict, g[i][0:C, 0:LANES], 0.0) for i in un]
    ak = [jnp.where(strict, g[i][0:C, LANES:2 * LANES], 0.0) for i in un]
    w = [ars[i][0:C, :] + _dg(_bf(ak[i]), vbd[i], NN) for i in un]
    t = [eye2 + x[i] for i in un]
    x = [_dg(_bf(x[i]), _bf(bd(x[i])), NN) for i in un]
    for _ in range(C.bit_length() - 3):
        xt = [_dg(_bf(jnp.concatenate([x[i], t[i]], axis=0)), _bf(bd(x[i])), NN) for i in un]
        x = [xt[i][0:C, :] for i in un]
        t = [t[i] + xt[i][C:2 * C, :] for i in un]
    t = [t[i] + _dg(_bf(t[i]), _bf(bd(x[i])), NN) for i in un]
    u = [_dg(_bf(t[i]), _bf(bd(w[i])), NN) for i in un]
    rbk = [_bf(jnp.where(incl2, g[i][C:2 * C, :], 0.0)) for i in un]
    uvbd = [jnp.concatenate([_bf(bd(u[i])), vbd[i]], axis=0) for i in un]
    y = [ars[i][C:2 * C, :] + _dg(rbk[i], uvbd[i], NN) for i in un]
    uv = [_bf(jnp.concatenate([u[i], blk(v, i)], axis=0)) for i in un]
    pc = [p_c[units[i][0]][:, LANES * units[i][1]:LANES * (units[i][1] + 1)] for i in un]
    bkh = [_bf(jnp.concatenate([blk(b_t, i), blk(k_t, i)], axis=0) * pc[i]) for i in un]
    s1 = [s0[i] * pc[i] + jnp.where(same_head, _dg(uv[i], bkh[i], TN), 0.0) for i in un]
    for i in un:
        s_s[i] = s1[i]

    y_all = jnp.concatenate(
        [jnp.concatenate(y[b * npair:(b + 1) * npair], axis=1) for b in seqs], axis=0)
    o = _wkv_post(y_all, r, k2, v, flat(zrg_ref), pv_ref, e)
    o_ref[...] = o.reshape(nb, C, RWKV_W)

    @pl.when(c == nc - 1)
    def _():
        for i in un:
            b, p = units[i]
            sout_ref[b, 2 * p] = s1[i][0:HEAD, 0:HEAD]
            sout_ref[b, 2 * p + 1] = s1[i][HEAD:2 * HEAD, HEAD:2 * HEAD]
        for b in seqs:
            for q, ref in enumerate((zr_ref, zk_ref, zv_ref)):
                nsh_ref[0, b:b + 1, RWKV_W * q:RWKV_W * (q + 1)] = ref[b, C - 1:C, :]
            nsh_ref[0, b:b + 1, SHIFT_MAIN:SHIFT_MAIN + LANES] = zl_ref[b, C - 1:C, :]


def _wkv_chunk(z, zl, pvec, mul, wd, wa, e, batch, seq, nb):
    C = WKV_CHUNK
    nc = seq // C
    full = lambda shp: pl.BlockSpec(shp, lambda b, c: (0,) * len(shp))
    col = lambda j: pl.BlockSpec((nb, C, RWKV_W), lambda b, c, j=j: (b, c, j))
    return pl.pallas_call(
        _wkv_chunk_kernel,
        out_shape=(jax.ShapeDtypeStruct((batch, seq, RWKV_W), BF16),
                   jax.ShapeDtypeStruct((batch, HEADS, HEAD, HEAD), F32),
                   jax.ShapeDtypeStruct((batch // nb, nb, SHIFT_MAIN + LANES), F32)),
        grid=(batch // nb, nc),
        in_specs=[col(0), col(1), col(2), col(3),
                  pl.BlockSpec((nb, C, LANES), lambda b, c: (b, c, 0)),
                  full(pvec.shape), full(mul.shape), full(wd.shape), full(wa.shape), full(e.shape)],
        out_specs=(pl.BlockSpec((nb, C, RWKV_W), lambda b, c: (b, c, 0)),
                   pl.BlockSpec((nb, HEADS, HEAD, HEAD), lambda b, c: (b, 0, 0, 0)),
                   pl.BlockSpec((1, nb, SHIFT_MAIN + LANES), lambda b, c: (b, 0, 0))),
        scratch_shapes=[pltpu.VMEM((nb * HEADS // 2, 2 * HEAD, 2 * HEAD), F32),
                        pltpu.VMEM((nb, SUBLANES, 3 * RWKV_W), F32),
                        pltpu.VMEM((nb, SUBLANES, LANES), F32)],
        compiler_params=pltpu.CompilerParams(
            dimension_semantics=("arbitrary", "arbitrary"), vmem_limit_bytes=VMEM_LIMIT),
        name="wkv_chunk",
    )(z, z, z, z, zl, pvec, mul, wd, wa, e)


def _wkv_step_kernel(zr_ref, zk_ref, zv_ref, zrg_ref, zl_ref, sh_ref, s0_ref, pv_ref,
                     mul_ref, wd_ref, wa_ref, e_ref, o_ref, sout_ref, nsh_ref,
                     at_s, drt_s, bt_s, kt_s, dt_s, vt_s, brt_s, krt_s, yt_s, keep_s):
    h = pl.program_id(0)
    nh = pl.num_programs(0)
    nseq = zr_ref.shape[0]

    @pl.when(h == 0)
    def _():
        e = e_ref[...]
        r, k2, v, av, bv, logd = _wkv_prep(
            zr_ref[...], zk_ref[...], zv_ref[...], zl_ref[...],
            sh_ref[:, 0:RWKV_W], sh_ref[:, RWKV_W:2 * RWKV_W], sh_ref[:, 2 * RWKV_W:3 * RWKV_W],
            sh_ref[:, SHIFT_MAIN:SHIFT_MAIN + LANES], pv_ref, mul_ref, wd_ref, wa_ref, e)
        nsh_ref[:, 0:RWKV_W] = zr_ref[...]
        nsh_ref[:, RWKV_W:2 * RWKV_W] = zk_ref[...]
        nsh_ref[:, 2 * RWKV_W:SHIFT_MAIN] = zv_ref[...]
        nsh_ref[:, SHIFT_MAIN:SHIFT_MAIN + LANES] = zl_ref[...]
        d = jnp.exp(logd)
        at_s[...] = av.T
        drt_s[...] = (d * r).T
        bt_s[...] = bv.T
        kt_s[...] = k2.T
        dt_s[...] = d.T
        vt_s[...] = v.T
        brt_s[...] = jnp.sum((bv * r).T.reshape(HEADS, HEAD, nseq), axis=1)
        krt_s[...] = jnp.sum((k2 * r).T.reshape(HEADS, HEAD, nseq), axis=1)
        keep_s[0] = r
        keep_s[1] = k2
        keep_s[2] = v

    for u in range(s0_ref.shape[0]):
        head = h * s0_ref.shape[0] + u
        base = pl.multiple_of(head * HEAD, HEAD)
        rows = pl.ds(base, HEAD)
        a_h, dr_h, b_h, k_h, d_h = at_s[rows, :], drt_s[rows, :], bt_s[rows, :], kt_s[rows, :], dt_s[rows, :]
        br_h = brt_s[pl.ds(head, 1), :]
        kr_h = krt_s[pl.ds(head, 1), :]

        def value_rows(g, carry, u=u, base=base, a_h=a_h, dr_h=dr_h, b_h=b_h, k_h=k_h, d_h=d_h,
                       br_h=br_h, kr_h=kr_h):
            off = pl.multiple_of(base + g * SUBLANES, SUBLANES)
            v8 = vt_s[pl.ds(off, SUBLANES), :]
            ys = []
            for j in range(SUBLANES):
                vi = g * SUBLANES + j
                s_v = s0_ref[u, vi]
                sa = jnp.sum(s_v * a_h, axis=0, keepdims=True)
                y0 = jnp.sum(s_v * dr_h, axis=0, keepdims=True)
                v_v = v8[j:j + 1, :]
                sout_ref[u, vi] = s_v * d_h + sa * b_h + v_v * k_h
                ys.append(y0 + sa * br_h + v_v * kr_h)
            yt_s[pl.ds(off, SUBLANES), :] = jnp.concatenate(ys, axis=0)
            return carry

        lax.fori_loop(0, HEAD // SUBLANES, value_rows, 0)

    @pl.when(h == nh - 1)
    def _():
        o_ref[...] = _wkv_post(yt_s[...].T, keep_s[0], keep_s[1], keep_s[2], zrg_ref[...], pv_ref,
                               e_ref[...])


def _wkv_step(z, zl, sh, s0t, pvec, mul, wd, wa, e):
    nseq = z.shape[0]
    full = lambda shp: pl.BlockSpec(shp, lambda i: (0,) * len(shp))
    col = lambda j: pl.BlockSpec((nseq, RWKV_W), lambda i, j=j: (0, j))
    st_block = (STEP_HEADS, HEAD, HEAD, nseq)
    wide = pltpu.VMEM((RWKV_W, nseq), F32)
    return pl.pallas_call(
        _wkv_step_kernel,
        out_shape=(jax.ShapeDtypeStruct((nseq, RWKV_W), BF16),
                   jax.ShapeDtypeStruct(s0t.shape, F32),
                   jax.ShapeDtypeStruct(sh.shape, F32)),
        grid=(HEADS // STEP_HEADS,),
        in_specs=[col(0), col(1), col(2), col(3),
                  full(zl.shape), full(sh.shape),
                  pl.BlockSpec(st_block, lambda i: (i, 0, 0, 0)),
                  full(pvec.shape), full(mul.shape), full(wd.shape), full(wa.shape), full(e.shape)],
        out_specs=(full((nseq, RWKV_W)),
                   pl.BlockSpec(st_block, lambda i: (i, 0, 0, 0)),
                   full(sh.shape)),
        scratch_shapes=[wide] * 6 + [pltpu.VMEM((HEADS, nseq), F32)] * 2
                       + [wide, pltpu.VMEM((3, nseq, RWKV_W), F32)],
        compiler_params=pltpu.CompilerParams(
            dimension_semantics=("arbitrary",), vmem_limit_bytes=VMEM_LIMIT),
        name="wkv_step",
    )(z, z, z, z, zl, sh, s0t, pvec, mul, wd, wa, e)


_CW0, _CW1, _CW2, _CW3, _CB, _GXB, _GAB, _LAM = range(8)


def _lru_gates(xc, lp_ref, wg_ref):
    xb = _bf(xc)
    ngroups = wg_ref.shape[0]
    gs = [_dg(xb[:, LANES * g:LANES * (g + 1)], wg_ref[g], NN) for g in range(ngroups)]
    gx_pre = jnp.concatenate([gs[g][:, 0:LANES] for g in range(ngroups)], axis=1)
    ga_pre = jnp.concatenate([gs[g][:, LANES:2 * LANES] for g in range(ngroups)], axis=1)
    gx = _sigmoid(gx_pre + _prow(lp_ref, _GXB))
    ga = _sigmoid(ga_pre + _prow(lp_ref, _GAB))
    log_a = -LRU_C * ga * _softplus(-_prow(lp_ref, _LAM))
    a = jnp.exp(log_a)
    mult = jnp.sqrt((1.0 - a) * (1.0 + a))
    return a, mult * gx * xc


def _lru_scan_rows(a, b, zg, hc):
    row8 = lax.broadcasted_iota(jnp.int32, (SUBLANES, 1), 0)
    hs = []
    for i in range(a.shape[0] // SUBLANES):
        a8 = a[SUBLANES * i:SUBLANES * (i + 1), :]
        b8 = b[SUBLANES * i:SUBLANES * (i + 1), :]
        for s in (1, 2, 4):
            keep = row8 >= s
            b8 = jnp.where(keep, a8 * pltpu.roll(b8, s, 0) + b8, b8)
            a8 = jnp.where(keep, a8 * pltpu.roll(a8, s, 0), a8)
        hb = b8 + a8 * hc
        hs.append(hb)
        hc = jnp.broadcast_to(hb[SUBLANES - 1:SUBLANES, :], hb.shape)
    return _bf(jnp.concatenate(hs, axis=0) * (zg * _sigmoid(zg))), hc


def _lru_step_kernel(zx_ref, zg_ref, conv_ref, h0_ref, lp_ref, wg_ref, o_ref, hnew_ref, cnew_ref):
    zx = zx_ref[...]
    for j in range(CONV_W - 2):
        cnew_ref[j] = conv_ref[j + 1]
    cnew_ref[CONV_W - 2] = zx
    xc = _prow(lp_ref, _CW3) * zx + _prow(lp_ref, _CB)
    for j in range(CONV_W - 1):
        xc = xc + _prow(lp_ref, j) * conv_ref[j]
    a, b = _lru_gates(xc, lp_ref, wg_ref)
    h = a * h0_ref[...] + b
    hnew_ref[...] = h
    zg = zg_ref[...]
    o_ref[...] = _bf(h * (zg * _sigmoid(zg)))


def _lru_step(z_main, conv, h0, lp, wg):
    nb = z_main.shape[0]
    full = lambda shp: pl.BlockSpec(shp, lambda i: (0,) * len(shp))
    col = lambda j: pl.BlockSpec((nb, LRU_W), lambda i, j=j: (0, j))
    return pl.pallas_call(
        _lru_step_kernel,
        out_shape=(jax.ShapeDtypeStruct((nb, LRU_W), BF16), jax.ShapeDtypeStruct((nb, LRU_W), F32),
                   jax.ShapeDtypeStruct(conv.shape, F32)),
        grid=(1,),
        in_specs=[col(4), col(5), full(conv.shape), full(h0.shape), full(lp.shape), full(wg.shape)],
        out_specs=(full((nb, LRU_W)), full((nb, LRU_W)), full(conv.shape)),
        compiler_params=pltpu.CompilerParams(
            dimension_semantics=("arbitrary",), vmem_limit_bytes=VMEM_LIMIT),
        name="lru_step",
    )(z_main, z_main, conv, h0, lp, wg)


def _project(x, o_r, o_g, m_r, m_g, wr_ref, wg_ref, wo_ref, fg_ref, final):
    y_r = _dg(o_r, wr_ref[...], NN)
    y_g = _dg(o_g, wg_ref[...], NN)
    merged = _sigmoid(m_r) * y_r + _sigmoid(m_g) * y_g
    out = x + _dg(_bf(merged), wo_ref[...], NN)
    return _rms(out, fg_ref[...]) if final else out


def _outproj_lru_kernel(x_ref, or_ref, mr_ref, mg_ref, zx_ref, zg_ref, xs_ref, ors_ref, ogs_ref,
                        mrs_ref, mgs_ref, lp_ref, wgate_ref, wr_ref, wg_ref, wo_ref, fg_ref,
                        out_ref, hlast_ref, cnew_ref, outs_ref, og_s, mg_s, xb_s, hc_s, *, final,
                        tiles_per_seq, n_tiles):
    i = pl.program_id(0)
    n = n_tiles
    tm, d = x_ref.shape

    @pl.when(i == 0)
    def _():
        xb_s[...] = jnp.zeros_like(xb_s)
        hc_s[...] = jnp.zeros_like(hc_s)
        outs_ref[...] = _project(xs_ref[...], ors_ref[...], ogs_ref[...], mrs_ref[...], mgs_ref[...],
                                 wr_ref, wg_ref, wo_ref, fg_ref, final)

    def stages(lru, branch, outp):
        slot = lax.rem(i, 2)
        if outp:
            mg_prev = mg_s[...]
        if lru:
            t = lax.rem(i, tiles_per_seq)
            first = t == 0
            xb_s[0:SUBLANES, :] = jnp.where(first, 0.0, xb_s[0:SUBLANES, :])
            xb_s[SUBLANES:SUBLANES + tm, :] = zx_ref[...]

        outs, a_parts, b_parts = [], [], []
        for c in range(FUSE_PIECES):
            if outp:
                cs = slice(c * d // FUSE_PIECES, (c + 1) * d // FUSE_PIECES)
                outs.append(x_ref[:, cs] + _dg(mg_prev, wo_ref[:, cs], NN))
            if lru:
                r0, r1 = SUBLANES + c * tm // FUSE_PIECES, SUBLANES + (c + 1) * tm // FUSE_PIECES
                xc = _prow(lp_ref, _CW3) * xb_s[r0:r1, :] + _prow(lp_ref, _CB)
                for j in range(1, CONV_W):
                    xc = xc + _prow(lp_ref, CONV_W - 1 - j) * xb_s[r0 - j:r1 - j, :]
                a_c, b_c = _lru_gates(xc, lp_ref, wgate_ref)
                a_parts.append(a_c)
                b_parts.append(b_c)
        if outp:
            out = jnp.concatenate(outs, axis=1)
            out_ref[...] = _rms(out, fg_ref[...]) if final else out
        if lru:
            xb_s[0:SUBLANES, :] = xb_s[tm:tm + SUBLANES, :]
            a, b = jnp.concatenate(a_parts, axis=0), jnp.concatenate(b_parts, axis=0)
            hc = jnp.where(first, 0.0, hc_s[...])
        for c in range(FUSE_PIECES):
            if branch:
                cs = slice(c * d // FUSE_PIECES, (c + 1) * d // FUSE_PIECES)
                y_r = _dg(or_ref[...], wr_ref[:, cs], NN)
                y_g = _dg(og_s[1 - slot], wg_ref[:, cs], NN)
                mg_s[:, cs] = _bf(_sigmoid(mr_ref[:, cs]) * y_r + _sigmoid(mg_ref[:, cs]) * y_g)
            if lru:
                rs = slice(c * tm // FUSE_PIECES, (c + 1) * tm // FUSE_PIECES)
                o_g, hc = _lru_scan_rows(a[rs], b[rs], zg_ref[rs, :], hc)
                og_s[slot, rs, :] = o_g
        if lru:
            hc_s[...] = hc

            @pl.when(t == tiles_per_seq - 1)
            def _():
                seq = i // tiles_per_seq
                hlast_ref[pl.ds(seq, 1), :] = hc[0:1, :]
                for j in range(CONV_W - 1):
                    row = tm - (CONV_W - 1) + j
                    cnew_ref[j, pl.ds(seq, 1), :] = zx_ref[row:row + 1, :]

    active = lambda step, s: 0 <= step - s < n
    holds = lambda s, flag: ((i >= s) & (i < n + s)) == flag
    for combo in sorted({tuple(active(step, s) for s in range(3)) for step in range(n + 2)}):
        lru, branch, outp = combo
        pl.when(holds(0, lru) & holds(1, branch) & holds(2, outp))(
            functools.partial(stages, lru, branch, outp))


def _outproj_lru(x, o_r, z, xs, o_rs, o_gs, zs, lp, wgate, w_r, w_g, w_o, fg, tm, seq, final):
    m, d = x.shape
    ms = xs.shape[0]
    n = m // tm
    tiles_per_seq = seq // tm
    const = lambda shp: pl.BlockSpec(shp, lambda i: (0,) * len(shp), pipeline_mode=pl.Buffered(1))
    back = lambda i, k: jnp.clip(i - k, 0, n - 1)
    return pl.pallas_call(
        functools.partial(_outproj_lru_kernel, final=final, tiles_per_seq=tiles_per_seq, n_tiles=n),
        out_shape=(jax.ShapeDtypeStruct((m, d), F32),
                   jax.ShapeDtypeStruct((m // seq, LRU_W), F32),
                   jax.ShapeDtypeStruct((CONV_W - 1, m // seq, LRU_W), F32),
                   jax.ShapeDtypeStruct((ms, d), F32)),
        grid=(n + 2,),
        in_specs=[
            pl.BlockSpec((tm, d), lambda i: (back(i, 2), 0)),
            pl.BlockSpec((tm, RWKV_W), lambda i: (back(i, 1), 0)),
            pl.BlockSpec((tm, d), lambda i: (back(i, 1), 3)),
            pl.BlockSpec((tm, d), lambda i: (back(i, 1), 4)),
            pl.BlockSpec((tm, LRU_W), lambda i: (back(i, 0), 4)),
            pl.BlockSpec((tm, LRU_W), lambda i: (back(i, 0), 5)),
            const(xs.shape), const(o_rs.shape), const(o_gs.shape),
            pl.BlockSpec((ms, d), lambda i: (0, 3), pipeline_mode=pl.Buffered(1)),
            pl.BlockSpec((ms, d), lambda i: (0, 4), pipeline_mode=pl.Buffered(1)),
            const(lp.shape), const(wgate.shape), const(w_r.shape), const(w_g.shape), const(w_o.shape),
            const(fg.shape),
        ],
        out_specs=(pl.BlockSpec((tm, d), lambda i: (back(i, 2), 0)),
                   pl.BlockSpec((m // seq, LRU_W), lambda i: (0, 0)),
                   pl.BlockSpec((CONV_W - 1, m // seq, LRU_W), lambda i: (0, 0, 0)),
                   pl.BlockSpec((ms, d), lambda i: (0, 0))),
        scratch_shapes=[pltpu.VMEM((2, tm, LRU_W), BF16),
                        pltpu.VMEM((tm, d), BF16),
                        pltpu.VMEM((SUBLANES + tm, LRU_W), F32),
                        pltpu.VMEM((SUBLANES, LRU_W), F32)],
        compiler_params=pltpu.CompilerParams(
            dimension_semantics=("arbitrary",), vmem_limit_bytes=VMEM_LIMIT),
        name="outproj_lru",
    )(x, o_r, z, z, z, z, xs, o_rs, o_gs, zs, zs, lp, wgate, w_r, w_g, w_o, fg)


def _row_tile(m, want):
    t = min(m, want)
    assert m % t == 0, (m, t)
    return t


def _pack_params_kernel(mu_ref, w0_ref, a0_ref, kk_ref, ka_ref, lng_ref, lnb_ref, rk_ref, wdu_ref,
                        wau_ref, cw_ref, cb_ref, gxb_ref, gab_ref, lam_ref, gxw_ref, gaw_ref,
                        pvec_ref, mul_ref, wd_ref, wa_ref, e_ref, lp_ref, wg_ref):
    pvec_ref[...] = jnp.zeros_like(pvec_ref)
    for i in range(3):
        pvec_ref[_MU_R + i:_MU_R + i + 1, :] = mu_ref[:, RWKV_W * i:RWKV_W * (i + 1)]
    for row, ref in ((_W0, w0_ref), (_A0, a0_ref), (_KK, kk_ref), (_KA, ka_ref),
                     (_LNG, lng_ref), (_LNB, lnb_ref)):
        pvec_ref[row:row + 1, :] = ref[...]
    for h in range(HEADS):
        pvec_ref[_RK:_RK + 1, HEAD * h:HEAD * (h + 1)] = rk_ref[h:h + 1, :]
    mul_ref[...] = jnp.broadcast_to(mu_ref[:, 3 * RWKV_W:3 * RWKV_W + 2 * LORA], mul_ref.shape)

    zeros = jnp.zeros((LORA, RWKV_W), BF16)
    wd_ref[0:LORA, :] = _bf(wdu_ref[...])
    wd_ref[LORA:2 * LORA, :] = zeros
    wa_ref[0:LORA, :] = zeros
    wa_ref[LORA:2 * LORA, :] = _bf(wau_ref[...])

    ri = lax.broadcasted_iota(jnp.int32, (LANES, LANES), 0)
    ci = lax.broadcasted_iota(jnp.int32, (LANES, LANES), 1)
    e_ref[...] = jnp.where((ri < HEAD) == (ci < HEAD), 1.0, 0.0).astype(BF16)

    lp_ref[_CW0:_CW0 + CONV_W, :] = cw_ref[...]
    for row, ref in ((_CB, cb_ref), (_GXB, gxb_ref), (_GAB, gab_ref), (_LAM, lam_ref)):
        lp_ref[row:row + 1, :] = ref[...]

    blk = LRU_W // LRU_BLOCKS
    z = jnp.zeros((blk, blk), F32)
    for g in range(LRU_BLOCKS // 2):
        top = jnp.concatenate([gxw_ref[2 * g], z, gaw_ref[2 * g], z], axis=1)
        bot = jnp.concatenate([z, gxw_ref[2 * g + 1], z, gaw_ref[2 * g + 1]], axis=1)
        wg_ref[g] = _bf(jnp.concatenate([top, bot], axis=0))


def _pack_params(l, rwkv_mu, w_decay0, w_decay_up, w_iclr0, w_iclr_up, k_k, k_a, r_k, ln_x_g,
                 ln_x_b, conv_w, conv_b, lru_gx_w, lru_gx_b, lru_ga_w, lru_ga_b, lru_lambda):
    blk = LRU_W // LRU_BLOCKS
    assert 2 * blk == LANES and 2 * LORA == LANES
    row = lambda a: pl.BlockSpec((1, a.shape[-1]), lambda i: (l, 0))
    mat = lambda a: pl.BlockSpec((None,) + a.shape[1:], lambda i: (l,) + (0,) * (a.ndim - 1))
    full = lambda shp: pl.BlockSpec(shp, lambda i: (0,) * len(shp))
    rows = (rwkv_mu, w_decay0, w_iclr0, k_k, k_a, ln_x_g, ln_x_b)
    out_shapes = ((16, RWKV_W, F32), (SUBLANES, LANES, F32), (LANES, RWKV_W, BF16), (LANES, RWKV_W, BF16),
                  (LANES, LANES, BF16), (SUBLANES, LRU_W, F32))
    outs = tuple(jax.ShapeDtypeStruct(s[:2], s[2]) for s in out_shapes)
    outs += (jax.ShapeDtypeStruct((LRU_BLOCKS // 2, LANES, 2 * LANES), BF16),)
    return pl.pallas_call(
        _pack_params_kernel,
        out_shape=outs,
        grid=(1,),
        in_specs=[row(a) for a in rows] + [mat(r_k), mat(w_decay_up), mat(w_iclr_up), mat(conv_w),
                                          row(conv_b), row(lru_gx_b), row(lru_ga_b), row(lru_lambda),
                                          mat(lru_gx_w), mat(lru_ga_w)],
        out_specs=tuple(full(o.shape) for o in outs),
        compiler_params=pltpu.CompilerParams(
            dimension_semantics=("arbitrary",), vmem_limit_bytes=VMEM_LIMIT),
        name="pack_params",
    )(*rows, r_k, w_decay_up, w_iclr_up, conv_w, conv_b, lru_gx_b, lru_ga_b, lru_lambda, lru_gx_w,
      lru_ga_w)


def kernel(x_prompt, x_sample, state_shift, state_wkv, state_conv, state_lru, norm_g, w_in, rwkv_mu,
           w_decay0, w_decay_up, w_iclr0, w_iclr_up, k_k, k_a, r_k, ln_x_g, ln_x_b, w_out_rwkv,
           conv_w, conv_b, lru_gx_w, lru_gx_b, lru_ga_w, lru_ga_b, lru_lambda, w_out_lru, w_out,
           final_norm_g):
    bp, seq, d = x_prompt.shape
    bs = x_sample.shape[0]
    assert x_sample.shape[1] == 1 and seq % WKV_CHUNK == 0
    depth = w_in.shape[0]
    xp = x_prompt.reshape(bp * seq, d)
    xs = x_sample.reshape(bs, d)
    fg = final_norm_g.reshape(1, d)
    outs = [[] for _ in range(8)]
    for l in range(depth):
        pvec, mul, wd, wa, e, lp, wg = _pack_params(
            l, rwkv_mu, w_decay0, w_decay_up, w_iclr0, w_iclr_up, k_k, k_a, r_k, ln_x_g, ln_x_b,
            conv_w, conv_b, lru_gx_w, lru_gx_b, lru_ga_w, lru_ga_b, lru_lambda)
        g = norm_g[l].reshape(1, d)
        rec = (pvec, mul, wd, wa, e)
        w, wl, zs, zls, w_r, w_g, w_o = _inproj_head(xs, g, w_in, w_out_rwkv, w_out_lru, w_out, l, INPROJ_TN)
        zp, zlp = _inproj(xp, g, w, wl, _row_tile(bp * seq, 1024), INPROJ_WIDE_TN)

        s0t = jnp.transpose(state_wkv[l], (1, 2, 3, 0))
        o_rs, s_new, sh_new = _wkv_step(zs, zls, state_shift[l], s0t, *rec)
        conv = jnp.transpose(state_conv[l], (1, 0, 2))
        o_gs, h_new, conv_new = _lru_step(zs, conv, state_lru[l], lp, wg)
        outs[4].append(sh_new)
        outs[5].append(jnp.transpose(s_new, (3, 0, 1, 2)))
        outs[6].append(jnp.transpose(conv_new, (1, 0, 2)))
        outs[7].append(h_new)

        zp3 = zp.reshape(bp, seq, -1)
        zlp3 = zlp.reshape(bp, seq, LANES)
        nb = max(n for n in (4, 2, 1) if bp % n == 0)
        o_r, s_new, sh_last = _wkv_chunk(zp3, zlp3, *rec, bp, seq, nb)
        o_r = o_r.reshape(bp * seq, RWKV_W)
        last = l == depth - 1
        xp, h_last, conv_last, xs = _outproj_lru(xp, o_r, zp, xs, o_rs, o_gs, zs, lp, wg, w_r, w_g, w_o,
                                                 fg, _row_tile(seq, 256), seq, last)
        outs[0].append(sh_last.reshape(bp, -1))
        outs[1].append(s_new)
        outs[2].append(jnp.transpose(conv_last, (1, 0, 2)))
        outs[3].append(h_last)

    return (xp.reshape(bp, seq, d), xs.reshape(bs, 1, d)) + tuple(jnp.stack(o) for o in outs)
```

```python
import functools

import jax
import jax.numpy as jnp
from jax import lax
from jax.experimental import pallas as pl
from jax.experimental.pallas import tpu as pltpu

F32 = jnp.float32
BF16 = jnp.bfloat16

HEADS = 16
HEAD = 64
RWKV_W = HEADS * HEAD
LORA = 64
LRU_W = 1024
LRU_BLOCKS = 16
CONV_W = 4
LRU_C = 8.0
RMS_EPS = 1e-6
GN_EPS = 1e-5 * HEAD
DECAY_SCALE = 0.6065306597126334

LANES = 128
SUBLANES = 8
WKV_CHUNK = 64
STEP_HEADS = 2
VMEM_LIMIT = 60 * 1024 * 1024

NN = (((1,), (0,)), ((), ()))
NT = (((1,), (1,)), ((), ()))
TN = (((0,), (0,)), ((), ()))


def _bf(x):
    return x.astype(BF16)


def _dg(a, b, dn):
    return lax.dot_general(a, b, dn, preferred_element_type=F32)


def _softplus(x):
    return jnp.maximum(x, 0.0) + jnp.log1p(jnp.exp(-jnp.abs(x)))


def _sigmoid(x):
    return 1.0 / (1.0 + jnp.exp(-x))


def _segsum(x, e):
    rows, n = x.shape[0], x.shape[1] // LANES
    stacked = jnp.concatenate([x[:, LANES * j:LANES * (j + 1)] for j in range(n)], axis=0)
    s = _dg(_bf(stacked), e, NN)
    return jnp.concatenate([s[rows * j:rows * (j + 1), :] for j in range(n)], axis=1)


def _rms(x, g):
    return x * lax.rsqrt(jnp.mean(x * x, axis=-1, keepdims=True) + RMS_EPS) * g


SHIFT_MAIN = 3 * RWKV_W
LORA_COL = 10 * RWKV_W
LORA_BLOCK = LORA_COL // LANES
FUSE_PIECES = 8
MERGE_COL = 6 * RWKV_W
MERGE_SLOTS = 3
INPROJ_TN = 1024
INPROJ_WIDE_TN = 2048
NORM_ROWS = 128
OUT_W_STEPS = 8


def _inproj_head_kernel(xs_ref, g_ref, w_ref, wt_ref, wlo_ref, wr_ref, wg_ref, wo_ref, wb_ref, wl_ref,
                        zs_ref, zls_ref, wrb_ref, wgb_ref, wob_ref, hs_ref):
    j = pl.program_id(0)
    tn = w_ref.shape[-1]

    @pl.when(j == 0)
    def _():
        hs_ref[...] = _bf(_rms(xs_ref[...], g_ref[...]))
        wl_ref[...] = _bf(wlo_ref[...])
        zls_ref[...] = _dg(hs_ref[...], wl_ref[...], NN)

    @pl.when(j < SHIFT_MAIN // tn)
    def _():
        wb_ref[...] = _bf(w_ref[...])

    @pl.when(j >= SHIFT_MAIN // tn)
    def _():
        wb_ref[:, 0:tn - 2 * LORA] = _bf(w_ref[:, 2 * LORA:tn])
        wb_ref[:, tn - 2 * LORA:tn] = _bf(wt_ref[...])

    zs_ref[...] = _dg(hs_ref[...], wb_ref[...], NN)

    @pl.when(j < OUT_W_STEPS)
    def _():
        wrb_ref[...] = _bf(wr_ref[...])
        wgb_ref[...] = _bf(wg_ref[...])
        wob_ref[...] = _bf(wo_ref[...])


def _inproj_head(xs, g, w_in, w_out_rwkv, w_out_lru, w_out, layer, tn):
    ms, d = xs.shape
    n = w_in.shape[-1]
    assert n == LORA_COL + 2 * LORA and SHIFT_MAIN % tn == 0 and 2 * LORA == LANES
    nj = LORA_COL // tn
    assert nj >= OUT_W_STEPS
    lanes_per_tile = tn // LANES
    one = lambda shp, imap: pl.BlockSpec(shp, imap, pipeline_mode=pl.Buffered(1))
    rows = lambda a: a.shape[1] // OUT_W_STEPS
    step = lambda j: jnp.minimum(j, OUT_W_STEPS - 1)
    w_outs = (w_out_rwkv, w_out_lru, w_out)
    return pl.pallas_call(
        _inproj_head_kernel,
        out_shape=(jax.ShapeDtypeStruct((d, LORA_COL), BF16), jax.ShapeDtypeStruct((d, LANES), BF16),
                   jax.ShapeDtypeStruct((ms, LORA_COL), F32), jax.ShapeDtypeStruct((ms, LANES), F32))
        + tuple(jax.ShapeDtypeStruct(a.shape[1:], BF16) for a in w_outs),
        grid=(nj,),
        in_specs=[
            one((ms, d), lambda j: (0, 0)),
            one((1, d), lambda j: (0, 0)),
            pl.BlockSpec((None, d, tn), lambda j: (layer, 0, j)),
            pl.BlockSpec((None, d, LANES), lambda j: (layer, 0, (j + 1) * lanes_per_tile)),
            one((None, d, LANES), lambda j: (layer, 0, SHIFT_MAIN // LANES)),
        ] + [pl.BlockSpec((None, rows(a), a.shape[2]), lambda j: (layer, step(j), 0)) for a in w_outs],
        out_specs=(
            pl.BlockSpec((d, tn), lambda j: (0, j)),
            pl.BlockSpec((d, LANES), lambda j: (0, 0)),
            pl.BlockSpec((ms, tn), lambda j: (0, j)),
            pl.BlockSpec((ms, LANES), lambda j: (0, 0)),
        ) + tuple(pl.BlockSpec((rows(a), a.shape[2]), lambda j: (step(j), 0)) for a in w_outs),
        scratch_shapes=[pltpu.VMEM((ms, d), BF16)],
        compiler_params=pltpu.CompilerParams(
            dimension_semantics=("arbitrary",), vmem_limit_bytes=VMEM_LIMIT),
        name="inproj_head",
    )(xs, g, w_in, w_in, w_in, *w_outs)


def _inproj_kernel(x_ref, g_ref, w_ref, wl_ref, z_ref, zl_ref, h_ref):
    @pl.when(pl.program_id(1) == 0)
    def _():
        for r in range(0, x_ref.shape[0], NORM_ROWS):
            rows = slice(r, r + NORM_ROWS)
            h_ref[rows, :] = _bf(_rms(x_ref[rows, :], g_ref[...]))
        zl_ref[...] = _dg(h_ref[...], wl_ref[...], NN)

    z_ref[...] = _dg(h_ref[...], w_ref[...], NN)


def _inproj(x, g, w, wl, tm, tn):
    m, d = x.shape
    return pl.pallas_call(
        _inproj_kernel,
        out_shape=(jax.ShapeDtypeStruct((m, LORA_COL), F32), jax.ShapeDtypeStruct((m, LANES), F32)),
        grid=(m // tm, LORA_COL // tn),
        in_specs=[
            pl.BlockSpec((tm, d), lambda i, j: (i, 0)),
            pl.BlockSpec((1, d), lambda i, j: (0, 0), pipeline_mode=pl.Buffered(1)),
            pl.BlockSpec((d, tn), lambda i, j: (0, j)),
            pl.BlockSpec((d, LANES), lambda i, j: (0, 0), pipeline_mode=pl.Buffered(1)),
        ],
        out_specs=(
            pl.BlockSpec((tm, tn), lambda i, j: (i, j)),
            pl.BlockSpec((tm, LANES), lambda i, j: (i, 0)),
        ),
        scratch_shapes=[pltpu.VMEM((tm, d), BF16)],
        compiler_params=pltpu.CompilerParams(
            dimension_semantics=("arbitrary", "arbitrary"), vmem_limit_bytes=VMEM_LIMIT),
        name="inproj",
    )(x, g, w, wl)


_MU_R, _MU_K, _MU_V, _W0, _A0, _KK, _KA, _RK, _LNG, _LNB = range(10)


def _prow(pv_ref, i):
    return pv_ref[i:i + 1, :]


def _wkv_prep(zr, zk, zv, zl, pr, pk, pv, pl_, pv_ref, mul_ref, wd_ref, wa_ref, e):
    r = zr + _prow(pv_ref, _MU_R) * (pr - zr)
    k = zk + _prow(pv_ref, _MU_K) * (pk - zk)
    v = zv + _prow(pv_ref, _MU_V) * (pv - zv)
    lo = zl + mul_ref[0:1, :] * (pl_ - zl)
    lw = _dg(_bf(jnp.tanh(lo)), wd_ref[...], NN)
    la = _dg(_bf(lo), wa_ref[...], NN)
    logd = -DECAY_SCALE * _sigmoid(_prow(pv_ref, _W0) + lw)
    a = _sigmoid(_prow(pv_ref, _A0) + la)
    kk = k * _prow(pv_ref, _KK)
    kk = kk * lax.rsqrt(jnp.maximum(_segsum(kk * kk, e), 1e-24))
    k2 = k * (1.0 + (a - 1.0) * _prow(pv_ref, _KA))
    return r, k2, v, -kk, kk * a, logd


def _wkv_bonus_gate(r, k2, v, zrg, pv_ref, e):
    return _segsum(r * k2 * _prow(pv_ref, _RK), e) * v, zrg * _sigmoid(zrg)


def _wkv_norm_gate(y, bonus_v, gate, pv_ref, e):
    mu = _segsum(y, e) * (1.0 / HEAD)
    yc = y - mu
    var = _segsum(yc * yc, e) * (1.0 / HEAD)
    yn = yc * lax.rsqrt(var + GN_EPS) * _prow(pv_ref, _LNG) + _prow(pv_ref, _LNB)
    return _bf((yn + bonus_v) * gate)


def _wkv_post(y, r, k2, v, zrg, pv_ref, e):
    bonus_v, gate = _wkv_bonus_gate(r, k2, v, zrg, pv_ref, e)
    return _wkv_norm_gate(y, bonus_v, gate, pv_ref, e)


def _wkv_chunk_kernel(zr_ref, zk_ref, zv_ref, zrg_ref, zl_ref, pv_ref, mul_ref, wd_ref, wa_ref,
                      e_ref, o_ref, sout_ref, nsh_ref, s_s, prev_s, prevl_s):
    c = pl.program_id(1)
    nc = pl.num_programs(1)
    C = WKV_CHUNK
    assert C == HEAD and 2 * HEAD == LANES
    nb = zr_ref.shape[0]
    rows_all = nb * C
    seqs = range(nb)

    @pl.when(c == 0)
    def _():
        s_s[...] = jnp.zeros_like(s_s)
        prev_s[...] = jnp.zeros_like(prev_s)
        prevl_s[...] = jnp.zeros_like(prevl_s)

    first = lax.broadcasted_iota(jnp.int32, (SUBLANES, 1), 0) == 0

    def shifted(z, prev_ref, lanes):
        rolled = pltpu.roll(z, 1, 0)
        pieces = []
        for b in seqs:
            head = jnp.where(first, prev_ref[b, 0:1, lanes], rolled[b * C:b * C + SUBLANES, :])
            pieces += [head, rolled[b * C + SUBLANES:(b + 1) * C, :]]
        return jnp.concatenate(pieces, axis=0)

    def flat(ref):
        return ref[...].reshape(rows_all, ref.shape[-1])

    zr, zk, zv, zl = flat(zr_ref), flat(zk_ref), flat(zv_ref), flat(zl_ref)
    seg = [slice(RWKV_W * i, RWKV_W * (i + 1)) for i in range(3)]
    pr = shifted(zr, prev_s, seg[0])
    pk = shifted(zk, prev_s, seg[1])
    pv = shifted(zv, prev_s, seg[2])
    pl_ = shifted(zl, prevl_s, slice(0, LANES))
    for b in seqs:
        last = slice(b * C + C - 1, b * C + C)
        prev_s[b, 0:1, seg[0]] = zr[last, :]
        prev_s[b, 0:1, seg[1]] = zk[last, :]
        prev_s[b, 0:1, seg[2]] = zv[last, :]
        prevl_s[b, 0:1, :] = zl[last, :]

    e = e_ref[...]
    r, k2, v, av, bv, logd = _wkv_prep(zr, zk, zv, zl, pr, pk, pv, pl_, pv_ref, mul_ref,
                                       wd_ref, wa_ref, e)

    ti = lax.broadcasted_iota(jnp.int32, (rows_all, rows_all), 0)
    tj = lax.broadcasted_iota(jnp.int32, (rows_all, rows_all), 1)
    tri = jnp.where((ti >= tj) & ((ti & -C) == (tj & -C)), 1.0, 0.0).astype(BF16)
    d_hi = _bf(logd)
    d_r1 = logd - d_hi.astype(F32)
    d_mid = _bf(d_r1)
    d_lo = _bf(d_r1 - d_mid.astype(F32))
    cum = _dg(jnp.concatenate([tri, tri, tri], axis=1), jnp.concatenate([d_hi, d_mid, d_lo], axis=0), NN)
    e_in = jnp.exp(cum)
    e_neg = jnp.exp(-cum)
    a_t = av * jnp.exp(cum - logd)
    r_t = r * e_in
    k_t = k2 * e_neg
    b_t = bv * e_neg
    p_c = [jnp.exp(cum[b * C + C - 1:b * C + C, :]) for b in seqs]

    lane = lax.broadcasted_iota(jnp.int32, (C, LANES), 1)
    trow = lax.broadcasted_iota(jnp.int32, (C, LANES), 0)
    lo = lane < HEAD
    s_in = lane & (HEAD - 1)
    strict = s_in < trow
    incl2 = ((lax.broadcasted_iota(jnp.int32, (C, 2 * LANES), 1) & (HEAD - 1))
             <= lax.broadcasted_iota(jnp.int32, (C, 2 * LANES), 0))
    eye2 = jnp.where(s_in == trow, 1.0, 0.0).astype(F32)
    vrow = lax.broadcasted_iota(jnp.int32, (2 * HEAD, LANES), 0)
    klane = lax.broadcasted_iota(jnp.int32, (2 * HEAD, LANES), 1)
    same_head = (vrow < HEAD) == (klane < HEAD)

    def bd(x):
        z = jnp.zeros_like(x)
        return jnp.concatenate([jnp.where(lo, x, z), jnp.where(lo, z, x)], axis=0)

    npair = HEADS // 2
    units = [(b, p) for b in seqs for p in range(npair)]
    un = range(len(units))
    blk = lambda arr, i: arr[units[i][0] * C:(units[i][0] + 1) * C, LANES * units[i][1]:LANES * (units[i][1] + 1)]
    ar = [_bf(jnp.concatenate([blk(a_t, i), blk(r_t, i)], axis=0)) for i in un]
    bk = [_bf(jnp.concatenate([bd(blk(b_t, i)), bd(blk(k_t, i))], axis=0)) for i in un]
    g = [_dg(ar[i], bk[i], NT) for i in un]
    s0 = [s_s[i] for i in un]
    ars = [_dg(ar[i], _bf(s0[i]), NT) for i in un]
    vbd = [_bf(bd(blk(v, i))) for i in un]
    x = [jnp.where(strict, g[i][0:C, 0:LANES], 0.0) for i in un]
    ak = [jnp.where(strict, g[i][0:C, LANES:2 * LANES], 0.0) for i in un]
    w = [ars[i][0:C, :] + _dg(_bf(ak[i]), vbd[i], NN) for i in un]
    t = [eye2 + x[i] for i in un]
    x = [_dg(_bf(x[i]), _bf(bd(x[i])), NN) for i in un]
    for _ in range(C.bit_length() - 3):
        xt = [_dg(_bf(jnp.concatenate([x[i], t[i]], axis=0)), _bf(bd(x[i])), NN) for i in un]
        x = [xt[i][0:C, :] for i in un]
        t = [t[i] + xt[i][C:2 * C, :] for i in un]
    t = [t[i] + _dg(_bf(t[i]), _bf(bd(x[i])), NN) for i in un]
    u = [_dg(_bf(t[i]), _bf(bd(w[i])), NN) for i in un]
    rbk = [_bf(jnp.where(incl2, g[i][C:2 * C, :], 0.0)) for i in un]
    uvbd = [jnp.concatenate([_bf(bd(u[i])), vbd[i]], axis=0) for i in un]
    y = [ars[i][C:2 * C, :] + _dg(rbk[i], uvbd[i], NN) for i in un]
    uv = [_bf(jnp.concatenate([u[i], blk(v, i)], axis=0)) for i in un]
    pc = [p_c[units[i][0]][:, LANES * units[i][1]:LANES * (units[i][1] + 1)] for i in un]
    bkh = [_bf(jnp.concatenate([blk(b_t, i), blk(k_t, i)], axis=0) * pc[i]) for i in un]
    s1 = [s0[i] * pc[i] + jnp.where(same_head, _dg(uv[i], bkh[i], TN), 0.0) for i in un]
    for i in un:
        s_s[i] = s1[i]

    y_all = jnp.concatenate(
        [jnp.concatenate(y[b * npair:(b + 1) * npair], axis=1) for b in seqs], axis=0)
    o = _wkv_post(y_all, r, k2, v, flat(zrg_ref), pv_ref, e)
    o_ref[...] = o.reshape(nb, C, RWKV_W)

    @pl.when(c == nc - 1)
    def _():
        for i in un:
            b, p = units[i]
            sout_ref[b, 2 * p] = s1[i][0:HEAD, 0:HEAD]
            sout_ref[b, 2 * p + 1] = s1[i][HEAD:2 * HEAD, HEAD:2 * HEAD]
        for b in seqs:
            for q, ref in enumerate((zr_ref, zk_ref, zv_ref)):
                nsh_ref[0, b:b + 1, RWKV_W * q:RWKV_W * (q + 1)] = ref[b, C - 1:C, :]
            nsh_ref[0, b:b + 1, SHIFT_MAIN:SHIFT_MAIN + LANES] = zl_ref[b, C - 1:C, :]


def _wkv_chunk(z, zl, pvec, mul, wd, wa, e, batch, seq, nb):
    C = WKV_CHUNK
    nc = seq // C
    full = lambda shp: pl.BlockSpec(shp, lambda b, c: (0,) * len(shp))
    col = lambda j: pl.BlockSpec((nb, C, RWKV_W), lambda b, c, j=j: (b, c, j))
    return pl.pallas_call(
        _wkv_chunk_kernel,
        out_shape=(jax.ShapeDtypeStruct((batch, seq, RWKV_W), BF16),
                   jax.ShapeDtypeStruct((batch, HEADS, HEAD, HEAD), F32),
                   jax.ShapeDtypeStruct((batch // nb, nb, SHIFT_MAIN + LANES), F32)),
        grid=(batch // nb, nc),
        in_specs=[col(0), col(1), col(2), col(3),
                  pl.BlockSpec((nb, C, LANES), lambda b, c: (b, c, 0)),
                  full(pvec.shape), full(mul.shape), full(wd.shape), full(wa.shape), full(e.shape)],
        out_specs=(pl.BlockSpec((nb, C, RWKV_W), lambda b, c: (b, c, 0)),
                   pl.BlockSpec((nb, HEADS, HEAD, HEAD), lambda b, c: (b, 0, 0, 0)),
                   pl.BlockSpec((1, nb, SHIFT_MAIN + LANES), lambda b, c: (b, 0, 0))),
        scratch_shapes=[pltpu.VMEM((nb * HEADS // 2, 2 * HEAD, 2 * HEAD), F32),
                        pltpu.VMEM((nb, SUBLANES, 3 * RWKV_W), F32),
                        pltpu.VMEM((nb, SUBLANES, LANES), F32)],
        compiler_params=pltpu.CompilerParams(
            dimension_semantics=("arbitrary", "arbitrary"), vmem_limit_bytes=VMEM_LIMIT),
        name="wkv_chunk",
    )(z, z, z, z, zl, pvec, mul, wd, wa, e)


def _wkv_step_kernel(zr_ref, zk_ref, zv_ref, zrg_ref, zl_ref, sh_ref, s0_ref, pv_ref,
                     mul_ref, wd_ref, wa_ref, e_ref, o_ref, sout_ref, nsh_ref,
                     at_s, drt_s, bt_s, kt_s, dt_s, vt_s, brt_s, krt_s, yt_s, keep_s):
    h = pl.program_id(0)
    nh = pl.num_programs(0)
    nseq = zr_ref.shape[0]

    @pl.when(h == 0)
    def _():
        e = e_ref[...]
        r, k2, v, av, bv, logd = _wkv_prep(
            zr_ref[...], zk_ref[...], zv_ref[...], zl_ref[...],
            sh_ref[:, 0:RWKV_W], sh_ref[:, RWKV_W:2 * RWKV_W], sh_ref[:, 2 * RWKV_W:3 * RWKV_W],
            sh_ref[:, SHIFT_MAIN:SHIFT_MAIN + LANES], pv_ref, mul_ref, wd_ref, wa_ref, e)
        nsh_ref[:, 0:RWKV_W] = zr_ref[...]
        nsh_ref[:, RWKV_W:2 * RWKV_W] = zk_ref[...]
        nsh_ref[:, 2 * RWKV_W:SHIFT_MAIN] = zv_ref[...]
        nsh_ref[:, SHIFT_MAIN:SHIFT_MAIN + LANES] = zl_ref[...]
        d = jnp.exp(logd)
        at_s[...] = av.T
        drt_s[...] = (d * r).T
        bt_s[...] = bv.T
        kt_s[...] = k2.T
        dt_s[...] = d.T
        vt_s[...] = v.T
        brt_s[...] = jnp.sum((bv * r).T.reshape(HEADS, HEAD, nseq), axis=1)
        krt_s[...] = jnp.sum((k2 * r).T.reshape(HEADS, HEAD, nseq), axis=1)
        keep_s[0] = r
        keep_s[1] = k2
        keep_s[2] = v

    for u in range(s0_ref.shape[0]):
        head = h * s0_ref.shape[0] + u
        base = pl.multiple_of(head * HEAD, HEAD)
        rows = pl.ds(base, HEAD)
        a_h, dr_h, b_h, k_h, d_h = at_s[rows, :], drt_s[rows, :], bt_s[rows, :], kt_s[rows, :], dt_s[rows, :]
        br_h = brt_s[pl.ds(head, 1), :]
        kr_h = krt_s[pl.ds(head, 1), :]

        def value_rows(g, carry, u=u, base=base, a_h=a_h, dr_h=dr_h, b_h=b_h, k_h=k_h, d_h=d_h,
                       br_h=br_h, kr_h=kr_h):
            off = pl.multiple_of(base + g * SUBLANES, SUBLANES)
            v8 = vt_s[pl.ds(off, SUBLANES), :]
            ys = []
            for j in range(SUBLANES):
                vi = g * SUBLANES + j
                s_v = s0_ref[u, vi]
                sa = jnp.sum(s_v * a_h, axis=0, keepdims=True)
                y0 = jnp.sum(s_v * dr_h, axis=0, keepdims=True)
                v_v = v8[j:j + 1, :]
                sout_ref[u, vi] = s_v * d_h + sa * b_h + v_v * k_h
                ys.append(y0 + sa * br_h + v_v * kr_h)
            yt_s[pl.ds(off, SUBLANES), :] = jnp.concatenate(ys, axis=0)
            return carry

        lax.fori_loop(0, HEAD // SUBLANES, value_rows, 0)

    @pl.when(h == nh - 1)
    def _():
        o_ref[...] = _wkv_post(yt_s[...].T, keep_s[0], keep_s[1], keep_s[2], zrg_ref[...], pv_ref,
                               e_ref[...])


def _wkv_step(z, zl, sh, s0t, pvec, mul, wd, wa, e):
    nseq = z.shape[0]
    full = lambda shp: pl.BlockSpec(shp, lambda i: (0,) * len(shp))
    col = lambda j: pl.BlockSpec((nseq, RWKV_W), lambda i, j=j: (0, j))
    st_block = (STEP_HEADS, HEAD, HEAD, nseq)
    wide = pltpu.VMEM((RWKV_W, nseq), F32)
    return pl.pallas_call(
        _wkv_step_kernel,
        out_shape=(jax.ShapeDtypeStruct((nseq, RWKV_W), BF16),
                   jax.ShapeDtypeStruct(s0t.shape, F32),
                   jax.ShapeDtypeStruct(sh.shape, F32)),
        grid=(HEADS // STEP_HEADS,),
        in_specs=[col(0), col(1), col(2), col(3),
                  full(zl.shape), full(sh.shape),
                  pl.BlockSpec(st_block, lambda i: (i, 0, 0, 0)),
                  full(pvec.shape), full(mul.shape), full(wd.shape), full(wa.shape), full(e.shape)],
        out_specs=(full((nseq, RWKV_W)),
                   pl.BlockSpec(st_block, lambda i: (i, 0, 0, 0)),
                   full(sh.shape)),
        scratch_shapes=[wide] * 6 + [pltpu.VMEM((HEADS, nseq), F32)] * 2
                       + [wide, pltpu.VMEM((3, nseq, RWKV_W), F32)],
        compiler_params=pltpu.CompilerParams(
            dimension_semantics=("arbitrary",), vmem_limit_bytes=VMEM_LIMIT),
        name="wkv_step",
    )(z, z, z, z, zl, sh, s0t, pvec, mul, wd, wa, e)


_CW0, _CW1, _CW2, _CW3, _CB, _GXB, _GAB, _LAM = range(8)


def _lru_gates(xc, lp_ref, wg_ref):
    xb = _bf(xc)
    ngroups = wg_ref.shape[0]
    gs = [_dg(xb[:, LANES * g:LANES * (g + 1)], wg_ref[g], NN) for g in range(ngroups)]
    gx_pre = jnp.concatenate([gs[g][:, 0:LANES] for g in range(ngroups)], axis=1)
    ga_pre = jnp.concatenate([gs[g][:, LANES:2 * LANES] for g in range(ngroups)], axis=1)
    gx = _sigmoid(gx_pre + _prow(lp_ref, _GXB))
    ga = _sigmoid(ga_pre + _prow(lp_ref, _GAB))
    log_a = -LRU_C * ga * _softplus(-_prow(lp_ref, _LAM))
    a = jnp.exp(log_a)
    mult = jnp.sqrt((1.0 - a) * (1.0 + a))
    return a, mult * gx * xc


def _lru_scan_rows(a, b, zg, hc):
    row8 = lax.broadcasted_iota(jnp.int32, (SUBLANES, 1), 0)
    hs = []
    for i in range(a.shape[0] // SUBLANES):
        a8 = a[SUBLANES * i:SUBLANES * (i + 1), :]
        b8 = b[SUBLANES * i:SUBLANES * (i + 1), :]
        for s in (1, 2, 4):
            keep = row8 >= s
            b8 = jnp.where(keep, a8 * pltpu.roll(b8, s, 0) + b8, b8)
            a8 = jnp.where(keep, a8 * pltpu.roll(a8, s, 0), a8)
        hb = b8 + a8 * hc
        hs.append(hb)
        hc = jnp.broadcast_to(hb[SUBLANES - 1:SUBLANES, :], hb.shape)
    return _bf(jnp.concatenate(hs, axis=0) * (zg * _sigmoid(zg))), hc


def _lru_step_kernel(zx_ref, zg_ref, conv_ref, h0_ref, lp_ref, wg_ref, o_ref, hnew_ref, cnew_ref):
    zx = zx_ref[...]
    for j in range(CONV_W - 2):
        cnew_ref[j] = conv_ref[j + 1]
    cnew_ref[CONV_W - 2] = zx
    xc = _prow(lp_ref, _CW3) * zx + _prow(lp_ref, _CB)
    for j in range(CONV_W - 1):
        xc = xc + _prow(lp_ref, j) * conv_ref[j]
    a, b = _lru_gates(xc, lp_ref, wg_ref)
    h = a * h0_ref[...] + b
    hnew_ref[...] = h
    zg = zg_ref[...]
    o_ref[...] = _bf(h * (zg * _sigmoid(zg)))


def _lru_step(z_main, conv, h0, lp, wg):
    nb = z_main.shape[0]
    full = lambda shp: pl.BlockSpec(shp, lambda i: (0,) * len(shp))
    col = lambda j: pl.BlockSpec((nb, LRU_W), lambda i, j=j: (0, j))
    return pl.pallas_call(
        _lru_step_kernel,
        out_shape=(jax.ShapeDtypeStruct((nb, LRU_W), BF16), jax.ShapeDtypeStruct((nb, LRU_W), F32),
                   jax.ShapeDtypeStruct(conv.shape, F32)),
        grid=(1,),
        in_specs=[col(4), col(5), full(conv.shape), full(h0.shape), full(lp.shape), full(wg.shape)],
        out_specs=(full((nb, LRU_W)), full((nb, LRU_W)), full(conv.shape)),
        compiler_params=pltpu.CompilerParams(
            dimension_semantics=("arbitrary",), vmem_limit_bytes=VMEM_LIMIT),
        name="lru_step",
    )(z_main, z_main, conv, h0, lp, wg)


def _project(x, o_r, o_g, m_r, m_g, wr_ref, wg_ref, wo_ref, fg_ref, final):
    y_r = _dg(o_r, wr_ref[...], NN)
    y_g = _dg(o_g, wg_ref[...], NN)
    merged = _sigmoid(m_r) * y_r + _sigmoid(m_g) * y_g
    out = x + _dg(_bf(merged), wo_ref[...], NN)
    return _rms(out, fg_ref[...]) if final else out


def _outproj_lru_kernel(x_ref, or_ref, z_hbm, zx_ref, zg_ref, xs_ref, ors_ref, ogs_ref,
                        mrs_ref, mgs_ref, lp_ref, wgate_ref, wr_ref, wg_ref, wo_ref, fg_ref,
                        out_ref, hlast_ref, cnew_ref, outs_ref, og_s, mg_s, xb_s, hc_s, m_s, m_sem, *,
                        final, tiles_per_seq, n_tiles):
    i = pl.program_id(0)
    n = n_tiles
    tm, d = x_ref.shape

    @pl.when(i == 0)
    def _():
        xb_s[...] = jnp.zeros_like(xb_s)
        hc_s[...] = jnp.zeros_like(hc_s)
        outs_ref[...] = _project(xs_ref[...], ors_ref[...], ogs_ref[...], mrs_ref[...], mgs_ref[...],
                                 wr_ref, wg_ref, wo_ref, fg_ref, final)

    def gate_copy(tile):
        slot = lax.rem(tile, MERGE_SLOTS)
        rows = pl.ds(pl.multiple_of(tile * tm, tm), tm)
        return pltpu.make_async_copy(z_hbm.at[rows, pl.ds(MERGE_COL, 2 * d)], m_s.at[slot], m_sem.at[slot])

    @pl.when(i == 0)
    def _():
        gate_copy(0).start()

    @pl.when(i + 1 < n)
    def _():
        gate_copy(i + 1).start()

    def stages(lru, branch, outp):
        if branch:
            og_prev = og_s[...]
            gate_copy(i - 1).wait()
            gates = m_s.at[lax.rem(i - 1, MERGE_SLOTS)]
        if outp:
            mg_prev = mg_s[...]
        if lru:
            t = lax.rem(i, tiles_per_seq)
            first = t == 0
            xb_s[0:SUBLANES, :] = jnp.where(first, 0.0, xb_s[0:SUBLANES, :])
            xb_s[SUBLANES:SUBLANES + tm, :] = zx_ref[...]

        outs, a_parts, b_parts = [], [], []
        for c in range(FUSE_PIECES):
            if outp:
                cs = slice(c * d // FUSE_PIECES, (c + 1) * d // FUSE_PIECES)
                outs.append(x_ref[:, cs] + _dg(mg_prev, wo_ref[:, cs], NN))
            if lru:
                r0, r1 = SUBLANES + c * tm // FUSE_PIECES, SUBLANES + (c + 1) * tm // FUSE_PIECES
                xc = _prow(lp_ref, _CW3) * xb_s[r0:r1, :] + _prow(lp_ref, _CB)
                for j in range(1, CONV_W):
                    xc = xc + _prow(lp_ref, CONV_W - 1 - j) * xb_s[r0 - j:r1 - j, :]
                a_c, b_c = _lru_gates(xc, lp_ref, wgate_ref)
                a_parts.append(a_c)
                b_parts.append(b_c)
        if outp:
            out = jnp.concatenate(outs, axis=1)
            out_ref[...] = _rms(out, fg_ref[...]) if final else out
        if lru:
            xb_s[0:SUBLANES, :] = xb_s[tm:tm + SUBLANES, :]
            o_g, hc = _lru_scan_rows(jnp.concatenate(a_parts, axis=0), jnp.concatenate(b_parts, axis=0),
                                     zg_ref[...], jnp.where(first, 0.0, hc_s[...]))
            hc_s[...] = hc
            og_s[...] = o_g
        if branch:
            y_r = _dg(or_ref[...], wr_ref[...], NN)
            y_g = _dg(og_prev, wg_ref[...], NN)
            mg_s[...] = _bf(_sigmoid(gates[:, 0:d]) * y_r + _sigmoid(gates[:, d:2 * d]) * y_g)
        if lru:
            @pl.when(t == tiles_per_seq - 1)
            def _():
                seq = i // tiles_per_seq
                hlast_ref[pl.ds(seq, 1), :] = hc[0:1, :]
                for j in range(CONV_W - 1):
                    row = tm - (CONV_W - 1) + j
                    cnew_ref[j, pl.ds(seq, 1), :] = zx_ref[row:row + 1, :]

    active = lambda step, s: 0 <= step - s < n
    holds = lambda s, flag: ((i >= s) & (i < n + s)) == flag
    for combo in sorted({tuple(active(step, s) for s in range(3)) for step in range(n + 2)}):
        lru, branch, outp = combo
        pl.when(holds(0, lru) & holds(1, branch) & holds(2, outp))(
            functools.partial(stages, lru, branch, outp))


def _outproj_lru(x, o_r, z, xs, o_rs, o_gs, zs, lp, wgate, w_r, w_g, w_o, fg, tm, seq, final):
    m, d = x.shape
    ms = xs.shape[0]
    n = m // tm
    tiles_per_seq = seq // tm
    const = lambda shp: pl.BlockSpec(shp, lambda i: (0,) * len(shp), pipeline_mode=pl.Buffered(1))
    back = lambda i, k: jnp.clip(i - k, 0, n - 1)
    return pl.pallas_call(
        functools.partial(_outproj_lru_kernel, final=final, tiles_per_seq=tiles_per_seq, n_tiles=n),
        out_shape=(jax.ShapeDtypeStruct((m, d), F32),
                   jax.ShapeDtypeStruct((m // seq, LRU_W), F32),
                   jax.ShapeDtypeStruct((CONV_W - 1, m // seq, LRU_W), F32),
                   jax.ShapeDtypeStruct((ms, d), F32)),
        grid=(n + 2,),
        in_specs=[
            pl.BlockSpec((tm, d), lambda i: (back(i, 2), 0)),
            pl.BlockSpec((tm, RWKV_W), lambda i: (back(i, 1), 0)),
            pl.BlockSpec(memory_space=pl.ANY),
            pl.BlockSpec((tm, LRU_W), lambda i: (back(i, 0), 4)),
            pl.BlockSpec((tm, LRU_W), lambda i: (back(i, 0), 5)),
            const(xs.shape), const(o_rs.shape), const(o_gs.shape),
            pl.BlockSpec((ms, d), lambda i: (0, 3), pipeline_mode=pl.Buffered(1)),
            pl.BlockSpec((ms, d), lambda i: (0, 4), pipeline_mode=pl.Buffered(1)),
            const(lp.shape), const(wgate.shape), const(w_r.shape), const(w_g.shape), const(w_o.shape),
            const(fg.shape),
        ],
        out_specs=(pl.BlockSpec((tm, d), lambda i: (back(i, 2), 0)),
                   pl.BlockSpec((m // seq, LRU_W), lambda i: (0, 0)),
                   pl.BlockSpec((CONV_W - 1, m // seq, LRU_W), lambda i: (0, 0, 0)),
                   pl.BlockSpec((ms, d), lambda i: (0, 0))),
        scratch_shapes=[pltpu.VMEM((tm, LRU_W), BF16),
                        pltpu.VMEM((tm, d), BF16),
                        pltpu.VMEM((SUBLANES + tm, LRU_W), F32),
                        pltpu.VMEM((SUBLANES, LRU_W), F32),
                        pltpu.VMEM((MERGE_SLOTS, tm, 2 * d), F32),
                        pltpu.SemaphoreType.DMA((MERGE_SLOTS,))],
        compiler_params=pltpu.CompilerParams(
            dimension_semantics=("arbitrary",), vmem_limit_bytes=VMEM_LIMIT),
        name="outproj_lru",
    )(x, o_r, z, z, z, xs, o_rs, o_gs, zs, zs, lp, wgate, w_r, w_g, w_o, fg)


def _row_tile(m, want):
    t = min(m, want)
    assert m % t == 0, (m, t)
    return t


def _pack_params_kernel(mu_ref, w0_ref, a0_ref, kk_ref, ka_ref, lng_ref, lnb_ref, rk_ref, wdu_ref,
                        wau_ref, cw_ref, cb_ref, gxb_ref, gab_ref, lam_ref, gxw_ref, gaw_ref,
                        pvec_ref, mul_ref, wd_ref, wa_ref, e_ref, lp_ref, wg_ref):
    pvec_ref[...] = jnp.zeros_like(pvec_ref)
    for i in range(3):
        pvec_ref[_MU_R + i:_MU_R + i + 1, :] = mu_ref[:, RWKV_W * i:RWKV_W * (i + 1)]
    for row, ref in ((_W0, w0_ref), (_A0, a0_ref), (_KK, kk_ref), (_KA, ka_ref),
                     (_LNG, lng_ref), (_LNB, lnb_ref)):
        pvec_ref[row:row + 1, :] = ref[...]
    for h in range(HEADS):
        pvec_ref[_RK:_RK + 1, HEAD * h:HEAD * (h + 1)] = rk_ref[h:h + 1, :]
    mul_ref[...] = jnp.broadcast_to(mu_ref[:, 3 * RWKV_W:3 * RWKV_W + 2 * LORA], mul_ref.shape)

    zeros = jnp.zeros((LORA, RWKV_W), BF16)
    wd_ref[0:LORA, :] = _bf(wdu_ref[...])
    wd_ref[LORA:2 * LORA, :] = zeros
    wa_ref[0:LORA, :] = zeros
    wa_ref[LORA:2 * LORA, :] = _bf(wau_ref[...])

    ri = lax.broadcasted_iota(jnp.int32, (LANES, LANES), 0)
    ci = lax.broadcasted_iota(jnp.int32, (LANES, LANES), 1)
    e_ref[...] = jnp.where((ri < HEAD) == (ci < HEAD), 1.0, 0.0).astype(BF16)

    lp_ref[_CW0:_CW0 + CONV_W, :] = cw_ref[...]
    for row, ref in ((_CB, cb_ref), (_GXB, gxb_ref), (_GAB, gab_ref), (_LAM, lam_ref)):
        lp_ref[row:row + 1, :] = ref[...]

    blk = LRU_W // LRU_BLOCKS
    z = jnp.zeros((blk, blk), F32)
    for g in range(LRU_BLOCKS // 2):
        top = jnp.concatenate([gxw_ref[2 * g], z, gaw_ref[2 * g], z], axis=1)
        bot = jnp.concatenate([z, gxw_ref[2 * g + 1], z, gaw_ref[2 * g + 1]], axis=1)
        wg_ref[g] = _bf(jnp.concatenate([top, bot], axis=0))


def _pack_params(l, rwkv_mu, w_decay0, w_decay_up, w_iclr0, w_iclr_up, k_k, k_a, r_k, ln_x_g,
                 ln_x_b, conv_w, conv_b, lru_gx_w, lru_gx_b, lru_ga_w, lru_ga_b, lru_lambda):
    blk = LRU_W // LRU_BLOCKS
    assert 2 * blk == LANES and 2 * LORA == LANES
    row = lambda a: pl.BlockSpec((1, a.shape[-1]), lambda i: (l, 0))
    mat = lambda a: pl.BlockSpec((None,) + a.shape[1:], lambda i: (l,) + (0,) * (a.ndim - 1))
    full = lambda shp: pl.BlockSpec(shp, lambda i: (0,) * len(shp))
    rows = (rwkv_mu, w_decay0, w_iclr0, k_k, k_a, ln_x_g, ln_x_b)
    out_shapes = ((16, RWKV_W, F32), (SUBLANES, LANES, F32), (LANES, RWKV_W, BF16), (LANES, RWKV_W, BF16),
                  (LANES, LANES, BF16), (SUBLANES, LRU_W, F32))
    outs = tuple(jax.ShapeDtypeStruct(s[:2], s[2]) for s in out_shapes)
    outs += (jax.ShapeDtypeStruct((LRU_BLOCKS // 2, LANES, 2 * LANES), BF16),)
    return pl.pallas_call(
        _pack_params_kernel,
        out_shape=outs,
        grid=(1,),
        in_specs=[row(a) for a in rows] + [mat(r_k), mat(w_decay_up), mat(w_iclr_up), mat(conv_w),
                                          row(conv_b), row(lru_gx_b), row(lru_ga_b), row(lru_lambda),
                                          mat(lru_gx_w), mat(lru_ga_w)],
        out_specs=tuple(full(o.shape) for o in outs),
        compiler_params=pltpu.CompilerParams(
            dimension_semantics=("arbitrary",), vmem_limit_bytes=VMEM_LIMIT),
        name="pack_params",
    )(*rows, r_k, w_decay_up, w_iclr_up, conv_w, conv_b, lru_gx_b, lru_ga_b, lru_lambda, lru_gx_w,
      lru_ga_w)


def kernel(x_prompt, x_sample, state_shift, state_wkv, state_conv, state_lru, norm_g, w_in, rwkv_mu,
           w_decay0, w_decay_up, w_iclr0, w_iclr_up, k_k, k_a, r_k, ln_x_g, ln_x_b, w_out_rwkv,
           conv_w, conv_b, lru_gx_w, lru_gx_b, lru_ga_w, lru_ga_b, lru_lambda, w_out_lru, w_out,
           final_norm_g):
    bp, seq, d = x_prompt.shape
    bs = x_sample.shape[0]
    assert x_sample.shape[1] == 1 and seq % WKV_CHUNK == 0
    depth = w_in.shape[0]
    xp = x_prompt.reshape(bp * seq, d)
    xs = x_sample.reshape(bs, d)
    fg = final_norm_g.reshape(1, d)
    outs = [[] for _ in range(8)]
    for l in range(depth):
        pvec, mul, wd, wa, e, lp, wg = _pack_params(
            l, rwkv_mu, w_decay0, w_decay_up, w_iclr0, w_iclr_up, k_k, k_a, r_k, ln_x_g, ln_x_b,
            conv_w, conv_b, lru_gx_w, lru_gx_b, lru_ga_w, lru_ga_b, lru_lambda)
        g = norm_g[l].reshape(1, d)
        rec = (pvec, mul, wd, wa, e)
        w, wl, zs, zls, w_r, w_g, w_o = _inproj_head(xs, g, w_in, w_out_rwkv, w_out_lru, w_out, l, INPROJ_TN)
        zp, zlp = _inproj(xp, g, w, wl, _row_tile(bp * seq, 1024), INPROJ_WIDE_TN)

        s0t = jnp.transpose(state_wkv[l], (1, 2, 3, 0))
        o_rs, s_new, sh_new = _wkv_step(zs, zls, state_shift[l], s0t, *rec)
        conv = jnp.transpose(state_conv[l], (1, 0, 2))
        o_gs, h_new, conv_new = _lru_step(zs, conv, state_lru[l], lp, wg)
        outs[4].append(sh_new)
        outs[5].append(jnp.transpose(s_new, (3, 0, 1, 2)))
        outs[6].append(jnp.transpose(conv_new, (1, 0, 2)))
        outs[7].append(h_new)

        zp3 = zp.reshape(bp, seq, -1)
        zlp3 = zlp.reshape(bp, seq, LANES)
        nb = max(n for n in (4, 2, 1) if bp % n == 0)
        o_r, s_new, sh_last = _wkv_chunk(zp3, zlp3, *rec, bp, seq, nb)
        o_r = o_r.reshape(bp * seq, RWKV_W)
        last = l == depth - 1
        xp, h_last, conv_last, xs = _outproj_lru(xp, o_r, zp, xs, o_rs, o_gs, zs, lp, wg, w_r, w_g, w_o,
                                                 fg, _row_tile(seq, 256), seq, last)
        outs[0].append(sh_last.reshape(bp, -1))
        outs[1].append(s_new)
        outs[2].append(jnp.transpose(conv_last, (1, 0, 2)))
        outs[3].append(h_last)

    return (xp.reshape(bp, seq, d), xs.reshape(bs, 1, d)) + tuple(jnp.stack(o) for o in outs)
```

```python
import functools

import jax
import jax.numpy as jnp
from jax import lax
from jax.experimental import pallas as pl
from jax.experimental.pallas import tpu as pltpu

F32 = jnp.float32
BF16 = jnp.bfloat16

HEADS = 16
HEAD = 64
RWKV_W = HEADS * HEAD
LORA = 64
LRU_W = 1024
LRU_BLOCKS = 16
CONV_W = 4
LRU_C = 8.0
RMS_EPS = 1e-6
GN_EPS = 1e-5 * HEAD
DECAY_SCALE = 0.6065306597126334

LANES = 128
SUBLANES = 8
WKV_CHUNK = 64
STEP_HEADS = 2
VMEM_LIMIT = 60 * 1024 * 1024

NN = (((1,), (0,)), ((), ()))
NT = (((1,), (1,)), ((), ()))
TN = (((0,), (0,)), ((), ()))


def _bf(x):
    return x.astype(BF16)


def _dg(a, b, dn):
    return lax.dot_general(a, b, dn, preferred_element_type=F32)


def _softplus(x):
    return jnp.maximum(x, 0.0) + jnp.log1p(jnp.exp(-jnp.abs(x)))


def _sigmoid(x):
    return 1.0 / (1.0 + jnp.exp(-x))


def _segsum(x, e):
    rows, n = x.shape[0], x.shape[1] // LANES
    stacked = jnp.concatenate([x[:, LANES * j:LANES * (j + 1)] for j in range(n)], axis=0)
    s = _dg(_bf(stacked), e, NN)
    return jnp.concatenate([s[rows * j:rows * (j + 1), :] for j in range(n)], axis=1)


def _rms(x, g):
    return x * lax.rsqrt(jnp.mean(x * x, axis=-1, keepdims=True) + RMS_EPS) * g


SHIFT_MAIN = 3 * RWKV_W
LORA_COL = 10 * RWKV_W
LORA_BLOCK = LORA_COL // LANES
FUSE_PIECES = 8
INPROJ_TN = 1024
INPROJ_WIDE_TN = 2048
NORM_ROWS = 128
OUT_W_STEPS = 8


def _inproj_head_kernel(xs_ref, g_ref, w_ref, wt_ref, wlo_ref, wr_ref, wg_ref, wo_ref, wb_ref, wl_ref,
                        zs_ref, zls_ref, wrb_ref, wgb_ref, wob_ref, hs_ref):
    j = pl.program_id(0)
    tn = w_ref.shape[-1]

    @pl.when(j == 0)
    def _():
        hs_ref[...] = _bf(_rms(xs_ref[...], g_ref[...]))
        wl_ref[...] = _bf(wlo_ref[...])
        zls_ref[...] = _dg(hs_ref[...], wl_ref[...], NN)

    @pl.when(j < SHIFT_MAIN // tn)
    def _():
        wb_ref[...] = _bf(w_ref[...])

    @pl.when(j >= SHIFT_MAIN // tn)
    def _():
        wb_ref[:, 0:tn - 2 * LORA] = _bf(w_ref[:, 2 * LORA:tn])
        wb_ref[:, tn - 2 * LORA:tn] = _bf(wt_ref[...])

    zs_ref[...] = _dg(hs_ref[...], wb_ref[...], NN)

    @pl.when(j < OUT_W_STEPS)
    def _():
        wrb_ref[...] = _bf(wr_ref[...])
        wgb_ref[...] = _bf(wg_ref[...])
        wob_ref[...] = _bf(wo_ref[...])


def _inproj_head(xs, g, w_in, w_out_rwkv, w_out_lru, w_out, layer, tn):
    ms, d = xs.shape
    n = w_in.shape[-1]
    assert n == LORA_COL + 2 * LORA and SHIFT_MAIN % tn == 0 and 2 * LORA == LANES
    nj = LORA_COL // tn
    assert nj >= OUT_W_STEPS
    lanes_per_tile = tn // LANES
    one = lambda shp, imap: pl.BlockSpec(shp, imap, pipeline_mode=pl.Buffered(1))
    rows = lambda a: a.shape[1] // OUT_W_STEPS
    step = lambda j: jnp.minimum(j, OUT_W_STEPS - 1)
    w_outs = (w_out_rwkv, w_out_lru, w_out)
    return pl.pallas_call(
        _inproj_head_kernel,
        out_shape=(jax.ShapeDtypeStruct((d, LORA_COL), BF16), jax.ShapeDtypeStruct((d, LANES), BF16),
                   jax.ShapeDtypeStruct((ms, LORA_COL), F32), jax.ShapeDtypeStruct((ms, LANES), F32))
        + tuple(jax.ShapeDtypeStruct(a.shape[1:], BF16) for a in w_outs),
        grid=(nj,),
        in_specs=[
            one((ms, d), lambda j: (0, 0)),
            one((1, d), lambda j: (0, 0)),
            pl.BlockSpec((None, d, tn), lambda j: (layer, 0, j)),
            pl.BlockSpec((None, d, LANES), lambda j: (layer, 0, (j + 1) * lanes_per_tile)),
            one((None, d, LANES), lambda j: (layer, 0, SHIFT_MAIN // LANES)),
        ] + [pl.BlockSpec((None, rows(a), a.shape[2]), lambda j: (layer, step(j), 0)) for a in w_outs],
        out_specs=(
            pl.BlockSpec((d, tn), lambda j: (0, j)),
            pl.BlockSpec((d, LANES), lambda j: (0, 0)),
            pl.BlockSpec((ms, tn), lambda j: (0, j)),
            pl.BlockSpec((ms, LANES), lambda j: (0, 0)),
        ) + tuple(pl.BlockSpec((rows(a), a.shape[2]), lambda j: (step(j), 0)) for a in w_outs),
        scratch_shapes=[pltpu.VMEM((ms, d), BF16)],
        compiler_params=pltpu.CompilerParams(
            dimension_semantics=("arbitrary",), vmem_limit_bytes=VMEM_LIMIT),
        name="inproj_head",
    )(xs, g, w_in, w_in, w_in, *w_outs)


def _inproj_kernel(x_ref, g_ref, w_ref, wl_ref, z_ref, zl_ref, h_ref):
    @pl.when(pl.program_id(1) == 0)
    def _():
        for r in range(0, x_ref.shape[0], NORM_ROWS):
            rows = slice(r, r + NORM_ROWS)
            h_ref[rows, :] = _bf(_rms(x_ref[rows, :], g_ref[...]))
        zl_ref[...] = _dg(h_ref[...], wl_ref[...], NN)

    z_ref[...] = _dg(h_ref[...], w_ref[...], NN)


def _inproj(x, g, w, wl, tm, tn):
    m, d = x.shape
    return pl.pallas_call(
        _inproj_kernel,
        out_shape=(jax.ShapeDtypeStruct((m, LORA_COL), F32), jax.ShapeDtypeStruct((m, LANES), F32)),
        grid=(m // tm, LORA_COL // tn),
        in_specs=[
            pl.BlockSpec((tm, d), lambda i, j: (i, 0)),
            pl.BlockSpec((1, d), lambda i, j: (0, 0), pipeline_mode=pl.Buffered(1)),
            pl.BlockSpec((d, tn), lambda i, j: (0, j)),
            pl.BlockSpec((d, LANES), lambda i, j: (0, 0), pipeline_mode=pl.Buffered(1)),
        ],
        out_specs=(
            pl.BlockSpec((tm, tn), lambda i, j: (i, j)),
            pl.BlockSpec((tm, LANES), lambda i, j: (i, 0)),
        ),
        scratch_shapes=[pltpu.VMEM((tm, d), BF16)],
        compiler_params=pltpu.CompilerParams(
            dimension_semantics=("arbitrary", "arbitrary"), vmem_limit_bytes=VMEM_LIMIT),
        name="inproj",
    )(x, g, w, wl)


_MU_R, _MU_K, _MU_V, _W0, _A0, _KK, _KA, _RK, _LNG, _LNB = range(10)


def _prow(pv_ref, i):
    return pv_ref[i:i + 1, :]


def _wkv_prep(zr, zk, zv, zl, pr, pk, pv, pl_, pv_ref, mul_ref, wd_ref, wa_ref, e):
    r = zr + _prow(pv_ref, _MU_R) * (pr - zr)
    k = zk + _prow(pv_ref, _MU_K) * (pk - zk)
    v = zv + _prow(pv_ref, _MU_V) * (pv - zv)
    lo = zl + mul_ref[0:1, :] * (pl_ - zl)
    lw = _dg(_bf(jnp.tanh(lo)), wd_ref[...], NN)
    la = _dg(_bf(lo), wa_ref[...], NN)
    logd = -DECAY_SCALE * _sigmoid(_prow(pv_ref, _W0) + lw)
    a = _sigmoid(_prow(pv_ref, _A0) + la)
    kk = k * _prow(pv_ref, _KK)
    kk = kk * lax.rsqrt(jnp.maximum(_segsum(kk * kk, e), 1e-24))
    k2 = k * (1.0 + (a - 1.0) * _prow(pv_ref, _KA))
    return r, k2, v, -kk, kk * a, logd


def _wkv_bonus_gate(r, k2, v, zrg, pv_ref, e):
    return _segsum(r * k2 * _prow(pv_ref, _RK), e) * v, zrg * _sigmoid(zrg)


def _wkv_norm_gate(y, bonus_v, gate, pv_ref, e):
    mu = _segsum(y, e) * (1.0 / HEAD)
    yc = y - mu
    var = _segsum(yc * yc, e) * (1.0 / HEAD)
    yn = yc * lax.rsqrt(var + GN_EPS) * _prow(pv_ref, _LNG) + _prow(pv_ref, _LNB)
    return _bf((yn + bonus_v) * gate)


def _wkv_post(y, r, k2, v, zrg, pv_ref, e):
    bonus_v, gate = _wkv_bonus_gate(r, k2, v, zrg, pv_ref, e)
    return _wkv_norm_gate(y, bonus_v, gate, pv_ref, e)


def _wkv_chunk_kernel(zr_ref, zk_ref, zv_ref, zrg_ref, zl_ref, pv_ref, mul_ref, wd_ref, wa_ref,
                      e_ref, o_ref, sout_ref, nsh_ref, s_s, prev_s, prevl_s):
    c = pl.program_id(1)
    nc = pl.num_programs(1)
    C = WKV_CHUNK
    assert C == HEAD and 2 * HEAD == LANES
    nb = zr_ref.shape[0]
    rows_all = nb * C
    seqs = range(nb)

    @pl.when(c == 0)
    def _():
        s_s[...] = jnp.zeros_like(s_s)
        prev_s[...] = jnp.zeros_like(prev_s)
        prevl_s[...] = jnp.zeros_like(prevl_s)

    first = lax.broadcasted_iota(jnp.int32, (SUBLANES, 1), 0) == 0

    def shifted(z, prev_ref, lanes):
        rolled = pltpu.roll(z, 1, 0)
        pieces = []
        for b in seqs:
            head = jnp.where(first, prev_ref[b, 0:1, lanes], rolled[b * C:b * C + SUBLANES, :])
            pieces += [head, rolled[b * C + SUBLANES:(b + 1) * C, :]]
        return jnp.concatenate(pieces, axis=0)

    def flat(ref):
        return ref[...].reshape(rows_all, ref.shape[-1])

    zr, zk, zv, zl = flat(zr_ref), flat(zk_ref), flat(zv_ref), flat(zl_ref)
    seg = [slice(RWKV_W * i, RWKV_W * (i + 1)) for i in range(3)]
    pr = shifted(zr, prev_s, seg[0])
    pk = shifted(zk, prev_s, seg[1])
    pv = shifted(zv, prev_s, seg[2])
    pl_ = shifted(zl, prevl_s, slice(0, LANES))
    for b in seqs:
        last = slice(b * C + C - 1, b * C + C)
        prev_s[b, 0:1, seg[0]] = zr[last, :]
        prev_s[b, 0:1, seg[1]] = zk[last, :]
        prev_s[b, 0:1, seg[2]] = zv[last, :]
        prevl_s[b, 0:1, :] = zl[last, :]

    e = e_ref[...]
    r, k2, v, av, bv, logd = _wkv_prep(zr, zk, zv, zl, pr, pk, pv, pl_, pv_ref, mul_ref,
                                       wd_ref, wa_ref, e)

    ti = lax.broadcasted_iota(jnp.int32, (rows_all, rows_all), 0)
    tj = lax.broadcasted_iota(jnp.int32, (rows_all, rows_all), 1)
    tri = jnp.where((ti >= tj) & ((ti & -C) == (tj & -C)), 1.0, 0.0).astype(BF16)
    d_hi = _bf(logd)
    d_r1 = logd - d_hi.astype(F32)
    d_mid = _bf(d_r1)
    d_lo = _bf(d_r1 - d_mid.astype(F32))
    cum = _dg(jnp.concatenate([tri, tri, tri], axis=1), jnp.concatenate([d_hi, d_mid, d_lo], axis=0), NN)
    e_in = jnp.exp(cum)
    e_neg = jnp.exp(-cum)
    a_t = av * jnp.exp(cum - logd)
    r_t = r * e_in
    k_t = k2 * e_neg
    b_t = bv * e_neg
    p_c = [jnp.exp(cum[b * C + C - 1:b * C + C, :]) for b in seqs]

    lane = lax.broadcasted_iota(jnp.int32, (C, LANES), 1)
    trow = lax.broadcasted_iota(jnp.int32, (C, LANES), 0)
    lo = lane < HEAD
    s_in = lane & (HEAD - 1)
    strict = s_in < trow
    incl2 = ((lax.broadcasted_iota(jnp.int32, (C, 2 * LANES), 1) & (HEAD - 1))
             <= lax.broadcasted_iota(jnp.int32, (C, 2 * LANES), 0))
    eye2 = jnp.where(s_in == trow, 1.0, 0.0).astype(F32)
    vrow = lax.broadcasted_iota(jnp.int32, (2 * HEAD, LANES), 0)
    klane = lax.broadcasted_iota(jnp.int32, (2 * HEAD, LANES), 1)
    same_head = (vrow < HEAD) == (klane < HEAD)

    def bd(x):
        z = jnp.zeros_like(x)
        return jnp.concatenate([jnp.where(lo, x, z), jnp.where(lo, z, x)], axis=0)

    npair = HEADS // 2
    units = [(b, p) for b in seqs for p in range(npair)]
    un = range(len(units))
    blk = lambda arr, i: arr[units[i][0] * C:(units[i][0] + 1) * C, LANES * units[i][1]:LANES * (units[i][1] + 1)]
    ar = [_bf(jnp.concatenate([blk(a_t, i), blk(r_t, i)], axis=0)) for i in un]
    bk = [_bf(jnp.concatenate([bd(blk(b_t, i)), bd(blk(k_t, i))], axis=0)) for i in un]
    g = [_dg(ar[i], bk[i], NT) for i in un]
    s0 = [s_s[i] for i in un]
    ars = [_dg(ar[i], _bf(s0[i]), NT) for i in un]
    vbd = [_bf(bd(blk(v, i))) for i in un]
    x = [jnp.where(strict, g[i][0:C, 0:LANES], 0.0) for i in un]
    ak = [jnp.where(strict, g[i][0:C, LANES:2 * LANES], 0.0) for i in un]
    w = [ars[i][0:C, :] + _dg(_bf(ak[i]), vbd[i], NN) for i in un]
    t = [eye2 + x[i] for i in un]
    x = [_dg(_bf(x[i]), _bf(bd(x[i])), NN) for i in un]
    for _ in range(C.bit_length() - 3):
        xt = [_dg(_bf(jnp.concatenate([x[i], t[i]], axis=0)), _bf(bd(x[i])), NN) for i in un]
        x = [xt[i][0:C, :] for i in un]
        t = [t[i] + xt[i][C:2 * C, :] for i in un]
    t = [t[i] + _dg(_bf(t[i]), _bf(bd(x[i])), NN) for i in un]
    u = [_dg(_bf(t[i]), _bf(bd(w[i])), NN) for i in un]
    rbk = [_bf(jnp.where(incl2, g[i][C:2 * C, :], 0.0)) for i in un]
    uvbd = [jnp.concatenate([_bf(bd(u[i])), vbd[i]], axis=0) for i in un]
    y = [ars[i][C:2 * C, :] + _dg(rbk[i], uvbd[i], NN) for i in un]
    uv = [_bf(jnp.concatenate([u[i], blk(v, i)], axis=0)) for i in un]
    pc = [p_c[units[i][0]][:, LANES * units[i][1]:LANES * (units[i][1] + 1)] for i in un]
    bkh = [_bf(jnp.concatenate([blk(b_t, i), blk(k_t, i)], axis=0) * pc[i]) for i in un]
    s1 = [s0[i] * pc[i] + jnp.where(same_head, _dg(uv[i], bkh[i], TN), 0.0) for i in un]
    for i in un:
        s_s[i] = s1[i]

    y_all = jnp.concatenate(
        [jnp.concatenate(y[b * npair:(b + 1) * npair], axis=1) for b in seqs], axis=0)
    o = _wkv_post(y_all, r, k2, v, flat(zrg_ref), pv_ref, e)
    o_ref[...] = o.reshape(nb, C, RWKV_W)

    @pl.when(c == nc - 1)
    def _():
        for i in un:
            b, p = units[i]
            sout_ref[b, 2 * p] = s1[i][0:HEAD, 0:HEAD]
            sout_ref[b, 2 * p + 1] = s1[i][HEAD:2 * HEAD, HEAD:2 * HEAD]
        for b in seqs:
            for q, ref in enumerate((zr_ref, zk_ref, zv_ref)):
                nsh_ref[0, b:b + 1, RWKV_W * q:RWKV_W * (q + 1)] = ref[b, C - 1:C, :]
            nsh_ref[0, b:b + 1, SHIFT_MAIN:SHIFT_MAIN + LANES] = zl_ref[b, C - 1:C, :]


def _wkv_chunk(z, zl, pvec, mul, wd, wa, e, batch, seq, nb):
    C = WKV_CHUNK
    nc = seq // C
    full = lambda shp: pl.BlockSpec(shp, lambda b, c: (0,) * len(shp))
    col = lambda j: pl.BlockSpec((nb, C, RWKV_W), lambda b, c, j=j: (b, c, j))
    return pl.pallas_call(
        _wkv_chunk_kernel,
        out_shape=(jax.ShapeDtypeStruct((batch, seq, RWKV_W), BF16),
                   jax.ShapeDtypeStruct((batch, HEADS, HEAD, HEAD), F32),
                   jax.ShapeDtypeStruct((batch // nb, nb, SHIFT_MAIN + LANES), F32)),
        grid=(batch // nb, nc),
        in_specs=[col(0), col(1), col(2), col(3),
                  pl.BlockSpec((nb, C, LANES), lambda b, c: (b, c, 0)),
                  full(pvec.shape), full(mul.shape), full(wd.shape), full(wa.shape), full(e.shape)],
        out_specs=(pl.BlockSpec((nb, C, RWKV_W), lambda b, c: (b, c, 0)),
                   pl.BlockSpec((nb, HEADS, HEAD, HEAD), lambda b, c: (b, 0, 0, 0)),
                   pl.BlockSpec((1, nb, SHIFT_MAIN + LANES), lambda b, c: (b, 0, 0))),
        scratch_shapes=[pltpu.VMEM((nb * HEADS // 2, 2 * HEAD, 2 * HEAD), F32),
                        pltpu.VMEM((nb, SUBLANES, 3 * RWKV_W), F32),
                        pltpu.VMEM((nb, SUBLANES, LANES), F32)],
        compiler_params=pltpu.CompilerParams(
            dimension_semantics=("arbitrary", "arbitrary"), vmem_limit_bytes=VMEM_LIMIT),
        name="wkv_chunk",
    )(z, z, z, z, zl, pvec, mul, wd, wa, e)


def _wkv_step_kernel(zr_ref, zk_ref, zv_ref, zrg_ref, zl_ref, sh_ref, s0_ref, pv_ref,
                     mul_ref, wd_ref, wa_ref, e_ref, o_ref, sout_ref, nsh_ref,
                     at_s, drt_s, bt_s, kt_s, dt_s, vt_s, brt_s, krt_s, yt_s, keep_s):
    h = pl.program_id(0)
    nh = pl.num_programs(0)
    nseq = zr_ref.shape[0]

    @pl.when(h == 0)
    def _():
        e = e_ref[...]
        r, k2, v, av, bv, logd = _wkv_prep(
            zr_ref[...], zk_ref[...], zv_ref[...], zl_ref[...],
            sh_ref[:, 0:RWKV_W], sh_ref[:, RWKV_W:2 * RWKV_W], sh_ref[:, 2 * RWKV_W:3 * RWKV_W],
            sh_ref[:, SHIFT_MAIN:SHIFT_MAIN + LANES], pv_ref, mul_ref, wd_ref, wa_ref, e)
        nsh_ref[:, 0:RWKV_W] = zr_ref[...]
        nsh_ref[:, RWKV_W:2 * RWKV_W] = zk_ref[...]
        nsh_ref[:, 2 * RWKV_W:SHIFT_MAIN] = zv_ref[...]
        nsh_ref[:, SHIFT_MAIN:SHIFT_MAIN + LANES] = zl_ref[...]
        d = jnp.exp(logd)
        at_s[...] = av.T
        drt_s[...] = (d * r).T
        bt_s[...] = bv.T
        kt_s[...] = k2.T
        dt_s[...] = d.T
        vt_s[...] = v.T
        brt_s[...] = jnp.sum((bv * r).T.reshape(HEADS, HEAD, nseq), axis=1)
        krt_s[...] = jnp.sum((k2 * r).T.reshape(HEADS, HEAD, nseq), axis=1)
        keep_s[0] = r
        keep_s[1] = k2
        keep_s[2] = v

    for u in range(s0_ref.shape[0]):
        head = h * s0_ref.shape[0] + u
        base = pl.multiple_of(head * HEAD, HEAD)
        rows = pl.ds(base, HEAD)
        a_h, dr_h, b_h, k_h, d_h = at_s[rows, :], drt_s[rows, :], bt_s[rows, :], kt_s[rows, :], dt_s[rows, :]
        br_h = brt_s[pl.ds(head, 1), :]
        kr_h = krt_s[pl.ds(head, 1), :]

        def value_rows(g, carry, u=u, base=base, a_h=a_h, dr_h=dr_h, b_h=b_h, k_h=k_h, d_h=d_h,
                       br_h=br_h, kr_h=kr_h):
            off = pl.multiple_of(base + g * SUBLANES, SUBLANES)
            v8 = vt_s[pl.ds(off, SUBLANES), :]
            ys = []
            for j in range(SUBLANES):
                vi = g * SUBLANES + j
                s_v = s0_ref[u, vi]
                sa = jnp.sum(s_v * a_h, axis=0, keepdims=True)
                y0 = jnp.sum(s_v * dr_h, axis=0, keepdims=True)
                v_v = v8[j:j + 1, :]
                sout_ref[u, vi] = s_v * d_h + sa * b_h + v_v * k_h
                ys.append(y0 + sa * br_h + v_v * kr_h)
            yt_s[pl.ds(off, SUBLANES), :] = jnp.concatenate(ys, axis=0)
            return carry

        lax.fori_loop(0, HEAD // SUBLANES, value_rows, 0)

    @pl.when(h == nh - 1)
    def _():
        o_ref[...] = _wkv_post(yt_s[...].T, keep_s[0], keep_s[1], keep_s[2], zrg_ref[...], pv_ref,
                               e_ref[...])


def _wkv_step(z, zl, sh, s0t, pvec, mul, wd, wa, e):
    nseq = z.shape[0]
    full = lambda shp: pl.BlockSpec(shp, lambda i: (0,) * len(shp))
    col = lambda j: pl.BlockSpec((nseq, RWKV_W), lambda i, j=j: (0, j))
    st_block = (STEP_HEADS, HEAD, HEAD, nseq)
    wide = pltpu.VMEM((RWKV_W, nseq), F32)
    return pl.pallas_call(
        _wkv_step_kernel,
        out_shape=(jax.ShapeDtypeStruct((nseq, RWKV_W), BF16),
                   jax.ShapeDtypeStruct(s0t.shape, F32),
                   jax.ShapeDtypeStruct(sh.shape, F32)),
        grid=(HEADS // STEP_HEADS,),
        in_specs=[col(0), col(1), col(2), col(3),
                  full(zl.shape), full(sh.shape),
                  pl.BlockSpec(st_block, lambda i: (i, 0, 0, 0)),
                  full(pvec.shape), full(mul.shape), full(wd.shape), full(wa.shape), full(e.shape)],
        out_specs=(full((nseq, RWKV_W)),
                   pl.BlockSpec(st_block, lambda i: (i, 0, 0, 0)),
                   full(sh.shape)),
        scratch_shapes=[wide] * 6 + [pltpu.VMEM((HEADS, nseq), F32)] * 2
                       + [wide, pltpu.VMEM((3, nseq, RWKV_W), F32)],
        compiler_params=pltpu.CompilerParams(
            dimension_semantics=("arbitrary",), vmem_limit_bytes=VMEM_LIMIT),
        name="wkv_step",
    )(z, z, z, z, zl, sh, s0t, pvec, mul, wd, wa, e)


_CW0, _CW1, _CW2, _CW3, _CB, _GXB, _GAB, _LAM = range(8)


def _lru_gates(xc, lp_ref, wg_ref):
    xb = _bf(xc)
    ngroups = wg_ref.shape[0]
    gs = [_dg(xb[:, LANES * g:LANES * (g + 1)], wg_ref[g], NN) for g in range(ngroups)]
    gx_pre = jnp.concatenate([gs[g][:, 0:LANES] for g in range(ngroups)], axis=1)
    ga_pre = jnp.concatenate([gs[g][:, LANES:2 * LANES] for g in range(ngroups)], axis=1)
    gx = _sigmoid(gx_pre + _prow(lp_ref, _GXB))
    ga = _sigmoid(ga_pre + _prow(lp_ref, _GAB))
    log_a = -LRU_C * ga * _softplus(-_prow(lp_ref, _LAM))
    a = jnp.exp(log_a)
    mult = jnp.sqrt((1.0 - a) * (1.0 + a))
    return a, mult * gx * xc


def _lru_scan_rows(a, b, zg, hc):
    row8 = lax.broadcasted_iota(jnp.int32, (SUBLANES, 1), 0)
    hs = []
    for i in range(a.shape[0] // SUBLANES):
        a8 = a[SUBLANES * i:SUBLANES * (i + 1), :]
        b8 = b[SUBLANES * i:SUBLANES * (i + 1), :]
        for s in (1, 2, 4):
            keep = row8 >= s
            b8 = jnp.where(keep, a8 * pltpu.roll(b8, s, 0) + b8, b8)
            a8 = jnp.where(keep, a8 * pltpu.roll(a8, s, 0), a8)
        hb = b8 + a8 * hc
        hs.append(hb)
        hc = jnp.broadcast_to(hb[SUBLANES - 1:SUBLANES, :], hb.shape)
    return _bf(jnp.concatenate(hs, axis=0) * (zg * _sigmoid(zg))), hc


def _lru_step_kernel(zx_ref, zg_ref, conv_ref, h0_ref, lp_ref, wg_ref, o_ref, hnew_ref, cnew_ref):
    zx = zx_ref[...]
    for j in range(CONV_W - 2):
        cnew_ref[j] = conv_ref[j + 1]
    cnew_ref[CONV_W - 2] = zx
    xc = _prow(lp_ref, _CW3) * zx + _prow(lp_ref, _CB)
    for j in range(CONV_W - 1):
        xc = xc + _prow(lp_ref, j) * conv_ref[j]
    a, b = _lru_gates(xc, lp_ref, wg_ref)
    h = a * h0_ref[...] + b
    hnew_ref[...] = h
    zg = zg_ref[...]
    o_ref[...] = _bf(h * (zg * _sigmoid(zg)))


def _lru_step(z_main, conv, h0, lp, wg):
    nb = z_main.shape[0]
    full = lambda shp: pl.BlockSpec(shp, lambda i: (0,) * len(shp))
    col = lambda j: pl.BlockSpec((nb, LRU_W), lambda i, j=j: (0, j))
    return pl.pallas_call(
        _lru_step_kernel,
        out_shape=(jax.ShapeDtypeStruct((nb, LRU_W), BF16), jax.ShapeDtypeStruct((nb, LRU_W), F32),
                   jax.ShapeDtypeStruct(conv.shape, F32)),
        grid=(1,),
        in_specs=[col(4), col(5), full(conv.shape), full(h0.shape), full(lp.shape), full(wg.shape)],
        out_specs=(full((nb, LRU_W)), full((nb, LRU_W)), full(conv.shape)),
        compiler_params=pltpu.CompilerParams(
            dimension_semantics=("arbitrary",), vmem_limit_bytes=VMEM_LIMIT),
        name="lru_step",
    )(z_main, z_main, conv, h0, lp, wg)


def _project(x, o_r, o_g, m_r, m_g, wr_ref, wg_ref, wo_ref, fg_ref, final):
    y_r = _dg(o_r, wr_ref[...], NN)
    y_g = _dg(o_g, wg_ref[...], NN)
    merged = _sigmoid(m_r) * y_r + _sigmoid(m_g) * y_g
    out = x + _dg(_bf(merged), wo_ref[...], NN)
    return _rms(out, fg_ref[...]) if final else out


def _outproj_lru_kernel(x_ref, or_ref, mr_ref, mg_ref, zx_ref, zg_ref, xs_ref, ors_ref, ogs_ref,
                        mrs_ref, mgs_ref, lp_ref, wgate_ref, wr_ref, wg_ref, wo_ref, fg_ref,
                        out_ref, hlast_ref, cnew_ref, outs_ref, og_s, mg_s, xb_s, hc_s, *, final,
                        tiles_per_seq, n_tiles):
    i = pl.program_id(0)
    n = n_tiles
    tm, d = x_ref.shape

    @pl.when(i == 0)
    def _():
        xb_s[...] = jnp.zeros_like(xb_s)
        hc_s[...] = jnp.zeros_like(hc_s)
        outs_ref[...] = _project(xs_ref[...], ors_ref[...], ogs_ref[...], mrs_ref[...], mgs_ref[...],
                                 wr_ref, wg_ref, wo_ref, fg_ref, final)

    def stages(lru, branch, outp):
        if branch:
            og_prev = og_s[...]
        if outp:
            mg_prev = mg_s[...]
        if lru:
            t = lax.rem(i, tiles_per_seq)
            first = t == 0
            xb_s[0:SUBLANES, :] = jnp.where(first, 0.0, xb_s[0:SUBLANES, :])
            xb_s[SUBLANES:SUBLANES + tm, :] = zx_ref[...]

        outs, a_parts, b_parts = [], [], []
        for c in range(FUSE_PIECES):
            if outp:
                cs = slice(c * d // FUSE_PIECES, (c + 1) * d // FUSE_PIECES)
                outs.append(x_ref[:, cs] + _dg(mg_prev, wo_ref[:, cs], NN))
            if lru:
                r0, r1 = SUBLANES + c * tm // FUSE_PIECES, SUBLANES + (c + 1) * tm // FUSE_PIECES
                xc = _prow(lp_ref, _CW3) * xb_s[r0:r1, :] + _prow(lp_ref, _CB)
                for j in range(1, CONV_W):
                    xc = xc + _prow(lp_ref, CONV_W - 1 - j) * xb_s[r0 - j:r1 - j, :]
                a_c, b_c = _lru_gates(xc, lp_ref, wgate_ref)
                a_parts.append(a_c)
                b_parts.append(b_c)
        if outp:
            out = jnp.concatenate(outs, axis=1)
            out_ref[...] = _rms(out, fg_ref[...]) if final else out
        if lru:
            xb_s[0:SUBLANES, :] = xb_s[tm:tm + SUBLANES, :]
            o_g, hc = _lru_scan_rows(jnp.concatenate(a_parts, axis=0), jnp.concatenate(b_parts, axis=0),
                                     zg_ref[...], jnp.where(first, 0.0, hc_s[...]))
            hc_s[...] = hc
            og_s[...] = o_g
        if branch:
            y_r = _dg(or_ref[...], wr_ref[...], NN)
            y_g = _dg(og_prev, wg_ref[...], NN)
            mg_s[...] = _bf(_sigmoid(mr_ref[...]) * y_r + _sigmoid(mg_ref[...]) * y_g)
        if lru:
            @pl.when(t == tiles_per_seq - 1)
            def _():
                seq = i // tiles_per_seq
                hlast_ref[pl.ds(seq, 1), :] = hc[0:1, :]
                for j in range(CONV_W - 1):
                    row = tm - (CONV_W - 1) + j
                    cnew_ref[j, pl.ds(seq, 1), :] = zx_ref[row:row + 1, :]

    active = lambda step, s: 0 <= step - s < n
    holds = lambda s, flag: ((i >= s) & (i < n + s)) == flag
    for combo in sorted({tuple(active(step, s) for s in range(3)) for step in range(n + 2)}):
        lru, branch, outp = combo
        pl.when(holds(0, lru) & holds(1, branch) & holds(2, outp))(
            functools.partial(stages, lru, branch, outp))


def _outproj_lru(x, o_r, z, xs, o_rs, o_gs, zs, lp, wgate, w_r, w_g, w_o, fg, tm, seq, final):
    m, d = x.shape
    ms = xs.shape[0]
    n = m // tm
    tiles_per_seq = seq // tm
    const = lambda shp: pl.BlockSpec(shp, lambda i: (0,) * len(shp), pipeline_mode=pl.Buffered(1))
    back = lambda i, k: jnp.clip(i - k, 0, n - 1)
    return pl.pallas_call(
        functools.partial(_outproj_lru_kernel, final=final, tiles_per_seq=tiles_per_seq, n_tiles=n),
        out_shape=(jax.ShapeDtypeStruct((m, d), F32),
                   jax.ShapeDtypeStruct((m // seq, LRU_W), F32),
                   jax.ShapeDtypeStruct((CONV_W - 1, m // seq, LRU_W), F32),
                   jax.ShapeDtypeStruct((ms, d), F32)),
        grid=(n + 2,),
        in_specs=[
            pl.BlockSpec((tm, d), lambda i: (back(i, 2), 0)),
            pl.BlockSpec((tm, RWKV_W), lambda i: (back(i, 1), 0)),
            pl.BlockSpec((tm, d), lambda i: (back(i, 1), 3)),
            pl.BlockSpec((tm, d), lambda i: (back(i, 1), 4)),
            pl.BlockSpec((tm, LRU_W), lambda i: (back(i, 0), 4)),
            pl.BlockSpec((tm, LRU_W), lambda i: (back(i, 0), 5)),
            const(xs.shape), const(o_rs.shape), const(o_gs.shape),
            pl.BlockSpec((ms, d), lambda i: (0, 3), pipeline_mode=pl.Buffered(1)),
            pl.BlockSpec((ms, d), lambda i: (0, 4), pipeline_mode=pl.Buffered(1)),
            const(lp.shape), const(wgate.shape), const(w_r.shape), const(w_g.shape), const(w_o.shape),
            const(fg.shape),
        ],
        out_specs=(pl.BlockSpec((tm, d), lambda i: (back(i, 2), 0)),
                   pl.BlockSpec((m // seq, LRU_W), lambda i: (0, 0)),
                   pl.BlockSpec((CONV_W - 1, m // seq, LRU_W), lambda i: (0, 0, 0)),
                   pl.BlockSpec((ms, d), lambda i: (0, 0))),
        scratch_shapes=[pltpu.VMEM((tm, LRU_W), BF16),
                        pltpu.VMEM((tm, d), BF16),
                        pltpu.VMEM((SUBLANES + tm, LRU_W), F32),
                        pltpu.VMEM((SUBLANES, LRU_W), F32)],
        compiler_params=pltpu.CompilerParams(
            dimension_semantics=("arbitrary",), vmem_limit_bytes=VMEM_LIMIT),
        name="outproj_lru",
    )(x, o_r, z, z, z, z, xs, o_rs, o_gs, zs, zs, lp, wgate, w_r, w_g, w_o, fg)


def _row_tile(m, want):
    t = min(m, want)
    assert m % t == 0, (m, t)
    return t


def _pack_params_kernel(mu_ref, w0_ref, a0_ref, kk_ref, ka_ref, lng_ref, lnb_ref, rk_ref, wdu_ref,
                        wau_ref, cw_ref, cb_ref, gxb_ref, gab_ref, lam_ref, gxw_ref, gaw_ref,
                        pvec_ref, mul_ref, wd_ref, wa_ref, e_ref, lp_ref, wg_ref):
    pvec_ref[...] = jnp.zeros_like(pvec_ref)
    for i in range(3):
        pvec_ref[_MU_R + i:_MU_R + i + 1, :] = mu_ref[:, RWKV_W * i:RWKV_W * (i + 1)]
    for row, ref in ((_W0, w0_ref), (_A0, a0_ref), (_KK, kk_ref), (_KA, ka_ref),
                     (_LNG, lng_ref), (_LNB, lnb_ref)):
        pvec_ref[row:row + 1, :] = ref[...]
    for h in range(HEADS):
        pvec_ref[_RK:_RK + 1, HEAD * h:HEAD * (h + 1)] = rk_ref[h:h + 1, :]
    mul_ref[...] = jnp.broadcast_to(mu_ref[:, 3 * RWKV_W:3 * RWKV_W + 2 * LORA], mul_ref.shape)

    zeros = jnp.zeros((LORA, RWKV_W), BF16)
    wd_ref[0:LORA, :] = _bf(wdu_ref[...])
    wd_ref[LORA:2 * LORA, :] = zeros
    wa_ref[0:LORA, :] = zeros
    wa_ref[LORA:2 * LORA, :] = _bf(wau_ref[...])

    ri = lax.broadcasted_iota(jnp.int32, (LANES, LANES), 0)
    ci = lax.broadcasted_iota(jnp.int32, (LANES, LANES), 1)
    e_ref[...] = jnp.where((ri < HEAD) == (ci < HEAD), 1.0, 0.0).astype(BF16)

    lp_ref[_CW0:_CW0 + CONV_W, :] = cw_ref[...]
    for row, ref in ((_CB, cb_ref), (_GXB, gxb_ref), (_GAB, gab_ref), (_LAM, lam_ref)):
        lp_ref[row:row + 1, :] = ref[...]

    blk = LRU_W // LRU_BLOCKS
    z = jnp.zeros((blk, blk), F32)
    for g in range(LRU_BLOCKS // 2):
        top = jnp.concatenate([gxw_ref[2 * g], z, gaw_ref[2 * g], z], axis=1)
        bot = jnp.concatenate([z, gxw_ref[2 * g + 1], z, gaw_ref[2 * g + 1]], axis=1)
        wg_ref[g] = _bf(jnp.concatenate([top, bot], axis=0))


def _pack_params(l, rwkv_mu, w_decay0, w_decay_up, w_iclr0, w_iclr_up, k_k, k_a, r_k, ln_x_g,
                 ln_x_b, conv_w, conv_b, lru_gx_w, lru_gx_b, lru_ga_w, lru_ga_b, lru_lambda):
    blk = LRU_W // LRU_BLOCKS
    assert 2 * blk == LANES and 2 * LORA == LANES
    row = lambda a: pl.BlockSpec((1, a.shape[-1]), lambda i: (l, 0))
    mat = lambda a: pl.BlockSpec((None,) + a.shape[1:], lambda i: (l,) + (0,) * (a.ndim - 1))
    full = lambda shp: pl.BlockSpec(shp, lambda i: (0,) * len(shp))
    rows = (rwkv_mu, w_decay0, w_iclr0, k_k, k_a, ln_x_g, ln_x_b)
    out_shapes = ((16, RWKV_W, F32), (SUBLANES, LANES, F32), (LANES, RWKV_W, BF16), (LANES, RWKV_W, BF16),
                  (LANES, LANES, BF16), (SUBLANES, LRU_W, F32))
    outs = tuple(jax.ShapeDtypeStruct(s[:2], s[2]) for s in out_shapes)
    outs += (jax.ShapeDtypeStruct((LRU_BLOCKS // 2, LANES, 2 * LANES), BF16),)
    return pl.pallas_call(
        _pack_params_kernel,
        out_shape=outs,
        grid=(1,),
        in_specs=[row(a) for a in rows] + [mat(r_k), mat(w_decay_up), mat(w_iclr_up), mat(conv_w),
                                          row(conv_b), row(lru_gx_b), row(lru_ga_b), row(lru_lambda),
                                          mat(lru_gx_w), mat(lru_ga_w)],
        out_specs=tuple(full(o.shape) for o in outs),
        compiler_params=pltpu.CompilerParams(
            dimension_semantics=("arbitrary",), vmem_limit_bytes=VMEM_LIMIT),
        name="pack_params",
    )(*rows, r_k, w_decay_up, w_iclr_up, conv_w, conv_b, lru_gx_b, lru_ga_b, lru_lambda, lru_gx_w,
      lru_ga_w)


def kernel(x_prompt, x_sample, state_shift, state_wkv, state_conv, state_lru, norm_g, w_in, rwkv_mu,
           w_decay0, w_decay_up, w_iclr0, w_iclr_up, k_k, k_a, r_k, ln_x_g, ln_x_b, w_out_rwkv,
           conv_w, conv_b, lru_gx_w, lru_gx_b, lru_ga_w, lru_ga_b, lru_lambda, w_out_lru, w_out,
           final_norm_g):
    bp, seq, d = x_prompt.shape
    bs = x_sample.shape[0]
    assert x_sample.shape[1] == 1 and seq % WKV_CHUNK == 0
    depth = w_in.shape[0]
    xp = x_prompt.reshape(bp * seq, d)
    xs = x_sample.reshape(bs, d)
    fg = final_norm_g.reshape(1, d)
    outs = [[] for _ in range(8)]
    for l in range(depth):
        pvec, mul, wd, wa, e, lp, wg = _pack_params(
            l, rwkv_mu, w_decay0, w_decay_up, w_iclr0, w_iclr_up, k_k, k_a, r_k, ln_x_g, ln_x_b,
            conv_w, conv_b, lru_gx_w, lru_gx_b, lru_ga_w, lru_ga_b, lru_lambda)
        g = norm_g[l].reshape(1, d)
        rec = (pvec, mul, wd, wa, e)
        w, wl, zs, zls, w_r, w_g, w_o = _inproj_head(xs, g, w_in, w_out_rwkv, w_out_lru, w_out, l, INPROJ_TN)
        zp, zlp = _inproj(xp, g, w, wl, _row_tile(bp * seq, 1024), INPROJ_WIDE_TN)

        s0t = jnp.transpose(state_wkv[l], (1, 2, 3, 0))
        o_rs, s_new, sh_new = _wkv_step(zs, zls, state_shift[l], s0t, *rec)
        conv = jnp.transpose(state_conv[l], (1, 0, 2))
        o_gs, h_new, conv_new = _lru_step(zs, conv, state_lru[l], lp, wg)
        outs[4].append(sh_new)
        outs[5].append(jnp.transpose(s_new, (3, 0, 1, 2)))
        outs[6].append(jnp.transpose(conv_new, (1, 0, 2)))
        outs[7].append(h_new)

        zp3 = zp.reshape(bp, seq, -1)
        zlp3 = zlp.reshape(bp, seq, LANES)
        nb = max(n for n in (2, 1) if bp % n == 0)
        o_r, s_new, sh_last = _wkv_chunk(zp3, zlp3, *rec, bp, seq, nb)
        o_r = o_r.reshape(bp * seq, RWKV_W)
        last = l == depth - 1
        xp, h_last, conv_last, xs = _outproj_lru(xp, o_r, zp, xs, o_rs, o_gs, zs, lp, wg, w_r, w_g, w_o,
                                                 fg, _row_tile(seq, 256), seq, last)
        outs[0].append(sh_last.reshape(bp, -1))
        outs[1].append(s_new)
        outs[2].append(jnp.transpose(conv_last, (1, 0, 2)))
        outs[3].append(h_last)

    return (xp.reshape(bp, seq, d), xs.reshape(bs, 1, d)) + tuple(jnp.stack(o) for o in outs)
```
